```python
import jax, jax.numpy as jnp
from jax import lax
import numpy as np

D_MODEL = 1024
BATCH = 4
SEQ = 4096
DEPTH = 1
DEC_BATCH = 128
DEC_SEQ = 1
PAST_LEN = 8192
PAGE_SIZE = 128

MIX_WIDTH = D_MODEL
MLSTM_HEADS = 4
MLSTM_HEAD_DIM = MIX_WIDTH // 2 // MLSTM_HEADS
MLSTM_WIDTH = MLSTM_HEADS * MLSTM_HEAD_DIM
MLSTM_CHUNK = 128
SWA_HEADS = 8
SWA_HEAD_DIM = (MIX_WIDTH - MLSTM_WIDTH) // SWA_HEADS
SWA_WIDTH = SWA_HEADS * SWA_HEAD_DIM
SWA_KV_HEADS = 2
SWA_GROUP = SWA_HEADS // SWA_KV_HEADS
SWA_KV_WIDTH = SWA_KV_HEADS * SWA_HEAD_DIM
WINDOW = 128
SWA_BLOCK = WINDOW
D_FF = 2816
FFN_RES_WEIGHT = 0.5
RMS_EPS = 1e-6
GATE_PAD = -1e30
PROJ_WIDTH = 4 * MLSTM_WIDTH + 2 * MLSTM_HEADS + SWA_WIDTH + 2 * SWA_KV_WIDTH

kernel_name = 'hymba_mlstm_swa_macaron_step'


def rms_norm(x, gain):
    xf = x.astype(jnp.float32)
    y = xf * lax.rsqrt(jnp.mean(xf * xf, axis=-1, keepdims=True) + RMS_EPS)
    return (y * gain.astype(jnp.float32)).astype(x.dtype)


def swiglu(x, w_gate, w_up, w_down):
    return (jax.nn.silu(x @ w_gate) * (x @ w_up)) @ w_down


def proj_split_points():
    sizes = [MLSTM_WIDTH] * 4 + [MLSTM_HEADS] * 2 + [SWA_WIDTH, SWA_KV_WIDTH, SWA_KV_WIDTH]
    return [int(s) for s in np.cumsum(sizes)[:-1]]


def mlstm_chunkwise(q, k, v, i_pre, log_f, c0, n0, m0):
    b_, t_, h_, dh = q.shape
    L = min(MLSTM_CHUNK, t_)
    n_chunks = -(-t_ // L)
    pad = n_chunks * L - t_
    if pad:
        pw = ((0, 0), (0, pad), (0, 0), (0, 0))
        q, k, v = jnp.pad(q, pw), jnp.pad(k, pw), jnp.pad(v, pw)
        i_pre = jnp.pad(i_pre, pw[:3], constant_values=GATE_PAD)
        log_f = jnp.pad(log_f, pw[:3])

    def to_chunks(a):
        return jnp.moveaxis(a.reshape((b_, n_chunks, L) + a.shape[2:]), 1, 0)

    causal = jnp.tril(jnp.ones((L, L), dtype=bool))[None, :, :, None]

    def step(carry, xs):
        c, n, m = carry
        qj, kj, vj, ij, fj = xs
        bcum = jnp.cumsum(fj, axis=1)
        log_inter = bcum + m[:, None, :]
        log_intra = bcum[:, :, None, :] - bcum[:, None, :, :] + ij[:, None, :, :]
        log_intra = jnp.where(causal, log_intra, -jnp.inf)
        m_t = jnp.maximum(log_inter, jnp.max(log_intra, axis=2))
        w_inter = jnp.exp(log_inter - m_t)
        scores = jnp.einsum('bthd,bshd->btsh', qj, kj) * jnp.exp(log_intra - m_t[:, :, None, :])
        num = w_inter[..., None] * jnp.einsum('bhvk,bthk->bthv', c, qj) + jnp.einsum('btsh,bshv->bthv', scores, vj)
        den = w_inter * jnp.einsum('bhk,bthk->bth', n, qj) + jnp.sum(scores, axis=2)
        h = num / jnp.maximum(jnp.abs(den), jnp.exp(-m_t))[..., None]
        b_last = bcum[:, -1, :]
        log_end = b_last[:, None, :] - bcum + ij
        m_new = jnp.maximum(b_last + m, jnp.max(log_end, axis=1))
        decay = jnp.exp(b_last + m - m_new)
        w_end = jnp.exp(log_end - m_new[:, None, :])
        c_new = decay[..., None, None] * c + jnp.einsum('bsh,bshv,bshk->bhvk', w_end, vj, kj)
        n_new = decay[..., None] * n + jnp.einsum('bsh,bshk->bhk', w_end, kj)
        return (c_new, n_new, m_new), h

    xs = (to_chunks(q), to_chunks(k), to_chunks(v), to_chunks(i_pre), to_chunks(log_f))
    (c1, n1, m1), hs = lax.scan(step, (c0, n0, m0), xs)
    hs = jnp.moveaxis(hs, 0, 1).reshape(b_, n_chunks * L, h_, dh)[:, :t_]
    return hs, (c1, n1, m1)


def mlstm_mixer(q_m, k_m, v_m, o_m, i_m, f_m, b_i, b_f, out_gain, c0, n0, m0):
    b_, t_, _ = q_m.shape
    shp = (b_, t_, MLSTM_HEADS, MLSTM_HEAD_DIM)
    q = q_m.reshape(shp).astype(jnp.float32)
    k = k_m.reshape(shp).astype(jnp.float32) * (MLSTM_HEAD_DIM ** -0.5)
    v = v_m.reshape(shp).astype(jnp.float32)
    i_pre = i_m.astype(jnp.float32) + b_i.astype(jnp.float32)
    log_f = jax.nn.log_sigmoid(f_m.astype(jnp.float32) + b_f.astype(jnp.float32))
    h, state = mlstm_chunkwise(q, k, v, i_pre, log_f, c0.astype(jnp.float32),
                               n0.astype(jnp.float32), m0.astype(jnp.float32))
    h = h * lax.rsqrt(jnp.mean(h * h, axis=-1, keepdims=True) + RMS_EPS)
    h = h * out_gain.astype(jnp.float32).reshape(MLSTM_HEADS, MLSTM_HEAD_DIM)
    h = h.reshape(b_, t_, MLSTM_WIDTH) * jax.nn.sigmoid(o_m.astype(jnp.float32))
    return h.astype(q_m.dtype), state


def sink_softmax(s, sink):
    mx = jnp.maximum(jnp.max(s, axis=-1, keepdims=True), sink)
    p = jnp.exp(s - mx)
    return p / (jnp.sum(p, axis=-1, keepdims=True) + jnp.exp(sink - mx))


def swa_banded(q, k, v, sinks):
    b_, t_, _, dh = q.shape
    nb = t_ // SWA_BLOCK
    qb = q.reshape(b_, nb, SWA_BLOCK, SWA_KV_HEADS, SWA_GROUP, dh)
    kb = k.reshape(b_, nb, SWA_BLOCK, SWA_KV_HEADS, dh)
    vb = v.reshape(b_, nb, SWA_BLOCK, SWA_KV_HEADS, dh)
    shift = ((0, 0), (1, 0), (0, 0), (0, 0), (0, 0))
    k_band = jnp.concatenate([jnp.pad(kb, shift)[:, :-1], kb], axis=2)
    v_band = jnp.concatenate([jnp.pad(vb, shift)[:, :-1], vb], axis=2)
    s = jnp.einsum('bnqkgd,bnskd->bnkgqs', qb, k_band).astype(jnp.float32) * (dh ** -0.5)
    q_pos = jnp.arange(SWA_BLOCK)[:, None] + SWA_BLOCK
    k_pos = jnp.arange(2 * SWA_BLOCK)[None, :]
    blk_start = (jnp.arange(nb) - 1) * SWA_BLOCK
    mask = ((k_pos <= q_pos) & (k_pos >= q_pos - WINDOW))[None] & ((blk_start[:, None, None] + k_pos[None]) >= 0)
    s = jnp.where(mask[None, :, None, None], s, -jnp.inf)
    sink = sinks.astype(jnp.float32).reshape(SWA_KV_HEADS, SWA_GROUP)[None, None, :, :, None, None]
    p = sink_softmax(s, sink)
    o = jnp.einsum('bnkgqs,bnskd->bnqkgd', p.astype(v.dtype), v_band)
    return o.reshape(b_, t_, SWA_WIDTH)


def swa_with_buffer(q, k, v, buf_k, buf_v, sinks):
    b_, t_, _, dh = q.shape
    w = buf_k.shape[1]
    k_all = jnp.concatenate([buf_k.astype(k.dtype), k], axis=1)
    v_all = jnp.concatenate([buf_v.astype(v.dtype), v], axis=1)
    qg = q.reshape(b_, t_, SWA_KV_HEADS, SWA_GROUP, dh)
    s = jnp.einsum('bqkgd,bskd->bkgqs', qg, k_all).astype(jnp.float32) * (dh ** -0.5)
    q_rel = jnp.arange(t_)[:, None]
    k_rel = jnp.arange(w + t_)[None, :] - w
    mask = (k_rel <= q_rel) & (k_rel >= q_rel - WINDOW)
    s = jnp.where(mask, s, -jnp.inf)
    sink = sinks.astype(jnp.float32).reshape(SWA_KV_HEADS, SWA_GROUP)[None, :, :, None, None]
    p = sink_softmax(s, sink)
    o = jnp.einsum('bkgqs,bskd->bqkgd', p.astype(v_all.dtype), v_all)
    return o.reshape(b_, t_, SWA_WIDTH), k_all[:, -w:], v_all[:, -w:]


def trunk_layer(x, p, c0, n0, m0, buf_k, buf_v):
    x = x + FFN_RES_WEIGHT * swiglu(rms_norm(x, p['ffn1_norm']), p['ffn1_w_gate'], p['ffn1_w_up'], p['ffn1_w_down'])
    h = rms_norm(x, p['mix_norm'])
    proj = h @ p['w_in']
    q_m, k_m, v_m, o_m, i_m, f_m, q_a, k_a, v_a = jnp.split(proj, proj_split_points(), axis=-1)
    b_, t_, _ = x.shape
    y_m, (c1, n1, m1) = mlstm_mixer(q_m, k_m, v_m, o_m, i_m, f_m, p['mlstm_b_i'], p['mlstm_b_f'],
                                    p['mlstm_out_norm'], c0, n0, m0)
    q_a = rms_norm(q_a.reshape(b_, t_, SWA_HEADS, SWA_HEAD_DIM), p['swa_q_norm'])
    k_a = rms_norm(k_a.reshape(b_, t_, SWA_KV_HEADS, SWA_HEAD_DIM), p['swa_k_norm'])
    v_a = v_a.reshape(b_, t_, SWA_KV_HEADS, SWA_HEAD_DIM)
    if buf_k is None:
        y_a = swa_banded(q_a, k_a, v_a, p['swa_sinks'])
        w = min(WINDOW, t_)
        new_k, new_v = k_a[:, t_ - w:], v_a[:, t_ - w:]
    else:
        y_a, new_k, new_v = swa_with_buffer(q_a, k_a, v_a, buf_k, buf_v, p['swa_sinks'])
    x = x + jnp.concatenate([y_m, y_a], axis=-1) @ p['w_out']
    x = x + FFN_RES_WEIGHT * swiglu(rms_norm(x, p['ffn2_norm']), p['ffn2_w_gate'], p['ffn2_w_up'], p['ffn2_w_down'])
    return x, (new_k, new_v, c1, n1, m1)


def setup_inputs(seed: int = 0) -> dict:
    key = jax.random.key(seed)
    ks = jax.random.split(key, 32)
    f32 = jnp.float32

    def nrm(k, shape, scale):
        return jax.random.normal(k, shape, f32) * scale

    def gain(k, shape):
        return 1.0 + 0.05 * jax.random.normal(k, shape, f32)

    L, d, hd = DEPTH, D_MODEL, MLSTM_HEAD_DIM
    w_buf = min(WINDOW, PAST_LEN)
    return {
        'x_prompt': nrm(ks[0], (BATCH, SEQ, d), 1.0),
        'x_sample': nrm(ks[1], (DEC_BATCH, DEC_SEQ, d), 1.0),
        'cache_swa_k': nrm(ks[2], (L, DEC_BATCH, w_buf, SWA_KV_HEADS, SWA_HEAD_DIM), 1.0),
        'cache_swa_v': nrm(ks[3], (L, DEC_BATCH, w_buf, SWA_KV_HEADS, SWA_HEAD_DIM), 1.0),
        'state_mlstm_C': nrm(ks[4], (L, DEC_BATCH, MLSTM_HEADS, hd, hd), 0.1),
        'state_mlstm_n': nrm(ks[5], (L, DEC_BATCH, MLSTM_HEADS, hd), 0.1),
        'state_mlstm_m': nrm(ks[6], (L, DEC_BATCH, MLSTM_HEADS), 1.0),
        'ffn1_norm': gain(ks[7], (L, d)),
        'ffn1_w_gate': nrm(ks[8], (L, d, D_FF), d ** -0.5),
        'ffn1_w_up': nrm(ks[9], (L, d, D_FF), d ** -0.5),
        'ffn1_w_down': nrm(ks[10], (L, D_FF, d), D_FF ** -0.5),
        'mix_norm': gain(ks[11], (L, d)),
        'w_in': nrm(ks[12], (L, d, PROJ_WIDTH), d ** -0.5),
        'mlstm_b_i': nrm(ks[13], (L, MLSTM_HEADS), 0.1),
        'mlstm_b_f': jnp.broadcast_to(jnp.linspace(3.0, 6.0, MLSTM_HEADS, dtype=f32), (L, MLSTM_HEADS)) + nrm(ks[14], (L, MLSTM_HEADS), 0.01),
        'mlstm_out_norm': gain(ks[15], (L, MLSTM_WIDTH)),
        'swa_q_norm': gain(ks[16], (L, SWA_HEAD_DIM)),
        'swa_k_norm': gain(ks[17], (L, SWA_HEAD_DIM)),
        'swa_sinks': nrm(ks[18], (L, SWA_HEADS), 0.5),
        'w_out': nrm(ks[19], (L, MIX_WIDTH, d), MIX_WIDTH ** -0.5),
        'ffn2_norm': gain(ks[20], (L, d)),
        'ffn2_w_gate': nrm(ks[21], (L, d, D_FF), d ** -0.5),
        'ffn2_w_up': nrm(ks[22], (L, d, D_FF), d ** -0.5),
        'ffn2_w_down': nrm(ks[23], (L, D_FF, d), D_FF ** -0.5),
    }


def reference(x_prompt, x_sample, cache_swa_k, cache_swa_v, state_mlstm_C, state_mlstm_n, state_mlstm_m,
              ffn1_norm, ffn1_w_gate, ffn1_w_up, ffn1_w_down, mix_norm, w_in, mlstm_b_i, mlstm_b_f,
              mlstm_out_norm, swa_q_norm, swa_k_norm, swa_sinks, w_out, ffn2_norm, ffn2_w_gate,
              ffn2_w_up, ffn2_w_down):
    yp, ys = x_prompt, x_sample
    bp = x_prompt.shape[0]
    pk, pv, pc, pn, pm = [], [], [], [], []
    sk, sv, sc, sn, sm = [], [], [], [], []
    for l in range(DEPTH):
        p = {
            'ffn1_norm': ffn1_norm[l], 'ffn1_w_gate': ffn1_w_gate[l], 'ffn1_w_up': ffn1_w_up[l],
            'ffn1_w_down': ffn1_w_down[l], 'mix_norm': mix_norm[l], 'w_in': w_in[l],
            'mlstm_b_i': mlstm_b_i[l], 'mlstm_b_f': mlstm_b_f[l], 'mlstm_out_norm': mlstm_out_norm[l],
            'swa_q_norm': swa_q_norm[l], 'swa_k_norm': swa_k_norm[l], 'swa_sinks': swa_sinks[l],
            'w_out': w_out[l], 'ffn2_norm': ffn2_norm[l], 'ffn2_w_gate': ffn2_w_gate[l],
            'ffn2_w_up': ffn2_w_up[l], 'ffn2_w_down': ffn2_w_down[l],
        }
        c0 = jnp.zeros((bp, MLSTM_HEADS, MLSTM_HEAD_DIM, MLSTM_HEAD_DIM), jnp.float32)
        n0 = jnp.zeros((bp, MLSTM_HEADS, MLSTM_HEAD_DIM), jnp.float32)
        m0 = jnp.zeros((bp, MLSTM_HEADS), jnp.float32)
        yp, (k1, v1, c1, n1, m1) = trunk_layer(yp, p, c0, n0, m0, None, None)
        ys, (k2, v2, c2, n2, m2) = trunk_layer(ys, p, state_mlstm_C[l], state_mlstm_n[l], state_mlstm_m[l],
                                               cache_swa_k[l], cache_swa_v[l])
        pk.append(k1); pv.append(v1); pc.append(c1); pn.append(n1); pm.append(m1)
        sk.append(k2); sv.append(v2); sc.append(c2); sn.append(n2); sm.append(m2)
    return (yp, ys, jnp.stack(pk), jnp.stack(pv), jnp.stack(pc), jnp.stack(pn), jnp.stack(pm),
            jnp.stack(sk), jnp.stack(sv), jnp.stack(sc), jnp.stack(sn), jnp.stack(sm))
```

```python
import functools

import jax
import jax.numpy as jnp
from jax import lax
from jax.experimental import pallas as pl
from jax.experimental.pallas import tpu as pltpu

F32 = jnp.float32
BF16 = jnp.bfloat16

D_MODEL = 1024
D_FF = 2816
FF_CHUNK = 256
N_FF_CHUNKS = D_FF // FF_CHUNK
M_HEADS = 4
M_DIM = 128
M_WIDTH = M_HEADS * M_DIM
A_HEADS = 8
A_DIM = 64
A_WIDTH = A_HEADS * A_DIM
KV_HEADS = 2
KV_WIDTH = KV_HEADS * A_DIM
A_GROUP = A_HEADS // KV_HEADS
WINDOW = 128
BLOCK = 128
GATE_PAD = 128
RMS_EPS = 1e-6
FFN_RES_WEIGHT = 0.5
NEG_INF = float("-inf")
VMEM_LIMIT_BYTES = 56 * 1024 * 1024


def _dot(a, b):
    return jnp.dot(a, b, preferred_element_type=F32)


def _dot_nt(a, b):
    return lax.dot_general(a, b, (((1,), (1,)), ((), ())), preferred_element_type=F32)


def _rms_rows(x, gain):
    ms = jnp.mean(x * x, axis=-1, keepdims=True)
    return x * lax.rsqrt(ms + RMS_EPS) * gain


def _log_sigmoid(x):
    return jnp.minimum(x, 0.0) - jnp.log1p(jnp.exp(-jnp.abs(x)))


def _group_mean_sq(x, ones_blockdiag, group):
    sq = x * x
    hi = sq.astype(BF16)
    lo = (sq - hi.astype(F32)).astype(BF16)
    return (_dot(hi, ones_blockdiag) + _dot(lo, ones_blockdiag)) * (1.0 / group)


def _ffn_into(h_ref, wg_ref, wu_ref, wd_ref, acc_ref):
    acc_ref[...] = jnp.zeros_like(acc_ref)

    def body(c, carry):
        h = h_ref[...]
        g = _dot(h, wg_ref[c])
        u = _dot(h, wu_ref[c])
        a = (g * jax.nn.sigmoid(g) * u).astype(BF16)
        acc_ref[...] += _dot(a, wd_ref[c])
        return carry

    lax.fori_loop(0, N_FF_CHUNKS, body, 0)


def _front_kernel(x_ref, n1_ref, wg_ref, wu_ref, wd_ref, n2_ref, wm_ref, wqa_ref, wkvg_ref,
                  qgain_ref, kgain_ref, e512_ref, e128_ref,
                  x1_ref, qkv_ref, om_ref, qa_ref, ka_ref, va_ref, gc_ref,
                  h_ref, acc_ref):
    x = x_ref[...]
    h_ref[...] = _rms_rows(x, n1_ref[...]).astype(BF16)
    _ffn_into(h_ref, wg_ref, wu_ref, wd_ref, acc_ref)
    x1 = x + FFN_RES_WEIGHT * acc_ref[...]
    x1_ref[...] = x1
    h_ref[...] = _rms_rows(x1, n2_ref[...]).astype(BF16)
    h = h_ref[...]

    qkv_ref[:, 0:M_WIDTH] = _dot(h, wm_ref[:, 0:M_WIDTH]).astype(BF16)
    k_m = _dot(h, wm_ref[:, M_WIDTH:2 * M_WIDTH]) * (M_DIM ** -0.5)
    qkv_ref[:, M_WIDTH:2 * M_WIDTH] = k_m.astype(BF16)
    qkv_ref[:, 2 * M_WIDTH:3 * M_WIDTH] = _dot(h, wm_ref[:, 2 * M_WIDTH:3 * M_WIDTH]).astype(BF16)
    om_ref[...] = _dot(h, wm_ref[:, 3 * M_WIDTH:4 * M_WIDTH])

    qa = _dot(h, wqa_ref[...])
    q_ms = _group_mean_sq(qa, e512_ref[...], A_DIM)
    qa_ref[...] = (qa * lax.rsqrt(q_ms + RMS_EPS) * qgain_ref[...] * (A_DIM ** -0.5)).astype(BF16)

    kvg = _dot(h, wkvg_ref[...])
    ka = kvg[:, 0:KV_WIDTH]
    k_ms = _group_mean_sq(ka, e128_ref[...], A_DIM)
    ka_ref[...] = ka * lax.rsqrt(k_ms + RMS_EPS) * kgain_ref[...]
    va_ref[...] = kvg[:, KV_WIDTH:2 * KV_WIDTH]
    gc_ref[...] = kvg[:, 2 * KV_WIDTH:2 * KV_WIDTH + GATE_PAD]


def _const_spec(shape):
    nd = len(shape)
    return pl.BlockSpec(shape, lambda i: (0,) * nd, pipeline_mode=pl.Buffered(1))


def _front(x2d, n1, wg, wu, wd, n2, wm, wqa, wkvg, qgain, kgain, e512, e128, tm):
    n = x2d.shape[0]
    row = lambda w: pl.BlockSpec((tm, w), lambda i: (i, 0))
    out_shape = (
        jax.ShapeDtypeStruct((n, D_MODEL), F32),
        jax.ShapeDtypeStruct((n, 3 * M_WIDTH), BF16),
        jax.ShapeDtypeStruct((n, M_WIDTH), F32),
        jax.ShapeDtypeStruct((n, A_WIDTH), BF16),
        jax.ShapeDtypeStruct((n, KV_WIDTH), F32),
        jax.ShapeDtypeStruct((n, KV_WIDTH), F32),
        jax.ShapeDtypeStruct((n, GATE_PAD), F32),
    )
    return pl.pallas_call(
        _front_kernel,
        out_shape=out_shape,
        grid=(n // tm,),
        in_specs=[row(D_MODEL), _const_spec(n1.shape), _const_spec(wg.shape), _const_spec(wu.shape),
                  _const_spec(wd.shape), _const_spec(n2.shape), _const_spec(wm.shape),
                  _const_spec(wqa.shape), _const_spec(wkvg.shape), _const_spec(qgain.shape),
                  _const_spec(kgain.shape), _const_spec(e512.shape), _const_spec(e128.shape)],
        out_specs=(row(D_MODEL), row(3 * M_WIDTH), row(M_WIDTH), row(A_WIDTH), row(KV_WIDTH),
                   row(KV_WIDTH), row(GATE_PAD)),
        scratch_shapes=[pltpu.VMEM((tm, D_MODEL), BF16), pltpu.VMEM((tm, D_MODEL), F32)],
        compiler_params=pltpu.CompilerParams(dimension_semantics=("arbitrary",),
                                             vmem_limit_bytes=VMEM_LIMIT_BYTES),
        name="front",
    )(x2d, n1, wg, wu, wd, n2, wm, wqa, wkvg, qgain, kgain, e512, e128)


def _back_kernel(x1_ref, y_ref, wo_ref, n_ref, wg_ref, wu_ref, wd_ref, out_ref, h_ref, acc_ref):
    x2 = x1_ref[...] + _dot(y_ref[...], wo_ref[...])
    h_ref[...] = _rms_rows(x2, n_ref[...]).astype(BF16)
    _ffn_into(h_ref, wg_ref, wu_ref, wd_ref, acc_ref)
    out_ref[...] = x2 + FFN_RES_WEIGHT * acc_ref[...]


def _back(x1, y, wo, nrm, wg, wu, wd, tm):
    n = x1.shape[0]
    row = lambda w: pl.BlockSpec((tm, w), lambda i: (i, 0))
    return pl.pallas_call(
        _back_kernel,
        out_shape=jax.ShapeDtypeStruct((n, D_MODEL), F32),
        grid=(n // tm,),
        in_specs=[row(D_MODEL), row(D_MODEL), _const_spec(wo.shape), _const_spec(nrm.shape),
                  _const_spec(wg.shape), _const_spec(wu.shape), _const_spec(wd.shape)],
        out_specs=row(D_MODEL),
        scratch_shapes=[pltpu.VMEM((tm, D_MODEL), BF16), pltpu.VMEM((tm, D_MODEL), F32)],
        compiler_params=pltpu.CompilerParams(dimension_semantics=("arbitrary",),
                                             vmem_limit_bytes=VMEM_LIMIT_BYTES),
        name="back",
    )(x1, y, wo, nrm, wg, wu, wd)


def _prompt_mixer_kernel(sinks_ref, qkv_ref, om_ref, gc_ref, qa_ref, ka_ref, va_ref, kap_ref, vap_ref,
                         bias_ref, ogain_ref, tri_ref,
                         y_ref, c_out_ref, n_out_ref, m_out_ref,
                         c_ref, n_ref, m_ref):
    j = pl.program_id(1)

    @pl.when(j == 0)
    def _():
        c_ref[...] = jnp.zeros_like(c_ref)
        n_ref[...] = jnp.zeros_like(n_ref)
        m_ref[...] = jnp.zeros_like(m_ref)

    rows = lax.broadcasted_iota(jnp.int32, (BLOCK, BLOCK), 0)
    cols = lax.broadcasted_iota(jnp.int32, (BLOCK, BLOCK), 1)

    g_rows = gc_ref[...].T[0:8, :] + bias_ref[...]
    sub8 = lax.broadcasted_iota(jnp.int32, (8, BLOCK), 0)
    r8 = jnp.where(sub8 < M_HEADS, g_rows, _log_sigmoid(g_rows))
    cum8 = jnp.dot(r8, tri_ref[...], preferred_element_type=F32,
                   precision=lax.Precision.HIGHEST)
    t8 = jnp.where(sub8 < M_HEADS, r8, cum8)
    t_cols = jnp.concatenate([t8, jnp.zeros((BLOCK - 8, BLOCK), F32)], axis=0).T

    causal = cols <= rows
    for h in range(M_HEADS):
        lo, hi = h * M_DIM, (h + 1) * M_DIM
        q = qkv_ref[:, lo:hi]
        k = qkv_ref[:, M_WIDTH + lo:M_WIDTH + hi]
        v = qkv_ref[:, 2 * M_WIDTH + lo:2 * M_WIDTH + hi]
        i_row = t8[h:h + 1, :]
        b_row = t8[M_HEADS + h:M_HEADS + h + 1, :]
        i_col = t_cols[:, h:h + 1]
        b_col = t_cols[:, M_HEADS + h:M_HEADS + h + 1]
        b_last = b_row[:, BLOCK - 1:BLOCK]
        m_prev = m_ref[h][0:1, 0:1]
        c_prev = c_ref[h]
        n_prev = n_ref[h][0:1, :]

        log_inter = b_col + m_prev
        log_intra = jnp.where(causal, b_col - b_row + i_row, NEG_INF)
        m_t = jnp.maximum(log_inter, jnp.max(log_intra, axis=-1, keepdims=True))
        w_inter = jnp.exp(log_inter - m_t)
        scores = _dot_nt(q, k) * jnp.exp(log_intra - m_t)
        num = w_inter * _dot_nt(q, c_prev.astype(BF16)) + _dot(scores.astype(BF16), v)
        den = (w_inter * jnp.sum(q.astype(F32) * n_prev, axis=-1, keepdims=True)
               + jnp.sum(scores, axis=-1, keepdims=True))
        hh = num / jnp.maximum(jnp.abs(den), jnp.exp(-m_t))
        hh = hh * lax.rsqrt(jnp.mean(hh * hh, axis=-1, keepdims=True) + RMS_EPS)
        hh = hh * ogain_ref[:, lo:hi] * jax.nn.sigmoid(om_ref[:, lo:hi])
        y_ref[:, lo:hi] = hh.astype(BF16)

        log_end_row = b_last - b_row + i_row
        m_new = jnp.maximum(b_last + m_prev, jnp.max(log_end_row, axis=-1, keepdims=True))
        decay = jnp.exp(b_last + m_prev - m_new)
        w_end_col = jnp.exp(b_last - b_col + i_col - m_new)
        vw_t = (v.astype(F32) * w_end_col).T
        c_ref[h] = decay * c_prev + _dot(vw_t.astype(BF16), k)
        n_new = decay * n_prev + jnp.sum(k.astype(F32) * w_end_col, axis=0, keepdims=True)
        n_ref[h] = jnp.broadcast_to(n_new, (8, M_DIM))
        m_ref[h] = jnp.broadcast_to(m_new, (8, 128))

    lane_lo = cols < A_DIM

    def band(prev_ref, cur_ref):
        x = jnp.concatenate([prev_ref[...], cur_ref[...]], axis=0)
        lanes_lo = lax.broadcasted_iota(jnp.int32, (2 * BLOCK, KV_WIDTH), 1) < A_DIM
        x0 = jnp.where(lanes_lo, x, 0.0)
        x1 = jnp.where(lanes_lo, 0.0, x)
        x0r = pltpu.roll(x0, A_DIM, 1)
        x1r = pltpu.roll(x1, A_DIM, 1)
        return ((x0.astype(BF16), x0r.astype(BF16)), (x1r.astype(BF16), x1.astype(BF16)))

    k_band = band(kap_ref, ka_ref)
    v_band = band(vap_ref, va_ref)

    qi = lax.broadcasted_iota(jnp.int32, (BLOCK, 2 * BLOCK), 0)
    kc = lax.broadcasted_iota(jnp.int32, (BLOCK, 2 * BLOCK), 1)
    first_valid = jnp.where(j == 0, BLOCK, 0)
    valid = (kc >= qi) & (kc <= qi + WINDOW) & (kc >= first_valid)

    for p in range(A_HEADS // 2):
        kv = (2 * p) // A_GROUP
        q_pair = qa_ref[:, p * 128:(p + 1) * 128]
        acc = None
        inv = []
        for parity in range(2):
            sink = sinks_ref[2 * p + parity]
            s = jnp.where(valid, _dot_nt(q_pair, k_band[kv][parity]), NEG_INF)
            mx = jnp.maximum(jnp.max(s, axis=-1, keepdims=True), sink)
            pr = jnp.exp(s - mx)
            denom = jnp.sum(pr, axis=-1, keepdims=True) + jnp.exp(sink - mx)
            inv.append(1.0 / denom)
            o = _dot(pr.astype(BF16), v_band[kv][parity])
            acc = o if acc is None else acc + o
        scale = jnp.where(lane_lo, inv[0], inv[1])
        y_ref[:, M_WIDTH + p * 128:M_WIDTH + (p + 1) * 128] = (acc * scale).astype(BF16)

    @pl.when(j == pl.num_programs(1) - 1)
    def _():
        c_out_ref[0] = c_ref[...]
        n_out_ref[0] = n_ref[...]
        m_out_ref[0] = m_ref[...]


def _prompt_mixer(sinks, qkv, om, gc, qa, ka, va, bias8, ogain, tri, batch, seq):
    nblk = seq // BLOCK
    n = batch * seq
    cur = lambda w: pl.BlockSpec((BLOCK, w), lambda b, j: (b * nblk + j, 0))
    prev = lambda w: pl.BlockSpec((BLOCK, w), lambda b, j: (b * nblk + jnp.maximum(j - 1, 0), 0))
    const = lambda shape: pl.BlockSpec(shape, lambda b, j: (0,) * len(shape))
    state = lambda shape: pl.BlockSpec((1,) + shape, lambda b, j: (b,) + (0,) * len(shape))
    out_shape = (
        jax.ShapeDtypeStruct((n, D_MODEL), BF16),
        jax.ShapeDtypeStruct((batch, M_HEADS, M_DIM, M_DIM), F32),
        jax.ShapeDtypeStruct((batch, M_HEADS, 8, M_DIM), F32),
        jax.ShapeDtypeStruct((batch, M_HEADS, 8, 128), F32),
    )
    return pl.pallas_call(
        _prompt_mixer_kernel,
        out_shape=out_shape,
        grid=(batch, nblk),
        in_specs=[pl.BlockSpec(memory_space=pltpu.SMEM),
                  cur(3 * M_WIDTH), cur(M_WIDTH), cur(GATE_PAD), cur(A_WIDTH), cur(KV_WIDTH),
                  cur(KV_WIDTH), prev(KV_WIDTH), prev(KV_WIDTH),
                  const(bias8.shape), const(ogain.shape), const(tri.shape)],
        out_specs=(cur(D_MODEL), state((M_HEADS, M_DIM, M_DIM)), state((M_HEADS, 8, M_DIM)),
                   state((M_HEADS, 8, 128))),
        scratch_shapes=[pltpu.VMEM((M_HEADS, M_DIM, M_DIM), F32), pltpu.VMEM((M_HEADS, 8, M_DIM), F32),
                        pltpu.VMEM((M_HEADS, 8, 128), F32)],
        compiler_params=pltpu.CompilerParams(dimension_semantics=("arbitrary", "arbitrary")),
        name="prompt_mixer",
    )(sinks, qkv, om, gc, qa, ka, va, ka, va, bias8, ogain, tri)


def _sample_mlstm_kernel(bi_ref, bf_ref, qt_ref, kt_ref, vt_ref, ot_ref, gi_ref, gf_ref, m0_ref, n0t_ref,
                         qr_ref, kr_ref, c_ref, gain_ref,
                         yt_ref, nt_out_ref, m_out_ref, c_out_ref):
    h = pl.program_id(0)
    nb = qr_ref.shape[0]
    qt, kt, vt = qt_ref[0], kt_ref[0], vt_ref[0]
    i_pre = gi_ref[0] + bi_ref[h]
    a = _log_sigmoid(gf_ref[0] + bf_ref[h]) + m0_ref[0]
    m_t = jnp.maximum(a, i_pre)
    w_inter = jnp.exp(a - m_t)
    w_in = jnp.exp(i_pre - m_t)
    scores = jnp.sum(qt * kt, axis=0, keepdims=True) * w_in
    n0t = n0t_ref[0]
    nq = jnp.sum(n0t * qt, axis=0, keepdims=True)

    rows = lax.broadcasted_iota(jnp.int32, (nb, M_DIM), 0)
    cols = lax.broadcasted_iota(jnp.int32, (M_DIM, nb), 1)
    lane_row = lax.broadcasted_iota(jnp.int32, (1, nb), 1)
    vw_t = (vt * w_in).astype(BF16)
    k_rows = kr_ref[...]

    def body(b, cq_t):
        c_b = c_ref[b, 0]
        q_row = qr_ref[pl.ds(b, 1), :]
        col = jnp.sum(c_b * q_row, axis=-1, keepdims=True)
        cq_t = jnp.where(cols == b, col, cq_t)
        k_only_b = jnp.where(rows == b, k_rows, 0.0).astype(BF16)
        outer = _dot(vw_t, k_only_b)
        decay_b = jnp.sum(jnp.where(lane_row == b, w_inter, 0.0), axis=-1, keepdims=True)
        c_out_ref[b, 0] = decay_b * c_b + outer
        return cq_t

    cq_t = lax.fori_loop(0, nb, body, jnp.zeros((M_DIM, nb), F32))

    num = w_inter * cq_t + scores * vt
    den = w_inter * nq + scores
    hh = num / jnp.maximum(jnp.abs(den), jnp.exp(-m_t))
    hh = hh * lax.rsqrt(jnp.mean(hh * hh, axis=0, keepdims=True) + RMS_EPS)
    yt_ref[0] = hh * gain_ref[0] * jax.nn.sigmoid(ot_ref[0])
    nt_out_ref[0] = w_inter * n0t + w_in * kt
    m_out_ref[0] = m_t


def _sample_mlstm(b_i, b_f, qt, kt, vt, ot, gi, gf, m0, n0t, q_rows, k_rows, c0, gain_col):
    nb = q_rows.shape[0]
    head3 = lambda r: pl.BlockSpec((1, r, nb), lambda h: (h, 0, 0))
    smem = pl.BlockSpec(memory_space=pltpu.SMEM)
    out_shape = (
        jax.ShapeDtypeStruct((M_HEADS, M_DIM, nb), F32),
        jax.ShapeDtypeStruct((M_HEADS, M_DIM, nb), F32),
        jax.ShapeDtypeStruct((M_HEADS, 1, nb), F32),
        jax.ShapeDtypeStruct((nb, M_HEADS, M_DIM, M_DIM), F32),
    )
    c_spec = pl.BlockSpec((nb, 1, M_DIM, M_DIM), lambda h: (0, h, 0, 0))
    return pl.pallas_call(
        _sample_mlstm_kernel,
        out_shape=out_shape,
        grid=(M_HEADS,),
        in_specs=[smem, smem, head3(M_DIM), head3(M_DIM), head3(M_DIM), head3(M_DIM), head3(1), head3(1),
                  head3(1), head3(M_DIM),
                  pl.BlockSpec((nb, M_DIM), lambda h: (0, h)), pl.BlockSpec((nb, M_DIM), lambda h: (0, h)),
                  c_spec, pl.BlockSpec((1, M_DIM, 1), lambda h: (h, 0, 0))],
        out_specs=(head3(M_DIM), head3(M_DIM), head3(1), c_spec),
        compiler_params=pltpu.CompilerParams(dimension_semantics=("arbitrary",),
                                             vmem_limit_bytes=VMEM_LIMIT_BYTES),
        name="sample_mlstm",
    )(b_i, b_f, qt, kt, vt, ot, gi, gf, m0, n0t, q_rows, k_rows, c0, gain_col)


SAMPLE_TILE = 8


def _sample_swa_kernel(q2_ref, kc_ref, vc_ref, kn_ref, vn_ref, sink_ref, o_ref, ko_ref, vo_ref):
    sink = sink_ref[...]
    w = kc_ref.shape[1]
    for b in range(SAMPLE_TILE):
        q2 = q2_ref[b]
        k_new = kn_ref[b:b + 1, :]
        v_new = vn_ref[b:b + 1, :]
        s_c = _dot_nt(q2, kc_ref[b].astype(BF16))
        s_n = jnp.sum(q2.astype(F32) * k_new, axis=-1, keepdims=True)
        mx = jnp.maximum(jnp.maximum(jnp.max(s_c, axis=-1, keepdims=True), s_n), sink)
        p_c = jnp.exp(s_c - mx)
        p_n = jnp.exp(s_n - mx)
        denom = jnp.sum(p_c, axis=-1, keepdims=True) + p_n + jnp.exp(sink - mx)
        o = _dot(p_c.astype(BF16), vc_ref[b].astype(BF16)) + p_n * v_new
        o_ref[b] = o / denom
        ko_ref[b, 0:w - 1, :] = kc_ref[b, 1:w, :]
        ko_ref[b, w - 1:w, :] = k_new
        vo_ref[b, 0:w - 1, :] = vc_ref[b, 1:w, :]
        vo_ref[b, w - 1:w, :] = v_new


def _sample_swa(q2, k_cache, v_cache, k_new, v_new, sink_col):
    nb, w, _ = k_cache.shape
    t3 = lambda a, c: pl.BlockSpec((SAMPLE_TILE, a, c), lambda i: (i, 0, 0))
    t2 = pl.BlockSpec((SAMPLE_TILE, KV_WIDTH), lambda i: (i, 0))
    out_shape = (
        jax.ShapeDtypeStruct((nb, A_HEADS, KV_WIDTH), F32),
        jax.ShapeDtypeStruct((nb, w, KV_WIDTH), F32),
        jax.ShapeDtypeStruct((nb, w, KV_WIDTH), F32),
    )
    return pl.pallas_call(
        _sample_swa_kernel,
        out_shape=out_shape,
        grid=(nb // SAMPLE_TILE,),
        in_specs=[t3(A_HEADS, KV_WIDTH), t3(w, KV_WIDTH), t3(w, KV_WIDTH), t2, t2,
                  pl.BlockSpec((A_HEADS, 1), lambda i: (0, 0))],
        out_specs=(t3(A_HEADS, KV_WIDTH), t3(w, KV_WIDTH), t3(w, KV_WIDTH)),
        compiler_params=pltpu.CompilerParams(dimension_semantics=("arbitrary",)),
        name="sample_swa",
    )(q2, k_cache, v_cache, k_new, v_new, sink_col)


def _blockdiag_ones(n, group):
    idx = jnp.arange(n) // group
    return (idx[:, None] == idx[None, :]).astype(BF16)


def _prep_weights(ffn1_w_gate, ffn1_w_up, ffn1_w_down, w_in, w_out, ffn2_w_gate, ffn2_w_up, ffn2_w_down):
    def cols(w):
        return jnp.transpose(w.astype(BF16).reshape(D_MODEL, N_FF_CHUNKS, FF_CHUNK), (1, 0, 2))

    def rows(w):
        return w.astype(BF16).reshape(N_FF_CHUNKS, FF_CHUNK, D_MODEL)

    m_end = 4 * M_WIDTH
    g_end = m_end + 2 * M_HEADS
    q_end = g_end + A_WIDTH
    wm = w_in[:, :m_end].astype(BF16)
    wqa = w_in[:, g_end:q_end].astype(BF16)
    gates = jnp.pad(w_in[:, m_end:g_end], ((0, 0), (0, GATE_PAD - 2 * M_HEADS)))
    wkvg = jnp.concatenate([w_in[:, q_end:], gates], axis=1).astype(BF16)
    return (cols(ffn1_w_gate), cols(ffn1_w_up), rows(ffn1_w_down), wm, wqa, wkvg, w_out.astype(BF16),
            cols(ffn2_w_gate), cols(ffn2_w_up), rows(ffn2_w_down))


def kernel(x_prompt, x_sample, cache_swa_k, cache_swa_v, state_mlstm_C, state_mlstm_n, state_mlstm_m,
           ffn1_norm, ffn1_w_gate, ffn1_w_up, ffn1_w_down, mix_norm, w_in, mlstm_b_i, mlstm_b_f,
           mlstm_out_norm, swa_q_norm, swa_k_norm, swa_sinks, w_out, ffn2_norm, ffn2_w_gate,
           ffn2_w_up, ffn2_w_down):
    depth = ffn1_norm.shape[0]
    assert depth == 1
    batch, seq, _ = x_prompt.shape
    nb = x_sample.shape[0]
    assert x_sample.shape[1] == 1 and seq % BLOCK == 0

    (wg1, wu1, wd1, wm, wqa, wkvg, wo, wg2, wu2, wd2) = _prep_weights(
        ffn1_w_gate[0], ffn1_w_up[0], ffn1_w_down[0], w_in[0], w_out[0],
        ffn2_w_gate[0], ffn2_w_up[0], ffn2_w_down[0])
    n1 = ffn1_norm[0].reshape(1, D_MODEL)
    n2 = mix_norm[0].reshape(1, D_MODEL)
    n3 = ffn2_norm[0].reshape(1, D_MODEL)
    qgain = jnp.tile(swa_q_norm[0], A_HEADS).reshape(1, A_WIDTH)
    kgain = jnp.tile(swa_k_norm[0], KV_HEADS).reshape(1, KV_WIDTH)
    e512 = _blockdiag_ones(A_WIDTH, A_DIM)
    e128 = _blockdiag_ones(KV_WIDTH, A_DIM)
    front = functools.partial(_front, n1=n1, wg=wg1, wu=wu1, wd=wd1, n2=n2, wm=wm, wqa=wqa, wkvg=wkvg,
                              qgain=qgain, kgain=kgain, e512=e512, e128=e128)
    back = functools.partial(_back, wo=wo, nrm=n3, wg=wg2, wu=wu2, wd=wd2)
    b_i, b_f = mlstm_b_i[0], mlstm_b_f[0]
    ogain = mlstm_out_norm[0].reshape(1, M_WIDTH)
    sinks = swa_sinks[0]

    xp = x_prompt.reshape(batch * seq, D_MODEL)
    x1, qkv, om, qa, ka, va, gc = front(xp, tm=512)
    bias8 = jnp.concatenate([b_i, b_f]).reshape(2 * M_HEADS, 1)
    tri = (jnp.arange(BLOCK)[:, None] <= jnp.arange(BLOCK)[None, :]).astype(F32)
    y, pc, pn, pm = _prompt_mixer(sinks, qkv, om, gc, qa, ka, va, bias8, ogain, tri, batch, seq)
    yp = back(x1, y, tm=512).reshape(batch, seq, D_MODEL)
    w = min(WINDOW, seq)
    pk = ka.reshape(batch, seq, KV_HEADS, A_DIM)[:, seq - w:]
    pv = va.reshape(batch, seq, KV_HEADS, A_DIM)[:, seq - w:]
    pn = pn[:, :, 0, :]
    pm = pm[:, :, 0, 0]

    xs = x_sample.reshape(nb, D_MODEL)
    x1s, qkvs, oms, qas, kas, vas, gcs = front(xs, tm=nb)
    qkv_f = qkvs.astype(F32)
    to_heads_t = lambda a: jnp.transpose(a.reshape(nb, M_HEADS, M_DIM), (1, 2, 0))
    q_rows = qkv_f[:, 0:M_WIDTH]
    k_rows = qkv_f[:, M_WIDTH:2 * M_WIDTH]
    gates_t = jnp.transpose(gcs[:, 0:2 * M_HEADS])
    ymt, nt, mt, sc = _sample_mlstm(
        b_i, b_f, to_heads_t(q_rows), to_heads_t(k_rows), to_heads_t(qkv_f[:, 2 * M_WIDTH:]), to_heads_t(oms),
        gates_t[0:M_HEADS].reshape(M_HEADS, 1, nb), gates_t[M_HEADS:].reshape(M_HEADS, 1, nb),
        jnp.transpose(state_mlstm_m[0]).reshape(M_HEADS, 1, nb), jnp.transpose(state_mlstm_n[0], (1, 2, 0)),
        q_rows, k_rows, state_mlstm_C[0], mlstm_out_norm[0].reshape(M_HEADS, M_DIM, 1))
    y_m = jnp.transpose(ymt, (2, 0, 1)).reshape(nb, M_WIDTH)
    sn = jnp.transpose(nt, (2, 0, 1))
    sm = jnp.transpose(mt[:, 0, :])

    qa_h = qas.reshape(nb, A_HEADS, A_DIM)
    zeros = jnp.zeros_like(qa_h)
    in_lo = (jnp.arange(A_HEADS) // A_GROUP == 0)[None, :, None]
    q2 = jnp.concatenate([jnp.where(in_lo, qa_h, zeros), jnp.where(in_lo, zeros, qa_h)], axis=-1)
    kc = cache_swa_k[0].reshape(nb, -1, KV_WIDTH)
    vc = cache_swa_v[0].reshape(nb, -1, KV_WIDTH)
    o2, sk, sv = _sample_swa(q2, kc, vc, kas, vas, sinks.reshape(A_HEADS, 1))
    o2 = o2.reshape(nb, A_HEADS, KV_HEADS, A_DIM)
    y_a = jnp.where(in_lo, o2[:, :, 0, :], o2[:, :, 1, :]).reshape(nb, A_WIDTH)
    ys_in = jnp.concatenate([y_m, y_a], axis=-1).astype(BF16)
    ys = back(x1s, ys_in, tm=nb).reshape(nb, 1, D_MODEL)

    wb = kc.shape[1]
    return (yp, ys, pk[None], pv[None], pc[None], pn[None], pm[None],
            sk.reshape(1, nb, wb, KV_HEADS, A_DIM), sv.reshape(1, nb, wb, KV_HEADS, A_DIM),
            sc[None], sn[None], sm[None])
```

```python
import functools

import jax
import jax.numpy as jnp
from jax import lax
from jax.experimental import pallas as pl
from jax.experimental.pallas import tpu as pltpu

F32 = jnp.float32
BF16 = jnp.bfloat16

D_MODEL = 1024
D_FF = 2816
FF_CHUNK = 256
N_FF_CHUNKS = D_FF // FF_CHUNK
M_HEADS = 4
M_DIM = 128
M_WIDTH = M_HEADS * M_DIM
A_HEADS = 8
A_DIM = 64
A_WIDTH = A_HEADS * A_DIM
KV_HEADS = 2
KV_WIDTH = KV_HEADS * A_DIM
A_GROUP = A_HEADS // KV_HEADS
WINDOW = 128
BLOCK = 128
GATE_PAD = 128
RMS_EPS = 1e-6
FFN_RES_WEIGHT = 0.5
NEG_INF = float("-inf")
VMEM_LIMIT_BYTES = 56 * 1024 * 1024


def _dot(a, b):
    return jnp.dot(a, b, preferred_element_type=F32)


def _dot_nt(a, b):
    return lax.dot_general(a, b, (((1,), (1,)), ((), ())), preferred_element_type=F32)


def _rms_rows(x, gain):
    ms = jnp.mean(x * x, axis=-1, keepdims=True)
    return x * lax.rsqrt(ms + RMS_EPS) * gain


def _log_sigmoid(x):
    return jnp.minimum(x, 0.0) - jnp.log1p(jnp.exp(-jnp.abs(x)))


def _group_mean_sq(x, ones_blockdiag, group):
    sq = x * x
    hi = sq.astype(BF16)
    lo = (sq - hi.astype(F32)).astype(BF16)
    return (_dot(hi, ones_blockdiag) + _dot(lo, ones_blockdiag)) * (1.0 / group)


def _ffn_into(h_ref, wg_ref, wu_ref, wd_ref, acc_ref):
    acc_ref[...] = jnp.zeros_like(acc_ref)

    def body(c, carry):
        h = h_ref[...]
        g = _dot(h, wg_ref[c])
        u = _dot(h, wu_ref[c])
        a = (g * jax.nn.sigmoid(g) * u).astype(BF16)
        acc_ref[...] += _dot(a, wd_ref[c])
        return carry

    lax.fori_loop(0, N_FF_CHUNKS, body, 0)


def _front_kernel(x_ref, n1_ref, wg_ref, wu_ref, wd_ref, n2_ref, wm_ref, wqa_ref, wkvg_ref,
                  qgain_ref, kgain_ref, e512_ref, e128_ref,
                  x1_ref, qkv_ref, om_ref, qa_ref, ka_ref, va_ref, gt_ref, vt_ref, kx_ref, vx_ref,
                  h_ref, acc_ref):
    x = x_ref[...]
    h_ref[...] = _rms_rows(x, n1_ref[...]).astype(BF16)
    _ffn_into(h_ref, wg_ref, wu_ref, wd_ref, acc_ref)
    x1 = x + FFN_RES_WEIGHT * acc_ref[...]
    x1_ref[...] = x1
    h_ref[...] = _rms_rows(x1, n2_ref[...]).astype(BF16)
    h = h_ref[...]

    qkv_ref[:, 0:M_WIDTH] = _dot(h, wm_ref[:, 0:M_WIDTH]).astype(BF16)
    k_m = _dot(h, wm_ref[:, M_WIDTH:2 * M_WIDTH]) * (M_DIM ** -0.5)
    qkv_ref[:, M_WIDTH:2 * M_WIDTH] = k_m.astype(BF16)
    v_m = _dot(h, wm_ref[:, 2 * M_WIDTH:3 * M_WIDTH])
    qkv_ref[:, 2 * M_WIDTH:3 * M_WIDTH] = v_m.astype(BF16)
    for c in range(vt_ref.shape[0]):
        for hd in range(M_HEADS):
            blk = v_m[c * BLOCK:(c + 1) * BLOCK, hd * M_DIM:(hd + 1) * M_DIM]
            vt_ref[c, hd] = blk.T.astype(BF16)
    om_ref[...] = _dot(h, wm_ref[:, 3 * M_WIDTH:4 * M_WIDTH])

    qa = _dot(h, wqa_ref[...])
    q_ms = _group_mean_sq(qa, e512_ref[...], A_DIM)
    qa_ref[...] = (qa * lax.rsqrt(q_ms + RMS_EPS) * qgain_ref[...] * (A_DIM ** -0.5)).astype(BF16)

    kvg = _dot(h, wkvg_ref[...])
    ka = kvg[:, 0:KV_WIDTH]
    k_ms = _group_mean_sq(ka, e128_ref[...], A_DIM)
    ka = ka * lax.rsqrt(k_ms + RMS_EPS) * kgain_ref[...]
    va = kvg[:, KV_WIDTH:2 * KV_WIDTH]
    ka_ref[...] = ka
    va_ref[...] = va
    gt_ref[...] = kvg[:, 2 * KV_WIDTH:2 * KV_WIDTH + GATE_PAD].T[0:2 * M_HEADS, :]

    in_lo = lax.broadcasted_iota(jnp.int32, ka.shape, 1) < A_DIM
    for src, dst in ((ka, kx_ref), (va, vx_ref)):
        x0 = jnp.where(in_lo, src, 0.0)
        x1 = jnp.where(in_lo, 0.0, src)
        dst[:, 0:128] = x0.astype(BF16)
        dst[:, 128:256] = pltpu.roll(x0, A_DIM, 1).astype(BF16)
        dst[:, 256:384] = pltpu.roll(x1, A_DIM, 1).astype(BF16)
        dst[:, 384:512] = x1.astype(BF16)


def _const_spec(shape):
    nd = len(shape)
    return pl.BlockSpec(shape, lambda i: (0,) * nd, pipeline_mode=pl.Buffered(1))


def _front(x2d, n1, wg, wu, wd, n2, wm, wqa, wkvg, qgain, kgain, e512, e128, tm):
    n = x2d.shape[0]
    row = lambda w: pl.BlockSpec((tm, w), lambda i: (i, 0))
    out_shape = (
        jax.ShapeDtypeStruct((n, D_MODEL), F32),
        jax.ShapeDtypeStruct((n, 3 * M_WIDTH), BF16),
        jax.ShapeDtypeStruct((n, M_WIDTH), F32),
        jax.ShapeDtypeStruct((n, A_WIDTH), BF16),
        jax.ShapeDtypeStruct((n, KV_WIDTH), F32),
        jax.ShapeDtypeStruct((n, KV_WIDTH), F32),
        jax.ShapeDtypeStruct((2 * M_HEADS, n), F32),
        jax.ShapeDtypeStruct((n // BLOCK, M_HEADS, M_DIM, BLOCK), BF16),
        jax.ShapeDtypeStruct((n, 4 * KV_WIDTH), BF16),
        jax.ShapeDtypeStruct((n, 4 * KV_WIDTH), BF16),
    )
    nb_t = tm // BLOCK
    return pl.pallas_call(
        _front_kernel,
        out_shape=out_shape,
        grid=(n // tm,),
        in_specs=[row(D_MODEL), _const_spec(n1.shape), _const_spec(wg.shape), _const_spec(wu.shape),
                  _const_spec(wd.shape), _const_spec(n2.shape), _const_spec(wm.shape),
                  _const_spec(wqa.shape), _const_spec(wkvg.shape), _const_spec(qgain.shape),
                  _const_spec(kgain.shape), _const_spec(e512.shape), _const_spec(e128.shape)],
        out_specs=(row(D_MODEL), row(3 * M_WIDTH), row(M_WIDTH), row(A_WIDTH), row(KV_WIDTH),
                   row(KV_WIDTH), pl.BlockSpec((2 * M_HEADS, tm), lambda i: (0, i)),
                   pl.BlockSpec((nb_t, M_HEADS, M_DIM, BLOCK), lambda i: (i, 0, 0, 0)),
                   row(4 * KV_WIDTH), row(4 * KV_WIDTH)),
        scratch_shapes=[pltpu.VMEM((tm, D_MODEL), BF16), pltpu.VMEM((tm, D_MODEL), F32)],
        compiler_params=pltpu.CompilerParams(dimension_semantics=("arbitrary",),
                                             vmem_limit_bytes=VMEM_LIMIT_BYTES),
        name="front",
    )(x2d, n1, wg, wu, wd, n2, wm, wqa, wkvg, qgain, kgain, e512, e128)


def _back_kernel(x1_ref, y_ref, wo_ref, n_ref, wg_ref, wu_ref, wd_ref, out_ref, h_ref, acc_ref):
    x2 = x1_ref[...] + _dot(y_ref[...], wo_ref[...])
    h_ref[...] = _rms_rows(x2, n_ref[...]).astype(BF16)
    _ffn_into(h_ref, wg_ref, wu_ref, wd_ref, acc_ref)
    out_ref[...] = x2 + FFN_RES_WEIGHT * acc_ref[...]


def _back(x1, y, wo, nrm, wg, wu, wd, tm):
    n = x1.shape[0]
    row = lambda w: pl.BlockSpec((tm, w), lambda i: (i, 0))
    return pl.pallas_call(
        _back_kernel,
        out_shape=jax.ShapeDtypeStruct((n, D_MODEL), F32),
        grid=(n // tm,),
        in_specs=[row(D_MODEL), row(D_MODEL), _const_spec(wo.shape), _const_spec(nrm.shape),
                  _const_spec(wg.shape), _const_spec(wu.shape), _const_spec(wd.shape)],
        out_specs=row(D_MODEL),
        scratch_shapes=[pltpu.VMEM((tm, D_MODEL), BF16), pltpu.VMEM((tm, D_MODEL), F32)],
        compiler_params=pltpu.CompilerParams(dimension_semantics=("arbitrary",),
                                             vmem_limit_bytes=VMEM_LIMIT_BYTES),
        name="back",
    )(x1, y, wo, nrm, wg, wu, wd)


def _gate_prep_kernel(gt_ref, bias_ref, tri_ref, rows_ref, cols_ref, m_ref):
    nblk = rows_ref.shape[1]
    sub = lax.broadcasted_iota(jnp.int32, (8, BLOCK), 0)
    lane = lax.broadcasted_iota(jnp.int32, (8, BLOCK), 1)
    is_head = sub < M_HEADS
    pad = jnp.zeros((BLOCK - 24, BLOCK), F32)
    m_prev = jnp.zeros((8, 1), F32)
    for j in range(nblk):
        pre = gt_ref[:, j * BLOCK:(j + 1) * BLOCK] + bias_ref[...]
        r = jnp.where(is_head, pre, _log_sigmoid(pre))
        cum = jnp.dot(r, tri_ref[...], preferred_element_type=F32, precision=lax.Precision.HIGHEST)
        bcum = pltpu.roll(cum, M_HEADS, 0)
        g = jnp.where(is_head, r - bcum, 0.0)
        bcum = jnp.where(is_head, bcum, 0.0)
        cm = g
        for shift in (1, 2, 4, 8, 16, 32, 64):
            cm = jnp.maximum(cm, jnp.where(lane >= shift, pltpu.roll(cm, shift, 1), NEG_INF))
        cm_last = jnp.max(cm, axis=-1, keepdims=True)
        b_last = jnp.sum(jnp.where(lane == BLOCK - 1, bcum, 0.0), axis=-1, keepdims=True)
        mx = jnp.maximum(m_prev, cm)
        mx_last = jnp.maximum(m_prev, cm_last)
        rows_ref[0, j, 0] = g
        rows_ref[0, j, 1] = jnp.exp(g - mx_last)
        rows_ref[0, j, 2] = jnp.broadcast_to(jnp.exp(m_prev - mx_last), (8, BLOCK))
        col_src = jnp.concatenate([-mx, jnp.exp(m_prev - mx), jnp.exp(-(bcum + mx)), pad], axis=0)
        cols_ref[j * BLOCK:(j + 1) * BLOCK, :] = col_src.T
        m_prev = b_last + mx_last
    m_ref[0] = jnp.broadcast_to(m_prev, (8, BLOCK))


def _gate_prep(gt, bias8, tri, batch, seq):
    nblk = seq // BLOCK
    out_shape = (
        jax.ShapeDtypeStruct((batch, nblk, 3, 8, BLOCK), F32),
        jax.ShapeDtypeStruct((batch * seq, 128), F32),
        jax.ShapeDtypeStruct((batch, 8, BLOCK), F32),
    )
    return pl.pallas_call(
        _gate_prep_kernel,
        out_shape=out_shape,
        grid=(batch,),
        in_specs=[pl.BlockSpec((2 * M_HEADS, seq), lambda b: (0, b)),
                  pl.BlockSpec(bias8.shape, lambda b: (0, 0)), pl.BlockSpec(tri.shape, lambda b: (0, 0))],
        out_specs=(pl.BlockSpec((1, nblk, 3, 8, BLOCK), lambda b: (b, 0, 0, 0, 0)),
                   pl.BlockSpec((seq, 128), lambda b: (b, 0)),
                   pl.BlockSpec((1, 8, BLOCK), lambda b: (b, 0, 0))),
        compiler_params=pltpu.CompilerParams(dimension_semantics=("arbitrary",)),
        name="gate_prep",
    )(gt, bias8, tri)


def _prompt_mixer_kernel(sinks_ref, qkv_ref, vt_ref, om_ref, cols_ref, rows_ref, qa_ref,
                         kx_ref, vx_ref, kxp_ref, vxp_ref, ogain_ref,
                         y_ref, c_out_ref, n_out_ref,
                         c_ref, n_ref):
    j = pl.program_id(0)
    batch = qkv_ref.shape[0]

    @pl.when(j == 0)
    def _():
        c_ref[...] = jnp.zeros_like(c_ref)
        n_ref[...] = jnp.zeros_like(n_ref)

    rows = lax.broadcasted_iota(jnp.int32, (BLOCK, BLOCK), 0)
    cols = lax.broadcasted_iota(jnp.int32, (BLOCK, BLOCK), 1)
    causal = cols <= rows
    lane_lo = lax.broadcasted_iota(jnp.int32, (2 * BLOCK, KV_WIDTH), 1) < A_DIM
    ones_m = jnp.ones((BLOCK, M_DIM), BF16)
    ones_half = (jnp.where(lane_lo, 1.0, 0.0).astype(BF16), jnp.where(lane_lo, 0.0, 1.0).astype(BF16))
    qi = lax.broadcasted_iota(jnp.int32, (2 * BLOCK, 2 * BLOCK), 0) % BLOCK
    kc = lax.broadcasted_iota(jnp.int32, (2 * BLOCK, 2 * BLOCK), 1)
    first_valid = jnp.where(j == 0, BLOCK, 0)
    valid = (kc >= qi) & (kc <= qi + WINDOW) & (kc >= first_valid)
    top_rows = lax.broadcasted_iota(jnp.int32, (2 * BLOCK, 1), 0) < BLOCK

    for b in range(batch):
        col_b = cols_ref[b]
        for h in range(M_HEADS):
            lo, hi = h * M_DIM, (h + 1) * M_DIM
            q = qkv_ref[b, :, lo:hi]
            k = qkv_ref[b, :, M_WIDTH + lo:M_WIDTH + hi]
            v = qkv_ref[b, :, 2 * M_WIDTH + lo:2 * M_WIDTH + hi]
            a_col = col_b[:, h:h + 1]
            wi_col = col_b[:, 8 + h:9 + h]
            einv_col = col_b[:, 16 + h:17 + h]
            g_row = rows_ref[b, 0, 0][h:h + 1, :]
            wend_row = rows_ref[b, 0, 1][h:h + 1, :]
            decay = rows_ref[b, 0, 2][h:h + 1, :]
            c_prev = c_ref[b, h]
            n_prev = n_ref[b, h]

            n_rep = jnp.broadcast_to(n_prev[0:1, :], (BLOCK, M_DIM)).astype(BF16)
            qkc = _dot_nt(q, jnp.concatenate([k, c_prev.astype(BF16), n_rep], axis=0))
            d = jnp.where(causal, jnp.exp(a_col + g_row), 0.0)
            s = (qkc[:, 0:BLOCK] * d).astype(BF16)
            sv = _dot(s, jnp.concatenate([v, ones_m], axis=1))
            num = wi_col * qkc[:, BLOCK:2 * BLOCK] + sv[:, 0:M_DIM]
            den = wi_col * qkc[:, 2 * BLOCK:3 * BLOCK] + sv[:, M_DIM:2 * M_DIM]
            hh = num / jnp.maximum(jnp.abs(den), einv_col)
            hh = hh * lax.rsqrt(jnp.mean(hh * hh, axis=-1, keepdims=True) + RMS_EPS)
            hh = hh * ogain_ref[:, lo:hi] * jax.nn.sigmoid(om_ref[b, :, lo:hi])
            y_ref[b, :, lo:hi] = hh.astype(BF16)

            vw_t = (vt_ref[b, 0, h].astype(F32) * wend_row).astype(BF16)
            w_rep = jnp.broadcast_to(wend_row, (16, BLOCK)).astype(BF16)
            upd = _dot(jnp.concatenate([vw_t, w_rep], axis=0), k)
            c_ref[b, h] = decay * c_prev + upd[0:M_DIM]
            n_ref[b, h] = decay * n_prev + upd[M_DIM:M_DIM + 8]

        for kv in range(KV_HEADS):
            p0, p1 = 2 * kv, 2 * kv + 1
            q2 = jnp.concatenate([qa_ref[b, :, p0 * 128:(p0 + 1) * 128],
                                  qa_ref[b, :, p1 * 128:(p1 + 1) * 128]], axis=0)
            acc = None
            e_sink = []
            for parity in range(2):
                var = (2 * kv + parity) * KV_WIDTH
                k_band = jnp.concatenate([kxp_ref[b, :, var:var + KV_WIDTH], kx_ref[b, :, var:var + KV_WIDTH]],
                                         axis=0)
                v_band = jnp.concatenate([vxp_ref[b, :, var:var + KV_WIDTH], vx_ref[b, :, var:var + KV_WIDTH]],
                                         axis=0)
                sink = jnp.where(top_rows, sinks_ref[2 * p0 + parity], sinks_ref[2 * p1 + parity])
                sc = jnp.where(valid, _dot_nt(q2, k_band), NEG_INF)
                mx = jnp.maximum(jnp.max(sc, axis=-1, keepdims=True), sink)
                pr = jnp.exp(sc - mx).astype(BF16)
                e_sink.append(jnp.exp(sink - mx))
                o = _dot(pr, jnp.concatenate([v_band, ones_half[parity]], axis=1))
                acc = o if acc is None else acc + o
            denom = acc[:, KV_WIDTH:] + jnp.where(lane_lo, e_sink[0], e_sink[1])
            ya = (acc[:, 0:KV_WIDTH] / denom).astype(BF16)
            y_ref[b, :, M_WIDTH + p0 * 128:M_WIDTH + (p0 + 1) * 128] = ya[0:BLOCK]
            y_ref[b, :, M_WIDTH + p1 * 128:M_WIDTH + (p1 + 1) * 128] = ya[BLOCK:2 * BLOCK]

    @pl.when(j == pl.num_programs(0) - 1)
    def _():
        c_out_ref[...] = c_ref[...]
        n_out_ref[...] = n_ref[...]


def _prompt_mixer(sinks, qkv, vt, om, cols, rows, qa, kx, vx, ogain, batch, seq):
    nblk = seq // BLOCK
    r3 = lambda a: a.reshape(batch, seq, a.shape[-1])
    cur = lambda w: pl.BlockSpec((batch, BLOCK, w), lambda j: (0, j, 0))
    prev = lambda w: pl.BlockSpec((batch, BLOCK, w), lambda j: (0, jnp.maximum(j - 1, 0), 0))
    state = lambda shape: pl.BlockSpec((batch,) + shape, lambda j: (0,) * (len(shape) + 1))
    out_shape = (
        jax.ShapeDtypeStruct((batch, seq, D_MODEL), BF16),
        jax.ShapeDtypeStruct((batch, M_HEADS, M_DIM, M_DIM), F32),
        jax.ShapeDtypeStruct((batch, M_HEADS, 8, M_DIM), F32),
    )
    y, pc, pn = pl.pallas_call(
        _prompt_mixer_kernel,
        out_shape=out_shape,
        grid=(nblk,),
        in_specs=[pl.BlockSpec(memory_space=pltpu.SMEM),
                  cur(3 * M_WIDTH),
                  pl.BlockSpec((batch, 1, M_HEADS, M_DIM, BLOCK), lambda j: (0, j, 0, 0, 0)),
                  cur(M_WIDTH), cur(128),
                  pl.BlockSpec((batch, 1, 3, 8, BLOCK), lambda j: (0, j, 0, 0, 0)),
                  cur(A_WIDTH), cur(4 * KV_WIDTH), cur(4 * KV_WIDTH), prev(4 * KV_WIDTH), prev(4 * KV_WIDTH),
                  pl.BlockSpec(ogain.shape, lambda j: (0, 0))],
        out_specs=(cur(D_MODEL), state((M_HEADS, M_DIM, M_DIM)), state((M_HEADS, 8, M_DIM))),
        scratch_shapes=[pltpu.VMEM((batch, M_HEADS, M_DIM, M_DIM), F32),
                        pltpu.VMEM((batch, M_HEADS, 8, M_DIM), F32)],
        compiler_params=pltpu.CompilerParams(dimension_semantics=("arbitrary",),
                                             vmem_limit_bytes=VMEM_LIMIT_BYTES),
        name="prompt_mixer",
    )(sinks, r3(qkv), vt.reshape(batch, nblk, M_HEADS, M_DIM, BLOCK), r3(om), r3(cols), rows, r3(qa),
      r3(kx), r3(vx), r3(kx), r3(vx), ogain)
    return y.reshape(batch * seq, D_MODEL), pc, pn


def _sample_mlstm_kernel(bi_ref, bf_ref, qt_ref, kt_ref, vt_ref, ot_ref, gi_ref, gf_ref, m0_ref, n0t_ref,
                         qr_ref, kr_ref, c_ref, gain_ref,
                         yt_ref, nt_out_ref, m_out_ref, c_out_ref):
    h = pl.program_id(0)
    nb = qr_ref.shape[0]
    qt, kt, vt = qt_ref[0], kt_ref[0], vt_ref[0]
    i_pre = gi_ref[0] + bi_ref[h]
    a = _log_sigmoid(gf_ref[0] + bf_ref[h]) + m0_ref[0]
    m_t = jnp.maximum(a, i_pre)
    w_inter = jnp.exp(a - m_t)
    w_in = jnp.exp(i_pre - m_t)
    scores = jnp.sum(qt * kt, axis=0, keepdims=True) * w_in
    n0t = n0t_ref[0]
    nq = jnp.sum(n0t * qt, axis=0, keepdims=True)

    rows = lax.broadcasted_iota(jnp.int32, (nb, M_DIM), 0)
    cols = lax.broadcasted_iota(jnp.int32, (M_DIM, nb), 1)
    lane_row = lax.broadcasted_iota(jnp.int32, (1, nb), 1)
    vw_t = (vt * w_in).astype(BF16)
    k_rows = kr_ref[...]

    def body(b, cq_t):
        c_b = c_ref[b, 0]
        q_row = qr_ref[pl.ds(b, 1), :]
        col = jnp.sum(c_b * q_row, axis=-1, keepdims=True)
        cq_t = jnp.where(cols == b, col, cq_t)
        k_only_b = jnp.where(rows == b, k_rows, 0.0).astype(BF16)
        outer = _dot(vw_t, k_only_b)
        decay_b = jnp.sum(jnp.where(lane_row == b, w_inter, 0.0), axis=-1, keepdims=True)
        c_out_ref[b, 0] = decay_b * c_b + outer
        return cq_t

    cq_t = lax.fori_loop(0, nb, body, jnp.zeros((M_DIM, nb), F32))

    num = w_inter * cq_t + scores * vt
    den = w_inter * nq + scores
    hh = num / jnp.maximum(jnp.abs(den), jnp.exp(-m_t))
    hh = hh * lax.rsqrt(jnp.mean(hh * hh, axis=0, keepdims=True) + RMS_EPS)
    yt_ref[0] = hh * gain_ref[0] * jax.nn.sigmoid(ot_ref[0])
    nt_out_ref[0] = w_inter * n0t + w_in * kt
    m_out_ref[0] = m_t


def _sample_mlstm(b_i, b_f, qt, kt, vt, ot, gi, gf, m0, n0t, q_rows, k_rows, c0, gain_col):
    nb = q_rows.shape[0]
    head3 = lambda r: pl.BlockSpec((1, r, nb), lambda h: (h, 0, 0))
    smem = pl.BlockSpec(memory_space=pltpu.SMEM)
    out_shape = (
        jax.ShapeDtypeStruct((M_HEADS, M_DIM, nb), F32),
        jax.ShapeDtypeStruct((M_HEADS, M_DIM, nb), F32),
        jax.ShapeDtypeStruct((M_HEADS, 1, nb), F32),
        jax.ShapeDtypeStruct((nb, M_HEADS, M_DIM, M_DIM), F32),
    )
    c_spec = pl.BlockSpec((nb, 1, M_DIM, M_DIM), lambda h: (0, h, 0, 0))
    return pl.pallas_call(
        _sample_mlstm_kernel,
        out_shape=out_shape,
        grid=(M_HEADS,),
        in_specs=[smem, smem, head3(M_DIM), head3(M_DIM), head3(M_DIM), head3(M_DIM), head3(1), head3(1),
                  head3(1), head3(M_DIM),
                  pl.BlockSpec((nb, M_DIM), lambda h: (0, h)), pl.BlockSpec((nb, M_DIM), lambda h: (0, h)),
                  c_spec, pl.BlockSpec((1, M_DIM, 1), lambda h: (h, 0, 0))],
        out_specs=(head3(M_DIM), head3(M_DIM), head3(1), c_spec),
        compiler_params=pltpu.CompilerParams(dimension_semantics=("arbitrary",),
                                             vmem_limit_bytes=VMEM_LIMIT_BYTES),
        name="sample_mlstm",
    )(b_i, b_f, qt, kt, vt, ot, gi, gf, m0, n0t, q_rows, k_rows, c0, gain_col)


SAMPLE_TILE = 8


def _sample_swa_kernel(q2_ref, kc_ref, vc_ref, kn_ref, vn_ref, sink_ref, o_ref, ko_ref, vo_ref):
    sink = sink_ref[...]
    w = kc_ref.shape[1]
    for b in range(SAMPLE_TILE):
        q2 = q2_ref[b]
        k_new = kn_ref[b:b + 1, :]
        v_new = vn_ref[b:b + 1, :]
        s_c = _dot_nt(q2, kc_ref[b].astype(BF16))
        s_n = jnp.sum(q2.astype(F32) * k_new, axis=-1, keepdims=True)
        mx = jnp.maximum(jnp.maximum(jnp.max(s_c, axis=-1, keepdims=True), s_n), sink)
        p_c = jnp.exp(s_c - mx)
        p_n = jnp.exp(s_n - mx)
        denom = jnp.sum(p_c, axis=-1, keepdims=True) + p_n + jnp.exp(sink - mx)
        o = _dot(p_c.astype(BF16), vc_ref[b].astype(BF16)) + p_n * v_new
        o_ref[b] = o / denom
        ko_ref[b, 0:w - 1, :] = kc_ref[b, 1:w, :]
        ko_ref[b, w - 1:w, :] = k_new
        vo_ref[b, 0:w - 1, :] = vc_ref[b, 1:w, :]
        vo_ref[b, w - 1:w, :] = v_new


def _sample_swa(q2, k_cache, v_cache, k_new, v_new, sink_col):
    nb, w, _ = k_cache.shape
    t3 = lambda a, c: pl.BlockSpec((SAMPLE_TILE, a, c), lambda i: (i, 0, 0))
    t2 = pl.BlockSpec((SAMPLE_TILE, KV_WIDTH), lambda i: (i, 0))
    out_shape = (
        jax.ShapeDtypeStruct((nb, A_HEADS, KV_WIDTH), F32),
        jax.ShapeDtypeStruct((nb, w, KV_WIDTH), F32),
        jax.ShapeDtypeStruct((nb, w, KV_WIDTH), F32),
    )
    return pl.pallas_call(
        _sample_swa_kernel,
        out_shape=out_shape,
        grid=(nb // SAMPLE_TILE,),
        in_specs=[t3(A_HEADS, KV_WIDTH), t3(w, KV_WIDTH), t3(w, KV_WIDTH), t2, t2,
                  pl.BlockSpec((A_HEADS, 1), lambda i: (0, 0))],
        out_specs=(t3(A_HEADS, KV_WIDTH), t3(w, KV_WIDTH), t3(w, KV_WIDTH)),
        compiler_params=pltpu.CompilerParams(dimension_semantics=("arbitrary",)),
        name="sample_swa",
    )(q2, k_cache, v_cache, k_new, v_new, sink_col)


def _blockdiag_ones(n, group):
    idx = jnp.arange(n) // group
    return (idx[:, None] == idx[None, :]).astype(BF16)


def _prep_weights(ffn1_w_gate, ffn1_w_up, ffn1_w_down, w_in, w_out, ffn2_w_gate, ffn2_w_up, ffn2_w_down):
    def cols(w):
        return jnp.transpose(w.astype(BF16).reshape(D_MODEL, N_FF_CHUNKS, FF_CHUNK), (1, 0, 2))

    def rows(w):
        return w.astype(BF16).reshape(N_FF_CHUNKS, FF_CHUNK, D_MODEL)

    m_end = 4 * M_WIDTH
    g_end = m_end + 2 * M_HEADS
    q_end = g_end + A_WIDTH
    wm = w_in[:, :m_end].astype(BF16)
    wqa = w_in[:, g_end:q_end].astype(BF16)
    gates = jnp.pad(w_in[:, m_end:g_end], ((0, 0), (0, GATE_PAD - 2 * M_HEADS)))
    wkvg = jnp.concatenate([w_in[:, q_end:], gates], axis=1).astype(BF16)
    return (cols(ffn1_w_gate), cols(ffn1_w_up), rows(ffn1_w_down), wm, wqa, wkvg, w_out.astype(BF16),
            cols(ffn2_w_gate), cols(ffn2_w_up), rows(ffn2_w_down))


def kernel(x_prompt, x_sample, cache_swa_k, cache_swa_v, state_mlstm_C, state_mlstm_n, state_mlstm_m,
           ffn1_norm, ffn1_w_gate, ffn1_w_up, ffn1_w_down, mix_norm, w_in, mlstm_b_i, mlstm_b_f,
           mlstm_out_norm, swa_q_norm, swa_k_norm, swa_sinks, w_out, ffn2_norm, ffn2_w_gate,
           ffn2_w_up, ffn2_w_down):
    depth = ffn1_norm.shape[0]
    assert depth == 1
    batch, seq, _ = x_prompt.shape
    nb = x_sample.shape[0]
    assert x_sample.shape[1] == 1 and seq % BLOCK == 0

    (wg1, wu1, wd1, wm, wqa, wkvg, wo, wg2, wu2, wd2) = _prep_weights(
        ffn1_w_gate[0], ffn1_w_up[0], ffn1_w_down[0], w_in[0], w_out[0],
        ffn2_w_gate[0], ffn2_w_up[0], ffn2_w_down[0])
    n1 = ffn1_norm[0].reshape(1, D_MODEL)
    n2 = mix_norm[0].reshape(1, D_MODEL)
    n3 = ffn2_norm[0].reshape(1, D_MODEL)
    qgain = jnp.tile(swa_q_norm[0], A_HEADS).reshape(1, A_WIDTH)
    kgain = jnp.tile(swa_k_norm[0], KV_HEADS).reshape(1, KV_WIDTH)
    e512 = _blockdiag_ones(A_WIDTH, A_DIM)
    e128 = _blockdiag_ones(KV_WIDTH, A_DIM)
    front = functools.partial(_front, n1=n1, wg=wg1, wu=wu1, wd=wd1, n2=n2, wm=wm, wqa=wqa, wkvg=wkvg,
                              qgain=qgain, kgain=kgain, e512=e512, e128=e128)
    back = functools.partial(_back, wo=wo, nrm=n3, wg=wg2, wu=wu2, wd=wd2)
    b_i, b_f = mlstm_b_i[0], mlstm_b_f[0]
    ogain = mlstm_out_norm[0].reshape(1, M_WIDTH)
    sinks = swa_sinks[0]

    xp = x_prompt.reshape(batch * seq, D_MODEL)
    x1, qkv, om, qa, ka, va, gt, vt, kx, vx = front(xp, tm=512)
    bias8 = jnp.concatenate([b_i, b_f]).reshape(2 * M_HEADS, 1)
    tri = (jnp.arange(BLOCK)[:, None] <= jnp.arange(BLOCK)[None, :]).astype(F32)
    rows, cols, pm = _gate_prep(gt, bias8, tri, batch, seq)
    y, pc, pn = _prompt_mixer(sinks, qkv, vt, om, cols, rows, qa, kx, vx, ogain, batch, seq)
    yp = back(x1, y, tm=512).reshape(batch, seq, D_MODEL)
    w = min(WINDOW, seq)
    pk = ka.reshape(batch, seq, KV_HEADS, A_DIM)[:, seq - w:]
    pv = va.reshape(batch, seq, KV_HEADS, A_DIM)[:, seq - w:]
    pn = pn[:, :, 0, :]
    pm = pm[:, 0:M_HEADS, 0]

    xs = x_sample.reshape(nb, D_MODEL)
    x1s, qkvs, oms, qas, kas, vas, gates_t = front(xs, tm=nb)[:7]
    qkv_f = qkvs.astype(F32)
    to_heads_t = lambda a: jnp.transpose(a.reshape(nb, M_HEADS, M_DIM), (1, 2, 0))
    q_rows = qkv_f[:, 0:M_WIDTH]
    k_rows = qkv_f[:, M_WIDTH:2 * M_WIDTH]
    ymt, nt, mt, sc = _sample_mlstm(
        b_i, b_f, to_heads_t(q_rows), to_heads_t(k_rows), to_heads_t(qkv_f[:, 2 * M_WIDTH:]), to_heads_t(oms),
        gates_t[0:M_HEADS].reshape(M_HEADS, 1, nb), gates_t[M_HEADS:].reshape(M_HEADS, 1, nb),
        jnp.transpose(state_mlstm_m[0]).reshape(M_HEADS, 1, nb), jnp.transpose(state_mlstm_n[0], (1, 2, 0)),
        q_rows, k_rows, state_mlstm_C[0], mlstm_out_norm[0].reshape(M_HEADS, M_DIM, 1))
    y_m = jnp.transpose(ymt, (2, 0, 1)).reshape(nb, M_WIDTH)
    sn = jnp.transpose(nt, (2, 0, 1))
    sm = jnp.transpose(mt[:, 0, :])

    qa_h = qas.reshape(nb, A_HEADS, A_DIM)
    zeros = jnp.zeros_like(qa_h)
    in_lo = (jnp.arange(A_HEADS) // A_GROUP == 0)[None, :, None]
    q2 = jnp.concatenate([jnp.where(in_lo, qa_h, zeros), jnp.where(in_lo, zeros, qa_h)], axis=-1)
    kc = cache_swa_k[0].reshape(nb, -1, KV_WIDTH)
    vc = cache_swa_v[0].reshape(nb, -1, KV_WIDTH)
    o2, sk, sv = _sample_swa(q2, kc, vc, kas, vas, sinks.reshape(A_HEADS, 1))
    o2 = o2.reshape(nb, A_HEADS, KV_HEADS, A_DIM)
    y_a = jnp.where(in_lo, o2[:, :, 0, :], o2[:, :, 1, :]).reshape(nb, A_WIDTH)
    ys_in = jnp.concatenate([y_m, y_a], axis=-1).astype(BF16)
    ys = back(x1s, ys_in, tm=nb).reshape(nb, 1, D_MODEL)

    wb = kc.shape[1]
    return (yp, ys, pk[None], pv[None], pc[None], pn[None], pm[None],
            sk.reshape(1, nb, wb, KV_HEADS, A_DIM), sv.reshape(1, nb, wb, KV_HEADS, A_DIM),
            sc[None], sn[None], sm[None])
```

```python
import functools

import jax
import jax.numpy as jnp
from jax import lax
from jax.experimental import pallas as pl
from jax.experimental.pallas import tpu as pltpu

F32 = jnp.float32
BF16 = jnp.bfloat16

D_MODEL = 1024
D_FF = 2816
FF_CHUNK = 256
N_FF_CHUNKS = D_FF // FF_CHUNK
M_HEADS = 4
M_DIM = 128
M_WIDTH = M_HEADS * M_DIM
A_HEADS = 8
A_DIM = 64
A_WIDTH = A_HEADS * A_DIM
KV_HEADS = 2
KV_WIDTH = KV_HEADS * A_DIM
A_GROUP = A_HEADS // KV_HEADS
WINDOW = 128
BLOCK = 128
GATE_PAD = 128
RMS_EPS = 1e-6
FFN_RES_WEIGHT = 0.5
NEG_INF = float("-inf")
VMEM_LIMIT_BYTES = 56 * 1024 * 1024


def _dot(a, b):
    return jnp.dot(a, b, preferred_element_type=F32)


def _dot_nt(a, b):
    return lax.dot_general(a, b, (((1,), (1,)), ((), ())), preferred_element_type=F32)


def _rms_rows(x, gain):
    ms = jnp.mean(x * x, axis=-1, keepdims=True)
    return x * lax.rsqrt(ms + RMS_EPS) * gain


def _log_sigmoid(x):
    return jnp.minimum(x, 0.0) - jnp.log1p(jnp.exp(-jnp.abs(x)))


def _group_mean_sq(x, ones_blockdiag, group):
    sq = x * x
    hi = sq.astype(BF16)
    lo = (sq - hi.astype(F32)).astype(BF16)
    return (_dot(hi, ones_blockdiag) + _dot(lo, ones_blockdiag)) * (1.0 / group)


def _ffn_into(h_ref, wg_ref, wu_ref, wd_ref, acc_ref):
    for c in range(N_FF_CHUNKS):
        lo, hi = c * FF_CHUNK, (c + 1) * FF_CHUNK
        h = h_ref[...]
        g = _dot(h, wg_ref[:, lo:hi])
        u = _dot(h, wu_ref[:, lo:hi])
        a = (g * jax.nn.sigmoid(g) * u).astype(BF16)
        d = _dot(a, wd_ref[lo:hi, :])
        if c == 0:
            acc_ref[...] = d
        else:
            acc_ref[...] += d


def _front_kernel(x_ref, n1_ref, wg_ref, wu_ref, wd_ref, n2_ref, wm_ref, wqa_ref, wkvg_ref,
                  qgain_ref, kgain_ref, e512_ref, e128_ref,
                  x1_ref, qkv_ref, om_ref, qa_ref, ka_ref, va_ref, gt_ref, vt_ref, kx_ref, vx_ref,
                  h_ref, acc_ref):
    x = x_ref[...]
    h_ref[...] = _rms_rows(x, n1_ref[...]).astype(BF16)
    _ffn_into(h_ref, wg_ref, wu_ref, wd_ref, acc_ref)
    x1 = x + FFN_RES_WEIGHT * acc_ref[...]
    x1_ref[...] = x1
    h_ref[...] = _rms_rows(x1, n2_ref[...]).astype(BF16)
    h = h_ref[...]

    qkv_ref[:, 0:M_WIDTH] = _dot(h, wm_ref[:, 0:M_WIDTH]).astype(BF16)
    k_m = _dot(h, wm_ref[:, M_WIDTH:2 * M_WIDTH]) * (M_DIM ** -0.5)
    qkv_ref[:, M_WIDTH:2 * M_WIDTH] = k_m.astype(BF16)
    v_m = _dot(h, wm_ref[:, 2 * M_WIDTH:3 * M_WIDTH])
    qkv_ref[:, 2 * M_WIDTH:3 * M_WIDTH] = v_m.astype(BF16)
    for c in range(vt_ref.shape[0]):
        for hd in range(M_HEADS):
            blk = v_m[c * BLOCK:(c + 1) * BLOCK, hd * M_DIM:(hd + 1) * M_DIM]
            vt_ref[c, hd] = blk.T.astype(BF16)
    om_ref[...] = _dot(h, wm_ref[:, 3 * M_WIDTH:4 * M_WIDTH])

    qa = _dot(h, wqa_ref[...])
    q_ms = _group_mean_sq(qa, e512_ref[...], A_DIM)
    qa_ref[...] = (qa * lax.rsqrt(q_ms + RMS_EPS) * qgain_ref[...] * (A_DIM ** -0.5)).astype(BF16)

    kvg = _dot(h, wkvg_ref[...])
    ka = kvg[:, 0:KV_WIDTH]
    k_ms = _group_mean_sq(ka, e128_ref[...], A_DIM)
    ka = ka * lax.rsqrt(k_ms + RMS_EPS) * kgain_ref[...]
    va = kvg[:, KV_WIDTH:2 * KV_WIDTH]
    ka_ref[...] = ka
    va_ref[...] = va
    gt_ref[...] = kvg[:, 2 * KV_WIDTH:2 * KV_WIDTH + GATE_PAD].T[0:2 * M_HEADS, :]

    in_lo = lax.broadcasted_iota(jnp.int32, ka.shape, 1) < A_DIM
    for src, dst in ((ka, kx_ref), (va, vx_ref)):
        x0 = jnp.where(in_lo, src, 0.0)
        x1 = jnp.where(in_lo, 0.0, src)
        dst[:, 0:128] = x0.astype(BF16)
        dst[:, 128:256] = pltpu.roll(x0, A_DIM, 1).astype(BF16)
        dst[:, 256:384] = pltpu.roll(x1, A_DIM, 1).astype(BF16)
        dst[:, 384:512] = x1.astype(BF16)


def _const_spec(shape):
    nd = len(shape)
    return pl.BlockSpec(shape, lambda i: (0,) * nd, pipeline_mode=pl.Buffered(1))


def _front(x2d, n1, wg, wu, wd, n2, wm, wqa, wkvg, qgain, kgain, e512, e128, tm):
    n = x2d.shape[0]
    row = lambda w: pl.BlockSpec((tm, w), lambda i: (i, 0))
    out_shape = (
        jax.ShapeDtypeStruct((n, D_MODEL), F32),
        jax.ShapeDtypeStruct((n, 3 * M_WIDTH), BF16),
        jax.ShapeDtypeStruct((n, M_WIDTH), F32),
        jax.ShapeDtypeStruct((n, A_WIDTH), BF16),
        jax.ShapeDtypeStruct((n, KV_WIDTH), F32),
        jax.ShapeDtypeStruct((n, KV_WIDTH), F32),
        jax.ShapeDtypeStruct((2 * M_HEADS, n), F32),
        jax.ShapeDtypeStruct((n // BLOCK, M_HEADS, M_DIM, BLOCK), BF16),
        jax.ShapeDtypeStruct((n, 4 * KV_WIDTH), BF16),
        jax.ShapeDtypeStruct((n, 4 * KV_WIDTH), BF16),
    )
    nb_t = tm // BLOCK
    return pl.pallas_call(
        _front_kernel,
        out_shape=out_shape,
        grid=(n // tm,),
        in_specs=[row(D_MODEL), _const_spec(n1.shape), _const_spec(wg.shape), _const_spec(wu.shape),
                  _const_spec(wd.shape), _const_spec(n2.shape), _const_spec(wm.shape),
                  _const_spec(wqa.shape), _const_spec(wkvg.shape), _const_spec(qgain.shape),
                  _const_spec(kgain.shape), _const_spec(e512.shape), _const_spec(e128.shape)],
        out_specs=(row(D_MODEL), row(3 * M_WIDTH), row(M_WIDTH), row(A_WIDTH), row(KV_WIDTH),
                   row(KV_WIDTH), pl.BlockSpec((2 * M_HEADS, tm), lambda i: (0, i)),
                   pl.BlockSpec((nb_t, M_HEADS, M_DIM, BLOCK), lambda i: (i, 0, 0, 0)),
                   row(4 * KV_WIDTH), row(4 * KV_WIDTH)),
        scratch_shapes=[pltpu.VMEM((tm, D_MODEL), BF16), pltpu.VMEM((tm, D_MODEL), F32)],
        compiler_params=pltpu.CompilerParams(dimension_semantics=("arbitrary",),
                                             vmem_limit_bytes=VMEM_LIMIT_BYTES),
        name="front",
    )(x2d, n1, wg, wu, wd, n2, wm, wqa, wkvg, qgain, kgain, e512, e128)


def _back_kernel(x1_ref, y_ref, wo_ref, n_ref, wg_ref, wu_ref, wd_ref, out_ref, h_ref, acc_ref):
    x2 = x1_ref[...] + _dot(y_ref[...], wo_ref[...])
    h_ref[...] = _rms_rows(x2, n_ref[...]).astype(BF16)
    _ffn_into(h_ref, wg_ref, wu_ref, wd_ref, acc_ref)
    out_ref[...] = x2 + FFN_RES_WEIGHT * acc_ref[...]


def _back(x1, y, wo, nrm, wg, wu, wd, tm):
    n = x1.shape[0]
    row = lambda w: pl.BlockSpec((tm, w), lambda i: (i, 0))
    return pl.pallas_call(
        _back_kernel,
        out_shape=jax.ShapeDtypeStruct((n, D_MODEL), F32),
        grid=(n // tm,),
        in_specs=[row(D_MODEL), row(D_MODEL), _const_spec(wo.shape), _const_spec(nrm.shape),
                  _const_spec(wg.shape), _const_spec(wu.shape), _const_spec(wd.shape)],
        out_specs=row(D_MODEL),
        scratch_shapes=[pltpu.VMEM((tm, D_MODEL), BF16), pltpu.VMEM((tm, D_MODEL), F32)],
        compiler_params=pltpu.CompilerParams(dimension_semantics=("arbitrary",),
                                             vmem_limit_bytes=VMEM_LIMIT_BYTES),
        name="back",
    )(x1, y, wo, nrm, wg, wu, wd)


def _gate_prep_kernel(gt_ref, bias_ref, tri_ref, rows_ref, cols_ref, m_ref):
    nblk = rows_ref.shape[1]
    sub = lax.broadcasted_iota(jnp.int32, (8, BLOCK), 0)
    lane = lax.broadcasted_iota(jnp.int32, (8, BLOCK), 1)
    is_head = sub < M_HEADS
    pad = jnp.zeros((BLOCK - 24, BLOCK), F32)
    m_prev = jnp.zeros((8, 1), F32)
    for j in range(nblk):
        pre = gt_ref[:, j * BLOCK:(j + 1) * BLOCK] + bias_ref[...]
        r = jnp.where(is_head, pre, _log_sigmoid(pre))
        cum = jnp.dot(r, tri_ref[...], preferred_element_type=F32, precision=lax.Precision.HIGHEST)
        bcum = pltpu.roll(cum, M_HEADS, 0)
        g = jnp.where(is_head, r - bcum, 0.0)
        bcum = jnp.where(is_head, bcum, 0.0)
        cm = g
        for shift in (1, 2, 4, 8, 16, 32, 64):
            cm = jnp.maximum(cm, jnp.where(lane >= shift, pltpu.roll(cm, shift, 1), NEG_INF))
        cm_last = jnp.max(cm, axis=-1, keepdims=True)
        b_last = jnp.sum(jnp.where(lane == BLOCK - 1, bcum, 0.0), axis=-1, keepdims=True)
        mx = jnp.maximum(m_prev, cm)
        mx_last = jnp.maximum(m_prev, cm_last)
        rows_ref[0, j, 0] = g
        rows_ref[0, j, 1] = jnp.exp(g - mx_last)
        rows_ref[0, j, 2] = jnp.broadcast_to(jnp.exp(m_prev - mx_last), (8, BLOCK))
        col_src = jnp.concatenate([-mx, jnp.exp(m_prev - mx), jnp.exp(-(bcum + mx)), pad], axis=0)
        cols_ref[j * BLOCK:(j + 1) * BLOCK, :] = col_src.T
        m_prev = b_last + mx_last
    m_ref[0] = jnp.broadcast_to(m_prev, (8, BLOCK))


def _gate_prep(gt, bias8, tri, batch, seq):
    nblk = seq // BLOCK
    out_shape = (
        jax.ShapeDtypeStruct((batch, nblk, 3, 8, BLOCK), F32),
        jax.ShapeDtypeStruct((batch * seq, 128), F32),
        jax.ShapeDtypeStruct((batch, 8, BLOCK), F32),
    )
    return pl.pallas_call(
        _gate_prep_kernel,
        out_shape=out_shape,
        grid=(batch,),
        in_specs=[pl.BlockSpec((2 * M_HEADS, seq), lambda b: (0, b)),
                  pl.BlockSpec(bias8.shape, lambda b: (0, 0)), pl.BlockSpec(tri.shape, lambda b: (0, 0))],
        out_specs=(pl.BlockSpec((1, nblk, 3, 8, BLOCK), lambda b: (b, 0, 0, 0, 0)),
                   pl.BlockSpec((seq, 128), lambda b: (b, 0)),
                   pl.BlockSpec((1, 8, BLOCK), lambda b: (b, 0, 0))),
        compiler_params=pltpu.CompilerParams(dimension_semantics=("arbitrary",)),
        name="gate_prep",
    )(gt, bias8, tri)


def _prompt_mixer_kernel(sinks_ref, qkv_ref, vt_ref, om_ref, cols_ref, rows_ref, qa_ref,
                         kx_ref, vx_ref, kxp_ref, vxp_ref, ogain_ref,
                         y_ref, c_out_ref, n_out_ref,
                         c_ref, n_ref):
    j = pl.program_id(0)
    batch = qkv_ref.shape[0]

    @pl.when(j == 0)
    def _():
        c_ref[...] = jnp.zeros_like(c_ref)
        n_ref[...] = jnp.zeros_like(n_ref)

    rows = lax.broadcasted_iota(jnp.int32, (BLOCK, BLOCK), 0)
    cols = lax.broadcasted_iota(jnp.int32, (BLOCK, BLOCK), 1)
    causal = cols <= rows
    lane_lo = lax.broadcasted_iota(jnp.int32, (2 * BLOCK, KV_WIDTH), 1) < A_DIM
    ones_m = jnp.ones((BLOCK, M_DIM), BF16)
    ones_half = (jnp.where(lane_lo, 1.0, 0.0).astype(BF16), jnp.where(lane_lo, 0.0, 1.0).astype(BF16))
    qi = lax.broadcasted_iota(jnp.int32, (2 * BLOCK, 2 * BLOCK), 0) % BLOCK
    kc = lax.broadcasted_iota(jnp.int32, (2 * BLOCK, 2 * BLOCK), 1)
    first_valid = jnp.where(j == 0, BLOCK, 0)
    valid = (kc >= qi) & (kc <= qi + WINDOW) & (kc >= first_valid)
    top_rows = lax.broadcasted_iota(jnp.int32, (2 * BLOCK, 1), 0) < BLOCK

    for b in range(batch):
        col_b = cols_ref[b]
        for h in range(M_HEADS):
            lo, hi = h * M_DIM, (h + 1) * M_DIM
            q = qkv_ref[b, :, lo:hi]
            k = qkv_ref[b, :, M_WIDTH + lo:M_WIDTH + hi]
            v = qkv_ref[b, :, 2 * M_WIDTH + lo:2 * M_WIDTH + hi]
            a_col = col_b[:, h:h + 1]
            wi_col = col_b[:, 8 + h:9 + h]
            einv_col = col_b[:, 16 + h:17 + h]
            g_row = rows_ref[b, 0, 0][h:h + 1, :]
            wend_row = rows_ref[b, 0, 1][h:h + 1, :]
            decay = rows_ref[b, 0, 2][h:h + 1, :]
            c_prev = c_ref[b, h]
            n_prev = n_ref[b, h]

            n_rep = jnp.broadcast_to(n_prev[0:1, :], (BLOCK, M_DIM)).astype(BF16)
            qkc = _dot_nt(q, jnp.concatenate([k, c_prev.astype(BF16), n_rep], axis=0))
            d = jnp.where(causal, jnp.exp(a_col + g_row), 0.0)
            s = (qkc[:, 0:BLOCK] * d).astype(BF16)
            sv = _dot(s, jnp.concatenate([v, ones_m], axis=1))
            num = wi_col * qkc[:, BLOCK:2 * BLOCK] + sv[:, 0:M_DIM]
            den = wi_col * qkc[:, 2 * BLOCK:3 * BLOCK] + sv[:, M_DIM:2 * M_DIM]
            hh = num / jnp.maximum(jnp.abs(den), einv_col)
            hh = hh * lax.rsqrt(jnp.mean(hh * hh, axis=-1, keepdims=True) + RMS_EPS)
            hh = hh * ogain_ref[:, lo:hi] * jax.nn.sigmoid(om_ref[b, :, lo:hi])
            y_ref[b, :, lo:hi] = hh.astype(BF16)

            vw_t = (vt_ref[b, 0, h].astype(F32) * wend_row).astype(BF16)
            w_rep = jnp.broadcast_to(wend_row, (16, BLOCK)).astype(BF16)
            upd = _dot(jnp.concatenate([vw_t, w_rep], axis=0), k)
            c_ref[b, h] = decay * c_prev + upd[0:M_DIM]
            n_ref[b, h] = decay * n_prev + upd[M_DIM:M_DIM + 8]

        for kv in range(KV_HEADS):
            p0, p1 = 2 * kv, 2 * kv + 1
            q2 = jnp.concatenate([qa_ref[b, :, p0 * 128:(p0 + 1) * 128],
                                  qa_ref[b, :, p1 * 128:(p1 + 1) * 128]], axis=0)
            acc = None
            e_sink = []
            for parity in range(2):
                var = (2 * kv + parity) * KV_WIDTH
                k_band = jnp.concatenate([kxp_ref[b, :, var:var + KV_WIDTH], kx_ref[b, :, var:var + KV_WIDTH]],
                                         axis=0)
                v_band = jnp.concatenate([vxp_ref[b, :, var:var + KV_WIDTH], vx_ref[b, :, var:var + KV_WIDTH]],
                                         axis=0)
                sink = jnp.where(top_rows, sinks_ref[2 * p0 + parity], sinks_ref[2 * p1 + parity])
                sc = jnp.where(valid, _dot_nt(q2, k_band), NEG_INF)
                mx = jnp.maximum(jnp.max(sc, axis=-1, keepdims=True), sink)
                pr = jnp.exp(sc - mx).astype(BF16)
                e_sink.append(jnp.exp(sink - mx))
                o = _dot(pr, jnp.concatenate([v_band, ones_half[parity]], axis=1))
                acc = o if acc is None else acc + o
            denom = acc[:, KV_WIDTH:] + jnp.where(lane_lo, e_sink[0], e_sink[1])
            ya = (acc[:, 0:KV_WIDTH] / denom).astype(BF16)
            y_ref[b, :, M_WIDTH + p0 * 128:M_WIDTH + (p0 + 1) * 128] = ya[0:BLOCK]
            y_ref[b, :, M_WIDTH + p1 * 128:M_WIDTH + (p1 + 1) * 128] = ya[BLOCK:2 * BLOCK]

    @pl.when(j == pl.num_programs(0) - 1)
    def _():
        c_out_ref[...] = c_ref[...]
        n_out_ref[...] = n_ref[...]


def _prompt_mixer(sinks, qkv, vt, om, cols, rows, qa, kx, vx, ogain, batch, seq):
    nblk = seq // BLOCK
    r3 = lambda a: a.reshape(batch, seq, a.shape[-1])
    cur = lambda w: pl.BlockSpec((batch, BLOCK, w), lambda j: (0, j, 0))
    prev = lambda w: pl.BlockSpec((batch, BLOCK, w), lambda j: (0, jnp.maximum(j - 1, 0), 0))
    state = lambda shape: pl.BlockSpec((batch,) + shape, lambda j: (0,) * (len(shape) + 1))
    out_shape = (
        jax.ShapeDtypeStruct((batch, seq, D_MODEL), BF16),
        jax.ShapeDtypeStruct((batch, M_HEADS, M_DIM, M_DIM), F32),
        jax.ShapeDtypeStruct((batch, M_HEADS, 8, M_DIM), F32),
    )
    y, pc, pn = pl.pallas_call(
        _prompt_mixer_kernel,
        out_shape=out_shape,
        grid=(nblk,),
        in_specs=[pl.BlockSpec(memory_space=pltpu.SMEM),
                  cur(3 * M_WIDTH),
                  pl.BlockSpec((batch, 1, M_HEADS, M_DIM, BLOCK), lambda j: (0, j, 0, 0, 0)),
                  cur(M_WIDTH), cur(128),
                  pl.BlockSpec((batch, 1, 3, 8, BLOCK), lambda j: (0, j, 0, 0, 0)),
                  cur(A_WIDTH), cur(4 * KV_WIDTH), cur(4 * KV_WIDTH), prev(4 * KV_WIDTH), prev(4 * KV_WIDTH),
                  pl.BlockSpec(ogain.shape, lambda j: (0, 0))],
        out_specs=(cur(D_MODEL), state((M_HEADS, M_DIM, M_DIM)), state((M_HEADS, 8, M_DIM))),
        scratch_shapes=[pltpu.VMEM((batch, M_HEADS, M_DIM, M_DIM), F32),
                        pltpu.VMEM((batch, M_HEADS, 8, M_DIM), F32)],
        compiler_params=pltpu.CompilerParams(dimension_semantics=("arbitrary",),
                                             vmem_limit_bytes=VMEM_LIMIT_BYTES),
        name="prompt_mixer",
    )(sinks, r3(qkv), vt.reshape(batch, nblk, M_HEADS, M_DIM, BLOCK), r3(om), r3(cols), rows, r3(qa),
      r3(kx), r3(vx), r3(kx), r3(vx), ogain)
    return y.reshape(batch * seq, D_MODEL), pc, pn


def _sample_mlstm_kernel(bi_ref, bf_ref, qt_ref, kt_ref, vt_ref, ot_ref, gi_ref, gf_ref, m0_ref, n0t_ref,
                         qr_ref, kr_ref, c_ref, gain_ref,
                         yt_ref, nt_out_ref, m_out_ref, c_out_ref):
    h = pl.program_id(0)
    nb = qr_ref.shape[0]
    qt, kt, vt = qt_ref[0], kt_ref[0], vt_ref[0]
    i_pre = gi_ref[0] + bi_ref[h]
    a = _log_sigmoid(gf_ref[0] + bf_ref[h]) + m0_ref[0]
    m_t = jnp.maximum(a, i_pre)
    w_inter = jnp.exp(a - m_t)
    w_in = jnp.exp(i_pre - m_t)
    scores = jnp.sum(qt * kt, axis=0, keepdims=True) * w_in
    n0t = n0t_ref[0]
    nq = jnp.sum(n0t * qt, axis=0, keepdims=True)

    rows = lax.broadcasted_iota(jnp.int32, (nb, M_DIM), 0)
    cols = lax.broadcasted_iota(jnp.int32, (M_DIM, nb), 1)
    lane_row = lax.broadcasted_iota(jnp.int32, (1, nb), 1)
    vw_t = (vt * w_in).astype(BF16)
    k_rows = kr_ref[...]

    def body(b, cq_t):
        c_b = c_ref[b, 0]
        q_row = qr_ref[pl.ds(b, 1), :]
        col = jnp.sum(c_b * q_row, axis=-1, keepdims=True)
        cq_t = jnp.where(cols == b, col, cq_t)
        k_only_b = jnp.where(rows == b, k_rows, 0.0).astype(BF16)
        outer = _dot(vw_t, k_only_b)
        decay_b = jnp.sum(jnp.where(lane_row == b, w_inter, 0.0), axis=-1, keepdims=True)
        c_out_ref[b, 0] = decay_b * c_b + outer
        return cq_t

    cq_t = lax.fori_loop(0, nb, body, jnp.zeros((M_DIM, nb), F32))

    num = w_inter * cq_t + scores * vt
    den = w_inter * nq + scores
    hh = num / jnp.maximum(jnp.abs(den), jnp.exp(-m_t))
    hh = hh * lax.rsqrt(jnp.mean(hh * hh, axis=0, keepdims=True) + RMS_EPS)
    yt_ref[0] = hh * gain_ref[0] * jax.nn.sigmoid(ot_ref[0])
    nt_out_ref[0] = w_inter * n0t + w_in * kt
    m_out_ref[0] = m_t


def _sample_mlstm(b_i, b_f, qt, kt, vt, ot, gi, gf, m0, n0t, q_rows, k_rows, c0, gain_col):
    nb = q_rows.shape[0]
    head3 = lambda r: pl.BlockSpec((1, r, nb), lambda h: (h, 0, 0))
    smem = pl.BlockSpec(memory_space=pltpu.SMEM)
    out_shape = (
        jax.ShapeDtypeStruct((M_HEADS, M_DIM, nb), F32),
        jax.ShapeDtypeStruct((M_HEADS, M_DIM, nb), F32),
        jax.ShapeDtypeStruct((M_HEADS, 1, nb), F32),
        jax.ShapeDtypeStruct((nb, M_HEADS, M_DIM, M_DIM), F32),
    )
    c_spec = pl.BlockSpec((nb, 1, M_DIM, M_DIM), lambda h: (0, h, 0, 0))
    return pl.pallas_call(
        _sample_mlstm_kernel,
        out_shape=out_shape,
        grid=(M_HEADS,),
        in_specs=[smem, smem, head3(M_DIM), head3(M_DIM), head3(M_DIM), head3(M_DIM), head3(1), head3(1),
                  head3(1), head3(M_DIM),
                  pl.BlockSpec((nb, M_DIM), lambda h: (0, h)), pl.BlockSpec((nb, M_DIM), lambda h: (0, h)),
                  c_spec, pl.BlockSpec((1, M_DIM, 1), lambda h: (h, 0, 0))],
        out_specs=(head3(M_DIM), head3(M_DIM), head3(1), c_spec),
        compiler_params=pltpu.CompilerParams(dimension_semantics=("arbitrary",),
                                             vmem_limit_bytes=VMEM_LIMIT_BYTES),
        name="sample_mlstm",
    )(b_i, b_f, qt, kt, vt, ot, gi, gf, m0, n0t, q_rows, k_rows, c0, gain_col)


SAMPLE_TILE = 8


def _sample_swa_kernel(q2_ref, kc_ref, vc_ref, kn_ref, vn_ref, sink_ref, o_ref, ko_ref, vo_ref):
    sink = sink_ref[...]
    w = kc_ref.shape[1]
    for b in range(SAMPLE_TILE):
        q2 = q2_ref[b]
        k_new = kn_ref[b:b + 1, :]
        v_new = vn_ref[b:b + 1, :]
        s_c = _dot_nt(q2, kc_ref[b].astype(BF16))
        s_n = jnp.sum(q2.astype(F32) * k_new, axis=-1, keepdims=True)
        mx = jnp.maximum(jnp.maximum(jnp.max(s_c, axis=-1, keepdims=True), s_n), sink)
        p_c = jnp.exp(s_c - mx)
        p_n = jnp.exp(s_n - mx)
        denom = jnp.sum(p_c, axis=-1, keepdims=True) + p_n + jnp.exp(sink - mx)
        o = _dot(p_c.astype(BF16), vc_ref[b].astype(BF16)) + p_n * v_new
        o_ref[b] = o / denom
        ko_ref[b, 0:w - 1, :] = kc_ref[b, 1:w, :]
        ko_ref[b, w - 1:w, :] = k_new
        vo_ref[b, 0:w - 1, :] = vc_ref[b, 1:w, :]
        vo_ref[b, w - 1:w, :] = v_new


def _sample_swa(q2, k_cache, v_cache, k_new, v_new, sink_col):
    nb, w, _ = k_cache.shape
    t3 = lambda a, c: pl.BlockSpec((SAMPLE_TILE, a, c), lambda i: (i, 0, 0))
    t2 = pl.BlockSpec((SAMPLE_TILE, KV_WIDTH), lambda i: (i, 0))
    out_shape = (
        jax.ShapeDtypeStruct((nb, A_HEADS, KV_WIDTH), F32),
        jax.ShapeDtypeStruct((nb, w, KV_WIDTH), F32),
        jax.ShapeDtypeStruct((nb, w, KV_WIDTH), F32),
    )
    return pl.pallas_call(
        _sample_swa_kernel,
        out_shape=out_shape,
        grid=(nb // SAMPLE_TILE,),
        in_specs=[t3(A_HEADS, KV_WIDTH), t3(w, KV_WIDTH), t3(w, KV_WIDTH), t2, t2,
                  pl.BlockSpec((A_HEADS, 1), lambda i: (0, 0))],
        out_specs=(t3(A_HEADS, KV_WIDTH), t3(w, KV_WIDTH), t3(w, KV_WIDTH)),
        compiler_params=pltpu.CompilerParams(dimension_semantics=("arbitrary",)),
        name="sample_swa",
    )(q2, k_cache, v_cache, k_new, v_new, sink_col)


def _blockdiag_ones(n, group):
    idx = jnp.arange(n) // group
    return (idx[:, None] == idx[None, :]).astype(BF16)


def _prep_weights(ffn1_w_gate, ffn1_w_up, ffn1_w_down, w_in, w_out, ffn2_w_gate, ffn2_w_up, ffn2_w_down):
    cols = rows = lambda w: w.astype(BF16)

    m_end = 4 * M_WIDTH
    g_end = m_end + 2 * M_HEADS
    q_end = g_end + A_WIDTH
    wm = w_in[:, :m_end].astype(BF16)
    wqa = w_in[:, g_end:q_end].astype(BF16)
    gates = jnp.pad(w_in[:, m_end:g_end], ((0, 0), (0, GATE_PAD - 2 * M_HEADS)))
    wkvg = jnp.concatenate([w_in[:, q_end:], gates], axis=1).astype(BF16)
    return (cols(ffn1_w_gate), cols(ffn1_w_up), rows(ffn1_w_down), wm, wqa, wkvg, w_out.astype(BF16),
            cols(ffn2_w_gate), cols(ffn2_w_up), rows(ffn2_w_down))


def kernel(x_prompt, x_sample, cache_swa_k, cache_swa_v, state_mlstm_C, state_mlstm_n, state_mlstm_m,
           ffn1_norm, ffn1_w_gate, ffn1_w_up, ffn1_w_down, mix_norm, w_in, mlstm_b_i, mlstm_b_f,
           mlstm_out_norm, swa_q_norm, swa_k_norm, swa_sinks, w_out, ffn2_norm, ffn2_w_gate,
           ffn2_w_up, ffn2_w_down):
    depth = ffn1_norm.shape[0]
    assert depth == 1
    batch, seq, _ = x_prompt.shape
    nb = x_sample.shape[0]
    assert x_sample.shape[1] == 1 and seq % BLOCK == 0

    (wg1, wu1, wd1, wm, wqa, wkvg, wo, wg2, wu2, wd2) = _prep_weights(
        ffn1_w_gate[0], ffn1_w_up[0], ffn1_w_down[0], w_in[0], w_out[0],
        ffn2_w_gate[0], ffn2_w_up[0], ffn2_w_down[0])
    n1 = ffn1_norm[0].reshape(1, D_MODEL)
    n2 = mix_norm[0].reshape(1, D_MODEL)
    n3 = ffn2_norm[0].reshape(1, D_MODEL)
    qgain = jnp.tile(swa_q_norm[0], A_HEADS).reshape(1, A_WIDTH)
    kgain = jnp.tile(swa_k_norm[0], KV_HEADS).reshape(1, KV_WIDTH)
    e512 = _blockdiag_ones(A_WIDTH, A_DIM)
    e128 = _blockdiag_ones(KV_WIDTH, A_DIM)
    front = functools.partial(_front, n1=n1, wg=wg1, wu=wu1, wd=wd1, n2=n2, wm=wm, wqa=wqa, wkvg=wkvg,
                              qgain=qgain, kgain=kgain, e512=e512, e128=e128)
    back = functools.partial(_back, wo=wo, nrm=n3, wg=wg2, wu=wu2, wd=wd2)
    b_i, b_f = mlstm_b_i[0], mlstm_b_f[0]
    ogain = mlstm_out_norm[0].reshape(1, M_WIDTH)
    sinks = swa_sinks[0]

    xp = x_prompt.reshape(batch * seq, D_MODEL)
    x1, qkv, om, qa, ka, va, gt, vt, kx, vx = front(xp, tm=512)
    bias8 = jnp.concatenate([b_i, b_f]).reshape(2 * M_HEADS, 1)
    tri = (jnp.arange(BLOCK)[:, None] <= jnp.arange(BLOCK)[None, :]).astype(F32)
    rows, cols, pm = _gate_prep(gt, bias8, tri, batch, seq)
    y, pc, pn = _prompt_mixer(sinks, qkv, vt, om, cols, rows, qa, kx, vx, ogain, batch, seq)
    yp = back(x1, y, tm=512).reshape(batch, seq, D_MODEL)
    w = min(WINDOW, seq)
    pk = ka.reshape(batch, seq, KV_HEADS, A_DIM)[:, seq - w:]
    pv = va.reshape(batch, seq, KV_HEADS, A_DIM)[:, seq - w:]
    pn = pn[:, :, 0, :]
    pm = pm[:, 0:M_HEADS, 0]

    xs = x_sample.reshape(nb, D_MODEL)
    x1s, qkvs, oms, qas, kas, vas, gates_t = front(xs, tm=nb)[:7]
    qkv_f = qkvs.astype(F32)
    to_heads_t = lambda a: jnp.transpose(a.reshape(nb, M_HEADS, M_DIM), (1, 2, 0))
    q_rows = qkv_f[:, 0:M_WIDTH]
    k_rows = qkv_f[:, M_WIDTH:2 * M_WIDTH]
    ymt, nt, mt, sc = _sample_mlstm(
        b_i, b_f, to_heads_t(q_rows), to_heads_t(k_rows), to_heads_t(qkv_f[:, 2 * M_WIDTH:]), to_heads_t(oms),
        gates_t[0:M_HEADS].reshape(M_HEADS, 1, nb), gates_t[M_HEADS:].reshape(M_HEADS, 1, nb),
        jnp.transpose(state_mlstm_m[0]).reshape(M_HEADS, 1, nb), jnp.transpose(state_mlstm_n[0], (1, 2, 0)),
        q_rows, k_rows, state_mlstm_C[0], mlstm_out_norm[0].reshape(M_HEADS, M_DIM, 1))
    y_m = jnp.transpose(ymt, (2, 0, 1)).reshape(nb, M_WIDTH)
    sn = jnp.transpose(nt, (2, 0, 1))
    sm = jnp.transpose(mt[:, 0, :])

    qa_h = qas.reshape(nb, A_HEADS, A_DIM)
    zeros = jnp.zeros_like(qa_h)
    in_lo = (jnp.arange(A_HEADS) // A_GROUP == 0)[None, :, None]
    q2 = jnp.concatenate([jnp.where(in_lo, qa_h, zeros), jnp.where(in_lo, zeros, qa_h)], axis=-1)
    kc = cache_swa_k[0].reshape(nb, -1, KV_WIDTH)
    vc = cache_swa_v[0].reshape(nb, -1, KV_WIDTH)
    o2, sk, sv = _sample_swa(q2, kc, vc, kas, vas, sinks.reshape(A_HEADS, 1))
    o2 = o2.reshape(nb, A_HEADS, KV_HEADS, A_DIM)
    y_a = jnp.where(in_lo, o2[:, :, 0, :], o2[:, :, 1, :]).reshape(nb, A_WIDTH)
    ys_in = jnp.concatenate([y_m, y_a], axis=-1).astype(BF16)
    ys = back(x1s, ys_in, tm=nb).reshape(nb, 1, D_MODEL)

    wb = kc.shape[1]
    return (yp, ys, pk[None], pv[None], pc[None], pn[None], pm[None],
            sk.reshape(1, nb, wb, KV_HEADS, A_DIM), sv.reshape(1, nb, wb, KV_HEADS, A_DIM),
            sc[None], sn[None], sm[None])
```

```python
import functools

import jax
import jax.numpy as jnp
from jax import lax
from jax.experimental import pallas as pl
from jax.experimental.pallas import tpu as pltpu

F32 = jnp.float32
BF16 = jnp.bfloat16

D_MODEL = 1024
D_FF = 2816
FF_CHUNK = 256
N_FF_CHUNKS = D_FF // FF_CHUNK
M_HEADS = 4
M_DIM = 128
M_WIDTH = M_HEADS * M_DIM
A_HEADS = 8
A_DIM = 64
A_WIDTH = A_HEADS * A_DIM
KV_HEADS = 2
KV_WIDTH = KV_HEADS * A_DIM
A_GROUP = A_HEADS // KV_HEADS
WINDOW = 128
BLOCK = 128
GATE_PAD = 128
RMS_EPS = 1e-6
FFN_RES_WEIGHT = 0.5
NEG_INF = float("-inf")
VMEM_LIMIT_BYTES = 56 * 1024 * 1024


def _dot(a, b):
    return jnp.dot(a, b, preferred_element_type=F32)


def _dot_nt(a, b):
    return lax.dot_general(a, b, (((1,), (1,)), ((), ())), preferred_element_type=F32)


def _rms_rows(x, gain):
    ms = jnp.mean(x * x, axis=-1, keepdims=True)
    return x * lax.rsqrt(ms + RMS_EPS) * gain


def _log_sigmoid(x):
    return jnp.minimum(x, 0.0) - jnp.log1p(jnp.exp(-jnp.abs(x)))


def _group_mean_sq(x, ones_blockdiag, group):
    sq = x * x
    hi = sq.astype(BF16)
    lo = (sq - hi.astype(F32)).astype(BF16)
    return (_dot(hi, ones_blockdiag) + _dot(lo, ones_blockdiag)) * (1.0 / group)


def _ffn_into(h_ref, wg_ref, wu_ref, wd_ref, acc_ref):
    for c in range(N_FF_CHUNKS):
        lo, hi = c * FF_CHUNK, (c + 1) * FF_CHUNK
        h = h_ref[...]
        g = _dot(h, wg_ref[:, lo:hi])
        u = _dot(h, wu_ref[:, lo:hi])
        a = (g * jax.nn.sigmoid(g) * u).astype(BF16)
        d = _dot(a, wd_ref[lo:hi, :])
        if c == 0:
            acc_ref[...] = d
        else:
            acc_ref[...] += d


def _front_kernel(x_ref, n1_ref, wg_ref, wu_ref, wd_ref, n2_ref, wm_ref, wqa_ref, wkvg_ref,
                  qgain_ref, kgain_ref, e512_ref, e128_ref,
                  x1_ref, qkv_ref, om_ref, qa_ref, ka_ref, va_ref, gt_ref, vt_ref, kx_ref, vx_ref,
                  h_ref, acc_ref):
    x = x_ref[...]
    h_ref[...] = _rms_rows(x, n1_ref[...]).astype(BF16)
    _ffn_into(h_ref, wg_ref, wu_ref, wd_ref, acc_ref)
    x1 = x + FFN_RES_WEIGHT * acc_ref[...]
    x1_ref[...] = x1
    h_ref[...] = _rms_rows(x1, n2_ref[...]).astype(BF16)
    h = h_ref[...]

    qkv_ref[:, 0:M_WIDTH] = _dot(h, wm_ref[:, 0:M_WIDTH]).astype(BF16)
    k_m = _dot(h, wm_ref[:, M_WIDTH:2 * M_WIDTH]) * (M_DIM ** -0.5)
    qkv_ref[:, M_WIDTH:2 * M_WIDTH] = k_m.astype(BF16)
    v_m = _dot(h, wm_ref[:, 2 * M_WIDTH:3 * M_WIDTH])
    qkv_ref[:, 2 * M_WIDTH:3 * M_WIDTH] = v_m.astype(BF16)
    for c in range(vt_ref.shape[0]):
        for hd in range(M_HEADS):
            blk = v_m[c * BLOCK:(c + 1) * BLOCK, hd * M_DIM:(hd + 1) * M_DIM]
            vt_ref[c, hd] = blk.T.astype(BF16)
    om_ref[...] = _dot(h, wm_ref[:, 3 * M_WIDTH:4 * M_WIDTH])

    qa = _dot(h, wqa_ref[...])
    q_ms = _group_mean_sq(qa, e512_ref[...], A_DIM)
    qa_ref[...] = (qa * lax.rsqrt(q_ms + RMS_EPS) * qgain_ref[...] * (A_DIM ** -0.5)).astype(BF16)

    kvg = _dot(h, wkvg_ref[...])
    ka = kvg[:, 0:KV_WIDTH]
    k_ms = _group_mean_sq(ka, e128_ref[...], A_DIM)
    ka = ka * lax.rsqrt(k_ms + RMS_EPS) * kgain_ref[...]
    va = kvg[:, KV_WIDTH:2 * KV_WIDTH]
    ka_ref[...] = ka
    va_ref[...] = va
    gt_ref[...] = kvg[:, 2 * KV_WIDTH:2 * KV_WIDTH + GATE_PAD].T[0:2 * M_HEADS, :]

    in_lo = lax.broadcasted_iota(jnp.int32, ka.shape, 1) < A_DIM
    for src, dst in ((ka, kx_ref), (va, vx_ref)):
        x0 = jnp.where(in_lo, src, 0.0)
        x1 = jnp.where(in_lo, 0.0, src)
        dst[:, 0:128] = x0.astype(BF16)
        dst[:, 128:256] = pltpu.roll(x0, A_DIM, 1).astype(BF16)
        dst[:, 256:384] = pltpu.roll(x1, A_DIM, 1).astype(BF16)
        dst[:, 384:512] = x1.astype(BF16)


def _const_spec(shape):
    nd = len(shape)
    return pl.BlockSpec(shape, lambda i: (0,) * nd, pipeline_mode=pl.Buffered(1))


def _front(x2d, n1, wg, wu, wd, n2, wm, wqa, wkvg, qgain, kgain, e512, e128, tm):
    n = x2d.shape[0]
    row = lambda w: pl.BlockSpec((tm, w), lambda i: (i, 0))
    out_shape = (
        jax.ShapeDtypeStruct((n, D_MODEL), F32),
        jax.ShapeDtypeStruct((n, 3 * M_WIDTH), BF16),
        jax.ShapeDtypeStruct((n, M_WIDTH), F32),
        jax.ShapeDtypeStruct((n, A_WIDTH), BF16),
        jax.ShapeDtypeStruct((n, KV_WIDTH), F32),
        jax.ShapeDtypeStruct((n, KV_WIDTH), F32),
        jax.ShapeDtypeStruct((2 * M_HEADS, n), F32),
        jax.ShapeDtypeStruct((n // BLOCK, M_HEADS, M_DIM, BLOCK), BF16),
        jax.ShapeDtypeStruct((n, 4 * KV_WIDTH), BF16),
        jax.ShapeDtypeStruct((n, 4 * KV_WIDTH), BF16),
    )
    nb_t = tm // BLOCK
    return pl.pallas_call(
        _front_kernel,
        out_shape=out_shape,
        grid=(n // tm,),
        in_specs=[row(D_MODEL), _const_spec(n1.shape), _const_spec(wg.shape), _const_spec(wu.shape),
                  _const_spec(wd.shape), _const_spec(n2.shape), _const_spec(wm.shape),
                  _const_spec(wqa.shape), _const_spec(wkvg.shape), _const_spec(qgain.shape),
                  _const_spec(kgain.shape), _const_spec(e512.shape), _const_spec(e128.shape)],
        out_specs=(row(D_MODEL), row(3 * M_WIDTH), row(M_WIDTH), row(A_WIDTH), row(KV_WIDTH),
                   row(KV_WIDTH), pl.BlockSpec((2 * M_HEADS, tm), lambda i: (0, i)),
                   pl.BlockSpec((nb_t, M_HEADS, M_DIM, BLOCK), lambda i: (i, 0, 0, 0)),
                   row(4 * KV_WIDTH), row(4 * KV_WIDTH)),
        scratch_shapes=[pltpu.VMEM((tm, D_MODEL), BF16), pltpu.VMEM((tm, D_MODEL), F32)],
        compiler_params=pltpu.CompilerParams(dimension_semantics=("arbitrary",),
                                             vmem_limit_bytes=VMEM_LIMIT_BYTES),
        name="front",
    )(x2d, n1, wg, wu, wd, n2, wm, wqa, wkvg, qgain, kgain, e512, e128)


def _back_kernel(x1_ref, y_ref, wo_ref, n_ref, wg_ref, wu_ref, wd_ref, out_ref, h_ref, acc_ref):
    x2 = x1_ref[...] + _dot(y_ref[...], wo_ref[...])
    h_ref[...] = _rms_rows(x2, n_ref[...]).astype(BF16)
    _ffn_into(h_ref, wg_ref, wu_ref, wd_ref, acc_ref)
    out_ref[...] = x2 + FFN_RES_WEIGHT * acc_ref[...]


def _back(x1, y, wo, nrm, wg, wu, wd, tm):
    n = x1.shape[0]
    row = lambda w: pl.BlockSpec((tm, w), lambda i: (i, 0))
    return pl.pallas_call(
        _back_kernel,
        out_shape=jax.ShapeDtypeStruct((n, D_MODEL), F32),
        grid=(n // tm,),
        in_specs=[row(D_MODEL), row(D_MODEL), _const_spec(wo.shape), _const_spec(nrm.shape),
                  _const_spec(wg.shape), _const_spec(wu.shape), _const_spec(wd.shape)],
        out_specs=row(D_MODEL),
        scratch_shapes=[pltpu.VMEM((tm, D_MODEL), BF16), pltpu.VMEM((tm, D_MODEL), F32)],
        compiler_params=pltpu.CompilerParams(dimension_semantics=("arbitrary",),
                                             vmem_limit_bytes=VMEM_LIMIT_BYTES),
        name="back",
    )(x1, y, wo, nrm, wg, wu, wd)


def _gate_prep_kernel(gt_ref, bias_ref, tri_ref, rows_ref, cols_ref, m_ref):
    nblk = rows_ref.shape[1]
    sub = lax.broadcasted_iota(jnp.int32, (8, BLOCK), 0)
    lane = lax.broadcasted_iota(jnp.int32, (8, BLOCK), 1)
    is_head = sub < M_HEADS
    pad = jnp.zeros((BLOCK - 24, BLOCK), F32)
    m_prev = jnp.zeros((8, 1), F32)
    for j in range(nblk):
        pre = gt_ref[:, j * BLOCK:(j + 1) * BLOCK] + bias_ref[...]
        r = jnp.where(is_head, pre, _log_sigmoid(pre))
        cum = jnp.dot(r, tri_ref[...], preferred_element_type=F32, precision=lax.Precision.HIGHEST)
        bcum = pltpu.roll(cum, M_HEADS, 0)
        g = jnp.where(is_head, r - bcum, 0.0)
        bcum = jnp.where(is_head, bcum, 0.0)
        cm = g
        for shift in (1, 2, 4, 8, 16, 32, 64):
            cm = jnp.maximum(cm, jnp.where(lane >= shift, pltpu.roll(cm, shift, 1), NEG_INF))
        cm_last = jnp.max(cm, axis=-1, keepdims=True)
        b_last = jnp.sum(jnp.where(lane == BLOCK - 1, bcum, 0.0), axis=-1, keepdims=True)
        mx = jnp.maximum(m_prev, cm)
        mx_last = jnp.maximum(m_prev, cm_last)
        rows_ref[0, j, 0] = g
        rows_ref[0, j, 1] = jnp.exp(g - mx_last)
        rows_ref[0, j, 2] = jnp.broadcast_to(jnp.exp(m_prev - mx_last), (8, BLOCK))
        col_src = jnp.concatenate([-mx, jnp.exp(m_prev - mx), jnp.exp(-(bcum + mx)), pad], axis=0)
        cols_ref[j * BLOCK:(j + 1) * BLOCK, :] = col_src.T
        m_prev = b_last + mx_last
    m_ref[0] = jnp.broadcast_to(m_prev, (8, BLOCK))


def _gate_prep(gt, bias8, tri, batch, seq):
    nblk = seq // BLOCK
    out_shape = (
        jax.ShapeDtypeStruct((batch, nblk, 3, 8, BLOCK), F32),
        jax.ShapeDtypeStruct((batch * seq, 128), F32),
        jax.ShapeDtypeStruct((batch, 8, BLOCK), F32),
    )
    return pl.pallas_call(
        _gate_prep_kernel,
        out_shape=out_shape,
        grid=(batch,),
        in_specs=[pl.BlockSpec((2 * M_HEADS, seq), lambda b: (0, b)),
                  pl.BlockSpec(bias8.shape, lambda b: (0, 0)), pl.BlockSpec(tri.shape, lambda b: (0, 0))],
        out_specs=(pl.BlockSpec((1, nblk, 3, 8, BLOCK), lambda b: (b, 0, 0, 0, 0)),
                   pl.BlockSpec((seq, 128), lambda b: (b, 0)),
                   pl.BlockSpec((1, 8, BLOCK), lambda b: (b, 0, 0))),
        compiler_params=pltpu.CompilerParams(dimension_semantics=("arbitrary",)),
        name="gate_prep",
    )(gt, bias8, tri)


def _prompt_mixer_kernel(sinks_ref, qkv_ref, vt_ref, om_ref, cols_ref, rows_ref, qa_ref,
                         kx_ref, vx_ref, kxp_ref, vxp_ref, ogain_ref,
                         y_ref, c_out_ref, n_out_ref,
                         c_ref, n_ref):
    j = pl.program_id(0)
    batch = qkv_ref.shape[0]

    @pl.when(j == 0)
    def _():
        c_ref[...] = jnp.zeros_like(c_ref)
        n_ref[...] = jnp.zeros_like(n_ref)

    rows = lax.broadcasted_iota(jnp.int32, (BLOCK, BLOCK), 0)
    cols = lax.broadcasted_iota(jnp.int32, (BLOCK, BLOCK), 1)
    causal = cols <= rows
    lane_lo = lax.broadcasted_iota(jnp.int32, (2 * BLOCK, KV_WIDTH), 1) < A_DIM
    ones_m = jnp.ones((BLOCK, M_DIM), BF16)
    ones_half = (jnp.where(lane_lo, 1.0, 0.0).astype(BF16), jnp.where(lane_lo, 0.0, 1.0).astype(BF16))
    qi = lax.broadcasted_iota(jnp.int32, (2 * BLOCK, 2 * BLOCK), 0) % BLOCK
    kc = lax.broadcasted_iota(jnp.int32, (2 * BLOCK, 2 * BLOCK), 1)
    first_valid = jnp.where(j == 0, BLOCK, 0)
    valid = (kc >= qi) & (kc <= qi + WINDOW) & (kc >= first_valid)
    top_rows = lax.broadcasted_iota(jnp.int32, (2 * BLOCK, 1), 0) < BLOCK

    heads = range(M_HEADS)
    groups = [(kv, parity) for kv in range(KV_HEADS) for parity in range(2)]
    m_lo = lambda h: h * M_DIM

    for b in range(batch):
        col_b = cols_ref[b]
        q = [qkv_ref[b, :, m_lo(h):m_lo(h) + M_DIM] for h in heads]
        k = [qkv_ref[b, :, M_WIDTH + m_lo(h):M_WIDTH + m_lo(h) + M_DIM] for h in heads]
        v = [qkv_ref[b, :, 2 * M_WIDTH + m_lo(h):2 * M_WIDTH + m_lo(h) + M_DIM] for h in heads]
        g_row = [rows_ref[b, 0, 0][h:h + 1, :] for h in heads]
        wend_row = [rows_ref[b, 0, 1][h:h + 1, :] for h in heads]
        decay = [rows_ref[b, 0, 2][h:h + 1, :] for h in heads]
        c_prev = [c_ref[b, h] for h in heads]
        n_prev = [n_ref[b, h] for h in heads]

        qkc, upd = [], []
        for h in heads:
            n_rep = jnp.broadcast_to(n_prev[h][0:1, :], (BLOCK, M_DIM)).astype(BF16)
            rhs = jnp.concatenate([k[h], c_prev[h].astype(BF16), n_rep], axis=0)
            qkc.append(_dot_nt(q[h], rhs))
        for h in heads:
            vw_t = (vt_ref[b, 0, h].astype(F32) * wend_row[h]).astype(BF16)
            w_rep = jnp.broadcast_to(wend_row[h], (16, BLOCK)).astype(BF16)
            upd.append(_dot(jnp.concatenate([vw_t, w_rep], axis=0), k[h]))
        q2 = [jnp.concatenate([qa_ref[b, :, (2 * kv) * 128:(2 * kv + 1) * 128],
                               qa_ref[b, :, (2 * kv + 1) * 128:(2 * kv + 2) * 128]], axis=0)
              for kv in range(KV_HEADS)]
        sc = []
        for kv, parity in groups:
            var = (2 * kv + parity) * KV_WIDTH
            k_band = jnp.concatenate([kxp_ref[b, :, var:var + KV_WIDTH], kx_ref[b, :, var:var + KV_WIDTH]], axis=0)
            sc.append(_dot_nt(q2[kv], k_band))

        s = []
        for h in heads:
            d = jnp.where(causal, jnp.exp(col_b[:, h:h + 1] + g_row[h]), 0.0)
            s.append((qkc[h][:, 0:BLOCK] * d).astype(BF16))
            c_ref[b, h] = decay[h] * c_prev[h] + upd[h][0:M_DIM]
            n_ref[b, h] = decay[h] * n_prev[h] + upd[h][M_DIM:M_DIM + 8]
        pr, e_sink = [], []
        for gi, (kv, parity) in enumerate(groups):
            sink = jnp.where(top_rows, sinks_ref[4 * kv + parity], sinks_ref[4 * kv + 2 + parity])
            scm = jnp.where(valid, sc[gi], NEG_INF)
            mx = jnp.maximum(jnp.max(scm, axis=-1, keepdims=True), sink)
            pr.append(jnp.exp(scm - mx).astype(BF16))
            e_sink.append(jnp.exp(sink - mx))

        sv = [_dot(s[h], jnp.concatenate([v[h], ones_m], axis=1)) for h in heads]
        pv = []
        for gi, (kv, parity) in enumerate(groups):
            var = (2 * kv + parity) * KV_WIDTH
            v_band = jnp.concatenate([vxp_ref[b, :, var:var + KV_WIDTH], vx_ref[b, :, var:var + KV_WIDTH]], axis=0)
            pv.append(_dot(pr[gi], jnp.concatenate([v_band, ones_half[parity]], axis=1)))

        for h in heads:
            lo, hi = m_lo(h), m_lo(h) + M_DIM
            wi_col = col_b[:, 8 + h:9 + h]
            num = wi_col * qkc[h][:, BLOCK:2 * BLOCK] + sv[h][:, 0:M_DIM]
            den = wi_col * qkc[h][:, 2 * BLOCK:3 * BLOCK] + sv[h][:, M_DIM:2 * M_DIM]
            hh = num / jnp.maximum(jnp.abs(den), col_b[:, 16 + h:17 + h])
            hh = hh * lax.rsqrt(jnp.mean(hh * hh, axis=-1, keepdims=True) + RMS_EPS)
            hh = hh * ogain_ref[:, lo:hi] * jax.nn.sigmoid(om_ref[b, :, lo:hi])
            y_ref[b, :, lo:hi] = hh.astype(BF16)
        for kv in range(KV_HEADS):
            acc = pv[2 * kv] + pv[2 * kv + 1]
            denom = acc[:, KV_WIDTH:] + jnp.where(lane_lo, e_sink[2 * kv], e_sink[2 * kv + 1])
            ya = (acc[:, 0:KV_WIDTH] / denom).astype(BF16)
            p0, p1 = 2 * kv, 2 * kv + 1
            y_ref[b, :, M_WIDTH + p0 * 128:M_WIDTH + (p0 + 1) * 128] = ya[0:BLOCK]
            y_ref[b, :, M_WIDTH + p1 * 128:M_WIDTH + (p1 + 1) * 128] = ya[BLOCK:2 * BLOCK]

    @pl.when(j == pl.num_programs(0) - 1)
    def _():
        c_out_ref[...] = c_ref[...]
        n_out_ref[...] = n_ref[...]


def _prompt_mixer(sinks, qkv, vt, om, cols, rows, qa, kx, vx, ogain, batch, seq):
    nblk = seq // BLOCK
    r3 = lambda a: a.reshape(batch, seq, a.shape[-1])
    cur = lambda w: pl.BlockSpec((batch, BLOCK, w), lambda j: (0, j, 0))
    prev = lambda w: pl.BlockSpec((batch, BLOCK, w), lambda j: (0, jnp.maximum(j - 1, 0), 0))
    state = lambda shape: pl.BlockSpec((batch,) + shape, lambda j: (0,) * (len(shape) + 1))
    out_shape = (
        jax.ShapeDtypeStruct((batch, seq, D_MODEL), BF16),
        jax.ShapeDtypeStruct((batch, M_HEADS, M_DIM, M_DIM), F32),
        jax.ShapeDtypeStruct((batch, M_HEADS, 8, M_DIM), F32),
    )
    y, pc, pn = pl.pallas_call(
        _prompt_mixer_kernel,
        out_shape=out_shape,
        grid=(nblk,),
        in_specs=[pl.BlockSpec(memory_space=pltpu.SMEM),
                  cur(3 * M_WIDTH),
                  pl.BlockSpec((batch, 1, M_HEADS, M_DIM, BLOCK), lambda j: (0, j, 0, 0, 0)),
                  cur(M_WIDTH), cur(128),
                  pl.BlockSpec((batch, 1, 3, 8, BLOCK), lambda j: (0, j, 0, 0, 0)),
                  cur(A_WIDTH), cur(4 * KV_WIDTH), cur(4 * KV_WIDTH), prev(4 * KV_WIDTH), prev(4 * KV_WIDTH),
                  pl.BlockSpec(ogain.shape, lambda j: (0, 0))],
        out_specs=(cur(D_MODEL), state((M_HEADS, M_DIM, M_DIM)), state((M_HEADS, 8, M_DIM))),
        scratch_shapes=[pltpu.VMEM((batch, M_HEADS, M_DIM, M_DIM), F32),
                        pltpu.VMEM((batch, M_HEADS, 8, M_DIM), F32)],
        compiler_params=pltpu.CompilerParams(dimension_semantics=("arbitrary",),
                                             vmem_limit_bytes=VMEM_LIMIT_BYTES),
        name="prompt_mixer",
    )(sinks, r3(qkv), vt.reshape(batch, nblk, M_HEADS, M_DIM, BLOCK), r3(om), r3(cols), rows, r3(qa),
      r3(kx), r3(vx), r3(kx), r3(vx), ogain)
    return y.reshape(batch * seq, D_MODEL), pc, pn


def _sample_mlstm_kernel(bi_ref, bf_ref, qt_ref, kt_ref, vt_ref, ot_ref, gi_ref, gf_ref, m0_ref, n0t_ref,
                         qr_ref, kr_ref, c_ref, gain_ref,
                         yt_ref, nt_out_ref, m_out_ref, c_out_ref, decay_ref):
    h = pl.program_id(0)
    nb = qr_ref.shape[0]
    qt, kt, vt = qt_ref[0], kt_ref[0], vt_ref[0]
    i_pre = gi_ref[0] + bi_ref[h]
    a = _log_sigmoid(gf_ref[0] + bf_ref[h]) + m0_ref[0]
    m_t = jnp.maximum(a, i_pre)
    w_inter = jnp.exp(a - m_t)
    w_in = jnp.exp(i_pre - m_t)
    scores = jnp.sum(qt * kt, axis=0, keepdims=True) * w_in
    n0t = n0t_ref[0]
    nq = jnp.sum(n0t * qt, axis=0, keepdims=True)

    rows = lax.broadcasted_iota(jnp.int32, (nb, M_DIM), 0)
    cols = lax.broadcasted_iota(jnp.int32, (M_DIM, nb), 1)
    vw_t = (vt * w_in).astype(BF16)
    k_rows = kr_ref[...]

    decay_ref[...] = jnp.broadcast_to(w_inter, (M_DIM, nb)).T

    def body(grp, cq_t):
        base = grp * SAMPLE_UNROLL
        for u in range(SAMPLE_UNROLL):
            b = base + u
            col = jnp.sum(c_ref[b, 0] * qr_ref[pl.ds(b, 1), :], axis=-1, keepdims=True)
            cq_t = jnp.where(cols == b, col, cq_t)
        outer = []
        for u in range(SAMPLE_UNROLL):
            k_only_b = jnp.where(rows == base + u, k_rows, 0.0).astype(BF16)
            outer.append(_dot(vw_t, k_only_b))
        for u in range(SAMPLE_UNROLL):
            b = base + u
            c_out_ref[b, 0] = decay_ref[pl.ds(b, 1), :] * c_ref[b, 0] + outer[u]
        return cq_t

    cq_t = lax.fori_loop(0, nb // SAMPLE_UNROLL, body, jnp.zeros((M_DIM, nb), F32))

    num = w_inter * cq_t + scores * vt
    den = w_inter * nq + scores
    hh = num / jnp.maximum(jnp.abs(den), jnp.exp(-m_t))
    hh = hh * lax.rsqrt(jnp.mean(hh * hh, axis=0, keepdims=True) + RMS_EPS)
    yt_ref[0] = hh * gain_ref[0] * jax.nn.sigmoid(ot_ref[0])
    nt_out_ref[0] = w_inter * n0t + w_in * kt
    m_out_ref[0] = m_t


def _sample_mlstm(b_i, b_f, qt, kt, vt, ot, gi, gf, m0, n0t, q_rows, k_rows, c0, gain_col):
    nb = q_rows.shape[0]
    head3 = lambda r: pl.BlockSpec((1, r, nb), lambda h: (h, 0, 0))
    smem = pl.BlockSpec(memory_space=pltpu.SMEM)
    out_shape = (
        jax.ShapeDtypeStruct((M_HEADS, M_DIM, nb), F32),
        jax.ShapeDtypeStruct((M_HEADS, M_DIM, nb), F32),
        jax.ShapeDtypeStruct((M_HEADS, 1, nb), F32),
        jax.ShapeDtypeStruct((nb, M_HEADS, M_DIM, M_DIM), F32),
    )
    c_spec = pl.BlockSpec((nb, 1, M_DIM, M_DIM), lambda h: (0, h, 0, 0))
    return pl.pallas_call(
        _sample_mlstm_kernel,
        out_shape=out_shape,
        grid=(M_HEADS,),
        in_specs=[smem, smem, head3(M_DIM), head3(M_DIM), head3(M_DIM), head3(M_DIM), head3(1), head3(1),
                  head3(1), head3(M_DIM),
                  pl.BlockSpec((nb, M_DIM), lambda h: (0, h)), pl.BlockSpec((nb, M_DIM), lambda h: (0, h)),
                  c_spec, pl.BlockSpec((1, M_DIM, 1), lambda h: (h, 0, 0))],
        out_specs=(head3(M_DIM), head3(M_DIM), head3(1), c_spec),
        scratch_shapes=[pltpu.VMEM((nb, M_DIM), F32)],
        compiler_params=pltpu.CompilerParams(dimension_semantics=("arbitrary",),
                                             vmem_limit_bytes=VMEM_LIMIT_BYTES),
        name="sample_mlstm",
    )(b_i, b_f, qt, kt, vt, ot, gi, gf, m0, n0t, q_rows, k_rows, c0, gain_col)


SAMPLE_TILE = 8
SAMPLE_UNROLL = 8


def _sample_swa_kernel(q2_ref, kc_ref, vc_ref, kn_ref, vn_ref, sink_ref, o_ref, ko_ref, vo_ref):
    sink = sink_ref[...]
    w = kc_ref.shape[1]
    tile = range(SAMPLE_TILE)
    q2 = [q2_ref[b] for b in tile]
    k_new = [kn_ref[b:b + 1, :] for b in tile]
    v_new = [vn_ref[b:b + 1, :] for b in tile]
    s_c = [_dot_nt(q2[b], kc_ref[b].astype(BF16)) for b in tile]
    s_n = [jnp.sum(q2[b].astype(F32) * k_new[b], axis=-1, keepdims=True) for b in tile]
    mx = [jnp.maximum(jnp.maximum(jnp.max(s_c[b], axis=-1, keepdims=True), s_n[b]), sink) for b in tile]
    p_c = [jnp.exp(s_c[b] - mx[b]) for b in tile]
    p_n = [jnp.exp(s_n[b] - mx[b]) for b in tile]
    denom = [jnp.sum(p_c[b], axis=-1, keepdims=True) + p_n[b] + jnp.exp(sink - mx[b]) for b in tile]
    o = [_dot(p_c[b].astype(BF16), vc_ref[b].astype(BF16)) for b in tile]
    for b in tile:
        o_ref[b] = (o[b] + p_n[b] * v_new[b]) / denom[b]
        ko_ref[b, 0:w - 1, :] = kc_ref[b, 1:w, :]
        ko_ref[b, w - 1:w, :] = k_new[b]
        vo_ref[b, 0:w - 1, :] = vc_ref[b, 1:w, :]
        vo_ref[b, w - 1:w, :] = v_new[b]


def _sample_swa(q2, k_cache, v_cache, k_new, v_new, sink_col):
    nb, w, _ = k_cache.shape
    t3 = lambda a, c: pl.BlockSpec((SAMPLE_TILE, a, c), lambda i: (i, 0, 0))
    t2 = pl.BlockSpec((SAMPLE_TILE, KV_WIDTH), lambda i: (i, 0))
    out_shape = (
        jax.ShapeDtypeStruct((nb, A_HEADS, KV_WIDTH), F32),
        jax.ShapeDtypeStruct((nb, w, KV_WIDTH), F32),
        jax.ShapeDtypeStruct((nb, w, KV_WIDTH), F32),
    )
    return pl.pallas_call(
        _sample_swa_kernel,
        out_shape=out_shape,
        grid=(nb // SAMPLE_TILE,),
        in_specs=[t3(A_HEADS, KV_WIDTH), t3(w, KV_WIDTH), t3(w, KV_WIDTH), t2, t2,
                  pl.BlockSpec((A_HEADS, 1), lambda i: (0, 0))],
        out_specs=(t3(A_HEADS, KV_WIDTH), t3(w, KV_WIDTH), t3(w, KV_WIDTH)),
        compiler_params=pltpu.CompilerParams(dimension_semantics=("arbitrary",)),
        name="sample_swa",
    )(q2, k_cache, v_cache, k_new, v_new, sink_col)


def _blockdiag_ones(n, group):
    idx = jnp.arange(n) // group
    return (idx[:, None] == idx[None, :]).astype(BF16)


def _prep_weights(ffn1_w_gate, ffn1_w_up, ffn1_w_down, w_in, w_out, ffn2_w_gate, ffn2_w_up, ffn2_w_down):
    cols = rows = lambda w: w.astype(BF16)

    m_end = 4 * M_WIDTH
    g_end = m_end + 2 * M_HEADS
    q_end = g_end + A_WIDTH
    wm = w_in[:, :m_end].astype(BF16)
    wqa = w_in[:, g_end:q_end].astype(BF16)
    gates = jnp.pad(w_in[:, m_end:g_end], ((0, 0), (0, GATE_PAD - 2 * M_HEADS)))
    wkvg = jnp.concatenate([w_in[:, q_end:], gates], axis=1).astype(BF16)
    return (cols(ffn1_w_gate), cols(ffn1_w_up), rows(ffn1_w_down), wm, wqa, wkvg, w_out.astype(BF16),
            cols(ffn2_w_gate), cols(ffn2_w_up), rows(ffn2_w_down))


def kernel(x_prompt, x_sample, cache_swa_k, cache_swa_v, state_mlstm_C, state_mlstm_n, state_mlstm_m,
           ffn1_norm, ffn1_w_gate, ffn1_w_up, ffn1_w_down, mix_norm, w_in, mlstm_b_i, mlstm_b_f,
           mlstm_out_norm, swa_q_norm, swa_k_norm, swa_sinks, w_out, ffn2_norm, ffn2_w_gate,
           ffn2_w_up, ffn2_w_down):
    depth = ffn1_norm.shape[0]
    assert depth == 1
    batch, seq, _ = x_prompt.shape
    nb = x_sample.shape[0]
    assert x_sample.shape[1] == 1 and seq % BLOCK == 0

    (wg1, wu1, wd1, wm, wqa, wkvg, wo, wg2, wu2, wd2) = _prep_weights(
        ffn1_w_gate[0], ffn1_w_up[0], ffn1_w_down[0], w_in[0], w_out[0],
        ffn2_w_gate[0], ffn2_w_up[0], ffn2_w_down[0])
    n1 = ffn1_norm[0].reshape(1, D_MODEL)
    n2 = mix_norm[0].reshape(1, D_MODEL)
    n3 = ffn2_norm[0].reshape(1, D_MODEL)
    qgain = jnp.tile(swa_q_norm[0], A_HEADS).reshape(1, A_WIDTH)
    kgain = jnp.tile(swa_k_norm[0], KV_HEADS).reshape(1, KV_WIDTH)
    e512 = _blockdiag_ones(A_WIDTH, A_DIM)
    e128 = _blockdiag_ones(KV_WIDTH, A_DIM)
    front = functools.partial(_front, n1=n1, wg=wg1, wu=wu1, wd=wd1, n2=n2, wm=wm, wqa=wqa, wkvg=wkvg,
                              qgain=qgain, kgain=kgain, e512=e512, e128=e128)
    back = functools.partial(_back, wo=wo, nrm=n3, wg=wg2, wu=wu2, wd=wd2)
    b_i, b_f = mlstm_b_i[0], mlstm_b_f[0]
    ogain = mlstm_out_norm[0].reshape(1, M_WIDTH)
    sinks = swa_sinks[0]

    xp = x_prompt.reshape(batch * seq, D_MODEL)
    x1, qkv, om, qa, ka, va, gt, vt, kx, vx = front(xp, tm=512)
    bias8 = jnp.concatenate([b_i, b_f]).reshape(2 * M_HEADS, 1)
    tri = (jnp.arange(BLOCK)[:, None] <= jnp.arange(BLOCK)[None, :]).astype(F32)
    rows, cols, pm = _gate_prep(gt, bias8, tri, batch, seq)
    y, pc, pn = _prompt_mixer(sinks, qkv, vt, om, cols, rows, qa, kx, vx, ogain, batch, seq)
    yp = back(x1, y, tm=512).reshape(batch, seq, D_MODEL)
    w = min(WINDOW, seq)
    pk = ka.reshape(batch, seq, KV_WIDTH)[:, seq - w:].reshape(batch, w, KV_HEADS, A_DIM)
    pv = va.reshape(batch, seq, KV_WIDTH)[:, seq - w:].reshape(batch, w, KV_HEADS, A_DIM)
    pn = pn[:, :, 0, :]
    pm = pm[:, 0:M_HEADS, 0]

    xs = x_sample.reshape(nb, D_MODEL)
    x1s, qkvs, oms, qas, kas, vas, gates_t = front(xs, tm=nb)[:7]
    qkv_f = qkvs.astype(F32)
    to_heads_t = lambda a: jnp.transpose(a.reshape(nb, M_HEADS, M_DIM), (1, 2, 0))
    q_rows = qkv_f[:, 0:M_WIDTH]
    k_rows = qkv_f[:, M_WIDTH:2 * M_WIDTH]
    ymt, nt, mt, sc = _sample_mlstm(
        b_i, b_f, to_heads_t(q_rows), to_heads_t(k_rows), to_heads_t(qkv_f[:, 2 * M_WIDTH:]), to_heads_t(oms),
        gates_t[0:M_HEADS].reshape(M_HEADS, 1, nb), gates_t[M_HEADS:].reshape(M_HEADS, 1, nb),
        jnp.transpose(state_mlstm_m[0]).reshape(M_HEADS, 1, nb), jnp.transpose(state_mlstm_n[0], (1, 2, 0)),
        q_rows, k_rows, state_mlstm_C[0], mlstm_out_norm[0].reshape(M_HEADS, M_DIM, 1))
    y_m = jnp.transpose(ymt, (2, 0, 1)).reshape(nb, M_WIDTH)
    sn = jnp.transpose(nt, (2, 0, 1))
    sm = jnp.transpose(mt[:, 0, :])

    qa_h = qas.reshape(nb, A_HEADS, A_DIM)
    zeros = jnp.zeros_like(qa_h)
    in_lo = (jnp.arange(A_HEADS) // A_GROUP == 0)[None, :, None]
    q2 = jnp.concatenate([jnp.where(in_lo, qa_h, zeros), jnp.where(in_lo, zeros, qa_h)], axis=-1)
    kc = cache_swa_k[0].reshape(nb, -1, KV_WIDTH)
    vc = cache_swa_v[0].reshape(nb, -1, KV_WIDTH)
    o2, sk, sv = _sample_swa(q2, kc, vc, kas, vas, sinks.reshape(A_HEADS, 1))
    o2 = o2.reshape(nb, A_HEADS, KV_HEADS, A_DIM)
    y_a = jnp.where(in_lo, o2[:, :, 0, :], o2[:, :, 1, :]).reshape(nb, A_WIDTH)
    ys_in = jnp.concatenate([y_m, y_a], axis=-1).astype(BF16)
    ys = back(x1s, ys_in, tm=nb).reshape(nb, 1, D_MODEL)

    wb = kc.shape[1]
    return (yp, ys, pk[None], pv[None], pc[None], pn[None], pm[None],
            sk.reshape(1, nb, wb, KV_HEADS, A_DIM), sv.reshape(1, nb, wb, KV_HEADS, A_DIM),
            sc[None], sn[None], sm[None])
```

```python
import functools

import jax
import jax.numpy as jnp
from jax import lax
from jax.experimental import pallas as pl
from jax.experimental.pallas import tpu as pltpu

F32 = jnp.float32
BF16 = jnp.bfloat16

D_MODEL = 1024
D_FF = 2816
FF_CHUNK = 256
N_FF_CHUNKS = D_FF // FF_CHUNK
M_HEADS = 4
M_DIM = 128
M_WIDTH = M_HEADS * M_DIM
A_HEADS = 8
A_DIM = 64
A_WIDTH = A_HEADS * A_DIM
KV_HEADS = 2
KV_WIDTH = KV_HEADS * A_DIM
A_GROUP = A_HEADS // KV_HEADS
WINDOW = 128
BLOCK = 128
GATE_PAD = 128
RMS_EPS = 1e-6
FFN_RES_WEIGHT = 0.5
NEG_INF = float("-inf")
LOG2E = 1.4426950408889634
VMEM_LIMIT_BYTES = 56 * 1024 * 1024


def _dot(a, b):
    return jnp.dot(a, b, preferred_element_type=F32)


def _dot_nt(a, b):
    return lax.dot_general(a, b, (((1,), (1,)), ((), ())), preferred_element_type=F32)


def _rms_rows(x, gain):
    ms = jnp.mean(x * x, axis=-1, keepdims=True)
    return x * lax.rsqrt(ms + RMS_EPS) * gain


def _log_sigmoid(x):
    return jnp.minimum(x, 0.0) - jnp.log1p(jnp.exp(-jnp.abs(x)))


def _half_tile_mean_sq(x):
    in_lo = lax.broadcasted_iota(jnp.int32, (x.shape[0], 128), 1) < A_DIM
    out = []
    for c in range(x.shape[1] // 128):
        sq = x[:, c * 128:(c + 1) * 128]
        sq = sq * sq
        s_lo = jnp.sum(jnp.where(in_lo, sq, 0.0), axis=-1, keepdims=True)
        s_hi = jnp.sum(jnp.where(in_lo, 0.0, sq), axis=-1, keepdims=True)
        out.append(jnp.where(in_lo, s_lo, s_hi) * (1.0 / A_DIM))
    return out[0] if len(out) == 1 else jnp.concatenate(out, axis=1)


def _ffn_into(h_ref, wg_ref, wu_ref, wd_ref, acc_ref):
    for c in range(N_FF_CHUNKS):
        lo, hi = c * FF_CHUNK, (c + 1) * FF_CHUNK
        h = h_ref[...]
        g = _dot(h, wg_ref[:, lo:hi])
        u = _dot(h, wu_ref[:, lo:hi])
        a = (g * jax.nn.sigmoid(g) * u).astype(BF16)
        d = _dot(a, wd_ref[lo:hi, :])
        if c == 0:
            acc_ref[...] = d
        else:
            acc_ref[...] += d


def _front_kernel(x_ref, n1_ref, wg_ref, wu_ref, wd_ref, n2_ref, wm_ref, wqa_ref, wkvg_ref,
                  qgain_ref, kgain_ref, ogain_ref,
                  x1_ref, qkv_ref, og_ref, qa_ref, ka_ref, va_ref, gt_ref, vt_ref, kx_ref, vx_ref,
                  h_ref, acc_ref):
    x = x_ref[...]
    h_ref[...] = _rms_rows(x, n1_ref[...]).astype(BF16)
    _ffn_into(h_ref, wg_ref, wu_ref, wd_ref, acc_ref)
    x1 = x + FFN_RES_WEIGHT * acc_ref[...]
    x1_ref[...] = x1
    h_ref[...] = _rms_rows(x1, n2_ref[...]).astype(BF16)
    h = h_ref[...]

    qa = _dot(h, wqa_ref[...])
    kvg = _dot(h, wkvg_ref[...])
    v_m = _dot(h, wm_ref[:, 2 * M_WIDTH:3 * M_WIDTH])
    qkv_ref[:, 2 * M_WIDTH:3 * M_WIDTH] = v_m.astype(BF16)
    for c in range(vt_ref.shape[0]):
        for hd in range(M_HEADS):
            blk = v_m[c * BLOCK:(c + 1) * BLOCK, hd * M_DIM:(hd + 1) * M_DIM]
            vt_ref[c, hd] = blk.T.astype(BF16)
    q_scale = qgain_ref[...] * (A_DIM ** -0.5 * LOG2E)
    qa_ref[...] = (qa * lax.rsqrt(_half_tile_mean_sq(qa) + RMS_EPS) * q_scale).astype(BF16)
    ka = kvg[:, 0:KV_WIDTH]
    ka = ka * lax.rsqrt(_half_tile_mean_sq(ka) + RMS_EPS) * kgain_ref[...]
    va = kvg[:, KV_WIDTH:2 * KV_WIDTH]
    ka_ref[...] = ka
    va_ref[...] = va
    gt_ref[...] = kvg[:, 2 * KV_WIDTH:2 * KV_WIDTH + GATE_PAD].T[0:2 * M_HEADS, :]

    in_lo = lax.broadcasted_iota(jnp.int32, ka.shape, 1) < A_DIM
    for src, dst in ((ka, kx_ref), (va, vx_ref)):
        x0 = jnp.where(in_lo, src, 0.0)
        x1 = jnp.where(in_lo, 0.0, src)
        dst[:, 0:128] = x0.astype(BF16)
        dst[:, 128:256] = pltpu.roll(x0, A_DIM, 1).astype(BF16)
        dst[:, 256:384] = pltpu.roll(x1, A_DIM, 1).astype(BF16)
        dst[:, 384:512] = x1.astype(BF16)

    qkv_ref[:, 0:M_WIDTH] = _dot(h, wm_ref[:, 0:M_WIDTH]).astype(BF16)
    k_m = _dot(h, wm_ref[:, M_WIDTH:2 * M_WIDTH]) * (M_DIM ** -0.5)
    qkv_ref[:, M_WIDTH:2 * M_WIDTH] = k_m.astype(BF16)
    og_ref[...] = jax.nn.sigmoid(_dot(h, wm_ref[:, 3 * M_WIDTH:4 * M_WIDTH])) * ogain_ref[...]


def _const_spec(shape):
    nd = len(shape)
    return pl.BlockSpec(shape, lambda i: (0,) * nd, pipeline_mode=pl.Buffered(1))


def _front(x2d, n1, wg, wu, wd, n2, wm, wqa, wkvg, qgain, kgain, ogain, tm):
    n = x2d.shape[0]
    row = lambda w: pl.BlockSpec((tm, w), lambda i: (i, 0))
    out_shape = (
        jax.ShapeDtypeStruct((n, D_MODEL), F32),
        jax.ShapeDtypeStruct((n, 3 * M_WIDTH), BF16),
        jax.ShapeDtypeStruct((n, M_WIDTH), F32),
        jax.ShapeDtypeStruct((n, A_WIDTH), BF16),
        jax.ShapeDtypeStruct((n, KV_WIDTH), F32),
        jax.ShapeDtypeStruct((n, KV_WIDTH), F32),
        jax.ShapeDtypeStruct((2 * M_HEADS, n), F32),
        jax.ShapeDtypeStruct((n // BLOCK, M_HEADS, M_DIM, BLOCK), BF16),
        jax.ShapeDtypeStruct((n, 4 * KV_WIDTH), BF16),
        jax.ShapeDtypeStruct((n, 4 * KV_WIDTH), BF16),
    )
    nb_t = tm // BLOCK
    return pl.pallas_call(
        _front_kernel,
        out_shape=out_shape,
        grid=(n // tm,),
        in_specs=[row(D_MODEL), _const_spec(n1.shape), _const_spec(wg.shape), _const_spec(wu.shape),
                  _const_spec(wd.shape), _const_spec(n2.shape), _const_spec(wm.shape),
                  _const_spec(wqa.shape), _const_spec(wkvg.shape), _const_spec(qgain.shape),
                  _const_spec(kgain.shape), _const_spec(ogain.shape)],
        out_specs=(row(D_MODEL), row(3 * M_WIDTH), row(M_WIDTH), row(A_WIDTH), row(KV_WIDTH),
                   row(KV_WIDTH), pl.BlockSpec((2 * M_HEADS, tm), lambda i: (0, i)),
                   pl.BlockSpec((nb_t, M_HEADS, M_DIM, BLOCK), lambda i: (i, 0, 0, 0)),
                   row(4 * KV_WIDTH), row(4 * KV_WIDTH)),
        scratch_shapes=[pltpu.VMEM((tm, D_MODEL), BF16), pltpu.VMEM((tm, D_MODEL), F32)],
        compiler_params=pltpu.CompilerParams(dimension_semantics=("arbitrary",),
                                             vmem_limit_bytes=VMEM_LIMIT_BYTES),
        name="front",
    )(x2d, n1, wg, wu, wd, n2, wm, wqa, wkvg, qgain, kgain, ogain)


def _back_kernel(x1_ref, y_ref, wo_ref, n_ref, wg_ref, wu_ref, wd_ref, out_ref, h_ref, acc_ref):
    x2 = x1_ref[...] + _dot(y_ref[...], wo_ref[...])
    h_ref[...] = _rms_rows(x2, n_ref[...]).astype(BF16)
    _ffn_into(h_ref, wg_ref, wu_ref, wd_ref, acc_ref)
    out_ref[...] = x2 + FFN_RES_WEIGHT * acc_ref[...]


def _back(x1, y, wo, nrm, wg, wu, wd, tm):
    n = x1.shape[0]
    row = lambda w: pl.BlockSpec((tm, w), lambda i: (i, 0))
    return pl.pallas_call(
        _back_kernel,
        out_shape=jax.ShapeDtypeStruct((n, D_MODEL), F32),
        grid=(n // tm,),
        in_specs=[row(D_MODEL), row(D_MODEL), _const_spec(wo.shape), _const_spec(nrm.shape),
                  _const_spec(wg.shape), _const_spec(wu.shape), _const_spec(wd.shape)],
        out_specs=row(D_MODEL),
        scratch_shapes=[pltpu.VMEM((tm, D_MODEL), BF16), pltpu.VMEM((tm, D_MODEL), F32)],
        compiler_params=pltpu.CompilerParams(dimension_semantics=("arbitrary",),
                                             vmem_limit_bytes=VMEM_LIMIT_BYTES),
        name="back",
    )(x1, y, wo, nrm, wg, wu, wd)


def _gate_prep_kernel(gt_ref, bias_ref, tri_ref, rows_ref, cols_ref, m_ref):
    nblk = rows_ref.shape[1]
    sub = lax.broadcasted_iota(jnp.int32, (8, BLOCK), 0)
    lane = lax.broadcasted_iota(jnp.int32, (8, BLOCK), 1)
    is_head = sub < M_HEADS
    pad = jnp.zeros((BLOCK - 24, BLOCK), F32)
    m_prev = jnp.zeros((8, 1), F32)
    for j in range(nblk):
        pre = gt_ref[:, j * BLOCK:(j + 1) * BLOCK] + bias_ref[...]
        r = jnp.where(is_head, pre, _log_sigmoid(pre))
        cum = jnp.dot(r, tri_ref[...], preferred_element_type=F32, precision=lax.Precision.HIGHEST)
        bcum = pltpu.roll(cum, M_HEADS, 0)
        g = jnp.where(is_head, r - bcum, 0.0)
        bcum = jnp.where(is_head, bcum, 0.0)
        cm = g
        for shift in (1, 2, 4, 8, 16, 32, 64):
            cm = jnp.maximum(cm, jnp.where(lane >= shift, pltpu.roll(cm, shift, 1), NEG_INF))
        cm_last = jnp.max(cm, axis=-1, keepdims=True)
        b_last = jnp.sum(jnp.where(lane == BLOCK - 1, bcum, 0.0), axis=-1, keepdims=True)
        mx = jnp.maximum(m_prev, cm)
        mx_last = jnp.maximum(m_prev, cm_last)
        rows_ref[0, j, 0] = g * LOG2E
        rows_ref[0, j, 1] = jnp.exp(g - mx_last)
        rows_ref[0, j, 2] = jnp.broadcast_to(jnp.exp(m_prev - mx_last), (8, BLOCK))
        col_src = jnp.concatenate([mx * -LOG2E, jnp.exp(m_prev - mx), jnp.exp(-(bcum + mx)), pad], axis=0)
        cols_ref[j * BLOCK:(j + 1) * BLOCK, :] = col_src.T
        m_prev = b_last + mx_last
    m_ref[0] = jnp.broadcast_to(m_prev, (8, BLOCK))


def _gate_prep(gt, bias8, tri, batch, seq):
    nblk = seq // BLOCK
    out_shape = (
        jax.ShapeDtypeStruct((batch, nblk, 3, 8, BLOCK), F32),
        jax.ShapeDtypeStruct((batch * seq, 128), F32),
        jax.ShapeDtypeStruct((batch, 8, BLOCK), F32),
    )
    return pl.pallas_call(
        _gate_prep_kernel,
        out_shape=out_shape,
        grid=(batch,),
        in_specs=[pl.BlockSpec((2 * M_HEADS, seq), lambda b: (0, b)),
                  pl.BlockSpec(bias8.shape, lambda b: (0, 0)), pl.BlockSpec(tri.shape, lambda b: (0, 0))],
        out_specs=(pl.BlockSpec((1, nblk, 3, 8, BLOCK), lambda b: (b, 0, 0, 0, 0)),
                   pl.BlockSpec((seq, 128), lambda b: (b, 0)),
                   pl.BlockSpec((1, 8, BLOCK), lambda b: (b, 0, 0))),
        compiler_params=pltpu.CompilerParams(dimension_semantics=("arbitrary",)),
        name="gate_prep",
    )(gt, bias8, tri)


def _prompt_mixer_kernel(sinks_ref, qkv_ref, vt_ref, og_ref, cols_ref, rows_ref, qa_ref,
                         kx_ref, vx_ref, kxp_ref, vxp_ref,
                         y_ref, c_out_ref, n_out_ref,
                         c_ref, n_ref):
    j = pl.program_id(0)
    batch = qkv_ref.shape[0]

    @pl.when(j == 0)
    def _():
        c_ref[...] = jnp.zeros_like(c_ref)
        n_ref[...] = jnp.zeros_like(n_ref)

    rows = lax.broadcasted_iota(jnp.int32, (BLOCK, BLOCK), 0)
    cols = lax.broadcasted_iota(jnp.int32, (BLOCK, BLOCK), 1)
    causal = cols <= rows
    lane_lo = lax.broadcasted_iota(jnp.int32, (2 * BLOCK, KV_WIDTH), 1) < A_DIM
    ones_m = jnp.ones((BLOCK, M_DIM), BF16)
    ones_half = (jnp.where(lane_lo, 1.0, 0.0).astype(BF16), jnp.where(lane_lo, 0.0, 1.0).astype(BF16))
    qi = lax.broadcasted_iota(jnp.int32, (2 * BLOCK, 2 * BLOCK), 0) % BLOCK
    kc = lax.broadcasted_iota(jnp.int32, (2 * BLOCK, 2 * BLOCK), 1)
    first_valid = jnp.where(j == 0, BLOCK, 0)
    valid = (kc >= qi) & (kc <= qi + WINDOW) & (kc >= first_valid)
    top_rows = lax.broadcasted_iota(jnp.int32, (2 * BLOCK, 1), 0) < BLOCK

    heads = range(M_HEADS)
    groups = [(kv, parity) for kv in range(KV_HEADS) for parity in range(2)]
    m_lo = lambda h: h * M_DIM

    for b in range(batch):
        col_b = cols_ref[b]
        q = [qkv_ref[b, :, m_lo(h):m_lo(h) + M_DIM] for h in heads]
        k = [qkv_ref[b, :, M_WIDTH + m_lo(h):M_WIDTH + m_lo(h) + M_DIM] for h in heads]
        v = [qkv_ref[b, :, 2 * M_WIDTH + m_lo(h):2 * M_WIDTH + m_lo(h) + M_DIM] for h in heads]
        g_row = [rows_ref[b, 0, 0][h:h + 1, :] for h in heads]
        wend_row = [rows_ref[b, 0, 1][h:h + 1, :] for h in heads]
        decay = [rows_ref[b, 0, 2][h:h + 1, :] for h in heads]
        c_prev = [c_ref[b, h] for h in heads]
        n_prev = [n_ref[b, h] for h in heads]

        qkc, upd = [], []
        for h in heads:
            n_rep = jnp.broadcast_to(n_prev[h][0:1, :], (BLOCK, M_DIM)).astype(BF16)
            rhs = jnp.concatenate([k[h], c_prev[h].astype(BF16), n_rep], axis=0)
            qkc.append(_dot_nt(q[h], rhs))
        for h in heads:
            vw_t = (vt_ref[b, 0, h].astype(F32) * wend_row[h]).astype(BF16)
            w_rep = jnp.broadcast_to(wend_row[h], (16, BLOCK)).astype(BF16)
            upd.append(_dot(jnp.concatenate([vw_t, w_rep], axis=0), k[h]))
        q2 = [jnp.concatenate([qa_ref[b, :, (2 * kv) * 128:(2 * kv + 1) * 128],
                               qa_ref[b, :, (2 * kv + 1) * 128:(2 * kv + 2) * 128]], axis=0)
              for kv in range(KV_HEADS)]
        sc = []
        for kv, parity in groups:
            var = (2 * kv + parity) * KV_WIDTH
            k_band = jnp.concatenate([kxp_ref[b, :, var:var + KV_WIDTH], kx_ref[b, :, var:var + KV_WIDTH]], axis=0)
            sc.append(_dot_nt(q2[kv], k_band))

        s = []
        for h in heads:
            d = jnp.where(causal, jnp.exp2(col_b[:, h:h + 1] + g_row[h]), 0.0)
            s.append((qkc[h][:, 0:BLOCK] * d).astype(BF16))
            c_ref[b, h] = decay[h] * c_prev[h] + upd[h][0:M_DIM]
            n_ref[b, h] = decay[h] * n_prev[h] + upd[h][M_DIM:M_DIM + 8]
        pr, e_sink = [], []
        for gi, (kv, parity) in enumerate(groups):
            sink = jnp.where(top_rows, sinks_ref[4 * kv + parity], sinks_ref[4 * kv + 2 + parity]) * LOG2E
            scm = jnp.where(valid, sc[gi], NEG_INF)
            mx = jnp.maximum(jnp.max(scm, axis=-1, keepdims=True), sink)
            pr.append(jnp.exp2(scm - mx).astype(BF16))
            e_sink.append(jnp.exp2(sink - mx))

        sv = [_dot(s[h], jnp.concatenate([v[h], ones_m], axis=1)) for h in heads]
        pv = []
        for gi, (kv, parity) in enumerate(groups):
            var = (2 * kv + parity) * KV_WIDTH
            v_band = jnp.concatenate([vxp_ref[b, :, var:var + KV_WIDTH], vx_ref[b, :, var:var + KV_WIDTH]], axis=0)
            pv.append(_dot(pr[gi], jnp.concatenate([v_band, ones_half[parity]], axis=1)))

        for h in heads:
            lo, hi = m_lo(h), m_lo(h) + M_DIM
            wi_col = col_b[:, 8 + h:9 + h]
            num = wi_col * qkc[h][:, BLOCK:2 * BLOCK] + sv[h][:, 0:M_DIM]
            den = wi_col * qkc[h][:, 2 * BLOCK:3 * BLOCK] + sv[h][:, M_DIM:2 * M_DIM]
            hh = num / jnp.maximum(jnp.abs(den), col_b[:, 16 + h:17 + h])
            hh = hh * lax.rsqrt(jnp.mean(hh * hh, axis=-1, keepdims=True) + RMS_EPS)
            y_ref[b, :, lo:hi] = (hh * og_ref[b, :, lo:hi]).astype(BF16)
        for kv in range(KV_HEADS):
            acc = pv[2 * kv] + pv[2 * kv + 1]
            denom = acc[:, KV_WIDTH:] + jnp.where(lane_lo, e_sink[2 * kv], e_sink[2 * kv + 1])
            ya = (acc[:, 0:KV_WIDTH] / denom).astype(BF16)
            p0, p1 = 2 * kv, 2 * kv + 1
            y_ref[b, :, M_WIDTH + p0 * 128:M_WIDTH + (p0 + 1) * 128] = ya[0:BLOCK]
            y_ref[b, :, M_WIDTH + p1 * 128:M_WIDTH + (p1 + 1) * 128] = ya[BLOCK:2 * BLOCK]

    @pl.when(j == pl.num_programs(0) - 1)
    def _():
        c_out_ref[...] = c_ref[...]
        n_out_ref[...] = n_ref[...]


def _prompt_mixer(sinks, qkv, vt, og, cols, rows, qa, kx, vx, batch, seq):
    nblk = seq // BLOCK
    r3 = lambda a: a.reshape(batch, seq, a.shape[-1])
    cur = lambda w: pl.BlockSpec((batch, BLOCK, w), lambda j: (0, j, 0))
    prev = lambda w: pl.BlockSpec((batch, BLOCK, w), lambda j: (0, jnp.maximum(j - 1, 0), 0))
    state = lambda shape: pl.BlockSpec((batch,) + shape, lambda j: (0,) * (len(shape) + 1))
    out_shape = (
        jax.ShapeDtypeStruct((batch, seq, D_MODEL), BF16),
        jax.ShapeDtypeStruct((batch, M_HEADS, M_DIM, M_DIM), F32),
        jax.ShapeDtypeStruct((batch, M_HEADS, 8, M_DIM), F32),
    )
    y, pc, pn = pl.pallas_call(
        _prompt_mixer_kernel,
        out_shape=out_shape,
        grid=(nblk,),
        in_specs=[pl.BlockSpec(memory_space=pltpu.SMEM),
                  cur(3 * M_WIDTH),
                  pl.BlockSpec((batch, 1, M_HEADS, M_DIM, BLOCK), lambda j: (0, j, 0, 0, 0)),
                  cur(M_WIDTH), cur(128),
                  pl.BlockSpec((batch, 1, 3, 8, BLOCK), lambda j: (0, j, 0, 0, 0)),
                  cur(A_WIDTH), cur(4 * KV_WIDTH), cur(4 * KV_WIDTH), prev(4 * KV_WIDTH), prev(4 * KV_WIDTH)],
        out_specs=(cur(D_MODEL), state((M_HEADS, M_DIM, M_DIM)), state((M_HEADS, 8, M_DIM))),
        scratch_shapes=[pltpu.VMEM((batch, M_HEADS, M_DIM, M_DIM), F32),
                        pltpu.VMEM((batch, M_HEADS, 8, M_DIM), F32)],
        compiler_params=pltpu.CompilerParams(dimension_semantics=("arbitrary",),
                                             vmem_limit_bytes=VMEM_LIMIT_BYTES),
        name="prompt_mixer",
    )(sinks, r3(qkv), vt.reshape(batch, nblk, M_HEADS, M_DIM, BLOCK), r3(og), r3(cols), rows, r3(qa),
      r3(kx), r3(vx), r3(kx), r3(vx))
    return y.reshape(batch * seq, D_MODEL), pc, pn


def _sample_mlstm_kernel(bi_ref, bf_ref, qt_ref, kt_ref, vt_ref, ogt_ref, gi_ref, gf_ref, m0_ref, n0t_ref,
                         qr_ref, kr_ref, c_ref,
                         yt_ref, nt_out_ref, m_out_ref, c_out_ref, decay_ref):
    h = pl.program_id(0)
    nb = qr_ref.shape[0]
    qt, kt, vt = qt_ref[0], kt_ref[0], vt_ref[0]
    i_pre = gi_ref[0] + bi_ref[h]
    a = _log_sigmoid(gf_ref[0] + bf_ref[h]) + m0_ref[0]
    m_t = jnp.maximum(a, i_pre)
    w_inter = jnp.exp(a - m_t)
    w_in = jnp.exp(i_pre - m_t)
    scores = jnp.sum(qt * kt, axis=0, keepdims=True) * w_in
    n0t = n0t_ref[0]
    nq = jnp.sum(n0t * qt, axis=0, keepdims=True)

    rows = lax.broadcasted_iota(jnp.int32, (nb, M_DIM), 0)
    cols = lax.broadcasted_iota(jnp.int32, (M_DIM, nb), 1)
    vw_t = (vt * w_in).astype(BF16)
    k_rows = kr_ref[...]

    decay_ref[...] = jnp.broadcast_to(w_inter, (M_DIM, nb)).T

    def body(grp, cq_t):
        base = grp * SAMPLE_UNROLL
        for u in range(SAMPLE_UNROLL):
            b = base + u
            col = jnp.sum(c_ref[b, 0] * qr_ref[pl.ds(b, 1), :], axis=-1, keepdims=True)
            cq_t = jnp.where(cols == b, col, cq_t)
        outer = []
        for u in range(SAMPLE_UNROLL):
            k_only_b = jnp.where(rows == base + u, k_rows, 0.0).astype(BF16)
            outer.append(_dot(vw_t, k_only_b))
        for u in range(SAMPLE_UNROLL):
            b = base + u
            c_out_ref[b, 0] = decay_ref[pl.ds(b, 1), :] * c_ref[b, 0] + outer[u]
        return cq_t

    cq_t = lax.fori_loop(0, nb // SAMPLE_UNROLL, body, jnp.zeros((M_DIM, nb), F32))

    num = w_inter * cq_t + scores * vt
    den = w_inter * nq + scores
    hh = num / jnp.maximum(jnp.abs(den), jnp.exp(-m_t))
    hh = hh * lax.rsqrt(jnp.mean(hh * hh, axis=0, keepdims=True) + RMS_EPS)
    yt_ref[0] = hh * ogt_ref[0]
    nt_out_ref[0] = w_inter * n0t + w_in * kt
    m_out_ref[0] = m_t


def _sample_mlstm(b_i, b_f, qt, kt, vt, ogt, gi, gf, m0, n0t, q_rows, k_rows, c0):
    nb = q_rows.shape[0]
    head3 = lambda r: pl.BlockSpec((1, r, nb), lambda h: (h, 0, 0))
    smem = pl.BlockSpec(memory_space=pltpu.SMEM)
    out_shape = (
        jax.ShapeDtypeStruct((M_HEADS, M_DIM, nb), F32),
        jax.ShapeDtypeStruct((M_HEADS, M_DIM, nb), F32),
        jax.ShapeDtypeStruct((M_HEADS, 1, nb), F32),
        jax.ShapeDtypeStruct((nb, M_HEADS, M_DIM, M_DIM), F32),
    )
    c_spec = pl.BlockSpec((nb, 1, M_DIM, M_DIM), lambda h: (0, h, 0, 0))
    return pl.pallas_call(
        _sample_mlstm_kernel,
        out_shape=out_shape,
        grid=(M_HEADS,),
        in_specs=[smem, smem, head3(M_DIM), head3(M_DIM), head3(M_DIM), head3(M_DIM), head3(1), head3(1),
                  head3(1), head3(M_DIM),
                  pl.BlockSpec((nb, M_DIM), lambda h: (0, h)), pl.BlockSpec((nb, M_DIM), lambda h: (0, h)),
                  c_spec],
        out_specs=(head3(M_DIM), head3(M_DIM), head3(1), c_spec),
        scratch_shapes=[pltpu.VMEM((nb, M_DIM), F32)],
        compiler_params=pltpu.CompilerParams(dimension_semantics=("arbitrary",),
                                             vmem_limit_bytes=VMEM_LIMIT_BYTES),
        name="sample_mlstm",
    )(b_i, b_f, qt, kt, vt, ogt, gi, gf, m0, n0t, q_rows, k_rows, c0)


SAMPLE_TILE = 8
SAMPLE_UNROLL = 8


def _sample_swa_kernel(q2_ref, kc_ref, vc_ref, kn_ref, vn_ref, sink_ref, o_ref, ko_ref, vo_ref):
    sink = sink_ref[...] * LOG2E
    w = kc_ref.shape[1]
    tile = range(SAMPLE_TILE)
    q2 = [q2_ref[b] for b in tile]
    k_new = [kn_ref[b:b + 1, :] for b in tile]
    v_new = [vn_ref[b:b + 1, :] for b in tile]
    s_c = [_dot_nt(q2[b], kc_ref[b].astype(BF16)) for b in tile]
    s_n = [jnp.sum(q2[b].astype(F32) * k_new[b], axis=-1, keepdims=True) for b in tile]
    mx = [jnp.maximum(jnp.maximum(jnp.max(s_c[b], axis=-1, keepdims=True), s_n[b]), sink) for b in tile]
    p_c = [jnp.exp2(s_c[b] - mx[b]) for b in tile]
    p_n = [jnp.exp2(s_n[b] - mx[b]) for b in tile]
    denom = [jnp.sum(p_c[b], axis=-1, keepdims=True) + p_n[b] + jnp.exp2(sink - mx[b]) for b in tile]
    o = [_dot(p_c[b].astype(BF16), vc_ref[b].astype(BF16)) for b in tile]
    for b in tile:
        o_ref[b] = (o[b] + p_n[b] * v_new[b]) / denom[b]
        ko_ref[b, 0:w - 1, :] = kc_ref[b, 1:w, :]
        ko_ref[b, w - 1:w, :] = k_new[b]
        vo_ref[b, 0:w - 1, :] = vc_ref[b, 1:w, :]
        vo_ref[b, w - 1:w, :] = v_new[b]


def _sample_swa(q2, k_cache, v_cache, k_new, v_new, sink_col):
    nb, w, _ = k_cache.shape
    t3 = lambda a, c: pl.BlockSpec((SAMPLE_TILE, a, c), lambda i: (i, 0, 0))
    t2 = pl.BlockSpec((SAMPLE_TILE, KV_WIDTH), lambda i: (i, 0))
    out_shape = (
        jax.ShapeDtypeStruct((nb, A_HEADS, KV_WIDTH), F32),
        jax.ShapeDtypeStruct((nb, w, KV_WIDTH), F32),
        jax.ShapeDtypeStruct((nb, w, KV_WIDTH), F32),
    )
    return pl.pallas_call(
        _sample_swa_kernel,
        out_shape=out_shape,
        grid=(nb // SAMPLE_TILE,),
        in_specs=[t3(A_HEADS, KV_WIDTH), t3(w, KV_WIDTH), t3(w, KV_WIDTH), t2, t2,
                  pl.BlockSpec((A_HEADS, 1), lambda i: (0, 0))],
        out_specs=(t3(A_HEADS, KV_WIDTH), t3(w, KV_WIDTH), t3(w, KV_WIDTH)),
        compiler_params=pltpu.CompilerParams(dimension_semantics=("arbitrary",)),
        name="sample_swa",
    )(q2, k_cache, v_cache, k_new, v_new, sink_col)


def _prep_weights(ffn1_w_gate, ffn1_w_up, ffn1_w_down, w_in, w_out, ffn2_w_gate, ffn2_w_up, ffn2_w_down):
    cols = rows = lambda w: w.astype(BF16)

    m_end = 4 * M_WIDTH
    g_end = m_end + 2 * M_HEADS
    q_end = g_end + A_WIDTH
    wm = w_in[:, :m_end].astype(BF16)
    wqa = w_in[:, g_end:q_end].astype(BF16)
    gates = jnp.pad(w_in[:, m_end:g_end], ((0, 0), (0, GATE_PAD - 2 * M_HEADS)))
    wkvg = jnp.concatenate([w_in[:, q_end:], gates], axis=1).astype(BF16)
    return (cols(ffn1_w_gate), cols(ffn1_w_up), rows(ffn1_w_down), wm, wqa, wkvg, w_out.astype(BF16),
            cols(ffn2_w_gate), cols(ffn2_w_up), rows(ffn2_w_down))


def kernel(x_prompt, x_sample, cache_swa_k, cache_swa_v, state_mlstm_C, state_mlstm_n, state_mlstm_m,
           ffn1_norm, ffn1_w_gate, ffn1_w_up, ffn1_w_down, mix_norm, w_in, mlstm_b_i, mlstm_b_f,
           mlstm_out_norm, swa_q_norm, swa_k_norm, swa_sinks, w_out, ffn2_norm, ffn2_w_gate,
           ffn2_w_up, ffn2_w_down):
    depth = ffn1_norm.shape[0]
    assert depth == 1
    batch, seq, _ = x_prompt.shape
    nb = x_sample.shape[0]
    assert x_sample.shape[1] == 1 and seq % BLOCK == 0

    (wg1, wu1, wd1, wm, wqa, wkvg, wo, wg2, wu2, wd2) = _prep_weights(
        ffn1_w_gate[0], ffn1_w_up[0], ffn1_w_down[0], w_in[0], w_out[0],
        ffn2_w_gate[0], ffn2_w_up[0], ffn2_w_down[0])
    n1 = ffn1_norm[0].reshape(1, D_MODEL)
    n2 = mix_norm[0].reshape(1, D_MODEL)
    n3 = ffn2_norm[0].reshape(1, D_MODEL)
    qgain = jnp.tile(swa_q_norm[0], A_HEADS).reshape(1, A_WIDTH)
    kgain = jnp.tile(swa_k_norm[0], KV_HEADS).reshape(1, KV_WIDTH)
    ogain = mlstm_out_norm[0].reshape(1, M_WIDTH)
    front = functools.partial(_front, n1=n1, wg=wg1, wu=wu1, wd=wd1, n2=n2, wm=wm, wqa=wqa, wkvg=wkvg,
                              qgain=qgain, kgain=kgain, ogain=ogain)
    back = functools.partial(_back, wo=wo, nrm=n3, wg=wg2, wu=wu2, wd=wd2)
    b_i, b_f = mlstm_b_i[0], mlstm_b_f[0]
    sinks = swa_sinks[0]

    xp = x_prompt.reshape(batch * seq, D_MODEL)
    x1, qkv, og, qa, ka, va, gt, vt, kx, vx = front(xp, tm=512)
    bias8 = jnp.concatenate([b_i, b_f]).reshape(2 * M_HEADS, 1)
    tri = (jnp.arange(BLOCK)[:, None] <= jnp.arange(BLOCK)[None, :]).astype(F32)
    rows, cols, pm = _gate_prep(gt, bias8, tri, batch, seq)
    y, pc, pn = _prompt_mixer(sinks, qkv, vt, og, cols, rows, qa, kx, vx, batch, seq)
    yp = back(x1, y, tm=512).reshape(batch, seq, D_MODEL)
    w = min(WINDOW, seq)
    pk = ka.reshape(batch, seq, KV_WIDTH)[:, seq - w:].reshape(batch, w, KV_HEADS, A_DIM)
    pv = va.reshape(batch, seq, KV_WIDTH)[:, seq - w:].reshape(batch, w, KV_HEADS, A_DIM)
    pn = pn[:, :, 0, :]
    pm = pm[:, 0:M_HEADS, 0]

    xs = x_sample.reshape(nb, D_MODEL)
    x1s, qkvs, ogs, qas, kas, vas, gates_t = front(xs, tm=nb)[:7]
    qkv_f = qkvs.astype(F32)
    to_heads_t = lambda a: jnp.transpose(a.reshape(nb, M_HEADS, M_DIM), (1, 2, 0))
    q_rows = qkv_f[:, 0:M_WIDTH]
    k_rows = qkv_f[:, M_WIDTH:2 * M_WIDTH]
    ymt, nt, mt, sc = _sample_mlstm(
        b_i, b_f, to_heads_t(q_rows), to_heads_t(k_rows), to_heads_t(qkv_f[:, 2 * M_WIDTH:]), to_heads_t(ogs),
        gates_t[0:M_HEADS].reshape(M_HEADS, 1, nb), gates_t[M_HEADS:].reshape(M_HEADS, 1, nb),
        jnp.transpose(state_mlstm_m[0]).reshape(M_HEADS, 1, nb), jnp.transpose(state_mlstm_n[0], (1, 2, 0)),
        q_rows, k_rows, state_mlstm_C[0])
    y_m = jnp.transpose(ymt, (2, 0, 1)).reshape(nb, M_WIDTH)
    sn = jnp.transpose(nt, (2, 0, 1))
    sm = jnp.transpose(mt[:, 0, :])

    qa_h = qas.reshape(nb, A_HEADS, A_DIM)
    zeros = jnp.zeros_like(qa_h)
    in_lo = (jnp.arange(A_HEADS) // A_GROUP == 0)[None, :, None]
    q2 = jnp.concatenate([jnp.where(in_lo, qa_h, zeros), jnp.where(in_lo, zeros, qa_h)], axis=-1)
    kc = cache_swa_k[0].reshape(nb, -1, KV_WIDTH)
    vc = cache_swa_v[0].reshape(nb, -1, KV_WIDTH)
    o2, sk, sv = _sample_swa(q2, kc, vc, kas, vas, sinks.reshape(A_HEADS, 1))
    o2 = o2.reshape(nb, A_HEADS, KV_HEADS, A_DIM)
    y_a = jnp.where(in_lo, o2[:, :, 0, :], o2[:, :, 1, :]).reshape(nb, A_WIDTH)
    ys_in = jnp.concatenate([y_m, y_a], axis=-1).astype(BF16)
    ys = back(x1s, ys_in, tm=nb).reshape(nb, 1, D_MODEL)

    wb = kc.shape[1]
    return (yp, ys, pk[None], pv[None], pc[None], pn[None], pm[None],
            sk.reshape(1, nb, wb, KV_HEADS, A_DIM), sv.reshape(1, nb, wb, KV_HEADS, A_DIM),
            sc[None], sn[None], sm[None])
```

```python
import functools

import jax
import jax.numpy as jnp
from jax import lax
from jax.experimental import pallas as pl
from jax.experimental.pallas import tpu as pltpu

F32 = jnp.float32
BF16 = jnp.bfloat16

D_MODEL = 1024
D_FF = 2816
FF_CHUNK = 256
N_FF_CHUNKS = D_FF // FF_CHUNK
M_HEADS = 4
M_DIM = 128
M_WIDTH = M_HEADS * M_DIM
A_HEADS = 8
A_DIM = 64
A_WIDTH = A_HEADS * A_DIM
KV_HEADS = 2
KV_WIDTH = KV_HEADS * A_DIM
A_GROUP = A_HEADS // KV_HEADS
WINDOW = 128
BLOCK = 128
GATE_PAD = 128
RMS_EPS = 1e-6
FFN_RES_WEIGHT = 0.5
NEG_INF = float("-inf")
LOG2E = 1.4426950408889634
VMEM_LIMIT_BYTES = 56 * 1024 * 1024


def _dot(a, b):
    return jnp.dot(a, b, preferred_element_type=F32)


def _dot_nt(a, b):
    return lax.dot_general(a, b, (((1,), (1,)), ((), ())), preferred_element_type=F32)


def _rms_rows(x, gain):
    ms = jnp.mean(x * x, axis=-1, keepdims=True)
    return x * lax.rsqrt(ms + RMS_EPS) * gain


def _log_sigmoid(x):
    return jnp.minimum(x, 0.0) - jnp.log1p(jnp.exp(-jnp.abs(x)))


def _half_tile_mean_sq(x):
    in_lo = lax.broadcasted_iota(jnp.int32, (x.shape[0], 128), 1) < A_DIM
    out = []
    for c in range(x.shape[1] // 128):
        sq = x[:, c * 128:(c + 1) * 128]
        sq = sq * sq
        s_lo = jnp.sum(jnp.where(in_lo, sq, 0.0), axis=-1, keepdims=True)
        s_hi = jnp.sum(jnp.where(in_lo, 0.0, sq), axis=-1, keepdims=True)
        out.append(jnp.where(in_lo, s_lo, s_hi) * (1.0 / A_DIM))
    return out[0] if len(out) == 1 else jnp.concatenate(out, axis=1)


def _ffn_into(h_ref, wg_ref, wu_ref, wd_ref, acc_ref):
    for c in range(N_FF_CHUNKS):
        lo, hi = c * FF_CHUNK, (c + 1) * FF_CHUNK
        h = h_ref[...]
        g = _dot(h, wg_ref[:, lo:hi])
        u = _dot(h, wu_ref[:, lo:hi])
        a = (g * jax.nn.sigmoid(g) * u).astype(BF16)
        d = _dot(a, wd_ref[lo:hi, :])
        if c == 0:
            acc_ref[...] = d
        else:
            acc_ref[...] += d


def _front_kernel(x_ref, n1_ref, wg_ref, wu_ref, wd_ref, n2_ref, wm_ref, wqa_ref, wkvg_ref,
                  qgain_ref, kgain_ref, ogain_ref,
                  x1_ref, qkv_ref, og_ref, qa_ref, ka_ref, va_ref, gt_ref, vt_ref, kx_ref, vx_ref,
                  h_ref, acc_ref):
    x = x_ref[...]
    h_ref[...] = _rms_rows(x, n1_ref[...]).astype(BF16)
    _ffn_into(h_ref, wg_ref, wu_ref, wd_ref, acc_ref)
    x1 = x + FFN_RES_WEIGHT * acc_ref[...]
    x1_ref[...] = x1
    h_ref[...] = _rms_rows(x1, n2_ref[...]).astype(BF16)
    h = h_ref[...]

    qa = _dot(h, wqa_ref[...])
    kvg = _dot(h, wkvg_ref[...])
    v_m = _dot(h, wm_ref[:, 2 * M_WIDTH:3 * M_WIDTH])
    qkv_ref[:, 2 * M_WIDTH:3 * M_WIDTH] = v_m.astype(BF16)
    for c in range(vt_ref.shape[0]):
        for hd in range(M_HEADS):
            blk = v_m[c * BLOCK:(c + 1) * BLOCK, hd * M_DIM:(hd + 1) * M_DIM]
            vt_ref[c, hd] = blk.T.astype(BF16)
    q_scale = qgain_ref[...] * (A_DIM ** -0.5 * LOG2E)
    qa_ref[...] = (qa * lax.rsqrt(_half_tile_mean_sq(qa) + RMS_EPS) * q_scale).astype(BF16)
    ka = kvg[:, 0:KV_WIDTH]
    ka = ka * lax.rsqrt(_half_tile_mean_sq(ka) + RMS_EPS) * kgain_ref[...]
    va = kvg[:, KV_WIDTH:2 * KV_WIDTH]
    ka_ref[...] = ka
    va_ref[...] = va
    gt_ref[...] = kvg[:, 2 * KV_WIDTH:2 * KV_WIDTH + GATE_PAD].T[0:2 * M_HEADS, :]

    in_lo = lax.broadcasted_iota(jnp.int32, ka.shape, 1) < A_DIM
    for src, dst in ((ka, kx_ref), (va, vx_ref)):
        x0 = jnp.where(in_lo, src, 0.0)
        x1 = jnp.where(in_lo, 0.0, src)
        dst[:, 0:128] = x0.astype(BF16)
        dst[:, 128:256] = pltpu.roll(x0, A_DIM, 1).astype(BF16)
        dst[:, 256:384] = pltpu.roll(x1, A_DIM, 1).astype(BF16)
        dst[:, 384:512] = x1.astype(BF16)

    og_ref[...] = jax.nn.sigmoid(_dot(h, wm_ref[:, 3 * M_WIDTH:4 * M_WIDTH])) * ogain_ref[...]
    k_m = _dot(h, wm_ref[:, M_WIDTH:2 * M_WIDTH]) * (M_DIM ** -0.5)
    qkv_ref[:, M_WIDTH:2 * M_WIDTH] = k_m.astype(BF16)
    qkv_ref[:, 0:M_WIDTH] = _dot(h, wm_ref[:, 0:M_WIDTH]).astype(BF16)


def _const_spec(shape):
    nd = len(shape)
    return pl.BlockSpec(shape, lambda i: (0,) * nd, pipeline_mode=pl.Buffered(1))


def _front(x2d, n1, wg, wu, wd, n2, wm, wqa, wkvg, qgain, kgain, ogain, tm):
    n = x2d.shape[0]
    row = lambda w: pl.BlockSpec((tm, w), lambda i: (i, 0))
    out_shape = (
        jax.ShapeDtypeStruct((n, D_MODEL), F32),
        jax.ShapeDtypeStruct((n, 3 * M_WIDTH), BF16),
        jax.ShapeDtypeStruct((n, M_WIDTH), F32),
        jax.ShapeDtypeStruct((n, A_WIDTH), BF16),
        jax.ShapeDtypeStruct((n, KV_WIDTH), F32),
        jax.ShapeDtypeStruct((n, KV_WIDTH), F32),
        jax.ShapeDtypeStruct((2 * M_HEADS, n), F32),
        jax.ShapeDtypeStruct((n // BLOCK, M_HEADS, M_DIM, BLOCK), BF16),
        jax.ShapeDtypeStruct((n, 4 * KV_WIDTH), BF16),
        jax.ShapeDtypeStruct((n, 4 * KV_WIDTH), BF16),
    )
    nb_t = tm // BLOCK
    return pl.pallas_call(
        _front_kernel,
        out_shape=out_shape,
        grid=(n // tm,),
        in_specs=[row(D_MODEL), _const_spec(n1.shape), _const_spec(wg.shape), _const_spec(wu.shape),
                  _const_spec(wd.shape), _const_spec(n2.shape), _const_spec(wm.shape),
                  _const_spec(wqa.shape), _const_spec(wkvg.shape), _const_spec(qgain.shape),
                  _const_spec(kgain.shape), _const_spec(ogain.shape)],
        out_specs=(row(D_MODEL), row(3 * M_WIDTH), row(M_WIDTH), row(A_WIDTH), row(KV_WIDTH),
                   row(KV_WIDTH), pl.BlockSpec((2 * M_HEADS, tm), lambda i: (0, i)),
                   pl.BlockSpec((nb_t, M_HEADS, M_DIM, BLOCK), lambda i: (i, 0, 0, 0)),
                   row(4 * KV_WIDTH), row(4 * KV_WIDTH)),
        scratch_shapes=[pltpu.VMEM((tm, D_MODEL), BF16), pltpu.VMEM((tm, D_MODEL), F32)],
        compiler_params=pltpu.CompilerParams(dimension_semantics=("arbitrary",),
                                             vmem_limit_bytes=VMEM_LIMIT_BYTES),
        name="front",
    )(x2d, n1, wg, wu, wd, n2, wm, wqa, wkvg, qgain, kgain, ogain)


def _back_kernel(x1_ref, y_ref, wo_ref, n_ref, wg_ref, wu_ref, wd_ref, out_ref, h_ref, acc_ref):
    x2 = x1_ref[...] + _dot(y_ref[...], wo_ref[...])
    h_ref[...] = _rms_rows(x2, n_ref[...]).astype(BF16)
    _ffn_into(h_ref, wg_ref, wu_ref, wd_ref, acc_ref)
    out_ref[...] = x2 + FFN_RES_WEIGHT * acc_ref[...]


def _back(x1, y, wo, nrm, wg, wu, wd, tm):
    n = x1.shape[0]
    row = lambda w: pl.BlockSpec((tm, w), lambda i: (i, 0))
    return pl.pallas_call(
        _back_kernel,
        out_shape=jax.ShapeDtypeStruct((n, D_MODEL), F32),
        grid=(n // tm,),
        in_specs=[row(D_MODEL), row(D_MODEL), _const_spec(wo.shape), _const_spec(nrm.shape),
                  _const_spec(wg.shape), _const_spec(wu.shape), _const_spec(wd.shape)],
        out_specs=row(D_MODEL),
        scratch_shapes=[pltpu.VMEM((tm, D_MODEL), BF16), pltpu.VMEM((tm, D_MODEL), F32)],
        compiler_params=pltpu.CompilerParams(dimension_semantics=("arbitrary",),
                                             vmem_limit_bytes=VMEM_LIMIT_BYTES),
        name="back",
    )(x1, y, wo, nrm, wg, wu, wd)


def _gate_prep_kernel(gt_ref, bias_ref, tri_ref, rows_ref, cols_ref, m_ref):
    nblk = rows_ref.shape[1]
    sub = lax.broadcasted_iota(jnp.int32, (8, BLOCK), 0)
    lane = lax.broadcasted_iota(jnp.int32, (8, BLOCK), 1)
    is_head = sub < M_HEADS
    pad = jnp.zeros((BLOCK - 24, BLOCK), F32)
    m_prev = jnp.zeros((8, 1), F32)
    for j in range(nblk):
        pre = gt_ref[:, j * BLOCK:(j + 1) * BLOCK] + bias_ref[...]
        r = jnp.where(is_head, pre, _log_sigmoid(pre))
        cum = jnp.dot(r, tri_ref[...], preferred_element_type=F32, precision=lax.Precision.HIGHEST)
        bcum = pltpu.roll(cum, M_HEADS, 0)
        g = jnp.where(is_head, r - bcum, 0.0)
        bcum = jnp.where(is_head, bcum, 0.0)
        cm = g
        for shift in (1, 2, 4, 8, 16, 32, 64):
            cm = jnp.maximum(cm, jnp.where(lane >= shift, pltpu.roll(cm, shift, 1), NEG_INF))
        cm_last = jnp.max(cm, axis=-1, keepdims=True)
        b_last = jnp.sum(jnp.where(lane == BLOCK - 1, bcum, 0.0), axis=-1, keepdims=True)
        mx = jnp.maximum(m_prev, cm)
        mx_last = jnp.maximum(m_prev, cm_last)
        rows_ref[0, j, 0] = g * LOG2E
        rows_ref[0, j, 1] = jnp.exp(g - mx_last)
        rows_ref[0, j, 2] = jnp.broadcast_to(jnp.exp(m_prev - mx_last), (8, BLOCK))
        col_src = jnp.concatenate([mx * -LOG2E, jnp.exp(m_prev - mx), jnp.exp(-(bcum + mx)), pad], axis=0)
        cols_ref[j * BLOCK:(j + 1) * BLOCK, :] = col_src.T
        m_prev = b_last + mx_last
    m_ref[0] = jnp.broadcast_to(m_prev, (8, BLOCK))


def _gate_prep(gt, bias8, tri, batch, seq):
    nblk = seq // BLOCK
    out_shape = (
        jax.ShapeDtypeStruct((batch, nblk, 3, 8, BLOCK), F32),
        jax.ShapeDtypeStruct((batch * seq, 128), F32),
        jax.ShapeDtypeStruct((batch, 8, BLOCK), F32),
    )
    return pl.pallas_call(
        _gate_prep_kernel,
        out_shape=out_shape,
        grid=(batch,),
        in_specs=[pl.BlockSpec((2 * M_HEADS, seq), lambda b: (0, b)),
                  pl.BlockSpec(bias8.shape, lambda b: (0, 0)), pl.BlockSpec(tri.shape, lambda b: (0, 0))],
        out_specs=(pl.BlockSpec((1, nblk, 3, 8, BLOCK), lambda b: (b, 0, 0, 0, 0)),
                   pl.BlockSpec((seq, 128), lambda b: (b, 0)),
                   pl.BlockSpec((1, 8, BLOCK), lambda b: (b, 0, 0))),
        compiler_params=pltpu.CompilerParams(dimension_semantics=("arbitrary",)),
        name="gate_prep",
    )(gt, bias8, tri)


def _prompt_mixer_kernel(sinks_ref, qkv_ref, vt_ref, og_ref, cols_ref, rows_ref, qa_ref,
                         kx_ref, vx_ref, kxp_ref, vxp_ref,
                         y_ref, c_out_ref, n_out_ref,
                         c_ref, n_ref):
    j = pl.program_id(0)
    batch = qkv_ref.shape[0]

    @pl.when(j == 0)
    def _():
        c_ref[...] = jnp.zeros_like(c_ref)
        n_ref[...] = jnp.zeros_like(n_ref)

    rows = lax.broadcasted_iota(jnp.int32, (BLOCK, BLOCK), 0)
    cols = lax.broadcasted_iota(jnp.int32, (BLOCK, BLOCK), 1)
    causal = cols <= rows
    lane_lo = lax.broadcasted_iota(jnp.int32, (2 * BLOCK, KV_WIDTH), 1) < A_DIM
    ones_m = jnp.ones((BLOCK, M_DIM), BF16)
    ones_half = (jnp.where(lane_lo, 1.0, 0.0).astype(BF16), jnp.where(lane_lo, 0.0, 1.0).astype(BF16))
    qi = lax.broadcasted_iota(jnp.int32, (2 * BLOCK, 2 * BLOCK), 0) % BLOCK
    kc = lax.broadcasted_iota(jnp.int32, (2 * BLOCK, 2 * BLOCK), 1)
    first_valid = jnp.where(j == 0, BLOCK, 0)
    valid = (kc >= qi) & (kc <= qi + WINDOW) & (kc >= first_valid)
    top_rows = lax.broadcasted_iota(jnp.int32, (2 * BLOCK, 1), 0) < BLOCK

    heads = range(M_HEADS)
    groups = [(kv, parity) for kv in range(KV_HEADS) for parity in range(2)]
    m_lo = lambda h: h * M_DIM

    for b in range(batch):
        col_b = cols_ref[b]
        q = [qkv_ref[b, :, m_lo(h):m_lo(h) + M_DIM] for h in heads]
        k = [qkv_ref[b, :, M_WIDTH + m_lo(h):M_WIDTH + m_lo(h) + M_DIM] for h in heads]
        v = [qkv_ref[b, :, 2 * M_WIDTH + m_lo(h):2 * M_WIDTH + m_lo(h) + M_DIM] for h in heads]
        g_row = [rows_ref[b, 0, 0][h:h + 1, :] for h in heads]
        wend_row = [rows_ref[b, 0, 1][h:h + 1, :] for h in heads]
        decay = [rows_ref[b, 0, 2][h:h + 1, :] for h in heads]
        c_prev = [c_ref[b, h] for h in heads]
        n_prev = [n_ref[b, h] for h in heads]

        qkc, upd = [], []
        for h in heads:
            n_rep = jnp.broadcast_to(n_prev[h][0:1, :], (BLOCK, M_DIM)).astype(BF16)
            rhs = jnp.concatenate([k[h], c_prev[h].astype(BF16), n_rep], axis=0)
            qkc.append(_dot_nt(q[h], rhs))
        for h in heads:
            vw_t = (vt_ref[b, 0, h].astype(F32) * wend_row[h]).astype(BF16)
            w_rep = jnp.broadcast_to(wend_row[h], (16, BLOCK)).astype(BF16)
            upd.append(_dot(jnp.concatenate([vw_t, w_rep], axis=0), k[h]))
        q2 = [jnp.concatenate([qa_ref[b, :, (2 * kv) * 128:(2 * kv + 1) * 128],
                               qa_ref[b, :, (2 * kv + 1) * 128:(2 * kv + 2) * 128]], axis=0)
              for kv in range(KV_HEADS)]
        sc = []
        for kv, parity in groups:
            var = (2 * kv + parity) * KV_WIDTH
            k_band = jnp.concatenate([kxp_ref[b, :, var:var + KV_WIDTH], kx_ref[b, :, var:var + KV_WIDTH]], axis=0)
            sc.append(_dot_nt(q2[kv], k_band))

        s = []
        for h in heads:
            d = jnp.where(causal, jnp.exp2(col_b[:, h:h + 1] + g_row[h]), 0.0)
            s.append((qkc[h][:, 0:BLOCK] * d).astype(BF16))
            c_ref[b, h] = decay[h] * c_prev[h] + upd[h][0:M_DIM]
            n_ref[b, h] = decay[h] * n_prev[h] + upd[h][M_DIM:M_DIM + 8]
        pr, e_sink = [], []
        for gi, (kv, parity) in enumerate(groups):
            sink = jnp.where(top_rows, sinks_ref[4 * kv + parity], sinks_ref[4 * kv + 2 + parity]) * LOG2E
            scm = jnp.where(valid, sc[gi], NEG_INF)
            mx = jnp.maximum(jnp.max(scm, axis=-1, keepdims=True), sink)
            pr.append(jnp.exp2(scm - mx).astype(BF16))
            e_sink.append(jnp.exp2(sink - mx))

        sv = [_dot(s[h], jnp.concatenate([v[h], ones_m], axis=1)) for h in heads]
        pv = []
        for gi, (kv, parity) in enumerate(groups):
            var = (2 * kv + parity) * KV_WIDTH
            v_band = jnp.concatenate([vxp_ref[b, :, var:var + KV_WIDTH], vx_ref[b, :, var:var + KV_WIDTH]], axis=0)
            pv.append(_dot(pr[gi], jnp.concatenate([v_band, ones_half[parity]], axis=1)))

        for h in heads:
            lo, hi = m_lo(h), m_lo(h) + M_DIM
            wi_col = col_b[:, 8 + h:9 + h]
            num = wi_col * qkc[h][:, BLOCK:2 * BLOCK] + sv[h][:, 0:M_DIM]
            den = wi_col * qkc[h][:, 2 * BLOCK:3 * BLOCK] + sv[h][:, M_DIM:2 * M_DIM]
            hh = num / jnp.maximum(jnp.abs(den), col_b[:, 16 + h:17 + h])
            hh = hh * lax.rsqrt(jnp.mean(hh * hh, axis=-1, keepdims=True) + RMS_EPS)
            y_ref[b, :, lo:hi] = (hh * og_ref[b, :, lo:hi]).astype(BF16)
        for kv in range(KV_HEADS):
            acc = pv[2 * kv] + pv[2 * kv + 1]
            denom = acc[:, KV_WIDTH:] + jnp.where(lane_lo, e_sink[2 * kv], e_sink[2 * kv + 1])
            ya = (acc[:, 0:KV_WIDTH] / denom).astype(BF16)
            p0, p1 = 2 * kv, 2 * kv + 1
            y_ref[b, :, M_WIDTH + p0 * 128:M_WIDTH + (p0 + 1) * 128] = ya[0:BLOCK]
            y_ref[b, :, M_WIDTH + p1 * 128:M_WIDTH + (p1 + 1) * 128] = ya[BLOCK:2 * BLOCK]

    @pl.when(j == pl.num_programs(0) - 1)
    def _():
        c_out_ref[...] = c_ref[...]
        n_out_ref[...] = n_ref[...]


def _prompt_mixer(sinks, qkv, vt, og, cols, rows, qa, kx, vx, batch, seq):
    nblk = seq // BLOCK
    r3 = lambda a: a.reshape(batch, seq, a.shape[-1])
    cur = lambda w: pl.BlockSpec((batch, BLOCK, w), lambda j: (0, j, 0))
    prev = lambda w: pl.BlockSpec((batch, BLOCK, w), lambda j: (0, jnp.maximum(j - 1, 0), 0))
    state = lambda shape: pl.BlockSpec((batch,) + shape, lambda j: (0,) * (len(shape) + 1))
    out_shape = (
        jax.ShapeDtypeStruct((batch, seq, D_MODEL), BF16),
        jax.ShapeDtypeStruct((batch, M_HEADS, M_DIM, M_DIM), F32),
        jax.ShapeDtypeStruct((batch, M_HEADS, 8, M_DIM), F32),
    )
    y, pc, pn = pl.pallas_call(
        _prompt_mixer_kernel,
        out_shape=out_shape,
        grid=(nblk,),
        in_specs=[pl.BlockSpec(memory_space=pltpu.SMEM),
                  cur(3 * M_WIDTH),
                  pl.BlockSpec((batch, 1, M_HEADS, M_DIM, BLOCK), lambda j: (0, j, 0, 0, 0)),
                  cur(M_WIDTH), cur(128),
                  pl.BlockSpec((batch, 1, 3, 8, BLOCK), lambda j: (0, j, 0, 0, 0)),
                  cur(A_WIDTH), cur(4 * KV_WIDTH), cur(4 * KV_WIDTH), prev(4 * KV_WIDTH), prev(4 * KV_WIDTH)],
        out_specs=(cur(D_MODEL), state((M_HEADS, M_DIM, M_DIM)), state((M_HEADS, 8, M_DIM))),
        scratch_shapes=[pltpu.VMEM((batch, M_HEADS, M_DIM, M_DIM), F32),
                        pltpu.VMEM((batch, M_HEADS, 8, M_DIM), F32)],
        compiler_params=pltpu.CompilerParams(dimension_semantics=("arbitrary",),
                                             vmem_limit_bytes=VMEM_LIMIT_BYTES),
        name="prompt_mixer",
    )(sinks, r3(qkv), vt.reshape(batch, nblk, M_HEADS, M_DIM, BLOCK), r3(og), r3(cols), rows, r3(qa),
      r3(kx), r3(vx), r3(kx), r3(vx))
    return y.reshape(batch * seq, D_MODEL), pc, pn


def _sample_mlstm_kernel(bi_ref, bf_ref, qt_ref, kt_ref, vt_ref, ogt_ref, gi_ref, gf_ref, m0_ref, n0t_ref,
                         qr_ref, kr_ref, c_ref,
                         yt_ref, nt_out_ref, m_out_ref, c_out_ref, decay_ref):
    h = pl.program_id(0)
    nb = qr_ref.shape[0]
    qt, kt, vt = qt_ref[0], kt_ref[0], vt_ref[0]
    i_pre = gi_ref[0] + bi_ref[h]
    a = _log_sigmoid(gf_ref[0] + bf_ref[h]) + m0_ref[0]
    m_t = jnp.maximum(a, i_pre)
    w_inter = jnp.exp(a - m_t)
    w_in = jnp.exp(i_pre - m_t)
    scores = jnp.sum(qt * kt, axis=0, keepdims=True) * w_in
    n0t = n0t_ref[0]
    nq = jnp.sum(n0t * qt, axis=0, keepdims=True)

    rows = lax.broadcasted_iota(jnp.int32, (nb, M_DIM), 0)
    cols = lax.broadcasted_iota(jnp.int32, (M_DIM, nb), 1)
    vw_t = (vt * w_in).astype(BF16)
    k_rows = kr_ref[...]

    decay_ref[...] = jnp.broadcast_to(w_inter, (M_DIM, nb)).T

    def body(grp, cq_t):
        base = grp * SAMPLE_UNROLL
        for u in range(SAMPLE_UNROLL):
            b = base + u
            col = jnp.sum(c_ref[b, 0] * qr_ref[pl.ds(b, 1), :], axis=-1, keepdims=True)
            cq_t = jnp.where(cols == b, col, cq_t)
        outer = []
        for u in range(SAMPLE_UNROLL):
            k_only_b = jnp.where(rows == base + u, k_rows, 0.0).astype(BF16)
            outer.append(_dot(vw_t, k_only_b))
        for u in range(SAMPLE_UNROLL):
            b = base + u
            c_out_ref[b, 0] = decay_ref[pl.ds(b, 1), :] * c_ref[b, 0] + outer[u]
        return cq_t

    cq_t = lax.fori_loop(0, nb // SAMPLE_UNROLL, body, jnp.zeros((M_DIM, nb), F32))

    num = w_inter * cq_t + scores * vt
    den = w_inter * nq + scores
    hh = num / jnp.maximum(jnp.abs(den), jnp.exp(-m_t))
    hh = hh * lax.rsqrt(jnp.mean(hh * hh, axis=0, keepdims=True) + RMS_EPS)
    yt_ref[0] = hh * ogt_ref[0]
    nt_out_ref[0] = w_inter * n0t + w_in * kt
    m_out_ref[0] = m_t


def _sample_mlstm(b_i, b_f, qt, kt, vt, ogt, gi, gf, m0, n0t, q_rows, k_rows, c0):
    nb = q_rows.shape[0]
    head3 = lambda r: pl.BlockSpec((1, r, nb), lambda h: (h, 0, 0))
    smem = pl.BlockSpec(memory_space=pltpu.SMEM)
    out_shape = (
        jax.ShapeDtypeStruct((M_HEADS, M_DIM, nb), F32),
        jax.ShapeDtypeStruct((M_HEADS, M_DIM, nb), F32),
        jax.ShapeDtypeStruct((M_HEADS, 1, nb), F32),
        jax.ShapeDtypeStruct((nb, M_HEADS, M_DIM, M_DIM), F32),
    )
    c_spec = pl.BlockSpec((nb, 1, M_DIM, M_DIM), lambda h: (0, h, 0, 0))
    return pl.pallas_call(
        _sample_mlstm_kernel,
        out_shape=out_shape,
        grid=(M_HEADS,),
        in_specs=[smem, smem, head3(M_DIM), head3(M_DIM), head3(M_DIM), head3(M_DIM), head3(1), head3(1),
                  head3(1), head3(M_DIM),
                  pl.BlockSpec((nb, M_DIM), lambda h: (0, h)), pl.BlockSpec((nb, M_DIM), lambda h: (0, h)),
                  c_spec],
        out_specs=(head3(M_DIM), head3(M_DIM), head3(1), c_spec),
        scratch_shapes=[pltpu.VMEM((nb, M_DIM), F32)],
        compiler_params=pltpu.CompilerParams(dimension_semantics=("arbitrary",),
                                             vmem_limit_bytes=VMEM_LIMIT_BYTES),
        name="sample_mlstm",
    )(b_i, b_f, qt, kt, vt, ogt, gi, gf, m0, n0t, q_rows, k_rows, c0)


SAMPLE_TILE = 16
SAMPLE_UNROLL = 8


def _sample_swa_kernel(q2_ref, kc_ref, vc_ref, kn_ref, vn_ref, sink_ref, o_ref, ko_ref, vo_ref):
    sink = sink_ref[...] * LOG2E
    w = kc_ref.shape[1]
    tile = range(SAMPLE_TILE)
    q2 = [q2_ref[b] for b in tile]
    k_new = [kn_ref[b:b + 1, :] for b in tile]
    v_new = [vn_ref[b:b + 1, :] for b in tile]
    s_c = [_dot_nt(q2[b], kc_ref[b].astype(BF16)) for b in tile]
    s_n = [jnp.sum(q2[b].astype(F32) * k_new[b], axis=-1, keepdims=True) for b in tile]
    mx = [jnp.maximum(jnp.maximum(jnp.max(s_c[b], axis=-1, keepdims=True), s_n[b]), sink) for b in tile]
    p_c = [jnp.exp2(s_c[b] - mx[b]) for b in tile]
    p_n = [jnp.exp2(s_n[b] - mx[b]) for b in tile]
    denom = [jnp.sum(p_c[b], axis=-1, keepdims=True) + p_n[b] + jnp.exp2(sink - mx[b]) for b in tile]
    o = [_dot(p_c[b].astype(BF16), vc_ref[b].astype(BF16)) for b in tile]
    for b in tile:
        o_ref[b] = (o[b] + p_n[b] * v_new[b]) / denom[b]
        ko_ref[b, 0:w - 1, :] = kc_ref[b, 1:w, :]
        ko_ref[b, w - 1:w, :] = k_new[b]
        vo_ref[b, 0:w - 1, :] = vc_ref[b, 1:w, :]
        vo_ref[b, w - 1:w, :] = v_new[b]


def _sample_swa(q2, k_cache, v_cache, k_new, v_new, sink_col):
    nb, w, _ = k_cache.shape
    t3 = lambda a, c: pl.BlockSpec((SAMPLE_TILE, a, c), lambda i: (i, 0, 0))
    t2 = pl.BlockSpec((SAMPLE_TILE, KV_WIDTH), lambda i: (i, 0))
    out_shape = (
        jax.ShapeDtypeStruct((nb, A_HEADS, KV_WIDTH), F32),
        jax.ShapeDtypeStruct((nb, w, KV_WIDTH), F32),
        jax.ShapeDtypeStruct((nb, w, KV_WIDTH), F32),
    )
    return pl.pallas_call(
        _sample_swa_kernel,
        out_shape=out_shape,
        grid=(nb // SAMPLE_TILE,),
        in_specs=[t3(A_HEADS, KV_WIDTH), t3(w, KV_WIDTH), t3(w, KV_WIDTH), t2, t2,
                  pl.BlockSpec((A_HEADS, 1), lambda i: (0, 0))],
        out_specs=(t3(A_HEADS, KV_WIDTH), t3(w, KV_WIDTH), t3(w, KV_WIDTH)),
        compiler_params=pltpu.CompilerParams(dimension_semantics=("arbitrary",)),
        name="sample_swa",
    )(q2, k_cache, v_cache, k_new, v_new, sink_col)


def _prep_weights(ffn1_w_gate, ffn1_w_up, ffn1_w_down, w_in, w_out, ffn2_w_gate, ffn2_w_up, ffn2_w_down):
    cols = rows = lambda w: w.astype(BF16)

    m_end = 4 * M_WIDTH
    g_end = m_end + 2 * M_HEADS
    q_end = g_end + A_WIDTH
    wm = w_in[:, :m_end].astype(BF16)
    wqa = w_in[:, g_end:q_end].astype(BF16)
    gates = jnp.pad(w_in[:, m_end:g_end], ((0, 0), (0, GATE_PAD - 2 * M_HEADS)))
    wkvg = jnp.concatenate([w_in[:, q_end:], gates], axis=1).astype(BF16)
    return (cols(ffn1_w_gate), cols(ffn1_w_up), rows(ffn1_w_down), wm, wqa, wkvg, w_out.astype(BF16),
            cols(ffn2_w_gate), cols(ffn2_w_up), rows(ffn2_w_down))


def kernel(x_prompt, x_sample, cache_swa_k, cache_swa_v, state_mlstm_C, state_mlstm_n, state_mlstm_m,
           ffn1_norm, ffn1_w_gate, ffn1_w_up, ffn1_w_down, mix_norm, w_in, mlstm_b_i, mlstm_b_f,
           mlstm_out_norm, swa_q_norm, swa_k_norm, swa_sinks, w_out, ffn2_norm, ffn2_w_gate,
           ffn2_w_up, ffn2_w_down):
    depth = ffn1_norm.shape[0]
    assert depth == 1
    batch, seq, _ = x_prompt.shape
    nb = x_sample.shape[0]
    assert x_sample.shape[1] == 1 and seq % BLOCK == 0

    (wg1, wu1, wd1, wm, wqa, wkvg, wo, wg2, wu2, wd2) = _prep_weights(
        ffn1_w_gate[0], ffn1_w_up[0], ffn1_w_down[0], w_in[0], w_out[0],
        ffn2_w_gate[0], ffn2_w_up[0], ffn2_w_down[0])
    n1 = ffn1_norm[0].reshape(1, D_MODEL)
    n2 = mix_norm[0].reshape(1, D_MODEL)
    n3 = ffn2_norm[0].reshape(1, D_MODEL)
    qgain = jnp.tile(swa_q_norm[0], A_HEADS).reshape(1, A_WIDTH)
    kgain = jnp.tile(swa_k_norm[0], KV_HEADS).reshape(1, KV_WIDTH)
    ogain = mlstm_out_norm[0].reshape(1, M_WIDTH)
    front = functools.partial(_front, n1=n1, wg=wg1, wu=wu1, wd=wd1, n2=n2, wm=wm, wqa=wqa, wkvg=wkvg,
                              qgain=qgain, kgain=kgain, ogain=ogain)
    back = functools.partial(_back, wo=wo, nrm=n3, wg=wg2, wu=wu2, wd=wd2)
    b_i, b_f = mlstm_b_i[0], mlstm_b_f[0]
    sinks = swa_sinks[0]

    xp = x_prompt.reshape(batch * seq, D_MODEL)
    x1, qkv, og, qa, ka, va, gt, vt, kx, vx = front(xp, tm=512)
    bias8 = jnp.concatenate([b_i, b_f]).reshape(2 * M_HEADS, 1)
    tri = (jnp.arange(BLOCK)[:, None] <= jnp.arange(BLOCK)[None, :]).astype(F32)
    rows, cols, pm = _gate_prep(gt, bias8, tri, batch, seq)
    y, pc, pn = _prompt_mixer(sinks, qkv, vt, og, cols, rows, qa, kx, vx, batch, seq)
    yp = back(x1, y, tm=1024).reshape(batch, seq, D_MODEL)
    w = min(WINDOW, seq)
    pk = ka.reshape(batch, seq, KV_WIDTH)[:, seq - w:].reshape(batch, w, KV_HEADS, A_DIM)
    pv = va.reshape(batch, seq, KV_WIDTH)[:, seq - w:].reshape(batch, w, KV_HEADS, A_DIM)
    pn = pn[:, :, 0, :]
    pm = pm[:, 0:M_HEADS, 0]

    xs = x_sample.reshape(nb, D_MODEL)
    x1s, qkvs, ogs, qas, kas, vas, gates_t = front(xs, tm=nb)[:7]
    qkv_f = qkvs.astype(F32)
    to_heads_t = lambda a: jnp.transpose(a.reshape(nb, M_HEADS, M_DIM), (1, 2, 0))
    q_rows = qkv_f[:, 0:M_WIDTH]
    k_rows = qkv_f[:, M_WIDTH:2 * M_WIDTH]
    ymt, nt, mt, sc = _sample_mlstm(
        b_i, b_f, to_heads_t(q_rows), to_heads_t(k_rows), to_heads_t(qkv_f[:, 2 * M_WIDTH:]), to_heads_t(ogs),
        gates_t[0:M_HEADS].reshape(M_HEADS, 1, nb), gates_t[M_HEADS:].reshape(M_HEADS, 1, nb),
        jnp.transpose(state_mlstm_m[0]).reshape(M_HEADS, 1, nb), jnp.transpose(state_mlstm_n[0], (1, 2, 0)),
        q_rows, k_rows, state_mlstm_C[0])
    y_m = jnp.transpose(ymt, (2, 0, 1)).reshape(nb, M_WIDTH)
    sn = jnp.transpose(nt, (2, 0, 1))
    sm = jnp.transpose(mt[:, 0, :])

    qa_h = qas.reshape(nb, A_HEADS, A_DIM)
    zeros = jnp.zeros_like(qa_h)
    in_lo = (jnp.arange(A_HEADS) // A_GROUP == 0)[None, :, None]
    q2 = jnp.concatenate([jnp.where(in_lo, qa_h, zeros), jnp.where(in_lo, zeros, qa_h)], axis=-1)
    kc = cache_swa_k[0].reshape(nb, -1, KV_WIDTH)
    vc = cache_swa_v[0].reshape(nb, -1, KV_WIDTH)
    o2, sk, sv = _sample_swa(q2, kc, vc, kas, vas, sinks.reshape(A_HEADS, 1))
    o2 = o2.reshape(nb, A_HEADS, KV_HEADS, A_DIM)
    y_a = jnp.where(in_lo, o2[:, :, 0, :], o2[:, :, 1, :]).reshape(nb, A_WIDTH)
    ys_in = jnp.concatenate([y_m, y_a], axis=-1).astype(BF16)
    ys = back(x1s, ys_in, tm=nb).reshape(nb, 1, D_MODEL)

    wb = kc.shape[1]
    return (yp, ys, pk[None], pv[None], pc[None], pn[None], pm[None],
            sk.reshape(1, nb, wb, KV_HEADS, A_DIM), sv.reshape(1, nb, wb, KV_HEADS, A_DIM),
            sc[None], sn[None], sm[None])
```

```python
import functools

import jax
import jax.numpy as jnp
from jax import lax
from jax.experimental import pallas as pl
from jax.experimental.pallas import tpu as pltpu

F32 = jnp.float32
BF16 = jnp.bfloat16

D_MODEL = 1024
D_FF = 2816
FF_CHUNK = 256
N_FF_CHUNKS = D_FF // FF_CHUNK
M_HEADS = 4
M_DIM = 128
M_WIDTH = M_HEADS * M_DIM
A_HEADS = 8
A_DIM = 64
A_WIDTH = A_HEADS * A_DIM
KV_HEADS = 2
KV_WIDTH = KV_HEADS * A_DIM
A_GROUP = A_HEADS // KV_HEADS
WINDOW = 128
BLOCK = 128
GATE_PAD = 128
RMS_EPS = 1e-6
FFN_RES_WEIGHT = 0.5
NEG_INF = float("-inf")
LOG2E = 1.4426950408889634
VMEM_LIMIT_BYTES = 56 * 1024 * 1024


def _dot(a, b):
    return jnp.dot(a, b, preferred_element_type=F32)


def _dot_nt(a, b):
    return lax.dot_general(a, b, (((1,), (1,)), ((), ())), preferred_element_type=F32)


def _rms_rows(x, gain):
    ms = jnp.mean(x * x, axis=-1, keepdims=True)
    return x * lax.rsqrt(ms + RMS_EPS) * gain


def _log_sigmoid(x):
    return jnp.minimum(x, 0.0) - jnp.log1p(jnp.exp(-jnp.abs(x)))


def _half_tile_mean_sq(x):
    in_lo = lax.broadcasted_iota(jnp.int32, (x.shape[0], 128), 1) < A_DIM
    out = []
    for c in range(x.shape[1] // 128):
        sq = x[:, c * 128:(c + 1) * 128]
        sq = sq * sq
        s_lo = jnp.sum(jnp.where(in_lo, sq, 0.0), axis=-1, keepdims=True)
        s_hi = jnp.sum(jnp.where(in_lo, 0.0, sq), axis=-1, keepdims=True)
        out.append(jnp.where(in_lo, s_lo, s_hi) * (1.0 / A_DIM))
    return out[0] if len(out) == 1 else jnp.concatenate(out, axis=1)


def _ffn_into(h_ref, wg_ref, wu_ref, wd_ref, acc_ref):
    for c in range(N_FF_CHUNKS):
        lo, hi = c * FF_CHUNK, (c + 1) * FF_CHUNK
        h = h_ref[...]
        g = _dot(h, wg_ref[:, lo:hi])
        u = _dot(h, wu_ref[:, lo:hi])
        a = (g * jax.nn.sigmoid(g) * u).astype(BF16)
        d = _dot(a, wd_ref[lo:hi, :])
        if c == 0:
            acc_ref[...] = d
        else:
            acc_ref[...] += d


N_FRONT_PARAMS = 11
N_FRONT_OUTS = 10


def _front_tile(x_ref, params, outs, h_ref, acc_ref):
    (n1_ref, wg_ref, wu_ref, wd_ref, n2_ref, wm_ref, wqa_ref, wkvg_ref, qgain_ref, kgain_ref, ogain_ref) = params
    (x1_ref, qkv_ref, og_ref, qa_ref, ka_ref, va_ref, gt_ref, vt_ref, kx_ref, vx_ref) = outs
    x = x_ref[...]
    h_ref[...] = _rms_rows(x, n1_ref[...]).astype(BF16)
    _ffn_into(h_ref, wg_ref, wu_ref, wd_ref, acc_ref)
    x1 = x + FFN_RES_WEIGHT * acc_ref[...]
    x1_ref[...] = x1
    h_ref[...] = _rms_rows(x1, n2_ref[...]).astype(BF16)
    h = h_ref[...]

    qa = _dot(h, wqa_ref[...])
    kvg = _dot(h, wkvg_ref[...])
    v_m = _dot(h, wm_ref[:, 2 * M_WIDTH:3 * M_WIDTH])
    qkv_ref[:, 2 * M_WIDTH:3 * M_WIDTH] = v_m.astype(BF16)
    for c in range(vt_ref.shape[0]):
        for hd in range(M_HEADS):
            blk = v_m[c * BLOCK:(c + 1) * BLOCK, hd * M_DIM:(hd + 1) * M_DIM]
            vt_ref[c, hd] = blk.T.astype(BF16)
    q_scale = qgain_ref[...] * (A_DIM ** -0.5 * LOG2E)
    qa_ref[...] = (qa * lax.rsqrt(_half_tile_mean_sq(qa) + RMS_EPS) * q_scale).astype(BF16)
    ka = kvg[:, 0:KV_WIDTH]
    ka = ka * lax.rsqrt(_half_tile_mean_sq(ka) + RMS_EPS) * kgain_ref[...]
    va = kvg[:, KV_WIDTH:2 * KV_WIDTH]
    ka_ref[...] = ka
    va_ref[...] = va
    gt_ref[...] = kvg[:, 2 * KV_WIDTH:2 * KV_WIDTH + GATE_PAD].T[0:2 * M_HEADS, :]

    in_lo = lax.broadcasted_iota(jnp.int32, ka.shape, 1) < A_DIM
    for src, dst in ((ka, kx_ref), (va, vx_ref)):
        x0 = jnp.where(in_lo, src, 0.0)
        x1 = jnp.where(in_lo, 0.0, src)
        dst[:, 0:128] = x0.astype(BF16)
        dst[:, 128:256] = pltpu.roll(x0, A_DIM, 1).astype(BF16)
        dst[:, 256:384] = pltpu.roll(x1, A_DIM, 1).astype(BF16)
        dst[:, 384:512] = x1.astype(BF16)

    og_ref[...] = jax.nn.sigmoid(_dot(h, wm_ref[:, 3 * M_WIDTH:4 * M_WIDTH])) * ogain_ref[...]
    k_m = _dot(h, wm_ref[:, M_WIDTH:2 * M_WIDTH]) * (M_DIM ** -0.5)
    qkv_ref[:, M_WIDTH:2 * M_WIDTH] = k_m.astype(BF16)
    qkv_ref[:, 0:M_WIDTH] = _dot(h, wm_ref[:, 0:M_WIDTH]).astype(BF16)


def _front_kernel(*refs, n_tiles):
    x_ref, xs_ref = refs[0:2]
    params = refs[2:2 + N_FRONT_PARAMS]
    outs_p = refs[2 + N_FRONT_PARAMS:2 + N_FRONT_PARAMS + N_FRONT_OUTS]
    outs_s = refs[2 + N_FRONT_PARAMS + N_FRONT_OUTS:2 + N_FRONT_PARAMS + 2 * N_FRONT_OUTS]
    h_ref, acc_ref = refs[2 + N_FRONT_PARAMS + 2 * N_FRONT_OUTS:]
    i = pl.program_id(0)

    @pl.when(i < n_tiles)
    def _():
        _front_tile(x_ref, params, outs_p, h_ref, acc_ref)

    @pl.when(i == n_tiles)
    def _():
        ns = xs_ref.shape[0]
        _front_tile(xs_ref, params, outs_s, h_ref.at[0:ns], acc_ref.at[0:ns])


def _const_spec(shape):
    nd = len(shape)
    return pl.BlockSpec(shape, lambda i: (0,) * nd, pipeline_mode=pl.Buffered(1))


def _whole_spec(shape):
    nd = len(shape)
    return pl.BlockSpec(shape, lambda i: (0,) * nd)


def _front_out_shapes(n):
    return (
        jax.ShapeDtypeStruct((n, D_MODEL), F32),
        jax.ShapeDtypeStruct((n, 3 * M_WIDTH), BF16),
        jax.ShapeDtypeStruct((n, M_WIDTH), F32),
        jax.ShapeDtypeStruct((n, A_WIDTH), BF16),
        jax.ShapeDtypeStruct((n, KV_WIDTH), F32),
        jax.ShapeDtypeStruct((n, KV_WIDTH), F32),
        jax.ShapeDtypeStruct((2 * M_HEADS, n), F32),
        jax.ShapeDtypeStruct((n // BLOCK, M_HEADS, M_DIM, BLOCK), BF16),
        jax.ShapeDtypeStruct((n, 4 * KV_WIDTH), BF16),
        jax.ShapeDtypeStruct((n, 4 * KV_WIDTH), BF16),
    )


def _front(x2d, xs2d, params, tm):
    n, ns = x2d.shape[0], xs2d.shape[0]
    n_tiles = n // tm
    nb_t = tm // BLOCK
    tile = lambda i: jnp.minimum(i, n_tiles - 1)
    row = lambda w: pl.BlockSpec((tm, w), lambda i: (tile(i), 0))
    prompt_specs = (row(D_MODEL), row(3 * M_WIDTH), row(M_WIDTH), row(A_WIDTH), row(KV_WIDTH), row(KV_WIDTH),
                    pl.BlockSpec((2 * M_HEADS, tm), lambda i: (0, tile(i))),
                    pl.BlockSpec((nb_t, M_HEADS, M_DIM, BLOCK), lambda i: (tile(i), 0, 0, 0)),
                    row(4 * KV_WIDTH), row(4 * KV_WIDTH))
    sample_shapes = _front_out_shapes(ns)
    outs = pl.pallas_call(
        functools.partial(_front_kernel, n_tiles=n_tiles),
        out_shape=_front_out_shapes(n) + sample_shapes,
        grid=(n_tiles + 1,),
        in_specs=[row(D_MODEL), _whole_spec(xs2d.shape)] + [_const_spec(p.shape) for p in params],
        out_specs=prompt_specs + tuple(_whole_spec(s.shape) for s in sample_shapes),
        scratch_shapes=[pltpu.VMEM((tm, D_MODEL), BF16), pltpu.VMEM((tm, D_MODEL), F32)],
        compiler_params=pltpu.CompilerParams(dimension_semantics=("arbitrary",),
                                             vmem_limit_bytes=VMEM_LIMIT_BYTES),
        name="front",
    )(x2d, xs2d, *params)
    return outs[:N_FRONT_OUTS], outs[N_FRONT_OUTS:]


def _back_tile(x1_ref, y_ref, params, out_ref, h_ref, acc_ref):
    wo_ref, n_ref, wg_ref, wu_ref, wd_ref = params
    x2 = x1_ref[...] + _dot(y_ref[...], wo_ref[...])
    h_ref[...] = _rms_rows(x2, n_ref[...]).astype(BF16)
    _ffn_into(h_ref, wg_ref, wu_ref, wd_ref, acc_ref)
    out_ref[...] = x2 + FFN_RES_WEIGHT * acc_ref[...]


def _back_kernel(x1_ref, y_ref, x1s_ref, ys_ref, wo_ref, n_ref, wg_ref, wu_ref, wd_ref, out_ref, outs_ref,
                 h_ref, acc_ref, *, n_tiles):
    params = (wo_ref, n_ref, wg_ref, wu_ref, wd_ref)
    i = pl.program_id(0)

    @pl.when(i < n_tiles)
    def _():
        _back_tile(x1_ref, y_ref, params, out_ref, h_ref, acc_ref)

    @pl.when(i == n_tiles)
    def _():
        ns = x1s_ref.shape[0]
        _back_tile(x1s_ref, ys_ref, params, outs_ref, h_ref.at[0:ns], acc_ref.at[0:ns])


def _back(x1, y, x1s, ys, params, tm):
    n, ns = x1.shape[0], x1s.shape[0]
    n_tiles = n // tm
    row = pl.BlockSpec((tm, D_MODEL), lambda i: (jnp.minimum(i, n_tiles - 1), 0))
    return pl.pallas_call(
        functools.partial(_back_kernel, n_tiles=n_tiles),
        out_shape=(jax.ShapeDtypeStruct((n, D_MODEL), F32), jax.ShapeDtypeStruct((ns, D_MODEL), F32)),
        grid=(n_tiles + 1,),
        in_specs=[row, row, _whole_spec(x1s.shape), _whole_spec(ys.shape)] + [_const_spec(p.shape) for p in params],
        out_specs=(row, _whole_spec((ns, D_MODEL))),
        scratch_shapes=[pltpu.VMEM((tm, D_MODEL), BF16), pltpu.VMEM((tm, D_MODEL), F32)],
        compiler_params=pltpu.CompilerParams(dimension_semantics=("arbitrary",),
                                             vmem_limit_bytes=VMEM_LIMIT_BYTES),
        name="back",
    )(x1, y, x1s, ys, *params)


def _gate_prep_kernel(gt_ref, bias_ref, tri_ref, rows_ref, cols_ref, m_ref):
    nblk = rows_ref.shape[1]
    sub = lax.broadcasted_iota(jnp.int32, (8, BLOCK), 0)
    lane = lax.broadcasted_iota(jnp.int32, (8, BLOCK), 1)
    is_head = sub < M_HEADS
    pad = jnp.zeros((BLOCK - 24, BLOCK), F32)
    m_prev = jnp.zeros((8, 1), F32)
    for j in range(nblk):
        pre = gt_ref[:, j * BLOCK:(j + 1) * BLOCK] + bias_ref[...]
        r = jnp.where(is_head, pre, _log_sigmoid(pre))
        cum = jnp.dot(r, tri_ref[...], preferred_element_type=F32, precision=lax.Precision.HIGHEST)
        bcum = pltpu.roll(cum, M_HEADS, 0)
        g = jnp.where(is_head, r - bcum, 0.0)
        bcum = jnp.where(is_head, bcum, 0.0)
        cm = g
        for shift in (1, 2, 4, 8, 16, 32, 64):
            cm = jnp.maximum(cm, jnp.where(lane >= shift, pltpu.roll(cm, shift, 1), NEG_INF))
        cm_last = jnp.max(cm, axis=-1, keepdims=True)
        b_last = jnp.sum(jnp.where(lane == BLOCK - 1, bcum, 0.0), axis=-1, keepdims=True)
        mx = jnp.maximum(m_prev, cm)
        mx_last = jnp.maximum(m_prev, cm_last)
        rows_ref[0, j, 0] = g * LOG2E
        rows_ref[0, j, 1] = jnp.exp(g - mx_last)
        rows_ref[0, j, 2] = jnp.broadcast_to(jnp.exp(m_prev - mx_last), (8, BLOCK))
        col_src = jnp.concatenate([mx * -LOG2E, jnp.exp(m_prev - mx), jnp.exp(-(bcum + mx)), pad], axis=0)
        cols_ref[j * BLOCK:(j + 1) * BLOCK, :] = col_src.T
        m_prev = b_last + mx_last
    m_ref[0] = jnp.broadcast_to(m_prev, (8, BLOCK))


def _gate_prep(gt, bias8, tri, batch, seq):
    nblk = seq // BLOCK
    out_shape = (
        jax.ShapeDtypeStruct((batch, nblk, 3, 8, BLOCK), F32),
        jax.ShapeDtypeStruct((batch * seq, 128), F32),
        jax.ShapeDtypeStruct((batch, 8, BLOCK), F32),
    )
    return pl.pallas_call(
        _gate_prep_kernel,
        out_shape=out_shape,
        grid=(batch,),
        in_specs=[pl.BlockSpec((2 * M_HEADS, seq), lambda b: (0, b)),
                  pl.BlockSpec(bias8.shape, lambda b: (0, 0)), pl.BlockSpec(tri.shape, lambda b: (0, 0))],
        out_specs=(pl.BlockSpec((1, nblk, 3, 8, BLOCK), lambda b: (b, 0, 0, 0, 0)),
                   pl.BlockSpec((seq, 128), lambda b: (b, 0)),
                   pl.BlockSpec((1, 8, BLOCK), lambda b: (b, 0, 0))),
        compiler_params=pltpu.CompilerParams(dimension_semantics=("arbitrary",)),
        name="gate_prep",
    )(gt, bias8, tri)


def _prompt_mixer_kernel(sinks_ref, qkv_ref, vt_ref, og_ref, cols_ref, rows_ref, qa_ref,
                         kx_ref, vx_ref, kxp_ref, vxp_ref,
                         y_ref, c_out_ref, n_out_ref,
                         c_ref, n_ref):
    j = pl.program_id(0)
    batch = qkv_ref.shape[0]

    @pl.when(j == 0)
    def _():
        c_ref[...] = jnp.zeros_like(c_ref)
        n_ref[...] = jnp.zeros_like(n_ref)

    rows = lax.broadcasted_iota(jnp.int32, (BLOCK, BLOCK), 0)
    cols = lax.broadcasted_iota(jnp.int32, (BLOCK, BLOCK), 1)
    causal = cols <= rows
    lane_lo = lax.broadcasted_iota(jnp.int32, (2 * BLOCK, KV_WIDTH), 1) < A_DIM
    ones_m = jnp.ones((BLOCK, M_DIM), BF16)
    ones_half = (jnp.where(lane_lo, 1.0, 0.0).astype(BF16), jnp.where(lane_lo, 0.0, 1.0).astype(BF16))
    qi = lax.broadcasted_iota(jnp.int32, (2 * BLOCK, 2 * BLOCK), 0) % BLOCK
    kc = lax.broadcasted_iota(jnp.int32, (2 * BLOCK, 2 * BLOCK), 1)
    first_valid = jnp.where(j == 0, BLOCK, 0)
    valid = (kc >= qi) & (kc <= qi + WINDOW) & (kc >= first_valid)
    top_rows = lax.broadcasted_iota(jnp.int32, (2 * BLOCK, 1), 0) < BLOCK

    heads = range(M_HEADS)
    groups = [(kv, parity) for kv in range(KV_HEADS) for parity in range(2)]
    m_lo = lambda h: h * M_DIM

    for b in range(batch):
        col_b = cols_ref[b]
        q = [qkv_ref[b, :, m_lo(h):m_lo(h) + M_DIM] for h in heads]
        k = [qkv_ref[b, :, M_WIDTH + m_lo(h):M_WIDTH + m_lo(h) + M_DIM] for h in heads]
        v = [qkv_ref[b, :, 2 * M_WIDTH + m_lo(h):2 * M_WIDTH + m_lo(h) + M_DIM] for h in heads]
        g_row = [rows_ref[b, 0, 0][h:h + 1, :] for h in heads]
        wend_row = [rows_ref[b, 0, 1][h:h + 1, :] for h in heads]
        decay = [rows_ref[b, 0, 2][h:h + 1, :] for h in heads]
        c_prev = [c_ref[b, h] for h in heads]
        n_prev = [n_ref[b, h] for h in heads]

        qkc, upd = [], []
        for h in heads:
            n_rep = jnp.broadcast_to(n_prev[h][0:1, :], (BLOCK, M_DIM)).astype(BF16)
            rhs = jnp.concatenate([k[h], c_prev[h].astype(BF16), n_rep], axis=0)
            qkc.append(_dot_nt(q[h], rhs))
        for h in heads:
            vw_t = (vt_ref[b, 0, h].astype(F32) * wend_row[h]).astype(BF16)
            w_rep = jnp.broadcast_to(wend_row[h], (16, BLOCK)).astype(BF16)
            upd.append(_dot(jnp.concatenate([vw_t, w_rep], axis=0), k[h]))
        q2 = [jnp.concatenate([qa_ref[b, :, (2 * kv) * 128:(2 * kv + 1) * 128],
                               qa_ref[b, :, (2 * kv + 1) * 128:(2 * kv + 2) * 128]], axis=0)
              for kv in range(KV_HEADS)]
        sc = []
        for kv, parity in groups:
            var = (2 * kv + parity) * KV_WIDTH
            k_band = jnp.concatenate([kxp_ref[b, :, var:var + KV_WIDTH], kx_ref[b, :, var:var + KV_WIDTH]], axis=0)
            sc.append(_dot_nt(q2[kv], k_band))

        s = []
        for h in heads:
            d = jnp.where(causal, jnp.exp2(col_b[:, h:h + 1] + g_row[h]), 0.0)
            s.append((qkc[h][:, 0:BLOCK] * d).astype(BF16))
            c_ref[b, h] = decay[h] * c_prev[h] + upd[h][0:M_DIM]
            n_ref[b, h] = decay[h] * n_prev[h] + upd[h][M_DIM:M_DIM + 8]
        pr, e_sink = [], []
        for gi, (kv, parity) in enumerate(groups):
            sink = jnp.where(top_rows, sinks_ref[4 * kv + parity], sinks_ref[4 * kv + 2 + parity]) * LOG2E
            scm = jnp.where(valid, sc[gi], NEG_INF)
            mx = jnp.maximum(jnp.max(scm, axis=-1, keepdims=True), sink)
            pr.append(jnp.exp2(scm - mx).astype(BF16))
            e_sink.append(jnp.exp2(sink - mx))

        sv = [_dot(s[h], jnp.concatenate([v[h], ones_m], axis=1)) for h in heads]
        pv = []
        for gi, (kv, parity) in enumerate(groups):
            var = (2 * kv + parity) * KV_WIDTH
            v_band = jnp.concatenate([vxp_ref[b, :, var:var + KV_WIDTH], vx_ref[b, :, var:var + KV_WIDTH]], axis=0)
            pv.append(_dot(pr[gi], jnp.concatenate([v_band, ones_half[parity]], axis=1)))

        for h in heads:
            lo, hi = m_lo(h), m_lo(h) + M_DIM
            wi_col = col_b[:, 8 + h:9 + h]
            num = wi_col * qkc[h][:, BLOCK:2 * BLOCK] + sv[h][:, 0:M_DIM]
            den = wi_col * qkc[h][:, 2 * BLOCK:3 * BLOCK] + sv[h][:, M_DIM:2 * M_DIM]
            hh = num / jnp.maximum(jnp.abs(den), col_b[:, 16 + h:17 + h])
            hh = hh * lax.rsqrt(jnp.mean(hh * hh, axis=-1, keepdims=True) + RMS_EPS)
            y_ref[b, :, lo:hi] = (hh * og_ref[b, :, lo:hi]).astype(BF16)
        for kv in range(KV_HEADS):
            acc = pv[2 * kv] + pv[2 * kv + 1]
            denom = acc[:, KV_WIDTH:] + jnp.where(lane_lo, e_sink[2 * kv], e_sink[2 * kv + 1])
            ya = (acc[:, 0:KV_WIDTH] / denom).astype(BF16)
            p0, p1 = 2 * kv, 2 * kv + 1
            y_ref[b, :, M_WIDTH + p0 * 128:M_WIDTH + (p0 + 1) * 128] = ya[0:BLOCK]
            y_ref[b, :, M_WIDTH + p1 * 128:M_WIDTH + (p1 + 1) * 128] = ya[BLOCK:2 * BLOCK]

    @pl.when(j == pl.num_programs(0) - 1)
    def _():
        c_out_ref[...] = c_ref[...]
        n_out_ref[...] = n_ref[...]


def _prompt_mixer(sinks, qkv, vt, og, cols, rows, qa, kx, vx, batch, seq):
    nblk = seq // BLOCK
    r3 = lambda a: a.reshape(batch, seq, a.shape[-1])
    cur = lambda w: pl.BlockSpec((batch, BLOCK, w), lambda j: (0, j, 0))
    prev = lambda w: pl.BlockSpec((batch, BLOCK, w), lambda j: (0, jnp.maximum(j - 1, 0), 0))
    state = lambda shape: pl.BlockSpec((batch,) + shape, lambda j: (0,) * (len(shape) + 1))
    out_shape = (
        jax.ShapeDtypeStruct((batch, seq, D_MODEL), BF16),
        jax.ShapeDtypeStruct((batch, M_HEADS, M_DIM, M_DIM), F32),
        jax.ShapeDtypeStruct((batch, M_HEADS, 8, M_DIM), F32),
    )
    y, pc, pn = pl.pallas_call(
        _prompt_mixer_kernel,
        out_shape=out_shape,
        grid=(nblk,),
        in_specs=[pl.BlockSpec(memory_space=pltpu.SMEM),
                  cur(3 * M_WIDTH),
                  pl.BlockSpec((batch, 1, M_HEADS, M_DIM, BLOCK), lambda j: (0, j, 0, 0, 0)),
                  cur(M_WIDTH), cur(128),
                  pl.BlockSpec((batch, 1, 3, 8, BLOCK), lambda j: (0, j, 0, 0, 0)),
                  cur(A_WIDTH), cur(4 * KV_WIDTH), cur(4 * KV_WIDTH), prev(4 * KV_WIDTH), prev(4 * KV_WIDTH)],
        out_specs=(cur(D_MODEL), state((M_HEADS, M_DIM, M_DIM)), state((M_HEADS, 8, M_DIM))),
        scratch_shapes=[pltpu.VMEM((batch, M_HEADS, M_DIM, M_DIM), F32),
                        pltpu.VMEM((batch, M_HEADS, 8, M_DIM), F32)],
        compiler_params=pltpu.CompilerParams(dimension_semantics=("arbitrary",),
                                             vmem_limit_bytes=VMEM_LIMIT_BYTES),
        name="prompt_mixer",
    )(sinks, r3(qkv), vt.reshape(batch, nblk, M_HEADS, M_DIM, BLOCK), r3(og), r3(cols), rows, r3(qa),
      r3(kx), r3(vx), r3(kx), r3(vx))
    return y.reshape(batch * seq, D_MODEL), pc, pn


def _sample_mlstm_kernel(bi_ref, bf_ref, qt_ref, kt_ref, vt_ref, ogt_ref, gi_ref, gf_ref, m0_ref, n0t_ref,
                         qr_ref, kr_ref, c_ref,
                         yt_ref, nt_out_ref, m_out_ref, c_out_ref, decay_ref):
    h = pl.program_id(0)
    nb = qr_ref.shape[0]
    qt, kt, vt = qt_ref[0], kt_ref[0], vt_ref[0]
    i_pre = gi_ref[0] + bi_ref[h]
    a = _log_sigmoid(gf_ref[0] + bf_ref[h]) + m0_ref[0]
    m_t = jnp.maximum(a, i_pre)
    w_inter = jnp.exp(a - m_t)
    w_in = jnp.exp(i_pre - m_t)
    scores = jnp.sum(qt * kt, axis=0, keepdims=True) * w_in
    n0t = n0t_ref[0]
    nq = jnp.sum(n0t * qt, axis=0, keepdims=True)

    rows = lax.broadcasted_iota(jnp.int32, (nb, M_DIM), 0)
    cols = lax.broadcasted_iota(jnp.int32, (M_DIM, nb), 1)
    vw_t = (vt * w_in).astype(BF16)
    k_rows = kr_ref[...]

    decay_ref[...] = jnp.broadcast_to(w_inter, (M_DIM, nb)).T

    def body(grp, cq_t):
        base = grp * SAMPLE_UNROLL
        for u in range(SAMPLE_UNROLL):
            b = base + u
            col = jnp.sum(c_ref[b, 0] * qr_ref[pl.ds(b, 1), :], axis=-1, keepdims=True)
            cq_t = jnp.where(cols == b, col, cq_t)
        outer = []
        for u in range(SAMPLE_UNROLL):
            k_only_b = jnp.where(rows == base + u, k_rows, 0.0).astype(BF16)
            outer.append(_dot(vw_t, k_only_b))
        for u in range(SAMPLE_UNROLL):
            b = base + u
            c_out_ref[b, 0] = decay_ref[pl.ds(b, 1), :] * c_ref[b, 0] + outer[u]
        return cq_t

    cq_t = lax.fori_loop(0, nb // SAMPLE_UNROLL, body, jnp.zeros((M_DIM, nb), F32))

    num = w_inter * cq_t + scores * vt
    den = w_inter * nq + scores
    hh = num / jnp.maximum(jnp.abs(den), jnp.exp(-m_t))
    hh = hh * lax.rsqrt(jnp.mean(hh * hh, axis=0, keepdims=True) + RMS_EPS)
    yt_ref[0] = hh * ogt_ref[0]
    nt_out_ref[0] = w_inter * n0t + w_in * kt
    m_out_ref[0] = m_t


def _sample_mlstm(b_i, b_f, qt, kt, vt, ogt, gi, gf, m0, n0t, q_rows, k_rows, c0):
    nb = q_rows.shape[0]
    head3 = lambda r: pl.BlockSpec((1, r, nb), lambda h: (h, 0, 0))
    smem = pl.BlockSpec(memory_space=pltpu.SMEM)
    out_shape = (
        jax.ShapeDtypeStruct((M_HEADS, M_DIM, nb), F32),
        jax.ShapeDtypeStruct((M_HEADS, M_DIM, nb), F32),
        jax.ShapeDtypeStruct((M_HEADS, 1, nb), F32),
        jax.ShapeDtypeStruct((nb, M_HEADS, M_DIM, M_DIM), F32),
    )
    c_spec = pl.BlockSpec((nb, 1, M_DIM, M_DIM), lambda h: (0, h, 0, 0))
    return pl.pallas_call(
        _sample_mlstm_kernel,
        out_shape=out_shape,
        grid=(M_HEADS,),
        in_specs=[smem, smem, head3(M_DIM), head3(M_DIM), head3(M_DIM), head3(M_DIM), head3(1), head3(1),
                  head3(1), head3(M_DIM),
                  pl.BlockSpec((nb, M_DIM), lambda h: (0, h)), pl.BlockSpec((nb, M_DIM), lambda h: (0, h)),
                  c_spec],
        out_specs=(head3(M_DIM), head3(M_DIM), head3(1), c_spec),
        scratch_shapes=[pltpu.VMEM((nb, M_DIM), F32)],
        compiler_params=pltpu.CompilerParams(dimension_semantics=("arbitrary",),
                                             vmem_limit_bytes=VMEM_LIMIT_BYTES),
        name="sample_mlstm",
    )(b_i, b_f, qt, kt, vt, ogt, gi, gf, m0, n0t, q_rows, k_rows, c0)


SAMPLE_TILE = 16
SAMPLE_UNROLL = 8


def _sample_swa_kernel(q2_ref, kc_ref, vc_ref, kn_ref, vn_ref, sink_ref, o_ref, ko_ref, vo_ref):
    sink = sink_ref[...] * LOG2E
    w = kc_ref.shape[1]
    tile = range(SAMPLE_TILE)
    q2 = [q2_ref[b] for b in tile]
    k_new = [kn_ref[b:b + 1, :] for b in tile]
    v_new = [vn_ref[b:b + 1, :] for b in tile]
    s_c = [_dot_nt(q2[b], kc_ref[b].astype(BF16)) for b in tile]
    s_n = [jnp.sum(q2[b].astype(F32) * k_new[b], axis=-1, keepdims=True) for b in tile]
    mx = [jnp.maximum(jnp.maximum(jnp.max(s_c[b], axis=-1, keepdims=True), s_n[b]), sink) for b in tile]
    p_c = [jnp.exp2(s_c[b] - mx[b]) for b in tile]
    p_n = [jnp.exp2(s_n[b] - mx[b]) for b in tile]
    denom = [jnp.sum(p_c[b], axis=-1, keepdims=True) + p_n[b] + jnp.exp2(sink - mx[b]) for b in tile]
    o = [_dot(p_c[b].astype(BF16), vc_ref[b].astype(BF16)) for b in tile]
    for b in tile:
        o_ref[b] = (o[b] + p_n[b] * v_new[b]) / denom[b]
        ko_ref[b, 0:w - 1, :] = kc_ref[b, 1:w, :]
        ko_ref[b, w - 1:w, :] = k_new[b]
        vo_ref[b, 0:w - 1, :] = vc_ref[b, 1:w, :]
        vo_ref[b, w - 1:w, :] = v_new[b]


def _sample_swa(q2, k_cache, v_cache, k_new, v_new, sink_col):
    nb, w, _ = k_cache.shape
    t3 = lambda a, c: pl.BlockSpec((SAMPLE_TILE, a, c), lambda i: (i, 0, 0))
    t2 = pl.BlockSpec((SAMPLE_TILE, KV_WIDTH), lambda i: (i, 0))
    out_shape = (
        jax.ShapeDtypeStruct((nb, A_HEADS, KV_WIDTH), F32),
        jax.ShapeDtypeStruct((nb, w, KV_WIDTH), F32),
        jax.ShapeDtypeStruct((nb, w, KV_WIDTH), F32),
    )
    return pl.pallas_call(
        _sample_swa_kernel,
        out_shape=out_shape,
        grid=(nb // SAMPLE_TILE,),
        in_specs=[t3(A_HEADS, KV_WIDTH), t3(w, KV_WIDTH), t3(w, KV_WIDTH), t2, t2,
                  pl.BlockSpec((A_HEADS, 1), lambda i: (0, 0))],
        out_specs=(t3(A_HEADS, KV_WIDTH), t3(w, KV_WIDTH), t3(w, KV_WIDTH)),
        compiler_params=pltpu.CompilerParams(dimension_semantics=("arbitrary",)),
        name="sample_swa",
    )(q2, k_cache, v_cache, k_new, v_new, sink_col)


def _prep_weights(ffn1_w_gate, ffn1_w_up, ffn1_w_down, w_in, w_out, ffn2_w_gate, ffn2_w_up, ffn2_w_down):
    cols = rows = lambda w: w.astype(BF16)

    m_end = 4 * M_WIDTH
    g_end = m_end + 2 * M_HEADS
    q_end = g_end + A_WIDTH
    wm = w_in[:, :m_end].astype(BF16)
    wqa = w_in[:, g_end:q_end].astype(BF16)
    gates = jnp.pad(w_in[:, m_end:g_end], ((0, 0), (0, GATE_PAD - 2 * M_HEADS)))
    wkvg = jnp.concatenate([w_in[:, q_end:], gates], axis=1).astype(BF16)
    return (cols(ffn1_w_gate), cols(ffn1_w_up), rows(ffn1_w_down), wm, wqa, wkvg, w_out.astype(BF16),
            cols(ffn2_w_gate), cols(ffn2_w_up), rows(ffn2_w_down))


def kernel(x_prompt, x_sample, cache_swa_k, cache_swa_v, state_mlstm_C, state_mlstm_n, state_mlstm_m,
           ffn1_norm, ffn1_w_gate, ffn1_w_up, ffn1_w_down, mix_norm, w_in, mlstm_b_i, mlstm_b_f,
           mlstm_out_norm, swa_q_norm, swa_k_norm, swa_sinks, w_out, ffn2_norm, ffn2_w_gate,
           ffn2_w_up, ffn2_w_down):
    depth = ffn1_norm.shape[0]
    assert depth == 1
    batch, seq, _ = x_prompt.shape
    nb = x_sample.shape[0]
    assert x_sample.shape[1] == 1 and seq % BLOCK == 0

    (wg1, wu1, wd1, wm, wqa, wkvg, wo, wg2, wu2, wd2) = _prep_weights(
        ffn1_w_gate[0], ffn1_w_up[0], ffn1_w_down[0], w_in[0], w_out[0],
        ffn2_w_gate[0], ffn2_w_up[0], ffn2_w_down[0])
    n1 = ffn1_norm[0].reshape(1, D_MODEL)
    n2 = mix_norm[0].reshape(1, D_MODEL)
    n3 = ffn2_norm[0].reshape(1, D_MODEL)
    qgain = jnp.tile(swa_q_norm[0], A_HEADS).reshape(1, A_WIDTH)
    kgain = jnp.tile(swa_k_norm[0], KV_HEADS).reshape(1, KV_WIDTH)
    ogain = mlstm_out_norm[0].reshape(1, M_WIDTH)
    front_params = (n1, wg1, wu1, wd1, n2, wm, wqa, wkvg, qgain, kgain, ogain)
    back_params = (wo, n3, wg2, wu2, wd2)
    b_i, b_f = mlstm_b_i[0], mlstm_b_f[0]
    sinks = swa_sinks[0]

    xp = x_prompt.reshape(batch * seq, D_MODEL)
    xs = x_sample.reshape(nb, D_MODEL)
    (x1, qkv, og, qa, ka, va, gt, vt, kx, vx), sample_front = _front(xp, xs, front_params, tm=512)
    x1s, qkvs, ogs, qas, kas, vas, gates_t = sample_front[:7]

    bias8 = jnp.concatenate([b_i, b_f]).reshape(2 * M_HEADS, 1)
    tri = (jnp.arange(BLOCK)[:, None] <= jnp.arange(BLOCK)[None, :]).astype(F32)
    rows, cols, pm = _gate_prep(gt, bias8, tri, batch, seq)
    y, pc, pn = _prompt_mixer(sinks, qkv, vt, og, cols, rows, qa, kx, vx, batch, seq)
    w = min(WINDOW, seq)
    pk = ka.reshape(batch, seq, KV_WIDTH)[:, seq - w:].reshape(batch, w, KV_HEADS, A_DIM)
    pv = va.reshape(batch, seq, KV_WIDTH)[:, seq - w:].reshape(batch, w, KV_HEADS, A_DIM)
    pn = pn[:, :, 0, :]
    pm = pm[:, 0:M_HEADS, 0]

    qkv_f = qkvs.astype(F32)
    to_heads_t = lambda a: jnp.transpose(a.reshape(nb, M_HEADS, M_DIM), (1, 2, 0))
    q_rows = qkv_f[:, 0:M_WIDTH]
    k_rows = qkv_f[:, M_WIDTH:2 * M_WIDTH]
    ymt, nt, mt, sc = _sample_mlstm(
        b_i, b_f, to_heads_t(q_rows), to_heads_t(k_rows), to_heads_t(qkv_f[:, 2 * M_WIDTH:]), to_heads_t(ogs),
        gates_t[0:M_HEADS].reshape(M_HEADS, 1, nb), gates_t[M_HEADS:].reshape(M_HEADS, 1, nb),
        jnp.transpose(state_mlstm_m[0]).reshape(M_HEADS, 1, nb), jnp.transpose(state_mlstm_n[0], (1, 2, 0)),
        q_rows, k_rows, state_mlstm_C[0])
    y_m = jnp.transpose(ymt, (2, 0, 1)).reshape(nb, M_WIDTH)
    sn = jnp.transpose(nt, (2, 0, 1))
    sm = jnp.transpose(mt[:, 0, :])

    qa_h = qas.reshape(nb, A_HEADS, A_DIM)
    zeros = jnp.zeros_like(qa_h)
    in_lo = (jnp.arange(A_HEADS) // A_GROUP == 0)[None, :, None]
    q2 = jnp.concatenate([jnp.where(in_lo, qa_h, zeros), jnp.where(in_lo, zeros, qa_h)], axis=-1)
    kc = cache_swa_k[0].reshape(nb, -1, KV_WIDTH)
    vc = cache_swa_v[0].reshape(nb, -1, KV_WIDTH)
    o2, sk, sv = _sample_swa(q2, kc, vc, kas, vas, sinks.reshape(A_HEADS, 1))
    o2 = o2.reshape(nb, A_HEADS, KV_HEADS, A_DIM)
    y_a = jnp.where(in_lo, o2[:, :, 0, :], o2[:, :, 1, :]).reshape(nb, A_WIDTH)
    ys_in = jnp.concatenate([y_m, y_a], axis=-1).astype(BF16)

    yp, ys = _back(x1, y, x1s, ys_in, back_params, tm=1024)
    yp = yp.reshape(batch, seq, D_MODEL)
    ys = ys.reshape(nb, 1, D_MODEL)

    wb = kc.shape[1]
    return (yp, ys, pk[None], pv[None], pc[None], pn[None], pm[None],
            sk.reshape(1, nb, wb, KV_HEADS, A_DIM), sv.reshape(1, nb, wb, KV_HEADS, A_DIM),
            sc[None], sn[None], sm[None])
```

```python
import functools

import jax
import jax.numpy as jnp
from jax import lax
from jax.experimental import pallas as pl
from jax.experimental.pallas import tpu as pltpu

F32 = jnp.float32
BF16 = jnp.bfloat16

D_MODEL = 1024
D_FF = 2816
FF_CHUNK = 256
N_FF_CHUNKS = D_FF // FF_CHUNK
M_HEADS = 4
M_DIM = 128
M_WIDTH = M_HEADS * M_DIM
A_HEADS = 8
A_DIM = 64
A_WIDTH = A_HEADS * A_DIM
KV_HEADS = 2
KV_WIDTH = KV_HEADS * A_DIM
A_GROUP = A_HEADS // KV_HEADS
WINDOW = 128
BLOCK = 128
GATE_PAD = 128
RMS_EPS = 1e-6
FFN_RES_WEIGHT = 0.5
NEG_INF = float("-inf")
LOG2E = 1.4426950408889634
VMEM_LIMIT_BYTES = 56 * 1024 * 1024


def _dot(a, b):
    return jnp.dot(a, b, preferred_element_type=F32)


def _dot_nt(a, b):
    return lax.dot_general(a, b, (((1,), (1,)), ((), ())), preferred_element_type=F32)


def _rms_rows(x, gain):
    ms = jnp.mean(x * x, axis=-1, keepdims=True)
    return x * lax.rsqrt(ms + RMS_EPS) * gain


def _log_sigmoid(x):
    return jnp.minimum(x, 0.0) - jnp.log1p(jnp.exp(-jnp.abs(x)))


def _half_tile_mean_sq(x):
    in_lo = lax.broadcasted_iota(jnp.int32, (x.shape[0], 128), 1) < A_DIM
    out = []
    for c in range(x.shape[1] // 128):
        sq = x[:, c * 128:(c + 1) * 128]
        sq = sq * sq
        s_lo = jnp.sum(jnp.where(in_lo, sq, 0.0), axis=-1, keepdims=True)
        s_hi = jnp.sum(jnp.where(in_lo, 0.0, sq), axis=-1, keepdims=True)
        out.append(jnp.where(in_lo, s_lo, s_hi) * (1.0 / A_DIM))
    return out[0] if len(out) == 1 else jnp.concatenate(out, axis=1)


def _ffn_into(h_ref, wg_ref, wu_ref, wd_ref, acc_ref):
    for c in range(N_FF_CHUNKS):
        lo, hi = c * FF_CHUNK, (c + 1) * FF_CHUNK
        h = h_ref[...]
        g = _dot(h, wg_ref[:, lo:hi])
        u = _dot(h, wu_ref[:, lo:hi])
        a = (g * jax.nn.sigmoid(g) * u).astype(BF16)
        d = _dot(a, wd_ref[lo:hi, :])
        if c == 0:
            acc_ref[...] = d
        else:
            acc_ref[...] += d


N_FRONT_PARAMS = 11
N_FRONT_OUTS = 10


def _front_tile(x_ref, params, outs, h_ref, acc_ref):
    (n1_ref, wg_ref, wu_ref, wd_ref, n2_ref, wm_ref, wqa_ref, wkvg_ref, qgain_ref, kgain_ref, ogain_ref) = params
    (x1_ref, qkv_ref, og_ref, qa_ref, ka_ref, va_ref, gt_ref, vt_ref, kx_ref, vx_ref) = outs
    x = x_ref[...]
    h_ref[...] = _rms_rows(x, n1_ref[...]).astype(BF16)
    _ffn_into(h_ref, wg_ref, wu_ref, wd_ref, acc_ref)
    x1 = x + FFN_RES_WEIGHT * acc_ref[...]
    x1_ref[...] = x1
    h_ref[...] = _rms_rows(x1, n2_ref[...]).astype(BF16)
    h = h_ref[...]

    qa = _dot(h, wqa_ref[...])
    kvg = _dot(h, wkvg_ref[...])
    v_m = _dot(h, wm_ref[:, 2 * M_WIDTH:3 * M_WIDTH])
    qkv_ref[:, 2 * M_WIDTH:3 * M_WIDTH] = v_m.astype(BF16)
    for c in range(vt_ref.shape[0]):
        for hd in range(M_HEADS):
            blk = v_m[c * BLOCK:(c + 1) * BLOCK, hd * M_DIM:(hd + 1) * M_DIM]
            vt_ref[c, hd] = blk.T.astype(BF16)
    q_scale = qgain_ref[...] * (A_DIM ** -0.5 * LOG2E)
    qa_ref[...] = (qa * lax.rsqrt(_half_tile_mean_sq(qa) + RMS_EPS) * q_scale).astype(BF16)
    ka = kvg[:, 0:KV_WIDTH]
    ka = ka * lax.rsqrt(_half_tile_mean_sq(ka) + RMS_EPS) * kgain_ref[...]
    va = kvg[:, KV_WIDTH:2 * KV_WIDTH]
    keep = ka_ref.shape[0]
    ka_ref[...] = ka[ka.shape[0] - keep:, :]
    va_ref[...] = va[va.shape[0] - keep:, :]
    gt_ref[...] = kvg[:, 2 * KV_WIDTH:2 * KV_WIDTH + GATE_PAD].T[0:2 * M_HEADS, :]

    in_lo = lax.broadcasted_iota(jnp.int32, ka.shape, 1) < A_DIM
    for src, dst in ((ka, kx_ref), (va, vx_ref)):
        x0 = jnp.where(in_lo, src, 0.0)
        x1 = jnp.where(in_lo, 0.0, src)
        dst[:, 0:128] = x0.astype(BF16)
        dst[:, 128:256] = pltpu.roll(x0, A_DIM, 1).astype(BF16)
        dst[:, 256:384] = pltpu.roll(x1, A_DIM, 1).astype(BF16)
        dst[:, 384:512] = x1.astype(BF16)

    og_ref[...] = jax.nn.sigmoid(_dot(h, wm_ref[:, 3 * M_WIDTH:4 * M_WIDTH])) * ogain_ref[...]
    k_m = _dot(h, wm_ref[:, M_WIDTH:2 * M_WIDTH]) * (M_DIM ** -0.5)
    qkv_ref[:, M_WIDTH:2 * M_WIDTH] = k_m.astype(BF16)
    qkv_ref[:, 0:M_WIDTH] = _dot(h, wm_ref[:, 0:M_WIDTH]).astype(BF16)


def _front_kernel(*refs, n_tiles):
    x_ref, xs_ref = refs[0:2]
    params = refs[2:2 + N_FRONT_PARAMS]
    outs_p = refs[2 + N_FRONT_PARAMS:2 + N_FRONT_PARAMS + N_FRONT_OUTS]
    outs_s = refs[2 + N_FRONT_PARAMS + N_FRONT_OUTS:2 + N_FRONT_PARAMS + 2 * N_FRONT_OUTS]
    h_ref, acc_ref = refs[2 + N_FRONT_PARAMS + 2 * N_FRONT_OUTS:]
    i = pl.program_id(0)

    @pl.when(i < n_tiles)
    def _():
        _front_tile(x_ref, params, outs_p, h_ref, acc_ref)

    @pl.when(i == n_tiles)
    def _():
        ns = xs_ref.shape[0]
        _front_tile(xs_ref, params, outs_s, h_ref.at[0:ns], acc_ref.at[0:ns])


def _const_spec(shape):
    nd = len(shape)
    return pl.BlockSpec(shape, lambda i: (0,) * nd, pipeline_mode=pl.Buffered(1))


def _whole_spec(shape):
    nd = len(shape)
    return pl.BlockSpec(shape, lambda i: (0,) * nd)


def _front_out_shapes(n, n_cache_rows):
    return (
        jax.ShapeDtypeStruct((n, D_MODEL), F32),
        jax.ShapeDtypeStruct((n, 3 * M_WIDTH), BF16),
        jax.ShapeDtypeStruct((n, M_WIDTH), F32),
        jax.ShapeDtypeStruct((n, A_WIDTH), BF16),
        jax.ShapeDtypeStruct((n_cache_rows, KV_WIDTH), F32),
        jax.ShapeDtypeStruct((n_cache_rows, KV_WIDTH), F32),
        jax.ShapeDtypeStruct((2 * M_HEADS, n), F32),
        jax.ShapeDtypeStruct((n // BLOCK, M_HEADS, M_DIM, BLOCK), BF16),
        jax.ShapeDtypeStruct((n, 4 * KV_WIDTH), BF16),
        jax.ShapeDtypeStruct((n, 4 * KV_WIDTH), BF16),
    )


def _front(x2d, xs2d, params, tm, seq):
    n, ns = x2d.shape[0], xs2d.shape[0]
    assert seq % tm == 0 and tm >= WINDOW
    n_tiles = n // tm
    tiles_per_seq = seq // tm
    nb_t = tm // BLOCK
    tile = lambda i: jnp.minimum(i, n_tiles - 1)
    row = lambda w: pl.BlockSpec((tm, w), lambda i: (tile(i), 0))
    tail = pl.BlockSpec((WINDOW, KV_WIDTH), lambda i: (tile(i) // tiles_per_seq, 0))
    prompt_specs = (row(D_MODEL), row(3 * M_WIDTH), row(M_WIDTH), row(A_WIDTH), tail, tail,
                    pl.BlockSpec((2 * M_HEADS, tm), lambda i: (0, tile(i))),
                    pl.BlockSpec((nb_t, M_HEADS, M_DIM, BLOCK), lambda i: (tile(i), 0, 0, 0)),
                    row(4 * KV_WIDTH), row(4 * KV_WIDTH))
    sample_shapes = _front_out_shapes(ns, ns)
    outs = pl.pallas_call(
        functools.partial(_front_kernel, n_tiles=n_tiles),
        out_shape=_front_out_shapes(n, (n // seq) * WINDOW) + sample_shapes,
        grid=(n_tiles + 1,),
        in_specs=[row(D_MODEL), _whole_spec(xs2d.shape)] + [_const_spec(p.shape) for p in params],
        out_specs=prompt_specs + tuple(_whole_spec(s.shape) for s in sample_shapes),
        scratch_shapes=[pltpu.VMEM((tm, D_MODEL), BF16), pltpu.VMEM((tm, D_MODEL), F32)],
        compiler_params=pltpu.CompilerParams(dimension_semantics=("arbitrary",),
                                             vmem_limit_bytes=VMEM_LIMIT_BYTES),
        name="front",
    )(x2d, xs2d, *params)
    return outs[:N_FRONT_OUTS], outs[N_FRONT_OUTS:]


def _back_tile(x1_ref, y_ref, params, out_ref, h_ref, acc_ref):
    wo_ref, n_ref, wg_ref, wu_ref, wd_ref = params
    x2 = x1_ref[...] + _dot(y_ref[...], wo_ref[...])
    h_ref[...] = _rms_rows(x2, n_ref[...]).astype(BF16)
    _ffn_into(h_ref, wg_ref, wu_ref, wd_ref, acc_ref)
    out_ref[...] = x2 + FFN_RES_WEIGHT * acc_ref[...]


def _back_kernel(x1_ref, y_ref, x1s_ref, ys_ref, wo_ref, n_ref, wg_ref, wu_ref, wd_ref, out_ref, outs_ref,
                 h_ref, acc_ref, *, n_tiles):
    params = (wo_ref, n_ref, wg_ref, wu_ref, wd_ref)
    i = pl.program_id(0)

    @pl.when(i < n_tiles)
    def _():
        _back_tile(x1_ref, y_ref, params, out_ref, h_ref, acc_ref)

    @pl.when(i == n_tiles)
    def _():
        ns = x1s_ref.shape[0]
        _back_tile(x1s_ref, ys_ref, params, outs_ref, h_ref.at[0:ns], acc_ref.at[0:ns])


def _back(x1, y, x1s, ys, params, tm):
    n, ns = x1.shape[0], x1s.shape[0]
    n_tiles = n // tm
    row = pl.BlockSpec((tm, D_MODEL), lambda i: (jnp.minimum(i, n_tiles - 1), 0))
    return pl.pallas_call(
        functools.partial(_back_kernel, n_tiles=n_tiles),
        out_shape=(jax.ShapeDtypeStruct((n, D_MODEL), F32), jax.ShapeDtypeStruct((ns, D_MODEL), F32)),
        grid=(n_tiles + 1,),
        in_specs=[row, row, _whole_spec(x1s.shape), _whole_spec(ys.shape)] + [_const_spec(p.shape) for p in params],
        out_specs=(row, _whole_spec((ns, D_MODEL))),
        scratch_shapes=[pltpu.VMEM((tm, D_MODEL), BF16), pltpu.VMEM((tm, D_MODEL), F32)],
        compiler_params=pltpu.CompilerParams(dimension_semantics=("arbitrary",),
                                             vmem_limit_bytes=VMEM_LIMIT_BYTES),
        name="back",
    )(x1, y, x1s, ys, *params)


def _gate_prep_kernel(gt_ref, bias_ref, tri_ref, rows_ref, cols_ref, m_ref):
    nblk = rows_ref.shape[1]
    sub = lax.broadcasted_iota(jnp.int32, (8, BLOCK), 0)
    lane = lax.broadcasted_iota(jnp.int32, (8, BLOCK), 1)
    is_head = sub < M_HEADS
    pad = jnp.zeros((BLOCK - 24, BLOCK), F32)
    m_prev = jnp.zeros((8, 1), F32)
    for j in range(nblk):
        pre = gt_ref[:, j * BLOCK:(j + 1) * BLOCK] + bias_ref[...]
        r = jnp.where(is_head, pre, _log_sigmoid(pre))
        cum = jnp.dot(r, tri_ref[...], preferred_element_type=F32, precision=lax.Precision.HIGHEST)
        bcum = pltpu.roll(cum, M_HEADS, 0)
        g = jnp.where(is_head, r - bcum, 0.0)
        bcum = jnp.where(is_head, bcum, 0.0)
        cm = g
        for shift in (1, 2, 4, 8, 16, 32, 64):
            cm = jnp.maximum(cm, jnp.where(lane >= shift, pltpu.roll(cm, shift, 1), NEG_INF))
        cm_last = jnp.max(cm, axis=-1, keepdims=True)
        b_last = jnp.sum(jnp.where(lane == BLOCK - 1, bcum, 0.0), axis=-1, keepdims=True)
        mx = jnp.maximum(m_prev, cm)
        mx_last = jnp.maximum(m_prev, cm_last)
        rows_ref[0, j, 0] = g * LOG2E
        rows_ref[0, j, 1] = jnp.exp(g - mx_last)
        rows_ref[0, j, 2] = jnp.broadcast_to(jnp.exp(m_prev - mx_last), (8, BLOCK))
        col_src = jnp.concatenate([mx * -LOG2E, jnp.exp(m_prev - mx), jnp.exp(-(bcum + mx)), pad], axis=0)
        cols_ref[j * BLOCK:(j + 1) * BLOCK, :] = col_src.T
        m_prev = b_last + mx_last
    m_ref[0] = jnp.broadcast_to(m_prev, (8, BLOCK))


def _gate_prep(gt, bias8, tri, batch, seq):
    nblk = seq // BLOCK
    out_shape = (
        jax.ShapeDtypeStruct((batch, nblk, 3, 8, BLOCK), F32),
        jax.ShapeDtypeStruct((batch * seq, 128), F32),
        jax.ShapeDtypeStruct((batch, 8, BLOCK), F32),
    )
    return pl.pallas_call(
        _gate_prep_kernel,
        out_shape=out_shape,
        grid=(batch,),
        in_specs=[pl.BlockSpec((2 * M_HEADS, seq), lambda b: (0, b)),
                  pl.BlockSpec(bias8.shape, lambda b: (0, 0)), pl.BlockSpec(tri.shape, lambda b: (0, 0))],
        out_specs=(pl.BlockSpec((1, nblk, 3, 8, BLOCK), lambda b: (b, 0, 0, 0, 0)),
                   pl.BlockSpec((seq, 128), lambda b: (b, 0)),
                   pl.BlockSpec((1, 8, BLOCK), lambda b: (b, 0, 0))),
        compiler_params=pltpu.CompilerParams(dimension_semantics=("arbitrary",)),
        name="gate_prep",
    )(gt, bias8, tri)


def _prompt_mixer_kernel(sinks_ref, qkv_ref, vt_ref, og_ref, cols_ref, rows_ref, qa_ref,
                         kx_ref, vx_ref, kxp_ref, vxp_ref,
                         y_ref, c_out_ref, n_out_ref,
                         c_ref, n_ref):
    j = pl.program_id(0)
    batch = qkv_ref.shape[0]

    @pl.when(j == 0)
    def _():
        c_ref[...] = jnp.zeros_like(c_ref)
        n_ref[...] = jnp.zeros_like(n_ref)

    rows = lax.broadcasted_iota(jnp.int32, (BLOCK, BLOCK), 0)
    cols = lax.broadcasted_iota(jnp.int32, (BLOCK, BLOCK), 1)
    causal = cols <= rows
    lane_lo = lax.broadcasted_iota(jnp.int32, (2 * BLOCK, KV_WIDTH), 1) < A_DIM
    ones_m = jnp.ones((BLOCK, M_DIM), BF16)
    ones_half = (jnp.where(lane_lo, 1.0, 0.0).astype(BF16), jnp.where(lane_lo, 0.0, 1.0).astype(BF16))
    qi = lax.broadcasted_iota(jnp.int32, (2 * BLOCK, 2 * BLOCK), 0) % BLOCK
    kc = lax.broadcasted_iota(jnp.int32, (2 * BLOCK, 2 * BLOCK), 1)
    first_valid = jnp.where(j == 0, BLOCK, 0)
    valid = (kc >= qi) & (kc <= qi + WINDOW) & (kc >= first_valid)
    top_rows = lax.broadcasted_iota(jnp.int32, (2 * BLOCK, 1), 0) < BLOCK

    heads = range(M_HEADS)
    groups = [(kv, parity) for kv in range(KV_HEADS) for parity in range(2)]
    m_lo = lambda h: h * M_DIM

    for b in range(batch):
        col_b = cols_ref[b]
        q = [qkv_ref[b, :, m_lo(h):m_lo(h) + M_DIM] for h in heads]
        k = [qkv_ref[b, :, M_WIDTH + m_lo(h):M_WIDTH + m_lo(h) + M_DIM] for h in heads]
        v = [qkv_ref[b, :, 2 * M_WIDTH + m_lo(h):2 * M_WIDTH + m_lo(h) + M_DIM] for h in heads]
        g_row = [rows_ref[b, 0, 0][h:h + 1, :] for h in heads]
        wend_row = [rows_ref[b, 0, 1][h:h + 1, :] for h in heads]
        decay = [rows_ref[b, 0, 2][h:h + 1, :] for h in heads]
        c_prev = [c_ref[b, h] for h in heads]
        n_prev = [n_ref[b, h] for h in heads]

        qkc, upd = [], []
        for h in heads:
            n_rep = jnp.broadcast_to(n_prev[h][0:1, :], (BLOCK, M_DIM)).astype(BF16)
            rhs = jnp.concatenate([k[h], c_prev[h].astype(BF16), n_rep], axis=0)
            qkc.append(_dot_nt(q[h], rhs))
        for h in heads:
            vw_t = (vt_ref[b, 0, h].astype(F32) * wend_row[h]).astype(BF16)
            w_rep = jnp.broadcast_to(wend_row[h], (16, BLOCK)).astype(BF16)
            upd.append(_dot(jnp.concatenate([vw_t, w_rep], axis=0), k[h]))
        q2 = [jnp.concatenate([qa_ref[b, :, (2 * kv) * 128:(2 * kv + 1) * 128],
                               qa_ref[b, :, (2 * kv + 1) * 128:(2 * kv + 2) * 128]], axis=0)
              for kv in range(KV_HEADS)]
        sc = []
        for kv, parity in groups:
            var = (2 * kv + parity) * KV_WIDTH
            k_band = jnp.concatenate([kxp_ref[b, :, var:var + KV_WIDTH], kx_ref[b, :, var:var + KV_WIDTH]], axis=0)
            sc.append(_dot_nt(q2[kv], k_band))

        s = []
        for h in heads:
            d = jnp.where(causal, jnp.exp2(col_b[:, h:h + 1] + g_row[h]), 0.0)
            s.append((qkc[h][:, 0:BLOCK] * d).astype(BF16))
            c_ref[b, h] = decay[h] * c_prev[h] + upd[h][0:M_DIM]
            n_ref[b, h] = decay[h] * n_prev[h] + upd[h][M_DIM:M_DIM + 8]
        pr, e_sink = [], []
        for gi, (kv, parity) in enumerate(groups):
            sink = jnp.where(top_rows, sinks_ref[4 * kv + parity], sinks_ref[4 * kv + 2 + parity]) * LOG2E
            scm = jnp.where(valid, sc[gi], NEG_INF)
            mx = jnp.maximum(jnp.max(scm, axis=-1, keepdims=True), sink)
            pr.append(jnp.exp2(scm - mx).astype(BF16))
            e_sink.append(jnp.exp2(sink - mx))

        sv = [_dot(s[h], jnp.concatenate([v[h], ones_m], axis=1)) for h in heads]
        pv = []
        for gi, (kv, parity) in enumerate(groups):
            var = (2 * kv + parity) * KV_WIDTH
            v_band = jnp.concatenate([vxp_ref[b, :, var:var + KV_WIDTH], vx_ref[b, :, var:var + KV_WIDTH]], axis=0)
            pv.append(_dot(pr[gi], jnp.concatenate([v_band, ones_half[parity]], axis=1)))

        for h in heads:
            lo, hi = m_lo(h), m_lo(h) + M_DIM
            wi_col = col_b[:, 8 + h:9 + h]
            num = wi_col * qkc[h][:, BLOCK:2 * BLOCK] + sv[h][:, 0:M_DIM]
            den = wi_col * qkc[h][:, 2 * BLOCK:3 * BLOCK] + sv[h][:, M_DIM:2 * M_DIM]
            hh = num / jnp.maximum(jnp.abs(den), col_b[:, 16 + h:17 + h])
            hh = hh * lax.rsqrt(jnp.mean(hh * hh, axis=-1, keepdims=True) + RMS_EPS)
            y_ref[b, :, lo:hi] = (hh * og_ref[b, :, lo:hi]).astype(BF16)
        for kv in range(KV_HEADS):
            acc = pv[2 * kv] + pv[2 * kv + 1]
            denom = acc[:, KV_WIDTH:] + jnp.where(lane_lo, e_sink[2 * kv], e_sink[2 * kv + 1])
            ya = (acc[:, 0:KV_WIDTH] / denom).astype(BF16)
            p0, p1 = 2 * kv, 2 * kv + 1
            y_ref[b, :, M_WIDTH + p0 * 128:M_WIDTH + (p0 + 1) * 128] = ya[0:BLOCK]
            y_ref[b, :, M_WIDTH + p1 * 128:M_WIDTH + (p1 + 1) * 128] = ya[BLOCK:2 * BLOCK]

    @pl.when(j == pl.num_programs(0) - 1)
    def _():
        c_out_ref[...] = c_ref[...]
        n_out_ref[...] = n_ref[...]


def _prompt_mixer(sinks, qkv, vt, og, cols, rows, qa, kx, vx, batch, seq):
    nblk = seq // BLOCK
    r3 = lambda a: a.reshape(batch, seq, a.shape[-1])
    cur = lambda w: pl.BlockSpec((batch, BLOCK, w), lambda j: (0, j, 0))
    prev = lambda w: pl.BlockSpec((batch, BLOCK, w), lambda j: (0, jnp.maximum(j - 1, 0), 0))
    state = lambda shape: pl.BlockSpec((batch,) + shape, lambda j: (0,) * (len(shape) + 1))
    out_shape = (
        jax.ShapeDtypeStruct((batch, seq, D_MODEL), BF16),
        jax.ShapeDtypeStruct((batch, M_HEADS, M_DIM, M_DIM), F32),
        jax.ShapeDtypeStruct((batch, M_HEADS, 8, M_DIM), F32),
    )
    y, pc, pn = pl.pallas_call(
        _prompt_mixer_kernel,
        out_shape=out_shape,
        grid=(nblk,),
        in_specs=[pl.BlockSpec(memory_space=pltpu.SMEM),
                  cur(3 * M_WIDTH),
                  pl.BlockSpec((batch, 1, M_HEADS, M_DIM, BLOCK), lambda j: (0, j, 0, 0, 0)),
                  cur(M_WIDTH), cur(128),
                  pl.BlockSpec((batch, 1, 3, 8, BLOCK), lambda j: (0, j, 0, 0, 0)),
                  cur(A_WIDTH), cur(4 * KV_WIDTH), cur(4 * KV_WIDTH), prev(4 * KV_WIDTH), prev(4 * KV_WIDTH)],
        out_specs=(cur(D_MODEL), state((M_HEADS, M_DIM, M_DIM)), state((M_HEADS, 8, M_DIM))),
        scratch_shapes=[pltpu.VMEM((batch, M_HEADS, M_DIM, M_DIM), F32),
                        pltpu.VMEM((batch, M_HEADS, 8, M_DIM), F32)],
        compiler_params=pltpu.CompilerParams(dimension_semantics=("arbitrary",),
                                             vmem_limit_bytes=VMEM_LIMIT_BYTES),
        name="prompt_mixer",
    )(sinks, r3(qkv), vt.reshape(batch, nblk, M_HEADS, M_DIM, BLOCK), r3(og), r3(cols), rows, r3(qa),
      r3(kx), r3(vx), r3(kx), r3(vx))
    return y.reshape(batch * seq, D_MODEL), pc, pn


def _sample_mlstm_kernel(bi_ref, bf_ref, q_ref, k_ref, v_ref, og_ref, gates_ref, m0_ref, n0_ref, c_ref,
                         y_ref, n_out_ref, m_out_ref, c_out_ref, decay_ref, qr_ref):
    h = pl.program_id(0)
    nb = q_ref.shape[0]
    q_rows = q_ref[...].astype(F32)
    k_rows = k_ref[...].astype(F32)
    qr_ref[...] = q_rows
    qt, kt, vt = q_rows.T, k_rows.T, v_ref[...].astype(F32).T
    i_pre = gates_ref[pl.ds(h, 1), :] + bi_ref[h]
    a = _log_sigmoid(gates_ref[pl.ds(M_HEADS + h, 1), :] + bf_ref[h]) + m0_ref[pl.ds(h, 1), :]
    m_t = jnp.maximum(a, i_pre)
    w_inter = jnp.exp(a - m_t)
    w_in = jnp.exp(i_pre - m_t)
    scores = jnp.sum(qt * kt, axis=0, keepdims=True) * w_in
    n0t = n0_ref[...].T
    nq = jnp.sum(n0t * qt, axis=0, keepdims=True)

    rows = lax.broadcasted_iota(jnp.int32, (nb, M_DIM), 0)
    cols = lax.broadcasted_iota(jnp.int32, (M_DIM, nb), 1)
    vw_t = (vt * w_in).astype(BF16)

    decay_ref[...] = jnp.broadcast_to(w_inter, (M_DIM, nb)).T

    def body(grp, cq_t):
        base = grp * SAMPLE_UNROLL
        for u in range(SAMPLE_UNROLL):
            b = base + u
            col = jnp.sum(c_ref[b, 0] * qr_ref[pl.ds(b, 1), :], axis=-1, keepdims=True)
            cq_t = jnp.where(cols == b, col, cq_t)
        outer = []
        for u in range(SAMPLE_UNROLL):
            k_only_b = jnp.where(rows == base + u, k_rows, 0.0).astype(BF16)
            outer.append(_dot(vw_t, k_only_b))
        for u in range(SAMPLE_UNROLL):
            b = base + u
            c_out_ref[b, 0] = decay_ref[pl.ds(b, 1), :] * c_ref[b, 0] + outer[u]
        return cq_t

    cq_t = lax.fori_loop(0, nb // SAMPLE_UNROLL, body, jnp.zeros((M_DIM, nb), F32))

    num = w_inter * cq_t + scores * vt
    den = w_inter * nq + scores
    hh = num / jnp.maximum(jnp.abs(den), jnp.exp(-m_t))
    hh = hh * lax.rsqrt(jnp.mean(hh * hh, axis=0, keepdims=True) + RMS_EPS)
    y_ref[...] = hh.T * og_ref[...]
    n_out_ref[...] = (w_inter * n0t + w_in * kt).T
    m_out_ref[0] = m_t


def _sample_mlstm(b_i, b_f, qkv, og, gates_t, m0_t, n0, c0):
    nb = qkv.shape[0]
    smem = pl.BlockSpec(memory_space=pltpu.SMEM)
    head = lambda off: pl.BlockSpec((nb, M_DIM), lambda h: (0, off + h))
    out_shape = (
        jax.ShapeDtypeStruct((nb, M_WIDTH), F32),
        jax.ShapeDtypeStruct((nb, M_WIDTH), F32),
        jax.ShapeDtypeStruct((M_HEADS, 1, nb), F32),
        jax.ShapeDtypeStruct((nb, M_HEADS, M_DIM, M_DIM), F32),
    )
    c_spec = pl.BlockSpec((nb, 1, M_DIM, M_DIM), lambda h: (0, h, 0, 0))
    return pl.pallas_call(
        _sample_mlstm_kernel,
        out_shape=out_shape,
        grid=(M_HEADS,),
        in_specs=[smem, smem, head(0), head(M_HEADS), head(2 * M_HEADS), head(0),
                  pl.BlockSpec(gates_t.shape, lambda h: (0, 0)), pl.BlockSpec(m0_t.shape, lambda h: (0, 0)),
                  head(0), c_spec],
        out_specs=(head(0), head(0), pl.BlockSpec((1, 1, nb), lambda h: (h, 0, 0)), c_spec),
        scratch_shapes=[pltpu.VMEM((nb, M_DIM), F32), pltpu.VMEM((nb, M_DIM), F32)],
        compiler_params=pltpu.CompilerParams(dimension_semantics=("arbitrary",),
                                             vmem_limit_bytes=VMEM_LIMIT_BYTES),
        name="sample_mlstm",
    )(b_i, b_f, qkv, qkv, qkv, og, gates_t, m0_t, n0, c0)


SAMPLE_TILE = 16
SAMPLE_UNROLL = 8


def _sample_swa_kernel(q2_ref, kc_ref, vc_ref, kn_ref, vn_ref, sink_ref, o_ref, ko_ref, vo_ref):
    sink = sink_ref[...] * LOG2E
    w = kc_ref.shape[1]
    tile = range(SAMPLE_TILE)
    q2 = [q2_ref[b] for b in tile]
    k_new = [kn_ref[b:b + 1, :] for b in tile]
    v_new = [vn_ref[b:b + 1, :] for b in tile]
    s_c = [_dot_nt(q2[b], kc_ref[b].astype(BF16)) for b in tile]
    s_n = [jnp.sum(q2[b].astype(F32) * k_new[b], axis=-1, keepdims=True) for b in tile]
    mx = [jnp.maximum(jnp.maximum(jnp.max(s_c[b], axis=-1, keepdims=True), s_n[b]), sink) for b in tile]
    p_c = [jnp.exp2(s_c[b] - mx[b]) for b in tile]
    p_n = [jnp.exp2(s_n[b] - mx[b]) for b in tile]
    denom = [jnp.sum(p_c[b], axis=-1, keepdims=True) + p_n[b] + jnp.exp2(sink - mx[b]) for b in tile]
    o = [_dot(p_c[b].astype(BF16), vc_ref[b].astype(BF16)) for b in tile]
    for b in tile:
        o_ref[b] = (o[b] + p_n[b] * v_new[b]) / denom[b]
        ko_ref[b, 0:w - 1, :] = kc_ref[b, 1:w, :]
        ko_ref[b, w - 1:w, :] = k_new[b]
        vo_ref[b, 0:w - 1, :] = vc_ref[b, 1:w, :]
        vo_ref[b, w - 1:w, :] = v_new[b]


def _sample_swa(q2, k_cache, v_cache, k_new, v_new, sink_col):
    nb, w, _ = k_cache.shape
    t3 = lambda a, c: pl.BlockSpec((SAMPLE_TILE, a, c), lambda i: (i, 0, 0))
    t2 = pl.BlockSpec((SAMPLE_TILE, KV_WIDTH), lambda i: (i, 0))
    out_shape = (
        jax.ShapeDtypeStruct((nb, A_HEADS, KV_WIDTH), F32),
        jax.ShapeDtypeStruct((nb, w, KV_WIDTH), F32),
        jax.ShapeDtypeStruct((nb, w, KV_WIDTH), F32),
    )
    return pl.pallas_call(
        _sample_swa_kernel,
        out_shape=out_shape,
        grid=(nb // SAMPLE_TILE,),
        in_specs=[t3(A_HEADS, KV_WIDTH), t3(w, KV_WIDTH), t3(w, KV_WIDTH), t2, t2,
                  pl.BlockSpec((A_HEADS, 1), lambda i: (0, 0))],
        out_specs=(t3(A_HEADS, KV_WIDTH), t3(w, KV_WIDTH), t3(w, KV_WIDTH)),
        compiler_params=pltpu.CompilerParams(dimension_semantics=("arbitrary",)),
        name="sample_swa",
    )(q2, k_cache, v_cache, k_new, v_new, sink_col)


def _prep_weights(ffn1_w_gate, ffn1_w_up, ffn1_w_down, w_in, w_out, ffn2_w_gate, ffn2_w_up, ffn2_w_down):
    cols = rows = lambda w: w.astype(BF16)

    m_end = 4 * M_WIDTH
    g_end = m_end + 2 * M_HEADS
    q_end = g_end + A_WIDTH
    wm = w_in[:, :m_end].astype(BF16)
    wqa = w_in[:, g_end:q_end].astype(BF16)
    gates = jnp.pad(w_in[:, m_end:g_end], ((0, 0), (0, GATE_PAD - 2 * M_HEADS)))
    wkvg = jnp.concatenate([w_in[:, q_end:], gates], axis=1).astype(BF16)
    return (cols(ffn1_w_gate), cols(ffn1_w_up), rows(ffn1_w_down), wm, wqa, wkvg, w_out.astype(BF16),
            cols(ffn2_w_gate), cols(ffn2_w_up), rows(ffn2_w_down))


def kernel(x_prompt, x_sample, cache_swa_k, cache_swa_v, state_mlstm_C, state_mlstm_n, state_mlstm_m,
           ffn1_norm, ffn1_w_gate, ffn1_w_up, ffn1_w_down, mix_norm, w_in, mlstm_b_i, mlstm_b_f,
           mlstm_out_norm, swa_q_norm, swa_k_norm, swa_sinks, w_out, ffn2_norm, ffn2_w_gate,
           ffn2_w_up, ffn2_w_down):
    depth = ffn1_norm.shape[0]
    assert depth == 1
    batch, seq, _ = x_prompt.shape
    nb = x_sample.shape[0]
    assert x_sample.shape[1] == 1 and seq % BLOCK == 0

    (wg1, wu1, wd1, wm, wqa, wkvg, wo, wg2, wu2, wd2) = _prep_weights(
        ffn1_w_gate[0], ffn1_w_up[0], ffn1_w_down[0], w_in[0], w_out[0],
        ffn2_w_gate[0], ffn2_w_up[0], ffn2_w_down[0])
    n1 = ffn1_norm[0].reshape(1, D_MODEL)
    n2 = mix_norm[0].reshape(1, D_MODEL)
    n3 = ffn2_norm[0].reshape(1, D_MODEL)
    qgain = jnp.tile(swa_q_norm[0], A_HEADS).reshape(1, A_WIDTH)
    kgain = jnp.tile(swa_k_norm[0], KV_HEADS).reshape(1, KV_WIDTH)
    ogain = mlstm_out_norm[0].reshape(1, M_WIDTH)
    front_params = (n1, wg1, wu1, wd1, n2, wm, wqa, wkvg, qgain, kgain, ogain)
    back_params = (wo, n3, wg2, wu2, wd2)
    b_i, b_f = mlstm_b_i[0], mlstm_b_f[0]
    sinks = swa_sinks[0]

    xp = x_prompt.reshape(batch * seq, D_MODEL)
    xs = x_sample.reshape(nb, D_MODEL)
    (x1, qkv, og, qa, ka, va, gt, vt, kx, vx), sample_front = _front(xp, xs, front_params, tm=512, seq=seq)
    x1s, qkvs, ogs, qas, kas, vas, gates_t = sample_front[:7]

    bias8 = jnp.concatenate([b_i, b_f]).reshape(2 * M_HEADS, 1)
    tri = (jnp.arange(BLOCK)[:, None] <= jnp.arange(BLOCK)[None, :]).astype(F32)
    rows, cols, pm = _gate_prep(gt, bias8, tri, batch, seq)
    y, pc, pn = _prompt_mixer(sinks, qkv, vt, og, cols, rows, qa, kx, vx, batch, seq)
    pk = ka.reshape(batch, WINDOW, KV_HEADS, A_DIM)
    pv = va.reshape(batch, WINDOW, KV_HEADS, A_DIM)
    pn = pn[:, :, 0, :]
    pm = pm[:, 0:M_HEADS, 0]

    y_m, sn, mt, sc = _sample_mlstm(b_i, b_f, qkvs, ogs, gates_t, jnp.transpose(state_mlstm_m[0]),
                                    state_mlstm_n[0].reshape(nb, M_WIDTH), state_mlstm_C[0])
    sn = sn.reshape(nb, M_HEADS, M_DIM)
    sm = jnp.transpose(mt[:, 0, :])

    qa_h = qas.reshape(nb, A_HEADS, A_DIM)
    zeros = jnp.zeros_like(qa_h)
    in_lo = (jnp.arange(A_HEADS) // A_GROUP == 0)[None, :, None]
    q2 = jnp.concatenate([jnp.where(in_lo, qa_h, zeros), jnp.where(in_lo, zeros, qa_h)], axis=-1)
    kc = cache_swa_k[0].reshape(nb, -1, KV_WIDTH)
    vc = cache_swa_v[0].reshape(nb, -1, KV_WIDTH)
    o2, sk, sv = _sample_swa(q2, kc, vc, kas, vas, sinks.reshape(A_HEADS, 1))
    o2 = o2.reshape(nb, A_HEADS, KV_HEADS, A_DIM)
    y_a = jnp.where(in_lo, o2[:, :, 0, :], o2[:, :, 1, :]).reshape(nb, A_WIDTH)
    ys_in = jnp.concatenate([y_m, y_a], axis=-1).astype(BF16)

    yp, ys = _back(x1, y, x1s, ys_in, back_params, tm=1024)
    yp = yp.reshape(batch, seq, D_MODEL)
    ys = ys.reshape(nb, 1, D_MODEL)

    wb = kc.shape[1]
    return (yp, ys, pk[None], pv[None], pc[None], pn[None], pm[None],
            sk.reshape(1, nb, wb, KV_HEADS, A_DIM), sv.reshape(1, nb, wb, KV_HEADS, A_DIM),
            sc[None], sn[None], sm[None])
```

```python
import functools

import jax
import jax.numpy as jnp
from jax import lax
from jax.experimental import pallas as pl
from jax.experimental.pallas import tpu as pltpu

F32 = jnp.float32
BF16 = jnp.bfloat16

D_MODEL = 1024
D_FF = 2816
FF_CHUNK = 256
N_FF_CHUNKS = D_FF // FF_CHUNK
M_HEADS = 4
M_DIM = 128
M_WIDTH = M_HEADS * M_DIM
A_HEADS = 8
A_DIM = 64
A_WIDTH = A_HEADS * A_DIM
KV_HEADS = 2
KV_WIDTH = KV_HEADS * A_DIM
A_GROUP = A_HEADS // KV_HEADS
WINDOW = 128
BLOCK = 128
GATE_PAD = 128
RMS_EPS = 1e-6
FFN_RES_WEIGHT = 0.5
NEG_INF = float("-inf")
LOG2E = 1.4426950408889634
VMEM_LIMIT_BYTES = 56 * 1024 * 1024


def _dot(a, b):
    return jnp.dot(a, b, preferred_element_type=F32)


def _dot_nt(a, b):
    return lax.dot_general(a, b, (((1,), (1,)), ((), ())), preferred_element_type=F32)


def _rms_rows(x, gain):
    ms = jnp.mean(x * x, axis=-1, keepdims=True)
    return x * lax.rsqrt(ms + RMS_EPS) * gain


def _log_sigmoid(x):
    return jnp.minimum(x, 0.0) - jnp.log1p(jnp.exp(-jnp.abs(x)))


def _half_tile_mean_sq(x):
    in_lo = lax.broadcasted_iota(jnp.int32, (x.shape[0], 128), 1) < A_DIM
    out = []
    for c in range(x.shape[1] // 128):
        sq = x[:, c * 128:(c + 1) * 128]
        sq = sq * sq
        s_lo = jnp.sum(jnp.where(in_lo, sq, 0.0), axis=-1, keepdims=True)
        s_hi = jnp.sum(jnp.where(in_lo, 0.0, sq), axis=-1, keepdims=True)
        out.append(jnp.where(in_lo, s_lo, s_hi) * (1.0 / A_DIM))
    return out[0] if len(out) == 1 else jnp.concatenate(out, axis=1)


def _ffn_into(h_ref, wg_ref, wu_ref, wd_ref, acc_ref):
    for c in range(N_FF_CHUNKS):
        lo, hi = c * FF_CHUNK, (c + 1) * FF_CHUNK
        h = h_ref[...]
        g = _dot(h, wg_ref[:, lo:hi])
        u = _dot(h, wu_ref[:, lo:hi])
        a = (g * jax.nn.sigmoid(g) * u).astype(BF16)
        d = _dot(a, wd_ref[lo:hi, :])
        if c == 0:
            acc_ref[...] = d
        else:
            acc_ref[...] += d


N_FRONT_PARAMS = 10
N_FRONT_COMMON_OUTS = 9
N_PROMPT_GATE_OUTS = 3
W_QA = 4 * M_WIDTH
W_KVG = W_QA + A_WIDTH
W_IN_COLS = W_KVG + 2 * KV_WIDTH + GATE_PAD


def _prompt_gates(gates_t, bias_ref, first_of_seq, m_ref, rows_ref, cols_ref, mfin_ref):
    sub = lax.broadcasted_iota(jnp.int32, (8, BLOCK), 0)
    lane = lax.broadcasted_iota(jnp.int32, (8, BLOCK), 1)
    is_head = sub < M_HEADS
    pad = jnp.zeros((BLOCK - 24, BLOCK), F32)
    scan_shifts = (1, 2, 4, 8, 16, 32, 64)
    m_prev = jnp.where(first_of_seq, 0.0, m_ref[...])[:, 0:1]
    for c in range(rows_ref.shape[0]):
        pre = gates_t[:, c * BLOCK:(c + 1) * BLOCK] + bias_ref[...]
        r = jnp.where(is_head, pre, _log_sigmoid(pre))
        cum = r
        for shift in scan_shifts:
            cum = cum + jnp.where(lane >= shift, pltpu.roll(cum, shift, 1), 0.0)
        bcum = pltpu.roll(cum, M_HEADS, 0)
        g = jnp.where(is_head, r - bcum, 0.0)
        bcum = jnp.where(is_head, bcum, 0.0)
        cm = g
        for shift in scan_shifts:
            cm = jnp.maximum(cm, jnp.where(lane >= shift, pltpu.roll(cm, shift, 1), NEG_INF))
        cm_last = jnp.max(cm, axis=-1, keepdims=True)
        b_last = jnp.sum(jnp.where(lane == BLOCK - 1, bcum, 0.0), axis=-1, keepdims=True)
        mx = jnp.maximum(m_prev, cm)
        mx_last = jnp.maximum(m_prev, cm_last)
        rows_ref[c, 0] = g * LOG2E
        rows_ref[c, 1] = jnp.exp(g - mx_last)
        rows_ref[c, 2] = jnp.broadcast_to(jnp.exp(m_prev - mx_last), (8, BLOCK))
        col_src = jnp.concatenate([mx * -LOG2E, jnp.exp(m_prev - mx), jnp.exp(-(bcum + mx)), pad], axis=0)
        cols_ref[c * BLOCK:(c + 1) * BLOCK, :] = col_src.T
        m_prev = b_last + mx_last
    m_full = jnp.broadcast_to(m_prev, (8, BLOCK))
    m_ref[...] = m_full
    mfin_ref[0] = m_full


def _front_tile(x_ref, params, outs, gate_sink, h_ref, acc_ref):
    (n1_ref, wg_ref, wu_ref, wd_ref, n2_ref, win_ref, qgain_ref, kgain_ref, ogain_ref, _) = params
    (x1_ref, qkv_ref, og_ref, qa_ref, ka_ref, va_ref, vt_ref, kx_ref, vx_ref) = outs
    wm_ref = win_ref
    x = x_ref[...]
    h_ref[...] = _rms_rows(x, n1_ref[...]).astype(BF16)
    _ffn_into(h_ref, wg_ref, wu_ref, wd_ref, acc_ref)
    x1 = x + FFN_RES_WEIGHT * acc_ref[...]
    x1_ref[...] = x1
    h_ref[...] = _rms_rows(x1, n2_ref[...]).astype(BF16)
    h = h_ref[...]

    qa = _dot(h, win_ref[:, W_QA:W_KVG])
    kvg = _dot(h, win_ref[:, W_KVG:W_IN_COLS])
    v_m = _dot(h, wm_ref[:, 2 * M_WIDTH:3 * M_WIDTH])
    qkv_ref[:, 2 * M_WIDTH:3 * M_WIDTH] = v_m.astype(BF16)
    for c in range(vt_ref.shape[0]):
        for hd in range(M_HEADS):
            blk = v_m[c * BLOCK:(c + 1) * BLOCK, hd * M_DIM:(hd + 1) * M_DIM]
            vt_ref[c, hd] = blk.T.astype(BF16)
    q_scale = qgain_ref[...] * (A_DIM ** -0.5 * LOG2E)
    qa_ref[...] = (qa * lax.rsqrt(_half_tile_mean_sq(qa) + RMS_EPS) * q_scale).astype(BF16)
    ka = kvg[:, 0:KV_WIDTH]
    ka = ka * lax.rsqrt(_half_tile_mean_sq(ka) + RMS_EPS) * kgain_ref[...]
    va = kvg[:, KV_WIDTH:2 * KV_WIDTH]
    keep = ka_ref.shape[0]
    ka_ref[...] = ka[ka.shape[0] - keep:, :]
    va_ref[...] = va[va.shape[0] - keep:, :]
    gate_sink(kvg[:, 2 * KV_WIDTH:2 * KV_WIDTH + GATE_PAD].T[0:2 * M_HEADS, :])

    in_lo = lax.broadcasted_iota(jnp.int32, ka.shape, 1) < A_DIM
    for src, dst in ((ka, kx_ref), (va, vx_ref)):
        x0 = jnp.where(in_lo, src, 0.0)
        x1 = jnp.where(in_lo, 0.0, src)
        dst[:, 0:128] = x0.astype(BF16)
        dst[:, 128:256] = pltpu.roll(x0, A_DIM, 1).astype(BF16)
        dst[:, 256:384] = pltpu.roll(x1, A_DIM, 1).astype(BF16)
        dst[:, 384:512] = x1.astype(BF16)

    og_ref[...] = jax.nn.sigmoid(_dot(h, wm_ref[:, 3 * M_WIDTH:4 * M_WIDTH])) * ogain_ref[...]
    k_m = _dot(h, wm_ref[:, M_WIDTH:2 * M_WIDTH]) * (M_DIM ** -0.5)
    qkv_ref[:, M_WIDTH:2 * M_WIDTH] = k_m.astype(BF16)
    qkv_ref[:, 0:M_WIDTH] = _dot(h, wm_ref[:, 0:M_WIDTH]).astype(BF16)


def _front_kernel(*refs, n_tiles, tiles_per_seq):
    refs = list(refs)
    take = lambda k: [refs.pop(0) for _ in range(k)]
    x_ref, xs_ref = take(2)
    params = take(N_FRONT_PARAMS)
    outs_p = take(N_FRONT_COMMON_OUTS)
    rows_ref, cols_ref, mfin_ref = take(N_PROMPT_GATE_OUTS)
    outs_s = take(N_FRONT_COMMON_OUTS)
    (gts_ref,) = take(1)
    h_ref, acc_ref, m_ref = refs
    bias_ref = params[-1]
    i = pl.program_id(0)

    @pl.when(i < n_tiles)
    def _():
        gates = functools.partial(_prompt_gates, bias_ref=bias_ref, first_of_seq=i % tiles_per_seq == 0,
                                  m_ref=m_ref, rows_ref=rows_ref, cols_ref=cols_ref, mfin_ref=mfin_ref)
        _front_tile(x_ref, params, outs_p, gates, h_ref, acc_ref)

    @pl.when(i == n_tiles)
    def _():
        ns = xs_ref.shape[0]

        def raw_gates(gates_t):
            gts_ref[...] = gates_t

        _front_tile(xs_ref, params, outs_s, raw_gates, h_ref.at[0:ns], acc_ref.at[0:ns])


def _const_spec(shape):
    nd = len(shape)
    return pl.BlockSpec(shape, lambda i: (0,) * nd, pipeline_mode=pl.Buffered(1))


def _whole_spec(shape):
    nd = len(shape)
    return pl.BlockSpec(shape, lambda i: (0,) * nd)


def _front_out_shapes(n, n_cache_rows):
    return (
        jax.ShapeDtypeStruct((n, D_MODEL), F32),
        jax.ShapeDtypeStruct((n, 3 * M_WIDTH), BF16),
        jax.ShapeDtypeStruct((n, M_WIDTH), F32),
        jax.ShapeDtypeStruct((n, A_WIDTH), BF16),
        jax.ShapeDtypeStruct((n_cache_rows, KV_WIDTH), F32),
        jax.ShapeDtypeStruct((n_cache_rows, KV_WIDTH), F32),
        jax.ShapeDtypeStruct((n // BLOCK, M_HEADS, M_DIM, BLOCK), BF16),
        jax.ShapeDtypeStruct((n, 4 * KV_WIDTH), BF16),
        jax.ShapeDtypeStruct((n, 4 * KV_WIDTH), BF16),
    )


def _front(x2d, xs2d, params, tm, seq):
    n, ns = x2d.shape[0], xs2d.shape[0]
    assert seq % tm == 0 and tm >= WINDOW and len(params) == N_FRONT_PARAMS
    n_tiles = n // tm
    n_seqs = n // seq
    tiles_per_seq = seq // tm
    nb_t = tm // BLOCK
    tile = lambda i: jnp.minimum(i, n_tiles - 1)
    seq_of = lambda i: tile(i) // tiles_per_seq
    row = lambda w: pl.BlockSpec((tm, w), lambda i: (tile(i), 0))
    tail = pl.BlockSpec((WINDOW, KV_WIDTH), lambda i: (seq_of(i), 0))
    prompt_specs = (row(D_MODEL), row(3 * M_WIDTH), row(M_WIDTH), row(A_WIDTH), tail, tail,
                    pl.BlockSpec((nb_t, M_HEADS, M_DIM, BLOCK), lambda i: (tile(i), 0, 0, 0)),
                    row(4 * KV_WIDTH), row(4 * KV_WIDTH),
                    pl.BlockSpec((nb_t, 3, 8, BLOCK), lambda i: (tile(i), 0, 0, 0)), row(128),
                    pl.BlockSpec((1, 8, BLOCK), lambda i: (seq_of(i), 0, 0)))
    prompt_shapes = _front_out_shapes(n, n_seqs * WINDOW) + (
        jax.ShapeDtypeStruct((n // BLOCK, 3, 8, BLOCK), F32),
        jax.ShapeDtypeStruct((n, 128), F32),
        jax.ShapeDtypeStruct((n_seqs, 8, BLOCK), F32),
    )
    sample_shapes = _front_out_shapes(ns, ns) + (
        jax.ShapeDtypeStruct((2 * M_HEADS, ns), F32),
    )
    outs = pl.pallas_call(
        functools.partial(_front_kernel, n_tiles=n_tiles, tiles_per_seq=tiles_per_seq),
        out_shape=prompt_shapes + sample_shapes,
        grid=(n_tiles + 1,),
        in_specs=[row(D_MODEL), _whole_spec(xs2d.shape)] + [_const_spec(p.shape) for p in params],
        out_specs=prompt_specs + tuple(_whole_spec(s.shape) for s in sample_shapes),
        scratch_shapes=[pltpu.VMEM((tm, D_MODEL), BF16), pltpu.VMEM((tm, D_MODEL), F32),
                        pltpu.VMEM((8, BLOCK), F32)],
        compiler_params=pltpu.CompilerParams(dimension_semantics=("arbitrary",),
                                             vmem_limit_bytes=VMEM_LIMIT_BYTES),
        name="front",
    )(x2d, xs2d, *params)
    return outs[:len(prompt_shapes)], outs[len(prompt_shapes):]


def _back_tile(x1_ref, y_ref, params, out_ref, h_ref, acc_ref):
    wo_ref, n_ref, wg_ref, wu_ref, wd_ref = params
    x2 = x1_ref[...] + _dot(y_ref[...], wo_ref[...])
    h_ref[...] = _rms_rows(x2, n_ref[...]).astype(BF16)
    _ffn_into(h_ref, wg_ref, wu_ref, wd_ref, acc_ref)
    out_ref[...] = x2 + FFN_RES_WEIGHT * acc_ref[...]


def _back_kernel(x1_ref, y_ref, x1s_ref, ys_ref, wo_ref, n_ref, wg_ref, wu_ref, wd_ref, out_ref, outs_ref,
                 h_ref, acc_ref, *, n_tiles):
    params = (wo_ref, n_ref, wg_ref, wu_ref, wd_ref)
    i = pl.program_id(0)

    @pl.when(i < n_tiles)
    def _():
        _back_tile(x1_ref, y_ref, params, out_ref, h_ref, acc_ref)

    @pl.when(i == n_tiles)
    def _():
        ns = x1s_ref.shape[0]
        _back_tile(x1s_ref, ys_ref, params, outs_ref, h_ref.at[0:ns], acc_ref.at[0:ns])


def _back(x1, y, x1s, ys, params, tm):
    n, ns = x1.shape[0], x1s.shape[0]
    n_tiles = n // tm
    row = pl.BlockSpec((tm, D_MODEL), lambda i: (jnp.minimum(i, n_tiles - 1), 0))
    return pl.pallas_call(
        functools.partial(_back_kernel, n_tiles=n_tiles),
        out_shape=(jax.ShapeDtypeStruct((n, D_MODEL), F32), jax.ShapeDtypeStruct((ns, D_MODEL), F32)),
        grid=(n_tiles + 1,),
        in_specs=[row, row, _whole_spec(x1s.shape), _whole_spec(ys.shape)] + [_const_spec(p.shape) for p in params],
        out_specs=(row, _whole_spec((ns, D_MODEL))),
        scratch_shapes=[pltpu.VMEM((tm, D_MODEL), BF16), pltpu.VMEM((tm, D_MODEL), F32)],
        compiler_params=pltpu.CompilerParams(dimension_semantics=("arbitrary",),
                                             vmem_limit_bytes=VMEM_LIMIT_BYTES),
        name="back",
    )(x1, y, x1s, ys, *params)


def _prompt_mixer_kernel(sinks_ref, qkv_ref, vt_ref, og_ref, cols_ref, rows_ref, qa_ref,
                         kx_ref, vx_ref, kxp_ref, vxp_ref,
                         y_ref, c_out_ref, n_out_ref,
                         c_ref, n_ref):
    j = pl.program_id(0)
    batch = qkv_ref.shape[0]

    @pl.when(j == 0)
    def _():
        c_ref[...] = jnp.zeros_like(c_ref)
        n_ref[...] = jnp.zeros_like(n_ref)

    rows = lax.broadcasted_iota(jnp.int32, (BLOCK, BLOCK), 0)
    cols = lax.broadcasted_iota(jnp.int32, (BLOCK, BLOCK), 1)
    causal = cols <= rows
    lane_lo = lax.broadcasted_iota(jnp.int32, (2 * BLOCK, KV_WIDTH), 1) < A_DIM
    ones_m = jnp.ones((BLOCK, M_DIM), BF16)
    ones_half = (jnp.where(lane_lo, 1.0, 0.0).astype(BF16), jnp.where(lane_lo, 0.0, 1.0).astype(BF16))
    qi = lax.broadcasted_iota(jnp.int32, (2 * BLOCK, 2 * BLOCK), 0) % BLOCK
    kc = lax.broadcasted_iota(jnp.int32, (2 * BLOCK, 2 * BLOCK), 1)
    first_valid = jnp.where(j == 0, BLOCK, 0)
    valid = (kc >= qi) & (kc <= qi + WINDOW) & (kc >= first_valid)
    top_rows = lax.broadcasted_iota(jnp.int32, (2 * BLOCK, 1), 0) < BLOCK

    heads = range(M_HEADS)
    groups = [(kv, parity) for kv in range(KV_HEADS) for parity in range(2)]
    m_lo = lambda h: h * M_DIM

    for b in range(batch):
        col_b = cols_ref[b]
        q = [qkv_ref[b, :, m_lo(h):m_lo(h) + M_DIM] for h in heads]
        k = [qkv_ref[b, :, M_WIDTH + m_lo(h):M_WIDTH + m_lo(h) + M_DIM] for h in heads]
        v = [qkv_ref[b, :, 2 * M_WIDTH + m_lo(h):2 * M_WIDTH + m_lo(h) + M_DIM] for h in heads]
        g_row = [rows_ref[b, 0, 0][h:h + 1, :] for h in heads]
        wend_row = [rows_ref[b, 0, 1][h:h + 1, :] for h in heads]
        decay = [rows_ref[b, 0, 2][h:h + 1, :] for h in heads]
        c_prev = [c_ref[b, h] for h in heads]
        n_prev = [n_ref[b, h] for h in heads]

        qkc, upd = [], []
        for h in heads:
            n_rep = jnp.broadcast_to(n_prev[h][0:1, :], (BLOCK, M_DIM)).astype(BF16)
            rhs = jnp.concatenate([k[h], c_prev[h].astype(BF16), n_rep], axis=0)
            qkc.append(_dot_nt(q[h], rhs))
        for h in heads:
            vw_t = (vt_ref[b, 0, h].astype(F32) * wend_row[h]).astype(BF16)
            w_rep = jnp.broadcast_to(wend_row[h], (16, BLOCK)).astype(BF16)
            upd.append(_dot(jnp.concatenate([vw_t, w_rep], axis=0), k[h]))
        q2 = [jnp.concatenate([qa_ref[b, :, (2 * kv) * 128:(2 * kv + 1) * 128],
                               qa_ref[b, :, (2 * kv + 1) * 128:(2 * kv + 2) * 128]], axis=0)
              for kv in range(KV_HEADS)]
        sc = []
        for kv, parity in groups:
            var = (2 * kv + parity) * KV_WIDTH
            k_band = jnp.concatenate([kxp_ref[b, :, var:var + KV_WIDTH], kx_ref[b, :, var:var + KV_WIDTH]], axis=0)
            sc.append(_dot_nt(q2[kv], k_band))

        s = []
        for h in heads:
            d = jnp.where(causal, jnp.exp2(col_b[:, h:h + 1] + g_row[h]), 0.0)
            s.append((qkc[h][:, 0:BLOCK] * d).astype(BF16))
            c_ref[b, h] = decay[h] * c_prev[h] + upd[h][0:M_DIM]
            n_ref[b, h] = decay[h] * n_prev[h] + upd[h][M_DIM:M_DIM + 8]
        pr, e_sink = [], []
        for gi, (kv, parity) in enumerate(groups):
            sink = jnp.where(top_rows, sinks_ref[4 * kv + parity], sinks_ref[4 * kv + 2 + parity]) * LOG2E
            scm = jnp.where(valid, sc[gi], NEG_INF)
            mx = jnp.maximum(jnp.max(scm, axis=-1, keepdims=True), sink)
            pr.append(jnp.exp2(scm - mx).astype(BF16))
            e_sink.append(jnp.exp2(sink - mx))

        sv = [_dot(s[h], jnp.concatenate([v[h], ones_m], axis=1)) for h in heads]
        pv = []
        for gi, (kv, parity) in enumerate(groups):
            var = (2 * kv + parity) * KV_WIDTH
            v_band = jnp.concatenate([vxp_ref[b, :, var:var + KV_WIDTH], vx_ref[b, :, var:var + KV_WIDTH]], axis=0)
            pv.append(_dot(pr[gi], jnp.concatenate([v_band, ones_half[parity]], axis=1)))

        for h in heads:
            lo, hi = m_lo(h), m_lo(h) + M_DIM
            wi_col = col_b[:, 8 + h:9 + h]
            num = wi_col * qkc[h][:, BLOCK:2 * BLOCK] + sv[h][:, 0:M_DIM]
            den = wi_col * qkc[h][:, 2 * BLOCK:3 * BLOCK] + sv[h][:, M_DIM:2 * M_DIM]
            hh = num / jnp.maximum(jnp.abs(den), col_b[:, 16 + h:17 + h])
            hh = hh * lax.rsqrt(jnp.mean(hh * hh, axis=-1, keepdims=True) + RMS_EPS)
            y_ref[b, :, lo:hi] = (hh * og_ref[b, :, lo:hi]).astype(BF16)
        for kv in range(KV_HEADS):
            acc = pv[2 * kv] + pv[2 * kv + 1]
            denom = acc[:, KV_WIDTH:] + jnp.where(lane_lo, e_sink[2 * kv], e_sink[2 * kv + 1])
            ya = (acc[:, 0:KV_WIDTH] / denom).astype(BF16)
            p0, p1 = 2 * kv, 2 * kv + 1
            y_ref[b, :, M_WIDTH + p0 * 128:M_WIDTH + (p0 + 1) * 128] = ya[0:BLOCK]
            y_ref[b, :, M_WIDTH + p1 * 128:M_WIDTH + (p1 + 1) * 128] = ya[BLOCK:2 * BLOCK]

    @pl.when(j == pl.num_programs(0) - 1)
    def _():
        c_out_ref[...] = c_ref[...]
        n_out_ref[...] = n_ref[...]


def _prompt_mixer(sinks, qkv, vt, og, cols, rows, qa, kx, vx, batch, seq):
    nblk = seq // BLOCK
    r3 = lambda a: a.reshape(batch, seq, a.shape[-1])
    cur = lambda w: pl.BlockSpec((batch, BLOCK, w), lambda j: (0, j, 0))
    prev = lambda w: pl.BlockSpec((batch, BLOCK, w), lambda j: (0, jnp.maximum(j - 1, 0), 0))
    state = lambda shape: pl.BlockSpec((batch,) + shape, lambda j: (0,) * (len(shape) + 1))
    out_shape = (
        jax.ShapeDtypeStruct((batch, seq, D_MODEL), BF16),
        jax.ShapeDtypeStruct((batch, M_HEADS, M_DIM, M_DIM), F32),
        jax.ShapeDtypeStruct((batch, M_HEADS, 8, M_DIM), F32),
    )
    y, pc, pn = pl.pallas_call(
        _prompt_mixer_kernel,
        out_shape=out_shape,
        grid=(nblk,),
        in_specs=[pl.BlockSpec(memory_space=pltpu.SMEM),
                  cur(3 * M_WIDTH),
                  pl.BlockSpec((batch, 1, M_HEADS, M_DIM, BLOCK), lambda j: (0, j, 0, 0, 0)),
                  cur(M_WIDTH), cur(128),
                  pl.BlockSpec((batch, 1, 3, 8, BLOCK), lambda j: (0, j, 0, 0, 0)),
                  cur(A_WIDTH), cur(4 * KV_WIDTH), cur(4 * KV_WIDTH), prev(4 * KV_WIDTH), prev(4 * KV_WIDTH)],
        out_specs=(cur(D_MODEL), state((M_HEADS, M_DIM, M_DIM)), state((M_HEADS, 8, M_DIM))),
        scratch_shapes=[pltpu.VMEM((batch, M_HEADS, M_DIM, M_DIM), F32),
                        pltpu.VMEM((batch, M_HEADS, 8, M_DIM), F32)],
        compiler_params=pltpu.CompilerParams(dimension_semantics=("arbitrary",),
                                             vmem_limit_bytes=VMEM_LIMIT_BYTES),
        name="prompt_mixer",
    )(sinks, r3(qkv), vt.reshape(batch, nblk, M_HEADS, M_DIM, BLOCK), r3(og), r3(cols), rows, r3(qa),
      r3(kx), r3(vx), r3(kx), r3(vx))
    return y.reshape(batch * seq, D_MODEL), pc, pn


def _sample_mlstm_kernel(bi_ref, bf_ref, q_ref, k_ref, v_ref, og_ref, gates_ref, m0_ref, n0_ref, c_ref,
                         y_ref, n_out_ref, m_out_ref, c_out_ref, decay_ref, qr_ref):
    h = pl.program_id(0)
    nb = q_ref.shape[0]
    q_rows = q_ref[...].astype(F32)
    k_rows = k_ref[...].astype(F32)
    qr_ref[...] = q_rows
    qt, kt, vt = q_rows.T, k_rows.T, v_ref[...].astype(F32).T
    i_pre = gates_ref[pl.ds(h, 1), :] + bi_ref[h]
    a = _log_sigmoid(gates_ref[pl.ds(M_HEADS + h, 1), :] + bf_ref[h]) + m0_ref[pl.ds(h, 1), :]
    m_t = jnp.maximum(a, i_pre)
    w_inter = jnp.exp(a - m_t)
    w_in = jnp.exp(i_pre - m_t)
    scores = jnp.sum(qt * kt, axis=0, keepdims=True) * w_in
    n0t = n0_ref[...].T
    nq = jnp.sum(n0t * qt, axis=0, keepdims=True)

    rows = lax.broadcasted_iota(jnp.int32, (nb, M_DIM), 0)
    cols = lax.broadcasted_iota(jnp.int32, (M_DIM, nb), 1)
    vw_t = (vt * w_in).astype(BF16)

    decay_ref[...] = jnp.broadcast_to(w_inter, (M_DIM, nb)).T

    def body(grp, cq_t):
        base = grp * SAMPLE_UNROLL
        for u in range(SAMPLE_UNROLL):
            b = base + u
            col = jnp.sum(c_ref[b, 0] * qr_ref[pl.ds(b, 1), :], axis=-1, keepdims=True)
            cq_t = jnp.where(cols == b, col, cq_t)
        outer = []
        for u in range(SAMPLE_UNROLL):
            k_only_b = jnp.where(rows == base + u, k_rows, 0.0).astype(BF16)
            outer.append(_dot(vw_t, k_only_b))
        for u in range(SAMPLE_UNROLL):
            b = base + u
            c_out_ref[b, 0] = decay_ref[pl.ds(b, 1), :] * c_ref[b, 0] + outer[u]
        return cq_t

    cq_t = lax.fori_loop(0, nb // SAMPLE_UNROLL, body, jnp.zeros((M_DIM, nb), F32))

    num = w_inter * cq_t + scores * vt
    den = w_inter * nq + scores
    hh = num / jnp.maximum(jnp.abs(den), jnp.exp(-m_t))
    hh = hh * lax.rsqrt(jnp.mean(hh * hh, axis=0, keepdims=True) + RMS_EPS)
    y_ref[...] = hh.T * og_ref[...]
    n_out_ref[...] = (w_inter * n0t + w_in * kt).T
    m_out_ref[0] = m_t


def _sample_mlstm(b_i, b_f, qkv, og, gates_t, m0_t, n0, c0):
    nb = qkv.shape[0]
    smem = pl.BlockSpec(memory_space=pltpu.SMEM)
    head = lambda off: pl.BlockSpec((nb, M_DIM), lambda h: (0, off + h))
    out_shape = (
        jax.ShapeDtypeStruct((nb, M_WIDTH), F32),
        jax.ShapeDtypeStruct((nb, M_WIDTH), F32),
        jax.ShapeDtypeStruct((M_HEADS, 1, nb), F32),
        jax.ShapeDtypeStruct((nb, M_HEADS, M_DIM, M_DIM), F32),
    )
    c_spec = pl.BlockSpec((nb, 1, M_DIM, M_DIM), lambda h: (0, h, 0, 0))
    return pl.pallas_call(
        _sample_mlstm_kernel,
        out_shape=out_shape,
        grid=(M_HEADS,),
        in_specs=[smem, smem, head(0), head(M_HEADS), head(2 * M_HEADS), head(0),
                  pl.BlockSpec(gates_t.shape, lambda h: (0, 0)), pl.BlockSpec(m0_t.shape, lambda h: (0, 0)),
                  head(0), c_spec],
        out_specs=(head(0), head(0), pl.BlockSpec((1, 1, nb), lambda h: (h, 0, 0)), c_spec),
        scratch_shapes=[pltpu.VMEM((nb, M_DIM), F32), pltpu.VMEM((nb, M_DIM), F32)],
        compiler_params=pltpu.CompilerParams(dimension_semantics=("arbitrary",),
                                             vmem_limit_bytes=VMEM_LIMIT_BYTES),
        name="sample_mlstm",
    )(b_i, b_f, qkv, qkv, qkv, og, gates_t, m0_t, n0, c0)


SAMPLE_TILE = 16
SAMPLE_UNROLL = 8


def _sample_swa_kernel(q2_ref, kc_ref, vc_ref, kn_ref, vn_ref, sink_ref, o_ref, ko_ref, vo_ref):
    sink = sink_ref[...] * LOG2E
    w = kc_ref.shape[1]
    tile = range(SAMPLE_TILE)
    q2 = [q2_ref[b] for b in tile]
    k_new = [kn_ref[b:b + 1, :] for b in tile]
    v_new = [vn_ref[b:b + 1, :] for b in tile]
    s_c = [_dot_nt(q2[b], kc_ref[b].astype(BF16)) for b in tile]
    s_n = [jnp.sum(q2[b].astype(F32) * k_new[b], axis=-1, keepdims=True) for b in tile]
    mx = [jnp.maximum(jnp.maximum(jnp.max(s_c[b], axis=-1, keepdims=True), s_n[b]), sink) for b in tile]
    p_c = [jnp.exp2(s_c[b] - mx[b]) for b in tile]
    p_n = [jnp.exp2(s_n[b] - mx[b]) for b in tile]
    denom = [jnp.sum(p_c[b], axis=-1, keepdims=True) + p_n[b] + jnp.exp2(sink - mx[b]) for b in tile]
    o = [_dot(p_c[b].astype(BF16), vc_ref[b].astype(BF16)) for b in tile]
    for b in tile:
        o_ref[b] = (o[b] + p_n[b] * v_new[b]) / denom[b]
        ko_ref[b, 0:w - 1, :] = kc_ref[b, 1:w, :]
        ko_ref[b, w - 1:w, :] = k_new[b]
        vo_ref[b, 0:w - 1, :] = vc_ref[b, 1:w, :]
        vo_ref[b, w - 1:w, :] = v_new[b]


def _sample_swa(q2, k_cache, v_cache, k_new, v_new, sink_col):
    nb, w, _ = k_cache.shape
    t3 = lambda a, c: pl.BlockSpec((SAMPLE_TILE, a, c), lambda i: (i, 0, 0))
    t2 = pl.BlockSpec((SAMPLE_TILE, KV_WIDTH), lambda i: (i, 0))
    out_shape = (
        jax.ShapeDtypeStruct((nb, A_HEADS, KV_WIDTH), F32),
        jax.ShapeDtypeStruct((nb, w, KV_WIDTH), F32),
        jax.ShapeDtypeStruct((nb, w, KV_WIDTH), F32),
    )
    return pl.pallas_call(
        _sample_swa_kernel,
        out_shape=out_shape,
        grid=(nb // SAMPLE_TILE,),
        in_specs=[t3(A_HEADS, KV_WIDTH), t3(w, KV_WIDTH), t3(w, KV_WIDTH), t2, t2,
                  pl.BlockSpec((A_HEADS, 1), lambda i: (0, 0))],
        out_specs=(t3(A_HEADS, KV_WIDTH), t3(w, KV_WIDTH), t3(w, KV_WIDTH)),
        compiler_params=pltpu.CompilerParams(dimension_semantics=("arbitrary",)),
        name="sample_swa",
    )(q2, k_cache, v_cache, k_new, v_new, sink_col)


def _prep_weights(ffn1_w_gate, ffn1_w_up, ffn1_w_down, w_in, w_out, ffn2_w_gate, ffn2_w_up, ffn2_w_down):
    cols = rows = lambda w: w.astype(BF16)

    m_end = 4 * M_WIDTH
    g_end = m_end + 2 * M_HEADS
    gates = jnp.pad(w_in[:, m_end:g_end], ((0, 0), (0, GATE_PAD - 2 * M_HEADS)))
    w_in_r = jnp.concatenate([w_in[:, :m_end], w_in[:, g_end:], gates], axis=1).astype(BF16)
    assert w_in_r.shape[1] == W_IN_COLS
    return (cols(ffn1_w_gate), cols(ffn1_w_up), rows(ffn1_w_down), w_in_r, w_out.astype(BF16),
            cols(ffn2_w_gate), cols(ffn2_w_up), rows(ffn2_w_down))


def kernel(x_prompt, x_sample, cache_swa_k, cache_swa_v, state_mlstm_C, state_mlstm_n, state_mlstm_m,
           ffn1_norm, ffn1_w_gate, ffn1_w_up, ffn1_w_down, mix_norm, w_in, mlstm_b_i, mlstm_b_f,
           mlstm_out_norm, swa_q_norm, swa_k_norm, swa_sinks, w_out, ffn2_norm, ffn2_w_gate,
           ffn2_w_up, ffn2_w_down):
    depth = ffn1_norm.shape[0]
    assert depth == 1
    batch, seq, _ = x_prompt.shape
    nb = x_sample.shape[0]
    assert x_sample.shape[1] == 1 and seq % BLOCK == 0

    (wg1, wu1, wd1, w_in_r, wo, wg2, wu2, wd2) = _prep_weights(
        ffn1_w_gate[0], ffn1_w_up[0], ffn1_w_down[0], w_in[0], w_out[0],
        ffn2_w_gate[0], ffn2_w_up[0], ffn2_w_down[0])
    n1 = ffn1_norm[0].reshape(1, D_MODEL)
    n2 = mix_norm[0].reshape(1, D_MODEL)
    n3 = ffn2_norm[0].reshape(1, D_MODEL)
    qgain = jnp.tile(swa_q_norm[0], A_HEADS).reshape(1, A_WIDTH)
    kgain = jnp.tile(swa_k_norm[0], KV_HEADS).reshape(1, KV_WIDTH)
    ogain = mlstm_out_norm[0].reshape(1, M_WIDTH)
    b_i, b_f = mlstm_b_i[0], mlstm_b_f[0]
    bias8 = jnp.concatenate([b_i, b_f]).reshape(2 * M_HEADS, 1)
    front_params = (n1, wg1, wu1, wd1, n2, w_in_r, qgain, kgain, ogain, bias8)
    back_params = (wo, n3, wg2, wu2, wd2)
    sinks = swa_sinks[0]

    xp = x_prompt.reshape(batch * seq, D_MODEL)
    xs = x_sample.reshape(nb, D_MODEL)
    prompt_front, sample_front = _front(xp, xs, front_params, tm=512, seq=seq)
    x1, qkv, og, qa, ka, va, vt, kx, vx, rows, cols, pm = prompt_front
    x1s, qkvs, ogs, qas, kas, vas = sample_front[:6]
    gates_t = sample_front[-1]

    rows = rows.reshape(batch, seq // BLOCK, 3, 8, BLOCK)
    y, pc, pn = _prompt_mixer(sinks, qkv, vt, og, cols, rows, qa, kx, vx, batch, seq)
    pk = ka.reshape(batch, WINDOW, KV_HEADS, A_DIM)
    pv = va.reshape(batch, WINDOW, KV_HEADS, A_DIM)
    pn = pn[:, :, 0, :]
    pm = pm[:, 0:M_HEADS, 0]

    y_m, sn, mt, sc = _sample_mlstm(b_i, b_f, qkvs, ogs, gates_t, jnp.transpose(state_mlstm_m[0]),
                                    state_mlstm_n[0].reshape(nb, M_WIDTH), state_mlstm_C[0])
    sn = sn.reshape(nb, M_HEADS, M_DIM)
    sm = jnp.transpose(mt[:, 0, :])

    qa_h = qas.reshape(nb, A_HEADS, A_DIM)
    zeros = jnp.zeros_like(qa_h)
    in_lo = (jnp.arange(A_HEADS) // A_GROUP == 0)[None, :, None]
    q2 = jnp.concatenate([jnp.where(in_lo, qa_h, zeros), jnp.where(in_lo, zeros, qa_h)], axis=-1)
    kc = cache_swa_k[0].reshape(nb, -1, KV_WIDTH)
    vc = cache_swa_v[0].reshape(nb, -1, KV_WIDTH)
    o2, sk, sv = _sample_swa(q2, kc, vc, kas, vas, sinks.reshape(A_HEADS, 1))
    o2 = o2.reshape(nb, A_HEADS, KV_HEADS, A_DIM)
    y_a = jnp.where(in_lo, o2[:, :, 0, :], o2[:, :, 1, :]).reshape(nb, A_WIDTH)
    ys_in = jnp.concatenate([y_m, y_a], axis=-1).astype(BF16)

    yp, ys = _back(x1, y, x1s, ys_in, back_params, tm=1024)
    yp = yp.reshape(batch, seq, D_MODEL)
    ys = ys.reshape(nb, 1, D_MODEL)

    wb = kc.shape[1]
    return (yp, ys, pk[None], pv[None], pc[None], pn[None], pm[None],
            sk.reshape(1, nb, wb, KV_HEADS, A_DIM), sv.reshape(1, nb, wb, KV_HEADS, A_DIM),
            sc[None], sn[None], sm[None])
```

```python
import functools

import jax
import jax.numpy as jnp
from jax import lax
from jax.experimental import pallas as pl
from jax.experimental.pallas import tpu as pltpu

F32 = jnp.float32
BF16 = jnp.bfloat16

D_MODEL = 1024
D_FF = 2816
FF_CHUNK = 256
N_FF_CHUNKS = D_FF // FF_CHUNK
M_HEADS = 4
M_DIM = 128
M_WIDTH = M_HEADS * M_DIM
A_HEADS = 8
A_DIM = 64
A_WIDTH = A_HEADS * A_DIM
KV_HEADS = 2
KV_WIDTH = KV_HEADS * A_DIM
A_GROUP = A_HEADS // KV_HEADS
WINDOW = 128
BLOCK = 128
GATE_PAD = 128
RMS_EPS = 1e-6
FFN_RES_WEIGHT = 0.5
NEG_INF = float("-inf")
LOG2E = 1.4426950408889634
VMEM_LIMIT_BYTES = 56 * 1024 * 1024


def _dot(a, b):
    return jnp.dot(a, b, preferred_element_type=F32)


def _dot_nt(a, b):
    return lax.dot_general(a, b, (((1,), (1,)), ((), ())), preferred_element_type=F32)


def _rms_rows(x, gain):
    ms = jnp.mean(x * x, axis=-1, keepdims=True)
    return x * lax.rsqrt(ms + RMS_EPS) * gain


def _log_sigmoid(x):
    return jnp.minimum(x, 0.0) - jnp.log1p(jnp.exp(-jnp.abs(x)))


def _half_tile_mean_sq(x):
    in_lo = lax.broadcasted_iota(jnp.int32, (x.shape[0], 128), 1) < A_DIM
    out = []
    for c in range(x.shape[1] // 128):
        sq = x[:, c * 128:(c + 1) * 128]
        sq = sq * sq
        s_lo = jnp.sum(jnp.where(in_lo, sq, 0.0), axis=-1, keepdims=True)
        s_hi = jnp.sum(jnp.where(in_lo, 0.0, sq), axis=-1, keepdims=True)
        out.append(jnp.where(in_lo, s_lo, s_hi) * (1.0 / A_DIM))
    return out[0] if len(out) == 1 else jnp.concatenate(out, axis=1)


def _ffn(h_ref, wg_ref, wu_ref, wd_ref, act_ref):
    for c in range(N_FF_CHUNKS):
        lo, hi = c * FF_CHUNK, (c + 1) * FF_CHUNK
        h = h_ref[...]
        g = _dot(h, wg_ref[:, lo:hi])
        u = _dot(h, wu_ref[:, lo:hi])
        act_ref[:, lo:hi] = (g * jax.nn.sigmoid(g) * u).astype(BF16)
    return _dot(act_ref[...], wd_ref[...])


N_FRONT_PARAMS = 10
N_FRONT_COMMON_OUTS = 9
N_PROMPT_GATE_OUTS = 3
W_QA = 4 * M_WIDTH
W_KVG = W_QA + A_WIDTH
W_IN_COLS = W_KVG + 2 * KV_WIDTH + GATE_PAD


def _prompt_gates(gates_t, bias_ref, first_of_seq, m_ref, rows_ref, cols_ref, mfin_ref):
    sub = lax.broadcasted_iota(jnp.int32, (8, BLOCK), 0)
    lane = lax.broadcasted_iota(jnp.int32, (8, BLOCK), 1)
    is_head = sub < M_HEADS
    pad = jnp.zeros((BLOCK - 24, BLOCK), F32)
    scan_shifts = (1, 2, 4, 8, 16, 32, 64)
    m_prev = jnp.where(first_of_seq, 0.0, m_ref[...])[:, 0:1]
    for c in range(rows_ref.shape[0]):
        pre = gates_t[:, c * BLOCK:(c + 1) * BLOCK] + bias_ref[...]
        r = jnp.where(is_head, pre, _log_sigmoid(pre))
        cum = r
        for shift in scan_shifts:
            cum = cum + jnp.where(lane >= shift, pltpu.roll(cum, shift, 1), 0.0)
        bcum = pltpu.roll(cum, M_HEADS, 0)
        g = jnp.where(is_head, r - bcum, 0.0)
        bcum = jnp.where(is_head, bcum, 0.0)
        cm = g
        for shift in scan_shifts:
            cm = jnp.maximum(cm, jnp.where(lane >= shift, pltpu.roll(cm, shift, 1), NEG_INF))
        cm_last = jnp.max(cm, axis=-1, keepdims=True)
        b_last = jnp.sum(jnp.where(lane == BLOCK - 1, bcum, 0.0), axis=-1, keepdims=True)
        mx = jnp.maximum(m_prev, cm)
        mx_last = jnp.maximum(m_prev, cm_last)
        rows_ref[c, 0] = g * LOG2E
        rows_ref[c, 1] = jnp.exp(g - mx_last)
        rows_ref[c, 2] = jnp.broadcast_to(jnp.exp(m_prev - mx_last), (8, BLOCK))
        col_src = jnp.concatenate([mx * -LOG2E, jnp.exp(m_prev - mx), jnp.exp(-(bcum + mx)), pad], axis=0)
        cols_ref[c * BLOCK:(c + 1) * BLOCK, :] = col_src.T
        m_prev = b_last + mx_last
    m_full = jnp.broadcast_to(m_prev, (8, BLOCK))
    m_ref[...] = m_full
    mfin_ref[0] = m_full


def _front_tile(x_ref, params, outs, gate_sink, h_ref, act_ref):
    (n1_ref, wg_ref, wu_ref, wd_ref, n2_ref, win_ref, qgain_ref, kgain_ref, ogain_ref, _) = params
    (x1_ref, qkv_ref, og_ref, qa_ref, ka_ref, va_ref, vt_ref, kx_ref, vx_ref) = outs
    wm_ref = win_ref
    x = x_ref[...]
    h_ref[...] = _rms_rows(x, n1_ref[...]).astype(BF16)
    x1 = x + FFN_RES_WEIGHT * _ffn(h_ref, wg_ref, wu_ref, wd_ref, act_ref)
    x1_ref[...] = x1
    h_ref[...] = _rms_rows(x1, n2_ref[...]).astype(BF16)
    h = h_ref[...]

    qa = _dot(h, win_ref[:, W_QA:W_KVG])
    kvg = _dot(h, win_ref[:, W_KVG:W_IN_COLS])
    v_m = _dot(h, wm_ref[:, 2 * M_WIDTH:3 * M_WIDTH])
    qkv_ref[:, 2 * M_WIDTH:3 * M_WIDTH] = v_m.astype(BF16)
    for c in range(vt_ref.shape[0]):
        for hd in range(M_HEADS):
            blk = v_m[c * BLOCK:(c + 1) * BLOCK, hd * M_DIM:(hd + 1) * M_DIM]
            vt_ref[c, hd] = blk.T.astype(BF16)
    q_scale = qgain_ref[...] * (A_DIM ** -0.5 * LOG2E)
    qa_ref[...] = (qa * lax.rsqrt(_half_tile_mean_sq(qa) + RMS_EPS) * q_scale).astype(BF16)
    ka = kvg[:, 0:KV_WIDTH]
    ka = ka * lax.rsqrt(_half_tile_mean_sq(ka) + RMS_EPS) * kgain_ref[...]
    va = kvg[:, KV_WIDTH:2 * KV_WIDTH]
    keep = ka_ref.shape[0]
    ka_ref[...] = ka[ka.shape[0] - keep:, :]
    va_ref[...] = va[va.shape[0] - keep:, :]
    gate_sink(kvg[:, 2 * KV_WIDTH:2 * KV_WIDTH + GATE_PAD].T[0:2 * M_HEADS, :])

    in_lo = lax.broadcasted_iota(jnp.int32, ka.shape, 1) < A_DIM
    for src, dst in ((ka, kx_ref), (va, vx_ref)):
        x0 = jnp.where(in_lo, src, 0.0)
        x1 = jnp.where(in_lo, 0.0, src)
        dst[:, 0:128] = x0.astype(BF16)
        dst[:, 128:256] = pltpu.roll(x0, A_DIM, 1).astype(BF16)
        dst[:, 256:384] = pltpu.roll(x1, A_DIM, 1).astype(BF16)
        dst[:, 384:512] = x1.astype(BF16)

    og_ref[...] = jax.nn.sigmoid(_dot(h, wm_ref[:, 3 * M_WIDTH:4 * M_WIDTH])) * ogain_ref[...]
    k_m = _dot(h, wm_ref[:, M_WIDTH:2 * M_WIDTH]) * (M_DIM ** -0.5)
    qkv_ref[:, M_WIDTH:2 * M_WIDTH] = k_m.astype(BF16)
    qkv_ref[:, 0:M_WIDTH] = _dot(h, wm_ref[:, 0:M_WIDTH]).astype(BF16)


def _front_kernel(*refs, n_tiles, tiles_per_seq):
    refs = list(refs)
    take = lambda k: [refs.pop(0) for _ in range(k)]
    x_ref, xs_ref = take(2)
    params = take(N_FRONT_PARAMS)
    outs_p = take(N_FRONT_COMMON_OUTS)
    rows_ref, cols_ref, mfin_ref = take(N_PROMPT_GATE_OUTS)
    outs_s = take(N_FRONT_COMMON_OUTS)
    (gts_ref,) = take(1)
    h_ref, act_ref, m_ref = refs
    bias_ref = params[-1]
    i = pl.program_id(0)

    @pl.when(i < n_tiles)
    def _():
        gates = functools.partial(_prompt_gates, bias_ref=bias_ref, first_of_seq=i % tiles_per_seq == 0,
                                  m_ref=m_ref, rows_ref=rows_ref, cols_ref=cols_ref, mfin_ref=mfin_ref)
        _front_tile(x_ref, params, outs_p, gates, h_ref, act_ref)

    @pl.when(i == n_tiles)
    def _():
        ns = xs_ref.shape[0]

        def raw_gates(gates_t):
            gts_ref[...] = gates_t

        _front_tile(xs_ref, params, outs_s, raw_gates, h_ref.at[0:ns], act_ref.at[0:ns])


def _const_spec(shape):
    nd = len(shape)
    return pl.BlockSpec(shape, lambda i: (0,) * nd, pipeline_mode=pl.Buffered(1))


def _whole_spec(shape):
    nd = len(shape)
    return pl.BlockSpec(shape, lambda i: (0,) * nd)


def _front_out_shapes(n, n_cache_rows):
    return (
        jax.ShapeDtypeStruct((n, D_MODEL), F32),
        jax.ShapeDtypeStruct((n, 3 * M_WIDTH), BF16),
        jax.ShapeDtypeStruct((n, M_WIDTH), F32),
        jax.ShapeDtypeStruct((n, A_WIDTH), BF16),
        jax.ShapeDtypeStruct((n_cache_rows, KV_WIDTH), F32),
        jax.ShapeDtypeStruct((n_cache_rows, KV_WIDTH), F32),
        jax.ShapeDtypeStruct((n // BLOCK, M_HEADS, M_DIM, BLOCK), BF16),
        jax.ShapeDtypeStruct((n, 4 * KV_WIDTH), BF16),
        jax.ShapeDtypeStruct((n, 4 * KV_WIDTH), BF16),
    )


def _front(x2d, xs2d, params, tm, seq):
    n, ns = x2d.shape[0], xs2d.shape[0]
    assert seq % tm == 0 and tm >= WINDOW and len(params) == N_FRONT_PARAMS
    n_tiles = n // tm
    n_seqs = n // seq
    tiles_per_seq = seq // tm
    nb_t = tm // BLOCK
    tile = lambda i: jnp.minimum(i, n_tiles - 1)
    seq_of = lambda i: tile(i) // tiles_per_seq
    row = lambda w: pl.BlockSpec((tm, w), lambda i: (tile(i), 0))
    tail = pl.BlockSpec((WINDOW, KV_WIDTH), lambda i: (seq_of(i), 0))
    prompt_specs = (row(D_MODEL), row(3 * M_WIDTH), row(M_WIDTH), row(A_WIDTH), tail, tail,
                    pl.BlockSpec((nb_t, M_HEADS, M_DIM, BLOCK), lambda i: (tile(i), 0, 0, 0)),
                    row(4 * KV_WIDTH), row(4 * KV_WIDTH),
                    pl.BlockSpec((nb_t, 3, 8, BLOCK), lambda i: (tile(i), 0, 0, 0)), row(128),
                    pl.BlockSpec((1, 8, BLOCK), lambda i: (seq_of(i), 0, 0)))
    prompt_shapes = _front_out_shapes(n, n_seqs * WINDOW) + (
        jax.ShapeDtypeStruct((n // BLOCK, 3, 8, BLOCK), F32),
        jax.ShapeDtypeStruct((n, 128), F32),
        jax.ShapeDtypeStruct((n_seqs, 8, BLOCK), F32),
    )
    sample_shapes = _front_out_shapes(ns, ns) + (
        jax.ShapeDtypeStruct((2 * M_HEADS, ns), F32),
    )
    outs = pl.pallas_call(
        functools.partial(_front_kernel, n_tiles=n_tiles, tiles_per_seq=tiles_per_seq),
        out_shape=prompt_shapes + sample_shapes,
        grid=(n_tiles + 1,),
        in_specs=[row(D_MODEL), _whole_spec(xs2d.shape)] + [_const_spec(p.shape) for p in params],
        out_specs=prompt_specs + tuple(_whole_spec(s.shape) for s in sample_shapes),
        scratch_shapes=[pltpu.VMEM((tm, D_MODEL), BF16), pltpu.VMEM((tm, D_FF), BF16),
                        pltpu.VMEM((8, BLOCK), F32)],
        compiler_params=pltpu.CompilerParams(dimension_semantics=("arbitrary",),
                                             vmem_limit_bytes=VMEM_LIMIT_BYTES),
        name="front",
    )(x2d, xs2d, *params)
    return outs[:len(prompt_shapes)], outs[len(prompt_shapes):]


def _back_tile(x1_ref, y_ref, params, out_ref, h_ref, act_ref):
    wo_ref, n_ref, wg_ref, wu_ref, wd_ref = params
    x2 = x1_ref[...] + _dot(y_ref[...], wo_ref[...])
    h_ref[...] = _rms_rows(x2, n_ref[...]).astype(BF16)
    out_ref[...] = x2 + FFN_RES_WEIGHT * _ffn(h_ref, wg_ref, wu_ref, wd_ref, act_ref)


def _back_kernel(x1_ref, y_ref, x1s_ref, ys_ref, wo_ref, n_ref, wg_ref, wu_ref, wd_ref, out_ref, outs_ref,
                 h_ref, act_ref, *, n_tiles):
    params = (wo_ref, n_ref, wg_ref, wu_ref, wd_ref)
    i = pl.program_id(0)

    @pl.when(i < n_tiles)
    def _():
        _back_tile(x1_ref, y_ref, params, out_ref, h_ref, act_ref)

    @pl.when(i == n_tiles)
    def _():
        ns = x1s_ref.shape[0]
        _back_tile(x1s_ref, ys_ref, params, outs_ref, h_ref.at[0:ns], act_ref.at[0:ns])


def _back(x1, y, x1s, ys, params, tm):
    n, ns = x1.shape[0], x1s.shape[0]
    n_tiles = n // tm
    row = pl.BlockSpec((tm, D_MODEL), lambda i: (jnp.minimum(i, n_tiles - 1), 0))
    return pl.pallas_call(
        functools.partial(_back_kernel, n_tiles=n_tiles),
        out_shape=(jax.ShapeDtypeStruct((n, D_MODEL), F32), jax.ShapeDtypeStruct((ns, D_MODEL), F32)),
        grid=(n_tiles + 1,),
        in_specs=[row, row, _whole_spec(x1s.shape), _whole_spec(ys.shape)] + [_const_spec(p.shape) for p in params],
        out_specs=(row, _whole_spec((ns, D_MODEL))),
        scratch_shapes=[pltpu.VMEM((tm, D_MODEL), BF16), pltpu.VMEM((tm, D_FF), BF16)],
        compiler_params=pltpu.CompilerParams(dimension_semantics=("arbitrary",),
                                             vmem_limit_bytes=VMEM_LIMIT_BYTES),
        name="back",
    )(x1, y, x1s, ys, *params)


def _prompt_mixer_kernel(sinks_ref, qkv_ref, vt_ref, og_ref, cols_ref, rows_ref, qa_ref,
                         kx_ref, vx_ref, kxp_ref, vxp_ref,
                         y_ref, c_out_ref, n_out_ref,
                         c_ref, n_ref):
    j = pl.program_id(0)
    batch = qkv_ref.shape[0]

    @pl.when(j == 0)
    def _():
        c_ref[...] = jnp.zeros_like(c_ref)
        n_ref[...] = jnp.zeros_like(n_ref)

    rows = lax.broadcasted_iota(jnp.int32, (BLOCK, BLOCK), 0)
    cols = lax.broadcasted_iota(jnp.int32, (BLOCK, BLOCK), 1)
    causal = cols <= rows
    lane_lo = lax.broadcasted_iota(jnp.int32, (2 * BLOCK, KV_WIDTH), 1) < A_DIM
    ones_m = jnp.ones((BLOCK, M_DIM), BF16)
    ones_half = (jnp.where(lane_lo, 1.0, 0.0).astype(BF16), jnp.where(lane_lo, 0.0, 1.0).astype(BF16))
    qi = lax.broadcasted_iota(jnp.int32, (2 * BLOCK, 2 * BLOCK), 0) % BLOCK
    kc = lax.broadcasted_iota(jnp.int32, (2 * BLOCK, 2 * BLOCK), 1)
    first_valid = jnp.where(j == 0, BLOCK, 0)
    valid = (kc >= qi) & (kc <= qi + WINDOW) & (kc >= first_valid)
    top_rows = lax.broadcasted_iota(jnp.int32, (2 * BLOCK, 1), 0) < BLOCK

    heads = range(M_HEADS)
    groups = [(kv, parity) for kv in range(KV_HEADS) for parity in range(2)]
    m_lo = lambda h: h * M_DIM

    for b in range(batch):
        col_b = cols_ref[b]
        q = [qkv_ref[b, :, m_lo(h):m_lo(h) + M_DIM] for h in heads]
        k = [qkv_ref[b, :, M_WIDTH + m_lo(h):M_WIDTH + m_lo(h) + M_DIM] for h in heads]
        v = [qkv_ref[b, :, 2 * M_WIDTH + m_lo(h):2 * M_WIDTH + m_lo(h) + M_DIM] for h in heads]
        g_row = [rows_ref[b, 0, 0][h:h + 1, :] for h in heads]
        wend_row = [rows_ref[b, 0, 1][h:h + 1, :] for h in heads]
        decay = [rows_ref[b, 0, 2][h:h + 1, :] for h in heads]
        c_prev = [c_ref[b, h] for h in heads]
        n_prev = [n_ref[b, h] for h in heads]

        qkc, upd = [], []
        for h in heads:
            n_rep = jnp.broadcast_to(n_prev[h][0:1, :], (BLOCK, M_DIM)).astype(BF16)
            rhs = jnp.concatenate([k[h], c_prev[h].astype(BF16), n_rep], axis=0)
            qkc.append(_dot_nt(q[h], rhs))
        for h in heads:
            vw_t = (vt_ref[b, 0, h].astype(F32) * wend_row[h]).astype(BF16)
            w_rep = jnp.broadcast_to(wend_row[h], (16, BLOCK)).astype(BF16)
            upd.append(_dot(jnp.concatenate([vw_t, w_rep], axis=0), k[h]))
        q2 = [jnp.concatenate([qa_ref[b, :, (2 * kv) * 128:(2 * kv + 1) * 128],
                               qa_ref[b, :, (2 * kv + 1) * 128:(2 * kv + 2) * 128]], axis=0)
              for kv in range(KV_HEADS)]
        sc = []
        for kv, parity in groups:
            var = (2 * kv + parity) * KV_WIDTH
            k_band = jnp.concatenate([kxp_ref[b, :, var:var + KV_WIDTH], kx_ref[b, :, var:var + KV_WIDTH]], axis=0)
            sc.append(_dot_nt(q2[kv], k_band))

        s = []
        for h in heads:
            d = jnp.where(causal, jnp.exp2(col_b[:, h:h + 1] + g_row[h]), 0.0)
            s.append((qkc[h][:, 0:BLOCK] * d).astype(BF16))
            c_ref[b, h] = decay[h] * c_prev[h] + upd[h][0:M_DIM]
            n_ref[b, h] = decay[h] * n_prev[h] + upd[h][M_DIM:M_DIM + 8]
        pr, e_sink = [], []
        for gi, (kv, parity) in enumerate(groups):
            sink = jnp.where(top_rows, sinks_ref[4 * kv + parity], sinks_ref[4 * kv + 2 + parity]) * LOG2E
            scm = jnp.where(valid, sc[gi], NEG_INF)
            mx = jnp.maximum(jnp.max(scm, axis=-1, keepdims=True), sink)
            pr.append(jnp.exp2(scm - mx).astype(BF16))
            e_sink.append(jnp.exp2(sink - mx))

        sv = [_dot(s[h], jnp.concatenate([v[h], ones_m], axis=1)) for h in heads]
        pv = []
        for gi, (kv, parity) in enumerate(groups):
            var = (2 * kv + parity) * KV_WIDTH
            v_band = jnp.concatenate([vxp_ref[b, :, var:var + KV_WIDTH], vx_ref[b, :, var:var + KV_WIDTH]], axis=0)
            pv.append(_dot(pr[gi], jnp.concatenate([v_band, ones_half[parity]], axis=1)))

        for h in heads:
            lo, hi = m_lo(h), m_lo(h) + M_DIM
            wi_col = col_b[:, 8 + h:9 + h]
            num = wi_col * qkc[h][:, BLOCK:2 * BLOCK] + sv[h][:, 0:M_DIM]
            den = wi_col * qkc[h][:, 2 * BLOCK:3 * BLOCK] + sv[h][:, M_DIM:2 * M_DIM]
            hh = num / jnp.maximum(jnp.abs(den), col_b[:, 16 + h:17 + h])
            hh = hh * lax.rsqrt(jnp.mean(hh * hh, axis=-1, keepdims=True) + RMS_EPS)
            y_ref[b, :, lo:hi] = (hh * og_ref[b, :, lo:hi]).astype(BF16)
        for kv in range(KV_HEADS):
            acc = pv[2 * kv] + pv[2 * kv + 1]
            denom = acc[:, KV_WIDTH:] + jnp.where(lane_lo, e_sink[2 * kv], e_sink[2 * kv + 1])
            ya = (acc[:, 0:KV_WIDTH] / denom).astype(BF16)
            p0, p1 = 2 * kv, 2 * kv + 1
            y_ref[b, :, M_WIDTH + p0 * 128:M_WIDTH + (p0 + 1) * 128] = ya[0:BLOCK]
            y_ref[b, :, M_WIDTH + p1 * 128:M_WIDTH + (p1 + 1) * 128] = ya[BLOCK:2 * BLOCK]

    @pl.when(j == pl.num_programs(0) - 1)
    def _():
        c_out_ref[...] = c_ref[...]
        n_out_ref[...] = n_ref[...]


def _prompt_mixer(sinks, qkv, vt, og, cols, rows, qa, kx, vx, batch, seq):
    nblk = seq // BLOCK
    r3 = lambda a: a.reshape(batch, seq, a.shape[-1])
    cur = lambda w: pl.BlockSpec((batch, BLOCK, w), lambda j: (0, j, 0))
    prev = lambda w: pl.BlockSpec((batch, BLOCK, w), lambda j: (0, jnp.maximum(j - 1, 0), 0))
    state = lambda shape: pl.BlockSpec((batch,) + shape, lambda j: (0,) * (len(shape) + 1))
    out_shape = (
        jax.ShapeDtypeStruct((batch, seq, D_MODEL), BF16),
        jax.ShapeDtypeStruct((batch, M_HEADS, M_DIM, M_DIM), F32),
        jax.ShapeDtypeStruct((batch, M_HEADS, 8, M_DIM), F32),
    )
    y, pc, pn = pl.pallas_call(
        _prompt_mixer_kernel,
        out_shape=out_shape,
        grid=(nblk,),
        in_specs=[pl.BlockSpec(memory_space=pltpu.SMEM),
                  cur(3 * M_WIDTH),
                  pl.BlockSpec((batch, 1, M_HEADS, M_DIM, BLOCK), lambda j: (0, j, 0, 0, 0)),
                  cur(M_WIDTH), cur(128),
                  pl.BlockSpec((batch, 1, 3, 8, BLOCK), lambda j: (0, j, 0, 0, 0)),
                  cur(A_WIDTH), cur(4 * KV_WIDTH), cur(4 * KV_WIDTH), prev(4 * KV_WIDTH), prev(4 * KV_WIDTH)],
        out_specs=(cur(D_MODEL), state((M_HEADS, M_DIM, M_DIM)), state((M_HEADS, 8, M_DIM))),
        scratch_shapes=[pltpu.VMEM((batch, M_HEADS, M_DIM, M_DIM), F32),
                        pltpu.VMEM((batch, M_HEADS, 8, M_DIM), F32)],
        compiler_params=pltpu.CompilerParams(dimension_semantics=("arbitrary",),
                                             vmem_limit_bytes=VMEM_LIMIT_BYTES),
        name="prompt_mixer",
    )(sinks, r3(qkv), vt.reshape(batch, nblk, M_HEADS, M_DIM, BLOCK), r3(og), r3(cols), rows, r3(qa),
      r3(kx), r3(vx), r3(kx), r3(vx))
    return y.reshape(batch * seq, D_MODEL), pc, pn


def _sample_mlstm_kernel(bi_ref, bf_ref, q_ref, k_ref, v_ref, og_ref, gates_ref, m0_ref, n0_ref, c_ref,
                         y_ref, n_out_ref, m_out_ref, c_out_ref, decay_ref, qr_ref):
    h = pl.program_id(0)
    nb = q_ref.shape[0]
    q_rows = q_ref[...].astype(F32)
    k_rows = k_ref[...].astype(F32)
    qr_ref[...] = q_rows
    qt, kt, vt = q_rows.T, k_rows.T, v_ref[...].astype(F32).T
    i_pre = gates_ref[pl.ds(h, 1), :] + bi_ref[h]
    a = _log_sigmoid(gates_ref[pl.ds(M_HEADS + h, 1), :] + bf_ref[h]) + m0_ref[pl.ds(h, 1), :]
    m_t = jnp.maximum(a, i_pre)
    w_inter = jnp.exp(a - m_t)
    w_in = jnp.exp(i_pre - m_t)
    scores = jnp.sum(qt * kt, axis=0, keepdims=True) * w_in
    n0t = n0_ref[...].T
    nq = jnp.sum(n0t * qt, axis=0, keepdims=True)

    rows = lax.broadcasted_iota(jnp.int32, (nb, M_DIM), 0)
    cols = lax.broadcasted_iota(jnp.int32, (M_DIM, nb), 1)
    vw_t = (vt * w_in).astype(BF16)

    decay_ref[...] = jnp.broadcast_to(w_inter, (M_DIM, nb)).T

    def body(grp, cq_t):
        base = grp * SAMPLE_UNROLL
        for u in range(SAMPLE_UNROLL):
            b = base + u
            col = jnp.sum(c_ref[b, 0] * qr_ref[pl.ds(b, 1), :], axis=-1, keepdims=True)
            cq_t = jnp.where(cols == b, col, cq_t)
        outer = []
        for u in range(SAMPLE_UNROLL):
            k_only_b = jnp.where(rows == base + u, k_rows, 0.0).astype(BF16)
            outer.append(_dot(vw_t, k_only_b))
        for u in range(SAMPLE_UNROLL):
            b = base + u
            c_out_ref[b, 0] = decay_ref[pl.ds(b, 1), :] * c_ref[b, 0] + outer[u]
        return cq_t

    cq_t = lax.fori_loop(0, nb // SAMPLE_UNROLL, body, jnp.zeros((M_DIM, nb), F32))

    num = w_inter * cq_t + scores * vt
    den = w_inter * nq + scores
    hh = num / jnp.maximum(jnp.abs(den), jnp.exp(-m_t))
    hh = hh * lax.rsqrt(jnp.mean(hh * hh, axis=0, keepdims=True) + RMS_EPS)
    y_ref[...] = hh.T * og_ref[...]
    n_out_ref[...] = (w_inter * n0t + w_in * kt).T
    m_out_ref[0] = m_t


def _sample_mlstm(b_i, b_f, qkv, og, gates_t, m0_t, n0, c0):
    nb = qkv.shape[0]
    smem = pl.BlockSpec(memory_space=pltpu.SMEM)
    head = lambda off: pl.BlockSpec((nb, M_DIM), lambda h: (0, off + h))
    out_shape = (
        jax.ShapeDtypeStruct((nb, M_WIDTH), F32),
        jax.ShapeDtypeStruct((nb, M_WIDTH), F32),
        jax.ShapeDtypeStruct((M_HEADS, 1, nb), F32),
        jax.ShapeDtypeStruct((nb, M_HEADS, M_DIM, M_DIM), F32),
    )
    c_spec = pl.BlockSpec((nb, 1, M_DIM, M_DIM), lambda h: (0, h, 0, 0))
    return pl.pallas_call(
        _sample_mlstm_kernel,
        out_shape=out_shape,
        grid=(M_HEADS,),
        in_specs=[smem, smem, head(0), head(M_HEADS), head(2 * M_HEADS), head(0),
                  pl.BlockSpec(gates_t.shape, lambda h: (0, 0)), pl.BlockSpec(m0_t.shape, lambda h: (0, 0)),
                  head(0), c_spec],
        out_specs=(head(0), head(0), pl.BlockSpec((1, 1, nb), lambda h: (h, 0, 0)), c_spec),
        scratch_shapes=[pltpu.VMEM((nb, M_DIM), F32), pltpu.VMEM((nb, M_DIM), F32)],
        compiler_params=pltpu.CompilerParams(dimension_semantics=("arbitrary",),
                                             vmem_limit_bytes=VMEM_LIMIT_BYTES),
        name="sample_mlstm",
    )(b_i, b_f, qkv, qkv, qkv, og, gates_t, m0_t, n0, c0)


SAMPLE_TILE = 16
SAMPLE_UNROLL = 8


def _sample_swa_kernel(q2_ref, kc_ref, vc_ref, kn_ref, vn_ref, sink_ref, o_ref, ko_ref, vo_ref):
    sink = sink_ref[...] * LOG2E
    w = kc_ref.shape[1]
    tile = range(SAMPLE_TILE)
    q2 = [q2_ref[b] for b in tile]
    k_new = [kn_ref[b:b + 1, :] for b in tile]
    v_new = [vn_ref[b:b + 1, :] for b in tile]
    s_c = [_dot_nt(q2[b], kc_ref[b].astype(BF16)) for b in tile]
    s_n = [jnp.sum(q2[b].astype(F32) * k_new[b], axis=-1, keepdims=True) for b in tile]
    mx = [jnp.maximum(jnp.maximum(jnp.max(s_c[b], axis=-1, keepdims=True), s_n[b]), sink) for b in tile]
    p_c = [jnp.exp2(s_c[b] - mx[b]) for b in tile]
    p_n = [jnp.exp2(s_n[b] - mx[b]) for b in tile]
    denom = [jnp.sum(p_c[b], axis=-1, keepdims=True) + p_n[b] + jnp.exp2(sink - mx[b]) for b in tile]
    o = [_dot(p_c[b].astype(BF16), vc_ref[b].astype(BF16)) for b in tile]
    for b in tile:
        o_ref[b] = (o[b] + p_n[b] * v_new[b]) / denom[b]
        ko_ref[b, 0:w - 1, :] = kc_ref[b, 1:w, :]
        ko_ref[b, w - 1:w, :] = k_new[b]
        vo_ref[b, 0:w - 1, :] = vc_ref[b, 1:w, :]
        vo_ref[b, w - 1:w, :] = v_new[b]


def _sample_swa(q2, k_cache, v_cache, k_new, v_new, sink_col):
    nb, w, _ = k_cache.shape
    t3 = lambda a, c: pl.BlockSpec((SAMPLE_TILE, a, c), lambda i: (i, 0, 0))
    t2 = pl.BlockSpec((SAMPLE_TILE, KV_WIDTH), lambda i: (i, 0))
    out_shape = (
        jax.ShapeDtypeStruct((nb, A_HEADS, KV_WIDTH), F32),
        jax.ShapeDtypeStruct((nb, w, KV_WIDTH), F32),
        jax.ShapeDtypeStruct((nb, w, KV_WIDTH), F32),
    )
    return pl.pallas_call(
        _sample_swa_kernel,
        out_shape=out_shape,
        grid=(nb // SAMPLE_TILE,),
        in_specs=[t3(A_HEADS, KV_WIDTH), t3(w, KV_WIDTH), t3(w, KV_WIDTH), t2, t2,
                  pl.BlockSpec((A_HEADS, 1), lambda i: (0, 0))],
        out_specs=(t3(A_HEADS, KV_WIDTH), t3(w, KV_WIDTH), t3(w, KV_WIDTH)),
        compiler_params=pltpu.CompilerParams(dimension_semantics=("arbitrary",)),
        name="sample_swa",
    )(q2, k_cache, v_cache, k_new, v_new, sink_col)


def _prep_weights(ffn1_w_gate, ffn1_w_up, ffn1_w_down, w_in, w_out, ffn2_w_gate, ffn2_w_up, ffn2_w_down):
    cols = rows = lambda w: w.astype(BF16)

    m_end = 4 * M_WIDTH
    g_end = m_end + 2 * M_HEADS
    gates = jnp.pad(w_in[:, m_end:g_end], ((0, 0), (0, GATE_PAD - 2 * M_HEADS)))
    w_in_r = jnp.concatenate([w_in[:, :m_end], w_in[:, g_end:], gates], axis=1).astype(BF16)
    assert w_in_r.shape[1] == W_IN_COLS
    return (cols(ffn1_w_gate), cols(ffn1_w_up), rows(ffn1_w_down), w_in_r, w_out.astype(BF16),
            cols(ffn2_w_gate), cols(ffn2_w_up), rows(ffn2_w_down))


def kernel(x_prompt, x_sample, cache_swa_k, cache_swa_v, state_mlstm_C, state_mlstm_n, state_mlstm_m,
           ffn1_norm, ffn1_w_gate, ffn1_w_up, ffn1_w_down, mix_norm, w_in, mlstm_b_i, mlstm_b_f,
           mlstm_out_norm, swa_q_norm, swa_k_norm, swa_sinks, w_out, ffn2_norm, ffn2_w_gate,
           ffn2_w_up, ffn2_w_down):
    depth = ffn1_norm.shape[0]
    assert depth == 1
    batch, seq, _ = x_prompt.shape
    nb = x_sample.shape[0]
    assert x_sample.shape[1] == 1 and seq % BLOCK == 0

    (wg1, wu1, wd1, w_in_r, wo, wg2, wu2, wd2) = _prep_weights(
        ffn1_w_gate[0], ffn1_w_up[0], ffn1_w_down[0], w_in[0], w_out[0],
        ffn2_w_gate[0], ffn2_w_up[0], ffn2_w_down[0])
    n1 = ffn1_norm[0].reshape(1, D_MODEL)
    n2 = mix_norm[0].reshape(1, D_MODEL)
    n3 = ffn2_norm[0].reshape(1, D_MODEL)
    qgain = jnp.tile(swa_q_norm[0], A_HEADS).reshape(1, A_WIDTH)
    kgain = jnp.tile(swa_k_norm[0], KV_HEADS).reshape(1, KV_WIDTH)
    ogain = mlstm_out_norm[0].reshape(1, M_WIDTH)
    b_i, b_f = mlstm_b_i[0], mlstm_b_f[0]
    bias8 = jnp.concatenate([b_i, b_f]).reshape(2 * M_HEADS, 1)
    front_params = (n1, wg1, wu1, wd1, n2, w_in_r, qgain, kgain, ogain, bias8)
    back_params = (wo, n3, wg2, wu2, wd2)
    sinks = swa_sinks[0]

    xp = x_prompt.reshape(batch * seq, D_MODEL)
    xs = x_sample.reshape(nb, D_MODEL)
    prompt_front, sample_front = _front(xp, xs, front_params, tm=512, seq=seq)
    x1, qkv, og, qa, ka, va, vt, kx, vx, rows, cols, pm = prompt_front
    x1s, qkvs, ogs, qas, kas, vas = sample_front[:6]
    gates_t = sample_front[-1]

    rows = rows.reshape(batch, seq // BLOCK, 3, 8, BLOCK)
    y, pc, pn = _prompt_mixer(sinks, qkv, vt, og, cols, rows, qa, kx, vx, batch, seq)
    pk = ka.reshape(batch, WINDOW, KV_HEADS, A_DIM)
    pv = va.reshape(batch, WINDOW, KV_HEADS, A_DIM)
    pn = pn[:, :, 0, :]
    pm = pm[:, 0:M_HEADS, 0]

    y_m, sn, mt, sc = _sample_mlstm(b_i, b_f, qkvs, ogs, gates_t, jnp.transpose(state_mlstm_m[0]),
                                    state_mlstm_n[0].reshape(nb, M_WIDTH), state_mlstm_C[0])
    sn = sn.reshape(nb, M_HEADS, M_DIM)
    sm = jnp.transpose(mt[:, 0, :])

    qa_h = qas.reshape(nb, A_HEADS, A_DIM)
    zeros = jnp.zeros_like(qa_h)
    in_lo = (jnp.arange(A_HEADS) // A_GROUP == 0)[None, :, None]
    q2 = jnp.concatenate([jnp.where(in_lo, qa_h, zeros), jnp.where(in_lo, zeros, qa_h)], axis=-1)
    kc = cache_swa_k[0].reshape(nb, -1, KV_WIDTH)
    vc = cache_swa_v[0].reshape(nb, -1, KV_WIDTH)
    o2, sk, sv = _sample_swa(q2, kc, vc, kas, vas, sinks.reshape(A_HEADS, 1))
    o2 = o2.reshape(nb, A_HEADS, KV_HEADS, A_DIM)
    y_a = jnp.where(in_lo, o2[:, :, 0, :], o2[:, :, 1, :]).reshape(nb, A_WIDTH)
    ys_in = jnp.concatenate([y_m, y_a], axis=-1).astype(BF16)

    yp, ys = _back(x1, y, x1s, ys_in, back_params, tm=1024)
    yp = yp.reshape(batch, seq, D_MODEL)
    ys = ys.reshape(nb, 1, D_MODEL)

    wb = kc.shape[1]
    return (yp, ys, pk[None], pv[None], pc[None], pn[None], pm[None],
            sk.reshape(1, nb, wb, KV_HEADS, A_DIM), sv.reshape(1, nb, wb, KV_HEADS, A_DIM),
            sc[None], sn[None], sm[None])
```

```python
import functools

import jax
import jax.numpy as jnp
from jax import lax
from jax.experimental import pallas as pl
from jax.experimental.pallas import tpu as pltpu

F32 = jnp.float32
BF16 = jnp.bfloat16

D_MODEL = 1024
D_FF = 2816
FF_CHUNK = 256
N_FF_CHUNKS = D_FF // FF_CHUNK
M_HEADS = 4
M_DIM = 128
M_WIDTH = M_HEADS * M_DIM
A_HEADS = 8
A_DIM = 64
A_WIDTH = A_HEADS * A_DIM
KV_HEADS = 2
KV_WIDTH = KV_HEADS * A_DIM
A_GROUP = A_HEADS // KV_HEADS
WINDOW = 128
BLOCK = 128
GATE_PAD = 128
RMS_EPS = 1e-6
FFN_RES_WEIGHT = 0.5
NEG_INF = float("-inf")
LOG2E = 1.4426950408889634
VMEM_LIMIT_BYTES = 56 * 1024 * 1024


def _dot(a, b):
    return jnp.dot(a, b, preferred_element_type=F32)


def _dot_nt(a, b):
    return lax.dot_general(a, b, (((1,), (1,)), ((), ())), preferred_element_type=F32)


def _rms_rows(x, gain):
    ms = jnp.mean(x * x, axis=-1, keepdims=True)
    return x * lax.rsqrt(ms + RMS_EPS) * gain


def _log_sigmoid(x):
    return jnp.minimum(x, 0.0) - jnp.log1p(jnp.exp(-jnp.abs(x)))


def _half_tile_mean_sq(x):
    in_lo = lax.broadcasted_iota(jnp.int32, (x.shape[0], 128), 1) < A_DIM
    out = []
    for c in range(x.shape[1] // 128):
        sq = x[:, c * 128:(c + 1) * 128]
        sq = sq * sq
        s_lo = jnp.sum(jnp.where(in_lo, sq, 0.0), axis=-1, keepdims=True)
        s_hi = jnp.sum(jnp.where(in_lo, 0.0, sq), axis=-1, keepdims=True)
        out.append(jnp.where(in_lo, s_lo, s_hi) * (1.0 / A_DIM))
    return out[0] if len(out) == 1 else jnp.concatenate(out, axis=1)


def _ffn(h_ref, wg_ref, wu_ref, wd_ref, act_ref):
    for c in range(N_FF_CHUNKS):
        lo, hi = c * FF_CHUNK, (c + 1) * FF_CHUNK
        h = h_ref[...]
        g = _dot(h, wg_ref[:, lo:hi])
        u = _dot(h, wu_ref[:, lo:hi])
        act_ref[:, lo:hi] = (g * jax.nn.sigmoid(g) * u).astype(BF16)
    return _dot(act_ref[...], wd_ref[...])


N_FRONT_PARAMS = 10
N_FRONT_COMMON_OUTS = 9
N_PROMPT_GATE_OUTS = 3
W_QA = 4 * M_WIDTH
W_KVG = W_QA + A_WIDTH
W_IN_COLS = W_KVG + 2 * KV_WIDTH + GATE_PAD


def _prompt_gates(gates_t, bias_ref, first_of_seq, m_ref, rows_ref, cols_ref, mfin_ref):
    sub = lax.broadcasted_iota(jnp.int32, (8, BLOCK), 0)
    lane = lax.broadcasted_iota(jnp.int32, (8, BLOCK), 1)
    is_head = sub < M_HEADS
    pad = jnp.zeros((BLOCK - 24, BLOCK), F32)
    scan_shifts = (1, 2, 4, 8, 16, 32, 64)
    m_prev = jnp.where(first_of_seq, 0.0, m_ref[...])[:, 0:1]
    for c in range(rows_ref.shape[0]):
        pre = gates_t[:, c * BLOCK:(c + 1) * BLOCK] + bias_ref[...]
        r = jnp.where(is_head, pre, _log_sigmoid(pre))
        cum = r
        for shift in scan_shifts:
            cum = cum + jnp.where(lane >= shift, pltpu.roll(cum, shift, 1), 0.0)
        bcum = pltpu.roll(cum, M_HEADS, 0)
        g = jnp.where(is_head, r - bcum, 0.0)
        bcum = jnp.where(is_head, bcum, 0.0)
        cm = g
        for shift in scan_shifts:
            cm = jnp.maximum(cm, jnp.where(lane >= shift, pltpu.roll(cm, shift, 1), NEG_INF))
        cm_last = jnp.max(cm, axis=-1, keepdims=True)
        b_last = jnp.sum(jnp.where(lane == BLOCK - 1, bcum, 0.0), axis=-1, keepdims=True)
        mx = jnp.maximum(m_prev, cm)
        mx_last = jnp.maximum(m_prev, cm_last)
        rows_ref[c, 0] = g * LOG2E
        rows_ref[c, 1] = jnp.exp(g - mx_last)
        rows_ref[c, 2] = jnp.broadcast_to(jnp.exp(m_prev - mx_last), (8, BLOCK))
        col_src = jnp.concatenate([mx * -LOG2E, jnp.exp(m_prev - mx), jnp.exp(-(bcum + mx)), pad], axis=0)
        cols_ref[c * BLOCK:(c + 1) * BLOCK, :] = col_src.T
        m_prev = b_last + mx_last
    m_full = jnp.broadcast_to(m_prev, (8, BLOCK))
    m_ref[...] = m_full
    mfin_ref[0] = m_full


def _front_tile(x_ref, params, outs, gate_sink, h_ref, act_ref):
    (n1_ref, wg_ref, wu_ref, wd_ref, n2_ref, win_ref, qgain_ref, kgain_ref, ogain_ref, _) = params
    (x1_ref, qkv_ref, og_ref, qa_ref, ka_ref, va_ref, vt_ref, kx_ref, vx_ref) = outs
    wm_ref = win_ref
    x = x_ref[...]
    h_ref[...] = _rms_rows(x, n1_ref[...]).astype(BF16)
    x1 = x + FFN_RES_WEIGHT * _ffn(h_ref, wg_ref, wu_ref, wd_ref, act_ref)
    x1_ref[...] = x1
    h_ref[...] = _rms_rows(x1, n2_ref[...]).astype(BF16)
    h = h_ref[...]

    qa = _dot(h, win_ref[:, W_QA:W_KVG])
    kvg = _dot(h, win_ref[:, W_KVG:W_IN_COLS])
    v_m = _dot(h, wm_ref[:, 2 * M_WIDTH:3 * M_WIDTH])
    qkv_ref[:, 2 * M_WIDTH:3 * M_WIDTH] = v_m.astype(BF16)
    for c in range(vt_ref.shape[0]):
        for hd in range(M_HEADS):
            blk = v_m[c * BLOCK:(c + 1) * BLOCK, hd * M_DIM:(hd + 1) * M_DIM]
            vt_ref[c, hd] = blk.T.astype(BF16)
    q_scale = qgain_ref[...] * (A_DIM ** -0.5 * LOG2E)
    qa_ref[...] = (qa * lax.rsqrt(_half_tile_mean_sq(qa) + RMS_EPS) * q_scale).astype(BF16)
    ka = kvg[:, 0:KV_WIDTH]
    ka = ka * lax.rsqrt(_half_tile_mean_sq(ka) + RMS_EPS) * kgain_ref[...]
    va = kvg[:, KV_WIDTH:2 * KV_WIDTH]
    keep = ka_ref.shape[0]
    ka_ref[...] = ka[ka.shape[0] - keep:, :]
    va_ref[...] = va[va.shape[0] - keep:, :]
    gate_sink(kvg[:, 2 * KV_WIDTH:2 * KV_WIDTH + GATE_PAD].T[0:2 * M_HEADS, :])

    in_lo = lax.broadcasted_iota(jnp.int32, ka.shape, 1) < A_DIM
    for src, dst in ((ka, kx_ref), (va, vx_ref)):
        x0 = jnp.where(in_lo, src, 0.0)
        x1 = jnp.where(in_lo, 0.0, src)
        dst[:, 0:128] = x0.astype(BF16)
        dst[:, 128:256] = pltpu.roll(x0, A_DIM, 1).astype(BF16)
        dst[:, 256:384] = pltpu.roll(x1, A_DIM, 1).astype(BF16)
        dst[:, 384:512] = x1.astype(BF16)

    og_ref[...] = jax.nn.sigmoid(_dot(h, wm_ref[:, 3 * M_WIDTH:4 * M_WIDTH])) * ogain_ref[...]
    k_m = _dot(h, wm_ref[:, M_WIDTH:2 * M_WIDTH]) * (M_DIM ** -0.5)
    qkv_ref[:, M_WIDTH:2 * M_WIDTH] = k_m.astype(BF16)
    qkv_ref[:, 0:M_WIDTH] = _dot(h, wm_ref[:, 0:M_WIDTH]).astype(BF16)


def _front_kernel(*refs, n_tiles, tiles_per_seq, n_cast):
    refs = list(refs)
    take = lambda k: [refs.pop(0) for _ in range(k)]
    x_ref, xs_ref = take(2)
    params = take(N_FRONT_PARAMS)
    cast_src = take(n_cast)
    outs_p = take(N_FRONT_COMMON_OUTS)
    rows_ref, cols_ref, mfin_ref = take(N_PROMPT_GATE_OUTS)
    outs_s = take(N_FRONT_COMMON_OUTS)
    (gts_ref,) = take(1)
    cast_dst = take(n_cast)
    h_ref, act_ref, m_ref = refs
    bias_ref = params[-1]
    i = pl.program_id(0)

    @pl.when(i < n_tiles)
    def _():
        for src, dst in zip(cast_src, cast_dst):
            dst[...] = src[...].astype(BF16)
        gates = functools.partial(_prompt_gates, bias_ref=bias_ref, first_of_seq=i % tiles_per_seq == 0,
                                  m_ref=m_ref, rows_ref=rows_ref, cols_ref=cols_ref, mfin_ref=mfin_ref)
        _front_tile(x_ref, params, outs_p, gates, h_ref, act_ref)

    @pl.when(i == n_tiles)
    def _():
        ns = xs_ref.shape[0]

        def raw_gates(gates_t):
            gts_ref[...] = gates_t

        _front_tile(xs_ref, params, outs_s, raw_gates, h_ref.at[0:ns], act_ref.at[0:ns])


def _const_spec(shape):
    nd = len(shape)
    return pl.BlockSpec(shape, lambda i: (0,) * nd, pipeline_mode=pl.Buffered(1))


def _whole_spec(shape):
    nd = len(shape)
    return pl.BlockSpec(shape, lambda i: (0,) * nd)


def _front_out_shapes(n, n_cache_rows):
    return (
        jax.ShapeDtypeStruct((n, D_MODEL), F32),
        jax.ShapeDtypeStruct((n, 3 * M_WIDTH), BF16),
        jax.ShapeDtypeStruct((n, M_WIDTH), F32),
        jax.ShapeDtypeStruct((n, A_WIDTH), BF16),
        jax.ShapeDtypeStruct((n_cache_rows, KV_WIDTH), F32),
        jax.ShapeDtypeStruct((n_cache_rows, KV_WIDTH), F32),
        jax.ShapeDtypeStruct((n // BLOCK, M_HEADS, M_DIM, BLOCK), BF16),
        jax.ShapeDtypeStruct((n, 4 * KV_WIDTH), BF16),
        jax.ShapeDtypeStruct((n, 4 * KV_WIDTH), BF16),
    )


def _cast_chunk_count(n_rows, max_chunks):
    for k in range(max_chunks, 0, -1):
        if n_rows % k == 0 and (n_rows // k) % 16 == 0:
            return k
    raise ValueError(n_rows)


def _front(x2d, xs2d, params, later_weights, tm, seq):
    n, ns = x2d.shape[0], xs2d.shape[0]
    assert seq % tm == 0 and tm >= WINDOW and len(params) == N_FRONT_PARAMS
    n_tiles = n // tm
    n_seqs = n // seq
    tiles_per_seq = seq // tm
    nb_t = tm // BLOCK
    tile = lambda i: jnp.minimum(i, n_tiles - 1)
    seq_of = lambda i: tile(i) // tiles_per_seq
    row = lambda w: pl.BlockSpec((tm, w), lambda i: (tile(i), 0))
    tail = pl.BlockSpec((WINDOW, KV_WIDTH), lambda i: (seq_of(i), 0))
    prompt_specs = (row(D_MODEL), row(3 * M_WIDTH), row(M_WIDTH), row(A_WIDTH), tail, tail,
                    pl.BlockSpec((nb_t, M_HEADS, M_DIM, BLOCK), lambda i: (tile(i), 0, 0, 0)),
                    row(4 * KV_WIDTH), row(4 * KV_WIDTH),
                    pl.BlockSpec((nb_t, 3, 8, BLOCK), lambda i: (tile(i), 0, 0, 0)), row(128),
                    pl.BlockSpec((1, 8, BLOCK), lambda i: (seq_of(i), 0, 0)))
    prompt_shapes = _front_out_shapes(n, n_seqs * WINDOW) + (
        jax.ShapeDtypeStruct((n // BLOCK, 3, 8, BLOCK), F32),
        jax.ShapeDtypeStruct((n, 128), F32),
        jax.ShapeDtypeStruct((n_seqs, 8, BLOCK), F32),
    )
    sample_shapes = _front_out_shapes(ns, ns) + (
        jax.ShapeDtypeStruct((2 * M_HEADS, ns), F32),
    )

    def chunk_spec(w):
        k = _cast_chunk_count(w.shape[0], n_tiles)
        return pl.BlockSpec((w.shape[0] // k, w.shape[1]), lambda i: (jnp.minimum(i, k - 1), 0))

    cast_specs = [chunk_spec(w) for w in later_weights]
    cast_shapes = tuple(jax.ShapeDtypeStruct(w.shape, BF16) for w in later_weights)
    outs = pl.pallas_call(
        functools.partial(_front_kernel, n_tiles=n_tiles, tiles_per_seq=tiles_per_seq, n_cast=len(later_weights)),
        out_shape=prompt_shapes + sample_shapes + cast_shapes,
        grid=(n_tiles + 1,),
        in_specs=[row(D_MODEL), _whole_spec(xs2d.shape)] + [_const_spec(p.shape) for p in params] + cast_specs,
        out_specs=prompt_specs + tuple(_whole_spec(s.shape) for s in sample_shapes) + tuple(cast_specs),
        scratch_shapes=[pltpu.VMEM((tm, D_MODEL), BF16), pltpu.VMEM((tm, D_FF), BF16),
                        pltpu.VMEM((8, BLOCK), F32)],
        compiler_params=pltpu.CompilerParams(dimension_semantics=("arbitrary",),
                                             vmem_limit_bytes=VMEM_LIMIT_BYTES),
        name="front",
    )(x2d, xs2d, *params, *later_weights)
    n_p, n_s = len(prompt_shapes), len(sample_shapes)
    return outs[:n_p], outs[n_p:n_p + n_s], outs[n_p + n_s:]


def _back_tile(x1_ref, y_ref, params, out_ref, h_ref, act_ref):
    wo_ref, n_ref, wg_ref, wu_ref, wd_ref = params
    x2 = x1_ref[...] + _dot(y_ref[...], wo_ref[...])
    h_ref[...] = _rms_rows(x2, n_ref[...]).astype(BF16)
    out_ref[...] = x2 + FFN_RES_WEIGHT * _ffn(h_ref, wg_ref, wu_ref, wd_ref, act_ref)


def _back_kernel(x1_ref, y_ref, x1s_ref, ys_ref, wo_ref, n_ref, wg_ref, wu_ref, wd_ref, out_ref, outs_ref,
                 h_ref, act_ref, *, n_tiles):
    params = (wo_ref, n_ref, wg_ref, wu_ref, wd_ref)
    i = pl.program_id(0)

    @pl.when(i < n_tiles)
    def _():
        _back_tile(x1_ref, y_ref, params, out_ref, h_ref, act_ref)

    @pl.when(i == n_tiles)
    def _():
        ns = x1s_ref.shape[0]
        _back_tile(x1s_ref, ys_ref, params, outs_ref, h_ref.at[0:ns], act_ref.at[0:ns])


def _back(x1, y, x1s, ys, params, tm):
    n, ns = x1.shape[0], x1s.shape[0]
    n_tiles = n // tm
    row = pl.BlockSpec((tm, D_MODEL), lambda i: (jnp.minimum(i, n_tiles - 1), 0))
    return pl.pallas_call(
        functools.partial(_back_kernel, n_tiles=n_tiles),
        out_shape=(jax.ShapeDtypeStruct((n, D_MODEL), F32), jax.ShapeDtypeStruct((ns, D_MODEL), F32)),
        grid=(n_tiles + 1,),
        in_specs=[row, row, _whole_spec(x1s.shape), _whole_spec(ys.shape)] + [_const_spec(p.shape) for p in params],
        out_specs=(row, _whole_spec((ns, D_MODEL))),
        scratch_shapes=[pltpu.VMEM((tm, D_MODEL), BF16), pltpu.VMEM((tm, D_FF), BF16)],
        compiler_params=pltpu.CompilerParams(dimension_semantics=("arbitrary",),
                                             vmem_limit_bytes=VMEM_LIMIT_BYTES),
        name="back",
    )(x1, y, x1s, ys, *params)


def _prompt_mixer_kernel(sinks_ref, qkv_ref, vt_ref, og_ref, cols_ref, rows_ref, qa_ref,
                         kx_ref, vx_ref, kxp_ref, vxp_ref,
                         y_ref, c_out_ref, n_out_ref,
                         c_ref, n_ref):
    j = pl.program_id(0)
    batch = qkv_ref.shape[0]

    @pl.when(j == 0)
    def _():
        c_ref[...] = jnp.zeros_like(c_ref)
        n_ref[...] = jnp.zeros_like(n_ref)

    rows = lax.broadcasted_iota(jnp.int32, (BLOCK, BLOCK), 0)
    cols = lax.broadcasted_iota(jnp.int32, (BLOCK, BLOCK), 1)
    causal = cols <= rows
    lane_lo = lax.broadcasted_iota(jnp.int32, (2 * BLOCK, KV_WIDTH), 1) < A_DIM
    ones_m = jnp.ones((BLOCK, M_DIM), BF16)
    ones_half = (jnp.where(lane_lo, 1.0, 0.0).astype(BF16), jnp.where(lane_lo, 0.0, 1.0).astype(BF16))
    qi = lax.broadcasted_iota(jnp.int32, (2 * BLOCK, 2 * BLOCK), 0) % BLOCK
    kc = lax.broadcasted_iota(jnp.int32, (2 * BLOCK, 2 * BLOCK), 1)
    first_valid = jnp.where(j == 0, BLOCK, 0)
    valid = (kc >= qi) & (kc <= qi + WINDOW) & (kc >= first_valid)
    top_rows = lax.broadcasted_iota(jnp.int32, (2 * BLOCK, 1), 0) < BLOCK

    heads = range(M_HEADS)
    groups = [(kv, parity) for kv in range(KV_HEADS) for parity in range(2)]
    m_lo = lambda h: h * M_DIM

    for b in range(batch):
        col_b = cols_ref[b]
        q = [qkv_ref[b, :, m_lo(h):m_lo(h) + M_DIM] for h in heads]
        k = [qkv_ref[b, :, M_WIDTH + m_lo(h):M_WIDTH + m_lo(h) + M_DIM] for h in heads]
        v = [qkv_ref[b, :, 2 * M_WIDTH + m_lo(h):2 * M_WIDTH + m_lo(h) + M_DIM] for h in heads]
        g_row = [rows_ref[b, 0, 0][h:h + 1, :] for h in heads]
        wend_row = [rows_ref[b, 0, 1][h:h + 1, :] for h in heads]
        decay = [rows_ref[b, 0, 2][h:h + 1, :] for h in heads]
        c_prev = [c_ref[b, h] for h in heads]
        n_prev = [n_ref[b, h] for h in heads]

        qkc, upd = [], []
        for h in heads:
            n_rep = jnp.broadcast_to(n_prev[h][0:1, :], (BLOCK, M_DIM)).astype(BF16)
            rhs = jnp.concatenate([k[h], c_prev[h].astype(BF16), n_rep], axis=0)
            qkc.append(_dot_nt(q[h], rhs))
        for h in heads:
            vw_t = (vt_ref[b, 0, h].astype(F32) * wend_row[h]).astype(BF16)
            w_rep = jnp.broadcast_to(wend_row[h], (16, BLOCK)).astype(BF16)
            upd.append(_dot(jnp.concatenate([vw_t, w_rep], axis=0), k[h]))
        q2 = [jnp.concatenate([qa_ref[b, :, (2 * kv) * 128:(2 * kv + 1) * 128],
                               qa_ref[b, :, (2 * kv + 1) * 128:(2 * kv + 2) * 128]], axis=0)
              for kv in range(KV_HEADS)]
        sc = []
        for kv, parity in groups:
            var = (2 * kv + parity) * KV_WIDTH
            k_band = jnp.concatenate([kxp_ref[b, :, var:var + KV_WIDTH], kx_ref[b, :, var:var + KV_WIDTH]], axis=0)
            sc.append(_dot_nt(q2[kv], k_band))

        s = []
        for h in heads:
            d = jnp.where(causal, jnp.exp2(col_b[:, h:h + 1] + g_row[h]), 0.0)
            s.append((qkc[h][:, 0:BLOCK] * d).astype(BF16))
            c_ref[b, h] = decay[h] * c_prev[h] + upd[h][0:M_DIM]
            n_ref[b, h] = decay[h] * n_prev[h] + upd[h][M_DIM:M_DIM + 8]
        pr, e_sink = [], []
        for gi, (kv, parity) in enumerate(groups):
            sink = jnp.where(top_rows, sinks_ref[4 * kv + parity], sinks_ref[4 * kv + 2 + parity]) * LOG2E
            scm = jnp.where(valid, sc[gi], NEG_INF)
            mx = jnp.maximum(jnp.max(scm, axis=-1, keepdims=True), sink)
            pr.append(jnp.exp2(scm - mx).astype(BF16))
            e_sink.append(jnp.exp2(sink - mx))

        sv = [_dot(s[h], jnp.concatenate([v[h], ones_m], axis=1)) for h in heads]
        pv = []
        for gi, (kv, parity) in enumerate(groups):
            var = (2 * kv + parity) * KV_WIDTH
            v_band = jnp.concatenate([vxp_ref[b, :, var:var + KV_WIDTH], vx_ref[b, :, var:var + KV_WIDTH]], axis=0)
            pv.append(_dot(pr[gi], jnp.concatenate([v_band, ones_half[parity]], axis=1)))

        for h in heads:
            lo, hi = m_lo(h), m_lo(h) + M_DIM
            wi_col = col_b[:, 8 + h:9 + h]
            num = wi_col * qkc[h][:, BLOCK:2 * BLOCK] + sv[h][:, 0:M_DIM]
            den = wi_col * qkc[h][:, 2 * BLOCK:3 * BLOCK] + sv[h][:, M_DIM:2 * M_DIM]
            hh = num / jnp.maximum(jnp.abs(den), col_b[:, 16 + h:17 + h])
            hh = hh * lax.rsqrt(jnp.mean(hh * hh, axis=-1, keepdims=True) + RMS_EPS)
            y_ref[b, :, lo:hi] = (hh * og_ref[b, :, lo:hi]).astype(BF16)
        for kv in range(KV_HEADS):
            acc = pv[2 * kv] + pv[2 * kv + 1]
            denom = acc[:, KV_WIDTH:] + jnp.where(lane_lo, e_sink[2 * kv], e_sink[2 * kv + 1])
            ya = (acc[:, 0:KV_WIDTH] / denom).astype(BF16)
            p0, p1 = 2 * kv, 2 * kv + 1
            y_ref[b, :, M_WIDTH + p0 * 128:M_WIDTH + (p0 + 1) * 128] = ya[0:BLOCK]
            y_ref[b, :, M_WIDTH + p1 * 128:M_WIDTH + (p1 + 1) * 128] = ya[BLOCK:2 * BLOCK]

    @pl.when(j == pl.num_programs(0) - 1)
    def _():
        c_out_ref[...] = c_ref[...]
        n_out_ref[...] = n_ref[...]


def _prompt_mixer(sinks, qkv, vt, og, cols, rows, qa, kx, vx, batch, seq):
    nblk = seq // BLOCK
    r3 = lambda a: a.reshape(batch, seq, a.shape[-1])
    cur = lambda w: pl.BlockSpec((batch, BLOCK, w), lambda j: (0, j, 0))
    prev = lambda w: pl.BlockSpec((batch, BLOCK, w), lambda j: (0, jnp.maximum(j - 1, 0), 0))
    state = lambda shape: pl.BlockSpec((batch,) + shape, lambda j: (0,) * (len(shape) + 1))
    out_shape = (
        jax.ShapeDtypeStruct((batch, seq, D_MODEL), BF16),
        jax.ShapeDtypeStruct((batch, M_HEADS, M_DIM, M_DIM), F32),
        jax.ShapeDtypeStruct((batch, M_HEADS, 8, M_DIM), F32),
    )
    y, pc, pn = pl.pallas_call(
        _prompt_mixer_kernel,
        out_shape=out_shape,
        grid=(nblk,),
        in_specs=[pl.BlockSpec(memory_space=pltpu.SMEM),
                  cur(3 * M_WIDTH),
                  pl.BlockSpec((batch, 1, M_HEADS, M_DIM, BLOCK), lambda j: (0, j, 0, 0, 0)),
                  cur(M_WIDTH), cur(128),
                  pl.BlockSpec((batch, 1, 3, 8, BLOCK), lambda j: (0, j, 0, 0, 0)),
                  cur(A_WIDTH), cur(4 * KV_WIDTH), cur(4 * KV_WIDTH), prev(4 * KV_WIDTH), prev(4 * KV_WIDTH)],
        out_specs=(cur(D_MODEL), state((M_HEADS, M_DIM, M_DIM)), state((M_HEADS, 8, M_DIM))),
        scratch_shapes=[pltpu.VMEM((batch, M_HEADS, M_DIM, M_DIM), F32),
                        pltpu.VMEM((batch, M_HEADS, 8, M_DIM), F32)],
        compiler_params=pltpu.CompilerParams(dimension_semantics=("arbitrary",),
                                             vmem_limit_bytes=VMEM_LIMIT_BYTES),
        name="prompt_mixer",
    )(sinks, r3(qkv), vt.reshape(batch, nblk, M_HEADS, M_DIM, BLOCK), r3(og), r3(cols), rows, r3(qa),
      r3(kx), r3(vx), r3(kx), r3(vx))
    return y.reshape(batch * seq, D_MODEL), pc, pn


def _sample_mlstm_kernel(bi_ref, bf_ref, q_ref, k_ref, v_ref, og_ref, gates_ref, m0_ref, n0_ref, c_ref,
                         y_ref, n_out_ref, m_out_ref, c_out_ref, decay_ref, qr_ref):
    h = pl.program_id(0)
    nb = q_ref.shape[0]
    q_rows = q_ref[...].astype(F32)
    k_rows = k_ref[...].astype(F32)
    qr_ref[...] = q_rows
    qt, kt, vt = q_rows.T, k_rows.T, v_ref[...].astype(F32).T
    i_pre = gates_ref[pl.ds(h, 1), :] + bi_ref[h]
    a = _log_sigmoid(gates_ref[pl.ds(M_HEADS + h, 1), :] + bf_ref[h]) + m0_ref[pl.ds(h, 1), :]
    m_t = jnp.maximum(a, i_pre)
    w_inter = jnp.exp(a - m_t)
    w_in = jnp.exp(i_pre - m_t)
    scores = jnp.sum(qt * kt, axis=0, keepdims=True) * w_in
    n0t = n0_ref[...].T
    nq = jnp.sum(n0t * qt, axis=0, keepdims=True)

    rows = lax.broadcasted_iota(jnp.int32, (nb, M_DIM), 0)
    cols = lax.broadcasted_iota(jnp.int32, (M_DIM, nb), 1)
    vw_t = (vt * w_in).astype(BF16)

    decay_ref[...] = jnp.broadcast_to(w_inter, (M_DIM, nb)).T

    def body(grp, cq_t):
        base = grp * SAMPLE_UNROLL
        for u in range(SAMPLE_UNROLL):
            b = base + u
            col = jnp.sum(c_ref[b, 0] * qr_ref[pl.ds(b, 1), :], axis=-1, keepdims=True)
            cq_t = jnp.where(cols == b, col, cq_t)
        outer = []
        for u in range(SAMPLE_UNROLL):
            k_only_b = jnp.where(rows == base + u, k_rows, 0.0).astype(BF16)
            outer.append(_dot(vw_t, k_only_b))
        for u in range(SAMPLE_UNROLL):
            b = base + u
            c_out_ref[b, 0] = decay_ref[pl.ds(b, 1), :] * c_ref[b, 0] + outer[u]
        return cq_t

    cq_t = lax.fori_loop(0, nb // SAMPLE_UNROLL, body, jnp.zeros((M_DIM, nb), F32))

    num = w_inter * cq_t + scores * vt
    den = w_inter * nq + scores
    hh = num / jnp.maximum(jnp.abs(den), jnp.exp(-m_t))
    hh = hh * lax.rsqrt(jnp.mean(hh * hh, axis=0, keepdims=True) + RMS_EPS)
    y_ref[...] = hh.T * og_ref[...]
    n_out_ref[...] = (w_inter * n0t + w_in * kt).T
    m_out_ref[0] = m_t


def _sample_mlstm(b_i, b_f, qkv, og, gates_t, m0_t, n0, c0):
    nb = qkv.shape[0]
    smem = pl.BlockSpec(memory_space=pltpu.SMEM)
    head = lambda off: pl.BlockSpec((nb, M_DIM), lambda h: (0, off + h))
    out_shape = (
        jax.ShapeDtypeStruct((nb, M_WIDTH), F32),
        jax.ShapeDtypeStruct((nb, M_WIDTH), F32),
        jax.ShapeDtypeStruct((M_HEADS, 1, nb), F32),
        jax.ShapeDtypeStruct((nb, M_HEADS, M_DIM, M_DIM), F32),
    )
    c_spec = pl.BlockSpec((nb, 1, M_DIM, M_DIM), lambda h: (0, h, 0, 0))
    return pl.pallas_call(
        _sample_mlstm_kernel,
        out_shape=out_shape,
        grid=(M_HEADS,),
        in_specs=[smem, smem, head(0), head(M_HEADS), head(2 * M_HEADS), head(0),
                  pl.BlockSpec(gates_t.shape, lambda h: (0, 0)), pl.BlockSpec(m0_t.shape, lambda h: (0, 0)),
                  head(0), c_spec],
        out_specs=(head(0), head(0), pl.BlockSpec((1, 1, nb), lambda h: (h, 0, 0)), c_spec),
        scratch_shapes=[pltpu.VMEM((nb, M_DIM), F32), pltpu.VMEM((nb, M_DIM), F32)],
        compiler_params=pltpu.CompilerParams(dimension_semantics=("arbitrary",),
                                             vmem_limit_bytes=VMEM_LIMIT_BYTES),
        name="sample_mlstm",
    )(b_i, b_f, qkv, qkv, qkv, og, gates_t, m0_t, n0, c0)


SAMPLE_TILE = 16
SAMPLE_UNROLL = 8


def _sample_swa_kernel(q2_ref, kc_ref, vc_ref, kn_ref, vn_ref, sink_ref, o_ref, ko_ref, vo_ref):
    sink = sink_ref[...] * LOG2E
    w = kc_ref.shape[1]
    tile = range(SAMPLE_TILE)
    q2 = [q2_ref[b] for b in tile]
    k_new = [kn_ref[b:b + 1, :] for b in tile]
    v_new = [vn_ref[b:b + 1, :] for b in tile]
    s_c = [_dot_nt(q2[b], kc_ref[b].astype(BF16)) for b in tile]
    s_n = [jnp.sum(q2[b].astype(F32) * k_new[b], axis=-1, keepdims=True) for b in tile]
    mx = [jnp.maximum(jnp.maximum(jnp.max(s_c[b], axis=-1, keepdims=True), s_n[b]), sink) for b in tile]
    p_c = [jnp.exp2(s_c[b] - mx[b]) for b in tile]
    p_n = [jnp.exp2(s_n[b] - mx[b]) for b in tile]
    denom = [jnp.sum(p_c[b], axis=-1, keepdims=True) + p_n[b] + jnp.exp2(sink - mx[b]) for b in tile]
    o = [_dot(p_c[b].astype(BF16), vc_ref[b].astype(BF16)) for b in tile]
    for b in tile:
        o_ref[b] = (o[b] + p_n[b] * v_new[b]) / denom[b]
        ko_ref[b, 0:w - 1, :] = kc_ref[b, 1:w, :]
        ko_ref[b, w - 1:w, :] = k_new[b]
        vo_ref[b, 0:w - 1, :] = vc_ref[b, 1:w, :]
        vo_ref[b, w - 1:w, :] = v_new[b]


def _sample_swa(q2, k_cache, v_cache, k_new, v_new, sink_col):
    nb, w, _ = k_cache.shape
    t3 = lambda a, c: pl.BlockSpec((SAMPLE_TILE, a, c), lambda i: (i, 0, 0))
    t2 = pl.BlockSpec((SAMPLE_TILE, KV_WIDTH), lambda i: (i, 0))
    out_shape = (
        jax.ShapeDtypeStruct((nb, A_HEADS, KV_WIDTH), F32),
        jax.ShapeDtypeStruct((nb, w, KV_WIDTH), F32),
        jax.ShapeDtypeStruct((nb, w, KV_WIDTH), F32),
    )
    return pl.pallas_call(
        _sample_swa_kernel,
        out_shape=out_shape,
        grid=(nb // SAMPLE_TILE,),
        in_specs=[t3(A_HEADS, KV_WIDTH), t3(w, KV_WIDTH), t3(w, KV_WIDTH), t2, t2,
                  pl.BlockSpec((A_HEADS, 1), lambda i: (0, 0))],
        out_specs=(t3(A_HEADS, KV_WIDTH), t3(w, KV_WIDTH), t3(w, KV_WIDTH)),
        compiler_params=pltpu.CompilerParams(dimension_semantics=("arbitrary",)),
        name="sample_swa",
    )(q2, k_cache, v_cache, k_new, v_new, sink_col)


def _rearranged_w_in(w_in):
    m_end = 4 * M_WIDTH
    g_end = m_end + 2 * M_HEADS
    gates = jnp.pad(w_in[:, m_end:g_end], ((0, 0), (0, GATE_PAD - 2 * M_HEADS)))
    w_in_r = jnp.concatenate([w_in[:, :m_end], w_in[:, g_end:], gates], axis=1).astype(BF16)
    assert w_in_r.shape[1] == W_IN_COLS
    return w_in_r


def kernel(x_prompt, x_sample, cache_swa_k, cache_swa_v, state_mlstm_C, state_mlstm_n, state_mlstm_m,
           ffn1_norm, ffn1_w_gate, ffn1_w_up, ffn1_w_down, mix_norm, w_in, mlstm_b_i, mlstm_b_f,
           mlstm_out_norm, swa_q_norm, swa_k_norm, swa_sinks, w_out, ffn2_norm, ffn2_w_gate,
           ffn2_w_up, ffn2_w_down):
    depth = ffn1_norm.shape[0]
    assert depth == 1
    batch, seq, _ = x_prompt.shape
    nb = x_sample.shape[0]
    assert x_sample.shape[1] == 1 and seq % BLOCK == 0

    wg1, wu1, wd1 = (w[0].astype(BF16) for w in (ffn1_w_gate, ffn1_w_up, ffn1_w_down))
    w_in_r = _rearranged_w_in(w_in[0])
    n1 = ffn1_norm[0].reshape(1, D_MODEL)
    n2 = mix_norm[0].reshape(1, D_MODEL)
    n3 = ffn2_norm[0].reshape(1, D_MODEL)
    qgain = jnp.tile(swa_q_norm[0], A_HEADS).reshape(1, A_WIDTH)
    kgain = jnp.tile(swa_k_norm[0], KV_HEADS).reshape(1, KV_WIDTH)
    ogain = mlstm_out_norm[0].reshape(1, M_WIDTH)
    b_i, b_f = mlstm_b_i[0], mlstm_b_f[0]
    bias8 = jnp.concatenate([b_i, b_f]).reshape(2 * M_HEADS, 1)
    front_params = (n1, wg1, wu1, wd1, n2, w_in_r, qgain, kgain, ogain, bias8)
    sinks = swa_sinks[0]

    xp = x_prompt.reshape(batch * seq, D_MODEL)
    xs = x_sample.reshape(nb, D_MODEL)
    back_weights_f32 = (w_out[0], ffn2_w_gate[0], ffn2_w_up[0], ffn2_w_down[0])
    prompt_front, sample_front, (wo, wg2, wu2, wd2) = _front(xp, xs, front_params, back_weights_f32,
                                                             tm=512, seq=seq)
    back_params = (wo, n3, wg2, wu2, wd2)
    x1, qkv, og, qa, ka, va, vt, kx, vx, rows, cols, pm = prompt_front
    x1s, qkvs, ogs, qas, kas, vas = sample_front[:6]
    gates_t = sample_front[-1]

    rows = rows.reshape(batch, seq // BLOCK, 3, 8, BLOCK)
    y, pc, pn = _prompt_mixer(sinks, qkv, vt, og, cols, rows, qa, kx, vx, batch, seq)
    pk = ka.reshape(batch, WINDOW, KV_HEADS, A_DIM)
    pv = va.reshape(batch, WINDOW, KV_HEADS, A_DIM)
    pn = pn[:, :, 0, :]
    pm = pm[:, 0:M_HEADS, 0]

    y_m, sn, mt, sc = _sample_mlstm(b_i, b_f, qkvs, ogs, gates_t, jnp.transpose(state_mlstm_m[0]),
                                    state_mlstm_n[0].reshape(nb, M_WIDTH), state_mlstm_C[0])
    sn = sn.reshape(nb, M_HEADS, M_DIM)
    sm = jnp.transpose(mt[:, 0, :])

    qa_h = qas.reshape(nb, A_HEADS, A_DIM)
    zeros = jnp.zeros_like(qa_h)
    in_lo = (jnp.arange(A_HEADS) // A_GROUP == 0)[None, :, None]
    q2 = jnp.concatenate([jnp.where(in_lo, qa_h, zeros), jnp.where(in_lo, zeros, qa_h)], axis=-1)
    kc = cache_swa_k[0].reshape(nb, -1, KV_WIDTH)
    vc = cache_swa_v[0].reshape(nb, -1, KV_WIDTH)
    o2, sk, sv = _sample_swa(q2, kc, vc, kas, vas, sinks.reshape(A_HEADS, 1))
    o2 = o2.reshape(nb, A_HEADS, KV_HEADS, A_DIM)
    y_a = jnp.where(in_lo, o2[:, :, 0, :], o2[:, :, 1, :]).reshape(nb, A_WIDTH)
    ys_in = jnp.concatenate([y_m, y_a], axis=-1).astype(BF16)

    yp, ys = _back(x1, y, x1s, ys_in, back_params, tm=1024)
    yp = yp.reshape(batch, seq, D_MODEL)
    ys = ys.reshape(nb, 1, D_MODEL)

    wb = kc.shape[1]
    return (yp, ys, pk[None], pv[None], pc[None], pn[None], pm[None],
            sk.reshape(1, nb, wb, KV_HEADS, A_DIM), sv.reshape(1, nb, wb, KV_HEADS, A_DIM),
            sc[None], sn[None], sm[None])
```

```python
import functools

import jax
import jax.numpy as jnp
from jax import lax
from jax.experimental import pallas as pl
from jax.experimental.pallas import tpu as pltpu

F32 = jnp.float32
BF16 = jnp.bfloat16

D_MODEL = 1024
D_FF = 2816
FF_CHUNK = 256
N_FF_CHUNKS = D_FF // FF_CHUNK
M_HEADS = 4
M_DIM = 128
M_WIDTH = M_HEADS * M_DIM
A_HEADS = 8
A_DIM = 64
A_WIDTH = A_HEADS * A_DIM
KV_HEADS = 2
KV_WIDTH = KV_HEADS * A_DIM
A_GROUP = A_HEADS // KV_HEADS
WINDOW = 128
BLOCK = 128
GATE_PAD = 128
RMS_EPS = 1e-6
FFN_RES_WEIGHT = 0.5
NEG_INF = float("-inf")
LOG2E = 1.4426950408889634
VMEM_LIMIT_BYTES = 56 * 1024 * 1024


def _dot(a, b):
    return jnp.dot(a, b, preferred_element_type=F32)


def _dot_nt(a, b):
    return lax.dot_general(a, b, (((1,), (1,)), ((), ())), preferred_element_type=F32)


def _rms_rows(x, gain):
    ms = jnp.mean(x * x, axis=-1, keepdims=True)
    return x * lax.rsqrt(ms + RMS_EPS) * gain


def _log_sigmoid(x):
    return jnp.minimum(x, 0.0) - jnp.log1p(jnp.exp(-jnp.abs(x)))


def _half_tile_mean_sq(x):
    in_lo = lax.broadcasted_iota(jnp.int32, (x.shape[0], 128), 1) < A_DIM
    out = []
    for c in range(x.shape[1] // 128):
        sq = x[:, c * 128:(c + 1) * 128]
        sq = sq * sq
        s_lo = jnp.sum(jnp.where(in_lo, sq, 0.0), axis=-1, keepdims=True)
        s_hi = jnp.sum(jnp.where(in_lo, 0.0, sq), axis=-1, keepdims=True)
        out.append(jnp.where(in_lo, s_lo, s_hi) * (1.0 / A_DIM))
    return out[0] if len(out) == 1 else jnp.concatenate(out, axis=1)


def _ffn(h_ref, wg_ref, wu_ref, wd_ref, act_ref):
    for c in range(N_FF_CHUNKS):
        lo, hi = c * FF_CHUNK, (c + 1) * FF_CHUNK
        h = h_ref[...]
        g = _dot(h, wg_ref[:, lo:hi])
        u = _dot(h, wu_ref[:, lo:hi])
        act_ref[:, lo:hi] = (g * jax.nn.sigmoid(g) * u).astype(BF16)
    return _dot(act_ref[...], wd_ref[...])


N_FRONT_PARAMS = 10
N_FRONT_COMMON_OUTS = 9
N_PROMPT_GATE_OUTS = 3
W_QA = 4 * M_WIDTH
W_KVG = W_QA + A_WIDTH
W_IN_COLS = W_KVG + 2 * KV_WIDTH + GATE_PAD


def _prompt_gates(gates_t, bias_ref, first_of_seq, m_ref, rows_ref, cols_ref, mfin_ref):
    sub = lax.broadcasted_iota(jnp.int32, (8, BLOCK), 0)
    lane = lax.broadcasted_iota(jnp.int32, (8, BLOCK), 1)
    is_head = sub < M_HEADS
    pad = jnp.zeros((BLOCK - 24, BLOCK), F32)
    scan_shifts = (1, 2, 4, 8, 16, 32, 64)
    m_prev = jnp.where(first_of_seq, 0.0, m_ref[...])[:, 0:1]
    for c in range(rows_ref.shape[0]):
        pre = gates_t[:, c * BLOCK:(c + 1) * BLOCK] + bias_ref[...]
        r = jnp.where(is_head, pre, _log_sigmoid(pre))
        cum = r
        for shift in scan_shifts:
            cum = cum + jnp.where(lane >= shift, pltpu.roll(cum, shift, 1), 0.0)
        bcum = pltpu.roll(cum, M_HEADS, 0)
        g = jnp.where(is_head, r - bcum, 0.0)
        bcum = jnp.where(is_head, bcum, 0.0)
        cm = g
        for shift in scan_shifts:
            cm = jnp.maximum(cm, jnp.where(lane >= shift, pltpu.roll(cm, shift, 1), NEG_INF))
        cm_last = jnp.max(cm, axis=-1, keepdims=True)
        b_last = jnp.sum(jnp.where(lane == BLOCK - 1, bcum, 0.0), axis=-1, keepdims=True)
        mx = jnp.maximum(m_prev, cm)
        mx_last = jnp.maximum(m_prev, cm_last)
        rows_ref[c, 0] = g * LOG2E
        rows_ref[c, 1] = jnp.exp(g - mx_last)
        rows_ref[c, 2] = jnp.broadcast_to(jnp.exp(m_prev - mx_last), (8, BLOCK))
        col_src = jnp.concatenate([mx * -LOG2E, jnp.exp(m_prev - mx), jnp.exp(-(bcum + mx)), pad], axis=0)
        cols_ref[c * BLOCK:(c + 1) * BLOCK, :] = col_src.T
        m_prev = b_last + mx_last
    m_full = jnp.broadcast_to(m_prev, (8, BLOCK))
    m_ref[...] = m_full
    mfin_ref[0] = m_full


def _front_tile(x_ref, params, outs, gate_sink, h_ref, act_ref):
    (n1_ref, wg_ref, wu_ref, wd_ref, n2_ref, win_ref, qgain_ref, kgain_ref, ogain_ref, _) = params
    (x1_ref, qkv_ref, og_ref, qa_ref, ka_ref, va_ref, vt_ref, kx_ref, vx_ref) = outs
    wm_ref = win_ref
    x = x_ref[...]
    h_ref[...] = _rms_rows(x, n1_ref[...]).astype(BF16)
    x1 = x + FFN_RES_WEIGHT * _ffn(h_ref, wg_ref, wu_ref, wd_ref, act_ref)
    x1_ref[...] = x1
    h_ref[...] = _rms_rows(x1, n2_ref[...]).astype(BF16)
    h = h_ref[...]

    qa = _dot(h, win_ref[:, W_QA:W_KVG])
    kvg = _dot(h, win_ref[:, W_KVG:W_IN_COLS])
    v_m = _dot(h, wm_ref[:, 2 * M_WIDTH:3 * M_WIDTH])
    qkv_ref[:, 2 * M_WIDTH:3 * M_WIDTH] = v_m.astype(BF16)
    for c in range(vt_ref.shape[0]):
        for hd in range(M_HEADS):
            blk = v_m[c * BLOCK:(c + 1) * BLOCK, hd * M_DIM:(hd + 1) * M_DIM]
            vt_ref[c, hd] = blk.T.astype(BF16)
    q_scale = qgain_ref[...] * (A_DIM ** -0.5 * LOG2E)
    qa_ref[...] = (qa * lax.rsqrt(_half_tile_mean_sq(qa) + RMS_EPS) * q_scale).astype(BF16)
    ka = kvg[:, 0:KV_WIDTH]
    ka = ka * lax.rsqrt(_half_tile_mean_sq(ka) + RMS_EPS) * kgain_ref[...]
    va = kvg[:, KV_WIDTH:2 * KV_WIDTH]
    keep = ka_ref.shape[0]
    ka_ref[...] = ka[ka.shape[0] - keep:, :]
    va_ref[...] = va[va.shape[0] - keep:, :]
    gate_sink(kvg[:, 2 * KV_WIDTH:2 * KV_WIDTH + GATE_PAD].T[0:2 * M_HEADS, :])

    in_lo = lax.broadcasted_iota(jnp.int32, ka.shape, 1) < A_DIM
    for src, dst in ((ka, kx_ref), (va, vx_ref)):
        x0 = jnp.where(in_lo, src, 0.0)
        x1 = jnp.where(in_lo, 0.0, src)
        dst[:, 0:128] = x0.astype(BF16)
        dst[:, 128:256] = pltpu.roll(x0, A_DIM, 1).astype(BF16)
        dst[:, 256:384] = pltpu.roll(x1, A_DIM, 1).astype(BF16)
        dst[:, 384:512] = x1.astype(BF16)

    og_ref[...] = jax.nn.sigmoid(_dot(h, wm_ref[:, 3 * M_WIDTH:4 * M_WIDTH])) * ogain_ref[...]
    k_m = _dot(h, wm_ref[:, M_WIDTH:2 * M_WIDTH]) * (M_DIM ** -0.5)
    qkv_ref[:, M_WIDTH:2 * M_WIDTH] = k_m.astype(BF16)
    qkv_ref[:, 0:M_WIDTH] = _dot(h, wm_ref[:, 0:M_WIDTH]).astype(BF16)


def _front_kernel(*refs, n_tiles, tiles_per_seq, n_cast):
    refs = list(refs)
    take = lambda k: [refs.pop(0) for _ in range(k)]
    x_ref, xs_ref = take(2)
    params = take(N_FRONT_PARAMS)
    cast_src = take(n_cast)
    outs_p = take(N_FRONT_COMMON_OUTS)
    rows_ref, cols_ref, mfin_ref = take(N_PROMPT_GATE_OUTS)
    outs_s = take(N_FRONT_COMMON_OUTS)
    (gts_ref,) = take(1)
    cast_dst = take(n_cast)
    h_ref, act_ref, m_ref = refs
    bias_ref = params[-1]
    i = pl.program_id(0)

    @pl.when(i < n_tiles)
    def _():
        for src, dst in zip(cast_src, cast_dst):
            dst[...] = src[...].astype(BF16)
        gates = functools.partial(_prompt_gates, bias_ref=bias_ref, first_of_seq=i % tiles_per_seq == 0,
                                  m_ref=m_ref, rows_ref=rows_ref, cols_ref=cols_ref, mfin_ref=mfin_ref)
        _front_tile(x_ref, params, outs_p, gates, h_ref, act_ref)

    @pl.when(i == n_tiles)
    def _():
        ns = xs_ref.shape[0]

        def raw_gates(gates_t):
            gts_ref[...] = gates_t

        _front_tile(xs_ref, params, outs_s, raw_gates, h_ref.at[0:ns], act_ref.at[0:ns])


def _const_spec(shape):
    nd = len(shape)
    return pl.BlockSpec(shape, lambda i: (0,) * nd, pipeline_mode=pl.Buffered(1))


def _whole_spec(shape):
    nd = len(shape)
    return pl.BlockSpec(shape, lambda i: (0,) * nd)


def _front_out_shapes(n, n_cache_rows):
    return (
        jax.ShapeDtypeStruct((n, D_MODEL), F32),
        jax.ShapeDtypeStruct((n, 3 * M_WIDTH), BF16),
        jax.ShapeDtypeStruct((n, M_WIDTH), F32),
        jax.ShapeDtypeStruct((n, A_WIDTH), BF16),
        jax.ShapeDtypeStruct((n_cache_rows, KV_WIDTH), F32),
        jax.ShapeDtypeStruct((n_cache_rows, KV_WIDTH), F32),
        jax.ShapeDtypeStruct((n // BLOCK, M_HEADS, M_DIM, BLOCK), BF16),
        jax.ShapeDtypeStruct((n, 4 * KV_WIDTH), BF16),
        jax.ShapeDtypeStruct((n, 4 * KV_WIDTH), BF16),
    )


def _cast_chunk_count(n_rows, max_chunks):
    for k in range(max_chunks, 0, -1):
        if n_rows % k == 0 and (n_rows // k) % 16 == 0:
            return k
    raise ValueError(n_rows)


def _front(x2d, xs2d, params, later_weights, tm, seq):
    n, ns = x2d.shape[0], xs2d.shape[0]
    assert seq % tm == 0 and tm >= WINDOW and len(params) == N_FRONT_PARAMS
    n_tiles = n // tm
    n_seqs = n // seq
    tiles_per_seq = seq // tm
    nb_t = tm // BLOCK
    tile = lambda i: jnp.minimum(i, n_tiles - 1)
    seq_of = lambda i: tile(i) // tiles_per_seq
    row = lambda w: pl.BlockSpec((tm, w), lambda i: (tile(i), 0))
    tail = pl.BlockSpec((WINDOW, KV_WIDTH), lambda i: (seq_of(i), 0))
    prompt_specs = (row(D_MODEL), row(3 * M_WIDTH), row(M_WIDTH), row(A_WIDTH), tail, tail,
                    pl.BlockSpec((nb_t, M_HEADS, M_DIM, BLOCK), lambda i: (tile(i), 0, 0, 0)),
                    row(4 * KV_WIDTH), row(4 * KV_WIDTH),
                    pl.BlockSpec((nb_t, 3, 8, BLOCK), lambda i: (tile(i), 0, 0, 0)), row(128),
                    pl.BlockSpec((1, 8, BLOCK), lambda i: (seq_of(i), 0, 0)))
    prompt_shapes = _front_out_shapes(n, n_seqs * WINDOW) + (
        jax.ShapeDtypeStruct((n // BLOCK, 3, 8, BLOCK), F32),
        jax.ShapeDtypeStruct((n, 128), F32),
        jax.ShapeDtypeStruct((n_seqs, 8, BLOCK), F32),
    )
    sample_shapes = _front_out_shapes(ns, ns) + (
        jax.ShapeDtypeStruct((2 * M_HEADS, ns), F32),
    )

    def chunk_spec(w):
        k = _cast_chunk_count(w.shape[0], n_tiles)
        return pl.BlockSpec((w.shape[0] // k, w.shape[1]), lambda i: (jnp.minimum(i, k - 1), 0))

    cast_specs = [chunk_spec(w) for w in later_weights]
    cast_shapes = tuple(jax.ShapeDtypeStruct(w.shape, BF16) for w in later_weights)
    outs = pl.pallas_call(
        functools.partial(_front_kernel, n_tiles=n_tiles, tiles_per_seq=tiles_per_seq, n_cast=len(later_weights)),
        out_shape=prompt_shapes + sample_shapes + cast_shapes,
        grid=(n_tiles + 1,),
        in_specs=[row(D_MODEL), _whole_spec(xs2d.shape)] + [_const_spec(p.shape) for p in params] + cast_specs,
        out_specs=prompt_specs + tuple(_whole_spec(s.shape) for s in sample_shapes) + tuple(cast_specs),
        scratch_shapes=[pltpu.VMEM((tm, D_MODEL), BF16), pltpu.VMEM((tm, D_FF), BF16),
                        pltpu.VMEM((8, BLOCK), F32)],
        compiler_params=pltpu.CompilerParams(dimension_semantics=("arbitrary",),
                                             vmem_limit_bytes=VMEM_LIMIT_BYTES),
        name="front",
    )(x2d, xs2d, *params, *later_weights)
    n_p, n_s = len(prompt_shapes), len(sample_shapes)
    return outs[:n_p], outs[n_p:n_p + n_s], outs[n_p + n_s:]


def _back_tile(x1_ref, y_ref, params, out_ref, h_ref, act_ref):
    wo_ref, n_ref, wg_ref, wu_ref, wd_ref = params
    x2 = x1_ref[...] + _dot(y_ref[...], wo_ref[...])
    h_ref[...] = _rms_rows(x2, n_ref[...]).astype(BF16)
    out_ref[...] = x2 + FFN_RES_WEIGHT * _ffn(h_ref, wg_ref, wu_ref, wd_ref, act_ref)


def _back_kernel(x1_ref, y_ref, x1s_ref, ys_ref, wo_ref, n_ref, wg_ref, wu_ref, wd_ref, out_ref, outs_ref,
                 h_ref, act_ref, *, n_tiles):
    params = (wo_ref, n_ref, wg_ref, wu_ref, wd_ref)
    i = pl.program_id(0)

    @pl.when(i < n_tiles)
    def _():
        _back_tile(x1_ref, y_ref, params, out_ref, h_ref, act_ref)

    @pl.when(i == n_tiles)
    def _():
        ns = x1s_ref.shape[0]
        _back_tile(x1s_ref, ys_ref, params, outs_ref, h_ref.at[0:ns], act_ref.at[0:ns])


def _back(x1, y, x1s, ys, params, tm):
    n, ns = x1.shape[0], x1s.shape[0]
    n_tiles = n // tm
    row = pl.BlockSpec((tm, D_MODEL), lambda i: (jnp.minimum(i, n_tiles - 1), 0))
    return pl.pallas_call(
        functools.partial(_back_kernel, n_tiles=n_tiles),
        out_shape=(jax.ShapeDtypeStruct((n, D_MODEL), F32), jax.ShapeDtypeStruct((ns, D_MODEL), F32)),
        grid=(n_tiles + 1,),
        in_specs=[row, row, _whole_spec(x1s.shape), _whole_spec(ys.shape)] + [_const_spec(p.shape) for p in params],
        out_specs=(row, _whole_spec((ns, D_MODEL))),
        scratch_shapes=[pltpu.VMEM((tm, D_MODEL), BF16), pltpu.VMEM((tm, D_FF), BF16)],
        compiler_params=pltpu.CompilerParams(dimension_semantics=("arbitrary",),
                                             vmem_limit_bytes=VMEM_LIMIT_BYTES),
        name="back",
    )(x1, y, x1s, ys, *params)


def _prompt_mixer_kernel(sinks_ref, qkv_ref, vt_ref, og_ref, cols_ref, rows_ref, qa_ref,
                         kx_ref, vx_ref, kxp_ref, vxp_ref,
                         y_ref, c_out_ref, n_out_ref,
                         c_ref, n_ref):
    j = pl.program_id(0)
    batch = qkv_ref.shape[0]

    @pl.when(j == 0)
    def _():
        c_ref[...] = jnp.zeros_like(c_ref)
        n_ref[...] = jnp.zeros_like(n_ref)

    rows = lax.broadcasted_iota(jnp.int32, (BLOCK, BLOCK), 0)
    cols = lax.broadcasted_iota(jnp.int32, (BLOCK, BLOCK), 1)
    causal = cols <= rows
    lane_lo = lax.broadcasted_iota(jnp.int32, (2 * BLOCK, KV_WIDTH), 1) < A_DIM
    ones_m = jnp.ones((BLOCK, M_DIM), BF16)
    ones_half = (jnp.where(lane_lo, 1.0, 0.0).astype(BF16), jnp.where(lane_lo, 0.0, 1.0).astype(BF16))
    qi = lax.broadcasted_iota(jnp.int32, (2 * BLOCK, 2 * BLOCK), 0) % BLOCK
    kc = lax.broadcasted_iota(jnp.int32, (2 * BLOCK, 2 * BLOCK), 1)
    first_valid = jnp.where(j == 0, BLOCK, 0)
    valid = (kc >= qi) & (kc <= qi + WINDOW) & (kc >= first_valid)
    top_rows = lax.broadcasted_iota(jnp.int32, (2 * BLOCK, 1), 0) < BLOCK

    heads = range(M_HEADS)
    groups = [(kv, parity) for kv in range(KV_HEADS) for parity in range(2)]
    m_lo = lambda h: h * M_DIM

    def first_matmuls(b):
        st = {"col": cols_ref[b]}
        k = [qkv_ref[b, :, M_WIDTH + m_lo(h):M_WIDTH + m_lo(h) + M_DIM] for h in heads]
        st["c_prev"] = [c_ref[b, h] for h in heads]
        st["n_prev"] = [n_ref[b, h] for h in heads]
        st["qkc"], st["upd"], st["sc"] = [], [], []
        for h in heads:
            q = qkv_ref[b, :, m_lo(h):m_lo(h) + M_DIM]
            n_rep = jnp.broadcast_to(st["n_prev"][h][0:1, :], (BLOCK, M_DIM)).astype(BF16)
            rhs = jnp.concatenate([k[h], st["c_prev"][h].astype(BF16), n_rep], axis=0)
            st["qkc"].append(_dot_nt(q, rhs))
        for h in heads:
            wend_row = rows_ref[b, 0, 1][h:h + 1, :]
            vw_t = (vt_ref[b, 0, h].astype(F32) * wend_row).astype(BF16)
            w_rep = jnp.broadcast_to(wend_row, (16, BLOCK)).astype(BF16)
            st["upd"].append(_dot(jnp.concatenate([vw_t, w_rep], axis=0), k[h]))
        q2 = [jnp.concatenate([qa_ref[b, :, (2 * kv) * 128:(2 * kv + 1) * 128],
                               qa_ref[b, :, (2 * kv + 1) * 128:(2 * kv + 2) * 128]], axis=0)
              for kv in range(KV_HEADS)]
        for kv, parity in groups:
            var = (2 * kv + parity) * KV_WIDTH
            k_band = jnp.concatenate([kxp_ref[b, :, var:var + KV_WIDTH], kx_ref[b, :, var:var + KV_WIDTH]], axis=0)
            st["sc"].append(_dot_nt(q2[kv], k_band))
        return st

    def vector_work(b, st):
        st["s"], st["pr"], st["e_sink"] = [], [], []
        for h in heads:
            g_row = rows_ref[b, 0, 0][h:h + 1, :]
            decay = rows_ref[b, 0, 2][h:h + 1, :]
            d = jnp.where(causal, jnp.exp2(st["col"][:, h:h + 1] + g_row), 0.0)
            st["s"].append((st["qkc"][h][:, 0:BLOCK] * d).astype(BF16))
            c_ref[b, h] = decay * st["c_prev"][h] + st["upd"][h][0:M_DIM]
            n_ref[b, h] = decay * st["n_prev"][h] + st["upd"][h][M_DIM:M_DIM + 8]
        for gi, (kv, parity) in enumerate(groups):
            sink = jnp.where(top_rows, sinks_ref[4 * kv + parity], sinks_ref[4 * kv + 2 + parity]) * LOG2E
            scm = jnp.where(valid, st["sc"][gi], NEG_INF)
            mx = jnp.maximum(jnp.max(scm, axis=-1, keepdims=True), sink)
            st["pr"].append(jnp.exp2(scm - mx).astype(BF16))
            st["e_sink"].append(jnp.exp2(sink - mx))

    def second_matmuls(b, st):
        st["sv"], st["pv"] = [], []
        for h in heads:
            v = qkv_ref[b, :, 2 * M_WIDTH + m_lo(h):2 * M_WIDTH + m_lo(h) + M_DIM]
            st["sv"].append(_dot(st["s"][h], jnp.concatenate([v, ones_m], axis=1)))
        for gi, (kv, parity) in enumerate(groups):
            var = (2 * kv + parity) * KV_WIDTH
            v_band = jnp.concatenate([vxp_ref[b, :, var:var + KV_WIDTH], vx_ref[b, :, var:var + KV_WIDTH]], axis=0)
            st["pv"].append(_dot(st["pr"][gi], jnp.concatenate([v_band, ones_half[parity]], axis=1)))

    def normalise(b, st):
        col_b, qkc, sv, pv, e_sink = st["col"], st["qkc"], st["sv"], st["pv"], st["e_sink"]
        for h in heads:
            lo, hi = m_lo(h), m_lo(h) + M_DIM
            wi_col = col_b[:, 8 + h:9 + h]
            num = wi_col * qkc[h][:, BLOCK:2 * BLOCK] + sv[h][:, 0:M_DIM]
            den = wi_col * qkc[h][:, 2 * BLOCK:3 * BLOCK] + sv[h][:, M_DIM:2 * M_DIM]
            hh = num / jnp.maximum(jnp.abs(den), col_b[:, 16 + h:17 + h])
            hh = hh * lax.rsqrt(jnp.mean(hh * hh, axis=-1, keepdims=True) + RMS_EPS)
            y_ref[b, :, lo:hi] = (hh * og_ref[b, :, lo:hi]).astype(BF16)
        for kv in range(KV_HEADS):
            acc = pv[2 * kv] + pv[2 * kv + 1]
            denom = acc[:, KV_WIDTH:] + jnp.where(lane_lo, e_sink[2 * kv], e_sink[2 * kv + 1])
            ya = (acc[:, 0:KV_WIDTH] / denom).astype(BF16)
            p0, p1 = 2 * kv, 2 * kv + 1
            y_ref[b, :, M_WIDTH + p0 * 128:M_WIDTH + (p0 + 1) * 128] = ya[0:BLOCK]
            y_ref[b, :, M_WIDTH + p1 * 128:M_WIDTH + (p1 + 1) * 128] = ya[BLOCK:2 * BLOCK]

    for b in range(batch):
        st = first_matmuls(b)
        vector_work(b, st)
        second_matmuls(b, st)
        normalise(b, st)

    @pl.when(j == pl.num_programs(0) - 1)
    def _():
        c_out_ref[...] = c_ref[...]
        n_out_ref[...] = n_ref[...]


def _prompt_mixer(sinks, qkv, vt, og, cols, rows, qa, kx, vx, batch, seq):
    nblk = seq // BLOCK
    r3 = lambda a: a.reshape(batch, seq, a.shape[-1])
    cur = lambda w: pl.BlockSpec((batch, BLOCK, w), lambda j: (0, j, 0))
    prev = lambda w: pl.BlockSpec((batch, BLOCK, w), lambda j: (0, jnp.maximum(j - 1, 0), 0))
    state = lambda shape: pl.BlockSpec((batch,) + shape, lambda j: (0,) * (len(shape) + 1))
    out_shape = (
        jax.ShapeDtypeStruct((batch, seq, D_MODEL), BF16),
        jax.ShapeDtypeStruct((batch, M_HEADS, M_DIM, M_DIM), F32),
        jax.ShapeDtypeStruct((batch, M_HEADS, 8, M_DIM), F32),
    )
    y, pc, pn = pl.pallas_call(
        _prompt_mixer_kernel,
        out_shape=out_shape,
        grid=(nblk,),
        in_specs=[pl.BlockSpec(memory_space=pltpu.SMEM),
                  cur(3 * M_WIDTH),
                  pl.BlockSpec((batch, 1, M_HEADS, M_DIM, BLOCK), lambda j: (0, j, 0, 0, 0)),
                  cur(M_WIDTH), cur(128),
                  pl.BlockSpec((batch, 1, 3, 8, BLOCK), lambda j: (0, j, 0, 0, 0)),
                  cur(A_WIDTH), cur(4 * KV_WIDTH), cur(4 * KV_WIDTH), prev(4 * KV_WIDTH), prev(4 * KV_WIDTH)],
        out_specs=(cur(D_MODEL), state((M_HEADS, M_DIM, M_DIM)), state((M_HEADS, 8, M_DIM))),
        scratch_shapes=[pltpu.VMEM((batch, M_HEADS, M_DIM, M_DIM), F32),
                        pltpu.VMEM((batch, M_HEADS, 8, M_DIM), F32)],
        compiler_params=pltpu.CompilerParams(dimension_semantics=("arbitrary",),
                                             vmem_limit_bytes=VMEM_LIMIT_BYTES),
        name="prompt_mixer",
    )(sinks, r3(qkv), vt.reshape(batch, nblk, M_HEADS, M_DIM, BLOCK), r3(og), r3(cols), rows, r3(qa),
      r3(kx), r3(vx), r3(kx), r3(vx))
    return y.reshape(batch * seq, D_MODEL), pc, pn


def _sample_mlstm_kernel(bi_ref, bf_ref, q_ref, k_ref, v_ref, og_ref, gates_ref, m0_ref, n0_ref, c_ref,
                         y_ref, n_out_ref, m_out_ref, c_out_ref, decay_ref, qr_ref):
    h = pl.program_id(0)
    nb = q_ref.shape[0]
    q_rows = q_ref[...].astype(F32)
    k_rows = k_ref[...].astype(F32)
    qr_ref[...] = q_rows
    qt, kt, vt = q_rows.T, k_rows.T, v_ref[...].astype(F32).T
    i_pre = gates_ref[pl.ds(h, 1), :] + bi_ref[h]
    a = _log_sigmoid(gates_ref[pl.ds(M_HEADS + h, 1), :] + bf_ref[h]) + m0_ref[pl.ds(h, 1), :]
    m_t = jnp.maximum(a, i_pre)
    w_inter = jnp.exp(a - m_t)
    w_in = jnp.exp(i_pre - m_t)
    scores = jnp.sum(qt * kt, axis=0, keepdims=True) * w_in
    n0t = n0_ref[...].T
    nq = jnp.sum(n0t * qt, axis=0, keepdims=True)

    rows = lax.broadcasted_iota(jnp.int32, (nb, M_DIM), 0)
    cols = lax.broadcasted_iota(jnp.int32, (M_DIM, nb), 1)
    vw_t = (vt * w_in).astype(BF16)

    decay_ref[...] = jnp.broadcast_to(w_inter, (M_DIM, nb)).T

    def body(grp, cq_t):
        base = grp * SAMPLE_UNROLL
        for u in range(SAMPLE_UNROLL):
            b = base + u
            col = jnp.sum(c_ref[b, 0] * qr_ref[pl.ds(b, 1), :], axis=-1, keepdims=True)
            cq_t = jnp.where(cols == b, col, cq_t)
        outer = []
        for u in range(SAMPLE_UNROLL):
            k_only_b = jnp.where(rows == base + u, k_rows, 0.0).astype(BF16)
            outer.append(_dot(vw_t, k_only_b))
        for u in range(SAMPLE_UNROLL):
            b = base + u
            c_out_ref[b, 0] = decay_ref[pl.ds(b, 1), :] * c_ref[b, 0] + outer[u]
        return cq_t

    cq_t = lax.fori_loop(0, nb // SAMPLE_UNROLL, body, jnp.zeros((M_DIM, nb), F32))

    num = w_inter * cq_t + scores * vt
    den = w_inter * nq + scores
    hh = num / jnp.maximum(jnp.abs(den), jnp.exp(-m_t))
    hh = hh * lax.rsqrt(jnp.mean(hh * hh, axis=0, keepdims=True) + RMS_EPS)
    y_ref[...] = hh.T * og_ref[...]
    n_out_ref[...] = (w_inter * n0t + w_in * kt).T
    m_out_ref[0] = m_t


def _sample_mlstm(b_i, b_f, qkv, og, gates_t, m0_t, n0, c0):
    nb = qkv.shape[0]
    smem = pl.BlockSpec(memory_space=pltpu.SMEM)
    head = lambda off: pl.BlockSpec((nb, M_DIM), lambda h: (0, off + h))
    out_shape = (
        jax.ShapeDtypeStruct((nb, M_WIDTH), F32),
        jax.ShapeDtypeStruct((nb, M_WIDTH), F32),
        jax.ShapeDtypeStruct((M_HEADS, 1, nb), F32),
        jax.ShapeDtypeStruct((nb, M_HEADS, M_DIM, M_DIM), F32),
    )
    c_spec = pl.BlockSpec((nb, 1, M_DIM, M_DIM), lambda h: (0, h, 0, 0))
    return pl.pallas_call(
        _sample_mlstm_kernel,
        out_shape=out_shape,
        grid=(M_HEADS,),
        in_specs=[smem, smem, head(0), head(M_HEADS), head(2 * M_HEADS), head(0),
                  pl.BlockSpec(gates_t.shape, lambda h: (0, 0)), pl.BlockSpec(m0_t.shape, lambda h: (0, 0)),
                  head(0), c_spec],
        out_specs=(head(0), head(0), pl.BlockSpec((1, 1, nb), lambda h: (h, 0, 0)), c_spec),
        scratch_shapes=[pltpu.VMEM((nb, M_DIM), F32), pltpu.VMEM((nb, M_DIM), F32)],
        compiler_params=pltpu.CompilerParams(dimension_semantics=("arbitrary",),
                                             vmem_limit_bytes=VMEM_LIMIT_BYTES),
        name="sample_mlstm",
    )(b_i, b_f, qkv, qkv, qkv, og, gates_t, m0_t, n0, c0)


SAMPLE_TILE = 16
SAMPLE_UNROLL = 8


def _sample_swa_kernel(q2_ref, kc_ref, vc_ref, kn_ref, vn_ref, sink_ref, o_ref, ko_ref, vo_ref):
    sink = sink_ref[...] * LOG2E
    w = kc_ref.shape[1]
    tile = range(SAMPLE_TILE)
    q2 = [q2_ref[b] for b in tile]
    k_new = [kn_ref[b:b + 1, :] for b in tile]
    v_new = [vn_ref[b:b + 1, :] for b in tile]
    s_c = [_dot_nt(q2[b], kc_ref[b].astype(BF16)) for b in tile]
    s_n = [jnp.sum(q2[b].astype(F32) * k_new[b], axis=-1, keepdims=True) for b in tile]
    mx = [jnp.maximum(jnp.maximum(jnp.max(s_c[b], axis=-1, keepdims=True), s_n[b]), sink) for b in tile]
    p_c = [jnp.exp2(s_c[b] - mx[b]) for b in tile]
    p_n = [jnp.exp2(s_n[b] - mx[b]) for b in tile]
    denom = [jnp.sum(p_c[b], axis=-1, keepdims=True) + p_n[b] + jnp.exp2(sink - mx[b]) for b in tile]
    o = [_dot(p_c[b].astype(BF16), vc_ref[b].astype(BF16)) for b in tile]
    for b in tile:
        o_ref[b] = (o[b] + p_n[b] * v_new[b]) / denom[b]
        ko_ref[b, 0:w - 1, :] = kc_ref[b, 1:w, :]
        ko_ref[b, w - 1:w, :] = k_new[b]
        vo_ref[b, 0:w - 1, :] = vc_ref[b, 1:w, :]
        vo_ref[b, w - 1:w, :] = v_new[b]


def _sample_swa(q2, k_cache, v_cache, k_new, v_new, sink_col):
    nb, w, _ = k_cache.shape
    t3 = lambda a, c: pl.BlockSpec((SAMPLE_TILE, a, c), lambda i: (i, 0, 0))
    t2 = pl.BlockSpec((SAMPLE_TILE, KV_WIDTH), lambda i: (i, 0))
    out_shape = (
        jax.ShapeDtypeStruct((nb, A_HEADS, KV_WIDTH), F32),
        jax.ShapeDtypeStruct((nb, w, KV_WIDTH), F32),
        jax.ShapeDtypeStruct((nb, w, KV_WIDTH), F32),
    )
    return pl.pallas_call(
        _sample_swa_kernel,
        out_shape=out_shape,
        grid=(nb // SAMPLE_TILE,),
        in_specs=[t3(A_HEADS, KV_WIDTH), t3(w, KV_WIDTH), t3(w, KV_WIDTH), t2, t2,
                  pl.BlockSpec((A_HEADS, 1), lambda i: (0, 0))],
        out_specs=(t3(A_HEADS, KV_WIDTH), t3(w, KV_WIDTH), t3(w, KV_WIDTH)),
        compiler_params=pltpu.CompilerParams(dimension_semantics=("arbitrary",)),
        name="sample_swa",
    )(q2, k_cache, v_cache, k_new, v_new, sink_col)


def _rearranged_w_in(w_in):
    m_end = 4 * M_WIDTH
    g_end = m_end + 2 * M_HEADS
    gates = jnp.pad(w_in[:, m_end:g_end], ((0, 0), (0, GATE_PAD - 2 * M_HEADS)))
    w_in_r = jnp.concatenate([w_in[:, :m_end], w_in[:, g_end:], gates], axis=1).astype(BF16)
    assert w_in_r.shape[1] == W_IN_COLS
    return w_in_r


def kernel(x_prompt, x_sample, cache_swa_k, cache_swa_v, state_mlstm_C, state_mlstm_n, state_mlstm_m,
           ffn1_norm, ffn1_w_gate, ffn1_w_up, ffn1_w_down, mix_norm, w_in, mlstm_b_i, mlstm_b_f,
           mlstm_out_norm, swa_q_norm, swa_k_norm, swa_sinks, w_out, ffn2_norm, ffn2_w_gate,
           ffn2_w_up, ffn2_w_down):
    depth = ffn1_norm.shape[0]
    assert depth == 1
    batch, seq, _ = x_prompt.shape
    nb = x_sample.shape[0]
    assert x_sample.shape[1] == 1 and seq % BLOCK == 0

    wg1, wu1, wd1 = (w[0].astype(BF16) for w in (ffn1_w_gate, ffn1_w_up, ffn1_w_down))
    w_in_r = _rearranged_w_in(w_in[0])
    n1 = ffn1_norm[0].reshape(1, D_MODEL)
    n2 = mix_norm[0].reshape(1, D_MODEL)
    n3 = ffn2_norm[0].reshape(1, D_MODEL)
    qgain = jnp.tile(swa_q_norm[0], A_HEADS).reshape(1, A_WIDTH)
    kgain = jnp.tile(swa_k_norm[0], KV_HEADS).reshape(1, KV_WIDTH)
    ogain = mlstm_out_norm[0].reshape(1, M_WIDTH)
    b_i, b_f = mlstm_b_i[0], mlstm_b_f[0]
    bias8 = jnp.concatenate([b_i, b_f]).reshape(2 * M_HEADS, 1)
    front_params = (n1, wg1, wu1, wd1, n2, w_in_r, qgain, kgain, ogain, bias8)
    sinks = swa_sinks[0]

    xp = x_prompt.reshape(batch * seq, D_MODEL)
    xs = x_sample.reshape(nb, D_MODEL)
    back_weights_f32 = (w_out[0], ffn2_w_gate[0], ffn2_w_up[0], ffn2_w_down[0])
    prompt_front, sample_front, (wo, wg2, wu2, wd2) = _front(xp, xs, front_params, back_weights_f32,
                                                             tm=512, seq=seq)
    back_params = (wo, n3, wg2, wu2, wd2)
    x1, qkv, og, qa, ka, va, vt, kx, vx, rows, cols, pm = prompt_front
    x1s, qkvs, ogs, qas, kas, vas = sample_front[:6]
    gates_t = sample_front[-1]

    rows = rows.reshape(batch, seq // BLOCK, 3, 8, BLOCK)
    y, pc, pn = _prompt_mixer(sinks, qkv, vt, og, cols, rows, qa, kx, vx, batch, seq)
    pk = ka.reshape(batch, WINDOW, KV_HEADS, A_DIM)
    pv = va.reshape(batch, WINDOW, KV_HEADS, A_DIM)
    pn = pn[:, :, 0, :]
    pm = pm[:, 0:M_HEADS, 0]

    y_m, sn, mt, sc = _sample_mlstm(b_i, b_f, qkvs, ogs, gates_t, jnp.transpose(state_mlstm_m[0]),
                                    state_mlstm_n[0].reshape(nb, M_WIDTH), state_mlstm_C[0])
    sn = sn.reshape(nb, M_HEADS, M_DIM)
    sm = jnp.transpose(mt[:, 0, :])

    qa_h = qas.reshape(nb, A_HEADS, A_DIM)
    zeros = jnp.zeros_like(qa_h)
    in_lo = (jnp.arange(A_HEADS) // A_GROUP == 0)[None, :, None]
    q2 = jnp.concatenate([jnp.where(in_lo, qa_h, zeros), jnp.where(in_lo, zeros, qa_h)], axis=-1)
    kc = cache_swa_k[0].reshape(nb, -1, KV_WIDTH)
    vc = cache_swa_v[0].reshape(nb, -1, KV_WIDTH)
    o2, sk, sv = _sample_swa(q2, kc, vc, kas, vas, sinks.reshape(A_HEADS, 1))
    o2 = o2.reshape(nb, A_HEADS, KV_HEADS, A_DIM)
    y_a = jnp.where(in_lo, o2[:, :, 0, :], o2[:, :, 1, :]).reshape(nb, A_WIDTH)
    ys_in = jnp.concatenate([y_m, y_a], axis=-1).astype(BF16)

    yp, ys = _back(x1, y, x1s, ys_in, back_params, tm=1024)
    yp = yp.reshape(batch, seq, D_MODEL)
    ys = ys.reshape(nb, 1, D_MODEL)

    wb = kc.shape[1]
    return (yp, ys, pk[None], pv[None], pc[None], pn[None], pm[None],
            sk.reshape(1, nb, wb, KV_HEADS, A_DIM), sv.reshape(1, nb, wb, KV_HEADS, A_DIM),
            sc[None], sn[None], sm[None])
```

```python
import functools

import jax
import jax.numpy as jnp
from jax import lax
from jax.experimental import pallas as pl
from jax.experimental.pallas import tpu as pltpu

F32 = jnp.float32
BF16 = jnp.bfloat16

D_MODEL = 1024
D_FF = 2816
FF_CHUNK = 256
N_FF_CHUNKS = D_FF // FF_CHUNK
M_HEADS = 4
M_DIM = 128
M_WIDTH = M_HEADS * M_DIM
A_HEADS = 8
A_DIM = 64
A_WIDTH = A_HEADS * A_DIM
KV_HEADS = 2
KV_WIDTH = KV_HEADS * A_DIM
A_GROUP = A_HEADS // KV_HEADS
WINDOW = 128
BLOCK = 128
GATE_PAD = 128
RMS_EPS = 1e-6
FFN_RES_WEIGHT = 0.5
NEG_INF = float("-inf")
LOG2E = 1.4426950408889634
VMEM_LIMIT_BYTES = 56 * 1024 * 1024


def _dot(a, b):
    return jnp.dot(a, b, preferred_element_type=F32)


def _dot_nt(a, b):
    return lax.dot_general(a, b, (((1,), (1,)), ((), ())), preferred_element_type=F32)


def _rms_rows(x, gain):
    ms = jnp.mean(x * x, axis=-1, keepdims=True)
    return x * lax.rsqrt(ms + RMS_EPS) * gain


def _log_sigmoid(x):
    return jnp.minimum(x, 0.0) - jnp.log1p(jnp.exp(-jnp.abs(x)))


def _half_tile_mean_sq(x):
    in_lo = lax.broadcasted_iota(jnp.int32, (x.shape[0], 128), 1) < A_DIM
    out = []
    for c in range(x.shape[1] // 128):
        sq = x[:, c * 128:(c + 1) * 128]
        sq = sq * sq
        s_lo = jnp.sum(jnp.where(in_lo, sq, 0.0), axis=-1, keepdims=True)
        s_hi = jnp.sum(jnp.where(in_lo, 0.0, sq), axis=-1, keepdims=True)
        out.append(jnp.where(in_lo, s_lo, s_hi) * (1.0 / A_DIM))
    return out[0] if len(out) == 1 else jnp.concatenate(out, axis=1)


def _ffn(h_ref, wg_ref, wu_ref, wd_ref, act_ref):
    for c in range(N_FF_CHUNKS):
        lo, hi = c * FF_CHUNK, (c + 1) * FF_CHUNK
        h = h_ref[...]
        g = _dot(h, wg_ref[:, lo:hi])
        u = _dot(h, wu_ref[:, lo:hi])
        act_ref[:, lo:hi] = (g * jax.nn.sigmoid(g) * u).astype(BF16)
    return _dot(act_ref[...], wd_ref[...])


N_FRONT_PARAMS = 10
W_QA = 4 * M_WIDTH
W_KVG = W_QA + A_WIDTH
W_IN_COLS = W_KVG + 2 * KV_WIDTH + GATE_PAD


def _prompt_gates(gates_t, bias_ref, first_of_seq, m_ref, rows_ref, cols_ref, mfin_ref):
    sub = lax.broadcasted_iota(jnp.int32, (8, BLOCK), 0)
    lane = lax.broadcasted_iota(jnp.int32, (8, BLOCK), 1)
    is_head = sub < M_HEADS
    pad = jnp.zeros((BLOCK - 24, BLOCK), F32)
    scan_shifts = (1, 2, 4, 8, 16, 32, 64)
    m_prev = jnp.where(first_of_seq, 0.0, m_ref[...])[:, 0:1]
    for c in range(rows_ref.shape[0]):
        pre = gates_t[:, c * BLOCK:(c + 1) * BLOCK] + bias_ref[...]
        r = jnp.where(is_head, pre, _log_sigmoid(pre))
        cum = r
        for shift in scan_shifts:
            cum = cum + jnp.where(lane >= shift, pltpu.roll(cum, shift, 1), 0.0)
        bcum = pltpu.roll(cum, M_HEADS, 0)
        g = jnp.where(is_head, r - bcum, 0.0)
        bcum = jnp.where(is_head, bcum, 0.0)
        cm = g
        for shift in scan_shifts:
            cm = jnp.maximum(cm, jnp.where(lane >= shift, pltpu.roll(cm, shift, 1), NEG_INF))
        cm_last = jnp.max(cm, axis=-1, keepdims=True)
        b_last = jnp.sum(jnp.where(lane == BLOCK - 1, bcum, 0.0), axis=-1, keepdims=True)
        mx = jnp.maximum(m_prev, cm)
        mx_last = jnp.maximum(m_prev, cm_last)
        rows_ref[c, 0] = g * LOG2E
        rows_ref[c, 1] = jnp.exp(g - mx_last)
        rows_ref[c, 2] = jnp.broadcast_to(jnp.exp(m_prev - mx_last), (8, BLOCK))
        col_src = jnp.concatenate([mx * -LOG2E, jnp.exp(m_prev - mx), jnp.exp(-(bcum + mx)), pad], axis=0)
        cols_ref[c * BLOCK:(c + 1) * BLOCK, :] = col_src.T
        m_prev = b_last + mx_last
    m_full = jnp.broadcast_to(m_prev, (8, BLOCK))
    m_ref[...] = m_full
    mfin_ref[0] = m_full


def _front_tile(x_ref, params, outs, gate_sink, h_ref, act_ref):
    (n1_ref, wg_ref, wu_ref, wd_ref, n2_ref, win_ref, qgain_ref, kgain_ref, ogain_ref, _) = params
    (x1_ref, qkv_ref, og_ref, qa_ref, ka_ref, va_ref, vt_ref, kx_ref, vx_ref) = outs
    wm_ref = win_ref
    x = x_ref[...]
    h_ref[...] = _rms_rows(x, n1_ref[...]).astype(BF16)
    x1 = x + FFN_RES_WEIGHT * _ffn(h_ref, wg_ref, wu_ref, wd_ref, act_ref)
    x1_ref[...] = x1
    h_ref[...] = _rms_rows(x1, n2_ref[...]).astype(BF16)
    h = h_ref[...]

    qa = _dot(h, win_ref[:, W_QA:W_KVG])
    kvg = _dot(h, win_ref[:, W_KVG:W_IN_COLS])
    v_m = _dot(h, wm_ref[:, 2 * M_WIDTH:3 * M_WIDTH])
    qkv_ref[:, 2 * M_WIDTH:3 * M_WIDTH] = v_m.astype(BF16)
    for c in range(vt_ref.shape[0]):
        for hd in range(M_HEADS):
            blk = v_m[c * BLOCK:(c + 1) * BLOCK, hd * M_DIM:(hd + 1) * M_DIM]
            vt_ref[c, hd] = blk.T.astype(BF16)
    q_scale = qgain_ref[...] * (A_DIM ** -0.5 * LOG2E)
    qa_ref[...] = (qa * lax.rsqrt(_half_tile_mean_sq(qa) + RMS_EPS) * q_scale).astype(BF16)
    ka = kvg[:, 0:KV_WIDTH]
    ka = ka * lax.rsqrt(_half_tile_mean_sq(ka) + RMS_EPS) * kgain_ref[...]
    va = kvg[:, KV_WIDTH:2 * KV_WIDTH]
    keep = ka_ref.shape[0]
    ka_ref[...] = ka[ka.shape[0] - keep:, :]
    va_ref[...] = va[va.shape[0] - keep:, :]
    gate_sink(kvg[:, 2 * KV_WIDTH:2 * KV_WIDTH + GATE_PAD].T[0:2 * M_HEADS, :])

    in_lo = lax.broadcasted_iota(jnp.int32, ka.shape, 1) < A_DIM
    for src, dst in ((ka, kx_ref), (va, vx_ref)):
        x0 = jnp.where(in_lo, src, 0.0)
        x1 = jnp.where(in_lo, 0.0, src)
        dst[:, 0:128] = x0.astype(BF16)
        dst[:, 128:256] = pltpu.roll(x0, A_DIM, 1).astype(BF16)
        dst[:, 256:384] = pltpu.roll(x1, A_DIM, 1).astype(BF16)
        dst[:, 384:512] = x1.astype(BF16)

    og_ref[...] = jax.nn.sigmoid(_dot(h, wm_ref[:, 3 * M_WIDTH:4 * M_WIDTH])) * ogain_ref[...]
    k_m = _dot(h, wm_ref[:, M_WIDTH:2 * M_WIDTH]) * (M_DIM ** -0.5)
    qkv_ref[:, M_WIDTH:2 * M_WIDTH] = k_m.astype(BF16)
    qkv_ref[:, 0:M_WIDTH] = _dot(h, wm_ref[:, 0:M_WIDTH]).astype(BF16)


def _front_kernel(*refs, n_tiles, tiles_per_seq, n_cast):
    refs = list(refs)
    take = lambda k: [refs.pop(0) for _ in range(k)]
    x_ref, xs_ref = take(2)
    params = take(N_FRONT_PARAMS)
    cast_src = take(n_cast)
    f32_p, b16_p, ka_p, va_p, vt_p, rows_ref, mfin_ref = take(7)
    f32_s, b16_s, ka_s, va_s, vt_s, gts_ref = take(6)
    cast_dst = take(n_cast)
    h_ref, act_ref, m_ref = refs
    bias_ref = params[-1]
    i = pl.program_id(0)

    def tile_outs(f32_ref, b16_ref, ka_ref, va_ref, vt_ref):
        f32c = lambda name: f32_ref.at[:, F32_COLS[name][0]:F32_COLS[name][1]]
        b16c = lambda name: b16_ref.at[:, B16_COLS[name][0]:B16_COLS[name][1]]
        return (f32c("x1"), b16c("qkv"), f32c("og"), b16c("qa"), ka_ref, va_ref, vt_ref, b16c("kx"), b16c("vx"))

    @pl.when(i < n_tiles)
    def _():
        for src, dst in zip(cast_src, cast_dst):
            dst[...] = src[...].astype(BF16)
        cols_ref = f32_p.at[:, F32_COLS["gate_cols"][0]:F32_COLS["gate_cols"][1]]
        gates = functools.partial(_prompt_gates, bias_ref=bias_ref, first_of_seq=i % tiles_per_seq == 0,
                                  m_ref=m_ref, rows_ref=rows_ref, cols_ref=cols_ref, mfin_ref=mfin_ref)
        _front_tile(x_ref, params, tile_outs(f32_p, b16_p, ka_p, va_p, vt_p), gates, h_ref, act_ref)

    @pl.when(i == n_tiles)
    def _():
        ns = xs_ref.shape[0]

        def raw_gates(gates_t):
            gts_ref[...] = gates_t

        lo, hi = F32_COLS["gate_cols"]
        f32_s[:, lo:hi] = jnp.zeros((ns, hi - lo), F32)
        _front_tile(xs_ref, params, tile_outs(f32_s, b16_s, ka_s, va_s, vt_s), raw_gates,
                    h_ref.at[0:ns], act_ref.at[0:ns])


def _const_spec(shape):
    nd = len(shape)
    return pl.BlockSpec(shape, lambda i: (0,) * nd, pipeline_mode=pl.Buffered(1))


def _whole_spec(shape):
    nd = len(shape)
    return pl.BlockSpec(shape, lambda i: (0,) * nd)


F32_COLS = {"x1": (0, 1024),
            "og": (1024, 1536),
            "gate_cols": (1536, 1664)}
F32_PACK = 1664
B16_COLS = {"qkv": (0, 1536),
            "qa": (1536, 2048),
            "kx": (2048, 2560),
            "vx": (2560, 3072)}
B16_PACK = 3072


def _front_out_shapes(n, n_cache_rows):
    return (
        jax.ShapeDtypeStruct((n, F32_PACK), F32),
        jax.ShapeDtypeStruct((n, B16_PACK), BF16),
        jax.ShapeDtypeStruct((n_cache_rows, KV_WIDTH), F32),
        jax.ShapeDtypeStruct((n_cache_rows, KV_WIDTH), F32),
        jax.ShapeDtypeStruct((n // BLOCK, M_HEADS, M_DIM, BLOCK), BF16),
    )


def _cast_chunk_count(n_rows, max_chunks):
    for k in range(max_chunks, 0, -1):
        if n_rows % k == 0 and (n_rows // k) % 16 == 0:
            return k
    raise ValueError(n_rows)


def _front(x2d, xs2d, params, later_weights, tm, seq):
    n, ns = x2d.shape[0], xs2d.shape[0]
    assert seq % tm == 0 and tm >= WINDOW and len(params) == N_FRONT_PARAMS
    n_tiles = n // tm
    n_seqs = n // seq
    tiles_per_seq = seq // tm
    nb_t = tm // BLOCK
    tile = lambda i: jnp.minimum(i, n_tiles - 1)
    seq_of = lambda i: tile(i) // tiles_per_seq
    row = lambda w: pl.BlockSpec((tm, w), lambda i: (tile(i), 0))
    tail = pl.BlockSpec((WINDOW, KV_WIDTH), lambda i: (seq_of(i), 0))
    prompt_specs = (row(F32_PACK), row(B16_PACK), tail, tail,
                    pl.BlockSpec((nb_t, M_HEADS, M_DIM, BLOCK), lambda i: (tile(i), 0, 0, 0)),
                    pl.BlockSpec((nb_t, 3, 8, BLOCK), lambda i: (tile(i), 0, 0, 0)),
                    pl.BlockSpec((1, 8, BLOCK), lambda i: (seq_of(i), 0, 0)))
    prompt_shapes = _front_out_shapes(n, n_seqs * WINDOW) + (
        jax.ShapeDtypeStruct((n // BLOCK, 3, 8, BLOCK), F32),
        jax.ShapeDtypeStruct((n_seqs, 8, BLOCK), F32),
    )
    sample_shapes = _front_out_shapes(ns, ns) + (
        jax.ShapeDtypeStruct((2 * M_HEADS, ns), F32),
    )

    def chunk_spec(w):
        k = _cast_chunk_count(w.shape[0], n_tiles)
        return pl.BlockSpec((w.shape[0] // k, w.shape[1]), lambda i: (jnp.minimum(i, k - 1), 0))

    cast_specs = [chunk_spec(w) for w in later_weights]
    cast_shapes = tuple(jax.ShapeDtypeStruct(w.shape, BF16) for w in later_weights)
    outs = pl.pallas_call(
        functools.partial(_front_kernel, n_tiles=n_tiles, tiles_per_seq=tiles_per_seq, n_cast=len(later_weights)),
        out_shape=prompt_shapes + sample_shapes + cast_shapes,
        grid=(n_tiles + 1,),
        in_specs=[row(D_MODEL), _whole_spec(xs2d.shape)] + [_const_spec(p.shape) for p in params] + cast_specs,
        out_specs=prompt_specs + tuple(_whole_spec(s.shape) for s in sample_shapes) + tuple(cast_specs),
        scratch_shapes=[pltpu.VMEM((tm, D_MODEL), BF16), pltpu.VMEM((tm, D_FF), BF16),
                        pltpu.VMEM((8, BLOCK), F32)],
        compiler_params=pltpu.CompilerParams(dimension_semantics=("arbitrary",),
                                             vmem_limit_bytes=VMEM_LIMIT_BYTES),
        name="front",
    )(x2d, xs2d, *params, *later_weights)
    n_p, n_s = len(prompt_shapes), len(sample_shapes)
    return outs[:n_p], outs[n_p:n_p + n_s], outs[n_p + n_s:]


def _back_tile(x1_ref, y_ref, params, out_ref, h_ref, act_ref):
    wo_ref, n_ref, wg_ref, wu_ref, wd_ref = params
    x2 = x1_ref[...] + _dot(y_ref[...], wo_ref[...])
    h_ref[...] = _rms_rows(x2, n_ref[...]).astype(BF16)
    out_ref[...] = x2 + FFN_RES_WEIGHT * _ffn(h_ref, wg_ref, wu_ref, wd_ref, act_ref)


def _back_kernel(x1_ref, y_ref, x1s_ref, ys_ref, wo_ref, n_ref, wg_ref, wu_ref, wd_ref, out_ref, outs_ref,
                 h_ref, act_ref, *, n_tiles):
    params = (wo_ref, n_ref, wg_ref, wu_ref, wd_ref)
    i = pl.program_id(0)

    @pl.when(i < n_tiles)
    def _():
        _back_tile(x1_ref, y_ref, params, out_ref, h_ref, act_ref)

    @pl.when(i == n_tiles)
    def _():
        ns = x1s_ref.shape[0]
        _back_tile(x1s_ref, ys_ref, params, outs_ref, h_ref.at[0:ns], act_ref.at[0:ns])


def _back(x1, y, x1s, ys, params, tm):
    n, ns = x1.shape[0], x1s.shape[0]
    assert F32_COLS["x1"] == (0, D_MODEL)
    n_tiles = n // tm
    row = pl.BlockSpec((tm, D_MODEL), lambda i: (jnp.minimum(i, n_tiles - 1), 0))
    return pl.pallas_call(
        functools.partial(_back_kernel, n_tiles=n_tiles),
        out_shape=(jax.ShapeDtypeStruct((n, D_MODEL), F32), jax.ShapeDtypeStruct((ns, D_MODEL), F32)),
        grid=(n_tiles + 1,),
        in_specs=[row, row, _whole_spec((ns, D_MODEL)), _whole_spec(ys.shape)] + [_const_spec(p.shape) for p in params],
        out_specs=(row, _whole_spec((ns, D_MODEL))),
        scratch_shapes=[pltpu.VMEM((tm, D_MODEL), BF16), pltpu.VMEM((tm, D_FF), BF16)],
        compiler_params=pltpu.CompilerParams(dimension_semantics=("arbitrary",),
                                             vmem_limit_bytes=VMEM_LIMIT_BYTES),
        name="back",
    )(x1, y, x1s, ys, *params)


def _prompt_mixer_kernel(sinks_ref, b16_ref, kvx_prev_ref, vt_ref, og_ref, cols_ref, rows_ref,
                         y_ref, c_out_ref, n_out_ref,
                         c_ref, n_ref):
    window = lambda name: b16_ref.at[:, :, B16_COLS[name][0]:B16_COLS[name][1]]
    qkv_ref, qa_ref, kx_ref, vx_ref = window("qkv"), window("qa"), window("kx"), window("vx")
    kxp_ref = kvx_prev_ref.at[:, :, 0:4 * KV_WIDTH]
    vxp_ref = kvx_prev_ref.at[:, :, 4 * KV_WIDTH:8 * KV_WIDTH]
    j = pl.program_id(0)
    batch = b16_ref.shape[0]

    @pl.when(j == 0)
    def _():
        c_ref[...] = jnp.zeros_like(c_ref)
        n_ref[...] = jnp.zeros_like(n_ref)

    rows = lax.broadcasted_iota(jnp.int32, (BLOCK, BLOCK), 0)
    cols = lax.broadcasted_iota(jnp.int32, (BLOCK, BLOCK), 1)
    causal = cols <= rows
    lane_lo = lax.broadcasted_iota(jnp.int32, (2 * BLOCK, KV_WIDTH), 1) < A_DIM
    ones_m = jnp.ones((BLOCK, M_DIM), BF16)
    ones_half = (jnp.where(lane_lo, 1.0, 0.0).astype(BF16), jnp.where(lane_lo, 0.0, 1.0).astype(BF16))
    qi = lax.broadcasted_iota(jnp.int32, (2 * BLOCK, 2 * BLOCK), 0) % BLOCK
    kc = lax.broadcasted_iota(jnp.int32, (2 * BLOCK, 2 * BLOCK), 1)
    first_valid = jnp.where(j == 0, BLOCK, 0)
    valid = (kc >= qi) & (kc <= qi + WINDOW) & (kc >= first_valid)
    top_rows = lax.broadcasted_iota(jnp.int32, (2 * BLOCK, 1), 0) < BLOCK

    heads = range(M_HEADS)
    groups = [(kv, parity) for kv in range(KV_HEADS) for parity in range(2)]
    m_lo = lambda h: h * M_DIM

    def first_matmuls(b):
        st = {"col": cols_ref[b]}
        k = [qkv_ref[b, :, M_WIDTH + m_lo(h):M_WIDTH + m_lo(h) + M_DIM] for h in heads]
        st["c_prev"] = [c_ref[b, h] for h in heads]
        st["n_prev"] = [n_ref[b, h] for h in heads]
        st["qkc"], st["upd"], st["sc"] = [], [], []
        for h in heads:
            q = qkv_ref[b, :, m_lo(h):m_lo(h) + M_DIM]
            n_rep = jnp.broadcast_to(st["n_prev"][h][0:1, :], (BLOCK, M_DIM)).astype(BF16)
            rhs = jnp.concatenate([k[h], st["c_prev"][h].astype(BF16), n_rep], axis=0)
            st["qkc"].append(_dot_nt(q, rhs))
        for h in heads:
            wend_row = rows_ref[b, 0, 1][h:h + 1, :]
            vw_t = (vt_ref[b, 0, h].astype(F32) * wend_row).astype(BF16)
            w_rep = jnp.broadcast_to(wend_row, (16, BLOCK)).astype(BF16)
            st["upd"].append(_dot(jnp.concatenate([vw_t, w_rep], axis=0), k[h]))
        q2 = [jnp.concatenate([qa_ref[b, :, (2 * kv) * 128:(2 * kv + 1) * 128],
                               qa_ref[b, :, (2 * kv + 1) * 128:(2 * kv + 2) * 128]], axis=0)
              for kv in range(KV_HEADS)]
        for kv, parity in groups:
            var = (2 * kv + parity) * KV_WIDTH
            k_band = jnp.concatenate([kxp_ref[b, :, var:var + KV_WIDTH], kx_ref[b, :, var:var + KV_WIDTH]], axis=0)
            st["sc"].append(_dot_nt(q2[kv], k_band))
        return st

    def vector_work(b, st):
        st["s"], st["pr"], st["e_sink"] = [], [], []
        for h in heads:
            g_row = rows_ref[b, 0, 0][h:h + 1, :]
            decay = rows_ref[b, 0, 2][h:h + 1, :]
            d = jnp.where(causal, jnp.exp2(st["col"][:, h:h + 1] + g_row), 0.0)
            st["s"].append((st["qkc"][h][:, 0:BLOCK] * d).astype(BF16))
            c_ref[b, h] = decay * st["c_prev"][h] + st["upd"][h][0:M_DIM]
            n_ref[b, h] = decay * st["n_prev"][h] + st["upd"][h][M_DIM:M_DIM + 8]
        for gi, (kv, parity) in enumerate(groups):
            sink = jnp.where(top_rows, sinks_ref[4 * kv + parity], sinks_ref[4 * kv + 2 + parity]) * LOG2E
            scm = jnp.where(valid, st["sc"][gi], NEG_INF)
            mx = jnp.maximum(jnp.max(scm, axis=-1, keepdims=True), sink)
            st["pr"].append(jnp.exp2(scm - mx).astype(BF16))
            st["e_sink"].append(jnp.exp2(sink - mx))

    def second_matmuls(b, st):
        st["sv"], st["pv"] = [], []
        for h in heads:
            v = qkv_ref[b, :, 2 * M_WIDTH + m_lo(h):2 * M_WIDTH + m_lo(h) + M_DIM]
            st["sv"].append(_dot(st["s"][h], jnp.concatenate([v, ones_m], axis=1)))
        for gi, (kv, parity) in enumerate(groups):
            var = (2 * kv + parity) * KV_WIDTH
            v_band = jnp.concatenate([vxp_ref[b, :, var:var + KV_WIDTH], vx_ref[b, :, var:var + KV_WIDTH]], axis=0)
            st["pv"].append(_dot(st["pr"][gi], jnp.concatenate([v_band, ones_half[parity]], axis=1)))

    def normalise(b, st):
        col_b, qkc, sv, pv, e_sink = st["col"], st["qkc"], st["sv"], st["pv"], st["e_sink"]
        for h in heads:
            lo, hi = m_lo(h), m_lo(h) + M_DIM
            wi_col = col_b[:, 8 + h:9 + h]
            num = wi_col * qkc[h][:, BLOCK:2 * BLOCK] + sv[h][:, 0:M_DIM]
            den = wi_col * qkc[h][:, 2 * BLOCK:3 * BLOCK] + sv[h][:, M_DIM:2 * M_DIM]
            hh = num / jnp.maximum(jnp.abs(den), col_b[:, 16 + h:17 + h])
            hh = hh * lax.rsqrt(jnp.mean(hh * hh, axis=-1, keepdims=True) + RMS_EPS)
            y_ref[b, :, lo:hi] = (hh * og_ref[b, :, lo:hi]).astype(BF16)
        for kv in range(KV_HEADS):
            acc = pv[2 * kv] + pv[2 * kv + 1]
            denom = acc[:, KV_WIDTH:] + jnp.where(lane_lo, e_sink[2 * kv], e_sink[2 * kv + 1])
            ya = (acc[:, 0:KV_WIDTH] / denom).astype(BF16)
            p0, p1 = 2 * kv, 2 * kv + 1
            y_ref[b, :, M_WIDTH + p0 * 128:M_WIDTH + (p0 + 1) * 128] = ya[0:BLOCK]
            y_ref[b, :, M_WIDTH + p1 * 128:M_WIDTH + (p1 + 1) * 128] = ya[BLOCK:2 * BLOCK]

    for b in range(batch):
        st = first_matmuls(b)
        vector_work(b, st)
        second_matmuls(b, st)
        normalise(b, st)

    @pl.when(j == pl.num_programs(0) - 1)
    def _():
        c_out_ref[...] = c_ref[...]
        n_out_ref[...] = n_ref[...]


def _prompt_mixer(sinks, f32_pack, b16_pack, vt, rows, batch, seq):
    nblk = seq // BLOCK
    r3 = lambda a: a.reshape(batch, seq, a.shape[-1])
    def cur(cols, at=lambda j: j):
        lo, hi = cols
        assert lo % (hi - lo) == 0
        return pl.BlockSpec((batch, BLOCK, hi - lo), lambda j: (0, at(j), lo // (hi - lo)))
    kvx = (B16_COLS["kx"][0], B16_COLS["vx"][1])
    state = lambda shape: pl.BlockSpec((batch,) + shape, lambda j: (0,) * (len(shape) + 1))
    out_shape = (
        jax.ShapeDtypeStruct((batch, seq, D_MODEL), BF16),
        jax.ShapeDtypeStruct((batch, M_HEADS, M_DIM, M_DIM), F32),
        jax.ShapeDtypeStruct((batch, M_HEADS, 8, M_DIM), F32),
    )
    y, pc, pn = pl.pallas_call(
        _prompt_mixer_kernel,
        out_shape=out_shape,
        grid=(nblk,),
        in_specs=[pl.BlockSpec(memory_space=pltpu.SMEM),
                  cur((0, B16_PACK)), cur(kvx, at=lambda j: jnp.maximum(j - 1, 0)),
                  pl.BlockSpec((batch, 1, M_HEADS, M_DIM, BLOCK), lambda j: (0, j, 0, 0, 0)),
                  cur(F32_COLS["og"]), cur(F32_COLS["gate_cols"]),
                  pl.BlockSpec((batch, 1, 3, 8, BLOCK), lambda j: (0, j, 0, 0, 0))],
        out_specs=(cur((0, D_MODEL)), state((M_HEADS, M_DIM, M_DIM)), state((M_HEADS, 8, M_DIM))),
        scratch_shapes=[pltpu.VMEM((batch, M_HEADS, M_DIM, M_DIM), F32),
                        pltpu.VMEM((batch, M_HEADS, 8, M_DIM), F32)],
        compiler_params=pltpu.CompilerParams(dimension_semantics=("arbitrary",),
                                             vmem_limit_bytes=VMEM_LIMIT_BYTES),
        name="prompt_mixer",
    )(sinks, r3(b16_pack), r3(b16_pack), vt.reshape(batch, nblk, M_HEADS, M_DIM, BLOCK), r3(f32_pack),
      r3(f32_pack), rows)
    return y.reshape(batch * seq, D_MODEL), pc, pn


def _sample_mlstm_kernel(bi_ref, bf_ref, q_ref, k_ref, v_ref, og_ref, gates_ref, m0_ref, n0_ref, c_ref,
                         y_ref, n_out_ref, m_out_ref, c_out_ref, decay_ref, qr_ref):
    h = pl.program_id(0)
    nb = q_ref.shape[0]
    q_rows = q_ref[...].astype(F32)
    k_rows = k_ref[...].astype(F32)
    qr_ref[...] = q_rows
    qt, kt, vt = q_rows.T, k_rows.T, v_ref[...].astype(F32).T
    i_pre = gates_ref[pl.ds(h, 1), :] + bi_ref[h]
    a = _log_sigmoid(gates_ref[pl.ds(M_HEADS + h, 1), :] + bf_ref[h]) + m0_ref[pl.ds(h, 1), :]
    m_t = jnp.maximum(a, i_pre)
    w_inter = jnp.exp(a - m_t)
    w_in = jnp.exp(i_pre - m_t)
    scores = jnp.sum(qt * kt, axis=0, keepdims=True) * w_in
    n0t = n0_ref[...].T
    nq = jnp.sum(n0t * qt, axis=0, keepdims=True)

    rows = lax.broadcasted_iota(jnp.int32, (nb, M_DIM), 0)
    cols = lax.broadcasted_iota(jnp.int32, (M_DIM, nb), 1)
    vw_t = (vt * w_in).astype(BF16)

    decay_ref[...] = jnp.broadcast_to(w_inter, (M_DIM, nb)).T

    def body(grp, cq_t):
        base = grp * SAMPLE_UNROLL
        for u in range(SAMPLE_UNROLL):
            b = base + u
            col = jnp.sum(c_ref[b, 0] * qr_ref[pl.ds(b, 1), :], axis=-1, keepdims=True)
            cq_t = jnp.where(cols == b, col, cq_t)
        outer = []
        for u in range(SAMPLE_UNROLL):
            k_only_b = jnp.where(rows == base + u, k_rows, 0.0).astype(BF16)
            outer.append(_dot(vw_t, k_only_b))
        for u in range(SAMPLE_UNROLL):
            b = base + u
            c_out_ref[b, 0] = decay_ref[pl.ds(b, 1), :] * c_ref[b, 0] + outer[u]
        return cq_t

    cq_t = lax.fori_loop(0, nb // SAMPLE_UNROLL, body, jnp.zeros((M_DIM, nb), F32))

    num = w_inter * cq_t + scores * vt
    den = w_inter * nq + scores
    hh = num / jnp.maximum(jnp.abs(den), jnp.exp(-m_t))
    hh = hh * lax.rsqrt(jnp.mean(hh * hh, axis=0, keepdims=True) + RMS_EPS)
    y_ref[...] = hh.T * og_ref[...]
    n_out_ref[...] = (w_inter * n0t + w_in * kt).T
    m_out_ref[0] = m_t


def _sample_mlstm(b_i, b_f, b16_pack, f32_pack, gates_t, m0_t, n0, c0):
    nb = b16_pack.shape[0]
    smem = pl.BlockSpec(memory_space=pltpu.SMEM)
    head = lambda off: pl.BlockSpec((nb, M_DIM), lambda h: (0, off + h))
    qkv_at = B16_COLS["qkv"][0] // M_DIM
    og_at = F32_COLS["og"][0] // M_DIM
    out_shape = (
        jax.ShapeDtypeStruct((nb, M_WIDTH), F32),
        jax.ShapeDtypeStruct((nb, M_WIDTH), F32),
        jax.ShapeDtypeStruct((M_HEADS, 1, nb), F32),
        jax.ShapeDtypeStruct((nb, M_HEADS, M_DIM, M_DIM), F32),
    )
    c_spec = pl.BlockSpec((nb, 1, M_DIM, M_DIM), lambda h: (0, h, 0, 0))
    return pl.pallas_call(
        _sample_mlstm_kernel,
        out_shape=out_shape,
        grid=(M_HEADS,),
        in_specs=[smem, smem, head(qkv_at), head(qkv_at + M_HEADS), head(qkv_at + 2 * M_HEADS), head(og_at),
                  pl.BlockSpec(gates_t.shape, lambda h: (0, 0)), pl.BlockSpec(m0_t.shape, lambda h: (0, 0)),
                  head(0), c_spec],
        out_specs=(head(0), head(0), pl.BlockSpec((1, 1, nb), lambda h: (h, 0, 0)), c_spec),
        scratch_shapes=[pltpu.VMEM((nb, M_DIM), F32), pltpu.VMEM((nb, M_DIM), F32)],
        compiler_params=pltpu.CompilerParams(dimension_semantics=("arbitrary",),
                                             vmem_limit_bytes=VMEM_LIMIT_BYTES),
        name="sample_mlstm",
    )(b_i, b_f, b16_pack, b16_pack, b16_pack, f32_pack, gates_t, m0_t, n0, c0)


SAMPLE_TILE = 16
SAMPLE_UNROLL = 8


def _sample_swa_kernel(q2_ref, kc_ref, vc_ref, kn_ref, vn_ref, sink_ref, o_ref, ko_ref, vo_ref):
    sink = sink_ref[...] * LOG2E
    w = kc_ref.shape[1]
    tile = range(SAMPLE_TILE)
    q2 = [q2_ref[b] for b in tile]
    k_new = [kn_ref[b:b + 1, :] for b in tile]
    v_new = [vn_ref[b:b + 1, :] for b in tile]
    s_c = [_dot_nt(q2[b], kc_ref[b].astype(BF16)) for b in tile]
    s_n = [jnp.sum(q2[b].astype(F32) * k_new[b], axis=-1, keepdims=True) for b in tile]
    mx = [jnp.maximum(jnp.maximum(jnp.max(s_c[b], axis=-1, keepdims=True), s_n[b]), sink) for b in tile]
    p_c = [jnp.exp2(s_c[b] - mx[b]) for b in tile]
    p_n = [jnp.exp2(s_n[b] - mx[b]) for b in tile]
    denom = [jnp.sum(p_c[b], axis=-1, keepdims=True) + p_n[b] + jnp.exp2(sink - mx[b]) for b in tile]
    o = [_dot(p_c[b].astype(BF16), vc_ref[b].astype(BF16)) for b in tile]
    for b in tile:
        o_ref[b] = (o[b] + p_n[b] * v_new[b]) / denom[b]
        ko_ref[b, 0:w - 1, :] = kc_ref[b, 1:w, :]
        ko_ref[b, w - 1:w, :] = k_new[b]
        vo_ref[b, 0:w - 1, :] = vc_ref[b, 1:w, :]
        vo_ref[b, w - 1:w, :] = v_new[b]


def _sample_swa(q2, k_cache, v_cache, k_new, v_new, sink_col):
    nb, w, _ = k_cache.shape
    t3 = lambda a, c: pl.BlockSpec((SAMPLE_TILE, a, c), lambda i: (i, 0, 0))
    t2 = pl.BlockSpec((SAMPLE_TILE, KV_WIDTH), lambda i: (i, 0))
    out_shape = (
        jax.ShapeDtypeStruct((nb, A_HEADS, KV_WIDTH), F32),
        jax.ShapeDtypeStruct((nb, w, KV_WIDTH), F32),
        jax.ShapeDtypeStruct((nb, w, KV_WIDTH), F32),
    )
    return pl.pallas_call(
        _sample_swa_kernel,
        out_shape=out_shape,
        grid=(nb // SAMPLE_TILE,),
        in_specs=[t3(A_HEADS, KV_WIDTH), t3(w, KV_WIDTH), t3(w, KV_WIDTH), t2, t2,
                  pl.BlockSpec((A_HEADS, 1), lambda i: (0, 0))],
        out_specs=(t3(A_HEADS, KV_WIDTH), t3(w, KV_WIDTH), t3(w, KV_WIDTH)),
        compiler_params=pltpu.CompilerParams(dimension_semantics=("arbitrary",)),
        name="sample_swa",
    )(q2, k_cache, v_cache, k_new, v_new, sink_col)


def _rearranged_w_in(w_in):
    m_end = 4 * M_WIDTH
    g_end = m_end + 2 * M_HEADS
    gates = jnp.pad(w_in[:, m_end:g_end], ((0, 0), (0, GATE_PAD - 2 * M_HEADS)))
    w_in_r = jnp.concatenate([w_in[:, :m_end], w_in[:, g_end:], gates], axis=1).astype(BF16)
    assert w_in_r.shape[1] == W_IN_COLS
    return w_in_r


def kernel(x_prompt, x_sample, cache_swa_k, cache_swa_v, state_mlstm_C, state_mlstm_n, state_mlstm_m,
           ffn1_norm, ffn1_w_gate, ffn1_w_up, ffn1_w_down, mix_norm, w_in, mlstm_b_i, mlstm_b_f,
           mlstm_out_norm, swa_q_norm, swa_k_norm, swa_sinks, w_out, ffn2_norm, ffn2_w_gate,
           ffn2_w_up, ffn2_w_down):
    depth = ffn1_norm.shape[0]
    assert depth == 1
    batch, seq, _ = x_prompt.shape
    nb = x_sample.shape[0]
    assert x_sample.shape[1] == 1 and seq % BLOCK == 0

    wg1, wu1, wd1 = (w[0].astype(BF16) for w in (ffn1_w_gate, ffn1_w_up, ffn1_w_down))
    w_in_r = _rearranged_w_in(w_in[0])
    n1 = ffn1_norm[0].reshape(1, D_MODEL)
    n2 = mix_norm[0].reshape(1, D_MODEL)
    n3 = ffn2_norm[0].reshape(1, D_MODEL)
    qgain = jnp.tile(swa_q_norm[0], A_HEADS).reshape(1, A_WIDTH)
    kgain = jnp.tile(swa_k_norm[0], KV_HEADS).reshape(1, KV_WIDTH)
    ogain = mlstm_out_norm[0].reshape(1, M_WIDTH)
    b_i, b_f = mlstm_b_i[0], mlstm_b_f[0]
    bias8 = jnp.concatenate([b_i, b_f]).reshape(2 * M_HEADS, 1)
    front_params = (n1, wg1, wu1, wd1, n2, w_in_r, qgain, kgain, ogain, bias8)
    sinks = swa_sinks[0]

    xp = x_prompt.reshape(batch * seq, D_MODEL)
    xs = x_sample.reshape(nb, D_MODEL)
    back_weights_f32 = (w_out[0], ffn2_w_gate[0], ffn2_w_up[0], ffn2_w_down[0])
    prompt_front, sample_front, (wo, wg2, wu2, wd2) = _front(xp, xs, front_params, back_weights_f32,
                                                             tm=512, seq=seq)
    back_params = (wo, n3, wg2, wu2, wd2)
    f32_p, b16_p, ka, va, vt, rows, pm = prompt_front
    f32_s, b16_s, kas, vas, _, gates_t = sample_front

    rows = rows.reshape(batch, seq // BLOCK, 3, 8, BLOCK)
    y, pc, pn = _prompt_mixer(sinks, f32_p, b16_p, vt, rows, batch, seq)
    pk = ka.reshape(batch, WINDOW, KV_HEADS, A_DIM)
    pv = va.reshape(batch, WINDOW, KV_HEADS, A_DIM)
    pn = pn[:, :, 0, :]
    pm = pm[:, 0:M_HEADS, 0]

    y_m, sn, mt, sc = _sample_mlstm(b_i, b_f, b16_s, f32_s, gates_t, jnp.transpose(state_mlstm_m[0]),
                                    state_mlstm_n[0].reshape(nb, M_WIDTH), state_mlstm_C[0])
    sn = sn.reshape(nb, M_HEADS, M_DIM)
    sm = jnp.transpose(mt[:, 0, :])

    qa_h = b16_s[:, B16_COLS["qa"][0]:B16_COLS["qa"][1]].reshape(nb, A_HEADS, A_DIM)
    zeros = jnp.zeros_like(qa_h)
    in_lo = (jnp.arange(A_HEADS) // A_GROUP == 0)[None, :, None]
    q2 = jnp.concatenate([jnp.where(in_lo, qa_h, zeros), jnp.where(in_lo, zeros, qa_h)], axis=-1)
    kc = cache_swa_k[0].reshape(nb, -1, KV_WIDTH)
    vc = cache_swa_v[0].reshape(nb, -1, KV_WIDTH)
    o2, sk, sv = _sample_swa(q2, kc, vc, kas, vas, sinks.reshape(A_HEADS, 1))
    o2 = o2.reshape(nb, A_HEADS, KV_HEADS, A_DIM)
    y_a = jnp.where(in_lo, o2[:, :, 0, :], o2[:, :, 1, :]).reshape(nb, A_WIDTH)
    ys_in = jnp.concatenate([y_m, y_a], axis=-1).astype(BF16)

    yp, ys = _back(f32_p, y, f32_s, ys_in, back_params, tm=1024)
    yp = yp.reshape(batch, seq, D_MODEL)
    ys = ys.reshape(nb, 1, D_MODEL)

    wb = kc.shape[1]
    return (yp, ys, pk[None], pv[None], pc[None], pn[None], pm[None],
            sk.reshape(1, nb, wb, KV_HEADS, A_DIM), sv.reshape(1, nb, wb, KV_HEADS, A_DIM),
            sc[None], sn[None], sm[None])
```

```python
import functools

import jax
import jax.numpy as jnp
from jax import lax
from jax.experimental import pallas as pl
from jax.experimental.pallas import tpu as pltpu

F32 = jnp.float32
BF16 = jnp.bfloat16

D_MODEL = 1024
D_FF = 2816
FF_CHUNK = 256
N_FF_CHUNKS = D_FF // FF_CHUNK
M_HEADS = 4
M_DIM = 128
M_WIDTH = M_HEADS * M_DIM
A_HEADS = 8
A_DIM = 64
A_WIDTH = A_HEADS * A_DIM
KV_HEADS = 2
KV_WIDTH = KV_HEADS * A_DIM
A_GROUP = A_HEADS // KV_HEADS
WINDOW = 128
BLOCK = 128
GATE_PAD = 128
RMS_EPS = 1e-6
FFN_RES_WEIGHT = 0.5
NEG_INF = float("-inf")
LOG2E = 1.4426950408889634
VMEM_LIMIT_BYTES = 56 * 1024 * 1024


def _dot(a, b):
    return jnp.dot(a, b, preferred_element_type=F32)


def _dot_nt(a, b):
    return lax.dot_general(a, b, (((1,), (1,)), ((), ())), preferred_element_type=F32)


def _rms_rows(x, gain):
    ms = jnp.mean(x * x, axis=-1, keepdims=True)
    return x * lax.rsqrt(ms + RMS_EPS) * gain


def _log_sigmoid(x):
    return jnp.minimum(x, 0.0) - jnp.log1p(jnp.exp(-jnp.abs(x)))


def _half_tile_mean_sq(x):
    in_lo = lax.broadcasted_iota(jnp.int32, (x.shape[0], 128), 1) < A_DIM
    out = []
    for c in range(x.shape[1] // 128):
        sq = x[:, c * 128:(c + 1) * 128]
        sq = sq * sq
        s_lo = jnp.sum(jnp.where(in_lo, sq, 0.0), axis=-1, keepdims=True)
        s_hi = jnp.sum(jnp.where(in_lo, 0.0, sq), axis=-1, keepdims=True)
        out.append(jnp.where(in_lo, s_lo, s_hi) * (1.0 / A_DIM))
    return out[0] if len(out) == 1 else jnp.concatenate(out, axis=1)


def _ffn(h_ref, wg_ref, wu_ref, wd_ref, act_ref):
    for c in range(N_FF_CHUNKS):
        lo, hi = c * FF_CHUNK, (c + 1) * FF_CHUNK
        h = h_ref[...]
        g = _dot(h, wg_ref[:, lo:hi])
        u = _dot(h, wu_ref[:, lo:hi])
        act_ref[:, lo:hi] = (g * jax.nn.sigmoid(g) * u).astype(BF16)
    return _dot(act_ref[...], wd_ref[...])


N_FRONT_PARAMS = 10
N_FRONT_COMMON_OUTS = 9
N_PROMPT_GATE_OUTS = 3
W_QA = 4 * M_WIDTH
W_KVG = W_QA + A_WIDTH
W_IN_COLS = W_KVG + 2 * KV_WIDTH + GATE_PAD


def _prompt_gates(gates_t, bias_ref, first_of_seq, m_ref, rows_ref, cols_ref, mfin_ref):
    sub = lax.broadcasted_iota(jnp.int32, (8, BLOCK), 0)
    lane = lax.broadcasted_iota(jnp.int32, (8, BLOCK), 1)
    is_head = sub < M_HEADS
    pad = jnp.zeros((BLOCK - 24, BLOCK), F32)
    scan_shifts = (1, 2, 4, 8, 16, 32, 64)
    m_prev = jnp.where(first_of_seq, 0.0, m_ref[...])[:, 0:1]
    for c in range(rows_ref.shape[0]):
        pre = gates_t[:, c * BLOCK:(c + 1) * BLOCK] + bias_ref[...]
        r = jnp.where(is_head, pre, _log_sigmoid(pre))
        cum = r
        for shift in scan_shifts:
            cum = cum + jnp.where(lane >= shift, pltpu.roll(cum, shift, 1), 0.0)
        bcum = pltpu.roll(cum, M_HEADS, 0)
        g = jnp.where(is_head, r - bcum, 0.0)
        bcum = jnp.where(is_head, bcum, 0.0)
        cm = g
        for shift in scan_shifts:
            cm = jnp.maximum(cm, jnp.where(lane >= shift, pltpu.roll(cm, shift, 1), NEG_INF))
        cm_last = jnp.max(cm, axis=-1, keepdims=True)
        b_last = jnp.sum(jnp.where(lane == BLOCK - 1, bcum, 0.0), axis=-1, keepdims=True)
        mx = jnp.maximum(m_prev, cm)
        mx_last = jnp.maximum(m_prev, cm_last)
        rows_ref[c, 0] = g * LOG2E
        rows_ref[c, 1] = jnp.exp(g - mx_last)
        rows_ref[c, 2] = jnp.broadcast_to(jnp.exp(m_prev - mx_last), (8, BLOCK))
        col_src = jnp.concatenate([mx * -LOG2E, jnp.exp(m_prev - mx), jnp.exp(-(bcum + mx)), pad], axis=0)
        cols_ref[c * BLOCK:(c + 1) * BLOCK, :] = col_src.T
        m_prev = b_last + mx_last
    m_full = jnp.broadcast_to(m_prev, (8, BLOCK))
    m_ref[...] = m_full
    mfin_ref[0] = m_full


def _front_tile(x_ref, params, outs, gate_sink, h_ref, act_ref):
    (n1_ref, wg_ref, wu_ref, wd_ref, n2_ref, win_ref, qgain_ref, kgain_ref, ogain_ref, _) = params
    (x1_ref, qkv_ref, og_ref, qa_ref, ka_ref, va_ref, vt_ref, kx_ref, vx_ref) = outs
    wm_ref = win_ref
    x = x_ref[...]
    h_ref[...] = _rms_rows(x, n1_ref[...]).astype(BF16)
    x1 = x + FFN_RES_WEIGHT * _ffn(h_ref, wg_ref, wu_ref, wd_ref, act_ref)
    x1_ref[...] = x1
    h_ref[...] = _rms_rows(x1, n2_ref[...]).astype(BF16)
    h = h_ref[...]

    qa = _dot(h, win_ref[:, W_QA:W_KVG])
    kvg = _dot(h, win_ref[:, W_KVG:W_IN_COLS])
    v_m = _dot(h, wm_ref[:, 2 * M_WIDTH:3 * M_WIDTH])
    qkv_ref[:, 2 * M_WIDTH:3 * M_WIDTH] = v_m.astype(BF16)
    for c in range(vt_ref.shape[0]):
        for hd in range(M_HEADS):
            blk = v_m[c * BLOCK:(c + 1) * BLOCK, hd * M_DIM:(hd + 1) * M_DIM]
            vt_ref[c, hd] = blk.T.astype(BF16)
    q_scale = qgain_ref[...] * (A_DIM ** -0.5 * LOG2E)
    qa_ref[...] = (qa * lax.rsqrt(_half_tile_mean_sq(qa) + RMS_EPS) * q_scale).astype(BF16)
    ka = kvg[:, 0:KV_WIDTH]
    ka = ka * lax.rsqrt(_half_tile_mean_sq(ka) + RMS_EPS) * kgain_ref[...]
    va = kvg[:, KV_WIDTH:2 * KV_WIDTH]
    keep = ka_ref.shape[0]
    ka_ref[...] = ka[ka.shape[0] - keep:, :]
    va_ref[...] = va[va.shape[0] - keep:, :]
    gate_sink(kvg[:, 2 * KV_WIDTH:2 * KV_WIDTH + GATE_PAD].T[0:2 * M_HEADS, :])

    in_lo = lax.broadcasted_iota(jnp.int32, ka.shape, 1) < A_DIM
    for src, dst in ((ka, kx_ref), (va, vx_ref)):
        x0 = jnp.where(in_lo, src, 0.0)
        x1 = jnp.where(in_lo, 0.0, src)
        dst[:, 0:128] = x0.astype(BF16)
        dst[:, 128:256] = pltpu.roll(x0, A_DIM, 1).astype(BF16)
        dst[:, 256:384] = pltpu.roll(x1, A_DIM, 1).astype(BF16)
        dst[:, 384:512] = x1.astype(BF16)

    og_ref[...] = jax.nn.sigmoid(_dot(h, wm_ref[:, 3 * M_WIDTH:4 * M_WIDTH])) * ogain_ref[...]
    k_m = _dot(h, wm_ref[:, M_WIDTH:2 * M_WIDTH]) * (M_DIM ** -0.5)
    qkv_ref[:, M_WIDTH:2 * M_WIDTH] = k_m.astype(BF16)
    qkv_ref[:, 0:M_WIDTH] = _dot(h, wm_ref[:, 0:M_WIDTH]).astype(BF16)


def _front_kernel(*refs, n_tiles, tiles_per_seq, n_cast):
    refs = list(refs)
    take = lambda k: [refs.pop(0) for _ in range(k)]
    x_ref, xs_ref = take(2)
    params = take(N_FRONT_PARAMS)
    cast_src = take(n_cast)
    outs_p = take(N_FRONT_COMMON_OUTS)
    rows_ref, cols_ref, mfin_ref = take(N_PROMPT_GATE_OUTS)
    outs_s = take(N_FRONT_COMMON_OUTS)
    (gts_ref,) = take(1)
    cast_dst = take(n_cast)
    h_ref, act_ref, m_ref = refs
    bias_ref = params[-1]
    i = pl.program_id(0)

    @pl.when(i < n_tiles)
    def _():
        for src, dst in zip(cast_src, cast_dst):
            dst[...] = src[...].astype(BF16)
        gates = functools.partial(_prompt_gates, bias_ref=bias_ref, first_of_seq=i % tiles_per_seq == 0,
                                  m_ref=m_ref, rows_ref=rows_ref, cols_ref=cols_ref, mfin_ref=mfin_ref)
        _front_tile(x_ref, params, outs_p, gates, h_ref, act_ref)

    @pl.when(i == n_tiles)
    def _():
        ns = xs_ref.shape[0]

        def raw_gates(gates_t):
            gts_ref[...] = gates_t

        _front_tile(xs_ref, params, outs_s, raw_gates, h_ref.at[0:ns], act_ref.at[0:ns])


def _const_spec(shape):
    nd = len(shape)
    return pl.BlockSpec(shape, lambda i: (0,) * nd, pipeline_mode=pl.Buffered(1))


def _whole_spec(shape):
    nd = len(shape)
    return pl.BlockSpec(shape, lambda i: (0,) * nd)


def _front_out_shapes(n, n_cache_rows):
    return (
        jax.ShapeDtypeStruct((n, D_MODEL), F32),
        jax.ShapeDtypeStruct((n, 3 * M_WIDTH), BF16),
        jax.ShapeDtypeStruct((n, M_WIDTH), F32),
        jax.ShapeDtypeStruct((n, A_WIDTH), BF16),
        jax.ShapeDtypeStruct((n_cache_rows, KV_WIDTH), F32),
        jax.ShapeDtypeStruct((n_cache_rows, KV_WIDTH), F32),
        jax.ShapeDtypeStruct((n // BLOCK, M_HEADS, M_DIM, BLOCK), BF16),
        jax.ShapeDtypeStruct((n, 4 * KV_WIDTH), BF16),
        jax.ShapeDtypeStruct((n, 4 * KV_WIDTH), BF16),
    )


def _cast_chunk_count(n_rows, max_chunks):
    for k in range(max_chunks, 0, -1):
        if n_rows % k == 0 and (n_rows // k) % 16 == 0:
            return k
    raise ValueError(n_rows)


def _front(x2d, xs2d, params, later_weights, tm, seq):
    n, ns = x2d.shape[0], xs2d.shape[0]
    assert seq % tm == 0 and tm >= WINDOW and len(params) == N_FRONT_PARAMS
    n_tiles = n // tm
    n_seqs = n // seq
    tiles_per_seq = seq // tm
    nb_t = tm // BLOCK
    tile = lambda i: jnp.minimum(i, n_tiles - 1)
    seq_of = lambda i: tile(i) // tiles_per_seq
    row = lambda w: pl.BlockSpec((tm, w), lambda i: (tile(i), 0))
    tail = pl.BlockSpec((WINDOW, KV_WIDTH), lambda i: (seq_of(i), 0))
    prompt_specs = (row(D_MODEL), row(3 * M_WIDTH), row(M_WIDTH), row(A_WIDTH), tail, tail,
                    pl.BlockSpec((nb_t, M_HEADS, M_DIM, BLOCK), lambda i: (tile(i), 0, 0, 0)),
                    row(4 * KV_WIDTH), row(4 * KV_WIDTH),
                    pl.BlockSpec((nb_t, 3, 8, BLOCK), lambda i: (tile(i), 0, 0, 0)), row(128),
                    pl.BlockSpec((1, 8, BLOCK), lambda i: (seq_of(i), 0, 0)))
    prompt_shapes = _front_out_shapes(n, n_seqs * WINDOW) + (
        jax.ShapeDtypeStruct((n // BLOCK, 3, 8, BLOCK), F32),
        jax.ShapeDtypeStruct((n, 128), F32),
        jax.ShapeDtypeStruct((n_seqs, 8, BLOCK), F32),
    )
    sample_shapes = _front_out_shapes(ns, ns) + (
        jax.ShapeDtypeStruct((2 * M_HEADS, ns), F32),
    )

    def chunk_spec(w):
        k = _cast_chunk_count(w.shape[0], n_tiles)
        return pl.BlockSpec((w.shape[0] // k, w.shape[1]), lambda i: (jnp.minimum(i, k - 1), 0))

    cast_specs = [chunk_spec(w) for w in later_weights]
    cast_shapes = tuple(jax.ShapeDtypeStruct(w.shape, BF16) for w in later_weights)
    outs = pl.pallas_call(
        functools.partial(_front_kernel, n_tiles=n_tiles, tiles_per_seq=tiles_per_seq, n_cast=len(later_weights)),
        out_shape=prompt_shapes + sample_shapes + cast_shapes,
        grid=(n_tiles + 1,),
        in_specs=[row(D_MODEL), _whole_spec(xs2d.shape)] + [_const_spec(p.shape) for p in params] + cast_specs,
        out_specs=prompt_specs + tuple(_whole_spec(s.shape) for s in sample_shapes) + tuple(cast_specs),
        scratch_shapes=[pltpu.VMEM((tm, D_MODEL), BF16), pltpu.VMEM((tm, D_FF), BF16),
                        pltpu.VMEM((8, BLOCK), F32)],
        compiler_params=pltpu.CompilerParams(dimension_semantics=("arbitrary",),
                                             vmem_limit_bytes=VMEM_LIMIT_BYTES),
        name="front",
    )(x2d, xs2d, *params, *later_weights)
    n_p, n_s = len(prompt_shapes), len(sample_shapes)
    return outs[:n_p], outs[n_p:n_p + n_s], outs[n_p + n_s:]


def _back_tile(x1_ref, y_ref, params, out_ref, h_ref, act_ref):
    wo_ref, n_ref, wg_ref, wu_ref, wd_ref = params
    x2 = x1_ref[...] + _dot(y_ref[...], wo_ref[...])
    h_ref[...] = _rms_rows(x2, n_ref[...]).astype(BF16)
    out_ref[...] = x2 + FFN_RES_WEIGHT * _ffn(h_ref, wg_ref, wu_ref, wd_ref, act_ref)


def _back_kernel(x1_ref, y_ref, x1s_ref, ys_ref, wo_ref, n_ref, wg_ref, wu_ref, wd_ref, out_ref, outs_ref,
                 h_ref, act_ref, *, n_tiles):
    params = (wo_ref, n_ref, wg_ref, wu_ref, wd_ref)
    i = pl.program_id(0)

    @pl.when(i < n_tiles)
    def _():
        _back_tile(x1_ref, y_ref, params, out_ref, h_ref, act_ref)

    @pl.when(i == n_tiles)
    def _():
        ns = x1s_ref.shape[0]
        _back_tile(x1s_ref, ys_ref, params, outs_ref, h_ref.at[0:ns], act_ref.at[0:ns])


def _back(x1, y, x1s, ys, params, tm):
    n, ns = x1.shape[0], x1s.shape[0]
    n_tiles = n // tm
    row = pl.BlockSpec((tm, D_MODEL), lambda i: (jnp.minimum(i, n_tiles - 1), 0))
    return pl.pallas_call(
        functools.partial(_back_kernel, n_tiles=n_tiles),
        out_shape=(jax.ShapeDtypeStruct((n, D_MODEL), F32), jax.ShapeDtypeStruct((ns, D_MODEL), F32)),
        grid=(n_tiles + 1,),
        in_specs=[row, row, _whole_spec(x1s.shape), _whole_spec(ys.shape)] + [_const_spec(p.shape) for p in params],
        out_specs=(row, _whole_spec((ns, D_MODEL))),
        scratch_shapes=[pltpu.VMEM((tm, D_MODEL), BF16), pltpu.VMEM((tm, D_FF), BF16)],
        compiler_params=pltpu.CompilerParams(dimension_semantics=("arbitrary",),
                                             vmem_limit_bytes=VMEM_LIMIT_BYTES),
        name="back",
    )(x1, y, x1s, ys, *params)


def _prompt_mixer_kernel(sinks_ref, qkv_ref, vt_ref, og_ref, cols_ref, rows_ref, qa_ref,
                         kx_ref, vx_ref, kxp_ref, vxp_ref,
                         y_ref, c_out_ref, n_out_ref,
                         c_ref, n_ref):
    j = pl.program_id(0)
    batch = qkv_ref.shape[0]

    @pl.when(j == 0)
    def _():
        c_ref[...] = jnp.zeros_like(c_ref)
        n_ref[...] = jnp.zeros_like(n_ref)

    rows = lax.broadcasted_iota(jnp.int32, (BLOCK, BLOCK), 0)
    cols = lax.broadcasted_iota(jnp.int32, (BLOCK, BLOCK), 1)
    causal = cols <= rows
    lane_lo = lax.broadcasted_iota(jnp.int32, (2 * BLOCK, KV_WIDTH), 1) < A_DIM
    ones_m = jnp.ones((BLOCK, M_DIM), BF16)
    ones_half = (jnp.where(lane_lo, 1.0, 0.0).astype(BF16), jnp.where(lane_lo, 0.0, 1.0).astype(BF16))
    qi = lax.broadcasted_iota(jnp.int32, (2 * BLOCK, 2 * BLOCK), 0) % BLOCK
    kc = lax.broadcasted_iota(jnp.int32, (2 * BLOCK, 2 * BLOCK), 1)
    first_valid = jnp.where(j == 0, BLOCK, 0)
    valid = (kc >= qi) & (kc <= qi + WINDOW) & (kc >= first_valid)
    top_rows = lax.broadcasted_iota(jnp.int32, (2 * BLOCK, 1), 0) < BLOCK

    heads = range(M_HEADS)
    groups = [(kv, parity) for kv in range(KV_HEADS) for parity in range(2)]
    m_lo = lambda h: h * M_DIM

    def first_matmuls(b):
        st = {"col": cols_ref[b]}
        k = [qkv_ref[b, :, M_WIDTH + m_lo(h):M_WIDTH + m_lo(h) + M_DIM] for h in heads]
        st["c_prev"] = [c_ref[b, h] for h in heads]
        st["n_prev"] = [n_ref[b, h] for h in heads]
        st["qkc"], st["upd"], st["sc"] = [], [], []
        for h in heads:
            q = qkv_ref[b, :, m_lo(h):m_lo(h) + M_DIM]
            n_rep = jnp.broadcast_to(st["n_prev"][h][0:1, :], (BLOCK, M_DIM)).astype(BF16)
            rhs = jnp.concatenate([k[h], st["c_prev"][h].astype(BF16), n_rep], axis=0)
            st["qkc"].append(_dot_nt(q, rhs))
        for h in heads:
            wend_row = rows_ref[b, 0, 1][h:h + 1, :]
            vw_t = (vt_ref[b, 0, h].astype(F32) * wend_row).astype(BF16)
            w_rep = jnp.broadcast_to(wend_row, (16, BLOCK)).astype(BF16)
            st["upd"].append(_dot(jnp.concatenate([vw_t, w_rep], axis=0), k[h]))
        q2 = [jnp.concatenate([qa_ref[b, :, (2 * kv) * 128:(2 * kv + 1) * 128],
                               qa_ref[b, :, (2 * kv + 1) * 128:(2 * kv + 2) * 128]], axis=0)
              for kv in range(KV_HEADS)]
        for kv, parity in groups:
            var = (2 * kv + parity) * KV_WIDTH
            k_band = jnp.concatenate([kxp_ref[b, :, var:var + KV_WIDTH], kx_ref[b, :, var:var + KV_WIDTH]], axis=0)
            st["sc"].append(_dot_nt(q2[kv], k_band))
        return st

    def vector_work(b, st):
        st["s"], st["pr"], st["e_sink"] = [], [], []
        for h in heads:
            g_row = rows_ref[b, 0, 0][h:h + 1, :]
            decay = rows_ref[b, 0, 2][h:h + 1, :]
            d = jnp.where(causal, jnp.exp2(st["col"][:, h:h + 1] + g_row), 0.0)
            st["s"].append((st["qkc"][h][:, 0:BLOCK] * d).astype(BF16))
            c_ref[b, h] = decay * st["c_prev"][h] + st["upd"][h][0:M_DIM]
            n_ref[b, h] = decay * st["n_prev"][h] + st["upd"][h][M_DIM:M_DIM + 8]
        for gi, (kv, parity) in enumerate(groups):
            sink = jnp.where(top_rows, sinks_ref[4 * kv + parity], sinks_ref[4 * kv + 2 + parity]) * LOG2E
            scm = jnp.where(valid, st["sc"][gi], NEG_INF)
            mx = jnp.maximum(jnp.max(scm, axis=-1, keepdims=True), sink)
            st["pr"].append(jnp.exp2(scm - mx).astype(BF16))
            st["e_sink"].append(jnp.exp2(sink - mx))

    def second_matmuls(b, st):
        st["sv"], st["pv"] = [], []
        for h in heads:
            v = qkv_ref[b, :, 2 * M_WIDTH + m_lo(h):2 * M_WIDTH + m_lo(h) + M_DIM]
            st["sv"].append(_dot(st["s"][h], jnp.concatenate([v, ones_m], axis=1)))
        for gi, (kv, parity) in enumerate(groups):
            var = (2 * kv + parity) * KV_WIDTH
            v_band = jnp.concatenate([vxp_ref[b, :, var:var + KV_WIDTH], vx_ref[b, :, var:var + KV_WIDTH]], axis=0)
            st["pv"].append(_dot(st["pr"][gi], jnp.concatenate([v_band, ones_half[parity]], axis=1)))

    def normalise(b, st):
        col_b, qkc, sv, pv, e_sink = st["col"], st["qkc"], st["sv"], st["pv"], st["e_sink"]
        for h in heads:
            lo, hi = m_lo(h), m_lo(h) + M_DIM
            wi_col = col_b[:, 8 + h:9 + h]
            num = wi_col * qkc[h][:, BLOCK:2 * BLOCK] + sv[h][:, 0:M_DIM]
            den = wi_col * qkc[h][:, 2 * BLOCK:3 * BLOCK] + sv[h][:, M_DIM:2 * M_DIM]
            hh = num / jnp.maximum(jnp.abs(den), col_b[:, 16 + h:17 + h])
            hh = hh * lax.rsqrt(jnp.mean(hh * hh, axis=-1, keepdims=True) + RMS_EPS)
            y_ref[b, :, lo:hi] = (hh * og_ref[b, :, lo:hi]).astype(BF16)
        for kv in range(KV_HEADS):
            acc = pv[2 * kv] + pv[2 * kv + 1]
            denom = acc[:, KV_WIDTH:] + jnp.where(lane_lo, e_sink[2 * kv], e_sink[2 * kv + 1])
            ya = (acc[:, 0:KV_WIDTH] / denom).astype(BF16)
            p0, p1 = 2 * kv, 2 * kv + 1
            y_ref[b, :, M_WIDTH + p0 * 128:M_WIDTH + (p0 + 1) * 128] = ya[0:BLOCK]
            y_ref[b, :, M_WIDTH + p1 * 128:M_WIDTH + (p1 + 1) * 128] = ya[BLOCK:2 * BLOCK]

    for b in range(batch):
        st = first_matmuls(b)
        vector_work(b, st)
        second_matmuls(b, st)
        normalise(b, st)

    @pl.when(j == pl.num_programs(0) - 1)
    def _():
        c_out_ref[...] = c_ref[...]
        n_out_ref[...] = n_ref[...]


def _prompt_mixer(sinks, qkv, vt, og, cols, rows, qa, kx, vx, batch, seq):
    nblk = seq // BLOCK
    r3 = lambda a: a.reshape(batch, seq, a.shape[-1])
    cur = lambda w: pl.BlockSpec((batch, BLOCK, w), lambda j: (0, j, 0))
    prev = lambda w: pl.BlockSpec((batch, BLOCK, w), lambda j: (0, jnp.maximum(j - 1, 0), 0))
    state = lambda shape: pl.BlockSpec((batch,) + shape, lambda j: (0,) * (len(shape) + 1))
    out_shape = (
        jax.ShapeDtypeStruct((batch, seq, D_MODEL), BF16),
        jax.ShapeDtypeStruct((batch, M_HEADS, M_DIM, M_DIM), F32),
        jax.ShapeDtypeStruct((batch, M_HEADS, 8, M_DIM), F32),
    )
    y, pc, pn = pl.pallas_call(
        _prompt_mixer_kernel,
        out_shape=out_shape,
        grid=(nblk,),
        in_specs=[pl.BlockSpec(memory_space=pltpu.SMEM),
                  cur(3 * M_WIDTH),
                  pl.BlockSpec((batch, 1, M_HEADS, M_DIM, BLOCK), lambda j: (0, j, 0, 0, 0)),
                  cur(M_WIDTH), cur(128),
                  pl.BlockSpec((batch, 1, 3, 8, BLOCK), lambda j: (0, j, 0, 0, 0)),
                  cur(A_WIDTH), cur(4 * KV_WIDTH), cur(4 * KV_WIDTH), prev(4 * KV_WIDTH), prev(4 * KV_WIDTH)],
        out_specs=(cur(D_MODEL), state((M_HEADS, M_DIM, M_DIM)), state((M_HEADS, 8, M_DIM))),
        scratch_shapes=[pltpu.VMEM((batch, M_HEADS, M_DIM, M_DIM), F32),
                        pltpu.VMEM((batch, M_HEADS, 8, M_DIM), F32)],
        compiler_params=pltpu.CompilerParams(dimension_semantics=("arbitrary",),
                                             vmem_limit_bytes=VMEM_LIMIT_BYTES),
        name="prompt_mixer",
    )(sinks, r3(qkv), vt.reshape(batch, nblk, M_HEADS, M_DIM, BLOCK), r3(og), r3(cols), rows, r3(qa),
      r3(kx), r3(vx), r3(kx), r3(vx))
    return y.reshape(batch * seq, D_MODEL), pc, pn


def _sample_mlstm_kernel(bi_ref, bf_ref, q_ref, k_ref, v_ref, og_ref, gates_ref, m0_ref, n0_ref, c_ref,
                         y_ref, n_out_ref, m_out_ref, c_out_ref, decay_ref, qr_ref):
    h = pl.program_id(0)
    nb = q_ref.shape[0]
    q_rows = q_ref[...].astype(F32)
    k_rows = k_ref[...].astype(F32)
    qr_ref[...] = q_rows
    qt, kt, vt = q_rows.T, k_rows.T, v_ref[...].astype(F32).T
    i_pre = gates_ref[pl.ds(h, 1), :] + bi_ref[h]
    a = _log_sigmoid(gates_ref[pl.ds(M_HEADS + h, 1), :] + bf_ref[h]) + m0_ref[pl.ds(h, 1), :]
    m_t = jnp.maximum(a, i_pre)
    w_inter = jnp.exp(a - m_t)
    w_in = jnp.exp(i_pre - m_t)
    scores = jnp.sum(qt * kt, axis=0, keepdims=True) * w_in
    n0t = n0_ref[...].T
    nq = jnp.sum(n0t * qt, axis=0, keepdims=True)

    rows = lax.broadcasted_iota(jnp.int32, (nb, M_DIM), 0)
    cols = lax.broadcasted_iota(jnp.int32, (M_DIM, nb), 1)
    vw_t = (vt * w_in).astype(BF16)

    decay_ref[...] = jnp.broadcast_to(w_inter, (M_DIM, nb)).T

    def body(grp, cq_t):
        base = grp * SAMPLE_UNROLL
        for u in range(SAMPLE_UNROLL):
            b = base + u
            col = jnp.sum(c_ref[b, 0] * qr_ref[pl.ds(b, 1), :], axis=-1, keepdims=True)
            cq_t = jnp.where(cols == b, col, cq_t)
        outer = []
        for u in range(SAMPLE_UNROLL):
            k_only_b = jnp.where(rows == base + u, k_rows, 0.0).astype(BF16)
            outer.append(_dot(vw_t, k_only_b))
        for u in range(SAMPLE_UNROLL):
            b = base + u
            c_out_ref[b, 0] = decay_ref[pl.ds(b, 1), :] * c_ref[b, 0] + outer[u]
        return cq_t

    cq_t = lax.fori_loop(0, nb // SAMPLE_UNROLL, body, jnp.zeros((M_DIM, nb), F32))

    num = w_inter * cq_t + scores * vt
    den = w_inter * nq + scores
    hh = num / jnp.maximum(jnp.abs(den), jnp.exp(-m_t))
    hh = hh * lax.rsqrt(jnp.mean(hh * hh, axis=0, keepdims=True) + RMS_EPS)
    y_ref[...] = hh.T * og_ref[...]
    n_out_ref[...] = (w_inter * n0t + w_in * kt).T
    m_out_ref[0] = m_t


def _sample_mlstm(b_i, b_f, qkv, og, gates_t, m0_t, n0, c0):
    nb = qkv.shape[0]
    smem = pl.BlockSpec(memory_space=pltpu.SMEM)
    head = lambda off: pl.BlockSpec((nb, M_DIM), lambda h: (0, off + h))
    out_shape = (
        jax.ShapeDtypeStruct((nb, M_WIDTH), F32),
        jax.ShapeDtypeStruct((nb, M_WIDTH), F32),
        jax.ShapeDtypeStruct((M_HEADS, 1, nb), F32),
        jax.ShapeDtypeStruct((nb, M_HEADS, M_DIM, M_DIM), F32),
    )
    c_spec = pl.BlockSpec((nb, 1, M_DIM, M_DIM), lambda h: (0, h, 0, 0))
    return pl.pallas_call(
        _sample_mlstm_kernel,
        out_shape=out_shape,
        grid=(M_HEADS,),
        in_specs=[smem, smem, head(0), head(M_HEADS), head(2 * M_HEADS), head(0),
                  pl.BlockSpec(gates_t.shape, lambda h: (0, 0)), pl.BlockSpec(m0_t.shape, lambda h: (0, 0)),
                  head(0), c_spec],
        out_specs=(head(0), head(0), pl.BlockSpec((1, 1, nb), lambda h: (h, 0, 0)), c_spec),
        scratch_shapes=[pltpu.VMEM((nb, M_DIM), F32), pltpu.VMEM((nb, M_DIM), F32)],
        compiler_params=pltpu.CompilerParams(dimension_semantics=("arbitrary",),
                                             vmem_limit_bytes=VMEM_LIMIT_BYTES),
        name="sample_mlstm",
    )(b_i, b_f, qkv, qkv, qkv, og, gates_t, m0_t, n0, c0)


SAMPLE_TILE = 16
SAMPLE_UNROLL = 8


def _sample_swa_kernel(q2_ref, kc_ref, vc_ref, kn_ref, vn_ref, sink_ref, o_ref, ko_ref, vo_ref):
    sink = sink_ref[...] * LOG2E
    w = kc_ref.shape[2]
    tile = range(SAMPLE_TILE)
    newest = lax.broadcasted_iota(jnp.int32, (KV_WIDTH, w), 1) == w - 1
    pad = jnp.zeros((KV_WIDTH - SAMPLE_TILE, KV_WIDTH), F32)
    kn_t = jnp.concatenate([kn_ref[...], pad], axis=0).T
    vn_t = jnp.concatenate([vn_ref[...], pad], axis=0).T
    q2 = [q2_ref[b] for b in tile]
    k_new = [kn_ref[b:b + 1, :] for b in tile]
    v_new = [vn_ref[b:b + 1, :] for b in tile]
    s_c = [_dot(q2[b], kc_ref[b].astype(BF16)) for b in tile]
    s_n = [jnp.sum(q2[b].astype(F32) * k_new[b], axis=-1, keepdims=True) for b in tile]
    mx = [jnp.maximum(jnp.maximum(jnp.max(s_c[b], axis=-1, keepdims=True), s_n[b]), sink) for b in tile]
    p_c = [jnp.exp2(s_c[b] - mx[b]) for b in tile]
    p_n = [jnp.exp2(s_n[b] - mx[b]) for b in tile]
    denom = [jnp.sum(p_c[b], axis=-1, keepdims=True) + p_n[b] + jnp.exp2(sink - mx[b]) for b in tile]
    o = [_dot_nt(p_c[b].astype(BF16), vc_ref[b].astype(BF16)) for b in tile]
    for b in tile:
        o_ref[b] = (o[b] + p_n[b] * v_new[b]) / denom[b]
        ko_ref[b] = jnp.where(newest, kn_t[:, b:b + 1], pltpu.roll(kc_ref[b], w - 1, 1))
        vo_ref[b] = jnp.where(newest, vn_t[:, b:b + 1], pltpu.roll(vc_ref[b], w - 1, 1))


def _sample_swa(q2, k_cache, v_cache, k_new, v_new, sink_col):
    nb, _, w = k_cache.shape
    t3 = lambda a, c: pl.BlockSpec((SAMPLE_TILE, a, c), lambda i: (i, 0, 0))
    t2 = pl.BlockSpec((SAMPLE_TILE, KV_WIDTH), lambda i: (i, 0))
    out_shape = (
        jax.ShapeDtypeStruct((nb, A_HEADS, KV_WIDTH), F32),
        jax.ShapeDtypeStruct((nb, KV_WIDTH, w), F32),
        jax.ShapeDtypeStruct((nb, KV_WIDTH, w), F32),
    )
    return pl.pallas_call(
        _sample_swa_kernel,
        out_shape=out_shape,
        grid=(nb // SAMPLE_TILE,),
        in_specs=[t3(A_HEADS, KV_WIDTH), t3(KV_WIDTH, w), t3(KV_WIDTH, w), t2, t2,
                  pl.BlockSpec((A_HEADS, 1), lambda i: (0, 0))],
        out_specs=(t3(A_HEADS, KV_WIDTH), t3(KV_WIDTH, w), t3(KV_WIDTH, w)),
        compiler_params=pltpu.CompilerParams(dimension_semantics=("arbitrary",)),
        name="sample_swa",
    )(q2, k_cache, v_cache, k_new, v_new, sink_col)


def _rearranged_w_in(w_in):
    m_end = 4 * M_WIDTH
    g_end = m_end + 2 * M_HEADS
    gates = jnp.pad(w_in[:, m_end:g_end], ((0, 0), (0, GATE_PAD - 2 * M_HEADS)))
    w_in_r = jnp.concatenate([w_in[:, :m_end], w_in[:, g_end:], gates], axis=1).astype(BF16)
    assert w_in_r.shape[1] == W_IN_COLS
    return w_in_r


def kernel(x_prompt, x_sample, cache_swa_k, cache_swa_v, state_mlstm_C, state_mlstm_n, state_mlstm_m,
           ffn1_norm, ffn1_w_gate, ffn1_w_up, ffn1_w_down, mix_norm, w_in, mlstm_b_i, mlstm_b_f,
           mlstm_out_norm, swa_q_norm, swa_k_norm, swa_sinks, w_out, ffn2_norm, ffn2_w_gate,
           ffn2_w_up, ffn2_w_down):
    depth = ffn1_norm.shape[0]
    assert depth == 1
    batch, seq, _ = x_prompt.shape
    nb = x_sample.shape[0]
    assert x_sample.shape[1] == 1 and seq % BLOCK == 0

    wg1, wu1, wd1 = (w[0].astype(BF16) for w in (ffn1_w_gate, ffn1_w_up, ffn1_w_down))
    w_in_r = _rearranged_w_in(w_in[0])
    n1 = ffn1_norm[0].reshape(1, D_MODEL)
    n2 = mix_norm[0].reshape(1, D_MODEL)
    n3 = ffn2_norm[0].reshape(1, D_MODEL)
    qgain = jnp.tile(swa_q_norm[0], A_HEADS).reshape(1, A_WIDTH)
    kgain = jnp.tile(swa_k_norm[0], KV_HEADS).reshape(1, KV_WIDTH)
    ogain = mlstm_out_norm[0].reshape(1, M_WIDTH)
    b_i, b_f = mlstm_b_i[0], mlstm_b_f[0]
    bias8 = jnp.concatenate([b_i, b_f]).reshape(2 * M_HEADS, 1)
    front_params = (n1, wg1, wu1, wd1, n2, w_in_r, qgain, kgain, ogain, bias8)
    sinks = swa_sinks[0]

    xp = x_prompt.reshape(batch * seq, D_MODEL)
    xs = x_sample.reshape(nb, D_MODEL)
    back_weights_f32 = (w_out[0], ffn2_w_gate[0], ffn2_w_up[0], ffn2_w_down[0])
    prompt_front, sample_front, (wo, wg2, wu2, wd2) = _front(xp, xs, front_params, back_weights_f32,
                                                             tm=512, seq=seq)
    back_params = (wo, n3, wg2, wu2, wd2)
    x1, qkv, og, qa, ka, va, vt, kx, vx, rows, cols, pm = prompt_front
    x1s, qkvs, ogs, qas, kas, vas = sample_front[:6]
    gates_t = sample_front[-1]

    rows = rows.reshape(batch, seq // BLOCK, 3, 8, BLOCK)
    y, pc, pn = _prompt_mixer(sinks, qkv, vt, og, cols, rows, qa, kx, vx, batch, seq)
    pk = ka.reshape(batch, WINDOW, KV_HEADS, A_DIM)
    pv = va.reshape(batch, WINDOW, KV_HEADS, A_DIM)
    pn = pn[:, :, 0, :]
    pm = pm[:, 0:M_HEADS, 0]

    y_m, sn, mt, sc = _sample_mlstm(b_i, b_f, qkvs, ogs, gates_t, jnp.transpose(state_mlstm_m[0]),
                                    state_mlstm_n[0].reshape(nb, M_WIDTH), state_mlstm_C[0])
    sn = sn.reshape(nb, M_HEADS, M_DIM)
    sm = jnp.transpose(mt[:, 0, :])

    qa_h = qas.reshape(nb, A_HEADS, A_DIM)
    zeros = jnp.zeros_like(qa_h)
    in_lo = (jnp.arange(A_HEADS) // A_GROUP == 0)[None, :, None]
    q2 = jnp.concatenate([jnp.where(in_lo, qa_h, zeros), jnp.where(in_lo, zeros, qa_h)], axis=-1)
    wb = cache_swa_k.shape[2]
    to_feature_major = lambda c: jnp.transpose(c[0], (0, 2, 3, 1)).reshape(nb, KV_WIDTH, wb)
    to_window_major = lambda c: jnp.transpose(c.reshape(nb, KV_HEADS, A_DIM, wb), (0, 3, 1, 2))[None]
    kc, vc = to_feature_major(cache_swa_k), to_feature_major(cache_swa_v)
    o2, sk, sv = _sample_swa(q2, kc, vc, kas, vas, sinks.reshape(A_HEADS, 1))
    o2 = o2.reshape(nb, A_HEADS, KV_HEADS, A_DIM)
    y_a = jnp.where(in_lo, o2[:, :, 0, :], o2[:, :, 1, :]).reshape(nb, A_WIDTH)
    ys_in = jnp.concatenate([y_m, y_a], axis=-1).astype(BF16)

    yp, ys = _back(x1, y, x1s, ys_in, back_params, tm=1024)
    yp = yp.reshape(batch, seq, D_MODEL)
    ys = ys.reshape(nb, 1, D_MODEL)

    return (yp, ys, pk[None], pv[None], pc[None], pn[None], pm[None],
            to_window_major(sk), to_window_major(sv), sc[None], sn[None], sm[None])
```

```python
import functools

import jax
import jax.numpy as jnp
from jax import lax
from jax.experimental import pallas as pl
from jax.experimental.pallas import tpu as pltpu

F32 = jnp.float32
BF16 = jnp.bfloat16

D_MODEL = 1024
D_FF = 2816
FF_CHUNK = 256
N_FF_CHUNKS = D_FF // FF_CHUNK
M_HEADS = 4
M_DIM = 128
M_WIDTH = M_HEADS * M_DIM
A_HEADS = 8
A_DIM = 64
A_WIDTH = A_HEADS * A_DIM
KV_HEADS = 2
KV_WIDTH = KV_HEADS * A_DIM
A_GROUP = A_HEADS // KV_HEADS
WINDOW = 128
BLOCK = 128
GATE_PAD = 128
RMS_EPS = 1e-6
FFN_RES_WEIGHT = 0.5
NEG_INF = float("-inf")
LOG2E = 1.4426950408889634
VMEM_LIMIT_BYTES = 56 * 1024 * 1024


def _dot(a, b):
    return jnp.dot(a, b, preferred_element_type=F32)


def _dot_nt(a, b):
    return lax.dot_general(a, b, (((1,), (1,)), ((), ())), preferred_element_type=F32)


def _rms_rows(x, gain):
    ms = jnp.mean(x * x, axis=-1, keepdims=True)
    return x * lax.rsqrt(ms + RMS_EPS) * gain


def _log_sigmoid(x):
    return jnp.minimum(x, 0.0) - jnp.log1p(jnp.exp(-jnp.abs(x)))


def _half_tile_mean_sq(x):
    in_lo = lax.broadcasted_iota(jnp.int32, (x.shape[0], 128), 1) < A_DIM
    out = []
    for c in range(x.shape[1] // 128):
        sq = x[:, c * 128:(c + 1) * 128]
        sq = sq * sq
        s_lo = jnp.sum(jnp.where(in_lo, sq, 0.0), axis=-1, keepdims=True)
        s_hi = jnp.sum(jnp.where(in_lo, 0.0, sq), axis=-1, keepdims=True)
        out.append(jnp.where(in_lo, s_lo, s_hi) * (1.0 / A_DIM))
    return out[0] if len(out) == 1 else jnp.concatenate(out, axis=1)


def _ffn(h_ref, wg_ref, wu_ref, wd_ref, act_ref):
    for c in range(N_FF_CHUNKS):
        lo, hi = c * FF_CHUNK, (c + 1) * FF_CHUNK
        h = h_ref[...]
        g = _dot(h, wg_ref[:, lo:hi])
        u = _dot(h, wu_ref[:, lo:hi])
        act_ref[:, lo:hi] = (g * jax.nn.sigmoid(g) * u).astype(BF16)
    return _dot(act_ref[...], wd_ref[...])


N_FRONT_PARAMS = 11
N_FRONT_COMMON_OUTS = 9
N_PROMPT_GATE_OUTS = 3
WA_KVG = A_WIDTH
WA_ROWS = WA_KVG + 2 * KV_WIDTH + GATE_PAD


def _prompt_gates(gates_t, bias_ref, first_of_seq, m_ref, rows_ref, cols_ref, mfin_ref):
    sub = lax.broadcasted_iota(jnp.int32, (8, BLOCK), 0)
    lane = lax.broadcasted_iota(jnp.int32, (8, BLOCK), 1)
    is_head = sub < M_HEADS
    pad = jnp.zeros((BLOCK - 24, BLOCK), F32)
    scan_shifts = (1, 2, 4, 8, 16, 32, 64)
    m_prev = jnp.where(first_of_seq, 0.0, m_ref[...])[:, 0:1]
    for c in range(rows_ref.shape[0]):
        pre = gates_t[:, c * BLOCK:(c + 1) * BLOCK] + bias_ref[...]
        r = jnp.where(is_head, pre, _log_sigmoid(pre))
        cum = r
        for shift in scan_shifts:
            cum = cum + jnp.where(lane >= shift, pltpu.roll(cum, shift, 1), 0.0)
        bcum = pltpu.roll(cum, M_HEADS, 0)
        g = jnp.where(is_head, r - bcum, 0.0)
        bcum = jnp.where(is_head, bcum, 0.0)
        cm = g
        for shift in scan_shifts:
            cm = jnp.maximum(cm, jnp.where(lane >= shift, pltpu.roll(cm, shift, 1), NEG_INF))
        cm_last = jnp.max(cm, axis=-1, keepdims=True)
        b_last = jnp.sum(jnp.where(lane == BLOCK - 1, bcum, 0.0), axis=-1, keepdims=True)
        mx = jnp.maximum(m_prev, cm)
        mx_last = jnp.maximum(m_prev, cm_last)
        rows_ref[c, 0] = g * LOG2E
        rows_ref[c, 1] = jnp.exp(g - mx_last)
        rows_ref[c, 2] = jnp.broadcast_to(jnp.exp(m_prev - mx_last), (8, BLOCK))
        col_src = jnp.concatenate([mx * -LOG2E, jnp.exp(m_prev - mx), jnp.exp(-(bcum + mx)), pad], axis=0)
        cols_ref[c * BLOCK:(c + 1) * BLOCK, :] = col_src.T
        m_prev = b_last + mx_last
    m_full = jnp.broadcast_to(m_prev, (8, BLOCK))
    m_ref[...] = m_full
    mfin_ref[0] = m_full


def _front_tile(x_ref, params, outs, gate_sink, h_ref, act_ref):
    (n1_ref, wg_ref, wu_ref, wd_ref, n2_ref, wm_ref, wa_ref, qgain_ref, kgain_ref, ogain_ref, _) = params
    (x1_ref, qkv_ref, og_ref, qa_ref, ka_ref, va_ref, vt_ref, kx_ref, vx_ref) = outs
    x = x_ref[...]
    h_ref[...] = _rms_rows(x, n1_ref[...]).astype(BF16)
    x1 = x + FFN_RES_WEIGHT * _ffn(h_ref, wg_ref, wu_ref, wd_ref, act_ref)
    x1_ref[...] = x1
    h_ref[...] = _rms_rows(x1, n2_ref[...]).astype(BF16)
    h = h_ref[...]
    project = lambda w_ref, lo, hi: _dot_nt(h, w_ref[lo:hi, :])

    qa = project(wa_ref, 0, WA_KVG)
    kvg = project(wa_ref, WA_KVG, WA_ROWS)
    v_m = project(wm_ref, 2 * M_WIDTH, 3 * M_WIDTH)
    qkv_ref[:, 2 * M_WIDTH:3 * M_WIDTH] = v_m.astype(BF16)
    for c in range(vt_ref.shape[0]):
        for hd in range(M_HEADS):
            blk = v_m[c * BLOCK:(c + 1) * BLOCK, hd * M_DIM:(hd + 1) * M_DIM]
            vt_ref[c, hd] = blk.T.astype(BF16)
    q_scale = qgain_ref[...] * (A_DIM ** -0.5 * LOG2E)
    qa_ref[...] = (qa * lax.rsqrt(_half_tile_mean_sq(qa) + RMS_EPS) * q_scale).astype(BF16)
    ka = kvg[:, 0:KV_WIDTH]
    ka = ka * lax.rsqrt(_half_tile_mean_sq(ka) + RMS_EPS) * kgain_ref[...]
    va = kvg[:, KV_WIDTH:2 * KV_WIDTH]
    keep = ka_ref.shape[0]
    ka_ref[...] = ka[ka.shape[0] - keep:, :]
    va_ref[...] = va[va.shape[0] - keep:, :]
    gate_sink(kvg[:, 2 * KV_WIDTH:2 * KV_WIDTH + GATE_PAD].T[0:2 * M_HEADS, :])

    in_lo = lax.broadcasted_iota(jnp.int32, ka.shape, 1) < A_DIM
    for src, dst in ((ka, kx_ref), (va, vx_ref)):
        x0 = jnp.where(in_lo, src, 0.0)
        x1 = jnp.where(in_lo, 0.0, src)
        dst[:, 0:128] = x0.astype(BF16)
        dst[:, 128:256] = pltpu.roll(x0, A_DIM, 1).astype(BF16)
        dst[:, 256:384] = pltpu.roll(x1, A_DIM, 1).astype(BF16)
        dst[:, 384:512] = x1.astype(BF16)

    og_ref[...] = jax.nn.sigmoid(project(wm_ref, 3 * M_WIDTH, 4 * M_WIDTH)) * ogain_ref[...]
    k_m = project(wm_ref, M_WIDTH, 2 * M_WIDTH) * (M_DIM ** -0.5)
    qkv_ref[:, M_WIDTH:2 * M_WIDTH] = k_m.astype(BF16)
    qkv_ref[:, 0:M_WIDTH] = project(wm_ref, 0, M_WIDTH).astype(BF16)


def _front_kernel(*refs, n_tiles, tiles_per_seq, n_cast):
    refs = list(refs)
    take = lambda k: [refs.pop(0) for _ in range(k)]
    x_ref, xs_ref = take(2)
    params = take(N_FRONT_PARAMS)
    cast_src = take(n_cast)
    outs_p = take(N_FRONT_COMMON_OUTS)
    rows_ref, cols_ref, mfin_ref = take(N_PROMPT_GATE_OUTS)
    outs_s = take(N_FRONT_COMMON_OUTS)
    (gts_ref,) = take(1)
    cast_dst = take(n_cast)
    h_ref, act_ref, m_ref = refs
    bias_ref = params[-1]
    i = pl.program_id(0)

    @pl.when(i < n_tiles)
    def _():
        for src, dst in zip(cast_src, cast_dst):
            dst[...] = src[...].astype(BF16)
        gates = functools.partial(_prompt_gates, bias_ref=bias_ref, first_of_seq=i % tiles_per_seq == 0,
                                  m_ref=m_ref, rows_ref=rows_ref, cols_ref=cols_ref, mfin_ref=mfin_ref)
        _front_tile(x_ref, params, outs_p, gates, h_ref, act_ref)

    @pl.when(i == n_tiles)
    def _():
        ns = xs_ref.shape[0]

        def raw_gates(gates_t):
            gts_ref[...] = gates_t

        _front_tile(xs_ref, params, outs_s, raw_gates, h_ref.at[0:ns], act_ref.at[0:ns])


def _const_spec(shape):
    nd = len(shape)
    return pl.BlockSpec(shape, lambda i: (0,) * nd, pipeline_mode=pl.Buffered(1))


def _whole_spec(shape):
    nd = len(shape)
    return pl.BlockSpec(shape, lambda i: (0,) * nd)


def _front_out_shapes(n, n_cache_rows):
    return (
        jax.ShapeDtypeStruct((n, D_MODEL), F32),
        jax.ShapeDtypeStruct((n, 3 * M_WIDTH), BF16),
        jax.ShapeDtypeStruct((n, M_WIDTH), F32),
        jax.ShapeDtypeStruct((n, A_WIDTH), BF16),
        jax.ShapeDtypeStruct((n_cache_rows, KV_WIDTH), F32),
        jax.ShapeDtypeStruct((n_cache_rows, KV_WIDTH), F32),
        jax.ShapeDtypeStruct((n // BLOCK, M_HEADS, M_DIM, BLOCK), BF16),
        jax.ShapeDtypeStruct((n, 4 * KV_WIDTH), BF16),
        jax.ShapeDtypeStruct((n, 4 * KV_WIDTH), BF16),
    )


def _cast_chunk_count(n_rows, max_chunks):
    for k in range(max_chunks, 0, -1):
        if n_rows % k == 0 and (n_rows // k) % 16 == 0:
            return k
    raise ValueError(n_rows)


def _front(x2d, xs2d, params, later_weights, tm, seq):
    n, ns = x2d.shape[0], xs2d.shape[0]
    assert seq % tm == 0 and tm >= WINDOW and len(params) == N_FRONT_PARAMS
    n_tiles = n // tm
    n_seqs = n // seq
    tiles_per_seq = seq // tm
    nb_t = tm // BLOCK
    tile = lambda i: jnp.minimum(i, n_tiles - 1)
    seq_of = lambda i: tile(i) // tiles_per_seq
    row = lambda w: pl.BlockSpec((tm, w), lambda i: (tile(i), 0))
    tail = pl.BlockSpec((WINDOW, KV_WIDTH), lambda i: (seq_of(i), 0))
    prompt_specs = (row(D_MODEL), row(3 * M_WIDTH), row(M_WIDTH), row(A_WIDTH), tail, tail,
                    pl.BlockSpec((nb_t, M_HEADS, M_DIM, BLOCK), lambda i: (tile(i), 0, 0, 0)),
                    row(4 * KV_WIDTH), row(4 * KV_WIDTH),
                    pl.BlockSpec((nb_t, 3, 8, BLOCK), lambda i: (tile(i), 0, 0, 0)), row(128),
                    pl.BlockSpec((1, 8, BLOCK), lambda i: (seq_of(i), 0, 0)))
    prompt_shapes = _front_out_shapes(n, n_seqs * WINDOW) + (
        jax.ShapeDtypeStruct((n // BLOCK, 3, 8, BLOCK), F32),
        jax.ShapeDtypeStruct((n, 128), F32),
        jax.ShapeDtypeStruct((n_seqs, 8, BLOCK), F32),
    )
    sample_shapes = _front_out_shapes(ns, ns) + (
        jax.ShapeDtypeStruct((2 * M_HEADS, ns), F32),
    )

    def chunk_spec(w):
        k = _cast_chunk_count(w.shape[0], n_tiles)
        return pl.BlockSpec((w.shape[0] // k, w.shape[1]), lambda i: (jnp.minimum(i, k - 1), 0))

    cast_specs = [chunk_spec(w) for w in later_weights]
    cast_shapes = tuple(jax.ShapeDtypeStruct(w.shape, BF16) for w in later_weights)
    outs = pl.pallas_call(
        functools.partial(_front_kernel, n_tiles=n_tiles, tiles_per_seq=tiles_per_seq, n_cast=len(later_weights)),
        out_shape=prompt_shapes + sample_shapes + cast_shapes,
        grid=(n_tiles + 1,),
        in_specs=[row(D_MODEL), _whole_spec(xs2d.shape)] + [_const_spec(p.shape) for p in params] + cast_specs,
        out_specs=prompt_specs + tuple(_whole_spec(s.shape) for s in sample_shapes) + tuple(cast_specs),
        scratch_shapes=[pltpu.VMEM((tm, D_MODEL), BF16), pltpu.VMEM((tm, D_FF), BF16),
                        pltpu.VMEM((8, BLOCK), F32)],
        compiler_params=pltpu.CompilerParams(dimension_semantics=("arbitrary",),
                                             vmem_limit_bytes=VMEM_LIMIT_BYTES),
        name="front",
    )(x2d, xs2d, *params, *later_weights)
    n_p, n_s = len(prompt_shapes), len(sample_shapes)
    return outs[:n_p], outs[n_p:n_p + n_s], outs[n_p + n_s:]


def _back_tile(x1_ref, y_ref, params, out_ref, h_ref, act_ref):
    wo_ref, n_ref, wg_ref, wu_ref, wd_ref = params
    x2 = x1_ref[...] + _dot(y_ref[...], wo_ref[...])
    h_ref[...] = _rms_rows(x2, n_ref[...]).astype(BF16)
    out_ref[...] = x2 + FFN_RES_WEIGHT * _ffn(h_ref, wg_ref, wu_ref, wd_ref, act_ref)


def _back_kernel(x1_ref, y_ref, x1s_ref, ys_ref, wo_ref, n_ref, wg_ref, wu_ref, wd_ref, out_ref, outs_ref,
                 h_ref, act_ref, *, n_tiles):
    params = (wo_ref, n_ref, wg_ref, wu_ref, wd_ref)
    i = pl.program_id(0)

    @pl.when(i < n_tiles)
    def _():
        _back_tile(x1_ref, y_ref, params, out_ref, h_ref, act_ref)

    @pl.when(i == n_tiles)
    def _():
        ns = x1s_ref.shape[0]
        _back_tile(x1s_ref, ys_ref, params, outs_ref, h_ref.at[0:ns], act_ref.at[0:ns])


def _back(x1, y, x1s, ys, params, tm):
    n, ns = x1.shape[0], x1s.shape[0]
    n_tiles = n // tm
    row = pl.BlockSpec((tm, D_MODEL), lambda i: (jnp.minimum(i, n_tiles - 1), 0))
    return pl.pallas_call(
        functools.partial(_back_kernel, n_tiles=n_tiles),
        out_shape=(jax.ShapeDtypeStruct((n, D_MODEL), F32), jax.ShapeDtypeStruct((ns, D_MODEL), F32)),
        grid=(n_tiles + 1,),
        in_specs=[row, row, _whole_spec(x1s.shape), _whole_spec(ys.shape)] + [_const_spec(p.shape) for p in params],
        out_specs=(row, _whole_spec((ns, D_MODEL))),
        scratch_shapes=[pltpu.VMEM((tm, D_MODEL), BF16), pltpu.VMEM((tm, D_FF), BF16)],
        compiler_params=pltpu.CompilerParams(dimension_semantics=("arbitrary",),
                                             vmem_limit_bytes=VMEM_LIMIT_BYTES),
        name="back",
    )(x1, y, x1s, ys, *params)


def _prompt_mixer_kernel(sinks_ref, qkv_ref, vt_ref, og_ref, cols_ref, rows_ref, qa_ref,
                         kx_ref, vx_ref, kxp_ref, vxp_ref,
                         y_ref, c_out_ref, n_out_ref,
                         c_ref, n_ref):
    j = pl.program_id(0)
    batch = qkv_ref.shape[0]

    @pl.when(j == 0)
    def _():
        c_ref[...] = jnp.zeros_like(c_ref)
        n_ref[...] = jnp.zeros_like(n_ref)

    rows = lax.broadcasted_iota(jnp.int32, (BLOCK, BLOCK), 0)
    cols = lax.broadcasted_iota(jnp.int32, (BLOCK, BLOCK), 1)
    causal = cols <= rows
    lane_lo = lax.broadcasted_iota(jnp.int32, (2 * BLOCK, KV_WIDTH), 1) < A_DIM
    ones_m = jnp.ones((BLOCK, M_DIM), BF16)
    ones_half = (jnp.where(lane_lo, 1.0, 0.0).astype(BF16), jnp.where(lane_lo, 0.0, 1.0).astype(BF16))
    qi = lax.broadcasted_iota(jnp.int32, (2 * BLOCK, 2 * BLOCK), 0) % BLOCK
    kc = lax.broadcasted_iota(jnp.int32, (2 * BLOCK, 2 * BLOCK), 1)
    first_valid = jnp.where(j == 0, BLOCK, 0)
    valid = (kc >= qi) & (kc <= qi + WINDOW) & (kc >= first_valid)
    top_rows = lax.broadcasted_iota(jnp.int32, (2 * BLOCK, 1), 0) < BLOCK

    heads = range(M_HEADS)
    groups = [(kv, parity) for kv in range(KV_HEADS) for parity in range(2)]
    m_lo = lambda h: h * M_DIM

    def first_matmuls(b):
        st = {"col": cols_ref[b]}
        k = [qkv_ref[b, :, M_WIDTH + m_lo(h):M_WIDTH + m_lo(h) + M_DIM] for h in heads]
        st["c_prev"] = [c_ref[b, h] for h in heads]
        st["n_prev"] = [n_ref[b, h] for h in heads]
        st["qkc"], st["upd"], st["sc"] = [], [], []
        for h in heads:
            q = qkv_ref[b, :, m_lo(h):m_lo(h) + M_DIM]
            n_rep = jnp.broadcast_to(st["n_prev"][h][0:1, :], (BLOCK, M_DIM)).astype(BF16)
            rhs = jnp.concatenate([k[h], st["c_prev"][h].astype(BF16), n_rep], axis=0)
            st["qkc"].append(_dot_nt(q, rhs))
        for h in heads:
            wend_row = rows_ref[b, 0, 1][h:h + 1, :]
            vw_t = (vt_ref[b, 0, h].astype(F32) * wend_row).astype(BF16)
            w_rep = jnp.broadcast_to(wend_row, (16, BLOCK)).astype(BF16)
            st["upd"].append(_dot(jnp.concatenate([vw_t, w_rep], axis=0), k[h]))
        q2 = [jnp.concatenate([qa_ref[b, :, (2 * kv) * 128:(2 * kv + 1) * 128],
                               qa_ref[b, :, (2 * kv + 1) * 128:(2 * kv + 2) * 128]], axis=0)
              for kv in range(KV_HEADS)]
        for kv, parity in groups:
            var = (2 * kv + parity) * KV_WIDTH
            k_band = jnp.concatenate([kxp_ref[b, :, var:var + KV_WIDTH], kx_ref[b, :, var:var + KV_WIDTH]], axis=0)
            st["sc"].append(_dot_nt(q2[kv], k_band))
        return st

    def vector_work(b, st):
        st["s"], st["pr"], st["e_sink"] = [], [], []
        for h in heads:
            g_row = rows_ref[b, 0, 0][h:h + 1, :]
            decay = rows_ref[b, 0, 2][h:h + 1, :]
            d = jnp.where(causal, jnp.exp2(st["col"][:, h:h + 1] + g_row), 0.0)
            st["s"].append((st["qkc"][h][:, 0:BLOCK] * d).astype(BF16))
            c_ref[b, h] = decay * st["c_prev"][h] + st["upd"][h][0:M_DIM]
            n_ref[b, h] = decay * st["n_prev"][h] + st["upd"][h][M_DIM:M_DIM + 8]
        for gi, (kv, parity) in enumerate(groups):
            sink = jnp.where(top_rows, sinks_ref[4 * kv + parity], sinks_ref[4 * kv + 2 + parity]) * LOG2E
            scm = jnp.where(valid, st["sc"][gi], NEG_INF)
            mx = jnp.maximum(jnp.max(scm, axis=-1, keepdims=True), sink)
            st["pr"].append(jnp.exp2(scm - mx).astype(BF16))
            st["e_sink"].append(jnp.exp2(sink - mx))

    def second_matmuls(b, st):
        st["sv"], st["pv"] = [], []
        for h in heads:
            v = qkv_ref[b, :, 2 * M_WIDTH + m_lo(h):2 * M_WIDTH + m_lo(h) + M_DIM]
            st["sv"].append(_dot(st["s"][h], jnp.concatenate([v, ones_m], axis=1)))
        for gi, (kv, parity) in enumerate(groups):
            var = (2 * kv + parity) * KV_WIDTH
            v_band = jnp.concatenate([vxp_ref[b, :, var:var + KV_WIDTH], vx_ref[b, :, var:var + KV_WIDTH]], axis=0)
            st["pv"].append(_dot(st["pr"][gi], jnp.concatenate([v_band, ones_half[parity]], axis=1)))

    def normalise(b, st):
        col_b, qkc, sv, pv, e_sink = st["col"], st["qkc"], st["sv"], st["pv"], st["e_sink"]
        for h in heads:
            lo, hi = m_lo(h), m_lo(h) + M_DIM
            wi_col = col_b[:, 8 + h:9 + h]
            num = wi_col * qkc[h][:, BLOCK:2 * BLOCK] + sv[h][:, 0:M_DIM]
            den = wi_col * qkc[h][:, 2 * BLOCK:3 * BLOCK] + sv[h][:, M_DIM:2 * M_DIM]
            hh = num / jnp.maximum(jnp.abs(den), col_b[:, 16 + h:17 + h])
            hh = hh * lax.rsqrt(jnp.mean(hh * hh, axis=-1, keepdims=True) + RMS_EPS)
            y_ref[b, :, lo:hi] = (hh * og_ref[b, :, lo:hi]).astype(BF16)
        for kv in range(KV_HEADS):
            acc = pv[2 * kv] + pv[2 * kv + 1]
            denom = acc[:, KV_WIDTH:] + jnp.where(lane_lo, e_sink[2 * kv], e_sink[2 * kv + 1])
            ya = (acc[:, 0:KV_WIDTH] / denom).astype(BF16)
            p0, p1 = 2 * kv, 2 * kv + 1
            y_ref[b, :, M_WIDTH + p0 * 128:M_WIDTH + (p0 + 1) * 128] = ya[0:BLOCK]
            y_ref[b, :, M_WIDTH + p1 * 128:M_WIDTH + (p1 + 1) * 128] = ya[BLOCK:2 * BLOCK]

    for b in range(batch):
        st = first_matmuls(b)
        vector_work(b, st)
        second_matmuls(b, st)
        normalise(b, st)

    @pl.when(j == pl.num_programs(0) - 1)
    def _():
        c_out_ref[...] = c_ref[...]
        n_out_ref[...] = n_ref[...]


def _prompt_mixer(sinks, qkv, vt, og, cols, rows, qa, kx, vx, batch, seq):
    nblk = seq // BLOCK
    r3 = lambda a: a.reshape(batch, seq, a.shape[-1])
    cur = lambda w: pl.BlockSpec((batch, BLOCK, w), lambda j: (0, j, 0))
    prev = lambda w: pl.BlockSpec((batch, BLOCK, w), lambda j: (0, jnp.maximum(j - 1, 0), 0))
    state = lambda shape: pl.BlockSpec((batch,) + shape, lambda j: (0,) * (len(shape) + 1))
    out_shape = (
        jax.ShapeDtypeStruct((batch, seq, D_MODEL), BF16),
        jax.ShapeDtypeStruct((batch, M_HEADS, M_DIM, M_DIM), F32),
        jax.ShapeDtypeStruct((batch, M_HEADS, 8, M_DIM), F32),
    )
    y, pc, pn = pl.pallas_call(
        _prompt_mixer_kernel,
        out_shape=out_shape,
        grid=(nblk,),
        in_specs=[pl.BlockSpec(memory_space=pltpu.SMEM),
                  cur(3 * M_WIDTH),
                  pl.BlockSpec((batch, 1, M_HEADS, M_DIM, BLOCK), lambda j: (0, j, 0, 0, 0)),
                  cur(M_WIDTH), cur(128),
                  pl.BlockSpec((batch, 1, 3, 8, BLOCK), lambda j: (0, j, 0, 0, 0)),
                  cur(A_WIDTH), cur(4 * KV_WIDTH), cur(4 * KV_WIDTH), prev(4 * KV_WIDTH), prev(4 * KV_WIDTH)],
        out_specs=(cur(D_MODEL), state((M_HEADS, M_DIM, M_DIM)), state((M_HEADS, 8, M_DIM))),
        scratch_shapes=[pltpu.VMEM((batch, M_HEADS, M_DIM, M_DIM), F32),
                        pltpu.VMEM((batch, M_HEADS, 8, M_DIM), F32)],
        compiler_params=pltpu.CompilerParams(dimension_semantics=("arbitrary",),
                                             vmem_limit_bytes=VMEM_LIMIT_BYTES),
        name="prompt_mixer",
    )(sinks, r3(qkv), vt.reshape(batch, nblk, M_HEADS, M_DIM, BLOCK), r3(og), r3(cols), rows, r3(qa),
      r3(kx), r3(vx), r3(kx), r3(vx))
    return y.reshape(batch * seq, D_MODEL), pc, pn


def _sample_mlstm_kernel(bi_ref, bf_ref, q_ref, k_ref, v_ref, og_ref, gates_ref, m0_ref, n0_ref, c_ref,
                         y_ref, n_out_ref, m_out_ref, c_out_ref, decay_ref, qr_ref):
    h = pl.program_id(0)
    nb = q_ref.shape[0]
    q_rows = q_ref[...].astype(F32)
    k_rows = k_ref[...].astype(F32)
    qr_ref[...] = q_rows
    qt, kt, vt = q_rows.T, k_rows.T, v_ref[...].astype(F32).T
    i_pre = gates_ref[pl.ds(h, 1), :] + bi_ref[h]
    a = _log_sigmoid(gates_ref[pl.ds(M_HEADS + h, 1), :] + bf_ref[h]) + m0_ref[pl.ds(h, 1), :]
    m_t = jnp.maximum(a, i_pre)
    w_inter = jnp.exp(a - m_t)
    w_in = jnp.exp(i_pre - m_t)
    scores = jnp.sum(qt * kt, axis=0, keepdims=True) * w_in
    n0t = n0_ref[...].T
    nq = jnp.sum(n0t * qt, axis=0, keepdims=True)

    rows = lax.broadcasted_iota(jnp.int32, (nb, M_DIM), 0)
    cols = lax.broadcasted_iota(jnp.int32, (M_DIM, nb), 1)
    vw_t = (vt * w_in).astype(BF16)

    decay_ref[...] = jnp.broadcast_to(w_inter, (M_DIM, nb)).T

    def body(grp, cq_t):
        base = grp * SAMPLE_UNROLL
        for u in range(SAMPLE_UNROLL):
            b = base + u
            col = jnp.sum(c_ref[b, 0] * qr_ref[pl.ds(b, 1), :], axis=-1, keepdims=True)
            cq_t = jnp.where(cols == b, col, cq_t)
        outer = []
        for u in range(SAMPLE_UNROLL):
            k_only_b = jnp.where(rows == base + u, k_rows, 0.0).astype(BF16)
            outer.append(_dot(vw_t, k_only_b))
        for u in range(SAMPLE_UNROLL):
            b = base + u
            c_out_ref[b, 0] = decay_ref[pl.ds(b, 1), :] * c_ref[b, 0] + outer[u]
        return cq_t

    cq_t = lax.fori_loop(0, nb // SAMPLE_UNROLL, body, jnp.zeros((M_DIM, nb), F32))

    num = w_inter * cq_t + scores * vt
    den = w_inter * nq + scores
    hh = num / jnp.maximum(jnp.abs(den), jnp.exp(-m_t))
    hh = hh * lax.rsqrt(jnp.mean(hh * hh, axis=0, keepdims=True) + RMS_EPS)
    y_ref[...] = hh.T * og_ref[...]
    n_out_ref[...] = (w_inter * n0t + w_in * kt).T
    m_out_ref[0] = m_t


def _sample_mlstm(b_i, b_f, qkv, og, gates_t, m0_t, n0, c0):
    nb = qkv.shape[0]
    smem = pl.BlockSpec(memory_space=pltpu.SMEM)
    head = lambda off: pl.BlockSpec((nb, M_DIM), lambda h: (0, off + h))
    out_shape = (
        jax.ShapeDtypeStruct((nb, M_WIDTH), F32),
        jax.ShapeDtypeStruct((nb, M_WIDTH), F32),
        jax.ShapeDtypeStruct((M_HEADS, 1, nb), F32),
        jax.ShapeDtypeStruct((nb, M_HEADS, M_DIM, M_DIM), F32),
    )
    c_spec = pl.BlockSpec((nb, 1, M_DIM, M_DIM), lambda h: (0, h, 0, 0))
    return pl.pallas_call(
        _sample_mlstm_kernel,
        out_shape=out_shape,
        grid=(M_HEADS,),
        in_specs=[smem, smem, head(0), head(M_HEADS), head(2 * M_HEADS), head(0),
                  pl.BlockSpec(gates_t.shape, lambda h: (0, 0)), pl.BlockSpec(m0_t.shape, lambda h: (0, 0)),
                  head(0), c_spec],
        out_specs=(head(0), head(0), pl.BlockSpec((1, 1, nb), lambda h: (h, 0, 0)), c_spec),
        scratch_shapes=[pltpu.VMEM((nb, M_DIM), F32), pltpu.VMEM((nb, M_DIM), F32)],
        compiler_params=pltpu.CompilerParams(dimension_semantics=("arbitrary",),
                                             vmem_limit_bytes=VMEM_LIMIT_BYTES),
        name="sample_mlstm",
    )(b_i, b_f, qkv, qkv, qkv, og, gates_t, m0_t, n0, c0)


SAMPLE_TILE = 16
SAMPLE_UNROLL = 8


def _sample_swa_kernel(q2_ref, kc_ref, vc_ref, kn_ref, vn_ref, sink_ref, o_ref, ko_ref, vo_ref):
    sink = sink_ref[...] * LOG2E
    w = kc_ref.shape[2]
    tile = range(SAMPLE_TILE)
    newest = lax.broadcasted_iota(jnp.int32, (KV_WIDTH, w), 1) == w - 1
    pad = jnp.zeros((KV_WIDTH - SAMPLE_TILE, KV_WIDTH), F32)
    kn_t = jnp.concatenate([kn_ref[...], pad], axis=0).T
    vn_t = jnp.concatenate([vn_ref[...], pad], axis=0).T
    q2 = [q2_ref[b] for b in tile]
    k_new = [kn_ref[b:b + 1, :] for b in tile]
    v_new = [vn_ref[b:b + 1, :] for b in tile]
    s_c = [_dot(q2[b], kc_ref[b].astype(BF16)) for b in tile]
    s_n = [jnp.sum(q2[b].astype(F32) * k_new[b], axis=-1, keepdims=True) for b in tile]
    mx = [jnp.maximum(jnp.maximum(jnp.max(s_c[b], axis=-1, keepdims=True), s_n[b]), sink) for b in tile]
    p_c = [jnp.exp2(s_c[b] - mx[b]) for b in tile]
    p_n = [jnp.exp2(s_n[b] - mx[b]) for b in tile]
    denom = [jnp.sum(p_c[b], axis=-1, keepdims=True) + p_n[b] + jnp.exp2(sink - mx[b]) for b in tile]
    o = [_dot_nt(p_c[b].astype(BF16), vc_ref[b].astype(BF16)) for b in tile]
    for b in tile:
        o_ref[b] = (o[b] + p_n[b] * v_new[b]) / denom[b]
        ko_ref[b] = jnp.where(newest, kn_t[:, b:b + 1], pltpu.roll(kc_ref[b], w - 1, 1))
        vo_ref[b] = jnp.where(newest, vn_t[:, b:b + 1], pltpu.roll(vc_ref[b], w - 1, 1))


def _sample_swa(q2, k_cache, v_cache, k_new, v_new, sink_col):
    nb, _, w = k_cache.shape
    t3 = lambda a, c: pl.BlockSpec((SAMPLE_TILE, a, c), lambda i: (i, 0, 0))
    t2 = pl.BlockSpec((SAMPLE_TILE, KV_WIDTH), lambda i: (i, 0))
    out_shape = (
        jax.ShapeDtypeStruct((nb, A_HEADS, KV_WIDTH), F32),
        jax.ShapeDtypeStruct((nb, KV_WIDTH, w), F32),
        jax.ShapeDtypeStruct((nb, KV_WIDTH, w), F32),
    )
    return pl.pallas_call(
        _sample_swa_kernel,
        out_shape=out_shape,
        grid=(nb // SAMPLE_TILE,),
        in_specs=[t3(A_HEADS, KV_WIDTH), t3(KV_WIDTH, w), t3(KV_WIDTH, w), t2, t2,
                  pl.BlockSpec((A_HEADS, 1), lambda i: (0, 0))],
        out_specs=(t3(A_HEADS, KV_WIDTH), t3(KV_WIDTH, w), t3(KV_WIDTH, w)),
        compiler_params=pltpu.CompilerParams(dimension_semantics=("arbitrary",)),
        name="sample_swa",
    )(q2, k_cache, v_cache, k_new, v_new, sink_col)


def _split_w_in(w_in):
    w_t = jnp.transpose(w_in).astype(BF16)
    m_end = 4 * M_WIDTH
    g_end = m_end + 2 * M_HEADS
    gates = jnp.pad(w_t[m_end:g_end], ((0, GATE_PAD - 2 * M_HEADS), (0, 0)))
    w_a = jnp.concatenate([w_t[g_end:], gates], axis=0)
    assert w_a.shape[0] == WA_ROWS
    return w_t, w_a


def kernel(x_prompt, x_sample, cache_swa_k, cache_swa_v, state_mlstm_C, state_mlstm_n, state_mlstm_m,
           ffn1_norm, ffn1_w_gate, ffn1_w_up, ffn1_w_down, mix_norm, w_in, mlstm_b_i, mlstm_b_f,
           mlstm_out_norm, swa_q_norm, swa_k_norm, swa_sinks, w_out, ffn2_norm, ffn2_w_gate,
           ffn2_w_up, ffn2_w_down):
    depth = ffn1_norm.shape[0]
    assert depth == 1
    batch, seq, _ = x_prompt.shape
    nb = x_sample.shape[0]
    assert x_sample.shape[1] == 1 and seq % BLOCK == 0

    wg1, wu1, wd1 = (w[0].astype(BF16) for w in (ffn1_w_gate, ffn1_w_up, ffn1_w_down))
    w_m, w_a = _split_w_in(w_in[0])
    n1 = ffn1_norm[0].reshape(1, D_MODEL)
    n2 = mix_norm[0].reshape(1, D_MODEL)
    n3 = ffn2_norm[0].reshape(1, D_MODEL)
    qgain = jnp.tile(swa_q_norm[0], A_HEADS).reshape(1, A_WIDTH)
    kgain = jnp.tile(swa_k_norm[0], KV_HEADS).reshape(1, KV_WIDTH)
    ogain = mlstm_out_norm[0].reshape(1, M_WIDTH)
    b_i, b_f = mlstm_b_i[0], mlstm_b_f[0]
    bias8 = jnp.concatenate([b_i, b_f]).reshape(2 * M_HEADS, 1)
    front_params = (n1, wg1, wu1, wd1, n2, w_m, w_a, qgain, kgain, ogain, bias8)
    sinks = swa_sinks[0]

    xp = x_prompt.reshape(batch * seq, D_MODEL)
    xs = x_sample.reshape(nb, D_MODEL)
    back_weights_f32 = (w_out[0], ffn2_w_gate[0], ffn2_w_up[0], ffn2_w_down[0])
    prompt_front, sample_front, (wo, wg2, wu2, wd2) = _front(xp, xs, front_params, back_weights_f32,
                                                             tm=512, seq=seq)
    back_params = (wo, n3, wg2, wu2, wd2)
    x1, qkv, og, qa, ka, va, vt, kx, vx, rows, cols, pm = prompt_front
    x1s, qkvs, ogs, qas, kas, vas = sample_front[:6]
    gates_t = sample_front[-1]

    rows = rows.reshape(batch, seq // BLOCK, 3, 8, BLOCK)
    y, pc, pn = _prompt_mixer(sinks, qkv, vt, og, cols, rows, qa, kx, vx, batch, seq)
    pk = ka.reshape(batch, WINDOW, KV_HEADS, A_DIM)
    pv = va.reshape(batch, WINDOW, KV_HEADS, A_DIM)
    pn = pn[:, :, 0, :]
    pm = pm[:, 0:M_HEADS, 0]

    y_m, sn, mt, sc = _sample_mlstm(b_i, b_f, qkvs, ogs, gates_t, jnp.transpose(state_mlstm_m[0]),
                                    state_mlstm_n[0].reshape(nb, M_WIDTH), state_mlstm_C[0])
    sn = sn.reshape(nb, M_HEADS, M_DIM)
    sm = jnp.transpose(mt[:, 0, :])

    qa_h = qas.reshape(nb, A_HEADS, A_DIM)
    zeros = jnp.zeros_like(qa_h)
    in_lo = (jnp.arange(A_HEADS) // A_GROUP == 0)[None, :, None]
    q2 = jnp.concatenate([jnp.where(in_lo, qa_h, zeros), jnp.where(in_lo, zeros, qa_h)], axis=-1)
    wb = cache_swa_k.shape[2]
    to_feature_major = lambda c: jnp.transpose(c[0], (0, 2, 3, 1)).reshape(nb, KV_WIDTH, wb)
    to_window_major = lambda c: jnp.transpose(c.reshape(nb, KV_HEADS, A_DIM, wb), (0, 3, 1, 2))[None]
    kc, vc = to_feature_major(cache_swa_k), to_feature_major(cache_swa_v)
    o2, sk, sv = _sample_swa(q2, kc, vc, kas, vas, sinks.reshape(A_HEADS, 1))
    o2 = o2.reshape(nb, A_HEADS, KV_HEADS, A_DIM)
    y_a = jnp.where(in_lo, o2[:, :, 0, :], o2[:, :, 1, :]).reshape(nb, A_WIDTH)
    ys_in = jnp.concatenate([y_m, y_a], axis=-1).astype(BF16)

    yp, ys = _back(x1, y, x1s, ys_in, back_params, tm=1024)
    yp = yp.reshape(batch, seq, D_MODEL)
    ys = ys.reshape(nb, 1, D_MODEL)

    return (yp, ys, pk[None], pv[None], pc[None], pn[None], pm[None],
            to_window_major(sk), to_window_major(sv), sc[None], sn[None], sm[None])
```

```python
import functools
from typing import NamedTuple

import jax
import jax.numpy as jnp
from jax import lax
from jax.experimental import pallas as pl
from jax.experimental.pallas import tpu as pltpu

F32 = jnp.float32
BF16 = jnp.bfloat16

D_MODEL = 1024
D_FF = 2816
FF_CHUNK = 256
N_FF_CHUNKS = D_FF // FF_CHUNK
M_HEADS = 4
M_DIM = 128
M_WIDTH = M_HEADS * M_DIM
A_HEADS = 8
A_DIM = 64
A_WIDTH = A_HEADS * A_DIM
KV_HEADS = 2
KV_WIDTH = KV_HEADS * A_DIM
A_GROUP = A_HEADS // KV_HEADS
WINDOW = 128
BLOCK = 128
GATE_PAD = 128
RMS_EPS = 1e-6
FFN_RES_WEIGHT = 0.5
NEG_INF = float("-inf")
LOG2E = 1.4426950408889634
VMEM_LIMIT_BYTES = 60 * 1024 * 1024


def _dot(a, b):
    return jnp.dot(a, b, preferred_element_type=F32)


def _dot_nt(a, b):
    return lax.dot_general(a, b, (((1,), (1,)), ((), ())), preferred_element_type=F32)


def _rms_rows(x, gain):
    ms = jnp.mean(x * x, axis=-1, keepdims=True)
    return x * lax.rsqrt(ms + RMS_EPS) * gain


def _log_sigmoid(x):
    return jnp.minimum(x, 0.0) - jnp.log1p(jnp.exp(-jnp.abs(x)))


def _half_tile_mean_sq(x):
    in_lo = lax.broadcasted_iota(jnp.int32, (x.shape[0], 128), 1) < A_DIM
    out = []
    for c in range(x.shape[1] // 128):
        sq = x[:, c * 128:(c + 1) * 128]
        sq = sq * sq
        s_lo = jnp.sum(jnp.where(in_lo, sq, 0.0), axis=-1, keepdims=True)
        s_hi = jnp.sum(jnp.where(in_lo, 0.0, sq), axis=-1, keepdims=True)
        out.append(jnp.where(in_lo, s_lo, s_hi) * (1.0 / A_DIM))
    return out[0] if len(out) == 1 else jnp.concatenate(out, axis=1)


def _ffn(h_ref, wg_ref, wu_ref, wd_ref, act_ref):
    for c in range(N_FF_CHUNKS):
        lo, hi = c * FF_CHUNK, (c + 1) * FF_CHUNK
        h = h_ref[...]
        g = _dot(h, wg_ref[:, lo:hi])
        u = _dot(h, wu_ref[:, lo:hi])
        act_ref[:, lo:hi] = (g * jax.nn.sigmoid(g) * u).astype(BF16)
    return _dot(act_ref[...], wd_ref[...])


N_FRONT_VECTORS = 6
N_FRONT_WEIGHTS = 4
N_FRONT_COMMON_OUTS = 9
N_PROMPT_GATE_OUTS = 3
W_M_ROWS = 4 * M_WIDTH
W_GATE_ROWS = 2 * M_HEADS
WA_KVG = A_WIDTH
WA_GATES = WA_KVG + 2 * KV_WIDTH
WA_ROWS = WA_GATES + GATE_PAD
CAST_ROWS, CAST_COLS = 256, 1024
CAST_SLOTS = 4


class _CastJob(NamedTuple):
    src: object
    src_row: int
    dst: object
    dst_row: int
    col: int
    rows: int
    cols: int
    dst_rows: int


def _cast_jobs(src, dst, src_row=0, dst_row=0, rows=None, dst_rows=None):
    rows = src.shape[0] - src_row if rows is None else rows
    jobs = []
    for r in range(0, rows, CAST_ROWS):
        nr = min(CAST_ROWS, rows - r)
        for c in range(0, src.shape[1], CAST_COLS):
            nc = min(CAST_COLS, src.shape[1] - c)
            jobs.append(_CastJob(src, src_row + r, dst, dst_row + r, c, nr, nc, nr if dst_rows is None else dst_rows))
    return jobs


def _run_cast_jobs(jobs, stage_ref, sem_ref):
    def copy(j):
        job, slot = jobs[j], j % CAST_SLOTS
        return pltpu.make_async_copy(job.src.at[pl.ds(job.src_row, job.rows), pl.ds(job.col, job.cols)],
                                     stage_ref.at[slot, pl.ds(0, job.rows), pl.ds(0, job.cols)], sem_ref.at[slot])

    ahead = CAST_SLOTS - 1
    for j in range(min(ahead, len(jobs))):
        copy(j).start()
    for j, job in enumerate(jobs):
        if j + ahead < len(jobs):
            copy(j + ahead).start()
        copy(j).wait()
        block = stage_ref[j % CAST_SLOTS, 0:job.rows, 0:job.cols]
        if job.dst_rows > job.rows:
            block = jnp.concatenate([block, jnp.zeros((job.dst_rows - job.rows, job.cols), F32)], axis=0)
        job.dst[job.dst_row:job.dst_row + job.dst_rows, job.col:job.col + job.cols] = block.astype(BF16)


def _prompt_gates(gates_t, bias_ref, first_of_seq, m_ref, rows_ref, cols_ref, mfin_ref):
    sub = lax.broadcasted_iota(jnp.int32, (8, BLOCK), 0)
    lane = lax.broadcasted_iota(jnp.int32, (8, BLOCK), 1)
    is_head = sub < M_HEADS
    pad = jnp.zeros((BLOCK - 24, BLOCK), F32)
    scan_shifts = (1, 2, 4, 8, 16, 32, 64)
    m_prev = jnp.where(first_of_seq, 0.0, m_ref[...])[:, 0:1]
    for c in range(rows_ref.shape[0]):
        pre = gates_t[:, c * BLOCK:(c + 1) * BLOCK] + bias_ref[...]
        r = jnp.where(is_head, pre, _log_sigmoid(pre))
        cum = r
        for shift in scan_shifts:
            cum = cum + jnp.where(lane >= shift, pltpu.roll(cum, shift, 1), 0.0)
        bcum = pltpu.roll(cum, M_HEADS, 0)
        g = jnp.where(is_head, r - bcum, 0.0)
        bcum = jnp.where(is_head, bcum, 0.0)
        cm = g
        for shift in scan_shifts:
            cm = jnp.maximum(cm, jnp.where(lane >= shift, pltpu.roll(cm, shift, 1), NEG_INF))
        cm_last = jnp.max(cm, axis=-1, keepdims=True)
        b_last = jnp.sum(jnp.where(lane == BLOCK - 1, bcum, 0.0), axis=-1, keepdims=True)
        mx = jnp.maximum(m_prev, cm)
        mx_last = jnp.maximum(m_prev, cm_last)
        rows_ref[c, 0] = g * LOG2E
        rows_ref[c, 1] = jnp.exp(g - mx_last)
        rows_ref[c, 2] = jnp.broadcast_to(jnp.exp(m_prev - mx_last), (8, BLOCK))
        col_src = jnp.concatenate([mx * -LOG2E, jnp.exp(m_prev - mx), jnp.exp(-(bcum + mx)), pad], axis=0)
        cols_ref[c * BLOCK:(c + 1) * BLOCK, :] = col_src.T
        m_prev = b_last + mx_last
    m_full = jnp.broadcast_to(m_prev, (8, BLOCK))
    m_ref[...] = m_full
    mfin_ref[0] = m_full


def _front_tile(x_ref, params, outs, gate_sink, h_ref, act_ref):
    (n1_ref, wg_ref, wu_ref, wd_ref, n2_ref, wm_ref, wa_ref, qgain_ref, kgain_ref, ogain_ref, _) = params
    (x1_ref, qkv_ref, og_ref, qa_ref, ka_ref, va_ref, vt_ref, kx_ref, vx_ref) = outs
    x = x_ref[...]
    h_ref[...] = _rms_rows(x, n1_ref[...]).astype(BF16)
    x1 = x + FFN_RES_WEIGHT * _ffn(h_ref, wg_ref, wu_ref, wd_ref, act_ref)
    x1_ref[...] = x1
    h_ref[...] = _rms_rows(x1, n2_ref[...]).astype(BF16)
    h = h_ref[...]
    project = lambda w_ref, lo, hi: _dot_nt(h, w_ref[lo:hi, :])

    qa = project(wa_ref, 0, WA_KVG)
    kvg = project(wa_ref, WA_KVG, WA_ROWS)
    v_m = project(wm_ref, 2 * M_WIDTH, 3 * M_WIDTH)
    qkv_ref[:, 2 * M_WIDTH:3 * M_WIDTH] = v_m.astype(BF16)
    for c in range(vt_ref.shape[0]):
        for hd in range(M_HEADS):
            blk = v_m[c * BLOCK:(c + 1) * BLOCK, hd * M_DIM:(hd + 1) * M_DIM]
            vt_ref[c, hd] = blk.T.astype(BF16)
    q_scale = qgain_ref[...] * (A_DIM ** -0.5 * LOG2E)
    qa_ref[...] = (qa * lax.rsqrt(_half_tile_mean_sq(qa) + RMS_EPS) * q_scale).astype(BF16)
    ka = kvg[:, 0:KV_WIDTH]
    ka = ka * lax.rsqrt(_half_tile_mean_sq(ka) + RMS_EPS) * kgain_ref[...]
    va = kvg[:, KV_WIDTH:2 * KV_WIDTH]
    keep = ka_ref.shape[0]
    ka_ref[...] = ka[ka.shape[0] - keep:, :]
    va_ref[...] = va[va.shape[0] - keep:, :]
    gate_sink(kvg[:, 2 * KV_WIDTH:2 * KV_WIDTH + GATE_PAD].T[0:2 * M_HEADS, :])

    in_lo = lax.broadcasted_iota(jnp.int32, ka.shape, 1) < A_DIM
    for src, dst in ((ka, kx_ref), (va, vx_ref)):
        x0 = jnp.where(in_lo, src, 0.0)
        x1 = jnp.where(in_lo, 0.0, src)
        dst[:, 0:128] = x0.astype(BF16)
        dst[:, 128:256] = pltpu.roll(x0, A_DIM, 1).astype(BF16)
        dst[:, 256:384] = pltpu.roll(x1, A_DIM, 1).astype(BF16)
        dst[:, 384:512] = x1.astype(BF16)

    og_ref[...] = jax.nn.sigmoid(project(wm_ref, 3 * M_WIDTH, 4 * M_WIDTH)) * ogain_ref[...]
    k_m = project(wm_ref, M_WIDTH, 2 * M_WIDTH) * (M_DIM ** -0.5)
    qkv_ref[:, M_WIDTH:2 * M_WIDTH] = k_m.astype(BF16)
    qkv_ref[:, 0:M_WIDTH] = project(wm_ref, 0, M_WIDTH).astype(BF16)


def _front_kernel(*refs, n_tiles, tiles_per_seq, n_cast):
    refs = list(refs)
    take = lambda k: [refs.pop(0) for _ in range(k)]
    x_ref, xs_ref = take(2)
    n1_ref, n2_ref, qgain_ref, kgain_ref, ogain_ref, bias_ref = take(N_FRONT_VECTORS)
    wg_hbm, wu_hbm, wd_hbm, wt_hbm = take(N_FRONT_WEIGHTS)
    cast_src = take(n_cast)
    outs_p = take(N_FRONT_COMMON_OUTS)
    rows_ref, cols_ref, mfin_ref = take(N_PROMPT_GATE_OUTS)
    outs_s = take(N_FRONT_COMMON_OUTS)
    (gts_ref,) = take(1)
    cast_dst = take(n_cast)
    h_ref, act_ref, m_ref, wg_ref, wu_ref, wd_ref, wm_ref, wa_ref, stage_ref, sem_ref = refs
    params = (n1_ref, wg_ref, wu_ref, wd_ref, n2_ref, wm_ref, wa_ref, qgain_ref, kgain_ref, ogain_ref, bias_ref)
    i = pl.program_id(0)

    @pl.when(i == 0)
    def _():
        a_src = W_M_ROWS + W_GATE_ROWS
        jobs = (_cast_jobs(wg_hbm, wg_ref) + _cast_jobs(wu_hbm, wu_ref) + _cast_jobs(wd_hbm, wd_ref)
                + _cast_jobs(wt_hbm, wm_ref, rows=W_M_ROWS)
                + _cast_jobs(wt_hbm, wa_ref, src_row=a_src, rows=WA_GATES)
                + _cast_jobs(wt_hbm, wa_ref, src_row=W_M_ROWS, dst_row=WA_GATES, rows=W_GATE_ROWS,
                             dst_rows=GATE_PAD))
        _run_cast_jobs(jobs, stage_ref, sem_ref)

    @pl.when(i < n_tiles)
    def _():
        for src, dst in zip(cast_src, cast_dst):
            dst[...] = src[...].astype(BF16)
        gates = functools.partial(_prompt_gates, bias_ref=bias_ref, first_of_seq=i % tiles_per_seq == 0,
                                  m_ref=m_ref, rows_ref=rows_ref, cols_ref=cols_ref, mfin_ref=mfin_ref)
        _front_tile(x_ref, params, outs_p, gates, h_ref, act_ref)

    @pl.when(i == n_tiles)
    def _():
        ns = xs_ref.shape[0]

        def raw_gates(gates_t):
            gts_ref[...] = gates_t

        _front_tile(xs_ref, params, outs_s, raw_gates, h_ref.at[0:ns], act_ref.at[0:ns])


def _const_spec(shape):
    nd = len(shape)
    return pl.BlockSpec(shape, lambda i: (0,) * nd, pipeline_mode=pl.Buffered(1))


def _whole_spec(shape):
    nd = len(shape)
    return pl.BlockSpec(shape, lambda i: (0,) * nd)


def _front_out_shapes(n, n_cache_rows):
    return (
        jax.ShapeDtypeStruct((n, D_MODEL), F32),
        jax.ShapeDtypeStruct((n, 3 * M_WIDTH), BF16),
        jax.ShapeDtypeStruct((n, M_WIDTH), F32),
        jax.ShapeDtypeStruct((n, A_WIDTH), BF16),
        jax.ShapeDtypeStruct((n_cache_rows, KV_WIDTH), F32),
        jax.ShapeDtypeStruct((n_cache_rows, KV_WIDTH), F32),
        jax.ShapeDtypeStruct((n // BLOCK, M_HEADS, M_DIM, BLOCK), BF16),
        jax.ShapeDtypeStruct((n, 4 * KV_WIDTH), BF16),
        jax.ShapeDtypeStruct((n, 4 * KV_WIDTH), BF16),
    )


def _cast_chunk_count(n_rows, max_chunks):
    for k in range(max_chunks, 0, -1):
        if n_rows % k == 0 and (n_rows // k) % 16 == 0:
            return k
    raise ValueError(n_rows)


def _front(x2d, xs2d, vectors, weights, later_weights, tm, seq):
    n, ns = x2d.shape[0], xs2d.shape[0]
    assert seq % tm == 0 and tm >= WINDOW
    assert len(vectors) == N_FRONT_VECTORS and len(weights) == N_FRONT_WEIGHTS
    assert weights[3].shape == (W_M_ROWS + W_GATE_ROWS + WA_GATES, D_MODEL)
    n_tiles = n // tm
    n_seqs = n // seq
    tiles_per_seq = seq // tm
    nb_t = tm // BLOCK
    tile = lambda i: jnp.minimum(i, n_tiles - 1)
    seq_of = lambda i: tile(i) // tiles_per_seq
    row = lambda w: pl.BlockSpec((tm, w), lambda i: (tile(i), 0))
    tail = pl.BlockSpec((WINDOW, KV_WIDTH), lambda i: (seq_of(i), 0))
    prompt_specs = (row(D_MODEL), row(3 * M_WIDTH), row(M_WIDTH), row(A_WIDTH), tail, tail,
                    pl.BlockSpec((nb_t, M_HEADS, M_DIM, BLOCK), lambda i: (tile(i), 0, 0, 0)),
                    row(4 * KV_WIDTH), row(4 * KV_WIDTH),
                    pl.BlockSpec((nb_t, 3, 8, BLOCK), lambda i: (tile(i), 0, 0, 0)), row(128),
                    pl.BlockSpec((1, 8, BLOCK), lambda i: (seq_of(i), 0, 0)))
    prompt_shapes = _front_out_shapes(n, n_seqs * WINDOW) + (
        jax.ShapeDtypeStruct((n // BLOCK, 3, 8, BLOCK), F32),
        jax.ShapeDtypeStruct((n, 128), F32),
        jax.ShapeDtypeStruct((n_seqs, 8, BLOCK), F32),
    )
    sample_shapes = _front_out_shapes(ns, ns) + (
        jax.ShapeDtypeStruct((2 * M_HEADS, ns), F32),
    )

    def chunk_spec(w):
        k = _cast_chunk_count(w.shape[0], n_tiles)
        return pl.BlockSpec((w.shape[0] // k, w.shape[1]), lambda i: (jnp.minimum(i, k - 1), 0))

    cast_specs = [chunk_spec(w) for w in later_weights]
    cast_shapes = tuple(jax.ShapeDtypeStruct(w.shape, BF16) for w in later_weights)
    outs = pl.pallas_call(
        functools.partial(_front_kernel, n_tiles=n_tiles, tiles_per_seq=tiles_per_seq, n_cast=len(later_weights)),
        out_shape=prompt_shapes + sample_shapes + cast_shapes,
        grid=(n_tiles + 1,),
        in_specs=([row(D_MODEL), _whole_spec(xs2d.shape)] + [_const_spec(v.shape) for v in vectors]
                  + [pl.BlockSpec(memory_space=pl.ANY)] * len(weights) + cast_specs),
        out_specs=prompt_specs + tuple(_whole_spec(s.shape) for s in sample_shapes) + tuple(cast_specs),
        scratch_shapes=[pltpu.VMEM((tm, D_MODEL), BF16), pltpu.VMEM((tm, D_FF), BF16),
                        pltpu.VMEM((8, BLOCK), F32),
                        pltpu.VMEM((D_MODEL, D_FF), BF16), pltpu.VMEM((D_MODEL, D_FF), BF16),
                        pltpu.VMEM((D_FF, D_MODEL), BF16),
                        pltpu.VMEM((W_M_ROWS, D_MODEL), BF16), pltpu.VMEM((WA_ROWS, D_MODEL), BF16),
                        pltpu.VMEM((CAST_SLOTS, CAST_ROWS, CAST_COLS), F32),
                        pltpu.SemaphoreType.DMA((CAST_SLOTS,))],
        compiler_params=pltpu.CompilerParams(dimension_semantics=("arbitrary",),
                                             vmem_limit_bytes=VMEM_LIMIT_BYTES),
        name="front",
    )(x2d, xs2d, *vectors, *weights, *later_weights)
    n_p, n_s = len(prompt_shapes), len(sample_shapes)
    return outs[:n_p], outs[n_p:n_p + n_s], outs[n_p + n_s:]


def _back_tile(x1_ref, y_ref, params, out_ref, h_ref, act_ref):
    wo_ref, n_ref, wg_ref, wu_ref, wd_ref = params
    x2 = x1_ref[...] + _dot(y_ref[...], wo_ref[...])
    h_ref[...] = _rms_rows(x2, n_ref[...]).astype(BF16)
    out_ref[...] = x2 + FFN_RES_WEIGHT * _ffn(h_ref, wg_ref, wu_ref, wd_ref, act_ref)


def _back_kernel(x1_ref, y_ref, x1s_ref, ys_ref, wo_ref, n_ref, wg_ref, wu_ref, wd_ref, out_ref, outs_ref,
                 h_ref, act_ref, *, n_tiles):
    params = (wo_ref, n_ref, wg_ref, wu_ref, wd_ref)
    i = pl.program_id(0)

    @pl.when(i < n_tiles)
    def _():
        _back_tile(x1_ref, y_ref, params, out_ref, h_ref, act_ref)

    @pl.when(i == n_tiles)
    def _():
        ns = x1s_ref.shape[0]
        _back_tile(x1s_ref, ys_ref, params, outs_ref, h_ref.at[0:ns], act_ref.at[0:ns])


def _back(x1, y, x1s, ys, params, tm):
    n, ns = x1.shape[0], x1s.shape[0]
    n_tiles = n // tm
    row = pl.BlockSpec((tm, D_MODEL), lambda i: (jnp.minimum(i, n_tiles - 1), 0))
    return pl.pallas_call(
        functools.partial(_back_kernel, n_tiles=n_tiles),
        out_shape=(jax.ShapeDtypeStruct((n, D_MODEL), F32), jax.ShapeDtypeStruct((ns, D_MODEL), F32)),
        grid=(n_tiles + 1,),
        in_specs=[row, row, _whole_spec(x1s.shape), _whole_spec(ys.shape)] + [_const_spec(p.shape) for p in params],
        out_specs=(row, _whole_spec((ns, D_MODEL))),
        scratch_shapes=[pltpu.VMEM((tm, D_MODEL), BF16), pltpu.VMEM((tm, D_FF), BF16)],
        compiler_params=pltpu.CompilerParams(dimension_semantics=("arbitrary",),
                                             vmem_limit_bytes=VMEM_LIMIT_BYTES),
        name="back",
    )(x1, y, x1s, ys, *params)


def _prompt_mixer_kernel(sinks_ref, qkv_ref, vt_ref, og_ref, cols_ref, rows_ref, qa_ref,
                         kx_ref, vx_ref, kxp_ref, vxp_ref,
                         y_ref, c_out_ref, n_out_ref,
                         c_ref, n_ref):
    j = pl.program_id(0)
    batch = qkv_ref.shape[0]

    @pl.when(j == 0)
    def _():
        c_ref[...] = jnp.zeros_like(c_ref)
        n_ref[...] = jnp.zeros_like(n_ref)

    rows = lax.broadcasted_iota(jnp.int32, (BLOCK, BLOCK), 0)
    cols = lax.broadcasted_iota(jnp.int32, (BLOCK, BLOCK), 1)
    causal = cols <= rows
    lane_lo = lax.broadcasted_iota(jnp.int32, (2 * BLOCK, KV_WIDTH), 1) < A_DIM
    ones_m = jnp.ones((BLOCK, M_DIM), BF16)
    ones_half = (jnp.where(lane_lo, 1.0, 0.0).astype(BF16), jnp.where(lane_lo, 0.0, 1.0).astype(BF16))
    qi = lax.broadcasted_iota(jnp.int32, (2 * BLOCK, 2 * BLOCK), 0) % BLOCK
    kc = lax.broadcasted_iota(jnp.int32, (2 * BLOCK, 2 * BLOCK), 1)
    first_valid = jnp.where(j == 0, BLOCK, 0)
    valid = (kc >= qi) & (kc <= qi + WINDOW) & (kc >= first_valid)
    top_rows = lax.broadcasted_iota(jnp.int32, (2 * BLOCK, 1), 0) < BLOCK

    heads = range(M_HEADS)
    groups = [(kv, parity) for kv in range(KV_HEADS) for parity in range(2)]
    m_lo = lambda h: h * M_DIM

    def first_matmuls(b):
        st = {"col": cols_ref[b]}
        k = [qkv_ref[b, :, M_WIDTH + m_lo(h):M_WIDTH + m_lo(h) + M_DIM] for h in heads]
        st["c_prev"] = [c_ref[b, h] for h in heads]
        st["n_prev"] = [n_ref[b, h] for h in heads]
        st["qkc"], st["upd"], st["sc"] = [], [], []
        for h in heads:
            q = qkv_ref[b, :, m_lo(h):m_lo(h) + M_DIM]
            n_rep = jnp.broadcast_to(st["n_prev"][h][0:1, :], (BLOCK, M_DIM)).astype(BF16)
            rhs = jnp.concatenate([k[h], st["c_prev"][h].astype(BF16), n_rep], axis=0)
            st["qkc"].append(_dot_nt(q, rhs))
        for h in heads:
            wend_row = rows_ref[b, 0, 1][h:h + 1, :]
            vw_t = (vt_ref[b, 0, h].astype(F32) * wend_row).astype(BF16)
            w_rep = jnp.broadcast_to(wend_row, (16, BLOCK)).astype(BF16)
            st["upd"].append(_dot(jnp.concatenate([vw_t, w_rep], axis=0), k[h]))
        q2 = [jnp.concatenate([qa_ref[b, :, (2 * kv) * 128:(2 * kv + 1) * 128],
                               qa_ref[b, :, (2 * kv + 1) * 128:(2 * kv + 2) * 128]], axis=0)
              for kv in range(KV_HEADS)]
        for kv, parity in groups:
            var = (2 * kv + parity) * KV_WIDTH
            k_band = jnp.concatenate([kxp_ref[b, :, var:var + KV_WIDTH], kx_ref[b, :, var:var + KV_WIDTH]], axis=0)
            st["sc"].append(_dot_nt(q2[kv], k_band))
        return st

    def vector_work(b, st):
        st["s"], st["pr"], st["e_sink"] = [], [], []
        for h in heads:
            g_row = rows_ref[b, 0, 0][h:h + 1, :]
            decay = rows_ref[b, 0, 2][h:h + 1, :]
            d = jnp.where(causal, jnp.exp2(st["col"][:, h:h + 1] + g_row), 0.0)
            st["s"].append((st["qkc"][h][:, 0:BLOCK] * d).astype(BF16))
            c_ref[b, h] = decay * st["c_prev"][h] + st["upd"][h][0:M_DIM]
            n_ref[b, h] = decay * st["n_prev"][h] + st["upd"][h][M_DIM:M_DIM + 8]
        for gi, (kv, parity) in enumerate(groups):
            sink = jnp.where(top_rows, sinks_ref[4 * kv + parity], sinks_ref[4 * kv + 2 + parity]) * LOG2E
            scm = jnp.where(valid, st["sc"][gi], NEG_INF)
            mx = jnp.maximum(jnp.max(scm, axis=-1, keepdims=True), sink)
            st["pr"].append(jnp.exp2(scm - mx).astype(BF16))
            st["e_sink"].append(jnp.exp2(sink - mx))

    def second_matmuls(b, st):
        st["sv"], st["pv"] = [], []
        for h in heads:
            v = qkv_ref[b, :, 2 * M_WIDTH + m_lo(h):2 * M_WIDTH + m_lo(h) + M_DIM]
            st["sv"].append(_dot(st["s"][h], jnp.concatenate([v, ones_m], axis=1)))
        for gi, (kv, parity) in enumerate(groups):
            var = (2 * kv + parity) * KV_WIDTH
            v_band = jnp.concatenate([vxp_ref[b, :, var:var + KV_WIDTH], vx_ref[b, :, var:var + KV_WIDTH]], axis=0)
            st["pv"].append(_dot(st["pr"][gi], jnp.concatenate([v_band, ones_half[parity]], axis=1)))

    def normalise(b, st):
        col_b, qkc, sv, pv, e_sink = st["col"], st["qkc"], st["sv"], st["pv"], st["e_sink"]
        for h in heads:
            lo, hi = m_lo(h), m_lo(h) + M_DIM
            wi_col = col_b[:, 8 + h:9 + h]
            num = wi_col * qkc[h][:, BLOCK:2 * BLOCK] + sv[h][:, 0:M_DIM]
            den = wi_col * qkc[h][:, 2 * BLOCK:3 * BLOCK] + sv[h][:, M_DIM:2 * M_DIM]
            hh = num / jnp.maximum(jnp.abs(den), col_b[:, 16 + h:17 + h])
            hh = hh * lax.rsqrt(jnp.mean(hh * hh, axis=-1, keepdims=True) + RMS_EPS)
            y_ref[b, :, lo:hi] = (hh * og_ref[b, :, lo:hi]).astype(BF16)
        for kv in range(KV_HEADS):
            acc = pv[2 * kv] + pv[2 * kv + 1]
            denom = acc[:, KV_WIDTH:] + jnp.where(lane_lo, e_sink[2 * kv], e_sink[2 * kv + 1])
            ya = (acc[:, 0:KV_WIDTH] / denom).astype(BF16)
            p0, p1 = 2 * kv, 2 * kv + 1
            y_ref[b, :, M_WIDTH + p0 * 128:M_WIDTH + (p0 + 1) * 128] = ya[0:BLOCK]
            y_ref[b, :, M_WIDTH + p1 * 128:M_WIDTH + (p1 + 1) * 128] = ya[BLOCK:2 * BLOCK]

    for b in range(batch):
        st = first_matmuls(b)
        vector_work(b, st)
        second_matmuls(b, st)
        normalise(b, st)

    @pl.when(j == pl.num_programs(0) - 1)
    def _():
        c_out_ref[...] = c_ref[...]
        n_out_ref[...] = n_ref[...]


def _prompt_mixer(sinks, qkv, vt, og, cols, rows, qa, kx, vx, batch, seq):
    nblk = seq // BLOCK
    r3 = lambda a: a.reshape(batch, seq, a.shape[-1])
    cur = lambda w: pl.BlockSpec((batch, BLOCK, w), lambda j: (0, j, 0))
    prev = lambda w: pl.BlockSpec((batch, BLOCK, w), lambda j: (0, jnp.maximum(j - 1, 0), 0))
    state = lambda shape: pl.BlockSpec((batch,) + shape, lambda j: (0,) * (len(shape) + 1))
    out_shape = (
        jax.ShapeDtypeStruct((batch, seq, D_MODEL), BF16),
        jax.ShapeDtypeStruct((batch, M_HEADS, M_DIM, M_DIM), F32),
        jax.ShapeDtypeStruct((batch, M_HEADS, 8, M_DIM), F32),
    )
    y, pc, pn = pl.pallas_call(
        _prompt_mixer_kernel,
        out_shape=out_shape,
        grid=(nblk,),
        in_specs=[pl.BlockSpec(memory_space=pltpu.SMEM),
                  cur(3 * M_WIDTH),
                  pl.BlockSpec((batch, 1, M_HEADS, M_DIM, BLOCK), lambda j: (0, j, 0, 0, 0)),
                  cur(M_WIDTH), cur(128),
                  pl.BlockSpec((batch, 1, 3, 8, BLOCK), lambda j: (0, j, 0, 0, 0)),
                  cur(A_WIDTH), cur(4 * KV_WIDTH), cur(4 * KV_WIDTH), prev(4 * KV_WIDTH), prev(4 * KV_WIDTH)],
        out_specs=(cur(D_MODEL), state((M_HEADS, M_DIM, M_DIM)), state((M_HEADS, 8, M_DIM))),
        scratch_shapes=[pltpu.VMEM((batch, M_HEADS, M_DIM, M_DIM), F32),
                        pltpu.VMEM((batch, M_HEADS, 8, M_DIM), F32)],
        compiler_params=pltpu.CompilerParams(dimension_semantics=("arbitrary",),
                                             vmem_limit_bytes=VMEM_LIMIT_BYTES),
        name="prompt_mixer",
    )(sinks, r3(qkv), vt.reshape(batch, nblk, M_HEADS, M_DIM, BLOCK), r3(og), r3(cols), rows, r3(qa),
      r3(kx), r3(vx), r3(kx), r3(vx))
    return y.reshape(batch * seq, D_MODEL), pc, pn


def _sample_mlstm_kernel(bi_ref, bf_ref, q_ref, k_ref, v_ref, og_ref, gates_ref, m0_ref, n0_ref, c_ref,
                         y_ref, n_out_ref, m_out_ref, c_out_ref, decay_ref, qr_ref):
    h = pl.program_id(0)
    nb = q_ref.shape[0]
    q_rows = q_ref[...].astype(F32)
    k_rows = k_ref[...].astype(F32)
    qr_ref[...] = q_rows
    qt, kt, vt = q_rows.T, k_rows.T, v_ref[...].astype(F32).T
    i_pre = gates_ref[pl.ds(h, 1), :] + bi_ref[h]
    a = _log_sigmoid(gates_ref[pl.ds(M_HEADS + h, 1), :] + bf_ref[h]) + m0_ref[pl.ds(h, 1), :]
    m_t = jnp.maximum(a, i_pre)
    w_inter = jnp.exp(a - m_t)
    w_in = jnp.exp(i_pre - m_t)
    scores = jnp.sum(qt * kt, axis=0, keepdims=True) * w_in
    n0t = n0_ref[...].T
    nq = jnp.sum(n0t * qt, axis=0, keepdims=True)

    rows = lax.broadcasted_iota(jnp.int32, (nb, M_DIM), 0)
    cols = lax.broadcasted_iota(jnp.int32, (M_DIM, nb), 1)
    vw_t = (vt * w_in).astype(BF16)

    decay_ref[...] = jnp.broadcast_to(w_inter, (M_DIM, nb)).T

    def body(grp, cq_t):
        base = grp * SAMPLE_UNROLL
        for u in range(SAMPLE_UNROLL):
            b = base + u
            col = jnp.sum(c_ref[b, 0] * qr_ref[pl.ds(b, 1), :], axis=-1, keepdims=True)
            cq_t = jnp.where(cols == b, col, cq_t)
        outer = []
        for u in range(SAMPLE_UNROLL):
            k_only_b = jnp.where(rows == base + u, k_rows, 0.0).astype(BF16)
            outer.append(_dot(vw_t, k_only_b))
        for u in range(SAMPLE_UNROLL):
            b = base + u
            c_out_ref[b, 0] = decay_ref[pl.ds(b, 1), :] * c_ref[b, 0] + outer[u]
        return cq_t

    cq_t = lax.fori_loop(0, nb // SAMPLE_UNROLL, body, jnp.zeros((M_DIM, nb), F32))

    num = w_inter * cq_t + scores * vt
    den = w_inter * nq + scores
    hh = num / jnp.maximum(jnp.abs(den), jnp.exp(-m_t))
    hh = hh * lax.rsqrt(jnp.mean(hh * hh, axis=0, keepdims=True) + RMS_EPS)
    y_ref[...] = hh.T * og_ref[...]
    n_out_ref[...] = (w_inter * n0t + w_in * kt).T
    m_out_ref[0] = m_t


def _sample_mlstm(b_i, b_f, qkv, og, gates_t, m0_t, n0, c0):
    nb = qkv.shape[0]
    smem = pl.BlockSpec(memory_space=pltpu.SMEM)
    head = lambda off: pl.BlockSpec((nb, M_DIM), lambda h: (0, off + h))
    out_shape = (
        jax.ShapeDtypeStruct((nb, M_WIDTH), F32),
        jax.ShapeDtypeStruct((nb, M_WIDTH), F32),
        jax.ShapeDtypeStruct((M_HEADS, 1, nb), F32),
        jax.ShapeDtypeStruct((nb, M_HEADS, M_DIM, M_DIM), F32),
    )
    c_spec = pl.BlockSpec((nb, 1, M_DIM, M_DIM), lambda h: (0, h, 0, 0))
    return pl.pallas_call(
        _sample_mlstm_kernel,
        out_shape=out_shape,
        grid=(M_HEADS,),
        in_specs=[smem, smem, head(0), head(M_HEADS), head(2 * M_HEADS), head(0),
                  pl.BlockSpec(gates_t.shape, lambda h: (0, 0)), pl.BlockSpec(m0_t.shape, lambda h: (0, 0)),
                  head(0), c_spec],
        out_specs=(head(0), head(0), pl.BlockSpec((1, 1, nb), lambda h: (h, 0, 0)), c_spec),
        scratch_shapes=[pltpu.VMEM((nb, M_DIM), F32), pltpu.VMEM((nb, M_DIM), F32)],
        compiler_params=pltpu.CompilerParams(dimension_semantics=("arbitrary",),
                                             vmem_limit_bytes=VMEM_LIMIT_BYTES),
        name="sample_mlstm",
    )(b_i, b_f, qkv, qkv, qkv, og, gates_t, m0_t, n0, c0)


SAMPLE_TILE = 16
SAMPLE_UNROLL = 8


def _sample_swa_kernel(q2_ref, kc_ref, vc_ref, kn_ref, vn_ref, sink_ref, o_ref, ko_ref, vo_ref):
    sink = sink_ref[...] * LOG2E
    w = kc_ref.shape[2]
    tile = range(SAMPLE_TILE)
    newest = lax.broadcasted_iota(jnp.int32, (KV_WIDTH, w), 1) == w - 1
    pad = jnp.zeros((KV_WIDTH - SAMPLE_TILE, KV_WIDTH), F32)
    kn_t = jnp.concatenate([kn_ref[...], pad], axis=0).T
    vn_t = jnp.concatenate([vn_ref[...], pad], axis=0).T
    q2 = [q2_ref[b] for b in tile]
    k_new = [kn_ref[b:b + 1, :] for b in tile]
    v_new = [vn_ref[b:b + 1, :] for b in tile]
    s_c = [_dot(q2[b], kc_ref[b].astype(BF16)) for b in tile]
    s_n = [jnp.sum(q2[b].astype(F32) * k_new[b], axis=-1, keepdims=True) for b in tile]
    mx = [jnp.maximum(jnp.maximum(jnp.max(s_c[b], axis=-1, keepdims=True), s_n[b]), sink) for b in tile]
    p_c = [jnp.exp2(s_c[b] - mx[b]) for b in tile]
    p_n = [jnp.exp2(s_n[b] - mx[b]) for b in tile]
    denom = [jnp.sum(p_c[b], axis=-1, keepdims=True) + p_n[b] + jnp.exp2(sink - mx[b]) for b in tile]
    o = [_dot_nt(p_c[b].astype(BF16), vc_ref[b].astype(BF16)) for b in tile]
    for b in tile:
        o_ref[b] = (o[b] + p_n[b] * v_new[b]) / denom[b]
        ko_ref[b] = jnp.where(newest, kn_t[:, b:b + 1], pltpu.roll(kc_ref[b], w - 1, 1))
        vo_ref[b] = jnp.where(newest, vn_t[:, b:b + 1], pltpu.roll(vc_ref[b], w - 1, 1))


def _sample_swa(q2, k_cache, v_cache, k_new, v_new, sink_col):
    nb, _, w = k_cache.shape
    t3 = lambda a, c: pl.BlockSpec((SAMPLE_TILE, a, c), lambda i: (i, 0, 0))
    t2 = pl.BlockSpec((SAMPLE_TILE, KV_WIDTH), lambda i: (i, 0))
    out_shape = (
        jax.ShapeDtypeStruct((nb, A_HEADS, KV_WIDTH), F32),
        jax.ShapeDtypeStruct((nb, KV_WIDTH, w), F32),
        jax.ShapeDtypeStruct((nb, KV_WIDTH, w), F32),
    )
    return pl.pallas_call(
        _sample_swa_kernel,
        out_shape=out_shape,
        grid=(nb // SAMPLE_TILE,),
        in_specs=[t3(A_HEADS, KV_WIDTH), t3(KV_WIDTH, w), t3(KV_WIDTH, w), t2, t2,
                  pl.BlockSpec((A_HEADS, 1), lambda i: (0, 0))],
        out_specs=(t3(A_HEADS, KV_WIDTH), t3(KV_WIDTH, w), t3(KV_WIDTH, w)),
        compiler_params=pltpu.CompilerParams(dimension_semantics=("arbitrary",)),
        name="sample_swa",
    )(q2, k_cache, v_cache, k_new, v_new, sink_col)


def kernel(x_prompt, x_sample, cache_swa_k, cache_swa_v, state_mlstm_C, state_mlstm_n, state_mlstm_m,
           ffn1_norm, ffn1_w_gate, ffn1_w_up, ffn1_w_down, mix_norm, w_in, mlstm_b_i, mlstm_b_f,
           mlstm_out_norm, swa_q_norm, swa_k_norm, swa_sinks, w_out, ffn2_norm, ffn2_w_gate,
           ffn2_w_up, ffn2_w_down):
    depth = ffn1_norm.shape[0]
    assert depth == 1
    batch, seq, _ = x_prompt.shape
    nb = x_sample.shape[0]
    assert x_sample.shape[1] == 1 and seq % BLOCK == 0

    front_weights = (ffn1_w_gate[0], ffn1_w_up[0], ffn1_w_down[0], jnp.transpose(w_in[0]))
    n1 = ffn1_norm[0].reshape(1, D_MODEL)
    n2 = mix_norm[0].reshape(1, D_MODEL)
    n3 = ffn2_norm[0].reshape(1, D_MODEL)
    qgain = jnp.tile(swa_q_norm[0], A_HEADS).reshape(1, A_WIDTH)
    kgain = jnp.tile(swa_k_norm[0], KV_HEADS).reshape(1, KV_WIDTH)
    ogain = mlstm_out_norm[0].reshape(1, M_WIDTH)
    b_i, b_f = mlstm_b_i[0], mlstm_b_f[0]
    bias8 = jnp.concatenate([b_i, b_f]).reshape(2 * M_HEADS, 1)
    front_vectors = (n1, n2, qgain, kgain, ogain, bias8)
    sinks = swa_sinks[0]

    xp = x_prompt.reshape(batch * seq, D_MODEL)
    xs = x_sample.reshape(nb, D_MODEL)
    back_weights_f32 = (w_out[0], ffn2_w_gate[0], ffn2_w_up[0], ffn2_w_down[0])
    prompt_front, sample_front, (wo, wg2, wu2, wd2) = _front(xp, xs, front_vectors, front_weights,
                                                             back_weights_f32, tm=512, seq=seq)
    back_params = (wo, n3, wg2, wu2, wd2)
    x1, qkv, og, qa, ka, va, vt, kx, vx, rows, cols, pm = prompt_front
    x1s, qkvs, ogs, qas, kas, vas = sample_front[:6]
    gates_t = sample_front[-1]

    rows = rows.reshape(batch, seq // BLOCK, 3, 8, BLOCK)
    y, pc, pn = _prompt_mixer(sinks, qkv, vt, og, cols, rows, qa, kx, vx, batch, seq)
    pk = ka.reshape(batch, WINDOW, KV_HEADS, A_DIM)
    pv = va.reshape(batch, WINDOW, KV_HEADS, A_DIM)
    pn = pn[:, :, 0, :]
    pm = pm[:, 0:M_HEADS, 0]

    y_m, sn, mt, sc = _sample_mlstm(b_i, b_f, qkvs, ogs, gates_t, jnp.transpose(state_mlstm_m[0]),
                                    state_mlstm_n[0].reshape(nb, M_WIDTH), state_mlstm_C[0])
    sn = sn.reshape(nb, M_HEADS, M_DIM)
    sm = jnp.transpose(mt[:, 0, :])

    qa_h = qas.reshape(nb, A_HEADS, A_DIM)
    zeros = jnp.zeros_like(qa_h)
    in_lo = (jnp.arange(A_HEADS) // A_GROUP == 0)[None, :, None]
    q2 = jnp.concatenate([jnp.where(in_lo, qa_h, zeros), jnp.where(in_lo, zeros, qa_h)], axis=-1)
    wb = cache_swa_k.shape[2]
    to_feature_major = lambda c: jnp.transpose(c[0], (0, 2, 3, 1)).reshape(nb, KV_WIDTH, wb)
    to_window_major = lambda c: jnp.transpose(c.reshape(nb, KV_HEADS, A_DIM, wb), (0, 3, 1, 2))[None]
    kc, vc = to_feature_major(cache_swa_k), to_feature_major(cache_swa_v)
    o2, sk, sv = _sample_swa(q2, kc, vc, kas, vas, sinks.reshape(A_HEADS, 1))
    o2 = o2.reshape(nb, A_HEADS, KV_HEADS, A_DIM)
    y_a = jnp.where(in_lo, o2[:, :, 0, :], o2[:, :, 1, :]).reshape(nb, A_WIDTH)
    ys_in = jnp.concatenate([y_m, y_a], axis=-1).astype(BF16)

    yp, ys = _back(x1, y, x1s, ys_in, back_params, tm=1024)
    yp = yp.reshape(batch, seq, D_MODEL)
    ys = ys.reshape(nb, 1, D_MODEL)

    return (yp, ys, pk[None], pv[None], pc[None], pn[None], pm[None],
            to_window_major(sk), to_window_major(sv), sc[None], sn[None], sm[None])
```

```python
import functools
from typing import NamedTuple

import jax
import jax.numpy as jnp
from jax import lax
from jax.experimental import pallas as pl
from jax.experimental.pallas import tpu as pltpu

F32 = jnp.float32
BF16 = jnp.bfloat16

D_MODEL = 1024
D_FF = 2816
FF_CHUNK = 256
N_FF_CHUNKS = D_FF // FF_CHUNK
M_HEADS = 4
M_DIM = 128
M_WIDTH = M_HEADS * M_DIM
A_HEADS = 8
A_DIM = 64
A_WIDTH = A_HEADS * A_DIM
KV_HEADS = 2
KV_WIDTH = KV_HEADS * A_DIM
A_GROUP = A_HEADS // KV_HEADS
WINDOW = 128
BLOCK = 128
GATE_PAD = 128
RMS_EPS = 1e-6
FFN_RES_WEIGHT = 0.5
NEG_INF = float("-inf")
LOG2E = 1.4426950408889634
VMEM_LIMIT_BYTES = 56 * 1024 * 1024
FRONT_VMEM_LIMIT_BYTES = 60 * 1024 * 1024


def _dot(a, b):
    return jnp.dot(a, b, preferred_element_type=F32)


def _dot_nt(a, b):
    return lax.dot_general(a, b, (((1,), (1,)), ((), ())), preferred_element_type=F32)


def _rms_rows(x, gain):
    ms = jnp.mean(x * x, axis=-1, keepdims=True)
    return x * lax.rsqrt(ms + RMS_EPS) * gain


def _log_sigmoid(x):
    return jnp.minimum(x, 0.0) - jnp.log1p(jnp.exp(-jnp.abs(x)))


def _half_tile_mean_sq(x):
    in_lo = lax.broadcasted_iota(jnp.int32, (x.shape[0], 128), 1) < A_DIM
    out = []
    for c in range(x.shape[1] // 128):
        sq = x[:, c * 128:(c + 1) * 128]
        sq = sq * sq
        s_lo = jnp.sum(jnp.where(in_lo, sq, 0.0), axis=-1, keepdims=True)
        s_hi = jnp.sum(jnp.where(in_lo, 0.0, sq), axis=-1, keepdims=True)
        out.append(jnp.where(in_lo, s_lo, s_hi) * (1.0 / A_DIM))
    return out[0] if len(out) == 1 else jnp.concatenate(out, axis=1)


def _ffn(h_ref, wg_ref, wu_ref, wd_ref, act_ref):
    for c in range(N_FF_CHUNKS):
        lo, hi = c * FF_CHUNK, (c + 1) * FF_CHUNK
        h = h_ref[...]
        g = _dot(h, wg_ref[:, lo:hi])
        u = _dot(h, wu_ref[:, lo:hi])
        act_ref[:, lo:hi] = (g * jax.nn.sigmoid(g) * u).astype(BF16)
    return _dot(act_ref[...], wd_ref[...])


N_FRONT_VECTORS = 6
N_FRONT_WEIGHTS = 4
N_FRONT_COMMON_OUTS = 9
N_PROMPT_GATE_OUTS = 3
W_M_ROWS = 4 * M_WIDTH
W_GATE_ROWS = 2 * M_HEADS
WA_KVG = A_WIDTH
WA_GATES = WA_KVG + 2 * KV_WIDTH
WA_ROWS = WA_GATES + GATE_PAD
CAST_ROWS, CAST_COLS = 256, 1024
CAST_SLOTS = 6


class _CastJob(NamedTuple):
    src: object
    src_row: int
    dst: object
    dst_row: int
    col: int
    rows: int
    cols: int
    dst_rows: int


def _cast_jobs(src, dst, src_row=0, dst_row=0, rows=None, dst_rows=None):
    rows = src.shape[0] - src_row if rows is None else rows
    jobs = []
    for r in range(0, rows, CAST_ROWS):
        nr = min(CAST_ROWS, rows - r)
        for c in range(0, src.shape[1], CAST_COLS):
            nc = min(CAST_COLS, src.shape[1] - c)
            jobs.append(_CastJob(src, src_row + r, dst, dst_row + r, c, nr, nc, nr if dst_rows is None else dst_rows))
    return jobs


def _run_cast_jobs(jobs, stage_ref, sem_ref):
    def copy(j):
        job, slot = jobs[j], j % CAST_SLOTS
        return pltpu.make_async_copy(job.src.at[pl.ds(job.src_row, job.rows), pl.ds(job.col, job.cols)],
                                     stage_ref.at[slot, pl.ds(0, job.rows), pl.ds(0, job.cols)], sem_ref.at[slot])

    ahead = CAST_SLOTS - 1
    for j in range(min(ahead, len(jobs))):
        copy(j).start()
    for j, job in enumerate(jobs):
        if j + ahead < len(jobs):
            copy(j + ahead).start()
        copy(j).wait()
        block = stage_ref[j % CAST_SLOTS, 0:job.rows, 0:job.cols]
        if job.dst_rows > job.rows:
            block = jnp.concatenate([block, jnp.zeros((job.dst_rows - job.rows, job.cols), F32)], axis=0)
        job.dst[job.dst_row:job.dst_row + job.dst_rows, job.col:job.col + job.cols] = block.astype(BF16)


def _prompt_gates(gates_t, bias_ref, first_of_seq, m_ref, rows_ref, cols_ref, mfin_ref):
    sub = lax.broadcasted_iota(jnp.int32, (8, BLOCK), 0)
    lane = lax.broadcasted_iota(jnp.int32, (8, BLOCK), 1)
    is_head = sub < M_HEADS
    pad = jnp.zeros((BLOCK - 24, BLOCK), F32)
    scan_shifts = (1, 2, 4, 8, 16, 32, 64)
    m_prev = jnp.where(first_of_seq, 0.0, m_ref[...])[:, 0:1]
    for c in range(rows_ref.shape[0]):
        pre = gates_t[:, c * BLOCK:(c + 1) * BLOCK] + bias_ref[...]
        r = jnp.where(is_head, pre, _log_sigmoid(pre))
        cum = r
        for shift in scan_shifts:
            cum = cum + jnp.where(lane >= shift, pltpu.roll(cum, shift, 1), 0.0)
        bcum = pltpu.roll(cum, M_HEADS, 0)
        g = jnp.where(is_head, r - bcum, 0.0)
        bcum = jnp.where(is_head, bcum, 0.0)
        cm = g
        for shift in scan_shifts:
            cm = jnp.maximum(cm, jnp.where(lane >= shift, pltpu.roll(cm, shift, 1), NEG_INF))
        cm_last = jnp.max(cm, axis=-1, keepdims=True)
        b_last = jnp.sum(jnp.where(lane == BLOCK - 1, bcum, 0.0), axis=-1, keepdims=True)
        mx = jnp.maximum(m_prev, cm)
        mx_last = jnp.maximum(m_prev, cm_last)
        rows_ref[c, 0] = g * LOG2E
        rows_ref[c, 1] = jnp.exp(g - mx_last)
        rows_ref[c, 2] = jnp.broadcast_to(jnp.exp(m_prev - mx_last), (8, BLOCK))
        col_src = jnp.concatenate([mx * -LOG2E, jnp.exp(m_prev - mx), jnp.exp(-(bcum + mx)), pad], axis=0)
        cols_ref[c * BLOCK:(c + 1) * BLOCK, :] = col_src.T
        m_prev = b_last + mx_last
    m_full = jnp.broadcast_to(m_prev, (8, BLOCK))
    m_ref[...] = m_full
    mfin_ref[0] = m_full


def _front_tile(x_ref, params, outs, gate_sink, h_ref, act_ref):
    (n1_ref, wg_ref, wu_ref, wd_ref, n2_ref, wm_ref, wa_ref, qgain_ref, kgain_ref, ogain_ref, _) = params
    (x1_ref, qkv_ref, og_ref, qa_ref, ka_ref, va_ref, vt_ref, kx_ref, vx_ref) = outs
    x = x_ref[...]
    h_ref[...] = _rms_rows(x, n1_ref[...]).astype(BF16)
    x1 = x + FFN_RES_WEIGHT * _ffn(h_ref, wg_ref, wu_ref, wd_ref, act_ref)
    x1_ref[...] = x1
    h_ref[...] = _rms_rows(x1, n2_ref[...]).astype(BF16)
    h = h_ref[...]
    project = lambda w_ref, lo, hi: _dot_nt(h, w_ref[lo:hi, :])

    qa = project(wa_ref, 0, WA_KVG)
    kvg = project(wa_ref, WA_KVG, WA_ROWS)
    v_m = project(wm_ref, 2 * M_WIDTH, 3 * M_WIDTH)
    qkv_ref[:, 2 * M_WIDTH:3 * M_WIDTH] = v_m.astype(BF16)
    for c in range(vt_ref.shape[0]):
        for hd in range(M_HEADS):
            blk = v_m[c * BLOCK:(c + 1) * BLOCK, hd * M_DIM:(hd + 1) * M_DIM]
            vt_ref[c, hd] = blk.T.astype(BF16)
    q_scale = qgain_ref[...] * (A_DIM ** -0.5 * LOG2E)
    qa_ref[...] = (qa * lax.rsqrt(_half_tile_mean_sq(qa) + RMS_EPS) * q_scale).astype(BF16)
    ka = kvg[:, 0:KV_WIDTH]
    ka = ka * lax.rsqrt(_half_tile_mean_sq(ka) + RMS_EPS) * kgain_ref[...]
    va = kvg[:, KV_WIDTH:2 * KV_WIDTH]
    keep = ka_ref.shape[0]
    ka_ref[...] = ka[ka.shape[0] - keep:, :]
    va_ref[...] = va[va.shape[0] - keep:, :]
    gate_sink(kvg[:, 2 * KV_WIDTH:2 * KV_WIDTH + GATE_PAD].T[0:2 * M_HEADS, :])

    in_lo = lax.broadcasted_iota(jnp.int32, ka.shape, 1) < A_DIM
    for src, dst in ((ka, kx_ref), (va, vx_ref)):
        x0 = jnp.where(in_lo, src, 0.0)
        x1 = jnp.where(in_lo, 0.0, src)
        dst[:, 0:128] = x0.astype(BF16)
        dst[:, 128:256] = pltpu.roll(x0, A_DIM, 1).astype(BF16)
        dst[:, 256:384] = pltpu.roll(x1, A_DIM, 1).astype(BF16)
        dst[:, 384:512] = x1.astype(BF16)

    og_ref[...] = jax.nn.sigmoid(project(wm_ref, 3 * M_WIDTH, 4 * M_WIDTH)) * ogain_ref[...]
    k_m = project(wm_ref, M_WIDTH, 2 * M_WIDTH) * (M_DIM ** -0.5)
    qkv_ref[:, M_WIDTH:2 * M_WIDTH] = k_m.astype(BF16)
    qkv_ref[:, 0:M_WIDTH] = project(wm_ref, 0, M_WIDTH).astype(BF16)


def _front_kernel(*refs, n_tiles, tiles_per_seq, n_cast):
    refs = list(refs)
    take = lambda k: [refs.pop(0) for _ in range(k)]
    x_ref, xs_ref = take(2)
    n1_ref, n2_ref, qgain_ref, kgain_ref, ogain_ref, bias_ref = take(N_FRONT_VECTORS)
    wg_hbm, wu_hbm, wd_hbm, wt_hbm = take(N_FRONT_WEIGHTS)
    cast_src = take(n_cast)
    outs_p = take(N_FRONT_COMMON_OUTS)
    rows_ref, cols_ref, mfin_ref = take(N_PROMPT_GATE_OUTS)
    outs_s = take(N_FRONT_COMMON_OUTS)
    (gts_ref,) = take(1)
    cast_dst = take(n_cast)
    h_ref, act_ref, m_ref, wg_ref, wu_ref, wd_ref, wm_ref, wa_ref, stage_ref, sem_ref = refs
    params = (n1_ref, wg_ref, wu_ref, wd_ref, n2_ref, wm_ref, wa_ref, qgain_ref, kgain_ref, ogain_ref, bias_ref)
    i = pl.program_id(0)

    @pl.when(i == 0)
    def _():
        a_src = W_M_ROWS + W_GATE_ROWS
        jobs = (_cast_jobs(wg_hbm, wg_ref) + _cast_jobs(wu_hbm, wu_ref) + _cast_jobs(wd_hbm, wd_ref)
                + _cast_jobs(wt_hbm, wm_ref, rows=W_M_ROWS)
                + _cast_jobs(wt_hbm, wa_ref, src_row=a_src, rows=WA_GATES)
                + _cast_jobs(wt_hbm, wa_ref, src_row=W_M_ROWS, dst_row=WA_GATES, rows=W_GATE_ROWS,
                             dst_rows=GATE_PAD))
        _run_cast_jobs(jobs, stage_ref, sem_ref)

    @pl.when(i < n_tiles)
    def _():
        for src, dst in zip(cast_src, cast_dst):
            dst[...] = src[...].astype(BF16)
        gates = functools.partial(_prompt_gates, bias_ref=bias_ref, first_of_seq=i % tiles_per_seq == 0,
                                  m_ref=m_ref, rows_ref=rows_ref, cols_ref=cols_ref, mfin_ref=mfin_ref)
        _front_tile(x_ref, params, outs_p, gates, h_ref, act_ref)

    @pl.when(i == n_tiles)
    def _():
        ns = xs_ref.shape[0]

        def raw_gates(gates_t):
            gts_ref[...] = gates_t

        _front_tile(xs_ref, params, outs_s, raw_gates, h_ref.at[0:ns], act_ref.at[0:ns])


def _const_spec(shape):
    nd = len(shape)
    return pl.BlockSpec(shape, lambda i: (0,) * nd, pipeline_mode=pl.Buffered(1))


def _whole_spec(shape):
    nd = len(shape)
    return pl.BlockSpec(shape, lambda i: (0,) * nd)


def _front_out_shapes(n, n_cache_rows):
    return (
        jax.ShapeDtypeStruct((n, D_MODEL), F32),
        jax.ShapeDtypeStruct((n, 3 * M_WIDTH), BF16),
        jax.ShapeDtypeStruct((n, M_WIDTH), F32),
        jax.ShapeDtypeStruct((n, A_WIDTH), BF16),
        jax.ShapeDtypeStruct((n_cache_rows, KV_WIDTH), F32),
        jax.ShapeDtypeStruct((n_cache_rows, KV_WIDTH), F32),
        jax.ShapeDtypeStruct((n // BLOCK, M_HEADS, M_DIM, BLOCK), BF16),
        jax.ShapeDtypeStruct((n, 4 * KV_WIDTH), BF16),
        jax.ShapeDtypeStruct((n, 4 * KV_WIDTH), BF16),
    )


def _cast_chunk_count(n_rows, max_chunks):
    for k in range(max_chunks, 0, -1):
        if n_rows % k == 0 and (n_rows // k) % 16 == 0:
            return k
    raise ValueError(n_rows)


def _front(x2d, xs2d, vectors, weights, later_weights, tm, seq):
    n, ns = x2d.shape[0], xs2d.shape[0]
    assert seq % tm == 0 and tm >= WINDOW
    assert len(vectors) == N_FRONT_VECTORS and len(weights) == N_FRONT_WEIGHTS
    assert weights[3].shape == (W_M_ROWS + W_GATE_ROWS + WA_GATES, D_MODEL)
    n_tiles = n // tm
    n_seqs = n // seq
    tiles_per_seq = seq // tm
    nb_t = tm // BLOCK
    tile = lambda i: jnp.minimum(i, n_tiles - 1)
    seq_of = lambda i: tile(i) // tiles_per_seq
    row = lambda w: pl.BlockSpec((tm, w), lambda i: (tile(i), 0))
    tail = pl.BlockSpec((WINDOW, KV_WIDTH), lambda i: (seq_of(i), 0))
    prompt_specs = (row(D_MODEL), row(3 * M_WIDTH), row(M_WIDTH), row(A_WIDTH), tail, tail,
                    pl.BlockSpec((nb_t, M_HEADS, M_DIM, BLOCK), lambda i: (tile(i), 0, 0, 0)),
                    row(4 * KV_WIDTH), row(4 * KV_WIDTH),
                    pl.BlockSpec((nb_t, 3, 8, BLOCK), lambda i: (tile(i), 0, 0, 0)), row(128),
                    pl.BlockSpec((1, 8, BLOCK), lambda i: (seq_of(i), 0, 0)))
    prompt_shapes = _front_out_shapes(n, n_seqs * WINDOW) + (
        jax.ShapeDtypeStruct((n // BLOCK, 3, 8, BLOCK), F32),
        jax.ShapeDtypeStruct((n, 128), F32),
        jax.ShapeDtypeStruct((n_seqs, 8, BLOCK), F32),
    )
    sample_shapes = _front_out_shapes(ns, ns) + (
        jax.ShapeDtypeStruct((2 * M_HEADS, ns), F32),
    )

    def chunk_spec(w):
        k = _cast_chunk_count(w.shape[0], n_tiles)
        return pl.BlockSpec((w.shape[0] // k, w.shape[1]), lambda i: (jnp.minimum(i, k - 1), 0))

    cast_specs = [chunk_spec(w) for w in later_weights]
    cast_shapes = tuple(jax.ShapeDtypeStruct(w.shape, BF16) for w in later_weights)
    outs = pl.pallas_call(
        functools.partial(_front_kernel, n_tiles=n_tiles, tiles_per_seq=tiles_per_seq, n_cast=len(later_weights)),
        out_shape=prompt_shapes + sample_shapes + cast_shapes,
        grid=(n_tiles + 1,),
        in_specs=([row(D_MODEL), _whole_spec(xs2d.shape)] + [_const_spec(v.shape) for v in vectors]
                  + [pl.BlockSpec(memory_space=pl.ANY)] * len(weights) + cast_specs),
        out_specs=prompt_specs + tuple(_whole_spec(s.shape) for s in sample_shapes) + tuple(cast_specs),
        scratch_shapes=[pltpu.VMEM((tm, D_MODEL), BF16), pltpu.VMEM((tm, D_FF), BF16),
                        pltpu.VMEM((8, BLOCK), F32),
                        pltpu.VMEM((D_MODEL, D_FF), BF16), pltpu.VMEM((D_MODEL, D_FF), BF16),
                        pltpu.VMEM((D_FF, D_MODEL), BF16),
                        pltpu.VMEM((W_M_ROWS, D_MODEL), BF16), pltpu.VMEM((WA_ROWS, D_MODEL), BF16),
                        pltpu.VMEM((CAST_SLOTS, CAST_ROWS, CAST_COLS), F32),
                        pltpu.SemaphoreType.DMA((CAST_SLOTS,))],
        compiler_params=pltpu.CompilerParams(dimension_semantics=("arbitrary",),
                                             vmem_limit_bytes=FRONT_VMEM_LIMIT_BYTES),
        name="front",
    )(x2d, xs2d, *vectors, *weights, *later_weights)
    n_p, n_s = len(prompt_shapes), len(sample_shapes)
    return outs[:n_p], outs[n_p:n_p + n_s], outs[n_p + n_s:]


def _back_tile(x1_ref, y_ref, params, out_ref, h_ref, act_ref):
    wo_ref, n_ref, wg_ref, wu_ref, wd_ref = params
    x2 = x1_ref[...] + _dot(y_ref[...], wo_ref[...])
    h_ref[...] = _rms_rows(x2, n_ref[...]).astype(BF16)
    out_ref[...] = x2 + FFN_RES_WEIGHT * _ffn(h_ref, wg_ref, wu_ref, wd_ref, act_ref)


def _back_kernel(x1_ref, y_ref, x1s_ref, ys_ref, wo_ref, n_ref, wg_ref, wu_ref, wd_ref, out_ref, outs_ref,
                 h_ref, act_ref, *, n_tiles):
    params = (wo_ref, n_ref, wg_ref, wu_ref, wd_ref)
    i = pl.program_id(0)

    @pl.when(i < n_tiles)
    def _():
        _back_tile(x1_ref, y_ref, params, out_ref, h_ref, act_ref)

    @pl.when(i == n_tiles)
    def _():
        ns = x1s_ref.shape[0]
        _back_tile(x1s_ref, ys_ref, params, outs_ref, h_ref.at[0:ns], act_ref.at[0:ns])


def _back(x1, y, x1s, ys, params, tm):
    n, ns = x1.shape[0], x1s.shape[0]
    n_tiles = n // tm
    row = pl.BlockSpec((tm, D_MODEL), lambda i: (jnp.minimum(i, n_tiles - 1), 0))
    return pl.pallas_call(
        functools.partial(_back_kernel, n_tiles=n_tiles),
        out_shape=(jax.ShapeDtypeStruct((n, D_MODEL), F32), jax.ShapeDtypeStruct((ns, D_MODEL), F32)),
        grid=(n_tiles + 1,),
        in_specs=[row, row, _whole_spec(x1s.shape), _whole_spec(ys.shape)] + [_const_spec(p.shape) for p in params],
        out_specs=(row, _whole_spec((ns, D_MODEL))),
        scratch_shapes=[pltpu.VMEM((tm, D_MODEL), BF16), pltpu.VMEM((tm, D_FF), BF16)],
        compiler_params=pltpu.CompilerParams(dimension_semantics=("arbitrary",),
                                             vmem_limit_bytes=VMEM_LIMIT_BYTES),
        name="back",
    )(x1, y, x1s, ys, *params)


def _prompt_mixer_kernel(sinks_ref, qkv_ref, vt_ref, og_ref, cols_ref, rows_ref, qa_ref,
                         kx_ref, vx_ref, kxp_ref, vxp_ref,
                         y_ref, c_out_ref, n_out_ref,
                         c_ref, n_ref):
    j = pl.program_id(0)
    batch = qkv_ref.shape[0]

    @pl.when(j == 0)
    def _():
        c_ref[...] = jnp.zeros_like(c_ref)
        n_ref[...] = jnp.zeros_like(n_ref)

    rows = lax.broadcasted_iota(jnp.int32, (BLOCK, BLOCK), 0)
    cols = lax.broadcasted_iota(jnp.int32, (BLOCK, BLOCK), 1)
    causal = cols <= rows
    lane_lo = lax.broadcasted_iota(jnp.int32, (2 * BLOCK, KV_WIDTH), 1) < A_DIM
    ones_m = jnp.ones((BLOCK, M_DIM), BF16)
    ones_half = (jnp.where(lane_lo, 1.0, 0.0).astype(BF16), jnp.where(lane_lo, 0.0, 1.0).astype(BF16))
    qi = lax.broadcasted_iota(jnp.int32, (2 * BLOCK, 2 * BLOCK), 0) % BLOCK
    kc = lax.broadcasted_iota(jnp.int32, (2 * BLOCK, 2 * BLOCK), 1)
    first_valid = jnp.where(j == 0, BLOCK, 0)
    valid = (kc >= qi) & (kc <= qi + WINDOW) & (kc >= first_valid)
    top_rows = lax.broadcasted_iota(jnp.int32, (2 * BLOCK, 1), 0) < BLOCK

    heads = range(M_HEADS)
    groups = [(kv, parity) for kv in range(KV_HEADS) for parity in range(2)]
    m_lo = lambda h: h * M_DIM

    def first_matmuls(b):
        st = {"col": cols_ref[b]}
        k = [qkv_ref[b, :, M_WIDTH + m_lo(h):M_WIDTH + m_lo(h) + M_DIM] for h in heads]
        st["c_prev"] = [c_ref[b, h] for h in heads]
        st["n_prev"] = [n_ref[b, h] for h in heads]
        st["qkc"], st["upd"], st["sc"] = [], [], []
        for h in heads:
            q = qkv_ref[b, :, m_lo(h):m_lo(h) + M_DIM]
            n_rep = jnp.broadcast_to(st["n_prev"][h][0:1, :], (BLOCK, M_DIM)).astype(BF16)
            rhs = jnp.concatenate([k[h], st["c_prev"][h].astype(BF16), n_rep], axis=0)
            st["qkc"].append(_dot_nt(q, rhs))
        for h in heads:
            wend_row = rows_ref[b, 0, 1][h:h + 1, :]
            vw_t = (vt_ref[b, 0, h].astype(F32) * wend_row).astype(BF16)
            w_rep = jnp.broadcast_to(wend_row, (16, BLOCK)).astype(BF16)
            st["upd"].append(_dot(jnp.concatenate([vw_t, w_rep], axis=0), k[h]))
        q2 = [jnp.concatenate([qa_ref[b, :, (2 * kv) * 128:(2 * kv + 1) * 128],
                               qa_ref[b, :, (2 * kv + 1) * 128:(2 * kv + 2) * 128]], axis=0)
              for kv in range(KV_HEADS)]
        for kv, parity in groups:
            var = (2 * kv + parity) * KV_WIDTH
            k_band = jnp.concatenate([kxp_ref[b, :, var:var + KV_WIDTH], kx_ref[b, :, var:var + KV_WIDTH]], axis=0)
            st["sc"].append(_dot_nt(q2[kv], k_band))
        return st

    def vector_work(b, st):
        st["s"], st["pr"], st["e_sink"] = [], [], []
        for h in heads:
            g_row = rows_ref[b, 0, 0][h:h + 1, :]
            decay = rows_ref[b, 0, 2][h:h + 1, :]
            d = jnp.where(causal, jnp.exp2(st["col"][:, h:h + 1] + g_row), 0.0)
            st["s"].append((st["qkc"][h][:, 0:BLOCK] * d).astype(BF16))
            c_ref[b, h] = decay * st["c_prev"][h] + st["upd"][h][0:M_DIM]
            n_ref[b, h] = decay * st["n_prev"][h] + st["upd"][h][M_DIM:M_DIM + 8]
        for gi, (kv, parity) in enumerate(groups):
            sink = jnp.where(top_rows, sinks_ref[4 * kv + parity], sinks_ref[4 * kv + 2 + parity]) * LOG2E
            scm = jnp.where(valid, st["sc"][gi], NEG_INF)
            mx = jnp.maximum(jnp.max(scm, axis=-1, keepdims=True), sink)
            st["pr"].append(jnp.exp2(scm - mx).astype(BF16))
            st["e_sink"].append(jnp.exp2(sink - mx))

    def second_matmuls(b, st):
        st["sv"], st["pv"] = [], []
        for h in heads:
            v = qkv_ref[b, :, 2 * M_WIDTH + m_lo(h):2 * M_WIDTH + m_lo(h) + M_DIM]
            st["sv"].append(_dot(st["s"][h], jnp.concatenate([v, ones_m], axis=1)))
        for gi, (kv, parity) in enumerate(groups):
            var = (2 * kv + parity) * KV_WIDTH
            v_band = jnp.concatenate([vxp_ref[b, :, var:var + KV_WIDTH], vx_ref[b, :, var:var + KV_WIDTH]], axis=0)
            st["pv"].append(_dot(st["pr"][gi], jnp.concatenate([v_band, ones_half[parity]], axis=1)))

    def normalise(b, st):
        col_b, qkc, sv, pv, e_sink = st["col"], st["qkc"], st["sv"], st["pv"], st["e_sink"]
        for h in heads:
            lo, hi = m_lo(h), m_lo(h) + M_DIM
            wi_col = col_b[:, 8 + h:9 + h]
            num = wi_col * qkc[h][:, BLOCK:2 * BLOCK] + sv[h][:, 0:M_DIM]
            den = wi_col * qkc[h][:, 2 * BLOCK:3 * BLOCK] + sv[h][:, M_DIM:2 * M_DIM]
            hh = num / jnp.maximum(jnp.abs(den), col_b[:, 16 + h:17 + h])
            hh = hh * lax.rsqrt(jnp.mean(hh * hh, axis=-1, keepdims=True) + RMS_EPS)
            y_ref[b, :, lo:hi] = (hh * og_ref[b, :, lo:hi]).astype(BF16)
        for kv in range(KV_HEADS):
            acc = pv[2 * kv] + pv[2 * kv + 1]
            denom = acc[:, KV_WIDTH:] + jnp.where(lane_lo, e_sink[2 * kv], e_sink[2 * kv + 1])
            ya = (acc[:, 0:KV_WIDTH] / denom).astype(BF16)
            p0, p1 = 2 * kv, 2 * kv + 1
            y_ref[b, :, M_WIDTH + p0 * 128:M_WIDTH + (p0 + 1) * 128] = ya[0:BLOCK]
            y_ref[b, :, M_WIDTH + p1 * 128:M_WIDTH + (p1 + 1) * 128] = ya[BLOCK:2 * BLOCK]

    for b in range(batch):
        st = first_matmuls(b)
        vector_work(b, st)
        second_matmuls(b, st)
        normalise(b, st)

    @pl.when(j == pl.num_programs(0) - 1)
    def _():
        c_out_ref[...] = c_ref[...]
        n_out_ref[...] = n_ref[...]


def _prompt_mixer(sinks, qkv, vt, og, cols, rows, qa, kx, vx, batch, seq):
    nblk = seq // BLOCK
    r3 = lambda a: a.reshape(batch, seq, a.shape[-1])
    cur = lambda w: pl.BlockSpec((batch, BLOCK, w), lambda j: (0, j, 0))
    prev = lambda w: pl.BlockSpec((batch, BLOCK, w), lambda j: (0, jnp.maximum(j - 1, 0), 0))
    state = lambda shape: pl.BlockSpec((batch,) + shape, lambda j: (0,) * (len(shape) + 1))
    out_shape = (
        jax.ShapeDtypeStruct((batch, seq, D_MODEL), BF16),
        jax.ShapeDtypeStruct((batch, M_HEADS, M_DIM, M_DIM), F32),
        jax.ShapeDtypeStruct((batch, M_HEADS, 8, M_DIM), F32),
    )
    y, pc, pn = pl.pallas_call(
        _prompt_mixer_kernel,
        out_shape=out_shape,
        grid=(nblk,),
        in_specs=[pl.BlockSpec(memory_space=pltpu.SMEM),
                  cur(3 * M_WIDTH),
                  pl.BlockSpec((batch, 1, M_HEADS, M_DIM, BLOCK), lambda j: (0, j, 0, 0, 0)),
                  cur(M_WIDTH), cur(128),
                  pl.BlockSpec((batch, 1, 3, 8, BLOCK), lambda j: (0, j, 0, 0, 0)),
                  cur(A_WIDTH), cur(4 * KV_WIDTH), cur(4 * KV_WIDTH), prev(4 * KV_WIDTH), prev(4 * KV_WIDTH)],
        out_specs=(cur(D_MODEL), state((M_HEADS, M_DIM, M_DIM)), state((M_HEADS, 8, M_DIM))),
        scratch_shapes=[pltpu.VMEM((batch, M_HEADS, M_DIM, M_DIM), F32),
                        pltpu.VMEM((batch, M_HEADS, 8, M_DIM), F32)],
        compiler_params=pltpu.CompilerParams(dimension_semantics=("arbitrary",),
                                             vmem_limit_bytes=VMEM_LIMIT_BYTES),
        name="prompt_mixer",
    )(sinks, r3(qkv), vt.reshape(batch, nblk, M_HEADS, M_DIM, BLOCK), r3(og), r3(cols), rows, r3(qa),
      r3(kx), r3(vx), r3(kx), r3(vx))
    return y.reshape(batch * seq, D_MODEL), pc, pn


def _sample_mlstm_kernel(bi_ref, bf_ref, q_ref, k_ref, v_ref, og_ref, gates_ref, m0_ref, n0_ref, c_ref,
                         y_ref, n_out_ref, m_out_ref, c_out_ref, decay_ref, qr_ref):
    h = pl.program_id(0)
    nb = q_ref.shape[0]
    q_rows = q_ref[...].astype(F32)
    k_rows = k_ref[...].astype(F32)
    qr_ref[...] = q_rows
    qt, kt, vt = q_rows.T, k_rows.T, v_ref[...].astype(F32).T
    i_pre = gates_ref[pl.ds(h, 1), :] + bi_ref[h]
    a = _log_sigmoid(gates_ref[pl.ds(M_HEADS + h, 1), :] + bf_ref[h]) + m0_ref[pl.ds(h, 1), :]
    m_t = jnp.maximum(a, i_pre)
    w_inter = jnp.exp(a - m_t)
    w_in = jnp.exp(i_pre - m_t)
    scores = jnp.sum(qt * kt, axis=0, keepdims=True) * w_in
    n0t = n0_ref[...].T
    nq = jnp.sum(n0t * qt, axis=0, keepdims=True)

    rows = lax.broadcasted_iota(jnp.int32, (nb, M_DIM), 0)
    cols = lax.broadcasted_iota(jnp.int32, (M_DIM, nb), 1)
    vw_t = (vt * w_in).astype(BF16)

    decay_ref[...] = jnp.broadcast_to(w_inter, (M_DIM, nb)).T

    def body(grp, cq_t):
        base = grp * SAMPLE_UNROLL
        for u in range(SAMPLE_UNROLL):
            b = base + u
            col = jnp.sum(c_ref[b, 0] * qr_ref[pl.ds(b, 1), :], axis=-1, keepdims=True)
            cq_t = jnp.where(cols == b, col, cq_t)
        outer = []
        for u in range(SAMPLE_UNROLL):
            k_only_b = jnp.where(rows == base + u, k_rows, 0.0).astype(BF16)
            outer.append(_dot(vw_t, k_only_b))
        for u in range(SAMPLE_UNROLL):
            b = base + u
            c_out_ref[b, 0] = decay_ref[pl.ds(b, 1), :] * c_ref[b, 0] + outer[u]
        return cq_t

    cq_t = lax.fori_loop(0, nb // SAMPLE_UNROLL, body, jnp.zeros((M_DIM, nb), F32))

    num = w_inter * cq_t + scores * vt
    den = w_inter * nq + scores
    hh = num / jnp.maximum(jnp.abs(den), jnp.exp(-m_t))
    hh = hh * lax.rsqrt(jnp.mean(hh * hh, axis=0, keepdims=True) + RMS_EPS)
    y_ref[...] = hh.T * og_ref[...]
    n_out_ref[...] = (w_inter * n0t + w_in * kt).T
    m_out_ref[0] = m_t


def _sample_mlstm(b_i, b_f, qkv, og, gates_t, m0_t, n0, c0):
    nb = qkv.shape[0]
    smem = pl.BlockSpec(memory_space=pltpu.SMEM)
    head = lambda off: pl.BlockSpec((nb, M_DIM), lambda h: (0, off + h))
    out_shape = (
        jax.ShapeDtypeStruct((nb, M_WIDTH), F32),
        jax.ShapeDtypeStruct((nb, M_WIDTH), F32),
        jax.ShapeDtypeStruct((M_HEADS, 1, nb), F32),
        jax.ShapeDtypeStruct((nb, M_HEADS, M_DIM, M_DIM), F32),
    )
    c_spec = pl.BlockSpec((nb, 1, M_DIM, M_DIM), lambda h: (0, h, 0, 0))
    return pl.pallas_call(
        _sample_mlstm_kernel,
        out_shape=out_shape,
        grid=(M_HEADS,),
        in_specs=[smem, smem, head(0), head(M_HEADS), head(2 * M_HEADS), head(0),
                  pl.BlockSpec(gates_t.shape, lambda h: (0, 0)), pl.BlockSpec(m0_t.shape, lambda h: (0, 0)),
                  head(0), c_spec],
        out_specs=(head(0), head(0), pl.BlockSpec((1, 1, nb), lambda h: (h, 0, 0)), c_spec),
        scratch_shapes=[pltpu.VMEM((nb, M_DIM), F32), pltpu.VMEM((nb, M_DIM), F32)],
        compiler_params=pltpu.CompilerParams(dimension_semantics=("arbitrary",),
                                             vmem_limit_bytes=VMEM_LIMIT_BYTES),
        name="sample_mlstm",
    )(b_i, b_f, qkv, qkv, qkv, og, gates_t, m0_t, n0, c0)


SAMPLE_TILE = 16
SAMPLE_UNROLL = 8


def _sample_swa_kernel(q2_ref, kc_ref, vc_ref, kn_ref, vn_ref, sink_ref, o_ref, ko_ref, vo_ref):
    sink = sink_ref[...] * LOG2E
    w = kc_ref.shape[2]
    tile = range(SAMPLE_TILE)
    newest = lax.broadcasted_iota(jnp.int32, (KV_WIDTH, w), 1) == w - 1
    pad = jnp.zeros((KV_WIDTH - SAMPLE_TILE, KV_WIDTH), F32)
    kn_t = jnp.concatenate([kn_ref[...], pad], axis=0).T
    vn_t = jnp.concatenate([vn_ref[...], pad], axis=0).T
    q2 = [q2_ref[b] for b in tile]
    k_new = [kn_ref[b:b + 1, :] for b in tile]
    v_new = [vn_ref[b:b + 1, :] for b in tile]
    s_c = [_dot(q2[b], kc_ref[b].astype(BF16)) for b in tile]
    s_n = [jnp.sum(q2[b].astype(F32) * k_new[b], axis=-1, keepdims=True) for b in tile]
    mx = [jnp.maximum(jnp.maximum(jnp.max(s_c[b], axis=-1, keepdims=True), s_n[b]), sink) for b in tile]
    p_c = [jnp.exp2(s_c[b] - mx[b]) for b in tile]
    p_n = [jnp.exp2(s_n[b] - mx[b]) for b in tile]
    denom = [jnp.sum(p_c[b], axis=-1, keepdims=True) + p_n[b] + jnp.exp2(sink - mx[b]) for b in tile]
    o = [_dot_nt(p_c[b].astype(BF16), vc_ref[b].astype(BF16)) for b in tile]
    for b in tile:
        o_ref[b] = (o[b] + p_n[b] * v_new[b]) / denom[b]
        ko_ref[b] = jnp.where(newest, kn_t[:, b:b + 1], pltpu.roll(kc_ref[b], w - 1, 1))
        vo_ref[b] = jnp.where(newest, vn_t[:, b:b + 1], pltpu.roll(vc_ref[b], w - 1, 1))


def _sample_swa(q2, k_cache, v_cache, k_new, v_new, sink_col):
    nb, _, w = k_cache.shape
    t3 = lambda a, c: pl.BlockSpec((SAMPLE_TILE, a, c), lambda i: (i, 0, 0))
    t2 = pl.BlockSpec((SAMPLE_TILE, KV_WIDTH), lambda i: (i, 0))
    out_shape = (
        jax.ShapeDtypeStruct((nb, A_HEADS, KV_WIDTH), F32),
        jax.ShapeDtypeStruct((nb, KV_WIDTH, w), F32),
        jax.ShapeDtypeStruct((nb, KV_WIDTH, w), F32),
    )
    return pl.pallas_call(
        _sample_swa_kernel,
        out_shape=out_shape,
        grid=(nb // SAMPLE_TILE,),
        in_specs=[t3(A_HEADS, KV_WIDTH), t3(KV_WIDTH, w), t3(KV_WIDTH, w), t2, t2,
                  pl.BlockSpec((A_HEADS, 1), lambda i: (0, 0))],
        out_specs=(t3(A_HEADS, KV_WIDTH), t3(KV_WIDTH, w), t3(KV_WIDTH, w)),
        compiler_params=pltpu.CompilerParams(dimension_semantics=("arbitrary",)),
        name="sample_swa",
    )(q2, k_cache, v_cache, k_new, v_new, sink_col)


def kernel(x_prompt, x_sample, cache_swa_k, cache_swa_v, state_mlstm_C, state_mlstm_n, state_mlstm_m,
           ffn1_norm, ffn1_w_gate, ffn1_w_up, ffn1_w_down, mix_norm, w_in, mlstm_b_i, mlstm_b_f,
           mlstm_out_norm, swa_q_norm, swa_k_norm, swa_sinks, w_out, ffn2_norm, ffn2_w_gate,
           ffn2_w_up, ffn2_w_down):
    depth = ffn1_norm.shape[0]
    assert depth == 1
    batch, seq, _ = x_prompt.shape
    nb = x_sample.shape[0]
    assert x_sample.shape[1] == 1 and seq % BLOCK == 0

    front_weights = (ffn1_w_gate[0], ffn1_w_up[0], ffn1_w_down[0], jnp.transpose(w_in[0]))
    n1 = ffn1_norm[0].reshape(1, D_MODEL)
    n2 = mix_norm[0].reshape(1, D_MODEL)
    n3 = ffn2_norm[0].reshape(1, D_MODEL)
    qgain = jnp.tile(swa_q_norm[0], A_HEADS).reshape(1, A_WIDTH)
    kgain = jnp.tile(swa_k_norm[0], KV_HEADS).reshape(1, KV_WIDTH)
    ogain = mlstm_out_norm[0].reshape(1, M_WIDTH)
    b_i, b_f = mlstm_b_i[0], mlstm_b_f[0]
    bias8 = jnp.concatenate([b_i, b_f]).reshape(2 * M_HEADS, 1)
    front_vectors = (n1, n2, qgain, kgain, ogain, bias8)
    sinks = swa_sinks[0]

    xp = x_prompt.reshape(batch * seq, D_MODEL)
    xs = x_sample.reshape(nb, D_MODEL)
    back_weights_f32 = (w_out[0], ffn2_w_gate[0], ffn2_w_up[0], ffn2_w_down[0])
    prompt_front, sample_front, (wo, wg2, wu2, wd2) = _front(xp, xs, front_vectors, front_weights,
                                                             back_weights_f32, tm=512, seq=seq)
    back_params = (wo, n3, wg2, wu2, wd2)
    x1, qkv, og, qa, ka, va, vt, kx, vx, rows, cols, pm = prompt_front
    x1s, qkvs, ogs, qas, kas, vas = sample_front[:6]
    gates_t = sample_front[-1]

    rows = rows.reshape(batch, seq // BLOCK, 3, 8, BLOCK)
    y, pc, pn = _prompt_mixer(sinks, qkv, vt, og, cols, rows, qa, kx, vx, batch, seq)
    pk = ka.reshape(batch, WINDOW, KV_HEADS, A_DIM)
    pv = va.reshape(batch, WINDOW, KV_HEADS, A_DIM)
    pn = pn[:, :, 0, :]
    pm = pm[:, 0:M_HEADS, 0]

    y_m, sn, mt, sc = _sample_mlstm(b_i, b_f, qkvs, ogs, gates_t, jnp.transpose(state_mlstm_m[0]),
                                    state_mlstm_n[0].reshape(nb, M_WIDTH), state_mlstm_C[0])
    sn = sn.reshape(nb, M_HEADS, M_DIM)
    sm = jnp.transpose(mt[:, 0, :])

    qa_h = qas.reshape(nb, A_HEADS, A_DIM)
    zeros = jnp.zeros_like(qa_h)
    in_lo = (jnp.arange(A_HEADS) // A_GROUP == 0)[None, :, None]
    q2 = jnp.concatenate([jnp.where(in_lo, qa_h, zeros), jnp.where(in_lo, zeros, qa_h)], axis=-1)
    wb = cache_swa_k.shape[2]
    to_feature_major = lambda c: jnp.transpose(c[0], (0, 2, 3, 1)).reshape(nb, KV_WIDTH, wb)
    to_window_major = lambda c: jnp.transpose(c.reshape(nb, KV_HEADS, A_DIM, wb), (0, 3, 1, 2))[None]
    kc, vc = to_feature_major(cache_swa_k), to_feature_major(cache_swa_v)
    o2, sk, sv = _sample_swa(q2, kc, vc, kas, vas, sinks.reshape(A_HEADS, 1))
    o2 = o2.reshape(nb, A_HEADS, KV_HEADS, A_DIM)
    y_a = jnp.where(in_lo, o2[:, :, 0, :], o2[:, :, 1, :]).reshape(nb, A_WIDTH)
    ys_in = jnp.concatenate([y_m, y_a], axis=-1).astype(BF16)

    yp, ys = _back(x1, y, x1s, ys_in, back_params, tm=1024)
    yp = yp.reshape(batch, seq, D_MODEL)
    ys = ys.reshape(nb, 1, D_MODEL)

    return (yp, ys, pk[None], pv[None], pc[None], pn[None], pm[None],
            to_window_major(sk), to_window_major(sv), sc[None], sn[None], sm[None])
```

```python
import functools
from typing import NamedTuple

import jax
import jax.numpy as jnp
from jax import lax
from jax.experimental import pallas as pl
from jax.experimental.pallas import tpu as pltpu

F32 = jnp.float32
BF16 = jnp.bfloat16

D_MODEL = 1024
D_FF = 2816
FF_CHUNK = 256
N_FF_CHUNKS = D_FF // FF_CHUNK
M_HEADS = 4
M_DIM = 128
M_WIDTH = M_HEADS * M_DIM
A_HEADS = 8
A_DIM = 64
A_WIDTH = A_HEADS * A_DIM
KV_HEADS = 2
KV_WIDTH = KV_HEADS * A_DIM
A_GROUP = A_HEADS // KV_HEADS
WINDOW = 128
BLOCK = 128
GATE_PAD = 128
RMS_EPS = 1e-6
FFN_RES_WEIGHT = 0.5
NEG_INF = float("-inf")
LOG2E = 1.4426950408889634
VMEM_LIMIT_BYTES = 56 * 1024 * 1024
FRONT_VMEM_LIMIT_BYTES = 60 * 1024 * 1024


def _dot(a, b):
    return jnp.dot(a, b, preferred_element_type=F32)


def _dot_nt(a, b):
    return lax.dot_general(a, b, (((1,), (1,)), ((), ())), preferred_element_type=F32)


def _rms_rows(x, gain):
    ms = jnp.mean(x * x, axis=-1, keepdims=True)
    return x * lax.rsqrt(ms + RMS_EPS) * gain


def _log_sigmoid(x):
    return jnp.minimum(x, 0.0) - jnp.log1p(jnp.exp(-jnp.abs(x)))


def _half_tile_mean_sq(x):
    in_lo = lax.broadcasted_iota(jnp.int32, (x.shape[0], 128), 1) < A_DIM
    out = []
    for c in range(x.shape[1] // 128):
        sq = x[:, c * 128:(c + 1) * 128]
        sq = sq * sq
        s_lo = jnp.sum(jnp.where(in_lo, sq, 0.0), axis=-1, keepdims=True)
        s_hi = jnp.sum(jnp.where(in_lo, 0.0, sq), axis=-1, keepdims=True)
        out.append(jnp.where(in_lo, s_lo, s_hi) * (1.0 / A_DIM))
    return out[0] if len(out) == 1 else jnp.concatenate(out, axis=1)


def _ffn(h_ref, wg_ref, wu_ref, wd_ref, act_ref, weights_ready=None):
    for c in range(N_FF_CHUNKS):
        lo, hi = c * FF_CHUNK, (c + 1) * FF_CHUNK
        if weights_ready is not None:
            weights_ready("gate_up", c)
        h = h_ref[...]
        g = _dot(h, wg_ref[:, lo:hi])
        u = _dot(h, wu_ref[:, lo:hi])
        act_ref[:, lo:hi] = (g * jax.nn.sigmoid(g) * u).astype(BF16)
    if weights_ready is not None:
        weights_ready("down", 0)
    return _dot(act_ref[...], wd_ref[...])


N_FRONT_VECTORS = 6
N_FRONT_WEIGHTS = 4
N_FRONT_COMMON_OUTS = 9
N_PROMPT_GATE_OUTS = 3
W_M_ROWS = 4 * M_WIDTH
W_GATE_ROWS = 2 * M_HEADS
WA_KVG = A_WIDTH
WA_GATES = WA_KVG + 2 * KV_WIDTH
WA_ROWS = WA_GATES + GATE_PAD
CAST_ROWS, CAST_COLS = D_MODEL, FF_CHUNK
CAST_SLOTS = 6


class _CastJob(NamedTuple):
    src: object
    src_row: int
    dst: object
    dst_row: int
    col: int
    rows: int
    cols: int
    dst_rows: int


def _cast_jobs(src, dst, src_row=0, dst_row=0, rows=None, dst_rows=None):
    rows = src.shape[0] - src_row if rows is None else rows
    jobs = []
    for r in range(0, rows, CAST_ROWS):
        nr = min(CAST_ROWS, rows - r)
        for c in range(0, src.shape[1], CAST_COLS):
            nc = min(CAST_COLS, src.shape[1] - c)
            jobs.append(_CastJob(src, src_row + r, dst, dst_row + r, c, nr, nc, nr if dst_rows is None else dst_rows))
    return jobs


class _CastRing:
    def __init__(self, jobs, stage_ref, sem_ref):
        self.jobs, self.stage_ref, self.sem_ref = jobs, stage_ref, sem_ref
        self.ahead = CAST_SLOTS - 1
        self.done = 0
        for j in range(min(self.ahead, len(jobs))):
            self._copy(j).start()

    def _copy(self, j):
        job, slot = self.jobs[j], j % CAST_SLOTS
        return pltpu.make_async_copy(job.src.at[pl.ds(job.src_row, job.rows), pl.ds(job.col, job.cols)],
                                     self.stage_ref.at[slot, pl.ds(0, job.rows), pl.ds(0, job.cols)],
                                     self.sem_ref.at[slot])

    def finish(self, upto):
        for j in range(self.done, upto):
            job = self.jobs[j]
            if j + self.ahead < len(self.jobs):
                self._copy(j + self.ahead).start()
            self._copy(j).wait()
            block = self.stage_ref[j % CAST_SLOTS, 0:job.rows, 0:job.cols]
            if job.dst_rows > job.rows:
                block = jnp.concatenate([block, jnp.zeros((job.dst_rows - job.rows, job.cols), F32)], axis=0)
            job.dst[job.dst_row:job.dst_row + job.dst_rows, job.col:job.col + job.cols] = block.astype(BF16)
        self.done = max(self.done, upto)


def _prompt_gates(gates_t, bias_ref, first_of_seq, m_ref, rows_ref, cols_ref, mfin_ref):
    sub = lax.broadcasted_iota(jnp.int32, (8, BLOCK), 0)
    lane = lax.broadcasted_iota(jnp.int32, (8, BLOCK), 1)
    is_head = sub < M_HEADS
    pad = jnp.zeros((BLOCK - 24, BLOCK), F32)
    scan_shifts = (1, 2, 4, 8, 16, 32, 64)
    m_prev = jnp.where(first_of_seq, 0.0, m_ref[...])[:, 0:1]
    for c in range(rows_ref.shape[0]):
        pre = gates_t[:, c * BLOCK:(c + 1) * BLOCK] + bias_ref[...]
        r = jnp.where(is_head, pre, _log_sigmoid(pre))
        cum = r
        for shift in scan_shifts:
            cum = cum + jnp.where(lane >= shift, pltpu.roll(cum, shift, 1), 0.0)
        bcum = pltpu.roll(cum, M_HEADS, 0)
        g = jnp.where(is_head, r - bcum, 0.0)
        bcum = jnp.where(is_head, bcum, 0.0)
        cm = g
        for shift in scan_shifts:
            cm = jnp.maximum(cm, jnp.where(lane >= shift, pltpu.roll(cm, shift, 1), NEG_INF))
        cm_last = jnp.max(cm, axis=-1, keepdims=True)
        b_last = jnp.sum(jnp.where(lane == BLOCK - 1, bcum, 0.0), axis=-1, keepdims=True)
        mx = jnp.maximum(m_prev, cm)
        mx_last = jnp.maximum(m_prev, cm_last)
        rows_ref[c, 0] = g * LOG2E
        rows_ref[c, 1] = jnp.exp(g - mx_last)
        rows_ref[c, 2] = jnp.broadcast_to(jnp.exp(m_prev - mx_last), (8, BLOCK))
        col_src = jnp.concatenate([mx * -LOG2E, jnp.exp(m_prev - mx), jnp.exp(-(bcum + mx)), pad], axis=0)
        cols_ref[c * BLOCK:(c + 1) * BLOCK, :] = col_src.T
        m_prev = b_last + mx_last
    m_full = jnp.broadcast_to(m_prev, (8, BLOCK))
    m_ref[...] = m_full
    mfin_ref[0] = m_full


def _front_tile(x_ref, params, outs, gate_sink, h_ref, act_ref, weights_ready=None):
    (n1_ref, wg_ref, wu_ref, wd_ref, n2_ref, wm_ref, wa_ref, qgain_ref, kgain_ref, ogain_ref, _) = params
    (x1_ref, qkv_ref, og_ref, qa_ref, ka_ref, va_ref, vt_ref, kx_ref, vx_ref) = outs
    x = x_ref[...]
    h_ref[...] = _rms_rows(x, n1_ref[...]).astype(BF16)
    x1 = x + FFN_RES_WEIGHT * _ffn(h_ref, wg_ref, wu_ref, wd_ref, act_ref, weights_ready)
    x1_ref[...] = x1
    h_ref[...] = _rms_rows(x1, n2_ref[...]).astype(BF16)
    if weights_ready is not None:
        weights_ready("projection", 0)
    h = h_ref[...]
    project = lambda w_ref, lo, hi: _dot_nt(h, w_ref[lo:hi, :])

    qa = project(wa_ref, 0, WA_KVG)
    kvg = project(wa_ref, WA_KVG, WA_ROWS)
    v_m = project(wm_ref, 2 * M_WIDTH, 3 * M_WIDTH)
    qkv_ref[:, 2 * M_WIDTH:3 * M_WIDTH] = v_m.astype(BF16)
    for c in range(vt_ref.shape[0]):
        for hd in range(M_HEADS):
            blk = v_m[c * BLOCK:(c + 1) * BLOCK, hd * M_DIM:(hd + 1) * M_DIM]
            vt_ref[c, hd] = blk.T.astype(BF16)
    q_scale = qgain_ref[...] * (A_DIM ** -0.5 * LOG2E)
    qa_ref[...] = (qa * lax.rsqrt(_half_tile_mean_sq(qa) + RMS_EPS) * q_scale).astype(BF16)
    ka = kvg[:, 0:KV_WIDTH]
    ka = ka * lax.rsqrt(_half_tile_mean_sq(ka) + RMS_EPS) * kgain_ref[...]
    va = kvg[:, KV_WIDTH:2 * KV_WIDTH]
    keep = ka_ref.shape[0]
    ka_ref[...] = ka[ka.shape[0] - keep:, :]
    va_ref[...] = va[va.shape[0] - keep:, :]
    gate_sink(kvg[:, 2 * KV_WIDTH:2 * KV_WIDTH + GATE_PAD].T[0:2 * M_HEADS, :])

    in_lo = lax.broadcasted_iota(jnp.int32, ka.shape, 1) < A_DIM
    for src, dst in ((ka, kx_ref), (va, vx_ref)):
        x0 = jnp.where(in_lo, src, 0.0)
        x1 = jnp.where(in_lo, 0.0, src)
        dst[:, 0:128] = x0.astype(BF16)
        dst[:, 128:256] = pltpu.roll(x0, A_DIM, 1).astype(BF16)
        dst[:, 256:384] = pltpu.roll(x1, A_DIM, 1).astype(BF16)
        dst[:, 384:512] = x1.astype(BF16)

    og_ref[...] = jax.nn.sigmoid(project(wm_ref, 3 * M_WIDTH, 4 * M_WIDTH)) * ogain_ref[...]
    k_m = project(wm_ref, M_WIDTH, 2 * M_WIDTH) * (M_DIM ** -0.5)
    qkv_ref[:, M_WIDTH:2 * M_WIDTH] = k_m.astype(BF16)
    qkv_ref[:, 0:M_WIDTH] = project(wm_ref, 0, M_WIDTH).astype(BF16)


def _front_kernel(*refs, n_tiles, tiles_per_seq, n_cast):
    refs = list(refs)
    take = lambda k: [refs.pop(0) for _ in range(k)]
    x_ref, xs_ref = take(2)
    n1_ref, n2_ref, qgain_ref, kgain_ref, ogain_ref, bias_ref = take(N_FRONT_VECTORS)
    wg_hbm, wu_hbm, wd_hbm, wt_hbm = take(N_FRONT_WEIGHTS)
    cast_src = take(n_cast)
    outs_p = take(N_FRONT_COMMON_OUTS)
    rows_ref, cols_ref, mfin_ref = take(N_PROMPT_GATE_OUTS)
    outs_s = take(N_FRONT_COMMON_OUTS)
    (gts_ref,) = take(1)
    cast_dst = take(n_cast)
    h_ref, act_ref, m_ref, wg_ref, wu_ref, wd_ref, wm_ref, wa_ref, stage_ref, sem_ref = refs
    params = (n1_ref, wg_ref, wu_ref, wd_ref, n2_ref, wm_ref, wa_ref, qgain_ref, kgain_ref, ogain_ref, bias_ref)
    i = pl.program_id(0)

    def prompt_tile(weights_ready):
        for src, dst in zip(cast_src, cast_dst):
            dst[...] = src[...].astype(BF16)
        gates = functools.partial(_prompt_gates, bias_ref=bias_ref, first_of_seq=i % tiles_per_seq == 0,
                                  m_ref=m_ref, rows_ref=rows_ref, cols_ref=cols_ref, mfin_ref=mfin_ref)
        _front_tile(x_ref, params, outs_p, gates, h_ref, act_ref, weights_ready)

    @pl.when(i == 0)
    def _():
        gate_up = [job for pair in zip(_cast_jobs(wg_hbm, wg_ref), _cast_jobs(wu_hbm, wu_ref)) for job in pair]
        down = _cast_jobs(wd_hbm, wd_ref)
        a_src = W_M_ROWS + W_GATE_ROWS
        projection = (_cast_jobs(wt_hbm, wm_ref, rows=W_M_ROWS)
                      + _cast_jobs(wt_hbm, wa_ref, src_row=a_src, rows=WA_GATES)
                      + _cast_jobs(wt_hbm, wa_ref, src_row=W_M_ROWS, dst_row=WA_GATES, rows=W_GATE_ROWS,
                                   dst_rows=GATE_PAD))
        assert len(gate_up) == 2 * N_FF_CHUNKS
        ring = _CastRing(gate_up + down + projection, stage_ref, sem_ref)
        upto = {"gate_up": lambda c: 2 * (c + 1), "down": lambda c: len(gate_up) + len(down),
                "projection": lambda c: len(ring.jobs)}
        prompt_tile(lambda stage, c: ring.finish(upto[stage](c)))

    @pl.when((i > 0) & (i < n_tiles))
    def _():
        prompt_tile(None)

    @pl.when(i == n_tiles)
    def _():
        ns = xs_ref.shape[0]

        def raw_gates(gates_t):
            gts_ref[...] = gates_t

        _front_tile(xs_ref, params, outs_s, raw_gates, h_ref.at[0:ns], act_ref.at[0:ns])


def _const_spec(shape):
    nd = len(shape)
    return pl.BlockSpec(shape, lambda i: (0,) * nd, pipeline_mode=pl.Buffered(1))


def _whole_spec(shape):
    nd = len(shape)
    return pl.BlockSpec(shape, lambda i: (0,) * nd)


def _front_out_shapes(n, n_cache_rows):
    return (
        jax.ShapeDtypeStruct((n, D_MODEL), F32),
        jax.ShapeDtypeStruct((n, 3 * M_WIDTH), BF16),
        jax.ShapeDtypeStruct((n, M_WIDTH), F32),
        jax.ShapeDtypeStruct((n, A_WIDTH), BF16),
        jax.ShapeDtypeStruct((n_cache_rows, KV_WIDTH), F32),
        jax.ShapeDtypeStruct((n_cache_rows, KV_WIDTH), F32),
        jax.ShapeDtypeStruct((n // BLOCK, M_HEADS, M_DIM, BLOCK), BF16),
        jax.ShapeDtypeStruct((n, 4 * KV_WIDTH), BF16),
        jax.ShapeDtypeStruct((n, 4 * KV_WIDTH), BF16),
    )


def _cast_chunk_count(n_rows, max_chunks):
    for k in range(max_chunks, 0, -1):
        if n_rows % k == 0 and (n_rows // k) % 16 == 0:
            return k
    raise ValueError(n_rows)


def _front(x2d, xs2d, vectors, weights, later_weights, tm, seq):
    n, ns = x2d.shape[0], xs2d.shape[0]
    assert seq % tm == 0 and tm >= WINDOW
    assert len(vectors) == N_FRONT_VECTORS and len(weights) == N_FRONT_WEIGHTS
    assert weights[3].shape == (W_M_ROWS + W_GATE_ROWS + WA_GATES, D_MODEL)
    n_tiles = n // tm
    n_seqs = n // seq
    tiles_per_seq = seq // tm
    nb_t = tm // BLOCK
    tile = lambda i: jnp.minimum(i, n_tiles - 1)
    seq_of = lambda i: tile(i) // tiles_per_seq
    row = lambda w: pl.BlockSpec((tm, w), lambda i: (tile(i), 0))
    tail = pl.BlockSpec((WINDOW, KV_WIDTH), lambda i: (seq_of(i), 0))
    prompt_specs = (row(D_MODEL), row(3 * M_WIDTH), row(M_WIDTH), row(A_WIDTH), tail, tail,
                    pl.BlockSpec((nb_t, M_HEADS, M_DIM, BLOCK), lambda i: (tile(i), 0, 0, 0)),
                    row(4 * KV_WIDTH), row(4 * KV_WIDTH),
                    pl.BlockSpec((nb_t, 3, 8, BLOCK), lambda i: (tile(i), 0, 0, 0)), row(128),
                    pl.BlockSpec((1, 8, BLOCK), lambda i: (seq_of(i), 0, 0)))
    prompt_shapes = _front_out_shapes(n, n_seqs * WINDOW) + (
        jax.ShapeDtypeStruct((n // BLOCK, 3, 8, BLOCK), F32),
        jax.ShapeDtypeStruct((n, 128), F32),
        jax.ShapeDtypeStruct((n_seqs, 8, BLOCK), F32),
    )
    sample_shapes = _front_out_shapes(ns, ns) + (
        jax.ShapeDtypeStruct((2 * M_HEADS, ns), F32),
    )

    def chunk_spec(w):
        k = _cast_chunk_count(w.shape[0], n_tiles)
        return pl.BlockSpec((w.shape[0] // k, w.shape[1]), lambda i: (jnp.minimum(i, k - 1), 0))

    cast_specs = [chunk_spec(w) for w in later_weights]
    cast_shapes = tuple(jax.ShapeDtypeStruct(w.shape, BF16) for w in later_weights)
    outs = pl.pallas_call(
        functools.partial(_front_kernel, n_tiles=n_tiles, tiles_per_seq=tiles_per_seq, n_cast=len(later_weights)),
        out_shape=prompt_shapes + sample_shapes + cast_shapes,
        grid=(n_tiles + 1,),
        in_specs=([row(D_MODEL), _whole_spec(xs2d.shape)] + [_const_spec(v.shape) for v in vectors]
                  + [pl.BlockSpec(memory_space=pl.ANY)] * len(weights) + cast_specs),
        out_specs=prompt_specs + tuple(_whole_spec(s.shape) for s in sample_shapes) + tuple(cast_specs),
        scratch_shapes=[pltpu.VMEM((tm, D_MODEL), BF16), pltpu.VMEM((tm, D_FF), BF16),
                        pltpu.VMEM((8, BLOCK), F32),
                        pltpu.VMEM((D_MODEL, D_FF), BF16), pltpu.VMEM((D_MODEL, D_FF), BF16),
                        pltpu.VMEM((D_FF, D_MODEL), BF16),
                        pltpu.VMEM((W_M_ROWS, D_MODEL), BF16), pltpu.VMEM((WA_ROWS, D_MODEL), BF16),
                        pltpu.VMEM((CAST_SLOTS, CAST_ROWS, CAST_COLS), F32),
                        pltpu.SemaphoreType.DMA((CAST_SLOTS,))],
        compiler_params=pltpu.CompilerParams(dimension_semantics=("arbitrary",),
                                             vmem_limit_bytes=FRONT_VMEM_LIMIT_BYTES),
        name="front",
    )(x2d, xs2d, *vectors, *weights, *later_weights)
    n_p, n_s = len(prompt_shapes), len(sample_shapes)
    return outs[:n_p], outs[n_p:n_p + n_s], outs[n_p + n_s:]


def _back_tile(x1_ref, y_ref, params, out_ref, h_ref, act_ref):
    wo_ref, n_ref, wg_ref, wu_ref, wd_ref = params
    x2 = x1_ref[...] + _dot(y_ref[...], wo_ref[...])
    h_ref[...] = _rms_rows(x2, n_ref[...]).astype(BF16)
    out_ref[...] = x2 + FFN_RES_WEIGHT * _ffn(h_ref, wg_ref, wu_ref, wd_ref, act_ref)


def _back_kernel(x1_ref, y_ref, x1s_ref, ys_ref, wo_ref, n_ref, wg_ref, wu_ref, wd_ref, out_ref, outs_ref,
                 h_ref, act_ref, *, n_tiles):
    params = (wo_ref, n_ref, wg_ref, wu_ref, wd_ref)
    i = pl.program_id(0)

    @pl.when(i < n_tiles)
    def _():
        _back_tile(x1_ref, y_ref, params, out_ref, h_ref, act_ref)

    @pl.when(i == n_tiles)
    def _():
        ns = x1s_ref.shape[0]
        _back_tile(x1s_ref, ys_ref, params, outs_ref, h_ref.at[0:ns], act_ref.at[0:ns])


def _back(x1, y, x1s, ys, params, tm):
    n, ns = x1.shape[0], x1s.shape[0]
    n_tiles = n // tm
    row = pl.BlockSpec((tm, D_MODEL), lambda i: (jnp.minimum(i, n_tiles - 1), 0))
    return pl.pallas_call(
        functools.partial(_back_kernel, n_tiles=n_tiles),
        out_shape=(jax.ShapeDtypeStruct((n, D_MODEL), F32), jax.ShapeDtypeStruct((ns, D_MODEL), F32)),
        grid=(n_tiles + 1,),
        in_specs=[row, row, _whole_spec(x1s.shape), _whole_spec(ys.shape)] + [_const_spec(p.shape) for p in params],
        out_specs=(row, _whole_spec((ns, D_MODEL))),
        scratch_shapes=[pltpu.VMEM((tm, D_MODEL), BF16), pltpu.VMEM((tm, D_FF), BF16)],
        compiler_params=pltpu.CompilerParams(dimension_semantics=("arbitrary",),
                                             vmem_limit_bytes=VMEM_LIMIT_BYTES),
        name="back",
    )(x1, y, x1s, ys, *params)


def _prompt_mixer_kernel(sinks_ref, qkv_ref, vt_ref, og_ref, cols_ref, rows_ref, qa_ref,
                         kx_ref, vx_ref, kxp_ref, vxp_ref,
                         y_ref, c_out_ref, n_out_ref,
                         c_ref, n_ref):
    j = pl.program_id(0)
    batch = qkv_ref.shape[0]

    @pl.when(j == 0)
    def _():
        c_ref[...] = jnp.zeros_like(c_ref)
        n_ref[...] = jnp.zeros_like(n_ref)

    rows = lax.broadcasted_iota(jnp.int32, (BLOCK, BLOCK), 0)
    cols = lax.broadcasted_iota(jnp.int32, (BLOCK, BLOCK), 1)
    causal = cols <= rows
    lane_lo = lax.broadcasted_iota(jnp.int32, (2 * BLOCK, KV_WIDTH), 1) < A_DIM
    ones_m = jnp.ones((BLOCK, M_DIM), BF16)
    ones_half = (jnp.where(lane_lo, 1.0, 0.0).astype(BF16), jnp.where(lane_lo, 0.0, 1.0).astype(BF16))
    qi = lax.broadcasted_iota(jnp.int32, (2 * BLOCK, 2 * BLOCK), 0) % BLOCK
    kc = lax.broadcasted_iota(jnp.int32, (2 * BLOCK, 2 * BLOCK), 1)
    first_valid = jnp.where(j == 0, BLOCK, 0)
    valid = (kc >= qi) & (kc <= qi + WINDOW) & (kc >= first_valid)
    top_rows = lax.broadcasted_iota(jnp.int32, (2 * BLOCK, 1), 0) < BLOCK

    heads = range(M_HEADS)
    groups = [(kv, parity) for kv in range(KV_HEADS) for parity in range(2)]
    m_lo = lambda h: h * M_DIM

    def first_matmuls(b):
        st = {"col": cols_ref[b]}
        k = [qkv_ref[b, :, M_WIDTH + m_lo(h):M_WIDTH + m_lo(h) + M_DIM] for h in heads]
        st["c_prev"] = [c_ref[b, h] for h in heads]
        st["n_prev"] = [n_ref[b, h] for h in heads]
        st["qkc"], st["upd"], st["sc"] = [], [], []
        for h in heads:
            q = qkv_ref[b, :, m_lo(h):m_lo(h) + M_DIM]
            n_rep = jnp.broadcast_to(st["n_prev"][h][0:1, :], (BLOCK, M_DIM)).astype(BF16)
            rhs = jnp.concatenate([k[h], st["c_prev"][h].astype(BF16), n_rep], axis=0)
            st["qkc"].append(_dot_nt(q, rhs))
        for h in heads:
            wend_row = rows_ref[b, 0, 1][h:h + 1, :]
            vw_t = (vt_ref[b, 0, h].astype(F32) * wend_row).astype(BF16)
            w_rep = jnp.broadcast_to(wend_row, (16, BLOCK)).astype(BF16)
            st["upd"].append(_dot(jnp.concatenate([vw_t, w_rep], axis=0), k[h]))
        q2 = [jnp.concatenate([qa_ref[b, :, (2 * kv) * 128:(2 * kv + 1) * 128],
                               qa_ref[b, :, (2 * kv + 1) * 128:(2 * kv + 2) * 128]], axis=0)
              for kv in range(KV_HEADS)]
        for kv, parity in groups:
            var = (2 * kv + parity) * KV_WIDTH
            k_band = jnp.concatenate([kxp_ref[b, :, var:var + KV_WIDTH], kx_ref[b, :, var:var + KV_WIDTH]], axis=0)
            st["sc"].append(_dot_nt(q2[kv], k_band))
        return st

    def vector_work(b, st):
        st["s"], st["pr"], st["e_sink"] = [], [], []
        for h in heads:
            g_row = rows_ref[b, 0, 0][h:h + 1, :]
            decay = rows_ref[b, 0, 2][h:h + 1, :]
            d = jnp.where(causal, jnp.exp2(st["col"][:, h:h + 1] + g_row), 0.0)
            st["s"].append((st["qkc"][h][:, 0:BLOCK] * d).astype(BF16))
            c_ref[b, h] = decay * st["c_prev"][h] + st["upd"][h][0:M_DIM]
            n_ref[b, h] = decay * st["n_prev"][h] + st["upd"][h][M_DIM:M_DIM + 8]
        for gi, (kv, parity) in enumerate(groups):
            sink = jnp.where(top_rows, sinks_ref[4 * kv + parity], sinks_ref[4 * kv + 2 + parity]) * LOG2E
            scm = jnp.where(valid, st["sc"][gi], NEG_INF)
            mx = jnp.maximum(jnp.max(scm, axis=-1, keepdims=True), sink)
            st["pr"].append(jnp.exp2(scm - mx).astype(BF16))
            st["e_sink"].append(jnp.exp2(sink - mx))

    def second_matmuls(b, st):
        st["sv"], st["pv"] = [], []
        for h in heads:
            v = qkv_ref[b, :, 2 * M_WIDTH + m_lo(h):2 * M_WIDTH + m_lo(h) + M_DIM]
            st["sv"].append(_dot(st["s"][h], jnp.concatenate([v, ones_m], axis=1)))
        for gi, (kv, parity) in enumerate(groups):
            var = (2 * kv + parity) * KV_WIDTH
            v_band = jnp.concatenate([vxp_ref[b, :, var:var + KV_WIDTH], vx_ref[b, :, var:var + KV_WIDTH]], axis=0)
            st["pv"].append(_dot(st["pr"][gi], jnp.concatenate([v_band, ones_half[parity]], axis=1)))

    def normalise(b, st):
        col_b, qkc, sv, pv, e_sink = st["col"], st["qkc"], st["sv"], st["pv"], st["e_sink"]
        for h in heads:
            lo, hi = m_lo(h), m_lo(h) + M_DIM
            wi_col = col_b[:, 8 + h:9 + h]
            num = wi_col * qkc[h][:, BLOCK:2 * BLOCK] + sv[h][:, 0:M_DIM]
            den = wi_col * qkc[h][:, 2 * BLOCK:3 * BLOCK] + sv[h][:, M_DIM:2 * M_DIM]
            hh = num / jnp.maximum(jnp.abs(den), col_b[:, 16 + h:17 + h])
            hh = hh * lax.rsqrt(jnp.mean(hh * hh, axis=-1, keepdims=True) + RMS_EPS)
            y_ref[b, :, lo:hi] = (hh * og_ref[b, :, lo:hi]).astype(BF16)
        for kv in range(KV_HEADS):
            acc = pv[2 * kv] + pv[2 * kv + 1]
            denom = acc[:, KV_WIDTH:] + jnp.where(lane_lo, e_sink[2 * kv], e_sink[2 * kv + 1])
            ya = (acc[:, 0:KV_WIDTH] / denom).astype(BF16)
            p0, p1 = 2 * kv, 2 * kv + 1
            y_ref[b, :, M_WIDTH + p0 * 128:M_WIDTH + (p0 + 1) * 128] = ya[0:BLOCK]
            y_ref[b, :, M_WIDTH + p1 * 128:M_WIDTH + (p1 + 1) * 128] = ya[BLOCK:2 * BLOCK]

    for b in range(batch):
        st = first_matmuls(b)
        vector_work(b, st)
        second_matmuls(b, st)
        normalise(b, st)

    @pl.when(j == pl.num_programs(0) - 1)
    def _():
        c_out_ref[...] = c_ref[...]
        n_out_ref[...] = n_ref[...]


def _prompt_mixer(sinks, qkv, vt, og, cols, rows, qa, kx, vx, batch, seq):
    nblk = seq // BLOCK
    r3 = lambda a: a.reshape(batch, seq, a.shape[-1])
    cur = lambda w: pl.BlockSpec((batch, BLOCK, w), lambda j: (0, j, 0))
    prev = lambda w: pl.BlockSpec((batch, BLOCK, w), lambda j: (0, jnp.maximum(j - 1, 0), 0))
    state = lambda shape: pl.BlockSpec((batch,) + shape, lambda j: (0,) * (len(shape) + 1))
    out_shape = (
        jax.ShapeDtypeStruct((batch, seq, D_MODEL), BF16),
        jax.ShapeDtypeStruct((batch, M_HEADS, M_DIM, M_DIM), F32),
        jax.ShapeDtypeStruct((batch, M_HEADS, 8, M_DIM), F32),
    )
    y, pc, pn = pl.pallas_call(
        _prompt_mixer_kernel,
        out_shape=out_shape,
        grid=(nblk,),
        in_specs=[pl.BlockSpec(memory_space=pltpu.SMEM),
                  cur(3 * M_WIDTH),
                  pl.BlockSpec((batch, 1, M_HEADS, M_DIM, BLOCK), lambda j: (0, j, 0, 0, 0)),
                  cur(M_WIDTH), cur(128),
                  pl.BlockSpec((batch, 1, 3, 8, BLOCK), lambda j: (0, j, 0, 0, 0)),
                  cur(A_WIDTH), cur(4 * KV_WIDTH), cur(4 * KV_WIDTH), prev(4 * KV_WIDTH), prev(4 * KV_WIDTH)],
        out_specs=(cur(D_MODEL), state((M_HEADS, M_DIM, M_DIM)), state((M_HEADS, 8, M_DIM))),
        scratch_shapes=[pltpu.VMEM((batch, M_HEADS, M_DIM, M_DIM), F32),
                        pltpu.VMEM((batch, M_HEADS, 8, M_DIM), F32)],
        compiler_params=pltpu.CompilerParams(dimension_semantics=("arbitrary",),
                                             vmem_limit_bytes=VMEM_LIMIT_BYTES),
        name="prompt_mixer",
    )(sinks, r3(qkv), vt.reshape(batch, nblk, M_HEADS, M_DIM, BLOCK), r3(og), r3(cols), rows, r3(qa),
      r3(kx), r3(vx), r3(kx), r3(vx))
    return y.reshape(batch * seq, D_MODEL), pc, pn


def _sample_mlstm_kernel(bi_ref, bf_ref, q_ref, k_ref, v_ref, og_ref, gates_ref, m0_ref, n0_ref, c_ref,
                         y_ref, n_out_ref, m_out_ref, c_out_ref, decay_ref, qr_ref):
    h = pl.program_id(0)
    nb = q_ref.shape[0]
    q_rows = q_ref[...].astype(F32)
    k_rows = k_ref[...].astype(F32)
    qr_ref[...] = q_rows
    qt, kt, vt = q_rows.T, k_rows.T, v_ref[...].astype(F32).T
    i_pre = gates_ref[pl.ds(h, 1), :] + bi_ref[h]
    a = _log_sigmoid(gates_ref[pl.ds(M_HEADS + h, 1), :] + bf_ref[h]) + m0_ref[pl.ds(h, 1), :]
    m_t = jnp.maximum(a, i_pre)
    w_inter = jnp.exp(a - m_t)
    w_in = jnp.exp(i_pre - m_t)
    scores = jnp.sum(qt * kt, axis=0, keepdims=True) * w_in
    n0t = n0_ref[...].T
    nq = jnp.sum(n0t * qt, axis=0, keepdims=True)

    rows = lax.broadcasted_iota(jnp.int32, (nb, M_DIM), 0)
    cols = lax.broadcasted_iota(jnp.int32, (M_DIM, nb), 1)
    vw_t = (vt * w_in).astype(BF16)

    decay_ref[...] = jnp.broadcast_to(w_inter, (M_DIM, nb)).T

    def body(grp, cq_t):
        base = grp * SAMPLE_UNROLL
        for u in range(SAMPLE_UNROLL):
            b = base + u
            col = jnp.sum(c_ref[b, 0] * qr_ref[pl.ds(b, 1), :], axis=-1, keepdims=True)
            cq_t = jnp.where(cols == b, col, cq_t)
        outer = []
        for u in range(SAMPLE_UNROLL):
            k_only_b = jnp.where(rows == base + u, k_rows, 0.0).astype(BF16)
            outer.append(_dot(vw_t, k_only_b))
        for u in range(SAMPLE_UNROLL):
            b = base + u
            c_out_ref[b, 0] = decay_ref[pl.ds(b, 1), :] * c_ref[b, 0] + outer[u]
        return cq_t

    cq_t = lax.fori_loop(0, nb // SAMPLE_UNROLL, body, jnp.zeros((M_DIM, nb), F32))

    num = w_inter * cq_t + scores * vt
    den = w_inter * nq + scores
    hh = num / jnp.maximum(jnp.abs(den), jnp.exp(-m_t))
    hh = hh * lax.rsqrt(jnp.mean(hh * hh, axis=0, keepdims=True) + RMS_EPS)
    y_ref[...] = hh.T * og_ref[...]
    n_out_ref[...] = (w_inter * n0t + w_in * kt).T
    m_out_ref[0] = m_t


def _sample_mlstm(b_i, b_f, qkv, og, gates_t, m0_t, n0, c0):
    nb = qkv.shape[0]
    smem = pl.BlockSpec(memory_space=pltpu.SMEM)
    head = lambda off: pl.BlockSpec((nb, M_DIM), lambda h: (0, off + h))
    out_shape = (
        jax.ShapeDtypeStruct((nb, M_WIDTH), F32),
        jax.ShapeDtypeStruct((nb, M_WIDTH), F32),
        jax.ShapeDtypeStruct((M_HEADS, 1, nb), F32),
        jax.ShapeDtypeStruct((nb, M_HEADS, M_DIM, M_DIM), F32),
    )
    c_spec = pl.BlockSpec((nb, 1, M_DIM, M_DIM), lambda h: (0, h, 0, 0))
    return pl.pallas_call(
        _sample_mlstm_kernel,
        out_shape=out_shape,
        grid=(M_HEADS,),
        in_specs=[smem, smem, head(0), head(M_HEADS), head(2 * M_HEADS), head(0),
                  pl.BlockSpec(gates_t.shape, lambda h: (0, 0)), pl.BlockSpec(m0_t.shape, lambda h: (0, 0)),
                  head(0), c_spec],
        out_specs=(head(0), head(0), pl.BlockSpec((1, 1, nb), lambda h: (h, 0, 0)), c_spec),
        scratch_shapes=[pltpu.VMEM((nb, M_DIM), F32), pltpu.VMEM((nb, M_DIM), F32)],
        compiler_params=pltpu.CompilerParams(dimension_semantics=("arbitrary",),
                                             vmem_limit_bytes=VMEM_LIMIT_BYTES),
        name="sample_mlstm",
    )(b_i, b_f, qkv, qkv, qkv, og, gates_t, m0_t, n0, c0)


SAMPLE_TILE = 16
SAMPLE_UNROLL = 8


def _sample_swa_kernel(q2_ref, kc_ref, vc_ref, kn_ref, vn_ref, sink_ref, o_ref, ko_ref, vo_ref):
    sink = sink_ref[...] * LOG2E
    w = kc_ref.shape[2]
    tile = range(SAMPLE_TILE)
    newest = lax.broadcasted_iota(jnp.int32, (KV_WIDTH, w), 1) == w - 1
    pad = jnp.zeros((KV_WIDTH - SAMPLE_TILE, KV_WIDTH), F32)
    kn_t = jnp.concatenate([kn_ref[...], pad], axis=0).T
    vn_t = jnp.concatenate([vn_ref[...], pad], axis=0).T
    q2 = [q2_ref[b] for b in tile]
    k_new = [kn_ref[b:b + 1, :] for b in tile]
    v_new = [vn_ref[b:b + 1, :] for b in tile]
    s_c = [_dot(q2[b], kc_ref[b].astype(BF16)) for b in tile]
    s_n = [jnp.sum(q2[b].astype(F32) * k_new[b], axis=-1, keepdims=True) for b in tile]
    mx = [jnp.maximum(jnp.maximum(jnp.max(s_c[b], axis=-1, keepdims=True), s_n[b]), sink) for b in tile]
    p_c = [jnp.exp2(s_c[b] - mx[b]) for b in tile]
    p_n = [jnp.exp2(s_n[b] - mx[b]) for b in tile]
    denom = [jnp.sum(p_c[b], axis=-1, keepdims=True) + p_n[b] + jnp.exp2(sink - mx[b]) for b in tile]
    o = [_dot_nt(p_c[b].astype(BF16), vc_ref[b].astype(BF16)) for b in tile]
    for b in tile:
        o_ref[b] = (o[b] + p_n[b] * v_new[b]) / denom[b]
        ko_ref[b] = jnp.where(newest, kn_t[:, b:b + 1], pltpu.roll(kc_ref[b], w - 1, 1))
        vo_ref[b] = jnp.where(newest, vn_t[:, b:b + 1], pltpu.roll(vc_ref[b], w - 1, 1))


def _sample_swa(q2, k_cache, v_cache, k_new, v_new, sink_col):
    nb, _, w = k_cache.shape
    t3 = lambda a, c: pl.BlockSpec((SAMPLE_TILE, a, c), lambda i: (i, 0, 0))
    t2 = pl.BlockSpec((SAMPLE_TILE, KV_WIDTH), lambda i: (i, 0))
    out_shape = (
        jax.ShapeDtypeStruct((nb, A_HEADS, KV_WIDTH), F32),
        jax.ShapeDtypeStruct((nb, KV_WIDTH, w), F32),
        jax.ShapeDtypeStruct((nb, KV_WIDTH, w), F32),
    )
    return pl.pallas_call(
        _sample_swa_kernel,
        out_shape=out_shape,
        grid=(nb // SAMPLE_TILE,),
        in_specs=[t3(A_HEADS, KV_WIDTH), t3(KV_WIDTH, w), t3(KV_WIDTH, w), t2, t2,
                  pl.BlockSpec((A_HEADS, 1), lambda i: (0, 0))],
        out_specs=(t3(A_HEADS, KV_WIDTH), t3(KV_WIDTH, w), t3(KV_WIDTH, w)),
        compiler_params=pltpu.CompilerParams(dimension_semantics=("arbitrary",)),
        name="sample_swa",
    )(q2, k_cache, v_cache, k_new, v_new, sink_col)


def kernel(x_prompt, x_sample, cache_swa_k, cache_swa_v, state_mlstm_C, state_mlstm_n, state_mlstm_m,
           ffn1_norm, ffn1_w_gate, ffn1_w_up, ffn1_w_down, mix_norm, w_in, mlstm_b_i, mlstm_b_f,
           mlstm_out_norm, swa_q_norm, swa_k_norm, swa_sinks, w_out, ffn2_norm, ffn2_w_gate,
           ffn2_w_up, ffn2_w_down):
    depth = ffn1_norm.shape[0]
    assert depth == 1
    batch, seq, _ = x_prompt.shape
    nb = x_sample.shape[0]
    assert x_sample.shape[1] == 1 and seq % BLOCK == 0

    front_weights = (ffn1_w_gate[0], ffn1_w_up[0], ffn1_w_down[0], jnp.transpose(w_in[0]))
    n1 = ffn1_norm[0].reshape(1, D_MODEL)
    n2 = mix_norm[0].reshape(1, D_MODEL)
    n3 = ffn2_norm[0].reshape(1, D_MODEL)
    qgain = jnp.tile(swa_q_norm[0], A_HEADS).reshape(1, A_WIDTH)
    kgain = jnp.tile(swa_k_norm[0], KV_HEADS).reshape(1, KV_WIDTH)
    ogain = mlstm_out_norm[0].reshape(1, M_WIDTH)
    b_i, b_f = mlstm_b_i[0], mlstm_b_f[0]
    bias8 = jnp.concatenate([b_i, b_f]).reshape(2 * M_HEADS, 1)
    front_vectors = (n1, n2, qgain, kgain, ogain, bias8)
    sinks = swa_sinks[0]

    xp = x_prompt.reshape(batch * seq, D_MODEL)
    xs = x_sample.reshape(nb, D_MODEL)
    back_weights_f32 = (w_out[0], ffn2_w_gate[0], ffn2_w_up[0], ffn2_w_down[0])
    prompt_front, sample_front, (wo, wg2, wu2, wd2) = _front(xp, xs, front_vectors, front_weights,
                                                             back_weights_f32, tm=512, seq=seq)
    back_params = (wo, n3, wg2, wu2, wd2)
    x1, qkv, og, qa, ka, va, vt, kx, vx, rows, cols, pm = prompt_front
    x1s, qkvs, ogs, qas, kas, vas = sample_front[:6]
    gates_t = sample_front[-1]

    rows = rows.reshape(batch, seq // BLOCK, 3, 8, BLOCK)
    y, pc, pn = _prompt_mixer(sinks, qkv, vt, og, cols, rows, qa, kx, vx, batch, seq)
    pk = ka.reshape(batch, WINDOW, KV_HEADS, A_DIM)
    pv = va.reshape(batch, WINDOW, KV_HEADS, A_DIM)
    pn = pn[:, :, 0, :]
    pm = pm[:, 0:M_HEADS, 0]

    y_m, sn, mt, sc = _sample_mlstm(b_i, b_f, qkvs, ogs, gates_t, jnp.transpose(state_mlstm_m[0]),
                                    state_mlstm_n[0].reshape(nb, M_WIDTH), state_mlstm_C[0])
    sn = sn.reshape(nb, M_HEADS, M_DIM)
    sm = jnp.transpose(mt[:, 0, :])

    qa_h = qas.reshape(nb, A_HEADS, A_DIM)
    zeros = jnp.zeros_like(qa_h)
    in_lo = (jnp.arange(A_HEADS) // A_GROUP == 0)[None, :, None]
    q2 = jnp.concatenate([jnp.where(in_lo, qa_h, zeros), jnp.where(in_lo, zeros, qa_h)], axis=-1)
    wb = cache_swa_k.shape[2]
    to_feature_major = lambda c: jnp.transpose(c[0], (0, 2, 3, 1)).reshape(nb, KV_WIDTH, wb)
    to_window_major = lambda c: jnp.transpose(c.reshape(nb, KV_HEADS, A_DIM, wb), (0, 3, 1, 2))[None]
    kc, vc = to_feature_major(cache_swa_k), to_feature_major(cache_swa_v)
    o2, sk, sv = _sample_swa(q2, kc, vc, kas, vas, sinks.reshape(A_HEADS, 1))
    o2 = o2.reshape(nb, A_HEADS, KV_HEADS, A_DIM)
    y_a = jnp.where(in_lo, o2[:, :, 0, :], o2[:, :, 1, :]).reshape(nb, A_WIDTH)
    ys_in = jnp.concatenate([y_m, y_a], axis=-1).astype(BF16)

    yp, ys = _back(x1, y, x1s, ys_in, back_params, tm=1024)
    yp = yp.reshape(batch, seq, D_MODEL)
    ys = ys.reshape(nb, 1, D_MODEL)

    return (yp, ys, pk[None], pv[None], pc[None], pn[None], pm[None],
            to_window_major(sk), to_window_major(sv), sc[None], sn[None], sm[None])
```

```python
import functools
from typing import NamedTuple

import jax
import jax.numpy as jnp
from jax import lax
from jax.experimental import pallas as pl
from jax.experimental.pallas import tpu as pltpu

F32 = jnp.float32
BF16 = jnp.bfloat16

D_MODEL = 1024
D_FF = 2816
FF_CHUNK = 256
N_FF_CHUNKS = D_FF // FF_CHUNK
M_HEADS = 4
M_DIM = 128
M_WIDTH = M_HEADS * M_DIM
A_HEADS = 8
A_DIM = 64
A_WIDTH = A_HEADS * A_DIM
KV_HEADS = 2
KV_WIDTH = KV_HEADS * A_DIM
A_GROUP = A_HEADS // KV_HEADS
WINDOW = 128
BLOCK = 128
GATE_PAD = 128
RMS_EPS = 1e-6
FFN_RES_WEIGHT = 0.5
NEG_INF = float("-inf")
LOG2E = 1.4426950408889634
VMEM_LIMIT_BYTES = 56 * 1024 * 1024
FRONT_VMEM_LIMIT_BYTES = 60 * 1024 * 1024


def _dot(a, b):
    return jnp.dot(a, b, preferred_element_type=F32)


def _dot_nt(a, b):
    return lax.dot_general(a, b, (((1,), (1,)), ((), ())), preferred_element_type=F32)


def _rms_rows(x, gain):
    ms = jnp.mean(x * x, axis=-1, keepdims=True)
    return x * lax.rsqrt(ms + RMS_EPS) * gain


def _log_sigmoid(x):
    return jnp.minimum(x, 0.0) - jnp.log1p(jnp.exp(-jnp.abs(x)))


def _half_tile_mean_sq(x):
    in_lo = lax.broadcasted_iota(jnp.int32, (x.shape[0], 128), 1) < A_DIM
    out = []
    for c in range(x.shape[1] // 128):
        sq = x[:, c * 128:(c + 1) * 128]
        sq = sq * sq
        s_lo = jnp.sum(jnp.where(in_lo, sq, 0.0), axis=-1, keepdims=True)
        s_hi = jnp.sum(jnp.where(in_lo, 0.0, sq), axis=-1, keepdims=True)
        out.append(jnp.where(in_lo, s_lo, s_hi) * (1.0 / A_DIM))
    return out[0] if len(out) == 1 else jnp.concatenate(out, axis=1)


def _ffn(h_ref, wg_ref, wu_ref, wd_ref, act_ref, weights_ready=None):
    for c in range(N_FF_CHUNKS):
        lo, hi = c * FF_CHUNK, (c + 1) * FF_CHUNK
        if weights_ready is not None:
            weights_ready("gate_up", c)
        h = h_ref[...]
        g = _dot(h, wg_ref[:, lo:hi])
        u = _dot(h, wu_ref[:, lo:hi])
        act_ref[:, lo:hi] = (g * jax.nn.sigmoid(g) * u).astype(BF16)
    if weights_ready is not None:
        weights_ready("down", 0)
    return _dot(act_ref[...], wd_ref[...])


N_FRONT_VECTORS = 6
N_FRONT_WEIGHTS = 4
N_FRONT_COMMON_OUTS = 9
N_PROMPT_GATE_OUTS = 3
W_M_ROWS = 4 * M_WIDTH
W_GATE_ROWS = 2 * M_HEADS
WA_KVG = A_WIDTH
WA_GATES = WA_KVG + 2 * KV_WIDTH
WA_ROWS = WA_GATES + GATE_PAD
CAST_ROWS, CAST_COLS = D_MODEL, FF_CHUNK
CAST_SLOTS = 6


class _CastJob(NamedTuple):
    src: object
    src_row: int
    dst: object
    dst_row: int
    col: int
    rows: int
    cols: int
    dst_rows: int


def _cast_jobs(src, dst, src_row=0, dst_row=0, rows=None, dst_rows=None):
    rows = src.shape[0] - src_row if rows is None else rows
    jobs = []
    for r in range(0, rows, CAST_ROWS):
        nr = min(CAST_ROWS, rows - r)
        for c in range(0, src.shape[1], CAST_COLS):
            nc = min(CAST_COLS, src.shape[1] - c)
            jobs.append(_CastJob(src, src_row + r, dst, dst_row + r, c, nr, nc, nr if dst_rows is None else dst_rows))
    return jobs


class _CastRing:
    def __init__(self, jobs, stage_ref, sem_ref):
        self.jobs, self.stage_ref, self.sem_ref = jobs, stage_ref, sem_ref
        self.ahead = CAST_SLOTS - 1
        self.done = 0
        for j in range(min(self.ahead, len(jobs))):
            self._copy(j).start()

    def _copy(self, j):
        job, slot = self.jobs[j], j % CAST_SLOTS
        return pltpu.make_async_copy(job.src.at[pl.ds(job.src_row, job.rows), pl.ds(job.col, job.cols)],
                                     self.stage_ref.at[slot, pl.ds(0, job.rows), pl.ds(0, job.cols)],
                                     self.sem_ref.at[slot])

    def finish(self, upto):
        for j in range(self.done, upto):
            job = self.jobs[j]
            if j + self.ahead < len(self.jobs):
                self._copy(j + self.ahead).start()
            self._copy(j).wait()
            block = self.stage_ref[j % CAST_SLOTS, 0:job.rows, 0:job.cols]
            if job.dst_rows > job.rows:
                block = jnp.concatenate([block, jnp.zeros((job.dst_rows - job.rows, job.cols), F32)], axis=0)
            job.dst[job.dst_row:job.dst_row + job.dst_rows, job.col:job.col + job.cols] = block.astype(BF16)
        self.done = max(self.done, upto)


def _prompt_gates(gates_t, bias_ref, first_of_seq, m_ref, rows_ref, cols_ref, mfin_ref):
    sub = lax.broadcasted_iota(jnp.int32, (8, BLOCK), 0)
    lane = lax.broadcasted_iota(jnp.int32, (8, BLOCK), 1)
    is_head = sub < M_HEADS
    pad = jnp.zeros((BLOCK - 24, BLOCK), F32)
    scan_shifts = (1, 2, 4, 8, 16, 32, 64)
    m_prev = jnp.where(first_of_seq, 0.0, m_ref[...])[:, 0:1]
    for c in range(rows_ref.shape[0]):
        pre = gates_t[:, c * BLOCK:(c + 1) * BLOCK] + bias_ref[...]
        r = jnp.where(is_head, pre, _log_sigmoid(pre))
        cum = r
        for shift in scan_shifts:
            cum = cum + jnp.where(lane >= shift, pltpu.roll(cum, shift, 1), 0.0)
        bcum = pltpu.roll(cum, M_HEADS, 0)
        g = jnp.where(is_head, r - bcum, 0.0)
        bcum = jnp.where(is_head, bcum, 0.0)
        cm = g
        for shift in scan_shifts:
            cm = jnp.maximum(cm, jnp.where(lane >= shift, pltpu.roll(cm, shift, 1), NEG_INF))
        cm_last = jnp.max(cm, axis=-1, keepdims=True)
        b_last = jnp.sum(jnp.where(lane == BLOCK - 1, bcum, 0.0), axis=-1, keepdims=True)
        mx = jnp.maximum(m_prev, cm)
        mx_last = jnp.maximum(m_prev, cm_last)
        rows_ref[c, 0] = g * LOG2E
        rows_ref[c, 1] = jnp.exp(g - mx_last)
        rows_ref[c, 2] = jnp.broadcast_to(jnp.exp(m_prev - mx_last), (8, BLOCK))
        col_src = jnp.concatenate([mx * -LOG2E, jnp.exp(m_prev - mx), jnp.exp(-(bcum + mx)), pad], axis=0)
        cols_ref[c * BLOCK:(c + 1) * BLOCK, :] = col_src.T
        m_prev = b_last + mx_last
    m_full = jnp.broadcast_to(m_prev, (8, BLOCK))
    m_ref[...] = m_full
    mfin_ref[0] = m_full


def _front_tile(x_ref, params, outs, gate_sink, h_ref, act_ref, weights_ready=None):
    (n1_ref, wg_ref, wu_ref, wd_ref, n2_ref, wm_ref, wa_ref, qgain_ref, kgain_ref, ogain_ref, _) = params
    (x1_ref, qkv_ref, og_ref, qa_ref, ka_ref, va_ref, vt_ref, kx_ref, vx_ref) = outs
    x = x_ref[...]
    h_ref[...] = _rms_rows(x, n1_ref[...]).astype(BF16)
    x1 = x + FFN_RES_WEIGHT * _ffn(h_ref, wg_ref, wu_ref, wd_ref, act_ref, weights_ready)
    x1_ref[...] = x1
    h_ref[...] = _rms_rows(x1, n2_ref[...]).astype(BF16)
    if weights_ready is not None:
        weights_ready("projection", 0)
    h = h_ref[...]
    project = lambda w_ref, lo, hi: _dot_nt(h, w_ref[lo:hi, :])

    qa = project(wa_ref, 0, WA_KVG)
    kvg = project(wa_ref, WA_KVG, WA_ROWS)
    v_m = project(wm_ref, 2 * M_WIDTH, 3 * M_WIDTH)
    qkv_ref[:, 2 * M_WIDTH:3 * M_WIDTH] = v_m.astype(BF16)
    for c in range(vt_ref.shape[0]):
        for hd in range(M_HEADS):
            blk = v_m[c * BLOCK:(c + 1) * BLOCK, hd * M_DIM:(hd + 1) * M_DIM]
            vt_ref[c, hd] = blk.T.astype(BF16)
    q_scale = qgain_ref[...] * (A_DIM ** -0.5 * LOG2E)
    qa_ref[...] = (qa * lax.rsqrt(_half_tile_mean_sq(qa) + RMS_EPS) * q_scale).astype(BF16)
    ka = kvg[:, 0:KV_WIDTH]
    ka = ka * lax.rsqrt(_half_tile_mean_sq(ka) + RMS_EPS) * kgain_ref[...]
    va = kvg[:, KV_WIDTH:2 * KV_WIDTH]
    keep = ka_ref.shape[0]
    ka_ref[...] = ka[ka.shape[0] - keep:, :]
    va_ref[...] = va[va.shape[0] - keep:, :]
    gate_sink(kvg[:, 2 * KV_WIDTH:2 * KV_WIDTH + GATE_PAD].T[0:2 * M_HEADS, :])

    in_lo = lax.broadcasted_iota(jnp.int32, ka.shape, 1) < A_DIM
    for src, dst in ((ka, kx_ref), (va, vx_ref)):
        x0 = jnp.where(in_lo, src, 0.0)
        x1 = jnp.where(in_lo, 0.0, src)
        dst[:, 0:128] = x0.astype(BF16)
        dst[:, 128:256] = pltpu.roll(x0, A_DIM, 1).astype(BF16)
        dst[:, 256:384] = pltpu.roll(x1, A_DIM, 1).astype(BF16)
        dst[:, 384:512] = x1.astype(BF16)

    og_ref[...] = jax.nn.sigmoid(project(wm_ref, 3 * M_WIDTH, 4 * M_WIDTH)) * ogain_ref[...]
    k_m = project(wm_ref, M_WIDTH, 2 * M_WIDTH) * (M_DIM ** -0.5)
    qkv_ref[:, M_WIDTH:2 * M_WIDTH] = k_m.astype(BF16)
    qkv_ref[:, 0:M_WIDTH] = project(wm_ref, 0, M_WIDTH).astype(BF16)


def _front_kernel(*refs, n_tiles, tiles_per_seq, n_cast):
    refs = list(refs)
    take = lambda k: [refs.pop(0) for _ in range(k)]
    x_ref, xs_ref = take(2)
    n1_ref, n2_ref, qgain_ref, kgain_ref, ogain_ref, bias_ref = take(N_FRONT_VECTORS)
    wg_hbm, wu_hbm, wd_hbm, wt_hbm = take(N_FRONT_WEIGHTS)
    cast_src = take(n_cast)
    outs_p = take(N_FRONT_COMMON_OUTS)
    rows_ref, cols_ref, mfin_ref = take(N_PROMPT_GATE_OUTS)
    outs_s = take(N_FRONT_COMMON_OUTS)
    (gts_ref,) = take(1)
    cast_dst = take(n_cast)
    h_ref, act_ref, m_ref, wg_ref, wu_ref, wd_ref, wm_ref, wa_ref, stage_ref, sem_ref = refs
    params = (n1_ref, wg_ref, wu_ref, wd_ref, n2_ref, wm_ref, wa_ref, qgain_ref, kgain_ref, ogain_ref, bias_ref)
    i = pl.program_id(0)

    def prompt_tile(weights_ready):
        for src, dst in zip(cast_src, cast_dst):
            dst[...] = src[...].astype(BF16)
        gates = functools.partial(_prompt_gates, bias_ref=bias_ref, first_of_seq=i % tiles_per_seq == 0,
                                  m_ref=m_ref, rows_ref=rows_ref, cols_ref=cols_ref, mfin_ref=mfin_ref)
        _front_tile(x_ref, params, outs_p, gates, h_ref, act_ref, weights_ready)

    @pl.when(i == 0)
    def _():
        gate_up = [job for pair in zip(_cast_jobs(wg_hbm, wg_ref), _cast_jobs(wu_hbm, wu_ref)) for job in pair]
        down = _cast_jobs(wd_hbm, wd_ref)
        a_src = W_M_ROWS + W_GATE_ROWS
        projection = (_cast_jobs(wt_hbm, wm_ref, rows=W_M_ROWS)
                      + _cast_jobs(wt_hbm, wa_ref, src_row=a_src, rows=WA_GATES)
                      + _cast_jobs(wt_hbm, wa_ref, src_row=W_M_ROWS, dst_row=WA_GATES, rows=W_GATE_ROWS,
                                   dst_rows=GATE_PAD))
        assert len(gate_up) == 2 * N_FF_CHUNKS
        ring = _CastRing(gate_up + down + projection, stage_ref, sem_ref)
        upto = {"gate_up": lambda c: 2 * (c + 1), "down": lambda c: len(gate_up) + len(down),
                "projection": lambda c: len(ring.jobs)}
        prompt_tile(lambda stage, c: ring.finish(upto[stage](c)))

    @pl.when((i > 0) & (i < n_tiles))
    def _():
        prompt_tile(None)

    @pl.when(i == n_tiles)
    def _():
        ns = xs_ref.shape[0]

        def raw_gates(gates_t):
            gts_ref[...] = gates_t

        _front_tile(xs_ref, params, outs_s, raw_gates, h_ref.at[0:ns], act_ref.at[0:ns])


def _const_spec(shape):
    nd = len(shape)
    return pl.BlockSpec(shape, lambda i: (0,) * nd, pipeline_mode=pl.Buffered(1))


def _whole_spec(shape):
    nd = len(shape)
    return pl.BlockSpec(shape, lambda i: (0,) * nd)


def _front_out_shapes(n, n_cache_rows):
    return (
        jax.ShapeDtypeStruct((n, D_MODEL), F32),
        jax.ShapeDtypeStruct((n, 3 * M_WIDTH), BF16),
        jax.ShapeDtypeStruct((n, M_WIDTH), F32),
        jax.ShapeDtypeStruct((n, A_WIDTH), BF16),
        jax.ShapeDtypeStruct((n_cache_rows, KV_WIDTH), F32),
        jax.ShapeDtypeStruct((n_cache_rows, KV_WIDTH), F32),
        jax.ShapeDtypeStruct((n // BLOCK, M_HEADS, M_DIM, BLOCK), BF16),
        jax.ShapeDtypeStruct((n, 4 * KV_WIDTH), BF16),
        jax.ShapeDtypeStruct((n, 4 * KV_WIDTH), BF16),
    )


def _cast_chunk_count(n_rows, max_chunks):
    for k in range(max_chunks, 0, -1):
        if n_rows % k == 0 and (n_rows // k) % 16 == 0:
            return k
    raise ValueError(n_rows)


def _front(x2d, xs2d, vectors, weights, later_weights, tm, seq):
    n, ns = x2d.shape[0], xs2d.shape[0]
    assert seq % tm == 0 and tm >= WINDOW
    assert len(vectors) == N_FRONT_VECTORS and len(weights) == N_FRONT_WEIGHTS
    assert weights[3].shape == (W_M_ROWS + W_GATE_ROWS + WA_GATES, D_MODEL)
    n_tiles = n // tm
    n_seqs = n // seq
    tiles_per_seq = seq // tm
    nb_t = tm // BLOCK
    tile = lambda i: jnp.minimum(i, n_tiles - 1)
    seq_of = lambda i: tile(i) // tiles_per_seq
    row = lambda w: pl.BlockSpec((tm, w), lambda i: (tile(i), 0))
    tail = pl.BlockSpec((WINDOW, KV_WIDTH), lambda i: (seq_of(i), 0))
    prompt_specs = (row(D_MODEL), row(3 * M_WIDTH), row(M_WIDTH), row(A_WIDTH), tail, tail,
                    pl.BlockSpec((nb_t, M_HEADS, M_DIM, BLOCK), lambda i: (tile(i), 0, 0, 0)),
                    row(4 * KV_WIDTH), row(4 * KV_WIDTH),
                    pl.BlockSpec((nb_t, 3, 8, BLOCK), lambda i: (tile(i), 0, 0, 0)), row(128),
                    pl.BlockSpec((1, 8, BLOCK), lambda i: (seq_of(i), 0, 0)))
    prompt_shapes = _front_out_shapes(n, n_seqs * WINDOW) + (
        jax.ShapeDtypeStruct((n // BLOCK, 3, 8, BLOCK), F32),
        jax.ShapeDtypeStruct((n, 128), F32),
        jax.ShapeDtypeStruct((n_seqs, 8, BLOCK), F32),
    )
    sample_shapes = _front_out_shapes(ns, ns) + (
        jax.ShapeDtypeStruct((2 * M_HEADS, ns), F32),
    )

    def chunk_spec(w):
        k = _cast_chunk_count(w.shape[0], n_tiles)
        return pl.BlockSpec((w.shape[0] // k, w.shape[1]), lambda i: (jnp.minimum(i, k - 1), 0))

    cast_specs = [chunk_spec(w) for w in later_weights]
    cast_shapes = tuple(jax.ShapeDtypeStruct(w.shape, BF16) for w in later_weights)
    outs = pl.pallas_call(
        functools.partial(_front_kernel, n_tiles=n_tiles, tiles_per_seq=tiles_per_seq, n_cast=len(later_weights)),
        out_shape=prompt_shapes + sample_shapes + cast_shapes,
        grid=(n_tiles + 1,),
        in_specs=([row(D_MODEL), _whole_spec(xs2d.shape)] + [_const_spec(v.shape) for v in vectors]
                  + [pl.BlockSpec(memory_space=pl.ANY)] * len(weights) + cast_specs),
        out_specs=prompt_specs + tuple(_whole_spec(s.shape) for s in sample_shapes) + tuple(cast_specs),
        scratch_shapes=[pltpu.VMEM((tm, D_MODEL), BF16), pltpu.VMEM((tm, D_FF), BF16),
                        pltpu.VMEM((8, BLOCK), F32),
                        pltpu.VMEM((D_MODEL, D_FF), BF16), pltpu.VMEM((D_MODEL, D_FF), BF16),
                        pltpu.VMEM((D_FF, D_MODEL), BF16),
                        pltpu.VMEM((W_M_ROWS, D_MODEL), BF16), pltpu.VMEM((WA_ROWS, D_MODEL), BF16),
                        pltpu.VMEM((CAST_SLOTS, CAST_ROWS, CAST_COLS), F32),
                        pltpu.SemaphoreType.DMA((CAST_SLOTS,))],
        compiler_params=pltpu.CompilerParams(dimension_semantics=("arbitrary",),
                                             vmem_limit_bytes=FRONT_VMEM_LIMIT_BYTES),
        name="front",
    )(x2d, xs2d, *vectors, *weights, *later_weights)
    n_p, n_s = len(prompt_shapes), len(sample_shapes)
    return outs[:n_p], outs[n_p:n_p + n_s], outs[n_p + n_s:]


def _back_tile(x1_ref, y_ref, params, out_ref, h_ref, act_ref):
    wo_ref, n_ref, wg_ref, wu_ref, wd_ref = params
    x2 = x1_ref[...] + _dot(y_ref[...], wo_ref[...])
    h_ref[...] = _rms_rows(x2, n_ref[...]).astype(BF16)
    out_ref[...] = x2 + FFN_RES_WEIGHT * _ffn(h_ref, wg_ref, wu_ref, wd_ref, act_ref)


def _back_kernel(x1_ref, y_ref, x1s_ref, ys_ref, wo_ref, n_ref, wg_ref, wu_ref, wd_ref, out_ref, outs_ref,
                 h_ref, act_ref, *, n_tiles):
    params = (wo_ref, n_ref, wg_ref, wu_ref, wd_ref)
    i = pl.program_id(0)

    @pl.when(i < n_tiles)
    def _():
        _back_tile(x1_ref, y_ref, params, out_ref, h_ref, act_ref)

    @pl.when(i == n_tiles)
    def _():
        ns = x1s_ref.shape[0]
        _back_tile(x1s_ref, ys_ref, params, outs_ref.at[:, 0], h_ref.at[0:ns], act_ref.at[0:ns])


def _back(x1, y, x1s, ys, params, tm):
    n, ns = x1.shape[0], x1s.shape[0]
    n_tiles = n // tm
    row = pl.BlockSpec((tm, D_MODEL), lambda i: (jnp.minimum(i, n_tiles - 1), 0))
    return pl.pallas_call(
        functools.partial(_back_kernel, n_tiles=n_tiles),
        out_shape=(jax.ShapeDtypeStruct((n, D_MODEL), F32), jax.ShapeDtypeStruct((ns, 1, D_MODEL), F32)),
        grid=(n_tiles + 1,),
        in_specs=[row, row, _whole_spec(x1s.shape), _whole_spec(ys.shape)] + [_const_spec(p.shape) for p in params],
        out_specs=(row, _whole_spec((ns, 1, D_MODEL))),
        scratch_shapes=[pltpu.VMEM((tm, D_MODEL), BF16), pltpu.VMEM((tm, D_FF), BF16)],
        compiler_params=pltpu.CompilerParams(dimension_semantics=("arbitrary",),
                                             vmem_limit_bytes=VMEM_LIMIT_BYTES),
        name="back",
    )(x1, y, x1s, ys, *params)


def _prompt_mixer_kernel(sinks_ref, qkv_ref, vt_ref, og_ref, cols_ref, rows_ref, qa_ref,
                         kx_ref, vx_ref, kxp_ref, vxp_ref,
                         y_ref, c_out_ref, n_out_ref,
                         c_ref, n_ref):
    j = pl.program_id(0)
    batch = qkv_ref.shape[0]

    @pl.when(j == 0)
    def _():
        c_ref[...] = jnp.zeros_like(c_ref)
        n_ref[...] = jnp.zeros_like(n_ref)

    rows = lax.broadcasted_iota(jnp.int32, (BLOCK, BLOCK), 0)
    cols = lax.broadcasted_iota(jnp.int32, (BLOCK, BLOCK), 1)
    causal = cols <= rows
    lane_lo = lax.broadcasted_iota(jnp.int32, (2 * BLOCK, KV_WIDTH), 1) < A_DIM
    ones_m = jnp.ones((BLOCK, M_DIM), BF16)
    ones_half = (jnp.where(lane_lo, 1.0, 0.0).astype(BF16), jnp.where(lane_lo, 0.0, 1.0).astype(BF16))
    qi = lax.broadcasted_iota(jnp.int32, (2 * BLOCK, 2 * BLOCK), 0) % BLOCK
    kc = lax.broadcasted_iota(jnp.int32, (2 * BLOCK, 2 * BLOCK), 1)
    first_valid = jnp.where(j == 0, BLOCK, 0)
    valid = (kc >= qi) & (kc <= qi + WINDOW) & (kc >= first_valid)
    top_rows = lax.broadcasted_iota(jnp.int32, (2 * BLOCK, 1), 0) < BLOCK

    heads = range(M_HEADS)
    groups = [(kv, parity) for kv in range(KV_HEADS) for parity in range(2)]
    m_lo = lambda h: h * M_DIM

    def first_matmuls(b):
        st = {"col": cols_ref[b]}
        k = [qkv_ref[b, :, M_WIDTH + m_lo(h):M_WIDTH + m_lo(h) + M_DIM] for h in heads]
        st["c_prev"] = [c_ref[b, h] for h in heads]
        st["n_prev"] = [n_ref[b, h] for h in heads]
        st["qkc"], st["upd"], st["sc"] = [], [], []
        for h in heads:
            q = qkv_ref[b, :, m_lo(h):m_lo(h) + M_DIM]
            n_rep = jnp.broadcast_to(st["n_prev"][h][0:1, :], (BLOCK, M_DIM)).astype(BF16)
            rhs = jnp.concatenate([k[h], st["c_prev"][h].astype(BF16), n_rep], axis=0)
            st["qkc"].append(_dot_nt(q, rhs))
        for h in heads:
            wend_row = rows_ref[b, 0, 1][h:h + 1, :]
            vw_t = (vt_ref[b, 0, h].astype(F32) * wend_row).astype(BF16)
            w_rep = jnp.broadcast_to(wend_row, (16, BLOCK)).astype(BF16)
            st["upd"].append(_dot(jnp.concatenate([vw_t, w_rep], axis=0), k[h]))
        q2 = [jnp.concatenate([qa_ref[b, :, (2 * kv) * 128:(2 * kv + 1) * 128],
                               qa_ref[b, :, (2 * kv + 1) * 128:(2 * kv + 2) * 128]], axis=0)
              for kv in range(KV_HEADS)]
        for kv, parity in groups:
            var = (2 * kv + parity) * KV_WIDTH
            k_band = jnp.concatenate([kxp_ref[b, :, var:var + KV_WIDTH], kx_ref[b, :, var:var + KV_WIDTH]], axis=0)
            st["sc"].append(_dot_nt(q2[kv], k_band))
        return st

    def vector_work(b, st):
        st["s"], st["pr"], st["e_sink"] = [], [], []
        for h in heads:
            g_row = rows_ref[b, 0, 0][h:h + 1, :]
            decay = rows_ref[b, 0, 2][h:h + 1, :]
            d = jnp.where(causal, jnp.exp2(st["col"][:, h:h + 1] + g_row), 0.0)
            st["s"].append((st["qkc"][h][:, 0:BLOCK] * d).astype(BF16))
            c_ref[b, h] = decay * st["c_prev"][h] + st["upd"][h][0:M_DIM]
            n_ref[b, h] = decay * st["n_prev"][h] + st["upd"][h][M_DIM:M_DIM + 8]
        for gi, (kv, parity) in enumerate(groups):
            sink = jnp.where(top_rows, sinks_ref[4 * kv + parity], sinks_ref[4 * kv + 2 + parity]) * LOG2E
            scm = jnp.where(valid, st["sc"][gi], NEG_INF)
            mx = jnp.maximum(jnp.max(scm, axis=-1, keepdims=True), sink)
            st["pr"].append(jnp.exp2(scm - mx).astype(BF16))
            st["e_sink"].append(jnp.exp2(sink - mx))

    def second_matmuls(b, st):
        st["sv"], st["pv"] = [], []
        for h in heads:
            v = qkv_ref[b, :, 2 * M_WIDTH + m_lo(h):2 * M_WIDTH + m_lo(h) + M_DIM]
            st["sv"].append(_dot(st["s"][h], jnp.concatenate([v, ones_m], axis=1)))
        for gi, (kv, parity) in enumerate(groups):
            var = (2 * kv + parity) * KV_WIDTH
            v_band = jnp.concatenate([vxp_ref[b, :, var:var + KV_WIDTH], vx_ref[b, :, var:var + KV_WIDTH]], axis=0)
            st["pv"].append(_dot(st["pr"][gi], jnp.concatenate([v_band, ones_half[parity]], axis=1)))

    def normalise(b, st):
        col_b, qkc, sv, pv, e_sink = st["col"], st["qkc"], st["sv"], st["pv"], st["e_sink"]
        for h in heads:
            lo, hi = m_lo(h), m_lo(h) + M_DIM
            wi_col = col_b[:, 8 + h:9 + h]
            num = wi_col * qkc[h][:, BLOCK:2 * BLOCK] + sv[h][:, 0:M_DIM]
            den = wi_col * qkc[h][:, 2 * BLOCK:3 * BLOCK] + sv[h][:, M_DIM:2 * M_DIM]
            hh = num / jnp.maximum(jnp.abs(den), col_b[:, 16 + h:17 + h])
            hh = hh * lax.rsqrt(jnp.mean(hh * hh, axis=-1, keepdims=True) + RMS_EPS)
            y_ref[b, :, lo:hi] = (hh * og_ref[b, :, lo:hi]).astype(BF16)
        for kv in range(KV_HEADS):
            acc = pv[2 * kv] + pv[2 * kv + 1]
            denom = acc[:, KV_WIDTH:] + jnp.where(lane_lo, e_sink[2 * kv], e_sink[2 * kv + 1])
            ya = (acc[:, 0:KV_WIDTH] / denom).astype(BF16)
            p0, p1 = 2 * kv, 2 * kv + 1
            y_ref[b, :, M_WIDTH + p0 * 128:M_WIDTH + (p0 + 1) * 128] = ya[0:BLOCK]
            y_ref[b, :, M_WIDTH + p1 * 128:M_WIDTH + (p1 + 1) * 128] = ya[BLOCK:2 * BLOCK]

    for b in range(batch):
        st = first_matmuls(b)
        vector_work(b, st)
        second_matmuls(b, st)
        normalise(b, st)

    @pl.when(j == pl.num_programs(0) - 1)
    def _():
        c_out_ref[...] = c_ref[...]
        n_out_ref[...] = n_ref[:, :, 0:1, :]


def _prompt_mixer(sinks, qkv, vt, og, cols, rows, qa, kx, vx, batch, seq):
    nblk = seq // BLOCK
    r3 = lambda a: a.reshape(batch, seq, a.shape[-1])
    cur = lambda w: pl.BlockSpec((batch, BLOCK, w), lambda j: (0, j, 0))
    prev = lambda w: pl.BlockSpec((batch, BLOCK, w), lambda j: (0, jnp.maximum(j - 1, 0), 0))
    state = lambda shape: pl.BlockSpec((batch,) + shape, lambda j: (0,) * (len(shape) + 1))
    out_shape = (
        jax.ShapeDtypeStruct((batch, seq, D_MODEL), BF16),
        jax.ShapeDtypeStruct((batch, M_HEADS, M_DIM, M_DIM), F32),
        jax.ShapeDtypeStruct((batch, M_HEADS, 1, M_DIM), F32),
    )
    y, pc, pn = pl.pallas_call(
        _prompt_mixer_kernel,
        out_shape=out_shape,
        grid=(nblk,),
        in_specs=[pl.BlockSpec(memory_space=pltpu.SMEM),
                  cur(3 * M_WIDTH),
                  pl.BlockSpec((batch, 1, M_HEADS, M_DIM, BLOCK), lambda j: (0, j, 0, 0, 0)),
                  cur(M_WIDTH), cur(128),
                  pl.BlockSpec((batch, 1, 3, 8, BLOCK), lambda j: (0, j, 0, 0, 0)),
                  cur(A_WIDTH), cur(4 * KV_WIDTH), cur(4 * KV_WIDTH), prev(4 * KV_WIDTH), prev(4 * KV_WIDTH)],
        out_specs=(cur(D_MODEL), state((M_HEADS, M_DIM, M_DIM)), state((M_HEADS, 1, M_DIM))),
        scratch_shapes=[pltpu.VMEM((batch, M_HEADS, M_DIM, M_DIM), F32),
                        pltpu.VMEM((batch, M_HEADS, 8, M_DIM), F32)],
        compiler_params=pltpu.CompilerParams(dimension_semantics=("arbitrary",),
                                             vmem_limit_bytes=VMEM_LIMIT_BYTES),
        name="prompt_mixer",
    )(sinks, r3(qkv), vt.reshape(batch, nblk, M_HEADS, M_DIM, BLOCK), r3(og), r3(cols), rows, r3(qa),
      r3(kx), r3(vx), r3(kx), r3(vx))
    return y.reshape(batch * seq, D_MODEL), pc, pn


def _sample_mlstm_kernel(bi_ref, bf_ref, q_ref, k_ref, v_ref, og_ref, gates_ref, m0_ref, n0_ref, c_ref,
                         y_ref, n_out_ref, m_out_ref, c_out_ref, decay_ref, qr_ref):
    h = pl.program_id(0)
    nb = q_ref.shape[0]
    q_rows = q_ref[...].astype(F32)
    k_rows = k_ref[...].astype(F32)
    qr_ref[...] = q_rows
    qt, kt, vt = q_rows.T, k_rows.T, v_ref[...].astype(F32).T
    i_pre = gates_ref[pl.ds(h, 1), :] + bi_ref[h]
    a = _log_sigmoid(gates_ref[pl.ds(M_HEADS + h, 1), :] + bf_ref[h]) + m0_ref[pl.ds(h, 1), :]
    m_t = jnp.maximum(a, i_pre)
    w_inter = jnp.exp(a - m_t)
    w_in = jnp.exp(i_pre - m_t)
    scores = jnp.sum(qt * kt, axis=0, keepdims=True) * w_in
    n0t = n0_ref[...].T
    nq = jnp.sum(n0t * qt, axis=0, keepdims=True)

    rows = lax.broadcasted_iota(jnp.int32, (nb, M_DIM), 0)
    cols = lax.broadcasted_iota(jnp.int32, (M_DIM, nb), 1)
    vw_t = (vt * w_in).astype(BF16)

    decay_ref[...] = jnp.broadcast_to(w_inter, (M_DIM, nb)).T

    def body(grp, cq_t):
        base = grp * SAMPLE_UNROLL
        for u in range(SAMPLE_UNROLL):
            b = base + u
            col = jnp.sum(c_ref[b, 0] * qr_ref[pl.ds(b, 1), :], axis=-1, keepdims=True)
            cq_t = jnp.where(cols == b, col, cq_t)
        outer = []
        for u in range(SAMPLE_UNROLL):
            k_only_b = jnp.where(rows == base + u, k_rows, 0.0).astype(BF16)
            outer.append(_dot(vw_t, k_only_b))
        for u in range(SAMPLE_UNROLL):
            b = base + u
            c_out_ref[b, 0] = decay_ref[pl.ds(b, 1), :] * c_ref[b, 0] + outer[u]
        return cq_t

    cq_t = lax.fori_loop(0, nb // SAMPLE_UNROLL, body, jnp.zeros((M_DIM, nb), F32))

    num = w_inter * cq_t + scores * vt
    den = w_inter * nq + scores
    hh = num / jnp.maximum(jnp.abs(den), jnp.exp(-m_t))
    hh = hh * lax.rsqrt(jnp.mean(hh * hh, axis=0, keepdims=True) + RMS_EPS)
    y_ref[...] = hh.T * og_ref[...]
    n_out_ref[...] = (w_inter * n0t + w_in * kt).T
    m_out_ref[0] = m_t


def _sample_mlstm(b_i, b_f, qkv, og, gates_t, m0_t, n0, c0):
    nb = qkv.shape[0]
    smem = pl.BlockSpec(memory_space=pltpu.SMEM)
    head = lambda off: pl.BlockSpec((nb, M_DIM), lambda h: (0, off + h))
    out_shape = (
        jax.ShapeDtypeStruct((nb, M_WIDTH), F32),
        jax.ShapeDtypeStruct((nb, M_WIDTH), F32),
        jax.ShapeDtypeStruct((M_HEADS, 1, nb), F32),
        jax.ShapeDtypeStruct((nb, M_HEADS, M_DIM, M_DIM), F32),
    )
    c_spec = pl.BlockSpec((nb, 1, M_DIM, M_DIM), lambda h: (0, h, 0, 0))
    return pl.pallas_call(
        _sample_mlstm_kernel,
        out_shape=out_shape,
        grid=(M_HEADS,),
        in_specs=[smem, smem, head(0), head(M_HEADS), head(2 * M_HEADS), head(0),
                  pl.BlockSpec(gates_t.shape, lambda h: (0, 0)), pl.BlockSpec(m0_t.shape, lambda h: (0, 0)),
                  head(0), c_spec],
        out_specs=(head(0), head(0), pl.BlockSpec((1, 1, nb), lambda h: (h, 0, 0)), c_spec),
        scratch_shapes=[pltpu.VMEM((nb, M_DIM), F32), pltpu.VMEM((nb, M_DIM), F32)],
        compiler_params=pltpu.CompilerParams(dimension_semantics=("arbitrary",),
                                             vmem_limit_bytes=VMEM_LIMIT_BYTES),
        name="sample_mlstm",
    )(b_i, b_f, qkv, qkv, qkv, og, gates_t, m0_t, n0, c0)


SAMPLE_TILE = 16
SAMPLE_UNROLL = 8


def _sample_swa_kernel(q2_ref, kc_ref, vc_ref, kn_ref, vn_ref, sink_ref, o_ref, ko_ref, vo_ref):
    sink = sink_ref[...] * LOG2E
    w = kc_ref.shape[2]
    tile = range(SAMPLE_TILE)
    newest = lax.broadcasted_iota(jnp.int32, (KV_WIDTH, w), 1) == w - 1
    pad = jnp.zeros((KV_WIDTH - SAMPLE_TILE, KV_WIDTH), F32)
    kn_t = jnp.concatenate([kn_ref[...], pad], axis=0).T
    vn_t = jnp.concatenate([vn_ref[...], pad], axis=0).T
    q2 = [q2_ref[b] for b in tile]
    k_new = [kn_ref[b:b + 1, :] for b in tile]
    v_new = [vn_ref[b:b + 1, :] for b in tile]
    s_c = [_dot(q2[b], kc_ref[b].astype(BF16)) for b in tile]
    s_n = [jnp.sum(q2[b].astype(F32) * k_new[b], axis=-1, keepdims=True) for b in tile]
    mx = [jnp.maximum(jnp.maximum(jnp.max(s_c[b], axis=-1, keepdims=True), s_n[b]), sink) for b in tile]
    p_c = [jnp.exp2(s_c[b] - mx[b]) for b in tile]
    p_n = [jnp.exp2(s_n[b] - mx[b]) for b in tile]
    denom = [jnp.sum(p_c[b], axis=-1, keepdims=True) + p_n[b] + jnp.exp2(sink - mx[b]) for b in tile]
    o = [_dot_nt(p_c[b].astype(BF16), vc_ref[b].astype(BF16)) for b in tile]
    for b in tile:
        o_ref[b] = (o[b] + p_n[b] * v_new[b]) / denom[b]
        ko_ref[b] = jnp.where(newest, kn_t[:, b:b + 1], pltpu.roll(kc_ref[b], w - 1, 1))
        vo_ref[b] = jnp.where(newest, vn_t[:, b:b + 1], pltpu.roll(vc_ref[b], w - 1, 1))


def _sample_swa(q2, k_cache, v_cache, k_new, v_new, sink_col):
    nb, _, w = k_cache.shape
    t3 = lambda a, c: pl.BlockSpec((SAMPLE_TILE, a, c), lambda i: (i, 0, 0))
    t2 = pl.BlockSpec((SAMPLE_TILE, KV_WIDTH), lambda i: (i, 0))
    out_shape = (
        jax.ShapeDtypeStruct((nb, A_HEADS, KV_WIDTH), F32),
        jax.ShapeDtypeStruct((nb, KV_WIDTH, w), F32),
        jax.ShapeDtypeStruct((nb, KV_WIDTH, w), F32),
    )
    return pl.pallas_call(
        _sample_swa_kernel,
        out_shape=out_shape,
        grid=(nb // SAMPLE_TILE,),
        in_specs=[t3(A_HEADS, KV_WIDTH), t3(KV_WIDTH, w), t3(KV_WIDTH, w), t2, t2,
                  pl.BlockSpec((A_HEADS, 1), lambda i: (0, 0))],
        out_specs=(t3(A_HEADS, KV_WIDTH), t3(KV_WIDTH, w), t3(KV_WIDTH, w)),
        compiler_params=pltpu.CompilerParams(dimension_semantics=("arbitrary",)),
        name="sample_swa",
    )(q2, k_cache, v_cache, k_new, v_new, sink_col)


def kernel(x_prompt, x_sample, cache_swa_k, cache_swa_v, state_mlstm_C, state_mlstm_n, state_mlstm_m,
           ffn1_norm, ffn1_w_gate, ffn1_w_up, ffn1_w_down, mix_norm, w_in, mlstm_b_i, mlstm_b_f,
           mlstm_out_norm, swa_q_norm, swa_k_norm, swa_sinks, w_out, ffn2_norm, ffn2_w_gate,
           ffn2_w_up, ffn2_w_down):
    depth = ffn1_norm.shape[0]
    assert depth == 1
    batch, seq, _ = x_prompt.shape
    nb = x_sample.shape[0]
    assert x_sample.shape[1] == 1 and seq % BLOCK == 0

    front_weights = (ffn1_w_gate[0], ffn1_w_up[0], ffn1_w_down[0], jnp.transpose(w_in[0]))
    n1 = ffn1_norm[0].reshape(1, D_MODEL)
    n2 = mix_norm[0].reshape(1, D_MODEL)
    n3 = ffn2_norm[0].reshape(1, D_MODEL)
    qgain = jnp.tile(swa_q_norm[0], A_HEADS).reshape(1, A_WIDTH)
    kgain = jnp.tile(swa_k_norm[0], KV_HEADS).reshape(1, KV_WIDTH)
    ogain = mlstm_out_norm[0].reshape(1, M_WIDTH)
    b_i, b_f = mlstm_b_i[0], mlstm_b_f[0]
    bias8 = jnp.concatenate([b_i, b_f]).reshape(2 * M_HEADS, 1)
    front_vectors = (n1, n2, qgain, kgain, ogain, bias8)
    sinks = swa_sinks[0]

    xp = x_prompt.reshape(batch * seq, D_MODEL)
    xs = x_sample.reshape(nb, D_MODEL)
    back_weights_f32 = (w_out[0], ffn2_w_gate[0], ffn2_w_up[0], ffn2_w_down[0])
    prompt_front, sample_front, (wo, wg2, wu2, wd2) = _front(xp, xs, front_vectors, front_weights,
                                                             back_weights_f32, tm=512, seq=seq)
    back_params = (wo, n3, wg2, wu2, wd2)
    x1, qkv, og, qa, ka, va, vt, kx, vx, rows, cols, pm = prompt_front
    x1s, qkvs, ogs, qas, kas, vas = sample_front[:6]
    gates_t = sample_front[-1]

    rows = rows.reshape(batch, seq // BLOCK, 3, 8, BLOCK)
    y, pc, pn = _prompt_mixer(sinks, qkv, vt, og, cols, rows, qa, kx, vx, batch, seq)
    pk = ka.reshape(batch, WINDOW, KV_HEADS, A_DIM)
    pv = va.reshape(batch, WINDOW, KV_HEADS, A_DIM)
    pn = pn[:, :, 0, :]
    pm = pm[:, 0:M_HEADS, 0]

    y_m, sn, mt, sc = _sample_mlstm(b_i, b_f, qkvs, ogs, gates_t, jnp.transpose(state_mlstm_m[0]),
                                    state_mlstm_n[0].reshape(nb, M_WIDTH), state_mlstm_C[0])
    sn = sn.reshape(nb, M_HEADS, M_DIM)
    sm = jnp.transpose(mt[:, 0, :])

    qa_h = qas.reshape(nb, A_HEADS, A_DIM)
    zeros = jnp.zeros_like(qa_h)
    in_lo = (jnp.arange(A_HEADS) // A_GROUP == 0)[None, :, None]
    q2 = jnp.concatenate([jnp.where(in_lo, qa_h, zeros), jnp.where(in_lo, zeros, qa_h)], axis=-1)
    wb = cache_swa_k.shape[2]
    to_feature_major = lambda c: jnp.transpose(c[0], (0, 2, 3, 1)).reshape(nb, KV_WIDTH, wb)
    to_window_major = lambda c: jnp.transpose(c.reshape(nb, KV_HEADS, A_DIM, wb), (0, 3, 1, 2))[None]
    kc, vc = to_feature_major(cache_swa_k), to_feature_major(cache_swa_v)
    o2, sk, sv = _sample_swa(q2, kc, vc, kas, vas, sinks.reshape(A_HEADS, 1))
    o2 = o2.reshape(nb, A_HEADS, KV_HEADS, A_DIM)
    y_a = jnp.where(in_lo, o2[:, :, 0, :], o2[:, :, 1, :]).reshape(nb, A_WIDTH)
    ys_in = jnp.concatenate([y_m, y_a], axis=-1).astype(BF16)

    yp, ys = _back(x1, y, x1s, ys_in, back_params, tm=1024)
    yp = yp.reshape(batch, seq, D_MODEL)

    return (yp, ys, pk[None], pv[None], pc[None], pn[None], pm[None],
            to_window_major(sk), to_window_major(sv), sc[None], sn[None], sm[None])
```

```python
import functools
from typing import NamedTuple

import jax
import jax.numpy as jnp
from jax import lax
from jax.experimental import pallas as pl
from jax.experimental.pallas import tpu as pltpu

F32 = jnp.float32
BF16 = jnp.bfloat16

D_MODEL = 1024
D_FF = 2816
FF_CHUNK = 256
N_FF_CHUNKS = D_FF // FF_CHUNK
M_HEADS = 4
M_DIM = 128
M_WIDTH = M_HEADS * M_DIM
A_HEADS = 8
A_DIM = 64
A_WIDTH = A_HEADS * A_DIM
KV_HEADS = 2
KV_WIDTH = KV_HEADS * A_DIM
A_GROUP = A_HEADS // KV_HEADS
WINDOW = 128
BLOCK = 128
GATE_PAD = 128
RMS_EPS = 1e-6
FFN_RES_WEIGHT = 0.5
NEG_INF = float("-inf")
LOG2E = 1.4426950408889634
VMEM_LIMIT_BYTES = 56 * 1024 * 1024
FRONT_VMEM_LIMIT_BYTES = 60 * 1024 * 1024


def _dot(a, b):
    return jnp.dot(a, b, preferred_element_type=F32)


def _dot_nt(a, b):
    return lax.dot_general(a, b, (((1,), (1,)), ((), ())), preferred_element_type=F32)


def _rms_rows(x, gain):
    ms = jnp.mean(x * x, axis=-1, keepdims=True)
    return x * lax.rsqrt(ms + RMS_EPS) * gain


def _log_sigmoid(x):
    return jnp.minimum(x, 0.0) - jnp.log1p(jnp.exp(-jnp.abs(x)))


def _half_tile_mean_sq(x):
    in_lo = lax.broadcasted_iota(jnp.int32, (x.shape[0], 128), 1) < A_DIM
    out = []
    for c in range(x.shape[1] // 128):
        sq = x[:, c * 128:(c + 1) * 128]
        sq = sq * sq
        s_lo = jnp.sum(jnp.where(in_lo, sq, 0.0), axis=-1, keepdims=True)
        s_hi = jnp.sum(jnp.where(in_lo, 0.0, sq), axis=-1, keepdims=True)
        out.append(jnp.where(in_lo, s_lo, s_hi) * (1.0 / A_DIM))
    return out[0] if len(out) == 1 else jnp.concatenate(out, axis=1)


def _ffn(h_ref, wg_ref, wu_ref, wd_ref, act_ref, weights_ready=None):
    for c in range(N_FF_CHUNKS):
        lo, hi = c * FF_CHUNK, (c + 1) * FF_CHUNK
        if weights_ready is not None:
            weights_ready("gate_up", c)
        h = h_ref[...]
        g = _dot(h, wg_ref[:, lo:hi])
        u = _dot(h, wu_ref[:, lo:hi])
        act_ref[:, lo:hi] = (g * jax.nn.sigmoid(g) * u).astype(BF16)
    if weights_ready is not None:
        weights_ready("down", 0)
    return _dot(act_ref[...], wd_ref[...])


N_FRONT_VECTORS = 6
N_FRONT_WEIGHTS = 4
N_FRONT_COMMON_OUTS = 9
N_PROMPT_GATE_OUTS = 3
W_M_ROWS = 4 * M_WIDTH
W_GATE_ROWS = 2 * M_HEADS
WA_KVG = A_WIDTH
WA_GATES = WA_KVG + 2 * KV_WIDTH
WA_ROWS = WA_GATES + GATE_PAD
CAST_ROWS, CAST_COLS = D_MODEL, FF_CHUNK
CAST_SLOTS = 6


class _CastJob(NamedTuple):
    src: object
    src_row: int
    dst: object
    dst_row: int
    col: int
    rows: int
    cols: int
    dst_rows: int


def _cast_jobs(src, dst, src_row=0, dst_row=0, rows=None, dst_rows=None):
    rows = src.shape[0] - src_row if rows is None else rows
    jobs = []
    for r in range(0, rows, CAST_ROWS):
        nr = min(CAST_ROWS, rows - r)
        for c in range(0, src.shape[1], CAST_COLS):
            nc = min(CAST_COLS, src.shape[1] - c)
            jobs.append(_CastJob(src, src_row + r, dst, dst_row + r, c, nr, nc, nr if dst_rows is None else dst_rows))
    return jobs


class _CastRing:
    def __init__(self, jobs, stage_ref, sem_ref):
        self.jobs, self.stage_ref, self.sem_ref = jobs, stage_ref, sem_ref
        self.ahead = CAST_SLOTS - 1
        self.done = 0
        for j in range(min(self.ahead, len(jobs))):
            self._copy(j).start()

    def _copy(self, j):
        job, slot = self.jobs[j], j % CAST_SLOTS
        return pltpu.make_async_copy(job.src.at[pl.ds(job.src_row, job.rows), pl.ds(job.col, job.cols)],
                                     self.stage_ref.at[slot, pl.ds(0, job.rows), pl.ds(0, job.cols)],
                                     self.sem_ref.at[slot])

    def finish(self, upto):
        for j in range(self.done, upto):
            job = self.jobs[j]
            if j + self.ahead < len(self.jobs):
                self._copy(j + self.ahead).start()
            self._copy(j).wait()
            block = self.stage_ref[j % CAST_SLOTS, 0:job.rows, 0:job.cols]
            if job.dst_rows > job.rows:
                block = jnp.concatenate([block, jnp.zeros((job.dst_rows - job.rows, job.cols), F32)], axis=0)
            job.dst[job.dst_row:job.dst_row + job.dst_rows, job.col:job.col + job.cols] = block.astype(BF16)
        self.done = max(self.done, upto)


def _prompt_gates(gates_t, bias_ref, first_of_seq, m_ref, rows_ref, cols_ref, mfin_ref):
    sub = lax.broadcasted_iota(jnp.int32, (8, BLOCK), 0)
    lane = lax.broadcasted_iota(jnp.int32, (8, BLOCK), 1)
    is_head = sub < M_HEADS
    pad = jnp.zeros((BLOCK - 24, BLOCK), F32)
    scan_shifts = (1, 2, 4, 8, 16, 32, 64)
    m_prev = jnp.where(first_of_seq, 0.0, m_ref[...])[:, 0:1]
    for c in range(rows_ref.shape[0]):
        pre = gates_t[:, c * BLOCK:(c + 1) * BLOCK] + bias_ref[...]
        r = jnp.where(is_head, pre, _log_sigmoid(pre))
        cum = r
        for shift in scan_shifts:
            cum = cum + jnp.where(lane >= shift, pltpu.roll(cum, shift, 1), 0.0)
        bcum = pltpu.roll(cum, M_HEADS, 0)
        g = jnp.where(is_head, r - bcum, 0.0)
        bcum = jnp.where(is_head, bcum, 0.0)
        cm = g
        for shift in scan_shifts:
            cm = jnp.maximum(cm, jnp.where(lane >= shift, pltpu.roll(cm, shift, 1), NEG_INF))
        cm_last = jnp.max(cm, axis=-1, keepdims=True)
        b_last = jnp.sum(jnp.where(lane == BLOCK - 1, bcum, 0.0), axis=-1, keepdims=True)
        mx = jnp.maximum(m_prev, cm)
        mx_last = jnp.maximum(m_prev, cm_last)
        rows_ref[c, 0] = g * LOG2E
        rows_ref[c, 1] = jnp.exp(g - mx_last)
        rows_ref[c, 2] = jnp.broadcast_to(jnp.exp(m_prev - mx_last), (8, BLOCK))
        col_src = jnp.concatenate([mx * -LOG2E, jnp.exp(m_prev - mx), jnp.exp(-(bcum + mx)), pad], axis=0)
        cols_ref[c * BLOCK:(c + 1) * BLOCK, :] = col_src.T
        m_prev = b_last + mx_last
    m_full = jnp.broadcast_to(m_prev, (8, BLOCK))
    m_ref[...] = m_full
    mfin_ref[0] = m_full


def _front_tile(x_ref, params, outs, gate_sink, h_ref, act_ref, cache_feature_major, weights_ready=None):
    (n1_ref, wg_ref, wu_ref, wd_ref, n2_ref, wm_ref, wa_ref, qgain_ref, kgain_ref, ogain_ref, _) = params
    (x1_ref, qkv_ref, og_ref, qa_ref, ka_ref, va_ref, vt_ref, kx_ref, vx_ref) = outs
    x = x_ref[...]
    h_ref[...] = _rms_rows(x, n1_ref[...]).astype(BF16)
    x1 = x + FFN_RES_WEIGHT * _ffn(h_ref, wg_ref, wu_ref, wd_ref, act_ref, weights_ready)
    x1_ref[...] = x1
    h_ref[...] = _rms_rows(x1, n2_ref[...]).astype(BF16)
    if weights_ready is not None:
        weights_ready("projection", 0)
    h = h_ref[...]
    project = lambda w_ref, lo, hi: _dot_nt(h, w_ref[lo:hi, :])

    qa = project(wa_ref, 0, WA_KVG)
    kvg = project(wa_ref, WA_KVG, WA_ROWS)
    v_m = project(wm_ref, 2 * M_WIDTH, 3 * M_WIDTH)
    qkv_ref[:, 2 * M_WIDTH:3 * M_WIDTH] = v_m.astype(BF16)
    for c in range(vt_ref.shape[0]):
        for hd in range(M_HEADS):
            blk = v_m[c * BLOCK:(c + 1) * BLOCK, hd * M_DIM:(hd + 1) * M_DIM]
            vt_ref[c, hd] = blk.T.astype(BF16)
    q_scale = qgain_ref[...] * (A_DIM ** -0.5 * LOG2E)
    qa_ref[...] = (qa * lax.rsqrt(_half_tile_mean_sq(qa) + RMS_EPS) * q_scale).astype(BF16)
    ka = kvg[:, 0:KV_WIDTH]
    ka = ka * lax.rsqrt(_half_tile_mean_sq(ka) + RMS_EPS) * kgain_ref[...]
    va = kvg[:, KV_WIDTH:2 * KV_WIDTH]
    for src, dst in ((ka, ka_ref), (va, va_ref)):
        if cache_feature_major:
            dst[...] = src[src.shape[0] - dst.shape[1]:, :].T
        else:
            dst[...] = src[src.shape[0] - dst.shape[0]:, :]
    gate_sink(kvg[:, 2 * KV_WIDTH:2 * KV_WIDTH + GATE_PAD].T[0:2 * M_HEADS, :])

    in_lo = lax.broadcasted_iota(jnp.int32, ka.shape, 1) < A_DIM
    for src, dst in ((ka, kx_ref), (va, vx_ref)):
        x0 = jnp.where(in_lo, src, 0.0)
        x1 = jnp.where(in_lo, 0.0, src)
        dst[:, 0:128] = x0.astype(BF16)
        dst[:, 128:256] = pltpu.roll(x0, A_DIM, 1).astype(BF16)
        dst[:, 256:384] = pltpu.roll(x1, A_DIM, 1).astype(BF16)
        dst[:, 384:512] = x1.astype(BF16)

    og_ref[...] = jax.nn.sigmoid(project(wm_ref, 3 * M_WIDTH, 4 * M_WIDTH)) * ogain_ref[...]
    k_m = project(wm_ref, M_WIDTH, 2 * M_WIDTH) * (M_DIM ** -0.5)
    qkv_ref[:, M_WIDTH:2 * M_WIDTH] = k_m.astype(BF16)
    qkv_ref[:, 0:M_WIDTH] = project(wm_ref, 0, M_WIDTH).astype(BF16)


def _front_kernel(*refs, n_tiles, tiles_per_seq, n_cast):
    refs = list(refs)
    take = lambda k: [refs.pop(0) for _ in range(k)]
    x_ref, xs_ref = take(2)
    n1_ref, n2_ref, qgain_ref, kgain_ref, ogain_ref, bias_ref = take(N_FRONT_VECTORS)
    wg_hbm, wu_hbm, wd_hbm, wt_hbm = take(N_FRONT_WEIGHTS)
    cast_src = take(n_cast)
    outs_p = take(N_FRONT_COMMON_OUTS)
    rows_ref, cols_ref, mfin_ref = take(N_PROMPT_GATE_OUTS)
    outs_s = take(N_FRONT_COMMON_OUTS)
    (gts_ref,) = take(1)
    cast_dst = take(n_cast)
    h_ref, act_ref, m_ref, wg_ref, wu_ref, wd_ref, wm_ref, wa_ref, stage_ref, sem_ref = refs
    params = (n1_ref, wg_ref, wu_ref, wd_ref, n2_ref, wm_ref, wa_ref, qgain_ref, kgain_ref, ogain_ref, bias_ref)
    i = pl.program_id(0)

    def prompt_tile(weights_ready):
        for src, dst in zip(cast_src, cast_dst):
            dst[...] = src[...].astype(BF16)
        gates = functools.partial(_prompt_gates, bias_ref=bias_ref, first_of_seq=i % tiles_per_seq == 0,
                                  m_ref=m_ref, rows_ref=rows_ref, cols_ref=cols_ref, mfin_ref=mfin_ref)
        _front_tile(x_ref, params, outs_p, gates, h_ref, act_ref, True, weights_ready)

    @pl.when(i == 0)
    def _():
        gate_up = [job for pair in zip(_cast_jobs(wg_hbm, wg_ref), _cast_jobs(wu_hbm, wu_ref)) for job in pair]
        down = _cast_jobs(wd_hbm, wd_ref)
        a_src = W_M_ROWS + W_GATE_ROWS
        projection = (_cast_jobs(wt_hbm, wm_ref, rows=W_M_ROWS)
                      + _cast_jobs(wt_hbm, wa_ref, src_row=a_src, rows=WA_GATES)
                      + _cast_jobs(wt_hbm, wa_ref, src_row=W_M_ROWS, dst_row=WA_GATES, rows=W_GATE_ROWS,
                                   dst_rows=GATE_PAD))
        assert len(gate_up) == 2 * N_FF_CHUNKS
        ring = _CastRing(gate_up + down + projection, stage_ref, sem_ref)
        upto = {"gate_up": lambda c: 2 * (c + 1), "down": lambda c: len(gate_up) + len(down),
                "projection": lambda c: len(ring.jobs)}
        prompt_tile(lambda stage, c: ring.finish(upto[stage](c)))

    @pl.when((i > 0) & (i < n_tiles))
    def _():
        prompt_tile(None)

    @pl.when(i == n_tiles)
    def _():
        ns = xs_ref.shape[0]

        def raw_gates(gates_t):
            gts_ref[...] = gates_t

        _front_tile(xs_ref, params, outs_s, raw_gates, h_ref.at[0:ns], act_ref.at[0:ns], False)


def _const_spec(shape):
    nd = len(shape)
    return pl.BlockSpec(shape, lambda i: (0,) * nd, pipeline_mode=pl.Buffered(1))


def _whole_spec(shape):
    nd = len(shape)
    return pl.BlockSpec(shape, lambda i: (0,) * nd)


def _front_out_shapes(n, cache_shape):
    return (
        jax.ShapeDtypeStruct((n, D_MODEL), F32),
        jax.ShapeDtypeStruct((n, 3 * M_WIDTH), BF16),
        jax.ShapeDtypeStruct((n, M_WIDTH), F32),
        jax.ShapeDtypeStruct((n, A_WIDTH), BF16),
        jax.ShapeDtypeStruct(cache_shape, F32),
        jax.ShapeDtypeStruct(cache_shape, F32),
        jax.ShapeDtypeStruct((n // BLOCK, M_HEADS, M_DIM, BLOCK), BF16),
        jax.ShapeDtypeStruct((n, 4 * KV_WIDTH), BF16),
        jax.ShapeDtypeStruct((n, 4 * KV_WIDTH), BF16),
    )


def _cast_chunk_count(n_rows, max_chunks):
    for k in range(max_chunks, 0, -1):
        if n_rows % k == 0 and (n_rows // k) % 16 == 0:
            return k
    raise ValueError(n_rows)


def _front(x2d, xs2d, vectors, weights, later_weights, tm, seq):
    n, ns = x2d.shape[0], xs2d.shape[0]
    assert seq % tm == 0 and tm >= WINDOW
    assert len(vectors) == N_FRONT_VECTORS and len(weights) == N_FRONT_WEIGHTS
    assert weights[3].shape == (W_M_ROWS + W_GATE_ROWS + WA_GATES, D_MODEL)
    n_tiles = n // tm
    n_seqs = n // seq
    tiles_per_seq = seq // tm
    nb_t = tm // BLOCK
    tile = lambda i: jnp.minimum(i, n_tiles - 1)
    seq_of = lambda i: tile(i) // tiles_per_seq
    row = lambda w: pl.BlockSpec((tm, w), lambda i: (tile(i), 0))
    tail = pl.BlockSpec((KV_WIDTH, WINDOW), lambda i: (seq_of(i), 0))
    prompt_specs = (row(D_MODEL), row(3 * M_WIDTH), row(M_WIDTH), row(A_WIDTH), tail, tail,
                    pl.BlockSpec((nb_t, M_HEADS, M_DIM, BLOCK), lambda i: (tile(i), 0, 0, 0)),
                    row(4 * KV_WIDTH), row(4 * KV_WIDTH),
                    pl.BlockSpec((nb_t, 3, 8, BLOCK), lambda i: (tile(i), 0, 0, 0)), row(128),
                    pl.BlockSpec((1, 8, BLOCK), lambda i: (seq_of(i), 0, 0)))
    prompt_shapes = _front_out_shapes(n, (n_seqs * KV_WIDTH, WINDOW)) + (
        jax.ShapeDtypeStruct((n // BLOCK, 3, 8, BLOCK), F32),
        jax.ShapeDtypeStruct((n, 128), F32),
        jax.ShapeDtypeStruct((n_seqs, 8, BLOCK), F32),
    )
    sample_shapes = _front_out_shapes(ns, (ns, KV_WIDTH)) + (
        jax.ShapeDtypeStruct((2 * M_HEADS, ns), F32),
    )

    def chunk_spec(w):
        k = _cast_chunk_count(w.shape[0], n_tiles)
        return pl.BlockSpec((w.shape[0] // k, w.shape[1]), lambda i: (jnp.minimum(i, k - 1), 0))

    cast_specs = [chunk_spec(w) for w in later_weights]
    cast_shapes = tuple(jax.ShapeDtypeStruct(w.shape, BF16) for w in later_weights)
    outs = pl.pallas_call(
        functools.partial(_front_kernel, n_tiles=n_tiles, tiles_per_seq=tiles_per_seq, n_cast=len(later_weights)),
        out_shape=prompt_shapes + sample_shapes + cast_shapes,
        grid=(n_tiles + 1,),
        in_specs=([row(D_MODEL), _whole_spec(xs2d.shape)] + [_const_spec(v.shape) for v in vectors]
                  + [pl.BlockSpec(memory_space=pl.ANY)] * len(weights) + cast_specs),
        out_specs=prompt_specs + tuple(_whole_spec(s.shape) for s in sample_shapes) + tuple(cast_specs),
        scratch_shapes=[pltpu.VMEM((tm, D_MODEL), BF16), pltpu.VMEM((tm, D_FF), BF16),
                        pltpu.VMEM((8, BLOCK), F32),
                        pltpu.VMEM((D_MODEL, D_FF), BF16), pltpu.VMEM((D_MODEL, D_FF), BF16),
                        pltpu.VMEM((D_FF, D_MODEL), BF16),
                        pltpu.VMEM((W_M_ROWS, D_MODEL), BF16), pltpu.VMEM((WA_ROWS, D_MODEL), BF16),
                        pltpu.VMEM((CAST_SLOTS, CAST_ROWS, CAST_COLS), F32),
                        pltpu.SemaphoreType.DMA((CAST_SLOTS,))],
        compiler_params=pltpu.CompilerParams(dimension_semantics=("arbitrary",),
                                             vmem_limit_bytes=FRONT_VMEM_LIMIT_BYTES),
        name="front",
    )(x2d, xs2d, *vectors, *weights, *later_weights)
    n_p, n_s = len(prompt_shapes), len(sample_shapes)
    return outs[:n_p], outs[n_p:n_p + n_s], outs[n_p + n_s:]


def _back_tile(x1_ref, y_ref, params, out_ref, h_ref, act_ref):
    wo_ref, n_ref, wg_ref, wu_ref, wd_ref = params
    x2 = x1_ref[...] + _dot(y_ref[...], wo_ref[...])
    h_ref[...] = _rms_rows(x2, n_ref[...]).astype(BF16)
    out_ref[...] = x2 + FFN_RES_WEIGHT * _ffn(h_ref, wg_ref, wu_ref, wd_ref, act_ref)


def _back_kernel(x1_ref, y_ref, x1s_ref, ys_ref, wo_ref, n_ref, wg_ref, wu_ref, wd_ref, out_ref, outs_ref,
                 h_ref, act_ref, *, n_tiles):
    params = (wo_ref, n_ref, wg_ref, wu_ref, wd_ref)
    i = pl.program_id(0)

    @pl.when(i < n_tiles)
    def _():
        _back_tile(x1_ref, y_ref, params, out_ref, h_ref, act_ref)

    @pl.when(i == n_tiles)
    def _():
        ns = x1s_ref.shape[0]
        _back_tile(x1s_ref, ys_ref, params, outs_ref.at[:, 0], h_ref.at[0:ns], act_ref.at[0:ns])


def _back(x1, y, x1s, ys, params, tm):
    n, ns = x1.shape[0], x1s.shape[0]
    n_tiles = n // tm
    row = pl.BlockSpec((tm, D_MODEL), lambda i: (jnp.minimum(i, n_tiles - 1), 0))
    return pl.pallas_call(
        functools.partial(_back_kernel, n_tiles=n_tiles),
        out_shape=(jax.ShapeDtypeStruct((n, D_MODEL), F32), jax.ShapeDtypeStruct((ns, 1, D_MODEL), F32)),
        grid=(n_tiles + 1,),
        in_specs=[row, row, _whole_spec(x1s.shape), _whole_spec(ys.shape)] + [_const_spec(p.shape) for p in params],
        out_specs=(row, _whole_spec((ns, 1, D_MODEL))),
        scratch_shapes=[pltpu.VMEM((tm, D_MODEL), BF16), pltpu.VMEM((tm, D_FF), BF16)],
        compiler_params=pltpu.CompilerParams(dimension_semantics=("arbitrary",),
                                             vmem_limit_bytes=VMEM_LIMIT_BYTES),
        name="back",
    )(x1, y, x1s, ys, *params)


def _prompt_mixer_kernel(sinks_ref, qkv_ref, vt_ref, og_ref, cols_ref, rows_ref, qa_ref,
                         kx_ref, vx_ref, kxp_ref, vxp_ref,
                         y_ref, c_out_ref, n_out_ref,
                         c_ref, n_ref):
    j = pl.program_id(0)
    batch = qkv_ref.shape[0]

    @pl.when(j == 0)
    def _():
        c_ref[...] = jnp.zeros_like(c_ref)
        n_ref[...] = jnp.zeros_like(n_ref)

    rows = lax.broadcasted_iota(jnp.int32, (BLOCK, BLOCK), 0)
    cols = lax.broadcasted_iota(jnp.int32, (BLOCK, BLOCK), 1)
    causal = cols <= rows
    lane_lo = lax.broadcasted_iota(jnp.int32, (2 * BLOCK, KV_WIDTH), 1) < A_DIM
    ones_m = jnp.ones((BLOCK, M_DIM), BF16)
    ones_half = (jnp.where(lane_lo, 1.0, 0.0).astype(BF16), jnp.where(lane_lo, 0.0, 1.0).astype(BF16))
    qi = lax.broadcasted_iota(jnp.int32, (2 * BLOCK, 2 * BLOCK), 0) % BLOCK
    kc = lax.broadcasted_iota(jnp.int32, (2 * BLOCK, 2 * BLOCK), 1)
    first_valid = jnp.where(j == 0, BLOCK, 0)
    valid = (kc >= qi) & (kc <= qi + WINDOW) & (kc >= first_valid)
    top_rows = lax.broadcasted_iota(jnp.int32, (2 * BLOCK, 1), 0) < BLOCK

    heads = range(M_HEADS)
    groups = [(kv, parity) for kv in range(KV_HEADS) for parity in range(2)]
    m_lo = lambda h: h * M_DIM

    def first_matmuls(b):
        st = {"col": cols_ref[b]}
        k = [qkv_ref[b, :, M_WIDTH + m_lo(h):M_WIDTH + m_lo(h) + M_DIM] for h in heads]
        st["c_prev"] = [c_ref[b, h] for h in heads]
        st["n_prev"] = [n_ref[b, h] for h in heads]
        st["qkc"], st["upd"], st["sc"] = [], [], []
        for h in heads:
            q = qkv_ref[b, :, m_lo(h):m_lo(h) + M_DIM]
            n_rep = jnp.broadcast_to(st["n_prev"][h][0:1, :], (BLOCK, M_DIM)).astype(BF16)
            rhs = jnp.concatenate([k[h], st["c_prev"][h].astype(BF16), n_rep], axis=0)
            st["qkc"].append(_dot_nt(q, rhs))
        for h in heads:
            wend_row = rows_ref[b, 0, 1][h:h + 1, :]
            vw_t = (vt_ref[b, 0, h].astype(F32) * wend_row).astype(BF16)
            w_rep = jnp.broadcast_to(wend_row, (16, BLOCK)).astype(BF16)
            st["upd"].append(_dot(jnp.concatenate([vw_t, w_rep], axis=0), k[h]))
        q2 = [jnp.concatenate([qa_ref[b, :, (2 * kv) * 128:(2 * kv + 1) * 128],
                               qa_ref[b, :, (2 * kv + 1) * 128:(2 * kv + 2) * 128]], axis=0)
              for kv in range(KV_HEADS)]
        for kv, parity in groups:
            var = (2 * kv + parity) * KV_WIDTH
            k_band = jnp.concatenate([kxp_ref[b, :, var:var + KV_WIDTH], kx_ref[b, :, var:var + KV_WIDTH]], axis=0)
            st["sc"].append(_dot_nt(q2[kv], k_band))
        return st

    def vector_work(b, st):
        st["s"], st["pr"], st["e_sink"] = [], [], []
        for h in heads:
            g_row = rows_ref[b, 0, 0][h:h + 1, :]
            decay = rows_ref[b, 0, 2][h:h + 1, :]
            d = jnp.where(causal, jnp.exp2(st["col"][:, h:h + 1] + g_row), 0.0)
            st["s"].append((st["qkc"][h][:, 0:BLOCK] * d).astype(BF16))
            c_ref[b, h] = decay * st["c_prev"][h] + st["upd"][h][0:M_DIM]
            n_ref[b, h] = decay * st["n_prev"][h] + st["upd"][h][M_DIM:M_DIM + 8]
        for gi, (kv, parity) in enumerate(groups):
            sink = jnp.where(top_rows, sinks_ref[4 * kv + parity], sinks_ref[4 * kv + 2 + parity]) * LOG2E
            scm = jnp.where(valid, st["sc"][gi], NEG_INF)
            mx = jnp.maximum(jnp.max(scm, axis=-1, keepdims=True), sink)
            st["pr"].append(jnp.exp2(scm - mx).astype(BF16))
            st["e_sink"].append(jnp.exp2(sink - mx))

    def second_matmuls(b, st):
        st["sv"], st["pv"] = [], []
        for h in heads:
            v = qkv_ref[b, :, 2 * M_WIDTH + m_lo(h):2 * M_WIDTH + m_lo(h) + M_DIM]
            st["sv"].append(_dot(st["s"][h], jnp.concatenate([v, ones_m], axis=1)))
        for gi, (kv, parity) in enumerate(groups):
            var = (2 * kv + parity) * KV_WIDTH
            v_band = jnp.concatenate([vxp_ref[b, :, var:var + KV_WIDTH], vx_ref[b, :, var:var + KV_WIDTH]], axis=0)
            st["pv"].append(_dot(st["pr"][gi], jnp.concatenate([v_band, ones_half[parity]], axis=1)))

    def normalise(b, st):
        col_b, qkc, sv, pv, e_sink = st["col"], st["qkc"], st["sv"], st["pv"], st["e_sink"]
        for h in heads:
            lo, hi = m_lo(h), m_lo(h) + M_DIM
            wi_col = col_b[:, 8 + h:9 + h]
            num = wi_col * qkc[h][:, BLOCK:2 * BLOCK] + sv[h][:, 0:M_DIM]
            den = wi_col * qkc[h][:, 2 * BLOCK:3 * BLOCK] + sv[h][:, M_DIM:2 * M_DIM]
            hh = num / jnp.maximum(jnp.abs(den), col_b[:, 16 + h:17 + h])
            hh = hh * lax.rsqrt(jnp.mean(hh * hh, axis=-1, keepdims=True) + RMS_EPS)
            y_ref[b, :, lo:hi] = (hh * og_ref[b, :, lo:hi]).astype(BF16)
        for kv in range(KV_HEADS):
            acc = pv[2 * kv] + pv[2 * kv + 1]
            denom = acc[:, KV_WIDTH:] + jnp.where(lane_lo, e_sink[2 * kv], e_sink[2 * kv + 1])
            ya = (acc[:, 0:KV_WIDTH] / denom).astype(BF16)
            p0, p1 = 2 * kv, 2 * kv + 1
            y_ref[b, :, M_WIDTH + p0 * 128:M_WIDTH + (p0 + 1) * 128] = ya[0:BLOCK]
            y_ref[b, :, M_WIDTH + p1 * 128:M_WIDTH + (p1 + 1) * 128] = ya[BLOCK:2 * BLOCK]

    for b in range(batch):
        st = first_matmuls(b)
        vector_work(b, st)
        second_matmuls(b, st)
        normalise(b, st)

    @pl.when(j == pl.num_programs(0) - 1)
    def _():
        c_out_ref[...] = c_ref[...]
        n_out_ref[...] = n_ref[:, :, 0:1, :]


def _prompt_mixer(sinks, qkv, vt, og, cols, rows, qa, kx, vx, batch, seq):
    nblk = seq // BLOCK
    r3 = lambda a: a.reshape(batch, seq, a.shape[-1])
    cur = lambda w: pl.BlockSpec((batch, BLOCK, w), lambda j: (0, j, 0))
    prev = lambda w: pl.BlockSpec((batch, BLOCK, w), lambda j: (0, jnp.maximum(j - 1, 0), 0))
    state = lambda shape: pl.BlockSpec((batch,) + shape, lambda j: (0,) * (len(shape) + 1))
    out_shape = (
        jax.ShapeDtypeStruct((batch, seq, D_MODEL), BF16),
        jax.ShapeDtypeStruct((batch, M_HEADS, M_DIM, M_DIM), F32),
        jax.ShapeDtypeStruct((batch, M_HEADS, 1, M_DIM), F32),
    )
    y, pc, pn = pl.pallas_call(
        _prompt_mixer_kernel,
        out_shape=out_shape,
        grid=(nblk,),
        in_specs=[pl.BlockSpec(memory_space=pltpu.SMEM),
                  cur(3 * M_WIDTH),
                  pl.BlockSpec((batch, 1, M_HEADS, M_DIM, BLOCK), lambda j: (0, j, 0, 0, 0)),
                  cur(M_WIDTH), cur(128),
                  pl.BlockSpec((batch, 1, 3, 8, BLOCK), lambda j: (0, j, 0, 0, 0)),
                  cur(A_WIDTH), cur(4 * KV_WIDTH), cur(4 * KV_WIDTH), prev(4 * KV_WIDTH), prev(4 * KV_WIDTH)],
        out_specs=(cur(D_MODEL), state((M_HEADS, M_DIM, M_DIM)), state((M_HEADS, 1, M_DIM))),
        scratch_shapes=[pltpu.VMEM((batch, M_HEADS, M_DIM, M_DIM), F32),
                        pltpu.VMEM((batch, M_HEADS, 8, M_DIM), F32)],
        compiler_params=pltpu.CompilerParams(dimension_semantics=("arbitrary",),
                                             vmem_limit_bytes=VMEM_LIMIT_BYTES),
        name="prompt_mixer",
    )(sinks, r3(qkv), vt.reshape(batch, nblk, M_HEADS, M_DIM, BLOCK), r3(og), r3(cols), rows, r3(qa),
      r3(kx), r3(vx), r3(kx), r3(vx))
    return y.reshape(batch * seq, D_MODEL), pc, pn


def _sample_mlstm_kernel(bi_ref, bf_ref, q_ref, k_ref, v_ref, og_ref, gates_ref, m0_ref, n0_ref, c_ref,
                         y_ref, n_out_ref, m_out_ref, c_out_ref, decay_ref, qr_ref):
    h = pl.program_id(0)
    nb = q_ref.shape[0]
    q_rows = q_ref[...].astype(F32)
    k_rows = k_ref[...].astype(F32)
    qr_ref[...] = q_rows
    qt, kt, vt = q_rows.T, k_rows.T, v_ref[...].astype(F32).T
    i_pre = gates_ref[pl.ds(h, 1), :] + bi_ref[h]
    a = _log_sigmoid(gates_ref[pl.ds(M_HEADS + h, 1), :] + bf_ref[h]) + m0_ref[pl.ds(h, 1), :]
    m_t = jnp.maximum(a, i_pre)
    w_inter = jnp.exp(a - m_t)
    w_in = jnp.exp(i_pre - m_t)
    scores = jnp.sum(qt * kt, axis=0, keepdims=True) * w_in
    n0t = n0_ref[...].T
    nq = jnp.sum(n0t * qt, axis=0, keepdims=True)

    rows = lax.broadcasted_iota(jnp.int32, (nb, M_DIM), 0)
    cols = lax.broadcasted_iota(jnp.int32, (M_DIM, nb), 1)
    vw_t = (vt * w_in).astype(BF16)

    decay_ref[...] = jnp.broadcast_to(w_inter, (M_DIM, nb)).T

    def body(grp, cq_t):
        base = grp * SAMPLE_UNROLL
        for u in range(SAMPLE_UNROLL):
            b = base + u
            col = jnp.sum(c_ref[b, 0] * qr_ref[pl.ds(b, 1), :], axis=-1, keepdims=True)
            cq_t = jnp.where(cols == b, col, cq_t)
        outer = []
        for u in range(SAMPLE_UNROLL):
            k_only_b = jnp.where(rows == base + u, k_rows, 0.0).astype(BF16)
            outer.append(_dot(vw_t, k_only_b))
        for u in range(SAMPLE_UNROLL):
            b = base + u
            c_out_ref[b, 0] = decay_ref[pl.ds(b, 1), :] * c_ref[b, 0] + outer[u]
        return cq_t

    cq_t = lax.fori_loop(0, nb // SAMPLE_UNROLL, body, jnp.zeros((M_DIM, nb), F32))

    num = w_inter * cq_t + scores * vt
    den = w_inter * nq + scores
    hh = num / jnp.maximum(jnp.abs(den), jnp.exp(-m_t))
    hh = hh * lax.rsqrt(jnp.mean(hh * hh, axis=0, keepdims=True) + RMS_EPS)
    y_ref[...] = hh.T * og_ref[...]
    n_out_ref[...] = (w_inter * n0t + w_in * kt).T
    m_out_ref[0] = m_t


def _sample_mlstm(b_i, b_f, qkv, og, gates_t, m0_t, n0, c0):
    nb = qkv.shape[0]
    smem = pl.BlockSpec(memory_space=pltpu.SMEM)
    head = lambda off: pl.BlockSpec((nb, M_DIM), lambda h: (0, off + h))
    out_shape = (
        jax.ShapeDtypeStruct((nb, M_WIDTH), F32),
        jax.ShapeDtypeStruct((nb, M_WIDTH), F32),
        jax.ShapeDtypeStruct((M_HEADS, 1, nb), F32),
        jax.ShapeDtypeStruct((nb, M_HEADS, M_DIM, M_DIM), F32),
    )
    c_spec = pl.BlockSpec((nb, 1, M_DIM, M_DIM), lambda h: (0, h, 0, 0))
    return pl.pallas_call(
        _sample_mlstm_kernel,
        out_shape=out_shape,
        grid=(M_HEADS,),
        in_specs=[smem, smem, head(0), head(M_HEADS), head(2 * M_HEADS), head(0),
                  pl.BlockSpec(gates_t.shape, lambda h: (0, 0)), pl.BlockSpec(m0_t.shape, lambda h: (0, 0)),
                  head(0), c_spec],
        out_specs=(head(0), head(0), pl.BlockSpec((1, 1, nb), lambda h: (h, 0, 0)), c_spec),
        scratch_shapes=[pltpu.VMEM((nb, M_DIM), F32), pltpu.VMEM((nb, M_DIM), F32)],
        compiler_params=pltpu.CompilerParams(dimension_semantics=("arbitrary",),
                                             vmem_limit_bytes=VMEM_LIMIT_BYTES),
        name="sample_mlstm",
    )(b_i, b_f, qkv, qkv, qkv, og, gates_t, m0_t, n0, c0)


SAMPLE_TILE = 16
SAMPLE_UNROLL = 8


def _sample_swa_kernel(q2_ref, kc_ref, vc_ref, kn_ref, vn_ref, sink_ref, o_ref, ko_ref, vo_ref):
    sink = sink_ref[...] * LOG2E
    w = kc_ref.shape[2]
    tile = range(SAMPLE_TILE)
    newest = lax.broadcasted_iota(jnp.int32, (KV_WIDTH, w), 1) == w - 1
    pad = jnp.zeros((KV_WIDTH - SAMPLE_TILE, KV_WIDTH), F32)
    kn_t = jnp.concatenate([kn_ref[...], pad], axis=0).T
    vn_t = jnp.concatenate([vn_ref[...], pad], axis=0).T
    q2 = [q2_ref[b] for b in tile]
    k_new = [kn_ref[b:b + 1, :] for b in tile]
    v_new = [vn_ref[b:b + 1, :] for b in tile]
    s_c = [_dot(q2[b], kc_ref[b].astype(BF16)) for b in tile]
    s_n = [jnp.sum(q2[b].astype(F32) * k_new[b], axis=-1, keepdims=True) for b in tile]
    mx = [jnp.maximum(jnp.maximum(jnp.max(s_c[b], axis=-1, keepdims=True), s_n[b]), sink) for b in tile]
    p_c = [jnp.exp2(s_c[b] - mx[b]) for b in tile]
    p_n = [jnp.exp2(s_n[b] - mx[b]) for b in tile]
    denom = [jnp.sum(p_c[b], axis=-1, keepdims=True) + p_n[b] + jnp.exp2(sink - mx[b]) for b in tile]
    o = [_dot_nt(p_c[b].astype(BF16), vc_ref[b].astype(BF16)) for b in tile]
    for b in tile:
        o_ref[b] = (o[b] + p_n[b] * v_new[b]) / denom[b]
        ko_ref[b] = jnp.where(newest, kn_t[:, b:b + 1], pltpu.roll(kc_ref[b], w - 1, 1))
        vo_ref[b] = jnp.where(newest, vn_t[:, b:b + 1], pltpu.roll(vc_ref[b], w - 1, 1))


def _sample_swa(q2, k_cache, v_cache, k_new, v_new, sink_col):
    nb, _, w = k_cache.shape
    t3 = lambda a, c: pl.BlockSpec((SAMPLE_TILE, a, c), lambda i: (i, 0, 0))
    t2 = pl.BlockSpec((SAMPLE_TILE, KV_WIDTH), lambda i: (i, 0))
    out_shape = (
        jax.ShapeDtypeStruct((nb, A_HEADS, KV_WIDTH), F32),
        jax.ShapeDtypeStruct((nb, KV_WIDTH, w), F32),
        jax.ShapeDtypeStruct((nb, KV_WIDTH, w), F32),
    )
    return pl.pallas_call(
        _sample_swa_kernel,
        out_shape=out_shape,
        grid=(nb // SAMPLE_TILE,),
        in_specs=[t3(A_HEADS, KV_WIDTH), t3(KV_WIDTH, w), t3(KV_WIDTH, w), t2, t2,
                  pl.BlockSpec((A_HEADS, 1), lambda i: (0, 0))],
        out_specs=(t3(A_HEADS, KV_WIDTH), t3(KV_WIDTH, w), t3(KV_WIDTH, w)),
        compiler_params=pltpu.CompilerParams(dimension_semantics=("arbitrary",)),
        name="sample_swa",
    )(q2, k_cache, v_cache, k_new, v_new, sink_col)


def kernel(x_prompt, x_sample, cache_swa_k, cache_swa_v, state_mlstm_C, state_mlstm_n, state_mlstm_m,
           ffn1_norm, ffn1_w_gate, ffn1_w_up, ffn1_w_down, mix_norm, w_in, mlstm_b_i, mlstm_b_f,
           mlstm_out_norm, swa_q_norm, swa_k_norm, swa_sinks, w_out, ffn2_norm, ffn2_w_gate,
           ffn2_w_up, ffn2_w_down):
    depth = ffn1_norm.shape[0]
    assert depth == 1
    batch, seq, _ = x_prompt.shape
    nb = x_sample.shape[0]
    assert x_sample.shape[1] == 1 and seq % BLOCK == 0

    front_weights = (ffn1_w_gate[0], ffn1_w_up[0], ffn1_w_down[0], jnp.transpose(w_in[0]))
    n1 = ffn1_norm[0].reshape(1, D_MODEL)
    n2 = mix_norm[0].reshape(1, D_MODEL)
    n3 = ffn2_norm[0].reshape(1, D_MODEL)
    qgain = jnp.tile(swa_q_norm[0], A_HEADS).reshape(1, A_WIDTH)
    kgain = jnp.tile(swa_k_norm[0], KV_HEADS).reshape(1, KV_WIDTH)
    ogain = mlstm_out_norm[0].reshape(1, M_WIDTH)
    b_i, b_f = mlstm_b_i[0], mlstm_b_f[0]
    bias8 = jnp.concatenate([b_i, b_f]).reshape(2 * M_HEADS, 1)
    front_vectors = (n1, n2, qgain, kgain, ogain, bias8)
    sinks = swa_sinks[0]

    xp = x_prompt.reshape(batch * seq, D_MODEL)
    xs = x_sample.reshape(nb, D_MODEL)
    back_weights_f32 = (w_out[0], ffn2_w_gate[0], ffn2_w_up[0], ffn2_w_down[0])
    prompt_front, sample_front, (wo, wg2, wu2, wd2) = _front(xp, xs, front_vectors, front_weights,
                                                             back_weights_f32, tm=512, seq=seq)
    back_params = (wo, n3, wg2, wu2, wd2)
    x1, qkv, og, qa, ka, va, vt, kx, vx, rows, cols, pm = prompt_front
    x1s, qkvs, ogs, qas, kas, vas = sample_front[:6]
    gates_t = sample_front[-1]

    rows = rows.reshape(batch, seq // BLOCK, 3, 8, BLOCK)
    y, pc, pn = _prompt_mixer(sinks, qkv, vt, og, cols, rows, qa, kx, vx, batch, seq)
    to_window_major = lambda c, n: jnp.transpose(c.reshape(n, KV_HEADS, A_DIM, -1), (0, 3, 1, 2))[None]
    pk, pv = to_window_major(ka, batch), to_window_major(va, batch)
    pn = pn[:, :, 0, :]
    pm = pm[:, 0:M_HEADS, 0]

    y_m, sn, mt, sc = _sample_mlstm(b_i, b_f, qkvs, ogs, gates_t, jnp.transpose(state_mlstm_m[0]),
                                    state_mlstm_n[0].reshape(nb, M_WIDTH), state_mlstm_C[0])
    sn = sn.reshape(nb, M_HEADS, M_DIM)
    sm = jnp.transpose(mt[:, 0, :])

    qa_h = qas.reshape(nb, A_HEADS, A_DIM)
    zeros = jnp.zeros_like(qa_h)
    in_lo = (jnp.arange(A_HEADS) // A_GROUP == 0)[None, :, None]
    q2 = jnp.concatenate([jnp.where(in_lo, qa_h, zeros), jnp.where(in_lo, zeros, qa_h)], axis=-1)
    to_feature_major = lambda c: jnp.transpose(c[0], (0, 2, 3, 1)).reshape(nb, KV_WIDTH, -1)
    kc, vc = to_feature_major(cache_swa_k), to_feature_major(cache_swa_v)
    o2, sk, sv = _sample_swa(q2, kc, vc, kas, vas, sinks.reshape(A_HEADS, 1))
    o2 = o2.reshape(nb, A_HEADS, KV_HEADS, A_DIM)
    y_a = jnp.where(in_lo, o2[:, :, 0, :], o2[:, :, 1, :]).reshape(nb, A_WIDTH)
    ys_in = jnp.concatenate([y_m, y_a], axis=-1).astype(BF16)

    yp, ys = _back(x1, y, x1s, ys_in, back_params, tm=1024)
    yp = yp.reshape(batch, seq, D_MODEL)

    return (yp, ys, pk, pv, pc[None], pn[None], pm[None],
            to_window_major(sk, nb), to_window_major(sv, nb), sc[None], sn[None], sm[None])
```

```python
import functools
from typing import NamedTuple

import jax
import jax.numpy as jnp
from jax import lax
from jax.experimental import pallas as pl
from jax.experimental.pallas import tpu as pltpu

F32 = jnp.float32
BF16 = jnp.bfloat16

D_MODEL = 1024
D_FF = 2816
FF_CHUNK = 256
N_FF_CHUNKS = D_FF // FF_CHUNK
M_HEADS = 4
M_DIM = 128
M_WIDTH = M_HEADS * M_DIM
A_HEADS = 8
A_DIM = 64
A_WIDTH = A_HEADS * A_DIM
KV_HEADS = 2
KV_WIDTH = KV_HEADS * A_DIM
A_GROUP = A_HEADS // KV_HEADS
WINDOW = 128
BLOCK = 128
GATE_PAD = 128
RMS_EPS = 1e-6
FFN_RES_WEIGHT = 0.5
NEG_INF = float("-inf")
LOG2E = 1.4426950408889634
VMEM_LIMIT_BYTES = 56 * 1024 * 1024
CASTING_VMEM_LIMIT_BYTES = 60 * 1024 * 1024


def _dot(a, b):
    return jnp.dot(a, b, preferred_element_type=F32)


def _dot_nt(a, b):
    return lax.dot_general(a, b, (((1,), (1,)), ((), ())), preferred_element_type=F32)


def _rms_rows(x, gain):
    ms = jnp.mean(x * x, axis=-1, keepdims=True)
    return x * lax.rsqrt(ms + RMS_EPS) * gain


def _log_sigmoid(x):
    return jnp.minimum(x, 0.0) - jnp.log1p(jnp.exp(-jnp.abs(x)))


def _half_tile_mean_sq(x):
    in_lo = lax.broadcasted_iota(jnp.int32, (x.shape[0], 128), 1) < A_DIM
    out = []
    for c in range(x.shape[1] // 128):
        sq = x[:, c * 128:(c + 1) * 128]
        sq = sq * sq
        s_lo = jnp.sum(jnp.where(in_lo, sq, 0.0), axis=-1, keepdims=True)
        s_hi = jnp.sum(jnp.where(in_lo, 0.0, sq), axis=-1, keepdims=True)
        out.append(jnp.where(in_lo, s_lo, s_hi) * (1.0 / A_DIM))
    return out[0] if len(out) == 1 else jnp.concatenate(out, axis=1)


def _ffn(h_ref, wg_ref, wu_ref, wd_ref, act_ref, weights_ready=None):
    for c in range(N_FF_CHUNKS):
        lo, hi = c * FF_CHUNK, (c + 1) * FF_CHUNK
        if weights_ready is not None:
            weights_ready("gate_up", c)
        h = h_ref[...]
        g = _dot(h, wg_ref[:, lo:hi])
        u = _dot(h, wu_ref[:, lo:hi])
        act_ref[:, lo:hi] = (g * jax.nn.sigmoid(g) * u).astype(BF16)
    if weights_ready is not None:
        weights_ready("down", 0)
    return _dot(act_ref[...], wd_ref[...])


N_FRONT_VECTORS = 6
N_FRONT_WEIGHTS = 4
N_FRONT_COMMON_OUTS = 9
N_PROMPT_GATE_OUTS = 3
W_M_ROWS = 4 * M_WIDTH
W_GATE_ROWS = 2 * M_HEADS
WA_KVG = A_WIDTH
WA_GATES = WA_KVG + 2 * KV_WIDTH
WA_ROWS = WA_GATES + GATE_PAD
CAST_ROWS, CAST_COLS = D_MODEL, FF_CHUNK
CAST_SLOTS = 6


class _CastJob(NamedTuple):
    src: object
    src_row: int
    dst: object
    dst_row: int
    col: int
    rows: int
    cols: int
    dst_rows: int


def _cast_jobs(src, dst, src_row=0, dst_row=0, rows=None, dst_rows=None):
    rows = src.shape[0] - src_row if rows is None else rows
    jobs = []
    for r in range(0, rows, CAST_ROWS):
        nr = min(CAST_ROWS, rows - r)
        for c in range(0, src.shape[1], CAST_COLS):
            nc = min(CAST_COLS, src.shape[1] - c)
            jobs.append(_CastJob(src, src_row + r, dst, dst_row + r, c, nr, nc, nr if dst_rows is None else dst_rows))
    return jobs


class _CastRing:
    def __init__(self, jobs, stage_ref, sem_ref):
        self.jobs, self.stage_ref, self.sem_ref = jobs, stage_ref, sem_ref
        self.ahead = CAST_SLOTS - 1
        self.done = 0
        for j in range(min(self.ahead, len(jobs))):
            self._copy(j).start()

    def _copy(self, j):
        job, slot = self.jobs[j], j % CAST_SLOTS
        return pltpu.make_async_copy(job.src.at[pl.ds(job.src_row, job.rows), pl.ds(job.col, job.cols)],
                                     self.stage_ref.at[slot, pl.ds(0, job.rows), pl.ds(0, job.cols)],
                                     self.sem_ref.at[slot])

    def finish(self, upto):
        for j in range(self.done, upto):
            job = self.jobs[j]
            if j + self.ahead < len(self.jobs):
                self._copy(j + self.ahead).start()
            self._copy(j).wait()
            block = self.stage_ref[j % CAST_SLOTS, 0:job.rows, 0:job.cols]
            if job.dst_rows > job.rows:
                block = jnp.concatenate([block, jnp.zeros((job.dst_rows - job.rows, job.cols), F32)], axis=0)
            job.dst[job.dst_row:job.dst_row + job.dst_rows, job.col:job.col + job.cols] = block.astype(BF16)
        self.done = max(self.done, upto)


def _prompt_gates(gates_t, bias_ref, first_of_seq, m_ref, rows_ref, cols_ref, mfin_ref):
    sub = lax.broadcasted_iota(jnp.int32, (8, BLOCK), 0)
    lane = lax.broadcasted_iota(jnp.int32, (8, BLOCK), 1)
    is_head = sub < M_HEADS
    pad = jnp.zeros((BLOCK - 24, BLOCK), F32)
    scan_shifts = (1, 2, 4, 8, 16, 32, 64)
    m_prev = jnp.where(first_of_seq, 0.0, m_ref[...])[:, 0:1]
    for c in range(rows_ref.shape[0]):
        pre = gates_t[:, c * BLOCK:(c + 1) * BLOCK] + bias_ref[...]
        r = jnp.where(is_head, pre, _log_sigmoid(pre))
        cum = r
        for shift in scan_shifts:
            cum = cum + jnp.where(lane >= shift, pltpu.roll(cum, shift, 1), 0.0)
        bcum = pltpu.roll(cum, M_HEADS, 0)
        g = jnp.where(is_head, r - bcum, 0.0)
        bcum = jnp.where(is_head, bcum, 0.0)
        cm = g
        for shift in scan_shifts:
            cm = jnp.maximum(cm, jnp.where(lane >= shift, pltpu.roll(cm, shift, 1), NEG_INF))
        cm_last = jnp.max(cm, axis=-1, keepdims=True)
        b_last = jnp.sum(jnp.where(lane == BLOCK - 1, bcum, 0.0), axis=-1, keepdims=True)
        mx = jnp.maximum(m_prev, cm)
        mx_last = jnp.maximum(m_prev, cm_last)
        rows_ref[c, 0] = g * LOG2E
        rows_ref[c, 1] = jnp.exp(g - mx_last)
        rows_ref[c, 2] = jnp.broadcast_to(jnp.exp(m_prev - mx_last), (8, BLOCK))
        col_src = jnp.concatenate([mx * -LOG2E, jnp.exp(m_prev - mx), jnp.exp(-(bcum + mx)), pad], axis=0)
        cols_ref[c * BLOCK:(c + 1) * BLOCK, :] = col_src.T
        m_prev = b_last + mx_last
    m_full = jnp.broadcast_to(m_prev, (8, BLOCK))
    m_ref[...] = m_full
    mfin_ref[0] = m_full


def _front_tile(x_ref, params, outs, gate_sink, h_ref, act_ref, cache_feature_major, weights_ready=None):
    (n1_ref, wg_ref, wu_ref, wd_ref, n2_ref, wm_ref, wa_ref, qgain_ref, kgain_ref, ogain_ref, _) = params
    (x1_ref, qkv_ref, og_ref, qa_ref, ka_ref, va_ref, vt_ref, kx_ref, vx_ref) = outs
    x = x_ref[...]
    h_ref[...] = _rms_rows(x, n1_ref[...]).astype(BF16)
    x1 = x + FFN_RES_WEIGHT * _ffn(h_ref, wg_ref, wu_ref, wd_ref, act_ref, weights_ready)
    x1_ref[...] = x1
    h_ref[...] = _rms_rows(x1, n2_ref[...]).astype(BF16)
    if weights_ready is not None:
        weights_ready("projection", 0)
    h = h_ref[...]
    project = lambda w_ref, lo, hi: _dot_nt(h, w_ref[lo:hi, :])

    qa = project(wa_ref, 0, WA_KVG)
    kvg = project(wa_ref, WA_KVG, WA_ROWS)
    v_m = project(wm_ref, 2 * M_WIDTH, 3 * M_WIDTH)
    qkv_ref[:, 2 * M_WIDTH:3 * M_WIDTH] = v_m.astype(BF16)
    for c in range(vt_ref.shape[0]):
        for hd in range(M_HEADS):
            blk = v_m[c * BLOCK:(c + 1) * BLOCK, hd * M_DIM:(hd + 1) * M_DIM]
            vt_ref[c, hd] = blk.T.astype(BF16)
    q_scale = qgain_ref[...] * (A_DIM ** -0.5 * LOG2E)
    qa_ref[...] = (qa * lax.rsqrt(_half_tile_mean_sq(qa) + RMS_EPS) * q_scale).astype(BF16)
    ka = kvg[:, 0:KV_WIDTH]
    ka = ka * lax.rsqrt(_half_tile_mean_sq(ka) + RMS_EPS) * kgain_ref[...]
    va = kvg[:, KV_WIDTH:2 * KV_WIDTH]
    for src, dst in ((ka, ka_ref), (va, va_ref)):
        if cache_feature_major:
            dst[...] = src[src.shape[0] - dst.shape[1]:, :].T
        else:
            dst[...] = src[src.shape[0] - dst.shape[0]:, :]
    gate_sink(kvg[:, 2 * KV_WIDTH:2 * KV_WIDTH + GATE_PAD].T[0:2 * M_HEADS, :])

    in_lo = lax.broadcasted_iota(jnp.int32, ka.shape, 1) < A_DIM
    for src, dst in ((ka, kx_ref), (va, vx_ref)):
        x0 = jnp.where(in_lo, src, 0.0)
        x1 = jnp.where(in_lo, 0.0, src)
        dst[:, 0:128] = x0.astype(BF16)
        dst[:, 128:256] = pltpu.roll(x0, A_DIM, 1).astype(BF16)
        dst[:, 256:384] = pltpu.roll(x1, A_DIM, 1).astype(BF16)
        dst[:, 384:512] = x1.astype(BF16)

    og_ref[...] = jax.nn.sigmoid(project(wm_ref, 3 * M_WIDTH, 4 * M_WIDTH)) * ogain_ref[...]
    k_m = project(wm_ref, M_WIDTH, 2 * M_WIDTH) * (M_DIM ** -0.5)
    qkv_ref[:, M_WIDTH:2 * M_WIDTH] = k_m.astype(BF16)
    qkv_ref[:, 0:M_WIDTH] = project(wm_ref, 0, M_WIDTH).astype(BF16)


def _front_kernel(*refs, n_tiles, tiles_per_seq, n_cast):
    refs = list(refs)
    take = lambda k: [refs.pop(0) for _ in range(k)]
    x_ref, xs_ref = take(2)
    n1_ref, n2_ref, qgain_ref, kgain_ref, ogain_ref, bias_ref = take(N_FRONT_VECTORS)
    wg_hbm, wu_hbm, wd_hbm, wt_hbm = take(N_FRONT_WEIGHTS)
    cast_src = take(n_cast)
    outs_p = take(N_FRONT_COMMON_OUTS)
    rows_ref, cols_ref, mfin_ref = take(N_PROMPT_GATE_OUTS)
    outs_s = take(N_FRONT_COMMON_OUTS)
    (gts_ref,) = take(1)
    cast_dst = take(n_cast)
    h_ref, act_ref, m_ref, wg_ref, wu_ref, wd_ref, wm_ref, wa_ref, stage_ref, sem_ref = refs
    params = (n1_ref, wg_ref, wu_ref, wd_ref, n2_ref, wm_ref, wa_ref, qgain_ref, kgain_ref, ogain_ref, bias_ref)
    i = pl.program_id(0)

    def prompt_tile(weights_ready):
        for src, dst in zip(cast_src, cast_dst):
            dst[...] = src[...].astype(BF16)
        gates = functools.partial(_prompt_gates, bias_ref=bias_ref, first_of_seq=i % tiles_per_seq == 0,
                                  m_ref=m_ref, rows_ref=rows_ref, cols_ref=cols_ref, mfin_ref=mfin_ref)
        _front_tile(x_ref, params, outs_p, gates, h_ref, act_ref, True, weights_ready)

    @pl.when(i == 0)
    def _():
        gate_up = [job for pair in zip(_cast_jobs(wg_hbm, wg_ref), _cast_jobs(wu_hbm, wu_ref)) for job in pair]
        down = _cast_jobs(wd_hbm, wd_ref)
        a_src = W_M_ROWS + W_GATE_ROWS
        projection = (_cast_jobs(wt_hbm, wm_ref, rows=W_M_ROWS)
                      + _cast_jobs(wt_hbm, wa_ref, src_row=a_src, rows=WA_GATES)
                      + _cast_jobs(wt_hbm, wa_ref, src_row=W_M_ROWS, dst_row=WA_GATES, rows=W_GATE_ROWS,
                                   dst_rows=GATE_PAD))
        assert len(gate_up) == 2 * N_FF_CHUNKS
        ring = _CastRing(gate_up + down + projection, stage_ref, sem_ref)
        upto = {"gate_up": lambda c: 2 * (c + 1), "down": lambda c: len(gate_up) + len(down),
                "projection": lambda c: len(ring.jobs)}
        prompt_tile(lambda stage, c: ring.finish(upto[stage](c)))

    @pl.when((i > 0) & (i < n_tiles))
    def _():
        prompt_tile(None)

    @pl.when(i == n_tiles)
    def _():
        ns = xs_ref.shape[0]

        def raw_gates(gates_t):
            gts_ref[...] = gates_t

        _front_tile(xs_ref, params, outs_s, raw_gates, h_ref.at[0:ns], act_ref.at[0:ns], False)


def _const_spec(shape):
    nd = len(shape)
    return pl.BlockSpec(shape, lambda i: (0,) * nd, pipeline_mode=pl.Buffered(1))


def _whole_spec(shape):
    nd = len(shape)
    return pl.BlockSpec(shape, lambda i: (0,) * nd)


def _front_out_shapes(n, cache_shape):
    return (
        jax.ShapeDtypeStruct((n, D_MODEL), F32),
        jax.ShapeDtypeStruct((n, 3 * M_WIDTH), BF16),
        jax.ShapeDtypeStruct((n, M_WIDTH), F32),
        jax.ShapeDtypeStruct((n, A_WIDTH), BF16),
        jax.ShapeDtypeStruct(cache_shape, F32),
        jax.ShapeDtypeStruct(cache_shape, F32),
        jax.ShapeDtypeStruct((n // BLOCK, M_HEADS, M_DIM, BLOCK), BF16),
        jax.ShapeDtypeStruct((n, 4 * KV_WIDTH), BF16),
        jax.ShapeDtypeStruct((n, 4 * KV_WIDTH), BF16),
    )


def _cast_chunk_count(n_rows, max_chunks):
    for k in range(max_chunks, 0, -1):
        if n_rows % k == 0 and (n_rows // k) % 16 == 0:
            return k
    raise ValueError(n_rows)


def _front(x2d, xs2d, vectors, weights, later_weights, tm, seq):
    n, ns = x2d.shape[0], xs2d.shape[0]
    assert seq % tm == 0 and tm >= WINDOW
    assert len(vectors) == N_FRONT_VECTORS and len(weights) == N_FRONT_WEIGHTS
    assert weights[3].shape == (W_M_ROWS + W_GATE_ROWS + WA_GATES, D_MODEL)
    n_tiles = n // tm
    n_seqs = n // seq
    tiles_per_seq = seq // tm
    nb_t = tm // BLOCK
    tile = lambda i: jnp.minimum(i, n_tiles - 1)
    seq_of = lambda i: tile(i) // tiles_per_seq
    row = lambda w: pl.BlockSpec((tm, w), lambda i: (tile(i), 0))
    tail = pl.BlockSpec((KV_WIDTH, WINDOW), lambda i: (seq_of(i), 0))
    prompt_specs = (row(D_MODEL), row(3 * M_WIDTH), row(M_WIDTH), row(A_WIDTH), tail, tail,
                    pl.BlockSpec((nb_t, M_HEADS, M_DIM, BLOCK), lambda i: (tile(i), 0, 0, 0)),
                    row(4 * KV_WIDTH), row(4 * KV_WIDTH),
                    pl.BlockSpec((nb_t, 3, 8, BLOCK), lambda i: (tile(i), 0, 0, 0)), row(128),
                    pl.BlockSpec((1, 8, BLOCK), lambda i: (seq_of(i), 0, 0)))
    prompt_shapes = _front_out_shapes(n, (n_seqs * KV_WIDTH, WINDOW)) + (
        jax.ShapeDtypeStruct((n // BLOCK, 3, 8, BLOCK), F32),
        jax.ShapeDtypeStruct((n, 128), F32),
        jax.ShapeDtypeStruct((n_seqs, 8, BLOCK), F32),
    )
    sample_shapes = _front_out_shapes(ns, (ns, KV_WIDTH)) + (
        jax.ShapeDtypeStruct((2 * M_HEADS, ns), F32),
    )

    def chunk_spec(w):
        k = _cast_chunk_count(w.shape[0], n_tiles)
        return pl.BlockSpec((w.shape[0] // k, w.shape[1]), lambda i: (jnp.minimum(i, k - 1), 0))

    cast_specs = [chunk_spec(w) for w in later_weights]
    cast_shapes = tuple(jax.ShapeDtypeStruct(w.shape, BF16) for w in later_weights)
    outs = pl.pallas_call(
        functools.partial(_front_kernel, n_tiles=n_tiles, tiles_per_seq=tiles_per_seq, n_cast=len(later_weights)),
        out_shape=prompt_shapes + sample_shapes + cast_shapes,
        grid=(n_tiles + 1,),
        in_specs=([row(D_MODEL), _whole_spec(xs2d.shape)] + [_const_spec(v.shape) for v in vectors]
                  + [pl.BlockSpec(memory_space=pl.ANY)] * len(weights) + cast_specs),
        out_specs=prompt_specs + tuple(_whole_spec(s.shape) for s in sample_shapes) + tuple(cast_specs),
        scratch_shapes=[pltpu.VMEM((tm, D_MODEL), BF16), pltpu.VMEM((tm, D_FF), BF16),
                        pltpu.VMEM((8, BLOCK), F32),
                        pltpu.VMEM((D_MODEL, D_FF), BF16), pltpu.VMEM((D_MODEL, D_FF), BF16),
                        pltpu.VMEM((D_FF, D_MODEL), BF16),
                        pltpu.VMEM((W_M_ROWS, D_MODEL), BF16), pltpu.VMEM((WA_ROWS, D_MODEL), BF16),
                        pltpu.VMEM((CAST_SLOTS, CAST_ROWS, CAST_COLS), F32),
                        pltpu.SemaphoreType.DMA((CAST_SLOTS,))],
        compiler_params=pltpu.CompilerParams(dimension_semantics=("arbitrary",),
                                             vmem_limit_bytes=CASTING_VMEM_LIMIT_BYTES),
        name="front",
    )(x2d, xs2d, *vectors, *weights, *later_weights)
    n_p, n_s = len(prompt_shapes), len(sample_shapes)
    return outs[:n_p], outs[n_p:n_p + n_s], outs[n_p + n_s:]


def _back_tile(x1_ref, y_ref, params, out_ref, h_ref, act_ref, weights_ready=None):
    wo_ref, n_ref, wg_ref, wu_ref, wd_ref = params
    if weights_ready is not None:
        weights_ready("out_projection", 0)
    x2 = x1_ref[...] + _dot(y_ref[...], wo_ref[...])
    h_ref[...] = _rms_rows(x2, n_ref[...]).astype(BF16)
    out_ref[...] = x2 + FFN_RES_WEIGHT * _ffn(h_ref, wg_ref, wu_ref, wd_ref, act_ref, weights_ready)


def _back_kernel(x1_ref, y_ref, x1s_ref, ys_ref, n_ref, wo_hbm, wg_hbm, wu_hbm, wd_hbm, out_ref, outs_ref,
                 h_ref, act_ref, wo_ref, wg_ref, wu_ref, wd_ref, stage_ref, sem_ref, *, n_tiles):
    params = (wo_ref, n_ref, wg_ref, wu_ref, wd_ref)
    i = pl.program_id(0)

    @pl.when(i == 0)
    def _():
        out_projection = _cast_jobs(wo_hbm, wo_ref)
        gate_up = [job for pair in zip(_cast_jobs(wg_hbm, wg_ref), _cast_jobs(wu_hbm, wu_ref)) for job in pair]
        down = _cast_jobs(wd_hbm, wd_ref)
        assert len(gate_up) == 2 * N_FF_CHUNKS
        ring = _CastRing(out_projection + gate_up + down, stage_ref, sem_ref)
        upto = {"out_projection": lambda c: len(out_projection),
                "gate_up": lambda c: len(out_projection) + 2 * (c + 1), "down": lambda c: len(ring.jobs)}
        _back_tile(x1_ref, y_ref, params, out_ref, h_ref, act_ref, lambda stage, c: ring.finish(upto[stage](c)))

    @pl.when((i > 0) & (i < n_tiles))
    def _():
        _back_tile(x1_ref, y_ref, params, out_ref, h_ref, act_ref)

    @pl.when(i == n_tiles)
    def _():
        ns = x1s_ref.shape[0]
        _back_tile(x1s_ref, ys_ref, params, outs_ref.at[:, 0], h_ref.at[0:ns], act_ref.at[0:ns])


def _back(x1, y, x1s, ys, norm_gain, weights, tm):
    n, ns = x1.shape[0], x1s.shape[0]
    n_tiles = n // tm
    row = pl.BlockSpec((tm, D_MODEL), lambda i: (jnp.minimum(i, n_tiles - 1), 0))
    return pl.pallas_call(
        functools.partial(_back_kernel, n_tiles=n_tiles),
        out_shape=(jax.ShapeDtypeStruct((n, D_MODEL), F32), jax.ShapeDtypeStruct((ns, 1, D_MODEL), F32)),
        grid=(n_tiles + 1,),
        in_specs=([row, row, _whole_spec(x1s.shape), _whole_spec(ys.shape), _const_spec(norm_gain.shape)]
                  + [pl.BlockSpec(memory_space=pl.ANY)] * len(weights)),
        out_specs=(row, _whole_spec((ns, 1, D_MODEL))),
        scratch_shapes=[pltpu.VMEM((tm, D_MODEL), BF16), pltpu.VMEM((tm, D_FF), BF16),
                        pltpu.VMEM((D_MODEL, D_MODEL), BF16),
                        pltpu.VMEM((D_MODEL, D_FF), BF16), pltpu.VMEM((D_MODEL, D_FF), BF16),
                        pltpu.VMEM((D_FF, D_MODEL), BF16),
                        pltpu.VMEM((CAST_SLOTS, CAST_ROWS, CAST_COLS), F32),
                        pltpu.SemaphoreType.DMA((CAST_SLOTS,))],
        compiler_params=pltpu.CompilerParams(dimension_semantics=("arbitrary",),
                                             vmem_limit_bytes=CASTING_VMEM_LIMIT_BYTES),
        name="back",
    )(x1, y, x1s, ys, norm_gain, *weights)


def _prompt_mixer_kernel(sinks_ref, qkv_ref, vt_ref, og_ref, cols_ref, rows_ref, qa_ref,
                         kx_ref, vx_ref, kxp_ref, vxp_ref,
                         y_ref, c_out_ref, n_out_ref,
                         c_ref, n_ref):
    j = pl.program_id(0)
    batch = qkv_ref.shape[0]

    @pl.when(j == 0)
    def _():
        c_ref[...] = jnp.zeros_like(c_ref)
        n_ref[...] = jnp.zeros_like(n_ref)

    rows = lax.broadcasted_iota(jnp.int32, (BLOCK, BLOCK), 0)
    cols = lax.broadcasted_iota(jnp.int32, (BLOCK, BLOCK), 1)
    causal = cols <= rows
    lane_lo = lax.broadcasted_iota(jnp.int32, (2 * BLOCK, KV_WIDTH), 1) < A_DIM
    ones_m = jnp.ones((BLOCK, M_DIM), BF16)
    ones_half = (jnp.where(lane_lo, 1.0, 0.0).astype(BF16), jnp.where(lane_lo, 0.0, 1.0).astype(BF16))
    qi = lax.broadcasted_iota(jnp.int32, (2 * BLOCK, 2 * BLOCK), 0) % BLOCK
    kc = lax.broadcasted_iota(jnp.int32, (2 * BLOCK, 2 * BLOCK), 1)
    first_valid = jnp.where(j == 0, BLOCK, 0)
    valid = (kc >= qi) & (kc <= qi + WINDOW) & (kc >= first_valid)
    top_rows = lax.broadcasted_iota(jnp.int32, (2 * BLOCK, 1), 0) < BLOCK

    heads = range(M_HEADS)
    groups = [(kv, parity) for kv in range(KV_HEADS) for parity in range(2)]
    m_lo = lambda h: h * M_DIM

    def first_matmuls(b):
        st = {"col": cols_ref[b]}
        k = [qkv_ref[b, :, M_WIDTH + m_lo(h):M_WIDTH + m_lo(h) + M_DIM] for h in heads]
        st["c_prev"] = [c_ref[b, h] for h in heads]
        st["n_prev"] = [n_ref[b, h] for h in heads]
        st["qkc"], st["upd"], st["sc"] = [], [], []
        for h in heads:
            q = qkv_ref[b, :, m_lo(h):m_lo(h) + M_DIM]
            n_rep = jnp.broadcast_to(st["n_prev"][h][0:1, :], (BLOCK, M_DIM)).astype(BF16)
            rhs = jnp.concatenate([k[h], st["c_prev"][h].astype(BF16), n_rep], axis=0)
            st["qkc"].append(_dot_nt(q, rhs))
        for h in heads:
            wend_row = rows_ref[b, 0, 1][h:h + 1, :]
            vw_t = (vt_ref[b, 0, h].astype(F32) * wend_row).astype(BF16)
            w_rep = jnp.broadcast_to(wend_row, (16, BLOCK)).astype(BF16)
            st["upd"].append(_dot(jnp.concatenate([vw_t, w_rep], axis=0), k[h]))
        q2 = [jnp.concatenate([qa_ref[b, :, (2 * kv) * 128:(2 * kv + 1) * 128],
                               qa_ref[b, :, (2 * kv + 1) * 128:(2 * kv + 2) * 128]], axis=0)
              for kv in range(KV_HEADS)]
        for kv, parity in groups:
            var = (2 * kv + parity) * KV_WIDTH
            k_band = jnp.concatenate([kxp_ref[b, :, var:var + KV_WIDTH], kx_ref[b, :, var:var + KV_WIDTH]], axis=0)
            st["sc"].append(_dot_nt(q2[kv], k_band))
        return st

    def vector_work(b, st):
        st["s"], st["pr"], st["e_sink"] = [], [], []
        for h in heads:
            g_row = rows_ref[b, 0, 0][h:h + 1, :]
            decay = rows_ref[b, 0, 2][h:h + 1, :]
            d = jnp.where(causal, jnp.exp2(st["col"][:, h:h + 1] + g_row), 0.0)
            st["s"].append((st["qkc"][h][:, 0:BLOCK] * d).astype(BF16))
            c_ref[b, h] = decay * st["c_prev"][h] + st["upd"][h][0:M_DIM]
            n_ref[b, h] = decay * st["n_prev"][h] + st["upd"][h][M_DIM:M_DIM + 8]
        for gi, (kv, parity) in enumerate(groups):
            sink = jnp.where(top_rows, sinks_ref[4 * kv + parity], sinks_ref[4 * kv + 2 + parity]) * LOG2E
            scm = jnp.where(valid, st["sc"][gi], NEG_INF)
            mx = jnp.maximum(jnp.max(scm, axis=-1, keepdims=True), sink)
            st["pr"].append(jnp.exp2(scm - mx).astype(BF16))
            st["e_sink"].append(jnp.exp2(sink - mx))

    def second_matmuls(b, st):
        st["sv"], st["pv"] = [], []
        for h in heads:
            v = qkv_ref[b, :, 2 * M_WIDTH + m_lo(h):2 * M_WIDTH + m_lo(h) + M_DIM]
            st["sv"].append(_dot(st["s"][h], jnp.concatenate([v, ones_m], axis=1)))
        for gi, (kv, parity) in enumerate(groups):
            var = (2 * kv + parity) * KV_WIDTH
            v_band = jnp.concatenate([vxp_ref[b, :, var:var + KV_WIDTH], vx_ref[b, :, var:var + KV_WIDTH]], axis=0)
            st["pv"].append(_dot(st["pr"][gi], jnp.concatenate([v_band, ones_half[parity]], axis=1)))

    def normalise(b, st):
        col_b, qkc, sv, pv, e_sink = st["col"], st["qkc"], st["sv"], st["pv"], st["e_sink"]
        for h in heads:
            lo, hi = m_lo(h), m_lo(h) + M_DIM
            wi_col = col_b[:, 8 + h:9 + h]
            num = wi_col * qkc[h][:, BLOCK:2 * BLOCK] + sv[h][:, 0:M_DIM]
            den = wi_col * qkc[h][:, 2 * BLOCK:3 * BLOCK] + sv[h][:, M_DIM:2 * M_DIM]
            hh = num / jnp.maximum(jnp.abs(den), col_b[:, 16 + h:17 + h])
            hh = hh * lax.rsqrt(jnp.mean(hh * hh, axis=-1, keepdims=True) + RMS_EPS)
            y_ref[b, :, lo:hi] = (hh * og_ref[b, :, lo:hi]).astype(BF16)
        for kv in range(KV_HEADS):
            acc = pv[2 * kv] + pv[2 * kv + 1]
            denom = acc[:, KV_WIDTH:] + jnp.where(lane_lo, e_sink[2 * kv], e_sink[2 * kv + 1])
            ya = (acc[:, 0:KV_WIDTH] / denom).astype(BF16)
            p0, p1 = 2 * kv, 2 * kv + 1
            y_ref[b, :, M_WIDTH + p0 * 128:M_WIDTH + (p0 + 1) * 128] = ya[0:BLOCK]
            y_ref[b, :, M_WIDTH + p1 * 128:M_WIDTH + (p1 + 1) * 128] = ya[BLOCK:2 * BLOCK]

    for b in range(batch):
        st = first_matmuls(b)
        vector_work(b, st)
        second_matmuls(b, st)
        normalise(b, st)

    @pl.when(j == pl.num_programs(0) - 1)
    def _():
        c_out_ref[...] = c_ref[...]
        n_out_ref[...] = n_ref[:, :, 0:1, :]


def _prompt_mixer(sinks, qkv, vt, og, cols, rows, qa, kx, vx, batch, seq):
    nblk = seq // BLOCK
    r3 = lambda a: a.reshape(batch, seq, a.shape[-1])
    cur = lambda w: pl.BlockSpec((batch, BLOCK, w), lambda j: (0, j, 0))
    prev = lambda w: pl.BlockSpec((batch, BLOCK, w), lambda j: (0, jnp.maximum(j - 1, 0), 0))
    state = lambda shape: pl.BlockSpec((batch,) + shape, lambda j: (0,) * (len(shape) + 1))
    out_shape = (
        jax.ShapeDtypeStruct((batch, seq, D_MODEL), BF16),
        jax.ShapeDtypeStruct((batch, M_HEADS, M_DIM, M_DIM), F32),
        jax.ShapeDtypeStruct((batch, M_HEADS, 1, M_DIM), F32),
    )
    y, pc, pn = pl.pallas_call(
        _prompt_mixer_kernel,
        out_shape=out_shape,
        grid=(nblk,),
        in_specs=[pl.BlockSpec(memory_space=pltpu.SMEM),
                  cur(3 * M_WIDTH),
                  pl.BlockSpec((batch, 1, M_HEADS, M_DIM, BLOCK), lambda j: (0, j, 0, 0, 0)),
                  cur(M_WIDTH), cur(128),
                  pl.BlockSpec((batch, 1, 3, 8, BLOCK), lambda j: (0, j, 0, 0, 0)),
                  cur(A_WIDTH), cur(4 * KV_WIDTH), cur(4 * KV_WIDTH), prev(4 * KV_WIDTH), prev(4 * KV_WIDTH)],
        out_specs=(cur(D_MODEL), state((M_HEADS, M_DIM, M_DIM)), state((M_HEADS, 1, M_DIM))),
        scratch_shapes=[pltpu.VMEM((batch, M_HEADS, M_DIM, M_DIM), F32),
                        pltpu.VMEM((batch, M_HEADS, 8, M_DIM), F32)],
        compiler_params=pltpu.CompilerParams(dimension_semantics=("arbitrary",),
                                             vmem_limit_bytes=VMEM_LIMIT_BYTES),
        name="prompt_mixer",
    )(sinks, r3(qkv), vt.reshape(batch, nblk, M_HEADS, M_DIM, BLOCK), r3(og), r3(cols), rows, r3(qa),
      r3(kx), r3(vx), r3(kx), r3(vx))
    return y.reshape(batch * seq, D_MODEL), pc, pn


def _sample_mlstm_kernel(bi_ref, bf_ref, q_ref, k_ref, v_ref, og_ref, gates_ref, m0_ref, n0_ref, c_ref,
                         y_ref, n_out_ref, m_out_ref, c_out_ref, decay_ref, qr_ref):
    h = pl.program_id(0)
    nb = q_ref.shape[0]
    q_rows = q_ref[...].astype(F32)
    k_rows = k_ref[...].astype(F32)
    qr_ref[...] = q_rows
    qt, kt, vt = q_rows.T, k_rows.T, v_ref[...].astype(F32).T
    i_pre = gates_ref[pl.ds(h, 1), :] + bi_ref[h]
    a = _log_sigmoid(gates_ref[pl.ds(M_HEADS + h, 1), :] + bf_ref[h]) + m0_ref[pl.ds(h, 1), :]
    m_t = jnp.maximum(a, i_pre)
    w_inter = jnp.exp(a - m_t)
    w_in = jnp.exp(i_pre - m_t)
    scores = jnp.sum(qt * kt, axis=0, keepdims=True) * w_in
    n0t = n0_ref[...].T
    nq = jnp.sum(n0t * qt, axis=0, keepdims=True)

    rows = lax.broadcasted_iota(jnp.int32, (nb, M_DIM), 0)
    cols = lax.broadcasted_iota(jnp.int32, (M_DIM, nb), 1)
    vw_t = (vt * w_in).astype(BF16)

    decay_ref[...] = jnp.broadcast_to(w_inter, (M_DIM, nb)).T

    def body(grp, cq_t):
        base = grp * SAMPLE_UNROLL
        for u in range(SAMPLE_UNROLL):
            b = base + u
            col = jnp.sum(c_ref[b, 0] * qr_ref[pl.ds(b, 1), :], axis=-1, keepdims=True)
            cq_t = jnp.where(cols == b, col, cq_t)
        outer = []
        for u in range(SAMPLE_UNROLL):
            k_only_b = jnp.where(rows == base + u, k_rows, 0.0).astype(BF16)
            outer.append(_dot(vw_t, k_only_b))
        for u in range(SAMPLE_UNROLL):
            b = base + u
            c_out_ref[b, 0] = decay_ref[pl.ds(b, 1), :] * c_ref[b, 0] + outer[u]
        return cq_t

    cq_t = lax.fori_loop(0, nb // SAMPLE_UNROLL, body, jnp.zeros((M_DIM, nb), F32))

    num = w_inter * cq_t + scores * vt
    den = w_inter * nq + scores
    hh = num / jnp.maximum(jnp.abs(den), jnp.exp(-m_t))
    hh = hh * lax.rsqrt(jnp.mean(hh * hh, axis=0, keepdims=True) + RMS_EPS)
    y_ref[...] = hh.T * og_ref[...]
    n_out_ref[...] = (w_inter * n0t + w_in * kt).T
    m_out_ref[0] = m_t


def _sample_mlstm(b_i, b_f, qkv, og, gates_t, m0_t, n0, c0):
    nb = qkv.shape[0]
    smem = pl.BlockSpec(memory_space=pltpu.SMEM)
    head = lambda off: pl.BlockSpec((nb, M_DIM), lambda h: (0, off + h))
    out_shape = (
        jax.ShapeDtypeStruct((nb, M_WIDTH), F32),
        jax.ShapeDtypeStruct((nb, M_WIDTH), F32),
        jax.ShapeDtypeStruct((M_HEADS, 1, nb), F32),
        jax.ShapeDtypeStruct((nb, M_HEADS, M_DIM, M_DIM), F32),
    )
    c_spec = pl.BlockSpec((nb, 1, M_DIM, M_DIM), lambda h: (0, h, 0, 0))
    return pl.pallas_call(
        _sample_mlstm_kernel,
        out_shape=out_shape,
        grid=(M_HEADS,),
        in_specs=[smem, smem, head(0), head(M_HEADS), head(2 * M_HEADS), head(0),
                  pl.BlockSpec(gates_t.shape, lambda h: (0, 0)), pl.BlockSpec(m0_t.shape, lambda h: (0, 0)),
                  head(0), c_spec],
        out_specs=(head(0), head(0), pl.BlockSpec((1, 1, nb), lambda h: (h, 0, 0)), c_spec),
        scratch_shapes=[pltpu.VMEM((nb, M_DIM), F32), pltpu.VMEM((nb, M_DIM), F32)],
        compiler_params=pltpu.CompilerParams(dimension_semantics=("arbitrary",),
                                             vmem_limit_bytes=VMEM_LIMIT_BYTES),
        name="sample_mlstm",
    )(b_i, b_f, qkv, qkv, qkv, og, gates_t, m0_t, n0, c0)


SAMPLE_TILE = 16
SAMPLE_UNROLL = 8


def _sample_swa_kernel(q2_ref, kc_ref, vc_ref, kn_ref, vn_ref, sink_ref, o_ref, ko_ref, vo_ref):
    sink = sink_ref[...] * LOG2E
    w = kc_ref.shape[2]
    tile = range(SAMPLE_TILE)
    newest = lax.broadcasted_iota(jnp.int32, (KV_WIDTH, w), 1) == w - 1
    pad = jnp.zeros((KV_WIDTH - SAMPLE_TILE, KV_WIDTH), F32)
    kn_t = jnp.concatenate([kn_ref[...], pad], axis=0).T
    vn_t = jnp.concatenate([vn_ref[...], pad], axis=0).T
    q2 = [q2_ref[b] for b in tile]
    k_new = [kn_ref[b:b + 1, :] for b in tile]
    v_new = [vn_ref[b:b + 1, :] for b in tile]
    s_c = [_dot(q2[b], kc_ref[b].astype(BF16)) for b in tile]
    s_n = [jnp.sum(q2[b].astype(F32) * k_new[b], axis=-1, keepdims=True) for b in tile]
    mx = [jnp.maximum(jnp.maximum(jnp.max(s_c[b], axis=-1, keepdims=True), s_n[b]), sink) for b in tile]
    p_c = [jnp.exp2(s_c[b] - mx[b]) for b in tile]
    p_n = [jnp.exp2(s_n[b] - mx[b]) for b in tile]
    denom = [jnp.sum(p_c[b], axis=-1, keepdims=True) + p_n[b] + jnp.exp2(sink - mx[b]) for b in tile]
    o = [_dot_nt(p_c[b].astype(BF16), vc_ref[b].astype(BF16)) for b in tile]
    for b in tile:
        o_ref[b] = (o[b] + p_n[b] * v_new[b]) / denom[b]
        ko_ref[b] = jnp.where(newest, kn_t[:, b:b + 1], pltpu.roll(kc_ref[b], w - 1, 1))
        vo_ref[b] = jnp.where(newest, vn_t[:, b:b + 1], pltpu.roll(vc_ref[b], w - 1, 1))


def _sample_swa(q2, k_cache, v_cache, k_new, v_new, sink_col):
    nb, _, w = k_cache.shape
    t3 = lambda a, c: pl.BlockSpec((SAMPLE_TILE, a, c), lambda i: (i, 0, 0))
    t2 = pl.BlockSpec((SAMPLE_TILE, KV_WIDTH), lambda i: (i, 0))
    out_shape = (
        jax.ShapeDtypeStruct((nb, A_HEADS, KV_WIDTH), F32),
        jax.ShapeDtypeStruct((nb, KV_WIDTH, w), F32),
        jax.ShapeDtypeStruct((nb, KV_WIDTH, w), F32),
    )
    return pl.pallas_call(
        _sample_swa_kernel,
        out_shape=out_shape,
        grid=(nb // SAMPLE_TILE,),
        in_specs=[t3(A_HEADS, KV_WIDTH), t3(KV_WIDTH, w), t3(KV_WIDTH, w), t2, t2,
                  pl.BlockSpec((A_HEADS, 1), lambda i: (0, 0))],
        out_specs=(t3(A_HEADS, KV_WIDTH), t3(KV_WIDTH, w), t3(KV_WIDTH, w)),
        compiler_params=pltpu.CompilerParams(dimension_semantics=("arbitrary",)),
        name="sample_swa",
    )(q2, k_cache, v_cache, k_new, v_new, sink_col)


def kernel(x_prompt, x_sample, cache_swa_k, cache_swa_v, state_mlstm_C, state_mlstm_n, state_mlstm_m,
           ffn1_norm, ffn1_w_gate, ffn1_w_up, ffn1_w_down, mix_norm, w_in, mlstm_b_i, mlstm_b_f,
           mlstm_out_norm, swa_q_norm, swa_k_norm, swa_sinks, w_out, ffn2_norm, ffn2_w_gate,
           ffn2_w_up, ffn2_w_down):
    depth = ffn1_norm.shape[0]
    assert depth == 1
    batch, seq, _ = x_prompt.shape
    nb = x_sample.shape[0]
    assert x_sample.shape[1] == 1 and seq % BLOCK == 0

    front_weights = (ffn1_w_gate[0], ffn1_w_up[0], ffn1_w_down[0], jnp.transpose(w_in[0]))
    n1 = ffn1_norm[0].reshape(1, D_MODEL)
    n2 = mix_norm[0].reshape(1, D_MODEL)
    n3 = ffn2_norm[0].reshape(1, D_MODEL)
    qgain = jnp.tile(swa_q_norm[0], A_HEADS).reshape(1, A_WIDTH)
    kgain = jnp.tile(swa_k_norm[0], KV_HEADS).reshape(1, KV_WIDTH)
    ogain = mlstm_out_norm[0].reshape(1, M_WIDTH)
    b_i, b_f = mlstm_b_i[0], mlstm_b_f[0]
    bias8 = jnp.concatenate([b_i, b_f]).reshape(2 * M_HEADS, 1)
    front_vectors = (n1, n2, qgain, kgain, ogain, bias8)
    sinks = swa_sinks[0]

    xp = x_prompt.reshape(batch * seq, D_MODEL)
    xs = x_sample.reshape(nb, D_MODEL)
    back_weights = (w_out[0], ffn2_w_gate[0], ffn2_w_up[0], ffn2_w_down[0])
    prompt_front, sample_front, _ = _front(xp, xs, front_vectors, front_weights, (), tm=512, seq=seq)
    x1, qkv, og, qa, ka, va, vt, kx, vx, rows, cols, pm = prompt_front
    x1s, qkvs, ogs, qas, kas, vas = sample_front[:6]
    gates_t = sample_front[-1]

    rows = rows.reshape(batch, seq // BLOCK, 3, 8, BLOCK)
    y, pc, pn = _prompt_mixer(sinks, qkv, vt, og, cols, rows, qa, kx, vx, batch, seq)
    to_window_major = lambda c, n: jnp.transpose(c.reshape(n, KV_HEADS, A_DIM, -1), (0, 3, 1, 2))[None]
    pk, pv = to_window_major(ka, batch), to_window_major(va, batch)
    pn = pn[:, :, 0, :]
    pm = pm[:, 0:M_HEADS, 0]

    y_m, sn, mt, sc = _sample_mlstm(b_i, b_f, qkvs, ogs, gates_t, jnp.transpose(state_mlstm_m[0]),
                                    state_mlstm_n[0].reshape(nb, M_WIDTH), state_mlstm_C[0])
    sn = sn.reshape(nb, M_HEADS, M_DIM)
    sm = jnp.transpose(mt[:, 0, :])

    qa_h = qas.reshape(nb, A_HEADS, A_DIM)
    zeros = jnp.zeros_like(qa_h)
    in_lo = (jnp.arange(A_HEADS) // A_GROUP == 0)[None, :, None]
    q2 = jnp.concatenate([jnp.where(in_lo, qa_h, zeros), jnp.where(in_lo, zeros, qa_h)], axis=-1)
    to_feature_major = lambda c: jnp.transpose(c[0], (0, 2, 3, 1)).reshape(nb, KV_WIDTH, -1)
    kc, vc = to_feature_major(cache_swa_k), to_feature_major(cache_swa_v)
    o2, sk, sv = _sample_swa(q2, kc, vc, kas, vas, sinks.reshape(A_HEADS, 1))
    o2 = o2.reshape(nb, A_HEADS, KV_HEADS, A_DIM)
    y_a = jnp.where(in_lo, o2[:, :, 0, :], o2[:, :, 1, :]).reshape(nb, A_WIDTH)
    ys_in = jnp.concatenate([y_m, y_a], axis=-1).astype(BF16)

    yp, ys = _back(x1, y, x1s, ys_in, n3, back_weights, tm=1024)
    yp = yp.reshape(batch, seq, D_MODEL)

    return (yp, ys, pk, pv, pc[None], pn[None], pm[None],
            to_window_major(sk, nb), to_window_major(sv, nb), sc[None], sn[None], sm[None])
```

```python
import functools
from typing import NamedTuple

import jax
import jax.numpy as jnp
from jax import lax
from jax.experimental import pallas as pl
from jax.experimental.pallas import tpu as pltpu

F32 = jnp.float32
BF16 = jnp.bfloat16

D_MODEL = 1024
D_FF = 2816
FF_CHUNK = 256
N_FF_CHUNKS = D_FF // FF_CHUNK
M_HEADS = 4
M_DIM = 128
M_WIDTH = M_HEADS * M_DIM
A_HEADS = 8
A_DIM = 64
A_WIDTH = A_HEADS * A_DIM
KV_HEADS = 2
KV_WIDTH = KV_HEADS * A_DIM
A_GROUP = A_HEADS // KV_HEADS
WINDOW = 128
BLOCK = 128
GATE_PAD = 128
RMS_EPS = 1e-6
FFN_RES_WEIGHT = 0.5
NEG_INF = float("-inf")
LOG2E = 1.4426950408889634
VMEM_LIMIT_BYTES = 56 * 1024 * 1024
FRONT_VMEM_LIMIT_BYTES = 60 * 1024 * 1024


def _dot(a, b):
    return jnp.dot(a, b, preferred_element_type=F32)


def _dot_nt(a, b):
    return lax.dot_general(a, b, (((1,), (1,)), ((), ())), preferred_element_type=F32)


def _rms_rows(x, gain):
    ms = jnp.mean(x * x, axis=-1, keepdims=True)
    return x * lax.rsqrt(ms + RMS_EPS) * gain


def _log_sigmoid(x):
    return jnp.minimum(x, 0.0) - jnp.log1p(jnp.exp(-jnp.abs(x)))


def _half_tile_mean_sq(x):
    in_lo = lax.broadcasted_iota(jnp.int32, (x.shape[0], 128), 1) < A_DIM
    out = []
    for c in range(x.shape[1] // 128):
        sq = x[:, c * 128:(c + 1) * 128]
        sq = sq * sq
        s_lo = jnp.sum(jnp.where(in_lo, sq, 0.0), axis=-1, keepdims=True)
        s_hi = jnp.sum(jnp.where(in_lo, 0.0, sq), axis=-1, keepdims=True)
        out.append(jnp.where(in_lo, s_lo, s_hi) * (1.0 / A_DIM))
    return out[0] if len(out) == 1 else jnp.concatenate(out, axis=1)


def _ffn(h_ref, wg_ref, wu_ref, wd_ref, act_ref, weights_ready=None):
    for c in range(N_FF_CHUNKS):
        lo, hi = c * FF_CHUNK, (c + 1) * FF_CHUNK
        if weights_ready is not None:
            weights_ready("gate_up", c)
        h = h_ref[...]
        g = _dot(h, wg_ref[:, lo:hi])
        u = _dot(h, wu_ref[:, lo:hi])
        act_ref[:, lo:hi] = (g * jax.nn.sigmoid(g) * u).astype(BF16)
    if weights_ready is not None:
        weights_ready("down", 0)
    return _dot(act_ref[...], wd_ref[...])


N_FRONT_VECTORS = 6
N_FRONT_WEIGHTS = 4
N_FRONT_COMMON_OUTS = 9
N_PROMPT_GATE_OUTS = 3
W_M_ROWS = 4 * M_WIDTH
W_GATE_ROWS = 2 * M_HEADS
WA_KVG = A_WIDTH
WA_GATES = WA_KVG + 2 * KV_WIDTH
WA_ROWS = WA_GATES + GATE_PAD
CAST_ROWS, CAST_COLS = D_MODEL, FF_CHUNK
CAST_SLOTS = 6


class _CastJob(NamedTuple):
    src: object
    src_row: int
    dst: object
    dst_row: int
    col: int
    rows: int
    cols: int
    dst_rows: int


def _cast_jobs(src, dst, src_row=0, dst_row=0, rows=None, dst_rows=None):
    rows = src.shape[0] - src_row if rows is None else rows
    jobs = []
    for r in range(0, rows, CAST_ROWS):
        nr = min(CAST_ROWS, rows - r)
        for c in range(0, src.shape[1], CAST_COLS):
            nc = min(CAST_COLS, src.shape[1] - c)
            jobs.append(_CastJob(src, src_row + r, dst, dst_row + r, c, nr, nc, nr if dst_rows is None else dst_rows))
    return jobs


class _CastRing:
    def __init__(self, jobs, stage_ref, sem_ref):
        self.jobs, self.stage_ref, self.sem_ref = jobs, stage_ref, sem_ref
        self.ahead = CAST_SLOTS - 1
        self.done = 0
        for j in range(min(self.ahead, len(jobs))):
            self._copy(j).start()

    def _copy(self, j):
        job, slot = self.jobs[j], j % CAST_SLOTS
        return pltpu.make_async_copy(job.src.at[pl.ds(job.src_row, job.rows), pl.ds(job.col, job.cols)],
                                     self.stage_ref.at[slot, pl.ds(0, job.rows), pl.ds(0, job.cols)],
                                     self.sem_ref.at[slot])

    def finish(self, upto):
        for j in range(self.done, upto):
            job = self.jobs[j]
            if j + self.ahead < len(self.jobs):
                self._copy(j + self.ahead).start()
            self._copy(j).wait()
            block = self.stage_ref[j % CAST_SLOTS, 0:job.rows, 0:job.cols]
            if job.dst_rows > job.rows:
                block = jnp.concatenate([block, jnp.zeros((job.dst_rows - job.rows, job.cols), F32)], axis=0)
            job.dst[job.dst_row:job.dst_row + job.dst_rows, job.col:job.col + job.cols] = block.astype(BF16)
        self.done = max(self.done, upto)


def _prompt_gates(gates_t, bias_ref, first_of_seq, m_ref, rows_ref, cols_ref, mfin_ref):
    sub = lax.broadcasted_iota(jnp.int32, (8, BLOCK), 0)
    lane = lax.broadcasted_iota(jnp.int32, (8, BLOCK), 1)
    is_head = sub < M_HEADS
    pad = jnp.zeros((BLOCK - 24, BLOCK), F32)
    scan_shifts = (1, 2, 4, 8, 16, 32, 64)
    m_prev = jnp.where(first_of_seq, 0.0, m_ref[...])[:, 0:1]
    for c in range(rows_ref.shape[0]):
        pre = gates_t[:, c * BLOCK:(c + 1) * BLOCK] + bias_ref[...]
        r = jnp.where(is_head, pre, _log_sigmoid(pre))
        cum = r
        for shift in scan_shifts:
            cum = cum + jnp.where(lane >= shift, pltpu.roll(cum, shift, 1), 0.0)
        bcum = pltpu.roll(cum, M_HEADS, 0)
        g = jnp.where(is_head, r - bcum, 0.0)
        bcum = jnp.where(is_head, bcum, 0.0)
        cm = g
        for shift in scan_shifts:
            cm = jnp.maximum(cm, jnp.where(lane >= shift, pltpu.roll(cm, shift, 1), NEG_INF))
        cm_last = jnp.max(cm, axis=-1, keepdims=True)
        b_last = jnp.sum(jnp.where(lane == BLOCK - 1, bcum, 0.0), axis=-1, keepdims=True)
        mx = jnp.maximum(m_prev, cm)
        mx_last = jnp.maximum(m_prev, cm_last)
        rows_ref[c, 0] = g * LOG2E
        rows_ref[c, 1] = jnp.exp(g - mx_last)
        rows_ref[c, 2] = jnp.broadcast_to(jnp.exp(m_prev - mx_last), (8, BLOCK))
        col_src = jnp.concatenate([mx * -LOG2E, jnp.exp(m_prev - mx), jnp.exp(-(bcum + mx)), pad], axis=0)
        cols_ref[c * BLOCK:(c + 1) * BLOCK, :] = col_src.T
        m_prev = b_last + mx_last
    m_full = jnp.broadcast_to(m_prev, (8, BLOCK))
    m_ref[...] = m_full
    mfin_ref[0] = m_full


def _front_tile(x_ref, params, outs, gate_sink, h_ref, act_ref, cache_feature_major, weights_ready=None):
    (n1_ref, wg_ref, wu_ref, wd_ref, n2_ref, wm_ref, wa_ref, qgain_ref, kgain_ref, ogain_ref, _) = params
    (x1_ref, qkv_ref, og_ref, qa_ref, ka_ref, va_ref, vt_ref, kx_ref, vx_ref) = outs
    x = x_ref[...]
    h_ref[...] = _rms_rows(x, n1_ref[...]).astype(BF16)
    x1 = x + FFN_RES_WEIGHT * _ffn(h_ref, wg_ref, wu_ref, wd_ref, act_ref, weights_ready)
    x1_ref[...] = x1
    h_ref[...] = _rms_rows(x1, n2_ref[...]).astype(BF16)
    if weights_ready is not None:
        weights_ready("projection", 0)
    h = h_ref[...]
    project = lambda w_ref, lo, hi: _dot_nt(h, w_ref[lo:hi, :])

    qa = project(wa_ref, 0, WA_KVG)
    kvg = project(wa_ref, WA_KVG, WA_ROWS)
    v_m = project(wm_ref, 2 * M_WIDTH, 3 * M_WIDTH)
    qkv_ref[:, 2 * M_WIDTH:3 * M_WIDTH] = v_m.astype(BF16)
    for c in range(vt_ref.shape[0]):
        for hd in range(M_HEADS):
            blk = v_m[c * BLOCK:(c + 1) * BLOCK, hd * M_DIM:(hd + 1) * M_DIM]
            vt_ref[c, hd] = blk.T.astype(BF16)
    q_scale = qgain_ref[...] * (A_DIM ** -0.5 * LOG2E)
    qa_ref[...] = (qa * lax.rsqrt(_half_tile_mean_sq(qa) + RMS_EPS) * q_scale).astype(BF16)
    ka = kvg[:, 0:KV_WIDTH]
    ka = ka * lax.rsqrt(_half_tile_mean_sq(ka) + RMS_EPS) * kgain_ref[...]
    va = kvg[:, KV_WIDTH:2 * KV_WIDTH]
    for src, dst in ((ka, ka_ref), (va, va_ref)):
        if cache_feature_major:
            dst[...] = src[src.shape[0] - dst.shape[1]:, :].T
        else:
            dst[...] = src[src.shape[0] - dst.shape[0]:, :]
    gate_sink(kvg[:, 2 * KV_WIDTH:2 * KV_WIDTH + GATE_PAD].T[0:2 * M_HEADS, :])

    in_lo = lax.broadcasted_iota(jnp.int32, ka.shape, 1) < A_DIM
    for src, dst in ((ka, kx_ref), (va, vx_ref)):
        x0 = jnp.where(in_lo, src, 0.0)
        x1 = jnp.where(in_lo, 0.0, src)
        dst[:, 0:128] = x0.astype(BF16)
        dst[:, 128:256] = pltpu.roll(x0, A_DIM, 1).astype(BF16)
        dst[:, 256:384] = pltpu.roll(x1, A_DIM, 1).astype(BF16)
        dst[:, 384:512] = x1.astype(BF16)

    og_ref[...] = jax.nn.sigmoid(project(wm_ref, 3 * M_WIDTH, 4 * M_WIDTH)) * ogain_ref[...]
    k_m = project(wm_ref, M_WIDTH, 2 * M_WIDTH) * (M_DIM ** -0.5)
    qkv_ref[:, M_WIDTH:2 * M_WIDTH] = k_m.astype(BF16)
    qkv_ref[:, 0:M_WIDTH] = project(wm_ref, 0, M_WIDTH).astype(BF16)


def _front_kernel(*refs, n_tiles, tiles_per_seq, n_cast):
    refs = list(refs)
    take = lambda k: [refs.pop(0) for _ in range(k)]
    x_ref, xs_ref = take(2)
    n1_ref, n2_ref, qgain_ref, kgain_ref, ogain_ref, bias_ref = take(N_FRONT_VECTORS)
    wg_hbm, wu_hbm, wd_hbm, wt_hbm = take(N_FRONT_WEIGHTS)
    cast_src = take(n_cast)
    outs_p = take(N_FRONT_COMMON_OUTS)
    rows_ref, cols_ref, mfin_ref = take(N_PROMPT_GATE_OUTS)
    outs_s = take(N_FRONT_COMMON_OUTS)
    (gts_ref,) = take(1)
    cast_dst = take(n_cast)
    h_ref, act_ref, m_ref, wg_ref, wu_ref, wd_ref, wm_ref, wa_ref, stage_ref, sem_ref = refs
    params = (n1_ref, wg_ref, wu_ref, wd_ref, n2_ref, wm_ref, wa_ref, qgain_ref, kgain_ref, ogain_ref, bias_ref)
    i = pl.program_id(0)

    def prompt_tile(weights_ready):
        for src, dst in zip(cast_src, cast_dst):
            dst[...] = src[...].astype(BF16)
        gates = functools.partial(_prompt_gates, bias_ref=bias_ref, first_of_seq=i % tiles_per_seq == 0,
                                  m_ref=m_ref, rows_ref=rows_ref, cols_ref=cols_ref, mfin_ref=mfin_ref)
        _front_tile(x_ref, params, outs_p, gates, h_ref, act_ref, True, weights_ready)

    @pl.when(i == 0)
    def _():
        gate_up = [job for pair in zip(_cast_jobs(wg_hbm, wg_ref), _cast_jobs(wu_hbm, wu_ref)) for job in pair]
        down = _cast_jobs(wd_hbm, wd_ref)
        a_src = W_M_ROWS + W_GATE_ROWS
        projection = (_cast_jobs(wt_hbm, wm_ref, rows=W_M_ROWS)
                      + _cast_jobs(wt_hbm, wa_ref, src_row=a_src, rows=WA_GATES)
                      + _cast_jobs(wt_hbm, wa_ref, src_row=W_M_ROWS, dst_row=WA_GATES, rows=W_GATE_ROWS,
                                   dst_rows=GATE_PAD))
        assert len(gate_up) == 2 * N_FF_CHUNKS
        ring = _CastRing(gate_up + down + projection, stage_ref, sem_ref)
        upto = {"gate_up": lambda c: 2 * (c + 1), "down": lambda c: len(gate_up) + len(down),
                "projection": lambda c: len(ring.jobs)}
        prompt_tile(lambda stage, c: ring.finish(upto[stage](c)))

    @pl.when((i > 0) & (i < n_tiles))
    def _():
        prompt_tile(None)

    @pl.when(i == n_tiles)
    def _():
        ns = xs_ref.shape[0]

        def raw_gates(gates_t):
            gts_ref[...] = gates_t

        _front_tile(xs_ref, params, outs_s, raw_gates, h_ref.at[0:ns], act_ref.at[0:ns], False)


def _const_spec(shape):
    nd = len(shape)
    return pl.BlockSpec(shape, lambda i: (0,) * nd, pipeline_mode=pl.Buffered(1))


def _whole_spec(shape):
    nd = len(shape)
    return pl.BlockSpec(shape, lambda i: (0,) * nd)


def _front_out_shapes(n, cache_shape):
    return (
        jax.ShapeDtypeStruct((n, D_MODEL), F32),
        jax.ShapeDtypeStruct((n, 3 * M_WIDTH), BF16),
        jax.ShapeDtypeStruct((n, M_WIDTH), F32),
        jax.ShapeDtypeStruct((n, A_WIDTH), BF16),
        jax.ShapeDtypeStruct(cache_shape, F32),
        jax.ShapeDtypeStruct(cache_shape, F32),
        jax.ShapeDtypeStruct((n // BLOCK, M_HEADS, M_DIM, BLOCK), BF16),
        jax.ShapeDtypeStruct((n, 4 * KV_WIDTH), BF16),
        jax.ShapeDtypeStruct((n, 4 * KV_WIDTH), BF16),
    )


def _cast_chunk_count(n_rows, max_chunks):
    for k in range(max_chunks, 0, -1):
        if n_rows % k == 0 and (n_rows // k) % 16 == 0:
            return k
    raise ValueError(n_rows)


def _front(x2d, xs2d, vectors, weights, later_weights, tm, seq):
    n, ns = x2d.shape[0], xs2d.shape[0]
    assert seq % tm == 0 and tm >= WINDOW
    assert len(vectors) == N_FRONT_VECTORS and len(weights) == N_FRONT_WEIGHTS
    assert weights[3].shape == (W_M_ROWS + W_GATE_ROWS + WA_GATES, D_MODEL)
    n_tiles = n // tm
    n_seqs = n // seq
    tiles_per_seq = seq // tm
    nb_t = tm // BLOCK
    tile = lambda i: jnp.minimum(i, n_tiles - 1)
    seq_of = lambda i: tile(i) // tiles_per_seq
    row = lambda w: pl.BlockSpec((tm, w), lambda i: (tile(i), 0))
    tail = pl.BlockSpec((KV_WIDTH, WINDOW), lambda i: (seq_of(i), 0))
    prompt_specs = (row(D_MODEL), row(3 * M_WIDTH), row(M_WIDTH), row(A_WIDTH), tail, tail,
                    pl.BlockSpec((nb_t, M_HEADS, M_DIM, BLOCK), lambda i: (tile(i), 0, 0, 0)),
                    row(4 * KV_WIDTH), row(4 * KV_WIDTH),
                    pl.BlockSpec((nb_t, 3, 8, BLOCK), lambda i: (tile(i), 0, 0, 0)), row(128),
                    pl.BlockSpec((1, 8, BLOCK), lambda i: (seq_of(i), 0, 0)))
    prompt_shapes = _front_out_shapes(n, (n_seqs * KV_WIDTH, WINDOW)) + (
        jax.ShapeDtypeStruct((n // BLOCK, 3, 8, BLOCK), F32),
        jax.ShapeDtypeStruct((n, 128), F32),
        jax.ShapeDtypeStruct((n_seqs, 8, BLOCK), F32),
    )
    sample_shapes = _front_out_shapes(ns, (ns, KV_WIDTH)) + (
        jax.ShapeDtypeStruct((2 * M_HEADS, ns), F32),
    )

    def chunk_spec(w):
        k = _cast_chunk_count(w.shape[0], n_tiles)
        return pl.BlockSpec((w.shape[0] // k, w.shape[1]), lambda i: (jnp.minimum(i, k - 1), 0))

    cast_specs = [chunk_spec(w) for w in later_weights]
    cast_shapes = tuple(jax.ShapeDtypeStruct(w.shape, BF16) for w in later_weights)
    outs = pl.pallas_call(
        functools.partial(_front_kernel, n_tiles=n_tiles, tiles_per_seq=tiles_per_seq, n_cast=len(later_weights)),
        out_shape=prompt_shapes + sample_shapes + cast_shapes,
        grid=(n_tiles + 1,),
        in_specs=([row(D_MODEL), _whole_spec(xs2d.shape)] + [_const_spec(v.shape) for v in vectors]
                  + [pl.BlockSpec(memory_space=pl.ANY)] * len(weights) + cast_specs),
        out_specs=prompt_specs + tuple(_whole_spec(s.shape) for s in sample_shapes) + tuple(cast_specs),
        scratch_shapes=[pltpu.VMEM((tm, D_MODEL), BF16), pltpu.VMEM((tm, D_FF), BF16),
                        pltpu.VMEM((8, BLOCK), F32),
                        pltpu.VMEM((D_MODEL, D_FF), BF16), pltpu.VMEM((D_MODEL, D_FF), BF16),
                        pltpu.VMEM((D_FF, D_MODEL), BF16),
                        pltpu.VMEM((W_M_ROWS, D_MODEL), BF16), pltpu.VMEM((WA_ROWS, D_MODEL), BF16),
                        pltpu.VMEM((CAST_SLOTS, CAST_ROWS, CAST_COLS), F32),
                        pltpu.SemaphoreType.DMA((CAST_SLOTS,))],
        compiler_params=pltpu.CompilerParams(dimension_semantics=("arbitrary",),
                                             vmem_limit_bytes=FRONT_VMEM_LIMIT_BYTES),
        name="front",
    )(x2d, xs2d, *vectors, *weights, *later_weights)
    n_p, n_s = len(prompt_shapes), len(sample_shapes)
    return outs[:n_p], outs[n_p:n_p + n_s], outs[n_p + n_s:]


def _back_tile(x1_ref, y_ref, params, out_ref, h_ref, act_ref):
    wo_ref, n_ref, wg_ref, wu_ref, wd_ref = params
    x2 = x1_ref[...] + _dot(y_ref[...], wo_ref[...])
    h_ref[...] = _rms_rows(x2, n_ref[...]).astype(BF16)
    out_ref[...] = x2 + FFN_RES_WEIGHT * _ffn(h_ref, wg_ref, wu_ref, wd_ref, act_ref)


def _back_kernel(x1_ref, y_ref, x1s_ref, ys_ref, wo_ref, n_ref, wg_ref, wu_ref, wd_ref, out_ref, outs_ref,
                 h_ref, act_ref, *, n_tiles):
    params = (wo_ref, n_ref, wg_ref, wu_ref, wd_ref)
    i = pl.program_id(0)

    @pl.when(i < n_tiles)
    def _():
        _back_tile(x1_ref, y_ref, params, out_ref, h_ref, act_ref)

    @pl.when(i == n_tiles)
    def _():
        ns = x1s_ref.shape[0]
        _back_tile(x1s_ref, ys_ref, params, outs_ref.at[:, 0], h_ref.at[0:ns], act_ref.at[0:ns])


def _back(x1, y, x1s, ys, params, tm):
    n, ns = x1.shape[0], x1s.shape[0]
    n_tiles = n // tm
    row = pl.BlockSpec((tm, D_MODEL), lambda i: (jnp.minimum(i, n_tiles - 1), 0))
    return pl.pallas_call(
        functools.partial(_back_kernel, n_tiles=n_tiles),
        out_shape=(jax.ShapeDtypeStruct((n, D_MODEL), F32), jax.ShapeDtypeStruct((ns, 1, D_MODEL), F32)),
        grid=(n_tiles + 1,),
        in_specs=[row, row, _whole_spec(x1s.shape), _whole_spec(ys.shape)] + [_const_spec(p.shape) for p in params],
        out_specs=(row, _whole_spec((ns, 1, D_MODEL))),
        scratch_shapes=[pltpu.VMEM((tm, D_MODEL), BF16), pltpu.VMEM((tm, D_FF), BF16)],
        compiler_params=pltpu.CompilerParams(dimension_semantics=("arbitrary",),
                                             vmem_limit_bytes=VMEM_LIMIT_BYTES),
        name="back",
    )(x1, y, x1s, ys, *params)


def _prompt_mixer_kernel(sinks_ref, qkv_ref, vt_ref, og_ref, cols_ref, rows_ref, qa_ref,
                         kx_ref, vx_ref, kxp_ref, vxp_ref, *rest):
    n_cast = (len(rest) - 5) // 2
    cast_src, rest = rest[:n_cast], rest[n_cast:]
    y_ref, c_out_ref, n_out_ref = rest[:3]
    cast_dst = rest[3:3 + n_cast]
    c_ref, n_ref = rest[3 + n_cast:]
    for src, dst in zip(cast_src, cast_dst):
        dst[...] = src[...].astype(BF16)
    j = pl.program_id(0)
    batch = qkv_ref.shape[0]

    @pl.when(j == 0)
    def _():
        c_ref[...] = jnp.zeros_like(c_ref)
        n_ref[...] = jnp.zeros_like(n_ref)

    rows = lax.broadcasted_iota(jnp.int32, (BLOCK, BLOCK), 0)
    cols = lax.broadcasted_iota(jnp.int32, (BLOCK, BLOCK), 1)
    causal = cols <= rows
    lane_lo = lax.broadcasted_iota(jnp.int32, (2 * BLOCK, KV_WIDTH), 1) < A_DIM
    ones_m = jnp.ones((BLOCK, M_DIM), BF16)
    ones_half = (jnp.where(lane_lo, 1.0, 0.0).astype(BF16), jnp.where(lane_lo, 0.0, 1.0).astype(BF16))
    qi = lax.broadcasted_iota(jnp.int32, (2 * BLOCK, 2 * BLOCK), 0) % BLOCK
    kc = lax.broadcasted_iota(jnp.int32, (2 * BLOCK, 2 * BLOCK), 1)
    first_valid = jnp.where(j == 0, BLOCK, 0)
    valid = (kc >= qi) & (kc <= qi + WINDOW) & (kc >= first_valid)
    top_rows = lax.broadcasted_iota(jnp.int32, (2 * BLOCK, 1), 0) < BLOCK

    heads = range(M_HEADS)
    groups = [(kv, parity) for kv in range(KV_HEADS) for parity in range(2)]
    m_lo = lambda h: h * M_DIM

    def first_matmuls(b):
        st = {"col": cols_ref[b]}
        k = [qkv_ref[b, :, M_WIDTH + m_lo(h):M_WIDTH + m_lo(h) + M_DIM] for h in heads]
        st["c_prev"] = [c_ref[b, h] for h in heads]
        st["n_prev"] = [n_ref[b, h] for h in heads]
        st["qkc"], st["upd"], st["sc"] = [], [], []
        for h in heads:
            q = qkv_ref[b, :, m_lo(h):m_lo(h) + M_DIM]
            n_rep = jnp.broadcast_to(st["n_prev"][h][0:1, :], (BLOCK, M_DIM)).astype(BF16)
            rhs = jnp.concatenate([k[h], st["c_prev"][h].astype(BF16), n_rep], axis=0)
            st["qkc"].append(_dot_nt(q, rhs))
        for h in heads:
            wend_row = rows_ref[b, 0, 1][h:h + 1, :]
            vw_t = (vt_ref[b, 0, h].astype(F32) * wend_row).astype(BF16)
            w_rep = jnp.broadcast_to(wend_row, (16, BLOCK)).astype(BF16)
            st["upd"].append(_dot(jnp.concatenate([vw_t, w_rep], axis=0), k[h]))
        q2 = [jnp.concatenate([qa_ref[b, :, (2 * kv) * 128:(2 * kv + 1) * 128],
                               qa_ref[b, :, (2 * kv + 1) * 128:(2 * kv + 2) * 128]], axis=0)
              for kv in range(KV_HEADS)]
        for kv, parity in groups:
            var = (2 * kv + parity) * KV_WIDTH
            k_band = jnp.concatenate([kxp_ref[b, :, var:var + KV_WIDTH], kx_ref[b, :, var:var + KV_WIDTH]], axis=0)
            st["sc"].append(_dot_nt(q2[kv], k_band))
        return st

    def vector_work(b, st):
        st["s"], st["pr"], st["e_sink"] = [], [], []
        for h in heads:
            g_row = rows_ref[b, 0, 0][h:h + 1, :]
            decay = rows_ref[b, 0, 2][h:h + 1, :]
            d = jnp.where(causal, jnp.exp2(st["col"][:, h:h + 1] + g_row), 0.0)
            st["s"].append((st["qkc"][h][:, 0:BLOCK] * d).astype(BF16))
            c_ref[b, h] = decay * st["c_prev"][h] + st["upd"][h][0:M_DIM]
            n_ref[b, h] = decay * st["n_prev"][h] + st["upd"][h][M_DIM:M_DIM + 8]
        for gi, (kv, parity) in enumerate(groups):
            sink = jnp.where(top_rows, sinks_ref[4 * kv + parity], sinks_ref[4 * kv + 2 + parity]) * LOG2E
            scm = jnp.where(valid, st["sc"][gi], NEG_INF)
            mx = jnp.maximum(jnp.max(scm, axis=-1, keepdims=True), sink)
            st["pr"].append(jnp.exp2(scm - mx).astype(BF16))
            st["e_sink"].append(jnp.exp2(sink - mx))

    def second_matmuls(b, st):
        st["sv"], st["pv"] = [], []
        for h in heads:
            v = qkv_ref[b, :, 2 * M_WIDTH + m_lo(h):2 * M_WIDTH + m_lo(h) + M_DIM]
            st["sv"].append(_dot(st["s"][h], jnp.concatenate([v, ones_m], axis=1)))
        for gi, (kv, parity) in enumerate(groups):
            var = (2 * kv + parity) * KV_WIDTH
            v_band = jnp.concatenate([vxp_ref[b, :, var:var + KV_WIDTH], vx_ref[b, :, var:var + KV_WIDTH]], axis=0)
            st["pv"].append(_dot(st["pr"][gi], jnp.concatenate([v_band, ones_half[parity]], axis=1)))

    def normalise(b, st):
        col_b, qkc, sv, pv, e_sink = st["col"], st["qkc"], st["sv"], st["pv"], st["e_sink"]
        for h in heads:
            lo, hi = m_lo(h), m_lo(h) + M_DIM
            wi_col = col_b[:, 8 + h:9 + h]
            num = wi_col * qkc[h][:, BLOCK:2 * BLOCK] + sv[h][:, 0:M_DIM]
            den = wi_col * qkc[h][:, 2 * BLOCK:3 * BLOCK] + sv[h][:, M_DIM:2 * M_DIM]
            hh = num / jnp.maximum(jnp.abs(den), col_b[:, 16 + h:17 + h])
            hh = hh * lax.rsqrt(jnp.mean(hh * hh, axis=-1, keepdims=True) + RMS_EPS)
            y_ref[b, :, lo:hi] = (hh * og_ref[b, :, lo:hi]).astype(BF16)
        for kv in range(KV_HEADS):
            acc = pv[2 * kv] + pv[2 * kv + 1]
            denom = acc[:, KV_WIDTH:] + jnp.where(lane_lo, e_sink[2 * kv], e_sink[2 * kv + 1])
            ya = (acc[:, 0:KV_WIDTH] / denom).astype(BF16)
            p0, p1 = 2 * kv, 2 * kv + 1
            y_ref[b, :, M_WIDTH + p0 * 128:M_WIDTH + (p0 + 1) * 128] = ya[0:BLOCK]
            y_ref[b, :, M_WIDTH + p1 * 128:M_WIDTH + (p1 + 1) * 128] = ya[BLOCK:2 * BLOCK]

    for b in range(batch):
        st = first_matmuls(b)
        vector_work(b, st)
        second_matmuls(b, st)
        normalise(b, st)

    @pl.when(j == pl.num_programs(0) - 1)
    def _():
        c_out_ref[...] = c_ref[...]
        n_out_ref[...] = n_ref[:, :, 0:1, :]


def _prompt_mixer(sinks, qkv, vt, og, cols, rows, qa, kx, vx, later_weights, batch, seq):
    nblk = seq // BLOCK

    def chunk_spec(w):
        k = _cast_chunk_count(w.shape[0], nblk)
        return pl.BlockSpec((w.shape[0] // k, w.shape[1]), lambda j: (jnp.minimum(j, k - 1), 0))

    cast_specs = [chunk_spec(w) for w in later_weights]
    cast_shapes = tuple(jax.ShapeDtypeStruct(w.shape, BF16) for w in later_weights)
    r3 = lambda a: a.reshape(batch, seq, a.shape[-1])
    cur = lambda w: pl.BlockSpec((batch, BLOCK, w), lambda j: (0, j, 0))
    prev = lambda w: pl.BlockSpec((batch, BLOCK, w), lambda j: (0, jnp.maximum(j - 1, 0), 0))
    state = lambda shape: pl.BlockSpec((batch,) + shape, lambda j: (0,) * (len(shape) + 1))
    out_shape = (
        jax.ShapeDtypeStruct((batch, seq, D_MODEL), BF16),
        jax.ShapeDtypeStruct((batch, M_HEADS, M_DIM, M_DIM), F32),
        jax.ShapeDtypeStruct((batch, M_HEADS, 1, M_DIM), F32),
    )
    y, pc, pn, *casts = pl.pallas_call(
        _prompt_mixer_kernel,
        out_shape=out_shape + cast_shapes,
        grid=(nblk,),
        in_specs=[pl.BlockSpec(memory_space=pltpu.SMEM),
                  cur(3 * M_WIDTH),
                  pl.BlockSpec((batch, 1, M_HEADS, M_DIM, BLOCK), lambda j: (0, j, 0, 0, 0)),
                  cur(M_WIDTH), cur(128),
                  pl.BlockSpec((batch, 1, 3, 8, BLOCK), lambda j: (0, j, 0, 0, 0)),
                  cur(A_WIDTH), cur(4 * KV_WIDTH), cur(4 * KV_WIDTH), prev(4 * KV_WIDTH), prev(4 * KV_WIDTH)]
                 + cast_specs,
        out_specs=(cur(D_MODEL), state((M_HEADS, M_DIM, M_DIM)), state((M_HEADS, 1, M_DIM))) + tuple(cast_specs),
        scratch_shapes=[pltpu.VMEM((batch, M_HEADS, M_DIM, M_DIM), F32),
                        pltpu.VMEM((batch, M_HEADS, 8, M_DIM), F32)],
        compiler_params=pltpu.CompilerParams(dimension_semantics=("arbitrary",),
                                             vmem_limit_bytes=VMEM_LIMIT_BYTES),
        name="prompt_mixer",
    )(sinks, r3(qkv), vt.reshape(batch, nblk, M_HEADS, M_DIM, BLOCK), r3(og), r3(cols), rows, r3(qa),
      r3(kx), r3(vx), r3(kx), r3(vx), *later_weights)
    return y.reshape(batch * seq, D_MODEL), pc, pn, casts


def _sample_mlstm_kernel(bi_ref, bf_ref, q_ref, k_ref, v_ref, og_ref, gates_ref, m0_ref, n0_ref, c_ref,
                         y_ref, n_out_ref, m_out_ref, c_out_ref, decay_ref, qr_ref):
    h = pl.program_id(0)
    nb = q_ref.shape[0]
    q_rows = q_ref[...].astype(F32)
    k_rows = k_ref[...].astype(F32)
    qr_ref[...] = q_rows
    qt, kt, vt = q_rows.T, k_rows.T, v_ref[...].astype(F32).T
    i_pre = gates_ref[pl.ds(h, 1), :] + bi_ref[h]
    a = _log_sigmoid(gates_ref[pl.ds(M_HEADS + h, 1), :] + bf_ref[h]) + m0_ref[pl.ds(h, 1), :]
    m_t = jnp.maximum(a, i_pre)
    w_inter = jnp.exp(a - m_t)
    w_in = jnp.exp(i_pre - m_t)
    scores = jnp.sum(qt * kt, axis=0, keepdims=True) * w_in
    n0t = n0_ref[...].T
    nq = jnp.sum(n0t * qt, axis=0, keepdims=True)

    rows = lax.broadcasted_iota(jnp.int32, (nb, M_DIM), 0)
    cols = lax.broadcasted_iota(jnp.int32, (M_DIM, nb), 1)
    vw_t = (vt * w_in).astype(BF16)

    decay_ref[...] = jnp.broadcast_to(w_inter, (M_DIM, nb)).T

    def body(grp, cq_t):
        base = grp * SAMPLE_UNROLL
        for u in range(SAMPLE_UNROLL):
            b = base + u
            col = jnp.sum(c_ref[b, 0] * qr_ref[pl.ds(b, 1), :], axis=-1, keepdims=True)
            cq_t = jnp.where(cols == b, col, cq_t)
        outer = []
        for u in range(SAMPLE_UNROLL):
            k_only_b = jnp.where(rows == base + u, k_rows, 0.0).astype(BF16)
            outer.append(_dot(vw_t, k_only_b))
        for u in range(SAMPLE_UNROLL):
            b = base + u
            c_out_ref[b, 0] = decay_ref[pl.ds(b, 1), :] * c_ref[b, 0] + outer[u]
        return cq_t

    cq_t = lax.fori_loop(0, nb // SAMPLE_UNROLL, body, jnp.zeros((M_DIM, nb), F32))

    num = w_inter * cq_t + scores * vt
    den = w_inter * nq + scores
    hh = num / jnp.maximum(jnp.abs(den), jnp.exp(-m_t))
    hh = hh * lax.rsqrt(jnp.mean(hh * hh, axis=0, keepdims=True) + RMS_EPS)
    y_ref[...] = hh.T * og_ref[...]
    n_out_ref[...] = (w_inter * n0t + w_in * kt).T
    m_out_ref[0] = m_t


def _sample_mlstm(b_i, b_f, qkv, og, gates_t, m0_t, n0, c0):
    nb = qkv.shape[0]
    smem = pl.BlockSpec(memory_space=pltpu.SMEM)
    head = lambda off: pl.BlockSpec((nb, M_DIM), lambda h: (0, off + h))
    out_shape = (
        jax.ShapeDtypeStruct((nb, M_WIDTH), F32),
        jax.ShapeDtypeStruct((nb, M_WIDTH), F32),
        jax.ShapeDtypeStruct((M_HEADS, 1, nb), F32),
        jax.ShapeDtypeStruct((nb, M_HEADS, M_DIM, M_DIM), F32),
    )
    c_spec = pl.BlockSpec((nb, 1, M_DIM, M_DIM), lambda h: (0, h, 0, 0))
    return pl.pallas_call(
        _sample_mlstm_kernel,
        out_shape=out_shape,
        grid=(M_HEADS,),
        in_specs=[smem, smem, head(0), head(M_HEADS), head(2 * M_HEADS), head(0),
                  pl.BlockSpec(gates_t.shape, lambda h: (0, 0)), pl.BlockSpec(m0_t.shape, lambda h: (0, 0)),
                  head(0), c_spec],
        out_specs=(head(0), head(0), pl.BlockSpec((1, 1, nb), lambda h: (h, 0, 0)), c_spec),
        scratch_shapes=[pltpu.VMEM((nb, M_DIM), F32), pltpu.VMEM((nb, M_DIM), F32)],
        compiler_params=pltpu.CompilerParams(dimension_semantics=("arbitrary",),
                                             vmem_limit_bytes=VMEM_LIMIT_BYTES),
        name="sample_mlstm",
    )(b_i, b_f, qkv, qkv, qkv, og, gates_t, m0_t, n0, c0)


SAMPLE_TILE = 16
SAMPLE_UNROLL = 8


def _sample_swa_kernel(q2_ref, kc_ref, vc_ref, kn_ref, vn_ref, sink_ref, o_ref, ko_ref, vo_ref):
    sink = sink_ref[...] * LOG2E
    w = kc_ref.shape[2]
    tile = range(SAMPLE_TILE)
    newest = lax.broadcasted_iota(jnp.int32, (KV_WIDTH, w), 1) == w - 1
    pad = jnp.zeros((KV_WIDTH - SAMPLE_TILE, KV_WIDTH), F32)
    kn_t = jnp.concatenate([kn_ref[...], pad], axis=0).T
    vn_t = jnp.concatenate([vn_ref[...], pad], axis=0).T
    q2 = [q2_ref[b] for b in tile]
    k_new = [kn_ref[b:b + 1, :] for b in tile]
    v_new = [vn_ref[b:b + 1, :] for b in tile]
    s_c = [_dot(q2[b], kc_ref[b].astype(BF16)) for b in tile]
    s_n = [jnp.sum(q2[b].astype(F32) * k_new[b], axis=-1, keepdims=True) for b in tile]
    mx = [jnp.maximum(jnp.maximum(jnp.max(s_c[b], axis=-1, keepdims=True), s_n[b]), sink) for b in tile]
    p_c = [jnp.exp2(s_c[b] - mx[b]) for b in tile]
    p_n = [jnp.exp2(s_n[b] - mx[b]) for b in tile]
    denom = [jnp.sum(p_c[b], axis=-1, keepdims=True) + p_n[b] + jnp.exp2(sink - mx[b]) for b in tile]
    o = [_dot_nt(p_c[b].astype(BF16), vc_ref[b].astype(BF16)) for b in tile]
    for b in tile:
        o_ref[b] = (o[b] + p_n[b] * v_new[b]) / denom[b]
        ko_ref[b] = jnp.where(newest, kn_t[:, b:b + 1], pltpu.roll(kc_ref[b], w - 1, 1))
        vo_ref[b] = jnp.where(newest, vn_t[:, b:b + 1], pltpu.roll(vc_ref[b], w - 1, 1))


def _sample_swa(q2, k_cache, v_cache, k_new, v_new, sink_col):
    nb, _, w = k_cache.shape
    t3 = lambda a, c: pl.BlockSpec((SAMPLE_TILE, a, c), lambda i: (i, 0, 0))
    t2 = pl.BlockSpec((SAMPLE_TILE, KV_WIDTH), lambda i: (i, 0))
    out_shape = (
        jax.ShapeDtypeStruct((nb, A_HEADS, KV_WIDTH), F32),
        jax.ShapeDtypeStruct((nb, KV_WIDTH, w), F32),
        jax.ShapeDtypeStruct((nb, KV_WIDTH, w), F32),
    )
    return pl.pallas_call(
        _sample_swa_kernel,
        out_shape=out_shape,
        grid=(nb // SAMPLE_TILE,),
        in_specs=[t3(A_HEADS, KV_WIDTH), t3(KV_WIDTH, w), t3(KV_WIDTH, w), t2, t2,
                  pl.BlockSpec((A_HEADS, 1), lambda i: (0, 0))],
        out_specs=(t3(A_HEADS, KV_WIDTH), t3(KV_WIDTH, w), t3(KV_WIDTH, w)),
        compiler_params=pltpu.CompilerParams(dimension_semantics=("arbitrary",)),
        name="sample_swa",
    )(q2, k_cache, v_cache, k_new, v_new, sink_col)


def kernel(x_prompt, x_sample, cache_swa_k, cache_swa_v, state_mlstm_C, state_mlstm_n, state_mlstm_m,
           ffn1_norm, ffn1_w_gate, ffn1_w_up, ffn1_w_down, mix_norm, w_in, mlstm_b_i, mlstm_b_f,
           mlstm_out_norm, swa_q_norm, swa_k_norm, swa_sinks, w_out, ffn2_norm, ffn2_w_gate,
           ffn2_w_up, ffn2_w_down):
    depth = ffn1_norm.shape[0]
    assert depth == 1
    batch, seq, _ = x_prompt.shape
    nb = x_sample.shape[0]
    assert x_sample.shape[1] == 1 and seq % BLOCK == 0

    front_weights = (ffn1_w_gate[0], ffn1_w_up[0], ffn1_w_down[0], jnp.transpose(w_in[0]))
    n1 = ffn1_norm[0].reshape(1, D_MODEL)
    n2 = mix_norm[0].reshape(1, D_MODEL)
    n3 = ffn2_norm[0].reshape(1, D_MODEL)
    qgain = jnp.tile(swa_q_norm[0], A_HEADS).reshape(1, A_WIDTH)
    kgain = jnp.tile(swa_k_norm[0], KV_HEADS).reshape(1, KV_WIDTH)
    ogain = mlstm_out_norm[0].reshape(1, M_WIDTH)
    b_i, b_f = mlstm_b_i[0], mlstm_b_f[0]
    bias8 = jnp.concatenate([b_i, b_f]).reshape(2 * M_HEADS, 1)
    front_vectors = (n1, n2, qgain, kgain, ogain, bias8)
    sinks = swa_sinks[0]

    xp = x_prompt.reshape(batch * seq, D_MODEL)
    xs = x_sample.reshape(nb, D_MODEL)
    back_weights_f32 = (w_out[0], ffn2_w_gate[0], ffn2_w_up[0], ffn2_w_down[0])
    prompt_front, sample_front, _ = _front(xp, xs, front_vectors, front_weights, (), tm=512, seq=seq)
    x1, qkv, og, qa, ka, va, vt, kx, vx, rows, cols, pm = prompt_front
    x1s, qkvs, ogs, qas, kas, vas = sample_front[:6]
    gates_t = sample_front[-1]

    rows = rows.reshape(batch, seq // BLOCK, 3, 8, BLOCK)
    y, pc, pn, (wo, wg2, wu2, wd2) = _prompt_mixer(sinks, qkv, vt, og, cols, rows, qa, kx, vx,
                                                   back_weights_f32, batch, seq)
    back_params = (wo, n3, wg2, wu2, wd2)
    to_window_major = lambda c, n: jnp.transpose(c.reshape(n, KV_HEADS, A_DIM, -1), (0, 3, 1, 2))[None]
    pk, pv = to_window_major(ka, batch), to_window_major(va, batch)
    pn = pn[:, :, 0, :]
    pm = pm[:, 0:M_HEADS, 0]

    y_m, sn, mt, sc = _sample_mlstm(b_i, b_f, qkvs, ogs, gates_t, jnp.transpose(state_mlstm_m[0]),
                                    state_mlstm_n[0].reshape(nb, M_WIDTH), state_mlstm_C[0])
    sn = sn.reshape(nb, M_HEADS, M_DIM)
    sm = jnp.transpose(mt[:, 0, :])

    qa_h = qas.reshape(nb, A_HEADS, A_DIM)
    zeros = jnp.zeros_like(qa_h)
    in_lo = (jnp.arange(A_HEADS) // A_GROUP == 0)[None, :, None]
    q2 = jnp.concatenate([jnp.where(in_lo, qa_h, zeros), jnp.where(in_lo, zeros, qa_h)], axis=-1)
    to_feature_major = lambda c: jnp.transpose(c[0], (0, 2, 3, 1)).reshape(nb, KV_WIDTH, -1)
    kc, vc = to_feature_major(cache_swa_k), to_feature_major(cache_swa_v)
    o2, sk, sv = _sample_swa(q2, kc, vc, kas, vas, sinks.reshape(A_HEADS, 1))
    o2 = o2.reshape(nb, A_HEADS, KV_HEADS, A_DIM)
    y_a = jnp.where(in_lo, o2[:, :, 0, :], o2[:, :, 1, :]).reshape(nb, A_WIDTH)
    ys_in = jnp.concatenate([y_m, y_a], axis=-1).astype(BF16)

    yp, ys = _back(x1, y, x1s, ys_in, back_params, tm=1024)
    yp = yp.reshape(batch, seq, D_MODEL)

    return (yp, ys, pk, pv, pc[None], pn[None], pm[None],
            to_window_major(sk, nb), to_window_major(sv, nb), sc[None], sn[None], sm[None])
```
